```python
import jax, jax.numpy as jnp
from jax import lax
import numpy as np

D_MODEL = 1024
BATCH = 16
SEQ = 2048
DEPTH = 1

MEM_LEN = 256
MEM_HEADS = 4
MEM_DIM = 128
MLA_HEADS = 8
MLA_NOPE = 64
MLA_ROPE = 32
MLA_V = 64
Q_LORA = 384
KV_LORA = 256
ROPE_THETA = 10000.0
SB_HEADS = 8
SB_DIM = 64
D_FF = 4 * D_MODEL
N_BRANCH = 3
BRANCH_W = 512
Q_BLOCK = 128
EPS = 1e-6

IN_SIZES = [Q_LORA, KV_LORA, MLA_ROPE, 3 * SB_HEADS * SB_DIM, MEM_HEADS * MEM_DIM, N_BRANCH * D_MODEL]
IN_WIDTH = int(sum(IN_SIZES))
IN_SPLITS = [int(v) for v in np.cumsum(IN_SIZES)[:-1]]

kernel_name = "hybrid_mla_stickbreak_memxattn_gated"


def rms_norm(x, g):
    xf = x.astype(jnp.float32)
    y = xf * lax.rsqrt(jnp.mean(xf * xf, axis=-1, keepdims=True) + EPS)
    return (y * g.astype(jnp.float32)).astype(x.dtype)


def apply_rope(x, positions):
    half = MLA_ROPE // 2
    inv_freq = 1.0 / (ROPE_THETA ** (jnp.arange(half, dtype=jnp.float32) * (2.0 / MLA_ROPE)))
    ang = positions.astype(jnp.float32)[..., None] * inv_freq
    cos = jnp.cos(ang)[:, :, None, :]
    sin = jnp.sin(ang)[:, :, None, :]
    xf = x.astype(jnp.float32)
    x1, x2 = xf[..., :half], xf[..., half:]
    out = jnp.concatenate([x1 * cos - x2 * sin, x1 * sin + x2 * cos], axis=-1)
    return out.astype(x.dtype)


def causal_softmax_attention(q, k, v, scale):
    b, s, h, _ = q.shape
    outs = []
    for i in range(s // Q_BLOCK):
        q0, q1 = i * Q_BLOCK, (i + 1) * Q_BLOCK
        qb, kb, vb = q[:, q0:q1], k[:, :q1], v[:, :q1]
        sc = jnp.einsum('bqhd,bkhd->bhqk', qb, kb).astype(jnp.float32) * scale
        t_idx = jnp.arange(q0, q1)[:, None]
        s_idx = jnp.arange(q1)[None, :]
        sc = jnp.where(t_idx >= s_idx, sc, jnp.finfo(jnp.float32).min)
        p = jax.nn.softmax(sc, axis=-1).astype(v.dtype)
        outs.append(jnp.einsum('bhqk,bkhd->bqhd', p, vb))
    o = jnp.concatenate(outs, axis=1)
    return o.reshape(b, s, h * v.shape[-1])


def stick_breaking_attention(q, k, v, scale):
    b, s, h, _ = q.shape
    outs = []
    for i in range(s // Q_BLOCK):
        q0, q1 = i * Q_BLOCK, (i + 1) * Q_BLOCK
        qb, kb, vb = q[:, q0:q1], k[:, :q1], v[:, :q1]
        z = jnp.einsum('bqhd,bkhd->bhqk', qb, kb).astype(jnp.float32) * scale
        t_idx = jnp.arange(q0, q1)[:, None]
        s_idx = jnp.arange(q1)[None, :]
        strict = t_idx > s_idx
        log_keep = jnp.where(strict, jax.nn.log_sigmoid(-z), 0.0)
        rev_incl = lax.cumsum(log_keep, axis=3, reverse=True)
        rev_excl = jnp.concatenate([rev_incl[..., 1:], jnp.zeros_like(rev_incl[..., :1])], axis=-1)
        log_a = jax.nn.log_sigmoid(z) + rev_excl
        a = jnp.where(strict, jnp.exp(log_a), 0.0).astype(v.dtype)
        outs.append(jnp.einsum('bhqk,bkhd->bqhd', a, vb))
    o = jnp.concatenate(outs, axis=1)
    return o.reshape(b, s, h * v.shape[-1])


def memory_cross_attention(q, k, v, scale):
    b, s, h, d = q.shape
    sc = jnp.einsum('bshd,bmhd->bhsm', q, k).astype(jnp.float32) * scale
    p = jax.nn.softmax(sc, axis=-1).astype(v.dtype)
    return jnp.einsum('bhsm,bmhd->bshd', p, v).reshape(b, s, h * d)


def _fwd_setup_inputs(seed: int = 0) -> dict:
    key = jax.random.key(seed)
    ks = jax.random.split(key, 24)
    f32 = jnp.float32

    def w(k, shape, fan_in):
        return jax.random.normal(k, shape, f32) * (fan_in ** -0.5)

    def gain(k, shape):
        return 1.0 + 0.02 * jax.random.normal(k, shape, f32)

    x = jax.random.normal(ks[0], (BATCH, SEQ, D_MODEL), f32)
    mem = jax.random.normal(ks[1], (BATCH, MEM_LEN, D_MODEL), f32)
    offsets = jax.random.randint(ks[2], (BATCH, 1), 0, 4096, dtype=jnp.int32)
    positions = (jnp.arange(SEQ, dtype=jnp.int32)[None, :] + offsets).astype(jnp.int32)
    return {
        "x": x,
        "mem": mem,
        "positions": positions,
        "ln_mix_pre": gain(ks[3], (DEPTH, D_MODEL)),
        "w_in": w(ks[4], (DEPTH, D_MODEL, IN_WIDTH), D_MODEL),
        "b_gate": 0.01 * jax.random.normal(ks[5], (DEPTH, N_BRANCH * D_MODEL), f32),
        "q_norm": gain(ks[6], (DEPTH, Q_LORA)),
        "w_uq": w(ks[7], (DEPTH, Q_LORA, MLA_HEADS * (MLA_NOPE + MLA_ROPE)), Q_LORA),
        "kv_norm": gain(ks[8], (DEPTH, KV_LORA)),
        "w_uk": w(ks[9], (DEPTH, KV_LORA, MLA_HEADS * MLA_NOPE), KV_LORA),
        "w_uv": w(ks[10], (DEPTH, KV_LORA, MLA_HEADS * MLA_V), KV_LORA),
        "mem_norm": gain(ks[11], (DEPTH, D_MODEL)),
        "w_mem_kv": w(ks[12], (DEPTH, D_MODEL, 2 * MEM_HEADS * MEM_DIM), D_MODEL),
        "w_branch_out": w(ks[13], (DEPTH, N_BRANCH, BRANCH_W, D_MODEL), BRANCH_W),
        "w_out": w(ks[14], (DEPTH, D_MODEL, D_MODEL), D_MODEL),
        "ln_mix_post": gain(ks[15], (DEPTH, D_MODEL)),
        "ln_mlp_pre": gain(ks[16], (DEPTH, D_MODEL)),
        "w_mlp_up": w(ks[17], (DEPTH, D_MODEL, D_FF), D_MODEL),
        "w_mlp_down": w(ks[18], (DEPTH, D_FF, D_MODEL), D_FF),
        "ln_mlp_post": gain(ks[19], (DEPTH, D_MODEL)),
    }


def _fwd_reference(x, mem, positions, ln_mix_pre, w_in, b_gate, q_norm, w_uq, kv_norm, w_uk, w_uv,
              mem_norm, w_mem_kv, w_branch_out, w_out, ln_mix_post, ln_mlp_pre, w_mlp_up,
              w_mlp_down, ln_mlp_post):
    b, s, d = x.shape
    m = mem.shape[1]
    for l in range(DEPTH):
        h = rms_norm(x, ln_mix_pre[l])
        proj = jnp.einsum('bsd,de->bse', h, w_in[l])
        c_q, c_kv, k_r, sb_qkv, q_m, gate_logits = jnp.split(proj, IN_SPLITS, axis=-1)

        c_q = rms_norm(c_q, q_norm[l])
        q = jnp.einsum('bsr,re->bse', c_q, w_uq[l]).reshape(b, s, MLA_HEADS, MLA_NOPE + MLA_ROPE)
        q_nope, q_rope = q[..., :MLA_NOPE], apply_rope(q[..., MLA_NOPE:], positions)
        c_kv = rms_norm(c_kv, kv_norm[l])
        k_nope = jnp.einsum('bsr,re->bse', c_kv, w_uk[l]).reshape(b, s, MLA_HEADS, MLA_NOPE)
        v_mla = jnp.einsum('bsr,re->bse', c_kv, w_uv[l]).reshape(b, s, MLA_HEADS, MLA_V)
        k_rope = apply_rope(k_r[:, :, None, :], positions)
        q_full = jnp.concatenate([q_nope, q_rope], axis=-1)
        k_full = jnp.concatenate([k_nope, jnp.broadcast_to(k_rope, (b, s, MLA_HEADS, MLA_ROPE))], axis=-1)
        o_mla = causal_softmax_attention(q_full, k_full, v_mla, (MLA_NOPE + MLA_ROPE) ** -0.5)

        sq, sk, sv = jnp.split(sb_qkv.reshape(b, s, 3, SB_HEADS, SB_DIM), 3, axis=2)
        o_sb = stick_breaking_attention(sq[:, :, 0], sk[:, :, 0], sv[:, :, 0], SB_DIM ** -0.5)

        mem_h = rms_norm(mem, mem_norm[l])
        mkv = jnp.einsum('bmd,de->bme', mem_h, w_mem_kv[l]).reshape(b, m, 2, MEM_HEADS, MEM_DIM)
        o_mem = memory_cross_attention(q_m.reshape(b, s, MEM_HEADS, MEM_DIM), mkv[:, :, 0], mkv[:, :, 1],
                                       MEM_DIM ** -0.5)

        gates = jax.nn.sigmoid(gate_logits.astype(jnp.float32) + b_gate[l].astype(jnp.float32))
        gates = gates.reshape(b, s, N_BRANCH, d).astype(x.dtype)
        merged = None
        for i, o in enumerate((o_mla, o_sb, o_mem)):
            yb = gates[:, :, i] * jnp.einsum('bsc,cd->bsd', o, w_branch_out[l, i])
            merged = yb if merged is None else merged + yb
        y = jnp.einsum('bsd,de->bse', merged, w_out[l])
        x = x + rms_norm(y, ln_mix_post[l])

        h = rms_norm(x, ln_mlp_pre[l])
        u = jnp.square(jax.nn.relu(jnp.einsum('bsd,df->bsf', h, w_mlp_up[l])))
        x = x + rms_norm(jnp.einsum('bsf,fd->bsd', u, w_mlp_down[l]), ln_mlp_post[l])
    return x


import jax as _jax
import jax.numpy as _jnp

TWIN_FORMAT = 'train_step'
FWD_PARAMS = ['x', 'mem', 'positions', 'ln_mix_pre', 'w_in', 'b_gate', 'q_norm', 'w_uq', 'kv_norm', 'w_uk', 'w_uv', 'mem_norm', 'w_mem_kv', 'w_branch_out', 'w_out', 'ln_mix_post', 'ln_mlp_pre', 'w_mlp_up', 'w_mlp_down', 'ln_mlp_post']
TWIN_WEIGHTS = ['ln_mix_pre', 'w_in', 'b_gate', 'q_norm', 'w_uq', 'kv_norm', 'w_uk', 'w_uv', 'mem_norm', 'w_mem_kv', 'w_branch_out', 'w_out', 'ln_mix_post', 'ln_mlp_pre', 'w_mlp_up', 'w_mlp_down', 'ln_mlp_post']
TWIN_DIFF_INPUT = 'x'
TWIN_INPUTS = ['x', 'mem', 'positions', 'ln_mix_pre', 'w_in', 'b_gate', 'q_norm', 'w_uq', 'kv_norm', 'w_uk', 'w_uv', 'mem_norm', 'w_mem_kv', 'w_branch_out', 'w_out', 'ln_mix_post', 'ln_mlp_pre', 'w_mlp_up', 'w_mlp_down', 'ln_mlp_post', 'loss_target', 'm_ln_mix_pre', 'm_w_in', 'm_b_gate', 'm_q_norm', 'm_w_uq', 'm_kv_norm', 'm_w_uk', 'm_w_uv', 'm_mem_norm', 'm_w_mem_kv', 'm_w_branch_out', 'm_w_out', 'm_ln_mix_post', 'm_ln_mlp_pre', 'm_w_mlp_up', 'm_w_mlp_down', 'm_ln_mlp_post', 'v_ln_mix_pre', 'v_w_in', 'v_b_gate', 'v_q_norm', 'v_w_uq', 'v_kv_norm', 'v_w_uk', 'v_w_uv', 'v_mem_norm', 'v_w_mem_kv', 'v_w_branch_out', 'v_w_out', 'v_ln_mix_post', 'v_ln_mlp_pre', 'v_w_mlp_up', 'v_w_mlp_down', 'v_ln_mlp_post']
TWIN_OUTPUTS = ['loss', 'grad_x', 'grad_ln_mix_pre', 'grad_w_in', 'grad_b_gate', 'grad_q_norm', 'grad_w_uq', 'grad_kv_norm', 'grad_w_uk', 'grad_w_uv', 'grad_mem_norm', 'grad_w_mem_kv', 'grad_w_branch_out', 'grad_w_out', 'grad_ln_mix_post', 'grad_ln_mlp_pre', 'grad_w_mlp_up', 'grad_w_mlp_down', 'grad_ln_mlp_post', 'delta_ln_mix_pre', 'delta_w_in', 'delta_b_gate', 'delta_q_norm', 'delta_w_uq', 'delta_kv_norm', 'delta_w_uk', 'delta_w_uv', 'delta_mem_norm', 'delta_w_mem_kv', 'delta_w_branch_out', 'delta_w_out', 'delta_ln_mix_post', 'delta_ln_mlp_pre', 'delta_w_mlp_up', 'delta_w_mlp_down', 'delta_ln_mlp_post', 'new_m_ln_mix_pre', 'new_m_w_in', 'new_m_b_gate', 'new_m_q_norm', 'new_m_w_uq', 'new_m_kv_norm', 'new_m_w_uk', 'new_m_w_uv', 'new_m_mem_norm', 'new_m_w_mem_kv', 'new_m_w_branch_out', 'new_m_w_out', 'new_m_ln_mix_post', 'new_m_ln_mlp_pre', 'new_m_w_mlp_up', 'new_m_w_mlp_down', 'new_m_ln_mlp_post', 'new_v_ln_mix_pre', 'new_v_w_in', 'new_v_b_gate', 'new_v_q_norm', 'new_v_w_uq', 'new_v_kv_norm', 'new_v_w_uk', 'new_v_w_uv', 'new_v_mem_norm', 'new_v_w_mem_kv', 'new_v_w_branch_out', 'new_v_w_out', 'new_v_ln_mix_post', 'new_v_ln_mlp_pre', 'new_v_w_mlp_up', 'new_v_w_mlp_down', 'new_v_ln_mlp_post']
TWIN_LEAF_KINDS = {'loss': 'loss', 'grad_x': 'grad_x', 'grad_ln_mix_pre': 'grad_w', 'grad_w_in': 'grad_w', 'grad_b_gate': 'grad_w', 'grad_q_norm': 'grad_w', 'grad_w_uq': 'grad_w', 'grad_kv_norm': 'grad_w', 'grad_w_uk': 'grad_w', 'grad_w_uv': 'grad_w', 'grad_mem_norm': 'grad_w', 'grad_w_mem_kv': 'grad_w', 'grad_w_branch_out': 'grad_w', 'grad_w_out': 'grad_w', 'grad_ln_mix_post': 'grad_w', 'grad_ln_mlp_pre': 'grad_w', 'grad_w_mlp_up': 'grad_w', 'grad_w_mlp_down': 'grad_w', 'grad_ln_mlp_post': 'grad_w', 'delta_ln_mix_pre': 'delta_w', 'delta_w_in': 'delta_w', 'delta_b_gate': 'delta_w', 'delta_q_norm': 'delta_w', 'delta_w_uq': 'delta_w', 'delta_kv_norm': 'delta_w', 'delta_w_uk': 'delta_w', 'delta_w_uv': 'delta_w', 'delta_mem_norm': 'delta_w', 'delta_w_mem_kv': 'delta_w', 'delta_w_branch_out': 'delta_w', 'delta_w_out': 'delta_w', 'delta_ln_mix_post': 'delta_w', 'delta_ln_mlp_pre': 'delta_w', 'delta_w_mlp_up': 'delta_w', 'delta_w_mlp_down': 'delta_w', 'delta_ln_mlp_post': 'delta_w', 'new_m_ln_mix_pre': 'new_m', 'new_m_w_in': 'new_m', 'new_m_b_gate': 'new_m', 'new_m_q_norm': 'new_m', 'new_m_w_uq': 'new_m', 'new_m_kv_norm': 'new_m', 'new_m_w_uk': 'new_m', 'new_m_w_uv': 'new_m', 'new_m_mem_norm': 'new_m', 'new_m_w_mem_kv': 'new_m', 'new_m_w_branch_out': 'new_m', 'new_m_w_out': 'new_m', 'new_m_ln_mix_post': 'new_m', 'new_m_ln_mlp_pre': 'new_m', 'new_m_w_mlp_up': 'new_m', 'new_m_w_mlp_down': 'new_m', 'new_m_ln_mlp_post': 'new_m', 'new_v_ln_mix_pre': 'new_v', 'new_v_w_in': 'new_v', 'new_v_b_gate': 'new_v', 'new_v_q_norm': 'new_v', 'new_v_w_uq': 'new_v', 'new_v_kv_norm': 'new_v', 'new_v_w_uk': 'new_v', 'new_v_w_uv': 'new_v', 'new_v_mem_norm': 'new_v', 'new_v_w_mem_kv': 'new_v', 'new_v_w_branch_out': 'new_v', 'new_v_w_out': 'new_v', 'new_v_ln_mix_post': 'new_v', 'new_v_ln_mlp_pre': 'new_v', 'new_v_w_mlp_up': 'new_v', 'new_v_w_mlp_down': 'new_v', 'new_v_ln_mlp_post': 'new_v'}


def _forward(args):
    return _fwd_reference(*[args[k] for k in FWD_PARAMS])


def _output_shape():
    out = _jax.eval_shape(lambda: _forward(_fwd_setup_inputs(0)))
    return out.shape, out.dtype

N_MICROBATCH = 1
ADAM_LR = 0.001
ADAM_B1 = 0.9
ADAM_B2 = 0.999
ADAM_EPS = 1e-08
ADAM_WD = 0.01
ADAM_STEP = 10
PER_EXAMPLE_BATCH_AXIS = {'x': 0, 'mem': 0, 'positions': 0, 'loss_target': 0}
SHARED_INPUTS = []
_WEIGHT_DTYPES = {'ln_mix_pre': _jnp.float32, 'w_in': _jnp.float32, 'b_gate': _jnp.float32, 'q_norm': _jnp.float32, 'w_uq': _jnp.float32, 'kv_norm': _jnp.float32, 'w_uk': _jnp.float32, 'w_uv': _jnp.float32, 'mem_norm': _jnp.float32, 'w_mem_kv': _jnp.float32, 'w_branch_out': _jnp.float32, 'w_out': _jnp.float32, 'ln_mix_post': _jnp.float32, 'ln_mlp_pre': _jnp.float32, 'w_mlp_up': _jnp.float32, 'w_mlp_down': _jnp.float32, 'ln_mlp_post': _jnp.float32}
MOMENT_SCALE = {'ln_mix_pre': 6.685194e-01, 'w_in': 2.439110e-01, 'b_gate': 2.138798e-01, 'q_norm': 1.702443e-01, 'w_uq': 1.180178e-01, 'kv_norm': 4.058756e-01, 'w_uk': 1.227623e-01, 'w_uv': 2.746894e-01, 'mem_norm': 4.936313e-01, 'w_mem_kv': 5.079860e-01, 'w_branch_out': 4.396783e-01, 'w_out': 8.853242e-01, 'ln_mix_post': 3.198116e+01, 'ln_mlp_pre': 7.738973e-01, 'w_mlp_up': 4.278509e-01, 'w_mlp_down': 9.662063e-01, 'ln_mlp_post': 3.286892e+01}


def _to_microbatches(a, axis):
    t = _jnp.moveaxis(a, axis, 0)
    t = t.reshape((N_MICROBATCH, t.shape[0] // N_MICROBATCH) + t.shape[1:])
    return _jnp.moveaxis(t, 1, axis + 1)


def setup_inputs(seed: int = 0) -> dict:
    inp = _fwd_setup_inputs(seed)
    key = _jax.random.fold_in(_jax.random.key(seed), 7919)
    shape, _ = _output_shape()
    out = dict(inp)
    out["loss_target"] = _jax.random.normal(_jax.random.fold_in(key, 0), shape, _jnp.float32)
    for i, name in enumerate(TWIN_WEIGHTS):
        w = inp[name].astype(_jnp.float32)
        if MOMENT_SCALE is None:
            s = _jnp.sqrt(_jnp.mean(_jnp.square(w)) + 1e-30)
        else:
            s = MOMENT_SCALE[name]
        km, kv = _jax.random.split(_jax.random.fold_in(key, i + 1))
        out[name] = w
        out["m_" + name] = s * _jax.random.normal(km, w.shape, _jnp.float32)
        out["v_" + name] = (s * s) * _jax.random.uniform(kv, w.shape, _jnp.float32, 0.5, 1.5)
    if N_MICROBATCH > 1:
        for name, axis in PER_EXAMPLE_BATCH_AXIS.items():
            out[name] = _to_microbatches(out[name], axis)
    return {'x': out['x'], 'mem': out['mem'], 'positions': out['positions'], 'ln_mix_pre': out['ln_mix_pre'], 'w_in': out['w_in'], 'b_gate': out['b_gate'], 'q_norm': out['q_norm'], 'w_uq': out['w_uq'], 'kv_norm': out['kv_norm'], 'w_uk': out['w_uk'], 'w_uv': out['w_uv'], 'mem_norm': out['mem_norm'], 'w_mem_kv': out['w_mem_kv'], 'w_branch_out': out['w_branch_out'], 'w_out': out['w_out'], 'ln_mix_post': out['ln_mix_post'], 'ln_mlp_pre': out['ln_mlp_pre'], 'w_mlp_up': out['w_mlp_up'], 'w_mlp_down': out['w_mlp_down'], 'ln_mlp_post': out['ln_mlp_post'], 'loss_target': out['loss_target'], 'm_ln_mix_pre': out['m_ln_mix_pre'], 'm_w_in': out['m_w_in'], 'm_b_gate': out['m_b_gate'], 'm_q_norm': out['m_q_norm'], 'm_w_uq': out['m_w_uq'], 'm_kv_norm': out['m_kv_norm'], 'm_w_uk': out['m_w_uk'], 'm_w_uv': out['m_w_uv'], 'm_mem_norm': out['m_mem_norm'], 'm_w_mem_kv': out['m_w_mem_kv'], 'm_w_branch_out': out['m_w_branch_out'], 'm_w_out': out['m_w_out'], 'm_ln_mix_post': out['m_ln_mix_post'], 'm_ln_mlp_pre': out['m_ln_mlp_pre'], 'm_w_mlp_up': out['m_w_mlp_up'], 'm_w_mlp_down': out['m_w_mlp_down'], 'm_ln_mlp_post': out['m_ln_mlp_post'], 'v_ln_mix_pre': out['v_ln_mix_pre'], 'v_w_in': out['v_w_in'], 'v_b_gate': out['v_b_gate'], 'v_q_norm': out['v_q_norm'], 'v_w_uq': out['v_w_uq'], 'v_kv_norm': out['v_kv_norm'], 'v_w_uk': out['v_w_uk'], 'v_w_uv': out['v_w_uv'], 'v_mem_norm': out['v_mem_norm'], 'v_w_mem_kv': out['v_w_mem_kv'], 'v_w_branch_out': out['v_w_branch_out'], 'v_w_out': out['v_w_out'], 'v_ln_mix_post': out['v_ln_mix_post'], 'v_ln_mlp_pre': out['v_ln_mlp_pre'], 'v_w_mlp_up': out['v_w_mlp_up'], 'v_w_mlp_down': out['v_w_mlp_down'], 'v_ln_mlp_post': out['v_ln_mlp_post']}


def _loss(weights, diff, rest, loss_target):
    with _jax.named_scope("forward"):
        args = {**rest, TWIN_DIFF_INPUT: diff, **{k: w.astype(_WEIGHT_DTYPES[k]) for k, w in weights.items()}}
        y = _forward(args)
    with _jax.named_scope("loss_head"):
        err = _jnp.square(y.astype(_jnp.float32) - loss_target)
        return 0.5 * _jnp.sum(_jnp.mean(err, axis=-1)) if err.ndim else 0.5 * err


def _adamw(w, g, m, v):
    m = ADAM_B1 * m + (1.0 - ADAM_B1) * g
    v = ADAM_B2 * v + (1.0 - ADAM_B2) * _jnp.square(g)
    m_hat = m / (1.0 - ADAM_B1 ** ADAM_STEP)
    v_hat = v / (1.0 - ADAM_B2 ** ADAM_STEP)
    delta = -ADAM_LR * (m_hat / (_jnp.sqrt(v_hat) + ADAM_EPS) + ADAM_WD * w)
    return delta, m, v


def reference(x, mem, positions, ln_mix_pre, w_in, b_gate, q_norm, w_uq, kv_norm, w_uk, w_uv, mem_norm, w_mem_kv, w_branch_out, w_out, ln_mix_post, ln_mlp_pre, w_mlp_up, w_mlp_down, ln_mlp_post, loss_target, m_ln_mix_pre, m_w_in, m_b_gate, m_q_norm, m_w_uq, m_kv_norm, m_w_uk, m_w_uv, m_mem_norm, m_w_mem_kv, m_w_branch_out, m_w_out, m_ln_mix_post, m_ln_mlp_pre, m_w_mlp_up, m_w_mlp_down, m_ln_mlp_post, v_ln_mix_pre, v_w_in, v_b_gate, v_q_norm, v_w_uq, v_kv_norm, v_w_uk, v_w_uv, v_mem_norm, v_w_mem_kv, v_w_branch_out, v_w_out, v_ln_mix_post, v_ln_mlp_pre, v_w_mlp_up, v_w_mlp_down, v_ln_mlp_post):
    given = dict(x=x, mem=mem, positions=positions, ln_mix_pre=ln_mix_pre, w_in=w_in, b_gate=b_gate, q_norm=q_norm, w_uq=w_uq, kv_norm=kv_norm, w_uk=w_uk, w_uv=w_uv, mem_norm=mem_norm, w_mem_kv=w_mem_kv, w_branch_out=w_branch_out, w_out=w_out, ln_mix_post=ln_mix_post, ln_mlp_pre=ln_mlp_pre, w_mlp_up=w_mlp_up, w_mlp_down=w_mlp_down, ln_mlp_post=ln_mlp_post, loss_target=loss_target, m_ln_mix_pre=m_ln_mix_pre, m_w_in=m_w_in, m_b_gate=m_b_gate, m_q_norm=m_q_norm, m_w_uq=m_w_uq, m_kv_norm=m_kv_norm, m_w_uk=m_w_uk, m_w_uv=m_w_uv, m_mem_norm=m_mem_norm, m_w_mem_kv=m_w_mem_kv, m_w_branch_out=m_w_branch_out, m_w_out=m_w_out, m_ln_mix_post=m_ln_mix_post, m_ln_mlp_pre=m_ln_mlp_pre, m_w_mlp_up=m_w_mlp_up, m_w_mlp_down=m_w_mlp_down, m_ln_mlp_post=m_ln_mlp_post, v_ln_mix_pre=v_ln_mix_pre, v_w_in=v_w_in, v_b_gate=v_b_gate, v_q_norm=v_q_norm, v_w_uq=v_w_uq, v_kv_norm=v_kv_norm, v_w_uk=v_w_uk, v_w_uv=v_w_uv, v_mem_norm=v_mem_norm, v_w_mem_kv=v_w_mem_kv, v_w_branch_out=v_w_branch_out, v_w_out=v_w_out, v_ln_mix_post=v_ln_mix_post, v_ln_mlp_pre=v_ln_mlp_pre, v_w_mlp_up=v_w_mlp_up, v_w_mlp_down=v_w_mlp_down, v_ln_mlp_post=v_ln_mlp_post)
    weights = {n: given[n] for n in TWIN_WEIGHTS}
    shared = {n: given[n] for n in SHARED_INPUTS}
    per_example = {n: given[n] for n in ['x', 'mem', 'positions']}
    grad_fn = _jax.value_and_grad(_loss, argnums=(0, 1))

    def one_microbatch(ex, loss_target):
        ex = dict(ex)
        diff = ex.pop(TWIN_DIFF_INPUT)
        return grad_fn(weights, diff, {**shared, **ex}, loss_target)

    if N_MICROBATCH == 1:
        loss, (grad_w, grad_x) = one_microbatch(per_example, given["loss_target"])
    else:
        def body(carry, xs):
            loss_sum, grad_sum = carry
            l_k, (gw_k, gx_k) = one_microbatch(xs[0], xs[1])
            with _jax.named_scope("update"):
                return (loss_sum + l_k, _jax.tree.map(_jnp.add, grad_sum, gw_k)), gx_k

        init = (_jnp.zeros((), _jnp.float32), _jax.tree.map(_jnp.zeros_like, weights))
        (loss, grad_w), grad_x = _jax.lax.scan(body, init, (per_example, given["loss_target"]))
    with _jax.named_scope("update"):
        delta_w, new_m, new_v = {}, {}, {}
        for n in TWIN_WEIGHTS:
            delta_w[n], new_m[n], new_v[n] = _adamw(weights[n], grad_w[n], given["m_" + n], given["v_" + n])
    return (loss, grad_x, *[grad_w[n] for n in TWIN_WEIGHTS], *[delta_w[n] for n in TWIN_WEIGHTS],
            *[new_m[n] for n in TWIN_WEIGHTS], *[new_v[n] for n in TWIN_WEIGHTS])
```

```python
import functools
import math

import numpy as np
import jax
import jax.numpy as jnp
from jax import lax
from jax.experimental import pallas as pl
from jax.experimental.pallas import tpu as pltpu

F32 = jnp.float32
BF16 = jnp.bfloat16

D_MODEL = 1024
MEM_HEADS, MEM_DIM = 4, 128
MLA_HEADS, MLA_NOPE, MLA_ROPE, MLA_V = 8, 64, 32, 64
Q_LORA, KV_LORA = 384, 256
ROPE_THETA = 10000.0
SB_HEADS, SB_DIM = 8, 64
D_FF = 4 * D_MODEL
N_BRANCH, BRANCH_W = 3, 512
EPS = 1e-6
SB_W = 3 * SB_HEADS * SB_DIM
QM_W = MEM_HEADS * MEM_DIM
GATE_W = N_BRANCH * D_MODEL
IN_SIZES = (Q_LORA, KV_LORA, MLA_ROPE, SB_W, QM_W, GATE_W)
IN_WIDTH = sum(IN_SIZES)
ADAM_LR, ADAM_B1, ADAM_B2, ADAM_EPS, ADAM_WD, ADAM_STEP = 0.001, 0.9, 0.999, 1e-08, 0.01, 10

N_DEV = 8
LANES = 128
KR_PAD = LANES
PAD_ORDER = ("gate", "sb", "qm", "ckv", "cq", "kr")
PAD_SIZES = dict(gate=GATE_W, sb=SB_W, qm=QM_W, ckv=KV_LORA, cq=Q_LORA, kr=KR_PAD)
PAD_OFF = {}
_o = 0
for _n in PAD_ORDER:
    PAD_OFF[_n] = _o
    _o += PAD_SIZES[_n]
IN_PAD = _o
HEAD_PAD = LANES
MLA_PAD_W = MLA_HEADS * HEAD_PAD
VMEM_LIMIT = 48 * 1024 * 1024


def _cparams(sem):
    return pltpu.CompilerParams(dimension_semantics=sem, vmem_limit_bytes=VMEM_LIMIT)


def _pick(n, pref):
    if n <= pref:
        return n
    t = (pref // LANES) * LANES
    while t >= LANES:
        if n % t == 0:
            return t
        t -= LANES
    return n


def matmul(a, b, *, name, ta=False, tb=False, out_dtype=F32, acc=None, act=None, act_in=None,
           tm_pref=512, tn_pref=512, tk_pref=1024):
    M, K = (a.shape[1], a.shape[0]) if ta else a.shape
    N = b.shape[0] if tb else b.shape[1]
    assert (b.shape[1] if tb else b.shape[0]) == K
    tm, tn, tk = _pick(M, tm_pref), _pick(N, tn_pref), _pick(K, tk_pref)
    nk = K // tk
    dims = (((0 if ta else 1,), (1 if tb else 0,)), ((), ()))
    n_in = 2 + (acc is not None) + (act_in is not None)
    n_out = 2 if act == "relu2" else 1

    def body(*refs):
        a_ref, b_ref = refs[0], refs[1]
        pos = 2
        acc_ref = act_ref = None
        if acc is not None:
            acc_ref = refs[pos]
            pos += 1
        if act_in is not None:
            act_ref = refs[pos]
            pos += 1
        outs = refs[pos:pos + n_out]
        scratch = refs[pos + n_out:]

        part = lax.dot_general(a_ref[...].astype(BF16), b_ref[...].astype(BF16), dims,
                               preferred_element_type=F32)

        def finish(r):
            if acc_ref is not None:
                r = r + acc_ref[...]
            if act == "relu2":
                outs[0][...] = r.astype(outs[0].dtype)
                outs[1][...] = jnp.square(jnp.maximum(r, 0.0)).astype(outs[1].dtype)
            elif act == "relu2_bwd":
                outs[0][...] = (r * (2.0 * jnp.maximum(act_ref[...], 0.0))).astype(outs[0].dtype)
            else:
                outs[0][...] = r.astype(outs[0].dtype)

        if nk == 1:
            finish(part)
        else:
            acc_sc = scratch[0]
            k = pl.program_id(2)

            @pl.when(k == 0)
            def _():
                acc_sc[...] = part

            @pl.when(k > 0)
            def _():
                acc_sc[...] += part

            @pl.when(k == nk - 1)
            def _():
                finish(acc_sc[...])

    a_spec = pl.BlockSpec((tk, tm), lambda i, j, k: (k, i)) if ta else pl.BlockSpec((tm, tk), lambda i, j, k: (i, k))
    b_spec = pl.BlockSpec((tn, tk), lambda i, j, k: (j, k)) if tb else pl.BlockSpec((tk, tn), lambda i, j, k: (k, j))
    o_spec = pl.BlockSpec((tm, tn), lambda i, j, k: (i, j))
    in_specs = [a_spec, b_spec]
    args = [a, b]
    if acc is not None:
        in_specs.append(o_spec)
        args.append(acc)
    if act_in is not None:
        in_specs.append(o_spec)
        args.append(act_in)
    if act == "relu2":
        out_shape = (jax.ShapeDtypeStruct((M, N), F32), jax.ShapeDtypeStruct((M, N), BF16))
        out_specs = (o_spec, o_spec)
    else:
        out_shape = jax.ShapeDtypeStruct((M, N), out_dtype)
        out_specs = o_spec
    return pl.pallas_call(
        body, name=name, grid=(M // tm, N // tn, nk), in_specs=in_specs, out_specs=out_specs, out_shape=out_shape,
        scratch_shapes=[pltpu.VMEM((tm, tn), F32)] if nk > 1 else [],
        compiler_params=_cparams(("parallel", "parallel", "arbitrary")),
    )(*args)


ROW_TILE = 256


def _rstd(xv):
    return lax.rsqrt(jnp.mean(xv * xv, axis=-1, keepdims=True) + EPS)


def _rms_bwd_rows(dy, xv, g):
    r = _rstd(xv)
    dyg = dy * g
    dx = r * dyg - xv * (r * r * r) * jnp.mean(dyg * xv, axis=-1, keepdims=True)
    return dx, dy * xv * r


def _row_spec(tm, n, col=0):
    return pl.BlockSpec((tm, n), lambda i: (i, col))


def _vec_spec(n):
    return pl.BlockSpec((1, n), lambda i: (0, 0))


def _acc_rows(ref, i, val):
    @pl.when(i == 0)
    def _():
        ref[...] = val

    @pl.when(i > 0)
    def _():
        ref[...] += val


def rms_fwd(x, g, *, name, n=None, col=0, out_dtype=BF16):
    T = x.shape[0]
    n = x.shape[1] if n is None else n
    tm = _pick(T, ROW_TILE)

    def body(x_ref, g_ref, o_ref):
        xv = x_ref[...]
        o_ref[...] = (xv * _rstd(xv) * g_ref[...]).astype(o_ref.dtype)

    return pl.pallas_call(
        body, name=name, grid=(T // tm,), in_specs=[_row_spec(tm, n, col), _vec_spec(n)],
        out_specs=_row_spec(tm, n), out_shape=jax.ShapeDtypeStruct((T, n), out_dtype),
        compiler_params=_cparams(("parallel",)),
    )(x, g)


def rms_bwd(dy, x, g, *, name, n=None, col=0, residual=None, out_dtype=F32):
    T = x.shape[0]
    n = x.shape[1] if n is None else n
    tm = _pick(T, ROW_TILE)

    def body(*refs):
        if residual is None:
            dy_ref, x_ref, g_ref, dx_ref, dg_ref = refs
        else:
            dy_ref, x_ref, g_ref, res_ref, dx_ref, dg_ref = refs
        dx, dgr = _rms_bwd_rows(dy_ref[...].astype(F32), x_ref[...], g_ref[...])
        if residual is not None:
            dx = dx + res_ref[...]
        dx_ref[...] = dx.astype(dx_ref.dtype)
        _acc_rows(dg_ref, pl.program_id(0), jnp.sum(dgr, axis=0, keepdims=True))

    in_specs = [_row_spec(tm, n), _row_spec(tm, n, col), _vec_spec(n)]
    args = [dy, x, g]
    if residual is not None:
        in_specs.append(_row_spec(tm, n))
        args.append(residual)
    return pl.pallas_call(
        body, name=name, grid=(T // tm,), in_specs=in_specs,
        out_specs=(_row_spec(tm, n), _vec_spec(n)),
        out_shape=(jax.ShapeDtypeStruct((T, n), out_dtype), jax.ShapeDtypeStruct((1, n), F32)),
        compiler_params=_cparams(("arbitrary",)),
    )(*args)


def mix_post_fwd(x, y, g_post, g_pre2, *, name):
    T, n = x.shape
    tm = _pick(T, ROW_TILE)

    def body(x_ref, y_ref, gp_ref, g2_ref, x1_ref, h2_ref):
        yv = y_ref[...]
        x1 = x_ref[...] + yv * _rstd(yv) * gp_ref[...]
        x1_ref[...] = x1
        h2_ref[...] = (x1 * _rstd(x1) * g2_ref[...]).astype(h2_ref.dtype)

    return pl.pallas_call(
        body, name=name, grid=(T // tm,),
        in_specs=[_row_spec(tm, n), _row_spec(tm, n), _vec_spec(n), _vec_spec(n)],
        out_specs=(_row_spec(tm, n), _row_spec(tm, n)),
        out_shape=(jax.ShapeDtypeStruct((T, n), F32), jax.ShapeDtypeStruct((T, n), BF16)),
        compiler_params=_cparams(("parallel",)),
    )(x, y, g_post, g_pre2)


def loss_head(x1, d, g, tgt, *, name):
    T, n = x1.shape
    tm = _pick(T, ROW_TILE)

    def body(x1_ref, d_ref, g_ref, t_ref, loss_ref, dx2_ref, dd_ref, dg_ref):
        i = pl.program_id(0)
        dv = d_ref[...]
        gv = g_ref[...]
        err = x1_ref[...] + dv * _rstd(dv) * gv - t_ref[...]
        part = 0.5 * jnp.sum(jnp.mean(err * err, axis=-1, keepdims=True), axis=0, keepdims=True)
        _acc_rows(loss_ref, i, jnp.broadcast_to(part, (1, LANES)))
        dx2 = err * (1.0 / n)
        dx2_ref[...] = dx2
        dd, dgr = _rms_bwd_rows(dx2, dv, gv)
        dd_ref[...] = dd.astype(dd_ref.dtype)
        _acc_rows(dg_ref, i, jnp.sum(dgr, axis=0, keepdims=True))

    return pl.pallas_call(
        body, name=name, grid=(T // tm,),
        in_specs=[_row_spec(tm, n), _row_spec(tm, n), _vec_spec(n), _row_spec(tm, n)],
        out_specs=(_vec_spec(LANES), _row_spec(tm, n), _row_spec(tm, n), _vec_spec(n)),
        out_shape=(jax.ShapeDtypeStruct((1, LANES), F32), jax.ShapeDtypeStruct((T, n), F32),
                   jax.ShapeDtypeStruct((T, n), BF16), jax.ShapeDtypeStruct((1, n), F32)),
        compiler_params=_cparams(("arbitrary",)),
    )(x1, d, g, tgt)


def mlp_pre_bwd(dx2, dh2, x1, g_pre2, y, g_post, *, name):
    T, n = x1.shape
    tm = _pick(T, ROW_TILE)

    def body(dx2_ref, dh2_ref, x1_ref, g2_ref, y_ref, gp_ref, dx1_ref, dy_ref, dg2_ref, dgp_ref):
        i = pl.program_id(0)
        d1, dg2 = _rms_bwd_rows(dh2_ref[...], x1_ref[...], g2_ref[...])
        dx1 = dx2_ref[...] + d1
        dx1_ref[...] = dx1
        dy, dgp = _rms_bwd_rows(dx1, y_ref[...], gp_ref[...])
        dy_ref[...] = dy.astype(dy_ref.dtype)
        _acc_rows(dg2_ref, i, jnp.sum(dg2, axis=0, keepdims=True))
        _acc_rows(dgp_ref, i, jnp.sum(dgp, axis=0, keepdims=True))

    rs = _row_spec(tm, n)
    return pl.pallas_call(
        body, name=name, grid=(T // tm,),
        in_specs=[rs, rs, rs, _vec_spec(n), rs, _vec_spec(n)],
        out_specs=(rs, rs, _vec_spec(n), _vec_spec(n)),
        out_shape=(jax.ShapeDtypeStruct((T, n), F32), jax.ShapeDtypeStruct((T, n), BF16),
                   jax.ShapeDtypeStruct((1, n), F32), jax.ShapeDtypeStruct((1, n), F32)),
        compiler_params=_cparams(("arbitrary",)),
    )(dx2, dh2, x1, g_pre2, y, g_post)


def _gate_specs(tm):
    gcol = PAD_OFF["gate"] // D_MODEL
    lspecs = [pl.BlockSpec((tm, D_MODEL), functools.partial(lambda i, c: (i, c), c=gcol + b)) for b in range(N_BRANCH)]
    bspecs = [pl.BlockSpec((1, D_MODEL), functools.partial(lambda i, c: (0, c), c=b)) for b in range(N_BRANCH)]
    pspecs = [_row_spec(tm, D_MODEL) for _ in range(N_BRANCH)]
    return lspecs, bspecs, pspecs


def gate_merge(proj, b_gate, ps, *, name):
    T = proj.shape[0]
    tm = _pick(T, ROW_TILE)
    lspecs, bspecs, pspecs = _gate_specs(tm)

    def body(*refs):
        l_refs, b_refs, p_refs, o_ref = refs[0:3], refs[3:6], refs[6:9], refs[9]
        tot = None
        for lr, br, pr in zip(l_refs, b_refs, p_refs):
            term = jax.nn.sigmoid(lr[...] + br[...]) * pr[...]
            tot = term if tot is None else tot + term
        o_ref[...] = tot.astype(o_ref.dtype)

    return pl.pallas_call(
        body, name=name, grid=(T // tm,), in_specs=lspecs + bspecs + pspecs,
        out_specs=_row_spec(tm, D_MODEL), out_shape=jax.ShapeDtypeStruct((T, D_MODEL), BF16),
        compiler_params=_cparams(("parallel",)),
    )(proj, proj, proj, b_gate, b_gate, b_gate, *ps)


def gate_bwd(dmerged, proj, b_gate, ps, *, name):
    T = proj.shape[0]
    tm = _pick(T, ROW_TILE)
    lspecs, bspecs, pspecs = _gate_specs(tm)

    def body(*refs):
        dm_ref = refs[0]
        l_refs, b_refs, p_refs = refs[1:4], refs[4:7], refs[7:10]
        dl_ref, dp_refs, db_ref = refs[10], refs[11:14], refs[14]
        i = pl.program_id(0)
        dm = dm_ref[...]
        for b, (lr, br, pr, dpr) in enumerate(zip(l_refs, b_refs, p_refs, dp_refs)):
            gt = jax.nn.sigmoid(lr[...] + br[...])
            dpr[...] = (dm * gt).astype(dpr.dtype)
            dl = dm * pr[...] * gt * (1.0 - gt)
            dl_ref[:, b * D_MODEL:(b + 1) * D_MODEL] = dl.astype(dl_ref.dtype)
            part = jnp.sum(dl, axis=0, keepdims=True)

            @pl.when(i == 0)
            def _():
                db_ref[:, b * D_MODEL:(b + 1) * D_MODEL] = part

            @pl.when(i > 0)
            def _():
                db_ref[:, b * D_MODEL:(b + 1) * D_MODEL] += part

    return pl.pallas_call(
        body, name=name, grid=(T // tm,), in_specs=[_row_spec(tm, D_MODEL)] + lspecs + bspecs + pspecs,
        out_specs=(_row_spec(tm, GATE_W), *[_row_spec(tm, D_MODEL) for _ in range(N_BRANCH)], _vec_spec(GATE_W)),
        out_shape=(jax.ShapeDtypeStruct((T, GATE_W), BF16),
                   *[jax.ShapeDtypeStruct((T, D_MODEL), BF16) for _ in range(N_BRANCH)],
                   jax.ShapeDtypeStruct((1, GATE_W), F32)),
        compiler_params=_cparams(("arbitrary",)),
    )(dmerged, proj, proj, proj, b_gate, b_gate, b_gate, *ps)


def _rope_tables(pos_ref, invf_ref):
    ang = pos_ref[...] * invf_ref[...]
    lane = lax.broadcasted_iota(jnp.int32, (1, LANES), 1)
    s = jnp.sin(ang)
    split = MLA_NOPE + MLA_ROPE // 2
    return jnp.cos(ang), jnp.where(lane >= split, s, 0.0), jnp.where(lane < split, s, 0.0)


def _rotate(xh, c, s_hi, s_lo, sign):
    half = MLA_ROPE // 2
    up = pltpu.roll(xh, half, 1)
    down = pltpu.roll(xh, LANES - half, 1)
    return xh * c + sign * (up * s_hi - down * s_lo)


def rope_fwd(qpre, kpre, proj, pos, invf, *, name):
    T = qpre.shape[0]
    tm = _pick(T, ROW_TILE)
    kr_col = PAD_OFF["kr"] // KR_PAD

    def body(q_ref, k_ref, kr_ref, pos_ref, invf_ref, qo_ref, ko_ref):
        c, s_hi, s_lo = _rope_tables(pos_ref, invf_ref)
        kr = _rotate(pltpu.roll(kr_ref[...], MLA_NOPE, 1), c, s_hi, s_lo, 1.0)
        for h in range(MLA_HEADS):
            sl = slice(h * HEAD_PAD, (h + 1) * HEAD_PAD)
            qo_ref[:, sl] = _rotate(q_ref[:, sl], c, s_hi, s_lo, 1.0).astype(qo_ref.dtype)
            ko_ref[:, sl] = (k_ref[:, sl] + kr).astype(ko_ref.dtype)

    rs = _row_spec(tm, MLA_PAD_W)
    return pl.pallas_call(
        body, name=name, grid=(T // tm,),
        in_specs=[rs, rs, _row_spec(tm, KR_PAD, kr_col), _row_spec(tm, 1), _vec_spec(LANES)],
        out_specs=(rs, rs),
        out_shape=(jax.ShapeDtypeStruct((T, MLA_PAD_W), BF16), jax.ShapeDtypeStruct((T, MLA_PAD_W), BF16)),
        compiler_params=_cparams(("parallel",)),
    )(qpre, kpre, proj, pos, invf)


def rope_bwd(dq_pad, dk_pad, pos, invf, *, name):
    T = dq_pad.shape[0]
    tm = _pick(T, ROW_TILE)

    def body(dq_ref, dk_ref, pos_ref, invf_ref, dqo_ref, dkr_ref):
        c, s_hi, s_lo = _rope_tables(pos_ref, invf_ref)
        tot = None
        for h in range(MLA_HEADS):
            sl = slice(h * HEAD_PAD, (h + 1) * HEAD_PAD)
            dqo_ref[:, sl] = _rotate(dq_ref[:, sl], c, s_hi, s_lo, -1.0).astype(dqo_ref.dtype)
            tot = dk_ref[:, sl] if tot is None else tot + dk_ref[:, sl]
        dkr = pltpu.roll(_rotate(tot, c, s_hi, s_lo, -1.0), LANES - MLA_NOPE, 1)
        lane = lax.broadcasted_iota(jnp.int32, (1, LANES), 1)
        dkr_ref[...] = jnp.where(lane < MLA_ROPE, dkr, 0.0).astype(dkr_ref.dtype)

    rs = _row_spec(tm, MLA_PAD_W)
    return pl.pallas_call(
        body, name=name, grid=(T // tm,),
        in_specs=[rs, rs, _row_spec(tm, 1), _vec_spec(LANES)],
        out_specs=(rs, _row_spec(tm, KR_PAD)),
        out_shape=(jax.ShapeDtypeStruct((T, MLA_PAD_W), BF16), jax.ShapeDtypeStruct((T, KR_PAD), BF16)),
        compiler_params=_cparams(("parallel",)),
    )(dq_pad, dk_pad, pos, invf)


ATT_TILE = 256
NEG = -1e30


def _split_bf16(v):
    hi = v.astype(BF16)
    return hi, (v - hi.astype(F32)).astype(BF16)


def _dot(a, b, ca, cb):
    return lax.dot_general(a, b, (((ca,), (cb,)), ((), ())), preferred_element_type=F32)


def _tri(tk, kind):
    r = lax.broadcasted_iota(jnp.int32, (tk, tk), 0)
    c = lax.broadcasted_iota(jnp.int32, (tk, tk), 1)
    m = {"gt": r > c, "le": r <= c, "lt": r < c}[kind]
    return jnp.where(m, 1.0, 0.0).astype(BF16)


def _cum(v, tri):
    hi, lo = _split_bf16(v)
    return _dot(hi, tri, 1, 0) + _dot(lo, tri, 1, 0)


def _lane_masks(hp, w):
    lane = lax.broadcasted_iota(jnp.int32, (1, LANES), 1)
    return [(lane >= e * w) & (lane < (e + 1) * w) for e in range(hp)]


def _att_dims(mode, hp, qk_w, v_w):
    assert mode in ("softmax", "sb")
    assert (hp, qk_w, v_w) in ((2, 128, 64), (2, 64, 64), (1, 128, 128))
    qw = hp * LANES if qk_w == LANES else LANES
    return qw


def attn_fwd(q, k, v, *, name, mode, causal, hp, qk_w, v_w, scale, nh, qc, kc, vc):
    qw = _att_dims(mode, hp, qk_w, v_w)
    B, S = q.shape[0], q.shape[1]
    Sk = k.shape[1]
    tq = _pick(S, ATT_TILE)
    tk = _pick(Sk, ATT_TILE)
    if causal:
        assert tq == tk and S == Sk
    nq, nkv = S // tq, Sk // tk

    def body(q_ref, k_ref, v_ref, o_ref, st_ref):
        i = pl.program_id(2)
        qv = q_ref[0].astype(BF16)
        masks_qk = _lane_masks(hp, qk_w) if qk_w < LANES else None
        masks_v = _lane_masks(hp, v_w) if v_w < LANES else None
        if qk_w == LANES:
            qs = [qv[:, e * LANES:(e + 1) * LANES] for e in range(hp)]
        else:
            qs = [jnp.where(masks_qk[e], qv, jnp.zeros_like(qv)) for e in range(hp)]
        rows = lax.broadcasted_iota(jnp.int32, (tq, tk), 0)
        cols = lax.broadcasted_iota(jnp.int32, (tq, tk), 1)
        tri = _tri(tk, "gt") if mode == "sb" else None

        def kv_tile(j):
            off = pl.multiple_of(j * tk, tk)
            kj = k_ref[0, pl.ds(off, tk), :].astype(BF16)
            vj = v_ref[0, pl.ds(off, tk), :].astype(BF16)
            ks = [kj[:, e * LANES:(e + 1) * LANES] for e in range(hp)] if qk_w == LANES else [kj] * hp
            return ks, vj

        def merge(vals):
            if hp == 1:
                return jnp.broadcast_to(vals[0], (tq, LANES))
            return jnp.where(masks_v[0], vals[0], vals[1])

        if mode == "softmax":
            def block(j, carry, diag):
                ms, ls, acc = carry
                ks, vj = kv_tile(j)
                new_m, new_l, alphas, pvs = [], [], [], []
                for e in range(hp):
                    s = _dot(qs[e], ks[e], 1, 1) * scale
                    if diag:
                        s = jnp.where(rows >= cols, s, NEG)
                    m_new = jnp.maximum(ms[e], jnp.max(s, axis=1, keepdims=True))
                    alpha = jnp.exp(ms[e] - m_new)
                    p = jnp.exp(s - m_new)
                    new_l.append(alpha * ls[e] + jnp.sum(p, axis=1, keepdims=True))
                    new_m.append(m_new)
                    alphas.append(alpha)
                    pvs.append(_dot(p.astype(BF16), vj, 1, 0))
                acc = acc * merge(alphas) + merge(pvs)
                return tuple(new_m), tuple(new_l), acc

            init = (tuple(jnp.full((tq, 1), NEG, F32) for _ in range(hp)),
                    tuple(jnp.zeros((tq, 1), F32) for _ in range(hp)),
                    jnp.zeros((tq, LANES), F32))
            if causal:
                carry = lax.fori_loop(0, i, lambda j, c: block(j, c, False), init)
                ms, ls, acc = block(i, carry, True)
            else:
                ms, ls, acc = lax.fori_loop(0, nkv, lambda j, c: block(j, c, False), init)
            o_ref[0] = acc / merge(list(ls))
            st_ref[0, 0] = merge([m + jnp.log(l) for m, l in zip(ms, ls)])
        else:
            def block(j, carry, diag):
                cs_, acc = carry
                ks, vj = kv_tile(j)
                new_c, pvs = [], []
                for e in range(hp):
                    z = _dot(qs[e], ks[e], 1, 1) * scale
                    lg = jnp.log(1.0 + jnp.exp(-jnp.abs(z)))
                    lz = jnp.minimum(z, 0.0) - lg
                    lk = -jnp.maximum(z, 0.0) - lg
                    if diag:
                        lk = jnp.where(rows > cols, lk, 0.0)
                    a = jnp.exp(lz + _cum(lk, tri) + cs_[e])
                    if diag:
                        a = jnp.where(rows > cols, a, 0.0)
                    pvs.append(_dot(a.astype(BF16), vj, 1, 0))
                    new_c.append(cs_[e] + jnp.sum(lk, axis=1, keepdims=True))
                return tuple(new_c), acc + merge(pvs)

            init = (tuple(jnp.zeros((tq, 1), F32) for _ in range(hp)), jnp.zeros((tq, LANES), F32))
            carry = block(i, init, True)
            cs_, acc = lax.fori_loop(0, i, lambda jj, c: block(i - 1 - jj, c, False), carry)
            o_ref[0] = acc
            st_ref[0, 0] = merge(list(cs_))

    return pl.pallas_call(
        body, name=name, grid=(B, nh, nq),
        in_specs=[pl.BlockSpec((1, tq, qw), lambda b, h, i: (b, i, qc + h)),
                  pl.BlockSpec((1, Sk, qw), lambda b, h, i: (b, 0, kc + h)),
                  pl.BlockSpec((1, Sk, LANES), lambda b, h, i: (b, 0, vc + h))],
        out_specs=(pl.BlockSpec((1, tq, LANES), lambda b, h, i: (b, i, h)),
                   pl.BlockSpec((1, 1, tq, LANES), lambda b, h, i: (b, h, i, 0))),
        out_shape=(jax.ShapeDtypeStruct((B, S, nh * LANES), F32), jax.ShapeDtypeStruct((B, nh, S, LANES), F32)),
        compiler_params=_cparams(("parallel", "parallel", "arbitrary")),
    )(q, k, v)


def attn_bwd(q, k, v, o, st, do, *, name, mode, causal, hp, qk_w, v_w, scale, nh, qc, kc, vc):
    qw = _att_dims(mode, hp, qk_w, v_w)
    B, S = q.shape[0], q.shape[1]
    Sk = k.shape[1]
    tq = _pick(S, ATT_TILE)
    tk = _pick(Sk, ATT_TILE)
    if causal:
        assert tq == tk and S == Sk
    nq, nkv = S // tq, Sk // tk

    def body(q_ref, k_ref, v_ref, o_ref, st_ref, do_ref, dq_ref, dk_ref, dv_ref):
        i = pl.program_id(2)

        @pl.when(i == 0)
        def _():
            dk_ref[...] = jnp.zeros_like(dk_ref)
            dv_ref[...] = jnp.zeros_like(dv_ref)

        qv = q_ref[0].astype(BF16)
        dov = do_ref[0]
        stv = st_ref[0, 0]
        masks_qk = _lane_masks(hp, qk_w) if qk_w < LANES else None
        masks_v = _lane_masks(hp, v_w) if v_w < LANES else None
        if qk_w == LANES:
            qs = [qv[:, e * LANES:(e + 1) * LANES] for e in range(hp)]
        else:
            qs = [jnp.where(masks_qk[e], qv, jnp.zeros_like(qv)) for e in range(hp)]
        if hp == 1:
            dos = [dov.astype(BF16)]
            stats = [stv[:, 0:1]]
        else:
            dos = [jnp.where(masks_v[e], dov, 0.0).astype(BF16) for e in range(hp)]
            stats = [stv[:, e * v_w:e * v_w + 1] for e in range(hp)]
        if mode == "softmax":
            prod = dov * o_ref[0]
            if hp == 1:
                dsum = [jnp.sum(prod, axis=1, keepdims=True)]
            else:
                dsum = [jnp.sum(jnp.where(masks_v[e], prod, 0.0), axis=1, keepdims=True) for e in range(hp)]
        rows = lax.broadcasted_iota(jnp.int32, (tq, tk), 0)
        cols = lax.broadcasted_iota(jnp.int32, (tq, tk), 1)
        if mode == "sb":
            tri_le, tri_lt = _tri(tk, "le"), _tri(tk, "lt")

        def kv_tile(j):
            off = pl.multiple_of(j * tk, tk)
            kj = k_ref[0, pl.ds(off, tk), :].astype(BF16)
            vj = v_ref[0, pl.ds(off, tk), :].astype(BF16)
            ks = [kj[:, e * LANES:(e + 1) * LANES] for e in range(hp)] if qk_w == LANES else [kj] * hp
            return off, ks, vj

        def scatter(off, dz_list, p_list):
            dvj = None
            for e in range(hp):
                t = _dot(p_list[e], dos[e], 0, 0)
                dvj = t if dvj is None else dvj + t
            dv_ref[0, pl.ds(off, tk), :] += dvj
            if qk_w == LANES:
                for e in range(hp):
                    dk_ref[0, pl.ds(off, tk), e * LANES:(e + 1) * LANES] += _dot(dz_list[e], qs[e], 0, 0)
            else:
                dkj = None
                for e in range(hp):
                    t = _dot(dz_list[e], qs[e], 0, 0)
                    dkj = t if dkj is None else dkj + t
                dk_ref[0, pl.ds(off, tk), :] += dkj

        def dq_add(dqs, dz_list, ks):
            out = []
            for e in range(hp):
                out.append(dqs[e] + _dot(dz_list[e], ks[e], 1, 0))
            return tuple(out)

        dq0 = tuple(jnp.zeros((tq, LANES), F32) for _ in range(hp))

        if mode == "softmax":
            def block(j, dqs, diag):
                off, ks, vj = kv_tile(j)
                dzs, ps = [], []
                for e in range(hp):
                    s = _dot(qs[e], ks[e], 1, 1) * scale
                    p = jnp.exp(s - stats[e])
                    if diag:
                        p = jnp.where(rows >= cols, p, 0.0)
                    dp = _dot(dos[e], vj, 1, 1)
                    dzs.append((p * (dp - dsum[e]) * scale).astype(BF16))
                    ps.append(p.astype(BF16))
                scatter(off, dzs, ps)
                return dq_add(dqs, dzs, ks)

            if causal:
                dqs = lax.fori_loop(0, i, lambda j, c: block(j, c, False), dq0)
                dqs = block(i, dqs, True)
            else:
                dqs = lax.fori_loop(0, nkv, lambda j, c: block(j, c, False), dq0)
        else:
            def block(j, carry, diag):
                dqs, cps, cgs = carry
                off, ks, vj = kv_tile(j)
                dzs, ps, new_p, new_g = [], [], [], []
                for e in range(hp):
                    z = _dot(qs[e], ks[e], 1, 1) * scale
                    ez = jnp.exp(-jnp.abs(z))
                    lg = jnp.log(1.0 + ez)
                    lz = jnp.minimum(z, 0.0) - lg
                    lk = -jnp.maximum(z, 0.0) - lg
                    if diag:
                        lk = jnp.where(rows > cols, lk, 0.0)
                    keep_after = stats[e] - cps[e] - _cum(lk, tri_le)
                    a = jnp.exp(lz + keep_after)
                    if diag:
                        a = jnp.where(rows > cols, a, 0.0)
                    g = _dot(dos[e], vj, 1, 1) * a
                    gsum = cgs[e] + _cum(g, tri_lt)
                    inv = 1.0 / (1.0 + ez)
                    sig = jnp.where(z >= 0.0, inv, ez * inv)
                    dz = (g * (1.0 - sig) - gsum * sig) * scale
                    if diag:
                        dz = jnp.where(rows > cols, dz, 0.0)
                    dzs.append(dz.astype(BF16))
                    ps.append(a.astype(BF16))
                    new_p.append(cps[e] + jnp.sum(lk, axis=1, keepdims=True))
                    new_g.append(cgs[e] + jnp.sum(g, axis=1, keepdims=True))
                scatter(off, dzs, ps)
                return dq_add(dqs, dzs, ks), tuple(new_p), tuple(new_g)

            zc = tuple(jnp.zeros((tq, 1), F32) for _ in range(hp))
            carry = lax.fori_loop(0, i, lambda j, c: block(j, c, False), (dq0, zc, zc))
            dqs, _, _ = block(i, carry, True)

        if qk_w == LANES:
            for e in range(hp):
                dq_ref[0, :, e * LANES:(e + 1) * LANES] = dqs[e]
        else:
            dq_ref[0] = jnp.where(masks_qk[0], dqs[0], dqs[1])

    return pl.pallas_call(
        body, name=name, grid=(B, nh, nq),
        in_specs=[pl.BlockSpec((1, tq, qw), lambda b, h, i: (b, i, qc + h)),
                  pl.BlockSpec((1, Sk, qw), lambda b, h, i: (b, 0, kc + h)),
                  pl.BlockSpec((1, Sk, LANES), lambda b, h, i: (b, 0, vc + h)),
                  pl.BlockSpec((1, tq, LANES), lambda b, h, i: (b, i, h)),
                  pl.BlockSpec((1, 1, tq, LANES), lambda b, h, i: (b, h, i, 0)),
                  pl.BlockSpec((1, tq, LANES), lambda b, h, i: (b, i, h))],
        out_specs=(pl.BlockSpec((1, tq, qw), lambda b, h, i: (b, i, h)),
                   pl.BlockSpec((1, Sk, qw), lambda b, h, i: (b, 0, h)),
                   pl.BlockSpec((1, Sk, LANES), lambda b, h, i: (b, 0, h))),
        out_shape=(jax.ShapeDtypeStruct((B, S, nh * qw), F32), jax.ShapeDtypeStruct((B, Sk, nh * qw), F32),
                   jax.ShapeDtypeStruct((B, Sk, nh * LANES), F32)),
        compiler_params=_cparams(("parallel", "parallel", "arbitrary")),
    )(q, k, v, o, st, do)


def _peer(x, y, c, k):
    return (x ^ ((k >> 2) & 1), y ^ ((k >> 1) & 1), c ^ (k & 1))


def _pos(p):
    return 4 * p[0] + 2 * p[1] + p[2]


def exchange(buf, *, name, gather):
    R = buf.shape[-2]

    def body(x_ref, out_ref, send_sems, recv_sems, local_sem):
        me = (lax.axis_index("x"), lax.axis_index("y"), lax.axis_index("c"))
        mine = _pos(me)

        def src(p):
            return x_ref if gather else x_ref.at[_pos(p)]

        local = pltpu.make_async_copy(src(me), out_ref.at[mine], local_sem)
        local.start()
        sends = []
        for k in range(1, N_DEV):
            peer = _peer(*me, k)
            cp = pltpu.make_async_remote_copy(
                src_ref=src(peer), dst_ref=out_ref.at[mine], send_sem=send_sems.at[k - 1], recv_sem=recv_sems.at[k - 1],
                device_id=peer, device_id_type=pl.DeviceIdType.MESH)
            cp.start()
            sends.append(cp)
        for k in range(1, N_DEV):
            peer = _peer(*me, k)
            pltpu.make_async_remote_copy(
                src_ref=src(peer), dst_ref=out_ref.at[_pos(peer)], send_sem=send_sems.at[k - 1],
                recv_sem=recv_sems.at[k - 1], device_id=peer, device_id_type=pl.DeviceIdType.MESH).wait_recv()
        for cp in sends:
            cp.wait_send()
        local.wait()

    return pl.pallas_call(
        body, name=name,
        in_specs=[pl.BlockSpec(memory_space=pl.ANY)], out_specs=pl.BlockSpec(memory_space=pl.ANY),
        out_shape=jax.ShapeDtypeStruct((N_DEV, R, LANES), buf.dtype),
        scratch_shapes=[pltpu.SemaphoreType.DMA((N_DEV - 1,)), pltpu.SemaphoreType.DMA((N_DEV - 1,)),
                        pltpu.SemaphoreType.DMA],
        compiler_params=pltpu.CompilerParams(has_side_effects=True),
    )(buf)


def sum_adamw(parts, w, m, v, *, name):
    R = w.shape[0]
    tr = min(R, PACK_TILE)
    assert R % tr == 0
    c1 =1.0 - ADAM_B1 ** ADAM_STEP
    c2 = 1.0 - ADAM_B2 ** ADAM_STEP

    def body(p_ref, w_ref, m_ref, v_ref, g_ref, d_ref, mo_ref, vo_ref):
        g = p_ref[0]
        for s in range(1, N_DEV):
            g = g + p_ref[s]
        mn = ADAM_B1 * m_ref[...] + (1.0 - ADAM_B1) * g
        vn = ADAM_B2 * v_ref[...] + (1.0 - ADAM_B2) * jnp.square(g)
        g_ref[...] = g
        mo_ref[...] = mn
        vo_ref[...] = vn
        d_ref[...] = -ADAM_LR * ((mn / c1) / (jnp.sqrt(vn / c2) + ADAM_EPS) + ADAM_WD * w_ref[...])

    rs = pl.BlockSpec((tr, LANES), lambda i: (i, 0))
    sd = jax.ShapeDtypeStruct((R, LANES), F32)
    return pl.pallas_call(
        body, name=name, grid=(R // tr,),
        in_specs=[pl.BlockSpec((N_DEV, tr, LANES), lambda i: (0, i, 0)), rs, rs, rs],
        out_specs=(rs, rs, rs, rs), out_shape=(sd, sd, sd, sd),
        compiler_params=_cparams(("parallel",)),
    )(parts, w, m, v)


WEIGHTS = ("ln_mix_pre", "w_in", "b_gate", "q_norm", "w_uq", "kv_norm", "w_uk", "w_uv", "mem_norm", "w_mem_kv",
           "w_branch_out", "w_out", "ln_mix_post", "ln_mlp_pre", "w_mlp_up", "w_mlp_down", "ln_mlp_post")
BIG = dict(w_in=((D_MODEL, IN_WIDTH), 1), w_uq=((Q_LORA, MLA_HEADS * (MLA_NOPE + MLA_ROPE)), 1),
           w_uk=((KV_LORA, MLA_HEADS * MLA_NOPE), 1), w_uv=((KV_LORA, MLA_HEADS * MLA_V), 1),
           w_mem_kv=((D_MODEL, 2 * QM_W), 0), w_branch_out=((N_BRANCH, BRANCH_W, D_MODEL), 2),
           w_out=((D_MODEL, D_MODEL), 0), w_mlp_up=((D_MODEL, D_FF), 1), w_mlp_down=((D_FF, D_MODEL), 0))
SMALL = tuple(n for n in WEIGHTS if n not in BIG)


def _shard_shape(name):
    full, ax = BIG[name]
    return tuple(s // N_DEV if a == ax else s for a, s in enumerate(full))


PACK_TILE = 512


def _pad_rows(a):
    r = a.shape[-2]
    to = PACK_TILE if r > PACK_TILE else 8
    pad = [(0, 0)] * a.ndim
    pad[-2] = (0, (-r) % to)
    return jnp.pad(a, pad)


def _pack_rows(arrs):
    return _pad_rows(jnp.concatenate([a.reshape(-1, LANES) for a in arrs], axis=0))


def _unpack_rows(packed, shapes, lead=()):
    out, r = [], 0
    for shp in shapes:
        n = math.prod(shp) // LANES
        out.append(packed[..., r:r + n, :].reshape(lead + tuple(shp)))
        r += n
    return out


def _slots_to_full(name, slots):
    full, ax = BIG[name]
    return jnp.moveaxis(slots, 0, ax).reshape(full)


def _full_to_slots(name, w):
    full, ax = BIG[name]
    split = full[:ax] + (N_DEV, full[ax] // N_DEV) + full[ax + 1:]
    return jnp.moveaxis(w.reshape(split), ax, 0)


def _split_in(w):
    offs = np.cumsum((0,) + IN_SIZES)
    names = ("cq", "ckv", "kr", "sb", "qm", "gate")
    return {n: w[:, offs[i]:offs[i + 1]] for i, n in enumerate(names)}


def _pad_in(w):
    p = _split_in(w)
    p["kr"] = jnp.pad(p["kr"], ((0, 0), (0, KR_PAD - MLA_ROPE)))
    return jnp.concatenate([p[n] for n in PAD_ORDER], axis=1)


def _unpad_in(wp):
    p = {n: wp[:, PAD_OFF[n]:PAD_OFF[n] + PAD_SIZES[n]] for n in PAD_ORDER}
    p["kr"] = p["kr"][:, :MLA_ROPE]
    return jnp.concatenate([p[n] for n in ("cq", "ckv", "kr", "sb", "qm", "gate")], axis=1)


def _pad_heads(w, width):
    r = w.shape[0]
    return jnp.pad(w.reshape(r, MLA_HEADS, width), ((0, 0), (0, 0), (0, HEAD_PAD - width))).reshape(r, MLA_PAD_W)


def _unpad_heads(wp, width):
    r = wp.shape[0]
    return wp.reshape(r, MLA_HEADS, HEAD_PAD)[:, :, :width].reshape(r, MLA_HEADS * width)


def _rope_inv_freq():
    half = MLA_ROPE // 2
    inv = 1.0 / (ROPE_THETA ** (jnp.arange(half, dtype=F32) * (2.0 / MLA_ROPE)))
    tab = jnp.zeros((LANES,), F32)
    tab = tab.at[MLA_NOPE:MLA_NOPE + half].set(inv).at[MLA_NOPE + half:MLA_NOPE + MLA_ROPE].set(inv)
    return tab.reshape(1, LANES)


def _local_step(x, mem, positions, tgt, sm, W):
    B, S, D = x.shape
    M = mem.shape[1]
    T = B * S
    x2 = x.reshape(T, D)
    mem2 = mem.reshape(B * M, D)
    pos = positions.reshape(T, 1).astype(F32)
    invf = _rope_inv_freq()
    w_in_pad = _pad_in(W["w_in"])
    w_uq_pad = _pad_heads(W["w_uq"], MLA_NOPE + MLA_ROPE)
    w_uk_pad = _pad_heads(W["w_uk"], MLA_NOPE)
    cq_col, ckv_col = PAD_OFF["cq"] // Q_LORA, PAD_OFF["ckv"] // KV_LORA
    sb_col, qm_col = PAD_OFF["sb"] // LANES, PAD_OFF["qm"] // LANES
    sb_blk = SB_HEADS * SB_DIM // LANES
    mla = dict(mode="softmax", causal=True, hp=2, qk_w=128, v_w=64, scale=(MLA_NOPE + MLA_ROPE) ** -0.5,
               nh=MLA_HEADS // 2, qc=0, kc=0, vc=0)
    sbk = dict(mode="sb", causal=True, hp=2, qk_w=64, v_w=64, scale=SB_DIM ** -0.5, nh=SB_HEADS // 2,
               qc=sb_col, kc=sb_col + sb_blk, vc=sb_col + 2 * sb_blk)
    mca = dict(mode="softmax", causal=False, hp=1, qk_w=128, v_w=128, scale=MEM_DIM ** -0.5, nh=MEM_HEADS,
               qc=qm_col, kc=0, vc=MEM_HEADS)

    h = rms_fwd(x2, sm["ln_mix_pre"], name="rms_mix_pre")
    proj = matmul(h, w_in_pad, name="mm_in")
    proj3 = proj.reshape(B, S, IN_PAD)
    cqn = rms_fwd(proj, sm["q_norm"], n=Q_LORA, col=cq_col, name="rms_q")
    ckvn = rms_fwd(proj, sm["kv_norm"], n=KV_LORA, col=ckv_col, name="rms_kv")
    qpre = matmul(cqn, w_uq_pad, name="mm_uq")
    kpre = matmul(ckvn, w_uk_pad, name="mm_uk")
    v_mla = matmul(ckvn, W["w_uv"], out_dtype=BF16, name="mm_uv").reshape(B, S, -1)
    q_pad, k_pad = rope_fwd(qpre, kpre, proj, pos, invf, name="rope_fwd")
    q_pad, k_pad = q_pad.reshape(B, S, -1), k_pad.reshape(B, S, -1)
    o_mla, st_mla = attn_fwd(q_pad, k_pad, v_mla, name="mla_fwd", **mla)
    o_sb, st_sb = attn_fwd(proj3, proj3, proj3, name="sb_fwd", **sbk)
    memh = rms_fwd(mem2, sm["mem_norm"], name="rms_mem")
    mkv = matmul(memh, W["w_mem_kv"], out_dtype=BF16, name="mm_memkv").reshape(B, M, -1)
    o_mem, st_mem = attn_fwd(proj3, mkv, mkv, name="mem_fwd", **mca)
    outs = [o.reshape(T, BRANCH_W) for o in (o_mla, o_sb, o_mem)]
    ps = [matmul(o, W["w_branch_out"][b], name=f"mm_bo{b}") for b, o in enumerate(outs)]
    merged = gate_merge(proj, sm["b_gate"], ps, name="gate_merge")
    y = matmul(merged, W["w_out"], name="mm_out")
    x1, h2 = mix_post_fwd(x2, y, sm["ln_mix_post"], sm["ln_mlp_pre"], name="mix_post")
    a, u = matmul(h2, W["w_mlp_up"], act="relu2", name="mm_up")
    d = matmul(u, W["w_mlp_down"], name="mm_down")
    loss_p, dx2, dd, dg_mlp_post = loss_head(x1, d, sm["ln_mlp_post"], tgt.reshape(T, D), name="loss_head")

    da = matmul(dd, W["w_mlp_down"], tb=True, act="relu2_bwd", act_in=a, out_dtype=BF16, name="mm_down_dx")
    g_down = matmul(u, dd, ta=True, name="mm_down_dw")
    dh2 = matmul(da, W["w_mlp_up"], tb=True, name="mm_up_dx")
    g_up = matmul(h2, da, ta=True, name="mm_up_dw")
    dx1, dy, dg_mlp_pre, dg_mix_post = mlp_pre_bwd(dx2, dh2, x1, sm["ln_mlp_pre"], y, sm["ln_mix_post"], name="mlp_pre_bwd")
    dmerged = matmul(dy, W["w_out"], tb=True, name="mm_out_dx")
    g_out = matmul(merged, dy, ta=True, name="mm_out_dw")
    dlog, dp0, dp1, dp2, db_gate = gate_bwd(dmerged, proj, sm["b_gate"], ps, name="gate_bwd")
    dps = (dp0, dp1, dp2)
    dos = [matmul(dps[b], W["w_branch_out"][b], tb=True, name=f"mm_bo{b}_dx").reshape(B, S, BRANCH_W) for b in range(N_BRANCH)]
    g_bo = jnp.stack([matmul(outs[b], dps[b], ta=True, name=f"mm_bo{b}_dw") for b in range(N_BRANCH)])
    dq_pad, dk_pad, dv_mla = attn_bwd(q_pad, k_pad, v_mla, o_mla, st_mla, dos[0], name="mla_bwd", **mla)
    dsq, dsk, dsv = attn_bwd(proj3, proj3, proj3, o_sb, st_sb, dos[1], name="sb_bwd", **sbk)
    dqm, dmk, dmv = attn_bwd(proj3, mkv, mkv, o_mem, st_mem, dos[2], name="mem_bwd", **mca)
    dmkv = jnp.concatenate([dmk, dmv], axis=-1).astype(BF16).reshape(B * M, -1)
    dmemh = matmul(dmkv, W["w_mem_kv"], tb=True, name="mm_memkv_dx")
    g_memkv = matmul(memh, dmkv, ta=True, name="mm_memkv_dw")
    _, dg_mem_norm = rms_bwd(dmemh, mem2, sm["mem_norm"], name="rms_mem_bwd")
    dq_pad, dk_pad, dv_mla = dq_pad.reshape(T, -1), dk_pad.reshape(T, -1), dv_mla.reshape(T, -1)
    dqpre, dkr = rope_bwd(dq_pad, dk_pad, pos, invf, name="rope_bwd")
    dcqn = matmul(dqpre, w_uq_pad, tb=True, name="mm_uq_dx")
    g_uq = _unpad_heads(matmul(cqn, dqpre, ta=True, name="mm_uq_dw"), MLA_NOPE + MLA_ROPE)
    dckvn = matmul(dk_pad, w_uk_pad, tb=True, name="mm_uk_dx")
    dckvn = matmul(dv_mla, W["w_uv"], tb=True, acc=dckvn, name="mm_uv_dx")
    g_uk = _unpad_heads(matmul(ckvn, dk_pad, ta=True, name="mm_uk_dw"), MLA_NOPE)
    g_uv = matmul(ckvn, dv_mla, ta=True, name="mm_uv_dw")
    dcq, dg_q_norm = rms_bwd(dcqn, proj, sm["q_norm"], n=Q_LORA, col=cq_col, out_dtype=BF16, name="rms_q_bwd")
    dckv, dg_kv_norm = rms_bwd(dckvn, proj, sm["kv_norm"], n=KV_LORA, col=ckv_col, out_dtype=BF16, name="rms_kv_bwd")
    pieces = dict(gate=dlog, sb=jnp.concatenate([dsq, dsk, dsv], axis=-1).reshape(T, -1), qm=dqm.reshape(T, -1),
                  ckv=dckv, cq=dcq, kr=dkr)
    dproj = jnp.concatenate([pieces[n].astype(BF16) for n in PAD_ORDER], axis=1)
    dh = matmul(dproj, w_in_pad, tb=True, name="mm_in_dx")
    g_in = _unpad_in(matmul(h, dproj, ta=True, name="mm_in_dw"))
    dx, dg_mix_pre = rms_bwd(dh, x2, sm["ln_mix_pre"], residual=dx1, name="rms_mix_pre_bwd")

    big = dict(w_in=g_in, w_uq=g_uq, w_uk=g_uk, w_uv=g_uv, w_mem_kv=g_memkv, w_branch_out=g_bo, w_out=g_out,
               w_mlp_up=g_up, w_mlp_down=g_down)
    small = dict(ln_mix_pre=dg_mix_pre, b_gate=db_gate, q_norm=dg_q_norm, kv_norm=dg_kv_norm, mem_norm=dg_mem_norm,
                 ln_mix_post=dg_mix_post, ln_mlp_pre=dg_mlp_pre, ln_mlp_post=dg_mlp_post)
    return loss_p[0, 0], dx.reshape(B, S, D), big, small


def kernel(x, mem, positions, ln_mix_pre, w_in, b_gate, q_norm, w_uq, kv_norm, w_uk, w_uv, mem_norm, w_mem_kv, w_branch_out, w_out, ln_mix_post, ln_mlp_pre, w_mlp_up, w_mlp_down, ln_mlp_post, loss_target, m_ln_mix_pre, m_w_in, m_b_gate, m_q_norm, m_w_uq, m_kv_norm, m_w_uk, m_w_uv, m_mem_norm, m_w_mem_kv, m_w_branch_out, m_w_out, m_ln_mix_post, m_ln_mlp_pre, m_w_mlp_up, m_w_mlp_down, m_ln_mlp_post, v_ln_mix_pre, v_w_in, v_b_gate, v_q_norm, v_w_uq, v_kv_norm, v_w_uk, v_w_uv, v_mem_norm, v_w_mem_kv, v_w_branch_out, v_w_out, v_ln_mix_post, v_ln_mlp_pre, v_w_mlp_up, v_w_mlp_down, v_ln_mlp_post):
    given = dict(locals())
    w = {n: given[n][0] for n in WEIGHTS}
    m = {n: given["m_" + n][0] for n in WEIGHTS}
    v = {n: given["v_" + n][0] for n in WEIGHTS}
    big_names = tuple(BIG)
    shard_shapes = [_shard_shape(n) for n in big_names]

    gathered = exchange(_pack_rows([w[n].astype(BF16) for n in big_names]), gather=True, name="gather_weights")
    slots = _unpack_rows(gathered, shard_shapes, lead=(N_DEV,))
    W = {n: _slots_to_full(n, s) for n, s in zip(big_names, slots)}
    sm = {n: w[n].reshape(1, -1) for n in SMALL}

    loss_part, grad_x, g_big, g_small = _local_step(x, mem, positions, loss_target, sm, W)
    loss = lax.psum(loss_part, ("x", "y", "c"))

    send = _pad_rows(jnp.concatenate([_full_to_slots(n, g_big[n]).reshape(N_DEV, -1, LANES) for n in big_names], axis=1))
    recv = exchange(send, gather=False, name="scatter_grads")
    res_big = sum_adamw(recv, *[_pack_rows([t[n] for n in big_names]) for t in (w, m, v)], name="adamw_sharded")
    res_big = [_unpack_rows(r, shard_shapes) for r in res_big]

    small_shapes = [w[n].shape for n in SMALL]
    parts = exchange(_pack_rows([g_small[n] for n in SMALL]), gather=True, name="gather_small_grads")
    res_small = sum_adamw(parts, *[_pack_rows([t[n] for n in SMALL]) for t in (w, m, v)], name="adamw_replicated")
    res_small = [_unpack_rows(r, small_shapes) for r in res_small]

    out = [loss, grad_x]
    for k in range(4):
        by_name = dict(zip(big_names, res_big[k]))
        by_name.update(zip(SMALL, res_small[k]))
        out += [by_name[n][None] for n in WEIGHTS]
    return tuple(out)
```

```python
import functools
import math

import numpy as np
import jax
import jax.numpy as jnp
from jax import lax
from jax.experimental import pallas as pl
from jax.experimental.pallas import tpu as pltpu

F32 = jnp.float32
BF16 = jnp.bfloat16

D_MODEL = 1024
MEM_HEADS, MEM_DIM = 4, 128
MLA_HEADS, MLA_NOPE, MLA_ROPE, MLA_V = 8, 64, 32, 64
Q_LORA, KV_LORA = 384, 256
ROPE_THETA = 10000.0
SB_HEADS, SB_DIM = 8, 64
D_FF = 4 * D_MODEL
N_BRANCH, BRANCH_W = 3, 512
EPS = 1e-6
SB_W = 3 * SB_HEADS * SB_DIM
QM_W = MEM_HEADS * MEM_DIM
GATE_W = N_BRANCH * D_MODEL
IN_SIZES = (Q_LORA, KV_LORA, MLA_ROPE, SB_W, QM_W, GATE_W)
IN_WIDTH = sum(IN_SIZES)
ADAM_LR, ADAM_B1, ADAM_B2, ADAM_EPS, ADAM_WD, ADAM_STEP = 0.001, 0.9, 0.999, 1e-08, 0.01, 10

N_DEV = 8
LANES = 128
KR_PAD = LANES
PAD_ORDER = ("gate", "cq", "kr", "ckv", "sb", "qm", "zpad")
MM_TILE = 1024
PAD_SIZES = dict(gate=GATE_W, cq=Q_LORA, kr=KR_PAD, ckv=KV_LORA, sb=SB_W, qm=QM_W)
PAD_SIZES["zpad"] = (-sum(PAD_SIZES.values())) % MM_TILE
PAD_OFF = {}
_o = 0
for _n in PAD_ORDER:
    PAD_OFF[_n] = _o
    _o += PAD_SIZES[_n]
IN_PAD = _o
HEAD_PAD = LANES
MLA_PAD_W = MLA_HEADS * HEAD_PAD
VMEM_LIMIT = 48 * 1024 * 1024


def _cparams(sem):
    return pltpu.CompilerParams(dimension_semantics=sem, vmem_limit_bytes=VMEM_LIMIT)


def _pick(n, pref):
    if n <= pref:
        return n
    t = (pref // LANES) * LANES
    while t >= LANES:
        if n % t == 0:
            return t
        t -= LANES
    return n


def matmul(a, b, *, name, ta=False, tb=False, out_dtype=F32, acc=None, act=None, act_in=None,
           tm_pref=MM_TILE, tn_pref=MM_TILE, tk_pref=MM_TILE):
    M, K = (a.shape[1], a.shape[0]) if ta else a.shape
    N = b.shape[0] if tb else b.shape[1]
    assert (b.shape[1] if tb else b.shape[0]) == K
    tm, tn, tk = _pick(M, tm_pref), _pick(N, tn_pref), _pick(K, tk_pref)
    nk = K // tk
    dims = (((0 if ta else 1,), (1 if tb else 0,)), ((), ()))
    assert act in (None, "relu2", "relu2_bwd") and (act == "relu2_bwd") == (act_in is not None)

    def body(*refs):
        a_ref, b_ref = refs[0], refs[1]
        pos = 2
        acc_ref = act_ref = None
        if acc is not None:
            acc_ref = refs[pos]
            pos += 1
        if act_in is not None:
            act_ref = refs[pos]
            pos += 1
        out = refs[pos]
        scratch = refs[pos + 1:]

        part = lax.dot_general(a_ref[...].astype(BF16), b_ref[...].astype(BF16), dims,
                               preferred_element_type=F32)

        def finish(r):
            if acc_ref is not None:
                r = r + acc_ref[...]
            if act == "relu2":
                r = jnp.square(jnp.maximum(r, 0.0))
            elif act == "relu2_bwd":
                r = r * (2.0 * jnp.sqrt(act_ref[...].astype(F32)))
            out[...] = r.astype(out.dtype)

        if nk == 1:
            finish(part)
        else:
            acc_sc = scratch[0]
            k = pl.program_id(2)

            @pl.when(k == 0)
            def _():
                acc_sc[...] = part

            @pl.when(k > 0)
            def _():
                acc_sc[...] += part

            @pl.when(k == nk - 1)
            def _():
                finish(acc_sc[...])

    a_spec = pl.BlockSpec((tk, tm), lambda i, j, k: (k, i)) if ta else pl.BlockSpec((tm, tk), lambda i, j, k: (i, k))
    b_spec = pl.BlockSpec((tn, tk), lambda i, j, k: (j, k)) if tb else pl.BlockSpec((tk, tn), lambda i, j, k: (k, j))
    o_spec = pl.BlockSpec((tm, tn), lambda i, j, k: (i, j))
    in_specs = [a_spec, b_spec]
    args = [a, b]
    if acc is not None:
        in_specs.append(o_spec)
        args.append(acc)
    if act_in is not None:
        in_specs.append(o_spec)
        args.append(act_in)
    return pl.pallas_call(
        body, name=name, grid=(M // tm, N // tn, nk), in_specs=in_specs, out_specs=o_spec,
        out_shape=jax.ShapeDtypeStruct((M, N), out_dtype),
        scratch_shapes=[pltpu.VMEM((tm, tn), F32)] if nk > 1 else [],
        compiler_params=_cparams(("parallel", "parallel", "arbitrary")),
    )(*args)


ROW_TILE = 256


def _rstd(xv):
    return lax.rsqrt(jnp.mean(xv * xv, axis=-1, keepdims=True) + EPS)


def _rms_bwd_rows(dy, xv, g):
    r = _rstd(xv)
    dyg = dy * g
    dx = r * dyg - xv * (r * r * r) * jnp.mean(dyg * xv, axis=-1, keepdims=True)
    return dx, dy * xv * r


def _row_spec(tm, n, col=0):
    return pl.BlockSpec((tm, n), lambda i: (i, col))


def _vec_spec(n):
    return pl.BlockSpec((1, n), lambda i: (0, 0))


def _acc_rows(ref, i, val):
    @pl.when(i == 0)
    def _():
        ref[...] = val

    @pl.when(i > 0)
    def _():
        ref[...] += val


def rms_fwd(x, g, *, name, n=None, col=0, out_dtype=BF16):
    T = x.shape[0]
    n = x.shape[1] if n is None else n
    tm = _pick(T, ROW_TILE)

    def body(x_ref, g_ref, o_ref):
        xv = x_ref[...]
        o_ref[...] = (xv * _rstd(xv) * g_ref[...]).astype(o_ref.dtype)

    return pl.pallas_call(
        body, name=name, grid=(T // tm,), in_specs=[_row_spec(tm, n, col), _vec_spec(n)],
        out_specs=_row_spec(tm, n), out_shape=jax.ShapeDtypeStruct((T, n), out_dtype),
        compiler_params=_cparams(("parallel",)),
    )(x, g)


def rms_bwd(dy, x, g, *, name, n=None, col=0, residual=None, out_dtype=F32):
    T = x.shape[0]
    n = x.shape[1] if n is None else n
    tm = _pick(T, ROW_TILE)

    def body(*refs):
        if residual is None:
            dy_ref, x_ref, g_ref, dx_ref, dg_ref = refs
        else:
            dy_ref, x_ref, g_ref, res_ref, dx_ref, dg_ref = refs
        dx, dgr = _rms_bwd_rows(dy_ref[...].astype(F32), x_ref[...], g_ref[...])
        if residual is not None:
            dx = dx + res_ref[...]
        dx_ref[...] = dx.astype(dx_ref.dtype)
        _acc_rows(dg_ref, pl.program_id(0), jnp.sum(dgr, axis=0, keepdims=True))

    in_specs = [_row_spec(tm, n), _row_spec(tm, n, col), _vec_spec(n)]
    args = [dy, x, g]
    if residual is not None:
        in_specs.append(_row_spec(tm, n))
        args.append(residual)
    return pl.pallas_call(
        body, name=name, grid=(T // tm,), in_specs=in_specs,
        out_specs=(_row_spec(tm, n), _vec_spec(n)),
        out_shape=(jax.ShapeDtypeStruct((T, n), out_dtype), jax.ShapeDtypeStruct((1, n), F32)),
        compiler_params=_cparams(("arbitrary",)),
    )(*args)


def mix_post_fwd(x, y, g_post, g_pre2, *, name):
    T, n = x.shape
    tm = _pick(T, ROW_TILE)

    def body(x_ref, y_ref, gp_ref, g2_ref, x1_ref, h2_ref):
        yv = y_ref[...]
        x1 = x_ref[...] + yv * _rstd(yv) * gp_ref[...]
        x1_ref[...] = x1
        h2_ref[...] = (x1 * _rstd(x1) * g2_ref[...]).astype(h2_ref.dtype)

    return pl.pallas_call(
        body, name=name, grid=(T // tm,),
        in_specs=[_row_spec(tm, n), _row_spec(tm, n), _vec_spec(n), _vec_spec(n)],
        out_specs=(_row_spec(tm, n), _row_spec(tm, n)),
        out_shape=(jax.ShapeDtypeStruct((T, n), F32), jax.ShapeDtypeStruct((T, n), BF16)),
        compiler_params=_cparams(("parallel",)),
    )(x, y, g_post, g_pre2)


def loss_head(x1, d, g, tgt, *, name):
    T, n = x1.shape
    tm = _pick(T, ROW_TILE)

    def body(x1_ref, d_ref, g_ref, t_ref, loss_ref, dx2_ref, dd_ref, dg_ref):
        i = pl.program_id(0)
        dv = d_ref[...]
        gv = g_ref[...]
        err = x1_ref[...] + dv * _rstd(dv) * gv - t_ref[...]
        part = 0.5 * jnp.sum(jnp.mean(err * err, axis=-1, keepdims=True), axis=0, keepdims=True)
        _acc_rows(loss_ref, i, jnp.broadcast_to(part, (1, LANES)))
        dx2 = err * (1.0 / n)
        dx2_ref[...] = dx2
        dd, dgr = _rms_bwd_rows(dx2, dv, gv)
        dd_ref[...] = dd.astype(dd_ref.dtype)
        _acc_rows(dg_ref, i, jnp.sum(dgr, axis=0, keepdims=True))

    return pl.pallas_call(
        body, name=name, grid=(T // tm,),
        in_specs=[_row_spec(tm, n), _row_spec(tm, n), _vec_spec(n), _row_spec(tm, n)],
        out_specs=(_vec_spec(LANES), _row_spec(tm, n), _row_spec(tm, n), _vec_spec(n)),
        out_shape=(jax.ShapeDtypeStruct((1, LANES), F32), jax.ShapeDtypeStruct((T, n), F32),
                   jax.ShapeDtypeStruct((T, n), BF16), jax.ShapeDtypeStruct((1, n), F32)),
        compiler_params=_cparams(("arbitrary",)),
    )(x1, d, g, tgt)


def mlp_pre_bwd(dx2, dh2, x1, g_pre2, y, g_post, *, name):
    T, n = x1.shape
    tm = _pick(T, ROW_TILE)

    def body(dx2_ref, dh2_ref, x1_ref, g2_ref, y_ref, gp_ref, dx1_ref, dy_ref, dg2_ref, dgp_ref):
        i = pl.program_id(0)
        d1, dg2 = _rms_bwd_rows(dh2_ref[...], x1_ref[...], g2_ref[...])
        dx1 = dx2_ref[...] + d1
        dx1_ref[...] = dx1
        dy, dgp = _rms_bwd_rows(dx1, y_ref[...], gp_ref[...])
        dy_ref[...] = dy.astype(dy_ref.dtype)
        _acc_rows(dg2_ref, i, jnp.sum(dg2, axis=0, keepdims=True))
        _acc_rows(dgp_ref, i, jnp.sum(dgp, axis=0, keepdims=True))

    rs = _row_spec(tm, n)
    return pl.pallas_call(
        body, name=name, grid=(T // tm,),
        in_specs=[rs, rs, rs, _vec_spec(n), rs, _vec_spec(n)],
        out_specs=(rs, rs, _vec_spec(n), _vec_spec(n)),
        out_shape=(jax.ShapeDtypeStruct((T, n), F32), jax.ShapeDtypeStruct((T, n), BF16),
                   jax.ShapeDtypeStruct((1, n), F32), jax.ShapeDtypeStruct((1, n), F32)),
        compiler_params=_cparams(("arbitrary",)),
    )(dx2, dh2, x1, g_pre2, y, g_post)


def _gate_specs(tm):
    gcol = PAD_OFF["gate"] // D_MODEL
    lspecs = [pl.BlockSpec((tm, D_MODEL), functools.partial(lambda i, c: (i, c), c=gcol + b)) for b in range(N_BRANCH)]
    bspecs = [pl.BlockSpec((1, D_MODEL), functools.partial(lambda i, c: (0, c), c=b)) for b in range(N_BRANCH)]
    pspecs = [_row_spec(tm, D_MODEL) for _ in range(N_BRANCH)]
    return lspecs, bspecs, pspecs


def gate_merge(proj, b_gate, ps, *, name):
    T = proj.shape[0]
    tm = _pick(T, ROW_TILE)
    lspecs, bspecs, pspecs = _gate_specs(tm)

    def body(*refs):
        l_refs, b_refs, p_refs, o_ref = refs[0:3], refs[3:6], refs[6:9], refs[9]
        tot = None
        for lr, br, pr in zip(l_refs, b_refs, p_refs):
            term = jax.nn.sigmoid(lr[...] + br[...]) * pr[...]
            tot = term if tot is None else tot + term
        o_ref[...] = tot.astype(o_ref.dtype)

    return pl.pallas_call(
        body, name=name, grid=(T // tm,), in_specs=lspecs + bspecs + pspecs,
        out_specs=_row_spec(tm, D_MODEL), out_shape=jax.ShapeDtypeStruct((T, D_MODEL), BF16),
        compiler_params=_cparams(("parallel",)),
    )(proj, proj, proj, b_gate, b_gate, b_gate, *ps)


def gate_bwd(dmerged, proj, b_gate, ps, *, name):
    T = proj.shape[0]
    tm = _pick(T, ROW_TILE)
    lspecs, bspecs, pspecs = _gate_specs(tm)

    def body(*refs):
        dm_ref = refs[0]
        l_refs, b_refs, p_refs = refs[1:4], refs[4:7], refs[7:10]
        dl_ref, dp_refs, db_ref = refs[10], refs[11:14], refs[14]
        i = pl.program_id(0)
        dm = dm_ref[...]
        for b, (lr, br, pr, dpr) in enumerate(zip(l_refs, b_refs, p_refs, dp_refs)):
            gt = jax.nn.sigmoid(lr[...] + br[...])
            dpr[...] = (dm * gt).astype(dpr.dtype)
            dl = dm * pr[...] * gt * (1.0 - gt)
            dl_ref[:, b * D_MODEL:(b + 1) * D_MODEL] = dl.astype(dl_ref.dtype)
            part = jnp.sum(dl, axis=0, keepdims=True)

            @pl.when(i == 0)
            def _():
                db_ref[:, b * D_MODEL:(b + 1) * D_MODEL] = part

            @pl.when(i > 0)
            def _():
                db_ref[:, b * D_MODEL:(b + 1) * D_MODEL] += part

    return pl.pallas_call(
        body, name=name, grid=(T // tm,), in_specs=[_row_spec(tm, D_MODEL)] + lspecs + bspecs + pspecs,
        out_specs=(_row_spec(tm, GATE_W), *[_row_spec(tm, D_MODEL) for _ in range(N_BRANCH)], _vec_spec(GATE_W)),
        out_shape=(jax.ShapeDtypeStruct((T, GATE_W), BF16),
                   *[jax.ShapeDtypeStruct((T, D_MODEL), BF16) for _ in range(N_BRANCH)],
                   jax.ShapeDtypeStruct((1, GATE_W), F32)),
        compiler_params=_cparams(("arbitrary",)),
    )(dmerged, proj, proj, proj, b_gate, b_gate, b_gate, *ps)


def _rope_tables(pos_ref, invf_ref):
    ang = pos_ref[...] * invf_ref[...]
    lane = lax.broadcasted_iota(jnp.int32, (1, LANES), 1)
    s = jnp.sin(ang)
    split = MLA_NOPE + MLA_ROPE // 2
    return jnp.cos(ang), jnp.where(lane >= split, s, 0.0), jnp.where(lane < split, s, 0.0)


def _rotate(xh, c, s_hi, s_lo, sign):
    half = MLA_ROPE // 2
    up = pltpu.roll(xh, half, 1)
    down = pltpu.roll(xh, LANES - half, 1)
    return xh * c + sign * (up * s_hi - down * s_lo)


def rope_fwd(qpre, kpre, proj, pos, invf, *, name):
    T = qpre.shape[0]
    tm = _pick(T, ROW_TILE)
    kr_col = PAD_OFF["kr"] // KR_PAD

    def body(q_ref, k_ref, kr_ref, pos_ref, invf_ref, qo_ref, ko_ref):
        c, s_hi, s_lo = _rope_tables(pos_ref, invf_ref)
        kr = _rotate(pltpu.roll(kr_ref[...], MLA_NOPE, 1), c, s_hi, s_lo, 1.0)
        for h in range(MLA_HEADS):
            sl = slice(h * HEAD_PAD, (h + 1) * HEAD_PAD)
            qo_ref[:, sl] = _rotate(q_ref[:, sl], c, s_hi, s_lo, 1.0).astype(qo_ref.dtype)
            ko_ref[:, sl] = (k_ref[:, sl] + kr).astype(ko_ref.dtype)

    rs = _row_spec(tm, MLA_PAD_W)
    return pl.pallas_call(
        body, name=name, grid=(T // tm,),
        in_specs=[rs, rs, _row_spec(tm, KR_PAD, kr_col), _row_spec(tm, 1), _vec_spec(LANES)],
        out_specs=(rs, rs),
        out_shape=(jax.ShapeDtypeStruct((T, MLA_PAD_W), BF16), jax.ShapeDtypeStruct((T, MLA_PAD_W), BF16)),
        compiler_params=_cparams(("parallel",)),
    )(qpre, kpre, proj, pos, invf)


def rope_bwd(dq_pad, dk_pad, pos, invf, *, name):
    T = dq_pad.shape[0]
    tm = _pick(T, ROW_TILE)

    def body(dq_ref, dk_ref, pos_ref, invf_ref, dqo_ref, dkr_ref):
        c, s_hi, s_lo = _rope_tables(pos_ref, invf_ref)
        tot = None
        for h in range(MLA_HEADS):
            sl = slice(h * HEAD_PAD, (h + 1) * HEAD_PAD)
            dqo_ref[:, sl] = _rotate(dq_ref[:, sl], c, s_hi, s_lo, -1.0).astype(dqo_ref.dtype)
            tot = dk_ref[:, sl] if tot is None else tot + dk_ref[:, sl]
        dkr = pltpu.roll(_rotate(tot, c, s_hi, s_lo, -1.0), LANES - MLA_NOPE, 1)
        lane = lax.broadcasted_iota(jnp.int32, (1, LANES), 1)
        dkr_ref[...] = jnp.where(lane < MLA_ROPE, dkr, 0.0).astype(dkr_ref.dtype)

    rs = _row_spec(tm, MLA_PAD_W)
    return pl.pallas_call(
        body, name=name, grid=(T // tm,),
        in_specs=[rs, rs, _row_spec(tm, 1), _vec_spec(LANES)],
        out_specs=(rs, _row_spec(tm, KR_PAD)),
        out_shape=(jax.ShapeDtypeStruct((T, MLA_PAD_W), BF16), jax.ShapeDtypeStruct((T, KR_PAD), BF16)),
        compiler_params=_cparams(("parallel",)),
    )(dq_pad, dk_pad, pos, invf)


ATT_TILE = 256
NEG = -1e30


def _split_bf16(v):
    hi = v.astype(BF16)
    return hi, (v - hi.astype(F32)).astype(BF16)


def _dot(a, b, ca, cb):
    return lax.dot_general(a, b, (((ca,), (cb,)), ((), ())), preferred_element_type=F32)


def _tri(tk, kind):
    r = lax.broadcasted_iota(jnp.int32, (tk, tk), 0)
    c = lax.broadcasted_iota(jnp.int32, (tk, tk), 1)
    m = {"gt": r > c, "le": r <= c, "lt": r < c}[kind]
    return jnp.where(m, 1.0, 0.0).astype(BF16)


def _cum(v, tri):
    hi, lo = _split_bf16(v)
    return _dot(hi, tri, 1, 0) + _dot(lo, tri, 1, 0)


def _lane_masks(hp, w):
    lane = lax.broadcasted_iota(jnp.int32, (1, LANES), 1)
    return [(lane >= e * w) & (lane < (e + 1) * w) for e in range(hp)]


def _att_dims(mode, hp, qk_w, v_w):
    assert mode in ("softmax", "sb")
    assert (hp, qk_w, v_w) in ((2, 128, 64), (2, 64, 64), (1, 128, 128))
    qw = hp * LANES if qk_w == LANES else LANES
    return qw


def attn_fwd(q, k, v, *, name, mode, causal, hp, qk_w, v_w, scale, nh, qc, kc, vc):
    qw = _att_dims(mode, hp, qk_w, v_w)
    B, S = q.shape[0], q.shape[1]
    Sk = k.shape[1]
    tq = _pick(S, ATT_TILE)
    tk = _pick(Sk, ATT_TILE)
    if causal:
        assert tq == tk and S == Sk
    nq, nkv = S // tq, Sk // tk

    def body(q_ref, k_ref, v_ref, o_ref, st_ref):
        i = pl.program_id(2)
        qv = q_ref[0].astype(BF16)
        masks_qk = _lane_masks(hp, qk_w) if qk_w < LANES else None
        masks_v = _lane_masks(hp, v_w) if v_w < LANES else None
        if qk_w == LANES:
            qs = [qv[:, e * LANES:(e + 1) * LANES] for e in range(hp)]
        else:
            qs = [jnp.where(masks_qk[e], qv, jnp.zeros_like(qv)) for e in range(hp)]
        rows = lax.broadcasted_iota(jnp.int32, (tq, tk), 0)
        cols = lax.broadcasted_iota(jnp.int32, (tq, tk), 1)
        tri = _tri(tk, "gt") if mode == "sb" else None

        def kv_tile(j):
            off = pl.multiple_of(j * tk, tk)
            kj = k_ref[0, pl.ds(off, tk), :].astype(BF16)
            vj = v_ref[0, pl.ds(off, tk), :].astype(BF16)
            ks = [kj[:, e * LANES:(e + 1) * LANES] for e in range(hp)] if qk_w == LANES else [kj] * hp
            return ks, vj

        def merge(vals):
            if hp == 1:
                return jnp.broadcast_to(vals[0], (tq, LANES))
            return jnp.where(masks_v[0], vals[0], vals[1])

        if mode == "softmax":
            def block(j, carry, diag):
                ms, ls, acc = carry
                ks, vj = kv_tile(j)
                new_m, new_l, alphas, pvs = [], [], [], []
                for e in range(hp):
                    s = _dot(qs[e], ks[e], 1, 1) * scale
                    if diag:
                        s = jnp.where(rows >= cols, s, NEG)
                    m_new = jnp.maximum(ms[e], jnp.max(s, axis=1, keepdims=True))
                    alpha = jnp.exp(ms[e] - m_new)
                    p = jnp.exp(s - m_new)
                    new_l.append(alpha * ls[e] + jnp.sum(p, axis=1, keepdims=True))
                    new_m.append(m_new)
                    alphas.append(alpha)
                    pvs.append(_dot(p.astype(BF16), vj, 1, 0))
                acc = acc * merge(alphas) + merge(pvs)
                return tuple(new_m), tuple(new_l), acc

            init = (tuple(jnp.full((tq, 1), NEG, F32) for _ in range(hp)),
                    tuple(jnp.zeros((tq, 1), F32) for _ in range(hp)),
                    jnp.zeros((tq, LANES), F32))
            if causal:
                carry = lax.fori_loop(0, i, lambda j, c: block(j, c, False), init)
                ms, ls, acc = block(i, carry, True)
            else:
                ms, ls, acc = lax.fori_loop(0, nkv, lambda j, c: block(j, c, False), init)
            o_ref[0] = acc / merge(list(ls))
            st_ref[0, 0] = merge([m + jnp.log(l) for m, l in zip(ms, ls)])
        else:
            def block(j, carry, diag):
                cs_, acc = carry
                ks, vj = kv_tile(j)
                new_c, pvs = [], []
                for e in range(hp):
                    z = _dot(qs[e], ks[e], 1, 1) * scale
                    lg = jnp.log(1.0 + jnp.exp(-jnp.abs(z)))
                    lz = jnp.minimum(z, 0.0) - lg
                    lk = -jnp.maximum(z, 0.0) - lg
                    if diag:
                        lk = jnp.where(rows > cols, lk, 0.0)
                    a = jnp.exp(lz + _cum(lk, tri) + cs_[e])
                    if diag:
                        a = jnp.where(rows > cols, a, 0.0)
                    pvs.append(_dot(a.astype(BF16), vj, 1, 0))
                    new_c.append(cs_[e] + jnp.sum(lk, axis=1, keepdims=True))
                return tuple(new_c), acc + merge(pvs)

            init = (tuple(jnp.zeros((tq, 1), F32) for _ in range(hp)), jnp.zeros((tq, LANES), F32))
            carry = block(i, init, True)
            cs_, acc = lax.fori_loop(0, i, lambda jj, c: block(i - 1 - jj, c, False), carry)
            o_ref[0] = acc
            st_ref[0, 0] = merge(list(cs_))

    return pl.pallas_call(
        body, name=name, grid=(B, nh, nq),
        in_specs=[pl.BlockSpec((1, tq, qw), lambda b, h, i: (b, i, qc + h)),
                  pl.BlockSpec((1, Sk, qw), lambda b, h, i: (b, 0, kc + h)),
                  pl.BlockSpec((1, Sk, LANES), lambda b, h, i: (b, 0, vc + h))],
        out_specs=(pl.BlockSpec((1, tq, LANES), lambda b, h, i: (b, i, h)),
                   pl.BlockSpec((1, 1, tq, LANES), lambda b, h, i: (b, h, i, 0))),
        out_shape=(jax.ShapeDtypeStruct((B, S, nh * LANES), F32), jax.ShapeDtypeStruct((B, nh, S, LANES), F32)),
        compiler_params=_cparams(("parallel", "parallel", "arbitrary")),
    )(q, k, v)


def attn_bwd(q, k, v, o, st, do, *, name, mode, causal, hp, qk_w, v_w, scale, nh, qc, kc, vc):
    qw = _att_dims(mode, hp, qk_w, v_w)
    B, S = q.shape[0], q.shape[1]
    Sk = k.shape[1]
    tq = _pick(S, ATT_TILE)
    tk = _pick(Sk, ATT_TILE)
    if causal:
        assert tq == tk and S == Sk
    nq, nkv = S // tq, Sk // tk

    def body(q_ref, k_ref, v_ref, o_ref, st_ref, do_ref, dq_ref, dk_ref, dv_ref):
        i = pl.program_id(2)

        @pl.when(i == 0)
        def _():
            dk_ref[...] = jnp.zeros_like(dk_ref)
            dv_ref[...] = jnp.zeros_like(dv_ref)

        qv = q_ref[0].astype(BF16)
        dov = do_ref[0]
        stv = st_ref[0, 0]
        masks_qk = _lane_masks(hp, qk_w) if qk_w < LANES else None
        masks_v = _lane_masks(hp, v_w) if v_w < LANES else None
        if qk_w == LANES:
            qs = [qv[:, e * LANES:(e + 1) * LANES] for e in range(hp)]
        else:
            qs = [jnp.where(masks_qk[e], qv, jnp.zeros_like(qv)) for e in range(hp)]
        if hp == 1:
            dos = [dov.astype(BF16)]
            stats = [stv[:, 0:1]]
        else:
            dos = [jnp.where(masks_v[e], dov, 0.0).astype(BF16) for e in range(hp)]
            stats = [stv[:, e * v_w:e * v_w + 1] for e in range(hp)]
        if mode == "softmax":
            prod = dov * o_ref[0]
            if hp == 1:
                dsum = [jnp.sum(prod, axis=1, keepdims=True)]
            else:
                dsum = [jnp.sum(jnp.where(masks_v[e], prod, 0.0), axis=1, keepdims=True) for e in range(hp)]
        rows = lax.broadcasted_iota(jnp.int32, (tq, tk), 0)
        cols = lax.broadcasted_iota(jnp.int32, (tq, tk), 1)
        if mode == "sb":
            tri_le, tri_lt = _tri(tk, "le"), _tri(tk, "lt")

        def kv_tile(j):
            off = pl.multiple_of(j * tk, tk)
            kj = k_ref[0, pl.ds(off, tk), :].astype(BF16)
            vj = v_ref[0, pl.ds(off, tk), :].astype(BF16)
            ks = [kj[:, e * LANES:(e + 1) * LANES] for e in range(hp)] if qk_w == LANES else [kj] * hp
            return off, ks, vj

        def scatter(off, dz_list, p_list):
            dvj = None
            for e in range(hp):
                t = _dot(p_list[e], dos[e], 0, 0)
                dvj = t if dvj is None else dvj + t
            dv_ref[0, pl.ds(off, tk), :] += dvj
            if qk_w == LANES:
                for e in range(hp):
                    dk_ref[0, pl.ds(off, tk), e * LANES:(e + 1) * LANES] += _dot(dz_list[e], qs[e], 0, 0)
            else:
                dkj = None
                for e in range(hp):
                    t = _dot(dz_list[e], qs[e], 0, 0)
                    dkj = t if dkj is None else dkj + t
                dk_ref[0, pl.ds(off, tk), :] += dkj

        def dq_add(dqs, dz_list, ks):
            out = []
            for e in range(hp):
                out.append(dqs[e] + _dot(dz_list[e], ks[e], 1, 0))
            return tuple(out)

        dq0 = tuple(jnp.zeros((tq, LANES), F32) for _ in range(hp))

        if mode == "softmax":
            def block(j, dqs, diag):
                off, ks, vj = kv_tile(j)
                dzs, ps = [], []
                for e in range(hp):
                    s = _dot(qs[e], ks[e], 1, 1) * scale
                    p = jnp.exp(s - stats[e])
                    if diag:
                        p = jnp.where(rows >= cols, p, 0.0)
                    dp = _dot(dos[e], vj, 1, 1)
                    dzs.append((p * (dp - dsum[e]) * scale).astype(BF16))
                    ps.append(p.astype(BF16))
                scatter(off, dzs, ps)
                return dq_add(dqs, dzs, ks)

            if causal:
                dqs = lax.fori_loop(0, i, lambda j, c: block(j, c, False), dq0)
                dqs = block(i, dqs, True)
            else:
                dqs = lax.fori_loop(0, nkv, lambda j, c: block(j, c, False), dq0)
        else:
            def block(j, carry, diag):
                dqs, cps, cgs = carry
                off, ks, vj = kv_tile(j)
                dzs, ps, new_p, new_g = [], [], [], []
                for e in range(hp):
                    z = _dot(qs[e], ks[e], 1, 1) * scale
                    ez = jnp.exp(-jnp.abs(z))
                    lg = jnp.log(1.0 + ez)
                    lz = jnp.minimum(z, 0.0) - lg
                    lk = -jnp.maximum(z, 0.0) - lg
                    if diag:
                        lk = jnp.where(rows > cols, lk, 0.0)
                    keep_after = stats[e] - cps[e] - _cum(lk, tri_le)
                    a = jnp.exp(lz + keep_after)
                    if diag:
                        a = jnp.where(rows > cols, a, 0.0)
                    g = _dot(dos[e], vj, 1, 1) * a
                    gsum = cgs[e] + _cum(g, tri_lt)
                    inv = 1.0 / (1.0 + ez)
                    sig = jnp.where(z >= 0.0, inv, ez * inv)
                    dz = (g * (1.0 - sig) - gsum * sig) * scale
                    if diag:
                        dz = jnp.where(rows > cols, dz, 0.0)
                    dzs.append(dz.astype(BF16))
                    ps.append(a.astype(BF16))
                    new_p.append(cps[e] + jnp.sum(lk, axis=1, keepdims=True))
                    new_g.append(cgs[e] + jnp.sum(g, axis=1, keepdims=True))
                scatter(off, dzs, ps)
                return dq_add(dqs, dzs, ks), tuple(new_p), tuple(new_g)

            zc = tuple(jnp.zeros((tq, 1), F32) for _ in range(hp))
            carry = lax.fori_loop(0, i, lambda j, c: block(j, c, False), (dq0, zc, zc))
            dqs, _, _ = block(i, carry, True)

        if qk_w == LANES:
            for e in range(hp):
                dq_ref[0, :, e * LANES:(e + 1) * LANES] = dqs[e]
        else:
            dq_ref[0] = jnp.where(masks_qk[0], dqs[0], dqs[1])

    return pl.pallas_call(
        body, name=name, grid=(B, nh, nq),
        in_specs=[pl.BlockSpec((1, tq, qw), lambda b, h, i: (b, i, qc + h)),
                  pl.BlockSpec((1, Sk, qw), lambda b, h, i: (b, 0, kc + h)),
                  pl.BlockSpec((1, Sk, LANES), lambda b, h, i: (b, 0, vc + h)),
                  pl.BlockSpec((1, tq, LANES), lambda b, h, i: (b, i, h)),
                  pl.BlockSpec((1, 1, tq, LANES), lambda b, h, i: (b, h, i, 0)),
                  pl.BlockSpec((1, tq, LANES), lambda b, h, i: (b, i, h))],
        out_specs=(pl.BlockSpec((1, tq, qw), lambda b, h, i: (b, i, h)),
                   pl.BlockSpec((1, Sk, qw), lambda b, h, i: (b, 0, h)),
                   pl.BlockSpec((1, Sk, LANES), lambda b, h, i: (b, 0, h))),
        out_shape=(jax.ShapeDtypeStruct((B, S, nh * qw), F32), jax.ShapeDtypeStruct((B, Sk, nh * qw), F32),
                   jax.ShapeDtypeStruct((B, Sk, nh * LANES), F32)),
        compiler_params=_cparams(("parallel", "parallel", "arbitrary")),
    )(q, k, v, o, st, do)


def _peer(x, y, c, k):
    return (x ^ ((k >> 2) & 1), y ^ ((k >> 1) & 1), c ^ (k & 1))


def _pos(p):
    return 4 * p[0] + 2 * p[1] + p[2]


N_CHIP = 4
MESH_ID = pl.DeviceIdType.MESH
_ANY = pl.BlockSpec(memory_space=pl.ANY)


def _me():
    return lax.axis_index("x"), lax.axis_index("y"), lax.axis_index("c")


def _other_chips(x, y):
    return [(1 - x, y), (x, 1 - y), (1 - x, 1 - y)]


def gather_two_level(buf, *, name):
    R = buf.shape[0]

    def body(x_ref, out_ref, send_sems, recv_sems, local_sem):
        x, y, c = _me()
        me, sibling = (x, y, c), (x, y, 1 - c)
        chips = _other_chips(x, y)

        def copy(k, block, to, src=None):
            dst = out_ref.at[_pos(block)]
            return pltpu.make_async_remote_copy(src_ref=dst if src is None else src, dst_ref=dst, send_sem=send_sems.at[k],
                                                recv_sem=recv_sems.at[k], device_id=to, device_id_type=MESH_ID)

        mine = pltpu.make_async_copy(x_ref, out_ref.at[_pos(me)], local_sem)
        mine.start()
        first = [copy(0, me, sibling, src=x_ref)]
        first += [copy(1 + j, me, (*chip, c), src=x_ref) for j, chip in enumerate(chips)]
        for cp in first:
            cp.start()
        passed = [copy(4 + j, (*chip, c), sibling) for j, chip in enumerate(chips)]
        for j, chip in enumerate(chips):
            copy(1 + j, (*chip, c), me).wait_recv()
            passed[j].start()
        copy(0, sibling, me).wait_recv()
        for j, chip in enumerate(chips):
            copy(4 + j, (*chip, 1 - c), me).wait_recv()
        for cp in first + passed:
            cp.wait_send()
        mine.wait()

    return pl.pallas_call(
        body, name=name, in_specs=[_ANY], out_specs=_ANY,
        out_shape=jax.ShapeDtypeStruct((N_DEV, R, LANES), buf.dtype),
        scratch_shapes=[pltpu.SemaphoreType.DMA((7,)), pltpu.SemaphoreType.DMA((7,)), pltpu.SemaphoreType.DMA],
        compiler_params=pltpu.CompilerParams(has_side_effects=True),
    )(buf)


def sibling_exchange(send, *, name):
    _, _, R, _ = send.shape

    def body(s_ref, out_ref, send_sem, recv_sem):
        x, y, c = _me()
        cp = pltpu.make_async_remote_copy(src_ref=s_ref.at[1 - c], dst_ref=out_ref, send_sem=send_sem, recv_sem=recv_sem,
                                          device_id=(x, y, 1 - c), device_id_type=MESH_ID)
        cp.start()
        cp.wait()

    return pl.pallas_call(
        body, name=name, in_specs=[_ANY], out_specs=_ANY,
        out_shape=jax.ShapeDtypeStruct((N_CHIP, R, LANES), send.dtype),
        scratch_shapes=[pltpu.SemaphoreType.DMA, pltpu.SemaphoreType.DMA],
        compiler_params=pltpu.CompilerParams(has_side_effects=True),
    )(send)


def chip_sum(send, got, core, *, name):
    _, _, R, _ = send.shape
    tr = min(R, PACK_TILE)

    def body(core_ref, s_ref, g_ref, o_ref):
        o_ref[...] = (s_ref[0] + g_ref[...]).astype(o_ref.dtype)

    blk = pl.BlockSpec((N_CHIP, tr, LANES), lambda i, core_ref: (0, i, 0))
    return pl.pallas_call(
        body, name=name,
        grid_spec=pltpu.PrefetchScalarGridSpec(
            num_scalar_prefetch=1, grid=(R // tr,),
            in_specs=[pl.BlockSpec((1, N_CHIP, tr, LANES), lambda i, core_ref: (core_ref[0], 0, i, 0)), blk],
            out_specs=blk),
        out_shape=jax.ShapeDtypeStruct((N_CHIP, R, LANES), BF16),
        compiler_params=_cparams(("parallel",)),
    )(core, send, got)


def chip_exchange(sums, *, name):
    _, R, _ = sums.shape

    def body(s_ref, out_ref, send_sems, recv_sems, local_sem):
        x, y, c = _me()
        mine = 2 * x + y
        chips = _other_chips(x, y)
        local = pltpu.make_async_copy(s_ref.at[mine], out_ref.at[mine], local_sem)
        local.start()
        sends = []
        for j, (px, py) in enumerate(chips):
            cp = pltpu.make_async_remote_copy(src_ref=s_ref.at[2 * px + py], dst_ref=out_ref.at[mine], send_sem=send_sems.at[j],
                                              recv_sem=recv_sems.at[j], device_id=(px, py, c), device_id_type=MESH_ID)
            cp.start()
            sends.append(cp)
        for j, (px, py) in enumerate(chips):
            pltpu.make_async_remote_copy(src_ref=s_ref.at[mine], dst_ref=out_ref.at[2 * px + py], send_sem=send_sems.at[j],
                                         recv_sem=recv_sems.at[j], device_id=(px, py, c), device_id_type=MESH_ID).wait_recv()
        for cp in sends:
            cp.wait_send()
        local.wait()

    return pl.pallas_call(
        body, name=name, in_specs=[_ANY], out_specs=_ANY,
        out_shape=jax.ShapeDtypeStruct(sums.shape, sums.dtype),
        scratch_shapes=[pltpu.SemaphoreType.DMA((3,)), pltpu.SemaphoreType.DMA((3,)), pltpu.SemaphoreType.DMA],
        compiler_params=pltpu.CompilerParams(has_side_effects=True),
    )(sums)


def sum_adamw(parts, w, m, v, *, name):
    R = w.shape[0]
    n_slots = parts.shape[0]
    tr = min(R, PACK_TILE)
    assert R % tr == 0
    c1 = 1.0 - ADAM_B1 ** ADAM_STEP
    c2 = 1.0 - ADAM_B2 ** ADAM_STEP

    def body(p_ref, w_ref, m_ref, v_ref, g_ref, d_ref, mo_ref, vo_ref):
        g = p_ref[0].astype(F32)
        for s in range(1, n_slots):
            g = g + p_ref[s].astype(F32)
        mn = ADAM_B1 * m_ref[...] + (1.0 - ADAM_B1) * g
        vn = ADAM_B2 * v_ref[...] + (1.0 - ADAM_B2) * jnp.square(g)
        g_ref[...] = g
        mo_ref[...] = mn
        vo_ref[...] = vn
        d_ref[...] = -ADAM_LR * ((mn / c1) / (jnp.sqrt(vn / c2) + ADAM_EPS) + ADAM_WD * w_ref[...])

    rs = pl.BlockSpec((tr, LANES), lambda i: (i, 0))
    sd = jax.ShapeDtypeStruct((R, LANES), F32)
    return pl.pallas_call(
        body, name=name, grid=(R // tr,),
        in_specs=[pl.BlockSpec((n_slots, tr, LANES), lambda i: (0, i, 0)), rs, rs, rs],
        out_specs=(rs, rs, rs, rs), out_shape=(sd, sd, sd, sd),
        compiler_params=_cparams(("parallel",)),
    )(parts, w, m, v)


WEIGHTS = ("ln_mix_pre", "w_in", "b_gate", "q_norm", "w_uq", "kv_norm", "w_uk", "w_uv", "mem_norm", "w_mem_kv",
           "w_branch_out", "w_out", "ln_mix_post", "ln_mlp_pre", "w_mlp_up", "w_mlp_down", "ln_mlp_post")
BIG = dict(w_in=((D_MODEL, IN_WIDTH), 1), w_uq=((Q_LORA, MLA_HEADS * (MLA_NOPE + MLA_ROPE)), 1),
           w_uk=((KV_LORA, MLA_HEADS * MLA_NOPE), 1), w_uv=((KV_LORA, MLA_HEADS * MLA_V), 1),
           w_mem_kv=((D_MODEL, 2 * QM_W), 0), w_branch_out=((N_BRANCH, BRANCH_W, D_MODEL), 2),
           w_out=((D_MODEL, D_MODEL), 0), w_mlp_up=((D_MODEL, D_FF), 1), w_mlp_down=((D_FF, D_MODEL), 0))
SMALL = tuple(n for n in WEIGHTS if n not in BIG)


def _shard_shape(name):
    full, ax = BIG[name]
    return tuple(s // N_DEV if a == ax else s for a, s in enumerate(full))


PACK_TILE = 512


def _pad_rows(a):
    r = a.shape[-2]
    to = PACK_TILE if r > PACK_TILE else 8
    pad = [(0, 0)] * a.ndim
    pad[-2] = (0, (-r) % to)
    return jnp.pad(a, pad)


def _pack_rows(arrs):
    return _pad_rows(jnp.concatenate([a.reshape(-1, LANES) for a in arrs], axis=0))


def _unpack_rows(packed, shapes, lead=()):
    out, r = [], 0
    for shp in shapes:
        n = math.prod(shp) // LANES
        out.append(packed[..., r:r + n, :].reshape(lead + tuple(shp)))
        r += n
    return out


def _slots_to_full(name, slots):
    full, ax = BIG[name]
    return jnp.moveaxis(slots, 0, ax).reshape(full)


def _full_to_slots(name, w):
    full, ax = BIG[name]
    split = full[:ax] + (N_DEV, full[ax] // N_DEV) + full[ax + 1:]
    return jnp.moveaxis(w.reshape(split), ax, 0)


def _split_in(w):
    offs = np.cumsum((0,) + IN_SIZES)
    names = ("cq", "ckv", "kr", "sb", "qm", "gate")
    return {n: w[:, offs[i]:offs[i + 1]] for i, n in enumerate(names)}


def _pad_in(w):
    p = _split_in(w)
    p["kr"] = jnp.pad(p["kr"], ((0, 0), (0, KR_PAD - MLA_ROPE)))
    p["zpad"] = jnp.zeros((w.shape[0], PAD_SIZES["zpad"]), w.dtype)
    return jnp.concatenate([p[n] for n in PAD_ORDER], axis=1)


def _unpad_in(wp):
    p = {n: wp[:, PAD_OFF[n]:PAD_OFF[n] + PAD_SIZES[n]] for n in PAD_ORDER}
    p["kr"] = p["kr"][:, :MLA_ROPE]
    return jnp.concatenate([p[n] for n in ("cq", "ckv", "kr", "sb", "qm", "gate")], axis=1)


def _pad_heads(w, width):
    r = w.shape[0]
    return jnp.pad(w.reshape(r, MLA_HEADS, width), ((0, 0), (0, 0), (0, HEAD_PAD - width))).reshape(r, MLA_PAD_W)


def _unpad_heads(wp, width):
    r = wp.shape[0]
    return wp.reshape(r, MLA_HEADS, HEAD_PAD)[:, :, :width].reshape(r, MLA_HEADS * width)


def _rope_inv_freq():
    half = MLA_ROPE // 2
    inv = 1.0 / (ROPE_THETA ** (jnp.arange(half, dtype=F32) * (2.0 / MLA_ROPE)))
    tab = jnp.zeros((LANES,), F32)
    tab = tab.at[MLA_NOPE:MLA_NOPE + half].set(inv).at[MLA_NOPE + half:MLA_NOPE + MLA_ROPE].set(inv)
    return tab.reshape(1, LANES)


def _local_step(x, mem, positions, tgt, sm, W):
    B, S, D = x.shape
    M = mem.shape[1]
    T = B * S
    x2 = x.reshape(T, D)
    mem2 = mem.reshape(B * M, D)
    pos = positions.reshape(T, 1).astype(F32)
    invf = _rope_inv_freq()
    w_in_pad = _pad_in(W["w_in"])
    w_uq_pad = _pad_heads(W["w_uq"], MLA_NOPE + MLA_ROPE)
    w_uk_pad = _pad_heads(W["w_uk"], MLA_NOPE)
    cq_col, ckv_col = PAD_OFF["cq"] // Q_LORA, PAD_OFF["ckv"] // KV_LORA
    sb_col, qm_col = PAD_OFF["sb"] // LANES, PAD_OFF["qm"] // LANES
    sb_blk = SB_HEADS * SB_DIM // LANES
    mla = dict(mode="softmax", causal=True, hp=2, qk_w=128, v_w=64, scale=(MLA_NOPE + MLA_ROPE) ** -0.5,
               nh=MLA_HEADS // 2, qc=0, kc=0, vc=0)
    sbk = dict(mode="sb", causal=True, hp=2, qk_w=64, v_w=64, scale=SB_DIM ** -0.5, nh=SB_HEADS // 2,
               qc=sb_col, kc=sb_col + sb_blk, vc=sb_col + 2 * sb_blk)
    mca = dict(mode="softmax", causal=False, hp=1, qk_w=128, v_w=128, scale=MEM_DIM ** -0.5, nh=MEM_HEADS,
               qc=qm_col, kc=0, vc=MEM_HEADS)

    h = rms_fwd(x2, sm["ln_mix_pre"], name="rms_mix_pre")
    proj = matmul(h, w_in_pad, name="mm_in")
    proj3 = proj.reshape(B, S, IN_PAD)
    cqn = rms_fwd(proj, sm["q_norm"], n=Q_LORA, col=cq_col, name="rms_q")
    ckvn = rms_fwd(proj, sm["kv_norm"], n=KV_LORA, col=ckv_col, name="rms_kv")
    qpre = matmul(cqn, w_uq_pad, name="mm_uq")
    kpre = matmul(ckvn, w_uk_pad, name="mm_uk")
    v_mla = matmul(ckvn, W["w_uv"], out_dtype=BF16, name="mm_uv").reshape(B, S, -1)
    q_pad, k_pad = rope_fwd(qpre, kpre, proj, pos, invf, name="rope_fwd")
    q_pad, k_pad = q_pad.reshape(B, S, -1), k_pad.reshape(B, S, -1)
    o_mla, st_mla = attn_fwd(q_pad, k_pad, v_mla, name="mla_fwd", **mla)
    o_sb, st_sb = attn_fwd(proj3, proj3, proj3, name="sb_fwd", **sbk)
    memh = rms_fwd(mem2, sm["mem_norm"], name="rms_mem")
    mkv = matmul(memh, W["w_mem_kv"], out_dtype=BF16, name="mm_memkv").reshape(B, M, -1)
    o_mem, st_mem = attn_fwd(proj3, mkv, mkv, name="mem_fwd", **mca)
    outs = [o.reshape(T, BRANCH_W) for o in (o_mla, o_sb, o_mem)]
    ps = [matmul(o, W["w_branch_out"][b], name=f"mm_bo{b}") for b, o in enumerate(outs)]
    merged = gate_merge(proj, sm["b_gate"], ps, name="gate_merge")
    y = matmul(merged, W["w_out"], name="mm_out")
    x1, h2 = mix_post_fwd(x2, y, sm["ln_mix_post"], sm["ln_mlp_pre"], name="mix_post")
    u = matmul(h2, W["w_mlp_up"], act="relu2", out_dtype=BF16, name="mm_up")
    d = matmul(u, W["w_mlp_down"], name="mm_down")
    loss_p, dx2, dd, dg_mlp_post = loss_head(x1, d, sm["ln_mlp_post"], tgt.reshape(T, D), name="loss_head")

    da = matmul(dd, W["w_mlp_down"], tb=True, act="relu2_bwd", act_in=u, out_dtype=BF16, name="mm_down_dx")
    g_down = matmul(u, dd, ta=True, name="mm_down_dw")
    dh2 = matmul(da, W["w_mlp_up"], tb=True, name="mm_up_dx")
    g_up = matmul(h2, da, ta=True, name="mm_up_dw")
    dx1, dy, dg_mlp_pre, dg_mix_post = mlp_pre_bwd(dx2, dh2, x1, sm["ln_mlp_pre"], y, sm["ln_mix_post"], name="mlp_pre_bwd")
    dmerged = matmul(dy, W["w_out"], tb=True, name="mm_out_dx")
    g_out = matmul(merged, dy, ta=True, name="mm_out_dw")
    dlog, dp0, dp1, dp2, db_gate = gate_bwd(dmerged, proj, sm["b_gate"], ps, name="gate_bwd")
    dps = (dp0, dp1, dp2)
    dos = [matmul(dps[b], W["w_branch_out"][b], tb=True, name=f"mm_bo{b}_dx").reshape(B, S, BRANCH_W) for b in range(N_BRANCH)]
    g_bo = jnp.stack([matmul(outs[b], dps[b], ta=True, name=f"mm_bo{b}_dw") for b in range(N_BRANCH)])
    dq_pad, dk_pad, dv_mla = attn_bwd(q_pad, k_pad, v_mla, o_mla, st_mla, dos[0], name="mla_bwd", **mla)
    dsq, dsk, dsv = attn_bwd(proj3, proj3, proj3, o_sb, st_sb, dos[1], name="sb_bwd", **sbk)
    dqm, dmk, dmv = attn_bwd(proj3, mkv, mkv, o_mem, st_mem, dos[2], name="mem_bwd", **mca)
    dmkv = jnp.concatenate([dmk, dmv], axis=-1).astype(BF16).reshape(B * M, -1)
    dmemh = matmul(dmkv, W["w_mem_kv"], tb=True, name="mm_memkv_dx")
    g_memkv = matmul(memh, dmkv, ta=True, name="mm_memkv_dw")
    _, dg_mem_norm = rms_bwd(dmemh, mem2, sm["mem_norm"], name="rms_mem_bwd")
    dq_pad, dk_pad, dv_mla = dq_pad.reshape(T, -1), dk_pad.reshape(T, -1), dv_mla.reshape(T, -1)
    dqpre, dkr = rope_bwd(dq_pad, dk_pad, pos, invf, name="rope_bwd")
    dcqn = matmul(dqpre, w_uq_pad, tb=True, name="mm_uq_dx")
    g_uq = _unpad_heads(matmul(cqn, dqpre, ta=True, name="mm_uq_dw"), MLA_NOPE + MLA_ROPE)
    dckvn = matmul(dk_pad, w_uk_pad, tb=True, name="mm_uk_dx")
    dckvn = matmul(dv_mla, W["w_uv"], tb=True, acc=dckvn, name="mm_uv_dx")
    g_uk = _unpad_heads(matmul(ckvn, dk_pad, ta=True, name="mm_uk_dw"), MLA_NOPE)
    g_uv = matmul(ckvn, dv_mla, ta=True, name="mm_uv_dw")
    dcq, dg_q_norm = rms_bwd(dcqn, proj, sm["q_norm"], n=Q_LORA, col=cq_col, out_dtype=BF16, name="rms_q_bwd")
    dckv, dg_kv_norm = rms_bwd(dckvn, proj, sm["kv_norm"], n=KV_LORA, col=ckv_col, out_dtype=BF16, name="rms_kv_bwd")
    pieces = dict(gate=dlog, sb=jnp.concatenate([dsq, dsk, dsv], axis=-1).reshape(T, -1), qm=dqm.reshape(T, -1),
                  ckv=dckv, cq=dcq, kr=dkr, zpad=jnp.zeros((T, PAD_SIZES["zpad"]), BF16))
    dproj = jnp.concatenate([pieces[n].astype(BF16) for n in PAD_ORDER], axis=1)
    dh = matmul(dproj, w_in_pad, tb=True, name="mm_in_dx")
    g_in = _unpad_in(matmul(h, dproj, ta=True, name="mm_in_dw"))
    dx, dg_mix_pre = rms_bwd(dh, x2, sm["ln_mix_pre"], residual=dx1, name="rms_mix_pre_bwd")

    big = dict(w_in=g_in, w_uq=g_uq, w_uk=g_uk, w_uv=g_uv, w_mem_kv=g_memkv, w_branch_out=g_bo, w_out=g_out,
               w_mlp_up=g_up, w_mlp_down=g_down)
    small = dict(ln_mix_pre=dg_mix_pre, b_gate=db_gate, q_norm=dg_q_norm, kv_norm=dg_kv_norm, mem_norm=dg_mem_norm,
                 ln_mix_post=dg_mix_post, ln_mlp_pre=dg_mlp_pre, ln_mlp_post=dg_mlp_post)
    return loss_p[0, 0], dx.reshape(B, S, D), big, small


def kernel(x, mem, positions, ln_mix_pre, w_in, b_gate, q_norm, w_uq, kv_norm, w_uk, w_uv, mem_norm, w_mem_kv, w_branch_out, w_out, ln_mix_post, ln_mlp_pre, w_mlp_up, w_mlp_down, ln_mlp_post, loss_target, m_ln_mix_pre, m_w_in, m_b_gate, m_q_norm, m_w_uq, m_kv_norm, m_w_uk, m_w_uv, m_mem_norm, m_w_mem_kv, m_w_branch_out, m_w_out, m_ln_mix_post, m_ln_mlp_pre, m_w_mlp_up, m_w_mlp_down, m_ln_mlp_post, v_ln_mix_pre, v_w_in, v_b_gate, v_q_norm, v_w_uq, v_kv_norm, v_w_uk, v_w_uv, v_mem_norm, v_w_mem_kv, v_w_branch_out, v_w_out, v_ln_mix_post, v_ln_mlp_pre, v_w_mlp_up, v_w_mlp_down, v_ln_mlp_post):
    given = dict(locals())
    w = {n: given[n][0] for n in WEIGHTS}
    m = {n: given["m_" + n][0] for n in WEIGHTS}
    v = {n: given["v_" + n][0] for n in WEIGHTS}
    big_names = tuple(BIG)
    shard_shapes = [_shard_shape(n) for n in big_names]

    gathered = gather_two_level(_pack_rows([w[n].astype(BF16) for n in big_names]), name="gather_weights")
    slots = _unpack_rows(gathered, shard_shapes, lead=(N_DEV,))
    W = {n: _slots_to_full(n, s) for n, s in zip(big_names, slots)}
    sm = {n: w[n].reshape(1, -1) for n in SMALL}

    loss_part, grad_x, g_big, g_small = _local_step(x, mem, positions, loss_target, sm, W)
    loss = lax.psum(loss_part, ("x", "y", "c"))

    def by_core_and_chip(n):
        s = _full_to_slots(n, g_big[n]).reshape(N_CHIP, 2, -1, LANES)
        return jnp.swapaxes(s, 0, 1)

    send = _pad_rows(jnp.concatenate([by_core_and_chip(n) for n in big_names], axis=2))
    got = sibling_exchange(send, name="grads_sibling")
    sums = chip_sum(send, got, lax.axis_index("c").astype(jnp.int32).reshape(1), name="grads_chip_sum")
    recv = chip_exchange(sums, name="grads_chips")
    res_big = sum_adamw(recv, *[_pack_rows([t[n] for n in big_names]) for t in (w, m, v)], name="adamw_sharded")
    res_big = [_unpack_rows(r, shard_shapes) for r in res_big]

    small_shapes = [w[n].shape for n in SMALL]
    parts = gather_two_level(_pack_rows([g_small[n] for n in SMALL]), name="gather_small_grads")
    res_small = sum_adamw(parts, *[_pack_rows([t[n] for n in SMALL]) for t in (w, m, v)], name="adamw_replicated")
    res_small = [_unpack_rows(r, small_shapes) for r in res_small]

    out = [loss, grad_x]
    for k in range(4):
        by_name = dict(zip(big_names, res_big[k]))
        by_name.update(zip(SMALL, res_small[k]))
        out += [by_name[n][None] for n in WEIGHTS]
    return tuple(out)
```

```python
import functools
import math

import numpy as np
import jax
import jax.numpy as jnp
from jax import lax
from jax.experimental import pallas as pl
from jax.experimental.pallas import tpu as pltpu

F32 = jnp.float32
BF16 = jnp.bfloat16

D_MODEL = 1024
MEM_HEADS, MEM_DIM = 4, 128
MLA_HEADS, MLA_NOPE, MLA_ROPE, MLA_V = 8, 64, 32, 64
Q_LORA, KV_LORA = 384, 256
ROPE_THETA = 10000.0
SB_HEADS, SB_DIM = 8, 64
D_FF = 4 * D_MODEL
N_BRANCH, BRANCH_W = 3, 512
EPS = 1e-6
SB_W = 3 * SB_HEADS * SB_DIM
QM_W = MEM_HEADS * MEM_DIM
GATE_W = N_BRANCH * D_MODEL
IN_SIZES = (Q_LORA, KV_LORA, MLA_ROPE, SB_W, QM_W, GATE_W)
IN_WIDTH = sum(IN_SIZES)
ADAM_LR, ADAM_B1, ADAM_B2, ADAM_EPS, ADAM_WD, ADAM_STEP = 0.001, 0.9, 0.999, 1e-08, 0.01, 10

N_DEV = 8
LANES = 128
KR_PAD = LANES
PAD_ORDER = ("gate", "cq", "kr", "ckv", "sb", "qm", "zpad")
MM_TILE = 1024
PAD_SIZES = dict(gate=GATE_W, cq=Q_LORA, kr=KR_PAD, ckv=KV_LORA, sb=SB_W, qm=QM_W)
PAD_SIZES["zpad"] = (-sum(PAD_SIZES.values())) % MM_TILE
PAD_OFF = {}
_o = 0
for _n in PAD_ORDER:
    PAD_OFF[_n] = _o
    _o += PAD_SIZES[_n]
IN_PAD = _o
HEAD_PAD = LANES
MLA_PAD_W = MLA_HEADS * HEAD_PAD
VMEM_LIMIT = 48 * 1024 * 1024


def _cparams(sem):
    return pltpu.CompilerParams(dimension_semantics=sem, vmem_limit_bytes=VMEM_LIMIT)


def _pick(n, pref):
    if n <= pref:
        return n
    t = (pref // LANES) * LANES
    while t >= LANES:
        if n % t == 0:
            return t
        t -= LANES
    return n


def matmul(a, b, *, name, ta=False, tb=False, out_dtype=F32, acc=None, act=None, act_in=None,
           b_slots=None, out_slots=None, tm_pref=MM_TILE, tn_pref=MM_TILE, tk_pref=MM_TILE):
    M, K = (a.shape[1], a.shape[0]) if ta else a.shape
    tm, tk = _pick(M, tm_pref), _pick(K, tk_pref)
    if b_slots == "n":
        assert not tb and b.shape[:2] == (N_DEV, K)
        tn = b.shape[2]
        N = N_DEV * tn
    elif b_slots == "k":
        assert tb and N_DEV * b.shape[2] == K
        N, tk = b.shape[1], b.shape[2]
        tn = _pick(N, tn_pref)
    else:
        N = b.shape[0] if tb else b.shape[1]
        assert (b.shape[1] if tb else b.shape[0]) == K
        tn = _pick(N, tn_pref)
    if out_slots == "m":
        tm = M // N_DEV
    elif out_slots == "n":
        tn = N // N_DEV
    assert out_slots is None or (acc is None and act_in is None)
    nk = K // tk
    dims = (((0 if ta else 1,), (1 if tb else 0,)), ((), ()))
    assert act in (None, "relu2", "relu2_bwd") and (act == "relu2_bwd") == (act_in is not None)

    def body(*refs):
        a_ref, b_ref = refs[0], refs[1]
        pos = 2
        acc_ref = act_ref = None
        if acc is not None:
            acc_ref = refs[pos]
            pos += 1
        if act_in is not None:
            act_ref = refs[pos]
            pos += 1
        out = refs[pos]
        scratch = refs[pos + 1:]

        part = lax.dot_general(a_ref[...].astype(BF16), b_ref[...].astype(BF16), dims,
                               preferred_element_type=F32)

        def finish(r):
            if acc_ref is not None:
                r = r + acc_ref[...]
            if act == "relu2":
                r = jnp.square(jnp.maximum(r, 0.0))
            elif act == "relu2_bwd":
                r = r * (2.0 * jnp.sqrt(act_ref[...].astype(F32)))
            out[...] = r.astype(out.dtype)

        if nk == 1:
            finish(part)
        else:
            acc_sc = scratch[0]
            k = pl.program_id(2)

            @pl.when(k == 0)
            def _():
                acc_sc[...] = part

            @pl.when(k > 0)
            def _():
                acc_sc[...] += part

            @pl.when(k == nk - 1)
            def _():
                finish(acc_sc[...])

    a_spec = pl.BlockSpec((tk, tm), lambda i, j, k: (k, i)) if ta else pl.BlockSpec((tm, tk), lambda i, j, k: (i, k))
    if b_slots == "n":
        b_spec = pl.BlockSpec((None, tk, tn), lambda i, j, k: (j, k, 0))
    elif b_slots == "k":
        b_spec = pl.BlockSpec((None, tn, tk), lambda i, j, k: (k, j, 0))
    else:
        b_spec = pl.BlockSpec((tn, tk), lambda i, j, k: (j, k)) if tb else pl.BlockSpec((tk, tn), lambda i, j, k: (k, j))
    if out_slots == "m":
        o_spec = pl.BlockSpec((None, None, tm, tn), lambda i, j, k: (i % 2, i // 2, 0, j))
        o_shape = (2, N_CHIP, tm, N)
    elif out_slots == "n":
        o_spec = pl.BlockSpec((None, None, tm, tn), lambda i, j, k: (j % 2, j // 2, i, 0))
        o_shape = (2, N_CHIP, M, tn)
    else:
        o_spec = pl.BlockSpec((tm, tn), lambda i, j, k: (i, j))
        o_shape = (M, N)
    in_specs = [a_spec, b_spec]
    args = [a, b]
    if acc is not None:
        in_specs.append(o_spec)
        args.append(acc)
    if act_in is not None:
        in_specs.append(o_spec)
        args.append(act_in)
    return pl.pallas_call(
        body, name=name, grid=(M // tm, N // tn, nk), in_specs=in_specs, out_specs=o_spec,
        out_shape=jax.ShapeDtypeStruct(o_shape, out_dtype),
        scratch_shapes=[pltpu.VMEM((tm, tn), F32)] if nk > 1 else [],
        compiler_params=_cparams(("parallel", "parallel", "arbitrary")),
    )(*args)


ROW_TILE = 256


def _rstd(xv):
    return lax.rsqrt(jnp.mean(xv * xv, axis=-1, keepdims=True) + EPS)


def _rms_bwd_rows(dy, xv, g):
    r = _rstd(xv)
    dyg = dy * g
    dx = r * dyg - xv * (r * r * r) * jnp.mean(dyg * xv, axis=-1, keepdims=True)
    return dx, dy * xv * r


def _row_spec(tm, n, col=0):
    return pl.BlockSpec((tm, n), lambda i: (i, col))


def _vec_spec(n):
    return pl.BlockSpec((1, n), lambda i: (0, 0))


def _acc_rows(ref, i, val):
    @pl.when(i == 0)
    def _():
        ref[...] = val

    @pl.when(i > 0)
    def _():
        ref[...] += val


def rms_fwd(x, g, *, name, n=None, col=0, out_dtype=BF16):
    T = x.shape[0]
    n = x.shape[1] if n is None else n
    tm = _pick(T, ROW_TILE)

    def body(x_ref, g_ref, o_ref):
        xv = x_ref[...]
        o_ref[...] = (xv * _rstd(xv) * g_ref[...]).astype(o_ref.dtype)

    return pl.pallas_call(
        body, name=name, grid=(T // tm,), in_specs=[_row_spec(tm, n, col), _vec_spec(n)],
        out_specs=_row_spec(tm, n), out_shape=jax.ShapeDtypeStruct((T, n), out_dtype),
        compiler_params=_cparams(("parallel",)),
    )(x, g)


def rms_bwd(dy, x, g, *, name, n=None, col=0, residual=None, out_dtype=F32):
    T = x.shape[0]
    n = x.shape[1] if n is None else n
    tm = _pick(T, ROW_TILE)

    def body(*refs):
        if residual is None:
            dy_ref, x_ref, g_ref, dx_ref, dg_ref = refs
        else:
            dy_ref, x_ref, g_ref, res_ref, dx_ref, dg_ref = refs
        dx, dgr = _rms_bwd_rows(dy_ref[...].astype(F32), x_ref[...], g_ref[...])
        if residual is not None:
            dx = dx + res_ref[...]
        dx_ref[...] = dx.astype(dx_ref.dtype)
        _acc_rows(dg_ref, pl.program_id(0), jnp.sum(dgr, axis=0, keepdims=True))

    in_specs = [_row_spec(tm, n), _row_spec(tm, n, col), _vec_spec(n)]
    args = [dy, x, g]
    if residual is not None:
        in_specs.append(_row_spec(tm, n))
        args.append(residual)
    return pl.pallas_call(
        body, name=name, grid=(T // tm,), in_specs=in_specs,
        out_specs=(_row_spec(tm, n), _vec_spec(n)),
        out_shape=(jax.ShapeDtypeStruct((T, n), out_dtype), jax.ShapeDtypeStruct((1, n), F32)),
        compiler_params=_cparams(("arbitrary",)),
    )(*args)


def mix_post_fwd(x, y, g_post, g_pre2, *, name):
    T, n = x.shape
    tm = _pick(T, ROW_TILE)

    def body(x_ref, y_ref, gp_ref, g2_ref, x1_ref, h2_ref):
        yv = y_ref[...]
        x1 = x_ref[...] + yv * _rstd(yv) * gp_ref[...]
        x1_ref[...] = x1
        h2_ref[...] = (x1 * _rstd(x1) * g2_ref[...]).astype(h2_ref.dtype)

    return pl.pallas_call(
        body, name=name, grid=(T // tm,),
        in_specs=[_row_spec(tm, n), _row_spec(tm, n), _vec_spec(n), _vec_spec(n)],
        out_specs=(_row_spec(tm, n), _row_spec(tm, n)),
        out_shape=(jax.ShapeDtypeStruct((T, n), F32), jax.ShapeDtypeStruct((T, n), BF16)),
        compiler_params=_cparams(("parallel",)),
    )(x, y, g_post, g_pre2)


def loss_head(x1, d, g, tgt, *, name):
    T, n = x1.shape
    tm = _pick(T, ROW_TILE)

    def body(x1_ref, d_ref, g_ref, t_ref, loss_ref, dx2_ref, dd_ref, dg_ref):
        i = pl.program_id(0)
        dv = d_ref[...]
        gv = g_ref[...]
        err = x1_ref[...] + dv * _rstd(dv) * gv - t_ref[...]
        part = 0.5 * jnp.sum(jnp.mean(err * err, axis=-1, keepdims=True), axis=0, keepdims=True)
        _acc_rows(loss_ref, i, jnp.broadcast_to(part, (1, LANES)))
        dx2 = err * (1.0 / n)
        dx2_ref[...] = dx2
        dd, dgr = _rms_bwd_rows(dx2, dv, gv)
        dd_ref[...] = dd.astype(dd_ref.dtype)
        _acc_rows(dg_ref, i, jnp.sum(dgr, axis=0, keepdims=True))

    return pl.pallas_call(
        body, name=name, grid=(T // tm,),
        in_specs=[_row_spec(tm, n), _row_spec(tm, n), _vec_spec(n), _row_spec(tm, n)],
        out_specs=(_vec_spec(LANES), _row_spec(tm, n), _row_spec(tm, n), _vec_spec(n)),
        out_shape=(jax.ShapeDtypeStruct((1, LANES), F32), jax.ShapeDtypeStruct((T, n), F32),
                   jax.ShapeDtypeStruct((T, n), BF16), jax.ShapeDtypeStruct((1, n), F32)),
        compiler_params=_cparams(("arbitrary",)),
    )(x1, d, g, tgt)


def mlp_pre_bwd(dx2, dh2, x1, g_pre2, y, g_post, *, name):
    T, n = x1.shape
    tm = _pick(T, ROW_TILE)

    def body(dx2_ref, dh2_ref, x1_ref, g2_ref, y_ref, gp_ref, dx1_ref, dy_ref, dg2_ref, dgp_ref):
        i = pl.program_id(0)
        d1, dg2 = _rms_bwd_rows(dh2_ref[...], x1_ref[...], g2_ref[...])
        dx1 = dx2_ref[...] + d1
        dx1_ref[...] = dx1
        dy, dgp = _rms_bwd_rows(dx1, y_ref[...], gp_ref[...])
        dy_ref[...] = dy.astype(dy_ref.dtype)
        _acc_rows(dg2_ref, i, jnp.sum(dg2, axis=0, keepdims=True))
        _acc_rows(dgp_ref, i, jnp.sum(dgp, axis=0, keepdims=True))

    rs = _row_spec(tm, n)
    return pl.pallas_call(
        body, name=name, grid=(T // tm,),
        in_specs=[rs, rs, rs, _vec_spec(n), rs, _vec_spec(n)],
        out_specs=(rs, rs, _vec_spec(n), _vec_spec(n)),
        out_shape=(jax.ShapeDtypeStruct((T, n), F32), jax.ShapeDtypeStruct((T, n), BF16),
                   jax.ShapeDtypeStruct((1, n), F32), jax.ShapeDtypeStruct((1, n), F32)),
        compiler_params=_cparams(("arbitrary",)),
    )(dx2, dh2, x1, g_pre2, y, g_post)


def _gate_specs(tm):
    gcol = PAD_OFF["gate"] // D_MODEL
    lspecs = [pl.BlockSpec((tm, D_MODEL), functools.partial(lambda i, c: (i, c), c=gcol + b)) for b in range(N_BRANCH)]
    bspecs = [pl.BlockSpec((1, D_MODEL), functools.partial(lambda i, c: (0, c), c=b)) for b in range(N_BRANCH)]
    pspecs = [_row_spec(tm, D_MODEL) for _ in range(N_BRANCH)]
    return lspecs, bspecs, pspecs


def gate_merge(proj, b_gate, ps, *, name):
    T = proj.shape[0]
    tm = _pick(T, ROW_TILE)
    lspecs, bspecs, pspecs = _gate_specs(tm)

    def body(*refs):
        l_refs, b_refs, p_refs, o_ref = refs[0:3], refs[3:6], refs[6:9], refs[9]
        tot = None
        for lr, br, pr in zip(l_refs, b_refs, p_refs):
            term = jax.nn.sigmoid(lr[...] + br[...]) * pr[...]
            tot = term if tot is None else tot + term
        o_ref[...] = tot.astype(o_ref.dtype)

    return pl.pallas_call(
        body, name=name, grid=(T // tm,), in_specs=lspecs + bspecs + pspecs,
        out_specs=_row_spec(tm, D_MODEL), out_shape=jax.ShapeDtypeStruct((T, D_MODEL), BF16),
        compiler_params=_cparams(("parallel",)),
    )(proj, proj, proj, b_gate, b_gate, b_gate, *ps)


def gate_bwd(dmerged, proj, b_gate, ps, *, name):
    T = proj.shape[0]
    tm = _pick(T, ROW_TILE)
    lspecs, bspecs, pspecs = _gate_specs(tm)

    def body(*refs):
        dm_ref = refs[0]
        l_refs, b_refs, p_refs = refs[1:4], refs[4:7], refs[7:10]
        dl_ref, dp_refs, db_ref = refs[10], refs[11:14], refs[14]
        i = pl.program_id(0)
        dm = dm_ref[...]
        for b, (lr, br, pr, dpr) in enumerate(zip(l_refs, b_refs, p_refs, dp_refs)):
            gt = jax.nn.sigmoid(lr[...] + br[...])
            dpr[...] = (dm * gt).astype(dpr.dtype)
            dl = dm * pr[...] * gt * (1.0 - gt)
            dl_ref[:, b * D_MODEL:(b + 1) * D_MODEL] = dl.astype(dl_ref.dtype)
            part = jnp.sum(dl, axis=0, keepdims=True)

            @pl.when(i == 0)
            def _():
                db_ref[:, b * D_MODEL:(b + 1) * D_MODEL] = part

            @pl.when(i > 0)
            def _():
                db_ref[:, b * D_MODEL:(b + 1) * D_MODEL] += part

    return pl.pallas_call(
        body, name=name, grid=(T // tm,), in_specs=[_row_spec(tm, D_MODEL)] + lspecs + bspecs + pspecs,
        out_specs=(_row_spec(tm, GATE_W), *[_row_spec(tm, D_MODEL) for _ in range(N_BRANCH)], _vec_spec(GATE_W)),
        out_shape=(jax.ShapeDtypeStruct((T, GATE_W), BF16),
                   *[jax.ShapeDtypeStruct((T, D_MODEL), BF16) for _ in range(N_BRANCH)],
                   jax.ShapeDtypeStruct((1, GATE_W), F32)),
        compiler_params=_cparams(("arbitrary",)),
    )(dmerged, proj, proj, proj, b_gate, b_gate, b_gate, *ps)


def _rope_tables(pos_ref, invf_ref):
    ang = pos_ref[...] * invf_ref[...]
    lane = lax.broadcasted_iota(jnp.int32, (1, LANES), 1)
    s = jnp.sin(ang)
    split = MLA_NOPE + MLA_ROPE // 2
    return jnp.cos(ang), jnp.where(lane >= split, s, 0.0), jnp.where(lane < split, s, 0.0)


def _rotate(xh, c, s_hi, s_lo, sign):
    half = MLA_ROPE // 2
    up = pltpu.roll(xh, half, 1)
    down = pltpu.roll(xh, LANES - half, 1)
    return xh * c + sign * (up * s_hi - down * s_lo)


def rope_fwd(qpre, kpre, proj, pos, invf, *, name):
    T = qpre.shape[0]
    tm = _pick(T, ROW_TILE)
    kr_col = PAD_OFF["kr"] // KR_PAD

    def body(q_ref, k_ref, kr_ref, pos_ref, invf_ref, qo_ref, ko_ref):
        c, s_hi, s_lo = _rope_tables(pos_ref, invf_ref)
        kr = _rotate(pltpu.roll(kr_ref[...], MLA_NOPE, 1), c, s_hi, s_lo, 1.0)
        for h in range(MLA_HEADS):
            sl = slice(h * HEAD_PAD, (h + 1) * HEAD_PAD)
            qo_ref[:, sl] = _rotate(q_ref[:, sl], c, s_hi, s_lo, 1.0).astype(qo_ref.dtype)
            ko_ref[:, sl] = (k_ref[:, sl] + kr).astype(ko_ref.dtype)

    rs = _row_spec(tm, MLA_PAD_W)
    return pl.pallas_call(
        body, name=name, grid=(T // tm,),
        in_specs=[rs, rs, _row_spec(tm, KR_PAD, kr_col), _row_spec(tm, 1), _vec_spec(LANES)],
        out_specs=(rs, rs),
        out_shape=(jax.ShapeDtypeStruct((T, MLA_PAD_W), BF16), jax.ShapeDtypeStruct((T, MLA_PAD_W), BF16)),
        compiler_params=_cparams(("parallel",)),
    )(qpre, kpre, proj, pos, invf)


def rope_bwd(dq_pad, dk_pad, pos, invf, *, name):
    T = dq_pad.shape[0]
    tm = _pick(T, ROW_TILE)

    def body(dq_ref, dk_ref, pos_ref, invf_ref, dqo_ref, dkr_ref):
        c, s_hi, s_lo = _rope_tables(pos_ref, invf_ref)
        tot = None
        for h in range(MLA_HEADS):
            sl = slice(h * HEAD_PAD, (h + 1) * HEAD_PAD)
            dqo_ref[:, sl] = _rotate(dq_ref[:, sl], c, s_hi, s_lo, -1.0).astype(dqo_ref.dtype)
            tot = dk_ref[:, sl] if tot is None else tot + dk_ref[:, sl]
        dkr = pltpu.roll(_rotate(tot, c, s_hi, s_lo, -1.0), LANES - MLA_NOPE, 1)
        lane = lax.broadcasted_iota(jnp.int32, (1, LANES), 1)
        dkr_ref[...] = jnp.where(lane < MLA_ROPE, dkr, 0.0).astype(dkr_ref.dtype)

    rs = _row_spec(tm, MLA_PAD_W)
    return pl.pallas_call(
        body, name=name, grid=(T // tm,),
        in_specs=[rs, rs, _row_spec(tm, 1), _vec_spec(LANES)],
        out_specs=(rs, _row_spec(tm, KR_PAD)),
        out_shape=(jax.ShapeDtypeStruct((T, MLA_PAD_W), BF16), jax.ShapeDtypeStruct((T, KR_PAD), BF16)),
        compiler_params=_cparams(("parallel",)),
    )(dq_pad, dk_pad, pos, invf)


ATT_TILE = 256
NEG = -1e30


def _split_bf16(v):
    hi = v.astype(BF16)
    return hi, (v - hi.astype(F32)).astype(BF16)


def _dot(a, b, ca, cb):
    return lax.dot_general(a, b, (((ca,), (cb,)), ((), ())), preferred_element_type=F32)


def _tri(tk, kind):
    r = lax.broadcasted_iota(jnp.int32, (tk, tk), 0)
    c = lax.broadcasted_iota(jnp.int32, (tk, tk), 1)
    m = {"gt": r > c, "le": r <= c, "lt": r < c}[kind]
    return jnp.where(m, 1.0, 0.0).astype(BF16)


def _cum(v, tri):
    hi, lo = _split_bf16(v)
    return _dot(hi, tri, 1, 0) + _dot(lo, tri, 1, 0)


def _lane_masks(hp, w):
    lane = lax.broadcasted_iota(jnp.int32, (1, LANES), 1)
    return [(lane >= e * w) & (lane < (e + 1) * w) for e in range(hp)]


def _att_dims(mode, hp, qk_w, v_w):
    assert mode in ("softmax", "sb")
    assert (hp, qk_w, v_w) in ((2, 128, 64), (2, 64, 64), (1, 128, 128))
    qw = hp * LANES if qk_w == LANES else LANES
    return qw


def attn_fwd(q, k, v, *, name, mode, causal, hp, qk_w, v_w, scale, nh, qc, kc, vc):
    qw = _att_dims(mode, hp, qk_w, v_w)
    B, S = q.shape[0], q.shape[1]
    Sk = k.shape[1]
    tq = _pick(S, ATT_TILE)
    tk = _pick(Sk, ATT_TILE)
    if causal:
        assert tq == tk and S == Sk
    nq, nkv = S // tq, Sk // tk

    def body(q_ref, k_ref, v_ref, o_ref, st_ref):
        i = pl.program_id(2)
        qv = q_ref[0].astype(BF16)
        masks_qk = _lane_masks(hp, qk_w) if qk_w < LANES else None
        masks_v = _lane_masks(hp, v_w) if v_w < LANES else None
        if qk_w == LANES:
            qs = [qv[:, e * LANES:(e + 1) * LANES] for e in range(hp)]
        else:
            qs = [jnp.where(masks_qk[e], qv, jnp.zeros_like(qv)) for e in range(hp)]
        rows = lax.broadcasted_iota(jnp.int32, (tq, tk), 0)
        cols = lax.broadcasted_iota(jnp.int32, (tq, tk), 1)
        tri = _tri(tk, "gt") if mode == "sb" else None

        def kv_tile(j):
            off = pl.multiple_of(j * tk, tk)
            kj = k_ref[0, pl.ds(off, tk), :].astype(BF16)
            vj = v_ref[0, pl.ds(off, tk), :].astype(BF16)
            ks = [kj[:, e * LANES:(e + 1) * LANES] for e in range(hp)] if qk_w == LANES else [kj] * hp
            return ks, vj

        def merge(vals):
            if hp == 1:
                return jnp.broadcast_to(vals[0], (tq, LANES))
            return jnp.where(masks_v[0], vals[0], vals[1])

        if mode == "softmax":
            def block(j, carry, diag):
                ms, ls, acc = carry
                ks, vj = kv_tile(j)
                new_m, new_l, alphas, pvs = [], [], [], []
                for e in range(hp):
                    s = _dot(qs[e], ks[e], 1, 1) * scale
                    if diag:
                        s = jnp.where(rows >= cols, s, NEG)
                    m_new = jnp.maximum(ms[e], jnp.max(s, axis=1, keepdims=True))
                    alpha = jnp.exp(ms[e] - m_new)
                    p = jnp.exp(s - m_new)
                    new_l.append(alpha * ls[e] + jnp.sum(p, axis=1, keepdims=True))
                    new_m.append(m_new)
                    alphas.append(alpha)
                    pvs.append(_dot(p.astype(BF16), vj, 1, 0))
                acc = acc * merge(alphas) + merge(pvs)
                return tuple(new_m), tuple(new_l), acc

            init = (tuple(jnp.full((tq, 1), NEG, F32) for _ in range(hp)),
                    tuple(jnp.zeros((tq, 1), F32) for _ in range(hp)),
                    jnp.zeros((tq, LANES), F32))
            if causal:
                carry = lax.fori_loop(0, i, lambda j, c: block(j, c, False), init)
                ms, ls, acc = block(i, carry, True)
            else:
                ms, ls, acc = lax.fori_loop(0, nkv, lambda j, c: block(j, c, False), init)
            o_ref[0] = acc / merge(list(ls))
            st_ref[0, 0] = merge([m + jnp.log(l) for m, l in zip(ms, ls)])
        else:
            def block(j, carry, diag):
                cs_, acc = carry
                ks, vj = kv_tile(j)
                new_c, pvs = [], []
                for e in range(hp):
                    z = _dot(qs[e], ks[e], 1, 1) * scale
                    lg = jnp.log(1.0 + jnp.exp(-jnp.abs(z)))
                    lz = jnp.minimum(z, 0.0) - lg
                    lk = -jnp.maximum(z, 0.0) - lg
                    if diag:
                        lk = jnp.where(rows > cols, lk, 0.0)
                    a = jnp.exp(lz + _cum(lk, tri) + cs_[e])
                    if diag:
                        a = jnp.where(rows > cols, a, 0.0)
                    pvs.append(_dot(a.astype(BF16), vj, 1, 0))
                    new_c.append(cs_[e] + jnp.sum(lk, axis=1, keepdims=True))
                return tuple(new_c), acc + merge(pvs)

            init = (tuple(jnp.zeros((tq, 1), F32) for _ in range(hp)), jnp.zeros((tq, LANES), F32))
            carry = block(i, init, True)
            cs_, acc = lax.fori_loop(0, i, lambda jj, c: block(i - 1 - jj, c, False), carry)
            o_ref[0] = acc
            st_ref[0, 0] = merge(list(cs_))

    return pl.pallas_call(
        body, name=name, grid=(B, nh, nq),
        in_specs=[pl.BlockSpec((1, tq, qw), lambda b, h, i: (b, i, qc + h)),
                  pl.BlockSpec((1, Sk, qw), lambda b, h, i: (b, 0, kc + h)),
                  pl.BlockSpec((1, Sk, LANES), lambda b, h, i: (b, 0, vc + h))],
        out_specs=(pl.BlockSpec((1, tq, LANES), lambda b, h, i: (b, i, h)),
                   pl.BlockSpec((1, 1, tq, LANES), lambda b, h, i: (b, h, i, 0))),
        out_shape=(jax.ShapeDtypeStruct((B, S, nh * LANES), F32), jax.ShapeDtypeStruct((B, nh, S, LANES), F32)),
        compiler_params=_cparams(("parallel", "parallel", "arbitrary")),
    )(q, k, v)


def attn_bwd(q, k, v, o, st, do, *, name, mode, causal, hp, qk_w, v_w, scale, nh, qc, kc, vc):
    qw = _att_dims(mode, hp, qk_w, v_w)
    B, S = q.shape[0], q.shape[1]
    Sk = k.shape[1]
    tq = _pick(S, ATT_TILE)
    tk = _pick(Sk, ATT_TILE)
    if causal:
        assert tq == tk and S == Sk
    nq, nkv = S // tq, Sk // tk

    def body(q_ref, k_ref, v_ref, o_ref, st_ref, do_ref, dq_ref, dk_ref, dv_ref):
        i = pl.program_id(2)

        @pl.when(i == 0)
        def _():
            dk_ref[...] = jnp.zeros_like(dk_ref)
            dv_ref[...] = jnp.zeros_like(dv_ref)

        qv = q_ref[0].astype(BF16)
        dov = do_ref[0]
        stv = st_ref[0, 0]
        masks_qk = _lane_masks(hp, qk_w) if qk_w < LANES else None
        masks_v = _lane_masks(hp, v_w) if v_w < LANES else None
        if qk_w == LANES:
            qs = [qv[:, e * LANES:(e + 1) * LANES] for e in range(hp)]
        else:
            qs = [jnp.where(masks_qk[e], qv, jnp.zeros_like(qv)) for e in range(hp)]
        if hp == 1:
            dos = [dov.astype(BF16)]
            stats = [stv[:, 0:1]]
        else:
            dos = [jnp.where(masks_v[e], dov, 0.0).astype(BF16) for e in range(hp)]
            stats = [stv[:, e * v_w:e * v_w + 1] for e in range(hp)]
        if mode == "softmax":
            prod = dov * o_ref[0]
            if hp == 1:
                dsum = [jnp.sum(prod, axis=1, keepdims=True)]
            else:
                dsum = [jnp.sum(jnp.where(masks_v[e], prod, 0.0), axis=1, keepdims=True) for e in range(hp)]
        rows = lax.broadcasted_iota(jnp.int32, (tq, tk), 0)
        cols = lax.broadcasted_iota(jnp.int32, (tq, tk), 1)
        if mode == "sb":
            tri_le, tri_lt = _tri(tk, "le"), _tri(tk, "lt")

        def kv_tile(j):
            off = pl.multiple_of(j * tk, tk)
            kj = k_ref[0, pl.ds(off, tk), :].astype(BF16)
            vj = v_ref[0, pl.ds(off, tk), :].astype(BF16)
            ks = [kj[:, e * LANES:(e + 1) * LANES] for e in range(hp)] if qk_w == LANES else [kj] * hp
            return off, ks, vj

        def scatter(off, dz_list, p_list):
            dvj = None
            for e in range(hp):
                t = _dot(p_list[e], dos[e], 0, 0)
                dvj = t if dvj is None else dvj + t
            dv_ref[0, pl.ds(off, tk), :] += dvj
            if qk_w == LANES:
                for e in range(hp):
                    dk_ref[0, pl.ds(off, tk), e * LANES:(e + 1) * LANES] += _dot(dz_list[e], qs[e], 0, 0)
            else:
                dkj = None
                for e in range(hp):
                    t = _dot(dz_list[e], qs[e], 0, 0)
                    dkj = t if dkj is None else dkj + t
                dk_ref[0, pl.ds(off, tk), :] += dkj

        def dq_add(dqs, dz_list, ks):
            out = []
            for e in range(hp):
                out.append(dqs[e] + _dot(dz_list[e], ks[e], 1, 0))
            return tuple(out)

        dq0 = tuple(jnp.zeros((tq, LANES), F32) for _ in range(hp))

        if mode == "softmax":
            def block(j, dqs, diag):
                off, ks, vj = kv_tile(j)
                dzs, ps = [], []
                for e in range(hp):
                    s = _dot(qs[e], ks[e], 1, 1) * scale
                    p = jnp.exp(s - stats[e])
                    if diag:
                        p = jnp.where(rows >= cols, p, 0.0)
                    dp = _dot(dos[e], vj, 1, 1)
                    dzs.append((p * (dp - dsum[e]) * scale).astype(BF16))
                    ps.append(p.astype(BF16))
                scatter(off, dzs, ps)
                return dq_add(dqs, dzs, ks)

            if causal:
                dqs = lax.fori_loop(0, i, lambda j, c: block(j, c, False), dq0)
                dqs = block(i, dqs, True)
            else:
                dqs = lax.fori_loop(0, nkv, lambda j, c: block(j, c, False), dq0)
        else:
            def block(j, carry, diag):
                dqs, cps, cgs = carry
                off, ks, vj = kv_tile(j)
                dzs, ps, new_p, new_g = [], [], [], []
                for e in range(hp):
                    z = _dot(qs[e], ks[e], 1, 1) * scale
                    ez = jnp.exp(-jnp.abs(z))
                    lg = jnp.log(1.0 + ez)
                    lz = jnp.minimum(z, 0.0) - lg
                    lk = -jnp.maximum(z, 0.0) - lg
                    if diag:
                        lk = jnp.where(rows > cols, lk, 0.0)
                    keep_after = stats[e] - cps[e] - _cum(lk, tri_le)
                    a = jnp.exp(lz + keep_after)
                    if diag:
                        a = jnp.where(rows > cols, a, 0.0)
                    g = _dot(dos[e], vj, 1, 1) * a
                    gsum = cgs[e] + _cum(g, tri_lt)
                    inv = 1.0 / (1.0 + ez)
                    sig = jnp.where(z >= 0.0, inv, ez * inv)
                    dz = (g * (1.0 - sig) - gsum * sig) * scale
                    if diag:
                        dz = jnp.where(rows > cols, dz, 0.0)
                    dzs.append(dz.astype(BF16))
                    ps.append(a.astype(BF16))
                    new_p.append(cps[e] + jnp.sum(lk, axis=1, keepdims=True))
                    new_g.append(cgs[e] + jnp.sum(g, axis=1, keepdims=True))
                scatter(off, dzs, ps)
                return dq_add(dqs, dzs, ks), tuple(new_p), tuple(new_g)

            zc = tuple(jnp.zeros((tq, 1), F32) for _ in range(hp))
            carry = lax.fori_loop(0, i, lambda j, c: block(j, c, False), (dq0, zc, zc))
            dqs, _, _ = block(i, carry, True)

        if qk_w == LANES:
            for e in range(hp):
                dq_ref[0, :, e * LANES:(e + 1) * LANES] = dqs[e]
        else:
            dq_ref[0] = jnp.where(masks_qk[0], dqs[0], dqs[1])

    return pl.pallas_call(
        body, name=name, grid=(B, nh, nq),
        in_specs=[pl.BlockSpec((1, tq, qw), lambda b, h, i: (b, i, qc + h)),
                  pl.BlockSpec((1, Sk, qw), lambda b, h, i: (b, 0, kc + h)),
                  pl.BlockSpec((1, Sk, LANES), lambda b, h, i: (b, 0, vc + h)),
                  pl.BlockSpec((1, tq, LANES), lambda b, h, i: (b, i, h)),
                  pl.BlockSpec((1, 1, tq, LANES), lambda b, h, i: (b, h, i, 0)),
                  pl.BlockSpec((1, tq, LANES), lambda b, h, i: (b, i, h))],
        out_specs=(pl.BlockSpec((1, tq, qw), lambda b, h, i: (b, i, h)),
                   pl.BlockSpec((1, Sk, qw), lambda b, h, i: (b, 0, h)),
                   pl.BlockSpec((1, Sk, LANES), lambda b, h, i: (b, 0, h))),
        out_shape=(jax.ShapeDtypeStruct((B, S, nh * qw), F32), jax.ShapeDtypeStruct((B, Sk, nh * qw), F32),
                   jax.ShapeDtypeStruct((B, Sk, nh * LANES), F32)),
        compiler_params=_cparams(("parallel", "parallel", "arbitrary")),
    )(q, k, v, o, st, do)


def _peer(x, y, c, k):
    return (x ^ ((k >> 2) & 1), y ^ ((k >> 1) & 1), c ^ (k & 1))


def _pos(p):
    return 4 * p[0] + 2 * p[1] + p[2]


N_CHIP = 4
MESH_ID = pl.DeviceIdType.MESH
_ANY = pl.BlockSpec(memory_space=pl.ANY)


def _me():
    return lax.axis_index("x"), lax.axis_index("y"), lax.axis_index("c")


def _other_chips(x, y):
    return [(1 - x, y), (x, 1 - y), (1 - x, 1 - y)]


def _comm_call(body, arrs, out_shapes, n_sem, n_local, name):
    n = len(arrs)

    def wrapped(*refs):
        body(refs[:n], refs[n:2 * n], refs[2 * n], refs[2 * n + 1], refs[2 * n + 2])

    return pl.pallas_call(
        wrapped, name=name, in_specs=[_ANY] * n, out_specs=[_ANY] * n, out_shape=out_shapes,
        scratch_shapes=[pltpu.SemaphoreType.DMA((n_sem,)), pltpu.SemaphoreType.DMA((n_sem,)),
                        pltpu.SemaphoreType.DMA((max(n_local, 1),))],
        compiler_params=pltpu.CompilerParams(has_side_effects=True),
    )(*arrs)


def gather_two_level(bufs, *, name):
    n = len(bufs)

    def body(x_refs, out_refs, send_sems, recv_sems, local_sems):
        x, y, c = _me()
        me, sibling = (x, y, c), (x, y, 1 - c)
        chips = _other_chips(x, y)

        def copy(a, k, block, to, from_input=False):
            dst = out_refs[a].at[_pos(block)]
            return pltpu.make_async_remote_copy(src_ref=x_refs[a] if from_input else dst, dst_ref=dst,
                                                send_sem=send_sems.at[7 * a + k], recv_sem=recv_sems.at[7 * a + k],
                                                device_id=to, device_id_type=MESH_ID)

        mine = [pltpu.make_async_copy(x_refs[a], out_refs[a].at[_pos(me)], local_sems.at[a]) for a in range(n)]
        for cp in mine:
            cp.start()
        first = []
        for a in range(n):
            first.append(copy(a, 0, me, sibling, from_input=True))
            first += [copy(a, 1 + j, me, (*chip, c), from_input=True) for j, chip in enumerate(chips)]
        for cp in first:
            cp.start()
        passed = []
        for j, chip in enumerate(chips):
            for a in range(n):
                copy(a, 1 + j, (*chip, c), me).wait_recv()
                passed.append(copy(a, 4 + j, (*chip, c), sibling))
                passed[-1].start()
        for a in range(n):
            copy(a, 0, sibling, me).wait_recv()
            for j, chip in enumerate(chips):
                copy(a, 4 + j, (*chip, 1 - c), me).wait_recv()
        for cp in first + passed:
            cp.wait_send()
        for cp in mine:
            cp.wait()

    out_shapes = [jax.ShapeDtypeStruct((N_DEV,) + b.shape, b.dtype) for b in bufs]
    return _comm_call(body, bufs, out_shapes, 7 * n, n, name)


def sibling_exchange(sends, *, name):
    n = len(sends)

    def body(s_refs, out_refs, send_sems, recv_sems, local_sems):
        x, y, c = _me()
        cps = [pltpu.make_async_remote_copy(src_ref=s_refs[a].at[1 - c], dst_ref=out_refs[a], send_sem=send_sems.at[a],
                                            recv_sem=recv_sems.at[a], device_id=(x, y, 1 - c), device_id_type=MESH_ID)
               for a in range(n)]
        for cp in cps:
            cp.start()
        for cp in cps:
            cp.wait()

    out_shapes = [jax.ShapeDtypeStruct(s.shape[1:], s.dtype) for s in sends]
    return _comm_call(body, sends, out_shapes, n, 0, name)


def _flat2(shape):
    return math.prod(shape[:-1]), shape[-1]


def _row_tile(r):
    return r if r <= 512 else _pick8(r, 256)


def _pick8(n, pref):
    t = pref
    while n % t:
        t -= 8
    return t


def chip_sum(send, got, core, *, name):
    shape = got.shape[1:]
    r, cdim = _flat2(shape)
    tr = _row_tile(r)
    send = send.reshape(2, N_CHIP, r, cdim)
    got = got.reshape(N_CHIP, r, cdim)

    def body(core_ref, s_ref, g_ref, o_ref):
        o_ref[...] = (s_ref[0] + g_ref[...]).astype(o_ref.dtype)

    blk = pl.BlockSpec((N_CHIP, tr, cdim), lambda i, core_ref: (0, i, 0))
    out = pl.pallas_call(
        body, name=name,
        grid_spec=pltpu.PrefetchScalarGridSpec(
            num_scalar_prefetch=1, grid=(r // tr,),
            in_specs=[pl.BlockSpec((1, N_CHIP, tr, cdim), lambda i, core_ref: (core_ref[0], 0, i, 0)), blk],
            out_specs=blk),
        out_shape=jax.ShapeDtypeStruct((N_CHIP, r, cdim), BF16),
        compiler_params=_cparams(("parallel",)),
    )(core, send, got)
    return out.reshape((N_CHIP,) + shape)


def chip_exchange(sums, *, name):
    n = len(sums)

    def body(s_refs, out_refs, send_sems, recv_sems, local_sems):
        x, y, c = _me()
        mine = 2 * x + y
        chips = _other_chips(x, y)
        local = [pltpu.make_async_copy(s_refs[a].at[mine], out_refs[a].at[mine], local_sems.at[a]) for a in range(n)]
        for cp in local:
            cp.start()
        sends = []
        for a in range(n):
            for j, (px, py) in enumerate(chips):
                sends.append(pltpu.make_async_remote_copy(
                    src_ref=s_refs[a].at[2 * px + py], dst_ref=out_refs[a].at[mine], send_sem=send_sems.at[3 * a + j],
                    recv_sem=recv_sems.at[3 * a + j], device_id=(px, py, c), device_id_type=MESH_ID))
                sends[-1].start()
        for a in range(n):
            for j, (px, py) in enumerate(chips):
                pltpu.make_async_remote_copy(
                    src_ref=s_refs[a].at[mine], dst_ref=out_refs[a].at[2 * px + py], send_sem=send_sems.at[3 * a + j],
                    recv_sem=recv_sems.at[3 * a + j], device_id=(px, py, c), device_id_type=MESH_ID).wait_recv()
        for cp in sends:
            cp.wait_send()
        for cp in local:
            cp.wait()

    out_shapes = [jax.ShapeDtypeStruct(s.shape, s.dtype) for s in sums]
    return _comm_call(body, sums, out_shapes, 3 * n, n, name)


def sum_adamw(parts, w, m, v, *, name):
    shape = w.shape
    R, cdim = _flat2(shape)
    n_slots = parts.shape[0]
    tr = _row_tile(R)
    parts = parts.reshape(n_slots, R, cdim)
    w, m, v = (t.reshape(R, cdim) for t in (w, m, v))
    c1 = 1.0 - ADAM_B1 ** ADAM_STEP
    c2 = 1.0 - ADAM_B2 ** ADAM_STEP

    def body(p_ref, w_ref, m_ref, v_ref, g_ref, d_ref, mo_ref, vo_ref):
        g = p_ref[0].astype(F32)
        for s in range(1, n_slots):
            g = g + p_ref[s].astype(F32)
        mn = ADAM_B1 * m_ref[...] + (1.0 - ADAM_B1) * g
        vn = ADAM_B2 * v_ref[...] + (1.0 - ADAM_B2) * jnp.square(g)
        g_ref[...] = g
        mo_ref[...] = mn
        vo_ref[...] = vn
        d_ref[...] = -ADAM_LR * ((mn / c1) / (jnp.sqrt(vn / c2) + ADAM_EPS) + ADAM_WD * w_ref[...])

    rs = pl.BlockSpec((tr, cdim), lambda i: (i, 0))
    sd = jax.ShapeDtypeStruct((R, cdim), F32)
    res = pl.pallas_call(
        body, name=name, grid=(R // tr,),
        in_specs=[pl.BlockSpec((n_slots, tr, cdim), lambda i: (0, i, 0)), rs, rs, rs],
        out_specs=(rs, rs, rs, rs), out_shape=(sd, sd, sd, sd),
        compiler_params=_cparams(("parallel",)),
    )(parts, w, m, v)
    return [t.reshape(shape) for t in res]


WEIGHTS = ("ln_mix_pre", "w_in", "b_gate", "q_norm", "w_uq", "kv_norm", "w_uk", "w_uv", "mem_norm", "w_mem_kv",
           "w_branch_out", "w_out", "ln_mix_post", "ln_mlp_pre", "w_mlp_up", "w_mlp_down", "ln_mlp_post")
BIG = dict(w_in=((D_MODEL, IN_WIDTH), 1), w_uq=((Q_LORA, MLA_HEADS * (MLA_NOPE + MLA_ROPE)), 1),
           w_uk=((KV_LORA, MLA_HEADS * MLA_NOPE), 1), w_uv=((KV_LORA, MLA_HEADS * MLA_V), 1),
           w_mem_kv=((D_MODEL, 2 * QM_W), 0), w_branch_out=((N_BRANCH, BRANCH_W, D_MODEL), 2),
           w_out=((D_MODEL, D_MODEL), 0), w_mlp_up=((D_MODEL, D_FF), 1), w_mlp_down=((D_FF, D_MODEL), 0))
SMALL = tuple(n for n in WEIGHTS if n not in BIG)


def _shard_shape(name):
    full, ax = BIG[name]
    return tuple(s // N_DEV if a == ax else s for a, s in enumerate(full))


PACK_TILE = 512


def _pad_rows(a):
    r = a.shape[-2]
    to = PACK_TILE if r > PACK_TILE else 8
    pad = [(0, 0)] * a.ndim
    pad[-2] = (0, (-r) % to)
    return jnp.pad(a, pad)


def _pack_rows(arrs):
    return _pad_rows(jnp.concatenate([a.reshape(-1, LANES) for a in arrs], axis=0))


def _unpack_rows(packed, shapes, lead=()):
    out, r = [], 0
    for shp in shapes:
        n = math.prod(shp) // LANES
        out.append(packed[..., r:r + n, :].reshape(lead + tuple(shp)))
        r += n
    return out


def _slots_to_full(name, slots):
    full, ax = BIG[name]
    return jnp.moveaxis(slots, 0, ax).reshape(full)


def _full_to_slots(name, w):
    full, ax = BIG[name]
    split = full[:ax] + (N_DEV, full[ax] // N_DEV) + full[ax + 1:]
    return jnp.moveaxis(w.reshape(split), ax, 0)


def _full_to_owner(name, w):
    s = _full_to_slots(name, w)
    return jnp.swapaxes(s.reshape((N_CHIP, 2) + s.shape[1:]), 0, 1)


def _split_in(w):
    offs = np.cumsum((0,) + IN_SIZES)
    names = ("cq", "ckv", "kr", "sb", "qm", "gate")
    return {n: w[:, offs[i]:offs[i + 1]] for i, n in enumerate(names)}


def _pad_in(w):
    p = _split_in(w)
    p["kr"] = jnp.pad(p["kr"], ((0, 0), (0, KR_PAD - MLA_ROPE)))
    p["zpad"] = jnp.zeros((w.shape[0], PAD_SIZES["zpad"]), w.dtype)
    return jnp.concatenate([p[n] for n in PAD_ORDER], axis=1)


def _unpad_in(wp):
    p = {n: wp[:, PAD_OFF[n]:PAD_OFF[n] + PAD_SIZES[n]] for n in PAD_ORDER}
    p["kr"] = p["kr"][:, :MLA_ROPE]
    return jnp.concatenate([p[n] for n in ("cq", "ckv", "kr", "sb", "qm", "gate")], axis=1)


def _pad_heads(w, width):
    r = w.shape[0]
    return jnp.pad(w.reshape(r, MLA_HEADS, width), ((0, 0), (0, 0), (0, HEAD_PAD - width))).reshape(r, MLA_PAD_W)


def _unpad_heads(wp, width):
    r = wp.shape[0]
    return wp.reshape(r, MLA_HEADS, HEAD_PAD)[:, :, :width].reshape(r, MLA_HEADS * width)


def _rope_inv_freq():
    half = MLA_ROPE // 2
    inv = 1.0 / (ROPE_THETA ** (jnp.arange(half, dtype=F32) * (2.0 / MLA_ROPE)))
    tab = jnp.zeros((LANES,), F32)
    tab = tab.at[MLA_NOPE:MLA_NOPE + half].set(inv).at[MLA_NOPE + half:MLA_NOPE + MLA_ROPE].set(inv)
    return tab.reshape(1, LANES)


def _local_step(x, mem, positions, tgt, sm, W):
    B, S, D = x.shape
    M = mem.shape[1]
    T = B * S
    x2 = x.reshape(T, D)
    mem2 = mem.reshape(B * M, D)
    pos = positions.reshape(T, 1).astype(F32)
    invf = _rope_inv_freq()
    w_in_pad = _pad_in(W["w_in"])
    w_uq_pad = _pad_heads(W["w_uq"], MLA_NOPE + MLA_ROPE)
    w_uk_pad = _pad_heads(W["w_uk"], MLA_NOPE)
    cq_col, ckv_col = PAD_OFF["cq"] // Q_LORA, PAD_OFF["ckv"] // KV_LORA
    sb_col, qm_col = PAD_OFF["sb"] // LANES, PAD_OFF["qm"] // LANES
    sb_blk = SB_HEADS * SB_DIM // LANES
    mla = dict(mode="softmax", causal=True, hp=2, qk_w=128, v_w=64, scale=(MLA_NOPE + MLA_ROPE) ** -0.5,
               nh=MLA_HEADS // 2, qc=0, kc=0, vc=0)
    sbk = dict(mode="sb", causal=True, hp=2, qk_w=64, v_w=64, scale=SB_DIM ** -0.5, nh=SB_HEADS // 2,
               qc=sb_col, kc=sb_col + sb_blk, vc=sb_col + 2 * sb_blk)
    mca = dict(mode="softmax", causal=False, hp=1, qk_w=128, v_w=128, scale=MEM_DIM ** -0.5, nh=MEM_HEADS,
               qc=qm_col, kc=0, vc=MEM_HEADS)

    h = rms_fwd(x2, sm["ln_mix_pre"], name="rms_mix_pre")
    proj = matmul(h, w_in_pad, name="mm_in")
    proj3 = proj.reshape(B, S, IN_PAD)
    cqn = rms_fwd(proj, sm["q_norm"], n=Q_LORA, col=cq_col, name="rms_q")
    ckvn = rms_fwd(proj, sm["kv_norm"], n=KV_LORA, col=ckv_col, name="rms_kv")
    qpre = matmul(cqn, w_uq_pad, name="mm_uq")
    kpre = matmul(ckvn, w_uk_pad, name="mm_uk")
    v_mla = matmul(ckvn, W["w_uv"], out_dtype=BF16, name="mm_uv").reshape(B, S, -1)
    q_pad, k_pad = rope_fwd(qpre, kpre, proj, pos, invf, name="rope_fwd")
    q_pad, k_pad = q_pad.reshape(B, S, -1), k_pad.reshape(B, S, -1)
    o_mla, st_mla = attn_fwd(q_pad, k_pad, v_mla, name="mla_fwd", **mla)
    o_sb, st_sb = attn_fwd(proj3, proj3, proj3, name="sb_fwd", **sbk)
    memh = rms_fwd(mem2, sm["mem_norm"], name="rms_mem")
    mkv = matmul(memh, W["w_mem_kv"], out_dtype=BF16, name="mm_memkv").reshape(B, M, -1)
    o_mem, st_mem = attn_fwd(proj3, mkv, mkv, name="mem_fwd", **mca)
    outs = [o.reshape(T, BRANCH_W) for o in (o_mla, o_sb, o_mem)]
    ps = [matmul(o, W["w_branch_out"][b], name=f"mm_bo{b}") for b, o in enumerate(outs)]
    merged = gate_merge(proj, sm["b_gate"], ps, name="gate_merge")
    y = matmul(merged, W["w_out"], name="mm_out")
    x1, h2 = mix_post_fwd(x2, y, sm["ln_mix_post"], sm["ln_mlp_pre"], name="mix_post")
    u = matmul(h2, W["w_mlp_up"], b_slots="n", act="relu2", out_dtype=BF16, name="mm_up")
    d = matmul(u, W["w_mlp_down"], name="mm_down")
    loss_p, dx2, dd, dg_mlp_post = loss_head(x1, d, sm["ln_mlp_post"], tgt.reshape(T, D), name="loss_head")

    da = matmul(dd, W["w_mlp_down"], tb=True, act="relu2_bwd", act_in=u, out_dtype=BF16, name="mm_down_dx")
    g_down = matmul(u, dd, ta=True, out_slots="m", name="mm_down_dw")
    dh2 = matmul(da, W["w_mlp_up"], tb=True, b_slots="k", name="mm_up_dx")
    g_up = matmul(h2, da, ta=True, out_slots="n", name="mm_up_dw")
    dx1, dy, dg_mlp_pre, dg_mix_post = mlp_pre_bwd(dx2, dh2, x1, sm["ln_mlp_pre"], y, sm["ln_mix_post"], name="mlp_pre_bwd")
    dmerged = matmul(dy, W["w_out"], tb=True, name="mm_out_dx")
    g_out = matmul(merged, dy, ta=True, name="mm_out_dw")
    dlog, dp0, dp1, dp2, db_gate = gate_bwd(dmerged, proj, sm["b_gate"], ps, name="gate_bwd")
    dps = (dp0, dp1, dp2)
    dos = [matmul(dps[b], W["w_branch_out"][b], tb=True, name=f"mm_bo{b}_dx").reshape(B, S, BRANCH_W) for b in range(N_BRANCH)]
    g_bo = jnp.stack([matmul(outs[b], dps[b], ta=True, name=f"mm_bo{b}_dw") for b in range(N_BRANCH)])
    dq_pad, dk_pad, dv_mla = attn_bwd(q_pad, k_pad, v_mla, o_mla, st_mla, dos[0], name="mla_bwd", **mla)
    dsq, dsk, dsv = attn_bwd(proj3, proj3, proj3, o_sb, st_sb, dos[1], name="sb_bwd", **sbk)
    dqm, dmk, dmv = attn_bwd(proj3, mkv, mkv, o_mem, st_mem, dos[2], name="mem_bwd", **mca)
    dmkv = jnp.concatenate([dmk, dmv], axis=-1).astype(BF16).reshape(B * M, -1)
    dmemh = matmul(dmkv, W["w_mem_kv"], tb=True, name="mm_memkv_dx")
    g_memkv = matmul(memh, dmkv, ta=True, name="mm_memkv_dw")
    _, dg_mem_norm = rms_bwd(dmemh, mem2, sm["mem_norm"], name="rms_mem_bwd")
    dq_pad, dk_pad, dv_mla = dq_pad.reshape(T, -1), dk_pad.reshape(T, -1), dv_mla.reshape(T, -1)
    dqpre, dkr = rope_bwd(dq_pad, dk_pad, pos, invf, name="rope_bwd")
    dcqn = matmul(dqpre, w_uq_pad, tb=True, name="mm_uq_dx")
    g_uq = _unpad_heads(matmul(cqn, dqpre, ta=True, name="mm_uq_dw"), MLA_NOPE + MLA_ROPE)
    dckvn = matmul(dk_pad, w_uk_pad, tb=True, name="mm_uk_dx")
    dckvn = matmul(dv_mla, W["w_uv"], tb=True, acc=dckvn, name="mm_uv_dx")
    g_uk = _unpad_heads(matmul(ckvn, dk_pad, ta=True, name="mm_uk_dw"), MLA_NOPE)
    g_uv = matmul(ckvn, dv_mla, ta=True, name="mm_uv_dw")
    dcq, dg_q_norm = rms_bwd(dcqn, proj, sm["q_norm"], n=Q_LORA, col=cq_col, out_dtype=BF16, name="rms_q_bwd")
    dckv, dg_kv_norm = rms_bwd(dckvn, proj, sm["kv_norm"], n=KV_LORA, col=ckv_col, out_dtype=BF16, name="rms_kv_bwd")
    pieces = dict(gate=dlog, sb=jnp.concatenate([dsq, dsk, dsv], axis=-1).reshape(T, -1), qm=dqm.reshape(T, -1),
                  ckv=dckv, cq=dcq, kr=dkr, zpad=jnp.zeros((T, PAD_SIZES["zpad"]), BF16))
    dproj = jnp.concatenate([pieces[n].astype(BF16) for n in PAD_ORDER], axis=1)
    dh = matmul(dproj, w_in_pad, tb=True, name="mm_in_dx")
    g_in = _unpad_in(matmul(h, dproj, ta=True, name="mm_in_dw"))
    dx, dg_mix_pre = rms_bwd(dh, x2, sm["ln_mix_pre"], residual=dx1, name="rms_mix_pre_bwd")

    big = dict(w_in=g_in, w_uq=g_uq, w_uk=g_uk, w_uv=g_uv, w_mem_kv=g_memkv, w_branch_out=g_bo, w_out=g_out)
    big = {n: _full_to_owner(n, g) for n, g in big.items()}
    big.update(w_mlp_up=g_up, w_mlp_down=g_down)
    small = dict(ln_mix_pre=dg_mix_pre, b_gate=db_gate, q_norm=dg_q_norm, kv_norm=dg_kv_norm, mem_norm=dg_mem_norm,
                 ln_mix_post=dg_mix_post, ln_mlp_pre=dg_mlp_pre, ln_mlp_post=dg_mlp_post)
    return loss_p[0, 0], dx.reshape(B, S, D), big, small


def kernel(x, mem, positions, ln_mix_pre, w_in, b_gate, q_norm, w_uq, kv_norm, w_uk, w_uv, mem_norm, w_mem_kv, w_branch_out, w_out, ln_mix_post, ln_mlp_pre, w_mlp_up, w_mlp_down, ln_mlp_post, loss_target, m_ln_mix_pre, m_w_in, m_b_gate, m_q_norm, m_w_uq, m_kv_norm, m_w_uk, m_w_uv, m_mem_norm, m_w_mem_kv, m_w_branch_out, m_w_out, m_ln_mix_post, m_ln_mlp_pre, m_w_mlp_up, m_w_mlp_down, m_ln_mlp_post, v_ln_mix_pre, v_w_in, v_b_gate, v_q_norm, v_w_uq, v_kv_norm, v_w_uk, v_w_uv, v_mem_norm, v_w_mem_kv, v_w_branch_out, v_w_out, v_ln_mix_post, v_ln_mlp_pre, v_w_mlp_up, v_w_mlp_down, v_ln_mlp_post):
    given = dict(locals())
    w = {n: given[n][0] for n in WEIGHTS}
    m = {n: given["m_" + n][0] for n in WEIGHTS}
    v = {n: given["v_" + n][0] for n in WEIGHTS}
    big_names = tuple(BIG)

    slots = gather_two_level([w[n].astype(BF16) for n in big_names], name="gather_weights")
    W = {n: s if n == "w_mlp_up" else _slots_to_full(n, s) for n, s in zip(big_names, slots)}
    sm = {n: w[n].reshape(1, -1) for n in SMALL}

    loss_part, grad_x, g_big, g_small = _local_step(x, mem, positions, loss_target, sm, W)
    loss = lax.psum(loss_part, ("x", "y", "c"))

    core = lax.axis_index("c").astype(jnp.int32).reshape(1)
    sends = [g_big[n] for n in big_names]
    gots = sibling_exchange(sends, name="grads_sibling")
    sums = [chip_sum(s, g, core, name=f"chip_sum_{n}") for n, s, g in zip(big_names, sends, gots)]
    recvs = chip_exchange(sums, name="grads_chips")
    res = {n: sum_adamw(r, w[n], m[n], v[n], name=f"adamw_{n}") for n, r in zip(big_names, recvs)}

    small_shapes = [w[n].shape for n in SMALL]
    parts = gather_two_level([_pack_rows([g_small[n] for n in SMALL])], name="gather_small_grads")[0]
    res_small = sum_adamw(parts, *[_pack_rows([t[n] for n in SMALL]) for t in (w, m, v)], name="adamw_replicated")
    res_small = [_unpack_rows(r, small_shapes) for r in res_small]
    for i, n in enumerate(SMALL):
        res[n] = [r[i] for r in res_small]

    out = [loss, grad_x]
    for k in range(4):
        out += [res[n][k][None] for n in WEIGHTS]
    return tuple(out)
```

```python
import functools
import math

import numpy as np
import jax
import jax.numpy as jnp
from jax import lax
from jax.experimental import pallas as pl
from jax.experimental.pallas import tpu as pltpu

F32 = jnp.float32
BF16 = jnp.bfloat16

D_MODEL = 1024
MEM_HEADS, MEM_DIM = 4, 128
MLA_HEADS, MLA_NOPE, MLA_ROPE, MLA_V = 8, 64, 32, 64
Q_LORA, KV_LORA = 384, 256
ROPE_THETA = 10000.0
SB_HEADS, SB_DIM = 8, 64
D_FF = 4 * D_MODEL
N_BRANCH, BRANCH_W = 3, 512
EPS = 1e-6
SB_W = 3 * SB_HEADS * SB_DIM
QM_W = MEM_HEADS * MEM_DIM
GATE_W = N_BRANCH * D_MODEL
IN_SIZES = (Q_LORA, KV_LORA, MLA_ROPE, SB_W, QM_W, GATE_W)
IN_WIDTH = sum(IN_SIZES)
ADAM_LR, ADAM_B1, ADAM_B2, ADAM_EPS, ADAM_WD, ADAM_STEP = 0.001, 0.9, 0.999, 1e-08, 0.01, 10

N_DEV = 8
LANES = 128
KR_PAD = LANES
PAD_ORDER = ("gate", "cq", "kr", "ckv", "sb", "qm", "zpad")
MM_TILE = 1024
PAD_SIZES = dict(gate=GATE_W, cq=Q_LORA, kr=KR_PAD, ckv=KV_LORA, sb=SB_W, qm=QM_W)
PAD_SIZES["zpad"] = (-sum(PAD_SIZES.values())) % MM_TILE
PAD_OFF = {}
_o = 0
for _n in PAD_ORDER:
    PAD_OFF[_n] = _o
    _o += PAD_SIZES[_n]
IN_PAD = _o
HEAD_PAD = LANES
MLA_PAD_W = MLA_HEADS * HEAD_PAD
VMEM_LIMIT = 48 * 1024 * 1024


def _cparams(sem):
    return pltpu.CompilerParams(dimension_semantics=sem, vmem_limit_bytes=VMEM_LIMIT)


def _pick(n, pref):
    if n <= pref:
        return n
    t = (pref // LANES) * LANES
    while t >= LANES:
        if n % t == 0:
            return t
        t -= LANES
    return n


def matmul(a, b, *, name, ta=False, tb=False, out_dtype=F32, acc=None, act=None, act_in=None,
           b_slots=None, out_slots=None, comm=None, tm_pref=MM_TILE, tn_pref=MM_TILE, tk_pref=MM_TILE):
    M, K = (a.shape[1], a.shape[0]) if ta else a.shape
    tm, tk = _pick(M, tm_pref), _pick(K, tk_pref)
    if b_slots == "n":
        assert not tb and b.shape[:2] == (N_DEV, K)
        tn = b.shape[2]
        N = N_DEV * tn
    elif b_slots == "k":
        assert tb and N_DEV * b.shape[2] == K
        N, tk = b.shape[1], b.shape[2]
        tn = _pick(N, tn_pref)
    else:
        N = b.shape[0] if tb else b.shape[1]
        assert (b.shape[1] if tb else b.shape[0]) == K
        tn = _pick(N, tn_pref)
    if out_slots == "m":
        tm = M // N_DEV
    elif out_slots == "n":
        tn = N // N_DEV
    assert out_slots is None or (acc is None and act_in is None)
    nk = K // tk
    dims = (((0 if ta else 1,), (1 if tb else 0,)), ((), ()))
    assert act in (None, "relu2", "relu2_bwd") and (act == "relu2_bwd") == (act_in is not None)

    def body(*refs):
        a_ref, b_ref = refs[0], refs[1]
        pos = 2
        acc_ref = act_ref = None
        if acc is not None:
            acc_ref = refs[pos]
            pos += 1
        if act_in is not None:
            act_ref = refs[pos]
            pos += 1
        out = refs[pos]
        scratch = refs[pos + 1:]

        part = lax.dot_general(a_ref[...].astype(BF16), b_ref[...].astype(BF16), dims,
                               preferred_element_type=F32)

        def finish(r):
            if acc_ref is not None:
                r = r + acc_ref[...]
            if act == "relu2":
                r = jnp.square(jnp.maximum(r, 0.0))
            elif act == "relu2_bwd":
                r = r * (2.0 * jnp.sqrt(act_ref[...].astype(F32)))
            out[...] = r.astype(out.dtype)

        if nk == 1:
            finish(part)
        else:
            acc_sc = scratch[0]
            k = pl.program_id(2)

            @pl.when(k == 0)
            def _():
                acc_sc[...] = part

            @pl.when(k > 0)
            def _():
                acc_sc[...] += part

            @pl.when(k == nk - 1)
            def _():
                finish(acc_sc[...])

    a_spec = pl.BlockSpec((tk, tm), lambda i, j, k: (k, i)) if ta else pl.BlockSpec((tm, tk), lambda i, j, k: (i, k))
    if b_slots == "n":
        b_spec = pl.BlockSpec((None, tk, tn), lambda i, j, k: (j, k, 0))
    elif b_slots == "k":
        b_spec = pl.BlockSpec((None, tn, tk), lambda i, j, k: (k, j, 0))
    else:
        b_spec = pl.BlockSpec((tn, tk), lambda i, j, k: (j, k)) if tb else pl.BlockSpec((tk, tn), lambda i, j, k: (k, j))
    if out_slots == "m":
        o_spec = pl.BlockSpec((None, None, tm, tn), lambda i, j, k: (i % 2, i // 2, 0, j))
        o_shape = (2, N_CHIP, tm, N)
    elif out_slots == "n":
        o_spec = pl.BlockSpec((None, None, tm, tn), lambda i, j, k: (j % 2, j // 2, i, 0))
        o_shape = (2, N_CHIP, M, tn)
    else:
        o_spec = pl.BlockSpec((tm, tn), lambda i, j, k: (i, j))
        o_shape = (M, N)
    in_specs = [a_spec, b_spec]
    args = [a, b]
    if acc is not None:
        in_specs.append(o_spec)
        args.append(acc)
    if act_in is not None:
        in_specs.append(o_spec)
        args.append(act_in)
    outs, c_outs = _call(
        body, name=name, grid=(M // tm, N // tn, nk), in_specs=in_specs, out_specs=[o_spec],
        out_shape=[jax.ShapeDtypeStruct(o_shape, out_dtype)],
        scratch_shapes=[pltpu.VMEM((tm, tn), F32)] if nk > 1 else [], args=args,
        sem=("parallel", "parallel", "arbitrary"), comm=comm)
    return (outs[0], c_outs) if comm is not None else outs[0]


ROW_TILE = 256


def _rstd(xv):
    return lax.rsqrt(jnp.mean(xv * xv, axis=-1, keepdims=True) + EPS)


def _rms_bwd_rows(dy, xv, g):
    r = _rstd(xv)
    dyg = dy * g
    dx = r * dyg - xv * (r * r * r) * jnp.mean(dyg * xv, axis=-1, keepdims=True)
    return dx, dy * xv * r


def _row_spec(tm, n, col=0):
    return pl.BlockSpec((tm, n), lambda i: (i, col))


def _vec_spec(n):
    return pl.BlockSpec((1, n), lambda i: (0, 0))


def _acc_rows(ref, i, val):
    @pl.when(i == 0)
    def _():
        ref[...] = val

    @pl.when(i > 0)
    def _():
        ref[...] += val


def rms_fwd(x, g, *, name, n=None, col=0, out_dtype=BF16):
    T = x.shape[0]
    n = x.shape[1] if n is None else n
    tm = _pick(T, ROW_TILE)

    def body(x_ref, g_ref, o_ref):
        xv = x_ref[...]
        o_ref[...] = (xv * _rstd(xv) * g_ref[...]).astype(o_ref.dtype)

    return pl.pallas_call(
        body, name=name, grid=(T // tm,), in_specs=[_row_spec(tm, n, col), _vec_spec(n)],
        out_specs=_row_spec(tm, n), out_shape=jax.ShapeDtypeStruct((T, n), out_dtype),
        compiler_params=_cparams(("parallel",)),
    )(x, g)


def rms_bwd(dy, x, g, *, name, n=None, col=0, residual=None, out_dtype=F32):
    T = x.shape[0]
    n = x.shape[1] if n is None else n
    tm = _pick(T, ROW_TILE)

    def body(*refs):
        if residual is None:
            dy_ref, x_ref, g_ref, dx_ref, dg_ref = refs
        else:
            dy_ref, x_ref, g_ref, res_ref, dx_ref, dg_ref = refs
        dx, dgr = _rms_bwd_rows(dy_ref[...].astype(F32), x_ref[...], g_ref[...])
        if residual is not None:
            dx = dx + res_ref[...]
        dx_ref[...] = dx.astype(dx_ref.dtype)
        _acc_rows(dg_ref, pl.program_id(0), jnp.sum(dgr, axis=0, keepdims=True))

    in_specs = [_row_spec(tm, n), _row_spec(tm, n, col), _vec_spec(n)]
    args = [dy, x, g]
    if residual is not None:
        in_specs.append(_row_spec(tm, n))
        args.append(residual)
    return pl.pallas_call(
        body, name=name, grid=(T // tm,), in_specs=in_specs,
        out_specs=(_row_spec(tm, n), _vec_spec(n)),
        out_shape=(jax.ShapeDtypeStruct((T, n), out_dtype), jax.ShapeDtypeStruct((1, n), F32)),
        compiler_params=_cparams(("arbitrary",)),
    )(*args)


def mix_post_fwd(x, y, g_post, g_pre2, *, name):
    T, n = x.shape
    tm = _pick(T, ROW_TILE)

    def body(x_ref, y_ref, gp_ref, g2_ref, x1_ref, h2_ref):
        yv = y_ref[...]
        x1 = x_ref[...] + yv * _rstd(yv) * gp_ref[...]
        x1_ref[...] = x1
        h2_ref[...] = (x1 * _rstd(x1) * g2_ref[...]).astype(h2_ref.dtype)

    return pl.pallas_call(
        body, name=name, grid=(T // tm,),
        in_specs=[_row_spec(tm, n), _row_spec(tm, n), _vec_spec(n), _vec_spec(n)],
        out_specs=(_row_spec(tm, n), _row_spec(tm, n)),
        out_shape=(jax.ShapeDtypeStruct((T, n), F32), jax.ShapeDtypeStruct((T, n), BF16)),
        compiler_params=_cparams(("parallel",)),
    )(x, y, g_post, g_pre2)


def loss_head(x1, d, g, tgt, *, name):
    T, n = x1.shape
    tm = _pick(T, ROW_TILE)

    def body(x1_ref, d_ref, g_ref, t_ref, loss_ref, dx2_ref, dd_ref, dg_ref):
        i = pl.program_id(0)
        dv = d_ref[...]
        gv = g_ref[...]
        err = x1_ref[...] + dv * _rstd(dv) * gv - t_ref[...]
        part = 0.5 * jnp.sum(jnp.mean(err * err, axis=-1, keepdims=True), axis=0, keepdims=True)
        _acc_rows(loss_ref, i, jnp.broadcast_to(part, (1, LANES)))
        dx2 = err * (1.0 / n)
        dx2_ref[...] = dx2
        dd, dgr = _rms_bwd_rows(dx2, dv, gv)
        dd_ref[...] = dd.astype(dd_ref.dtype)
        _acc_rows(dg_ref, i, jnp.sum(dgr, axis=0, keepdims=True))

    return pl.pallas_call(
        body, name=name, grid=(T // tm,),
        in_specs=[_row_spec(tm, n), _row_spec(tm, n), _vec_spec(n), _row_spec(tm, n)],
        out_specs=(_vec_spec(LANES), _row_spec(tm, n), _row_spec(tm, n), _vec_spec(n)),
        out_shape=(jax.ShapeDtypeStruct((1, LANES), F32), jax.ShapeDtypeStruct((T, n), F32),
                   jax.ShapeDtypeStruct((T, n), BF16), jax.ShapeDtypeStruct((1, n), F32)),
        compiler_params=_cparams(("arbitrary",)),
    )(x1, d, g, tgt)


def mlp_pre_bwd(dx2, dh2, x1, g_pre2, y, g_post, *, name):
    T, n = x1.shape
    tm = _pick(T, ROW_TILE)

    def body(dx2_ref, dh2_ref, x1_ref, g2_ref, y_ref, gp_ref, dx1_ref, dy_ref, dg2_ref, dgp_ref):
        i = pl.program_id(0)
        d1, dg2 = _rms_bwd_rows(dh2_ref[...], x1_ref[...], g2_ref[...])
        dx1 = dx2_ref[...] + d1
        dx1_ref[...] = dx1
        dy, dgp = _rms_bwd_rows(dx1, y_ref[...], gp_ref[...])
        dy_ref[...] = dy.astype(dy_ref.dtype)
        _acc_rows(dg2_ref, i, jnp.sum(dg2, axis=0, keepdims=True))
        _acc_rows(dgp_ref, i, jnp.sum(dgp, axis=0, keepdims=True))

    rs = _row_spec(tm, n)
    return pl.pallas_call(
        body, name=name, grid=(T // tm,),
        in_specs=[rs, rs, rs, _vec_spec(n), rs, _vec_spec(n)],
        out_specs=(rs, rs, _vec_spec(n), _vec_spec(n)),
        out_shape=(jax.ShapeDtypeStruct((T, n), F32), jax.ShapeDtypeStruct((T, n), BF16),
                   jax.ShapeDtypeStruct((1, n), F32), jax.ShapeDtypeStruct((1, n), F32)),
        compiler_params=_cparams(("arbitrary",)),
    )(dx2, dh2, x1, g_pre2, y, g_post)


def _gate_specs(tm):
    gcol = PAD_OFF["gate"] // D_MODEL
    lspecs = [pl.BlockSpec((tm, D_MODEL), functools.partial(lambda i, c: (i, c), c=gcol + b)) for b in range(N_BRANCH)]
    bspecs = [pl.BlockSpec((1, D_MODEL), functools.partial(lambda i, c: (0, c), c=b)) for b in range(N_BRANCH)]
    pspecs = [_row_spec(tm, D_MODEL) for _ in range(N_BRANCH)]
    return lspecs, bspecs, pspecs


def gate_merge(proj, b_gate, ps, *, name):
    T = proj.shape[0]
    tm = _pick(T, ROW_TILE)
    lspecs, bspecs, pspecs = _gate_specs(tm)

    def body(*refs):
        l_refs, b_refs, p_refs, o_ref = refs[0:3], refs[3:6], refs[6:9], refs[9]
        tot = None
        for lr, br, pr in zip(l_refs, b_refs, p_refs):
            term = jax.nn.sigmoid(lr[...] + br[...]) * pr[...]
            tot = term if tot is None else tot + term
        o_ref[...] = tot.astype(o_ref.dtype)

    return pl.pallas_call(
        body, name=name, grid=(T // tm,), in_specs=lspecs + bspecs + pspecs,
        out_specs=_row_spec(tm, D_MODEL), out_shape=jax.ShapeDtypeStruct((T, D_MODEL), BF16),
        compiler_params=_cparams(("parallel",)),
    )(proj, proj, proj, b_gate, b_gate, b_gate, *ps)


def gate_bwd(dmerged, proj, b_gate, ps, *, name):
    T = proj.shape[0]
    tm = _pick(T, ROW_TILE)
    lspecs, bspecs, pspecs = _gate_specs(tm)

    def body(*refs):
        dm_ref = refs[0]
        l_refs, b_refs, p_refs = refs[1:4], refs[4:7], refs[7:10]
        dl_ref, dp_refs, db_ref = refs[10], refs[11:14], refs[14]
        i = pl.program_id(0)
        dm = dm_ref[...]
        for b, (lr, br, pr, dpr) in enumerate(zip(l_refs, b_refs, p_refs, dp_refs)):
            gt = jax.nn.sigmoid(lr[...] + br[...])
            dpr[...] = (dm * gt).astype(dpr.dtype)
            dl = dm * pr[...] * gt * (1.0 - gt)
            dl_ref[:, b * D_MODEL:(b + 1) * D_MODEL] = dl.astype(dl_ref.dtype)
            part = jnp.sum(dl, axis=0, keepdims=True)

            @pl.when(i == 0)
            def _():
                db_ref[:, b * D_MODEL:(b + 1) * D_MODEL] = part

            @pl.when(i > 0)
            def _():
                db_ref[:, b * D_MODEL:(b + 1) * D_MODEL] += part

    return pl.pallas_call(
        body, name=name, grid=(T // tm,), in_specs=[_row_spec(tm, D_MODEL)] + lspecs + bspecs + pspecs,
        out_specs=(_row_spec(tm, GATE_W), *[_row_spec(tm, D_MODEL) for _ in range(N_BRANCH)], _vec_spec(GATE_W)),
        out_shape=(jax.ShapeDtypeStruct((T, GATE_W), BF16),
                   *[jax.ShapeDtypeStruct((T, D_MODEL), BF16) for _ in range(N_BRANCH)],
                   jax.ShapeDtypeStruct((1, GATE_W), F32)),
        compiler_params=_cparams(("arbitrary",)),
    )(dmerged, proj, proj, proj, b_gate, b_gate, b_gate, *ps)


def _rope_tables(pos_ref, invf_ref):
    ang = pos_ref[...] * invf_ref[...]
    lane = lax.broadcasted_iota(jnp.int32, (1, LANES), 1)
    s = jnp.sin(ang)
    split = MLA_NOPE + MLA_ROPE // 2
    return jnp.cos(ang), jnp.where(lane >= split, s, 0.0), jnp.where(lane < split, s, 0.0)


def _rotate(xh, c, s_hi, s_lo, sign):
    half = MLA_ROPE // 2
    up = pltpu.roll(xh, half, 1)
    down = pltpu.roll(xh, LANES - half, 1)
    return xh * c + sign * (up * s_hi - down * s_lo)


def rope_fwd(qpre, kpre, proj, pos, invf, *, name):
    T = qpre.shape[0]
    tm = _pick(T, ROW_TILE)
    kr_col = PAD_OFF["kr"] // KR_PAD

    def body(q_ref, k_ref, kr_ref, pos_ref, invf_ref, qo_ref, ko_ref):
        c, s_hi, s_lo = _rope_tables(pos_ref, invf_ref)
        kr = _rotate(pltpu.roll(kr_ref[...], MLA_NOPE, 1), c, s_hi, s_lo, 1.0)
        for h in range(MLA_HEADS):
            sl = slice(h * HEAD_PAD, (h + 1) * HEAD_PAD)
            qo_ref[:, sl] = _rotate(q_ref[:, sl], c, s_hi, s_lo, 1.0).astype(qo_ref.dtype)
            ko_ref[:, sl] = (k_ref[:, sl] + kr).astype(ko_ref.dtype)

    rs = _row_spec(tm, MLA_PAD_W)
    return pl.pallas_call(
        body, name=name, grid=(T // tm,),
        in_specs=[rs, rs, _row_spec(tm, KR_PAD, kr_col), _row_spec(tm, 1), _vec_spec(LANES)],
        out_specs=(rs, rs),
        out_shape=(jax.ShapeDtypeStruct((T, MLA_PAD_W), BF16), jax.ShapeDtypeStruct((T, MLA_PAD_W), BF16)),
        compiler_params=_cparams(("parallel",)),
    )(qpre, kpre, proj, pos, invf)


def rope_bwd(dq_pad, dk_pad, pos, invf, *, name):
    T = dq_pad.shape[0]
    tm = _pick(T, ROW_TILE)

    def body(dq_ref, dk_ref, pos_ref, invf_ref, dqo_ref, dkr_ref):
        c, s_hi, s_lo = _rope_tables(pos_ref, invf_ref)
        tot = None
        for h in range(MLA_HEADS):
            sl = slice(h * HEAD_PAD, (h + 1) * HEAD_PAD)
            dqo_ref[:, sl] = _rotate(dq_ref[:, sl], c, s_hi, s_lo, -1.0).astype(dqo_ref.dtype)
            tot = dk_ref[:, sl] if tot is None else tot + dk_ref[:, sl]
        dkr = pltpu.roll(_rotate(tot, c, s_hi, s_lo, -1.0), LANES - MLA_NOPE, 1)
        lane = lax.broadcasted_iota(jnp.int32, (1, LANES), 1)
        dkr_ref[...] = jnp.where(lane < MLA_ROPE, dkr, 0.0).astype(dkr_ref.dtype)

    rs = _row_spec(tm, MLA_PAD_W)
    return pl.pallas_call(
        body, name=name, grid=(T // tm,),
        in_specs=[rs, rs, _row_spec(tm, 1), _vec_spec(LANES)],
        out_specs=(rs, _row_spec(tm, KR_PAD)),
        out_shape=(jax.ShapeDtypeStruct((T, MLA_PAD_W), BF16), jax.ShapeDtypeStruct((T, KR_PAD), BF16)),
        compiler_params=_cparams(("parallel",)),
    )(dq_pad, dk_pad, pos, invf)


ATT_TILE = 256
NEG = -1e30


def _split_bf16(v):
    hi = v.astype(BF16)
    return hi, (v - hi.astype(F32)).astype(BF16)


def _dot(a, b, ca, cb):
    return lax.dot_general(a, b, (((ca,), (cb,)), ((), ())), preferred_element_type=F32)


def _tri(tk, kind):
    r = lax.broadcasted_iota(jnp.int32, (tk, tk), 0)
    c = lax.broadcasted_iota(jnp.int32, (tk, tk), 1)
    m = {"gt": r > c, "le": r <= c, "lt": r < c}[kind]
    return jnp.where(m, 1.0, 0.0).astype(BF16)


def _cum(v, tri):
    hi, lo = _split_bf16(v)
    return _dot(hi, tri, 1, 0) + _dot(lo, tri, 1, 0)


def _lane_masks(hp, w):
    lane = lax.broadcasted_iota(jnp.int32, (1, LANES), 1)
    return [(lane >= e * w) & (lane < (e + 1) * w) for e in range(hp)]


def _att_dims(mode, hp, qk_w, v_w):
    assert mode in ("softmax", "sb")
    assert (hp, qk_w, v_w) in ((2, 128, 64), (2, 64, 64), (1, 128, 128))
    qw = hp * LANES if qk_w == LANES else LANES
    return qw


def attn_fwd(q, k, v, *, name, mode, causal, hp, qk_w, v_w, scale, nh, qc, kc, vc, comm=None):
    qw = _att_dims(mode, hp, qk_w, v_w)
    B, S = q.shape[0], q.shape[1]
    Sk = k.shape[1]
    tq = _pick(S, ATT_TILE)
    tk = _pick(Sk, ATT_TILE)
    if causal:
        assert tq == tk and S == Sk
    nq, nkv = S // tq, Sk // tk

    def body(q_ref, k_ref, v_ref, o_ref, st_ref):
        i = pl.program_id(2)
        qv = q_ref[0].astype(BF16)
        masks_qk = _lane_masks(hp, qk_w) if qk_w < LANES else None
        masks_v = _lane_masks(hp, v_w) if v_w < LANES else None
        if qk_w == LANES:
            qs = [qv[:, e * LANES:(e + 1) * LANES] for e in range(hp)]
        else:
            qs = [jnp.where(masks_qk[e], qv, jnp.zeros_like(qv)) for e in range(hp)]
        rows = lax.broadcasted_iota(jnp.int32, (tq, tk), 0)
        cols = lax.broadcasted_iota(jnp.int32, (tq, tk), 1)
        tri = _tri(tk, "gt") if mode == "sb" else None

        def kv_tile(j):
            off = pl.multiple_of(j * tk, tk)
            kj = k_ref[0, pl.ds(off, tk), :].astype(BF16)
            vj = v_ref[0, pl.ds(off, tk), :].astype(BF16)
            ks = [kj[:, e * LANES:(e + 1) * LANES] for e in range(hp)] if qk_w == LANES else [kj] * hp
            return ks, vj

        def merge(vals):
            if hp == 1:
                return jnp.broadcast_to(vals[0], (tq, LANES))
            return jnp.where(masks_v[0], vals[0], vals[1])

        if mode == "softmax":
            def block(j, carry, diag):
                ms, ls, acc = carry
                ks, vj = kv_tile(j)
                new_m, new_l, alphas, pvs = [], [], [], []
                for e in range(hp):
                    s = _dot(qs[e], ks[e], 1, 1) * scale
                    if diag:
                        s = jnp.where(rows >= cols, s, NEG)
                    m_new = jnp.maximum(ms[e], jnp.max(s, axis=1, keepdims=True))
                    alpha = jnp.exp(ms[e] - m_new)
                    p = jnp.exp(s - m_new)
                    new_l.append(alpha * ls[e] + jnp.sum(p, axis=1, keepdims=True))
                    new_m.append(m_new)
                    alphas.append(alpha)
                    pvs.append(_dot(p.astype(BF16), vj, 1, 0))
                acc = acc * merge(alphas) + merge(pvs)
                return tuple(new_m), tuple(new_l), acc

            init = (tuple(jnp.full((tq, 1), NEG, F32) for _ in range(hp)),
                    tuple(jnp.zeros((tq, 1), F32) for _ in range(hp)),
                    jnp.zeros((tq, LANES), F32))
            if causal:
                carry = lax.fori_loop(0, i, lambda j, c: block(j, c, False), init)
                ms, ls, acc = block(i, carry, True)
            else:
                ms, ls, acc = lax.fori_loop(0, nkv, lambda j, c: block(j, c, False), init)
            o_ref[0] = acc / merge(list(ls))
            st_ref[0, 0] = merge([m + jnp.log(l) for m, l in zip(ms, ls)])
        else:
            def block(j, carry, diag):
                cs_, acc = carry
                ks, vj = kv_tile(j)
                new_c, pvs = [], []
                for e in range(hp):
                    z = _dot(qs[e], ks[e], 1, 1) * scale
                    lg = jnp.log(1.0 + jnp.exp(-jnp.abs(z)))
                    lz = jnp.minimum(z, 0.0) - lg
                    lk = -jnp.maximum(z, 0.0) - lg
                    if diag:
                        lk = jnp.where(rows > cols, lk, 0.0)
                    a = jnp.exp(lz + _cum(lk, tri) + cs_[e])
                    if diag:
                        a = jnp.where(rows > cols, a, 0.0)
                    pvs.append(_dot(a.astype(BF16), vj, 1, 0))
                    new_c.append(cs_[e] + jnp.sum(lk, axis=1, keepdims=True))
                return tuple(new_c), acc + merge(pvs)

            init = (tuple(jnp.zeros((tq, 1), F32) for _ in range(hp)), jnp.zeros((tq, LANES), F32))
            carry = block(i, init, True)
            cs_, acc = lax.fori_loop(0, i, lambda jj, c: block(i - 1 - jj, c, False), carry)
            o_ref[0] = acc
            st_ref[0, 0] = merge(list(cs_))

    outs, c_outs = _call(
        body, name=name, grid=(B, nh, nq),
        in_specs=[pl.BlockSpec((1, tq, qw), lambda b, h, i: (b, i, qc + h)),
                  pl.BlockSpec((1, Sk, qw), lambda b, h, i: (b, 0, kc + h)),
                  pl.BlockSpec((1, Sk, LANES), lambda b, h, i: (b, 0, vc + h))],
        out_specs=(pl.BlockSpec((1, tq, LANES), lambda b, h, i: (b, i, h)),
                   pl.BlockSpec((1, 1, tq, LANES), lambda b, h, i: (b, h, i, 0))),
        out_shape=(jax.ShapeDtypeStruct((B, S, nh * LANES), F32), jax.ShapeDtypeStruct((B, nh, S, LANES), F32)),
        scratch_shapes=[], args=(q, k, v), sem=("parallel", "parallel", "arbitrary"), comm=comm)
    return (*outs, c_outs) if comm is not None else tuple(outs)


def attn_bwd(q, k, v, o, st, do, *, name, mode, causal, hp, qk_w, v_w, scale, nh, qc, kc, vc, comm=None):
    qw = _att_dims(mode, hp, qk_w, v_w)
    B, S = q.shape[0], q.shape[1]
    Sk = k.shape[1]
    tq = _pick(S, ATT_TILE)
    tk = _pick(Sk, ATT_TILE)
    if causal:
        assert tq == tk and S == Sk
    nq, nkv = S // tq, Sk // tk

    def body(q_ref, k_ref, v_ref, o_ref, st_ref, do_ref, dq_ref, dk_ref, dv_ref):
        i = pl.program_id(2)

        @pl.when(i == 0)
        def _():
            dk_ref[...] = jnp.zeros_like(dk_ref)
            dv_ref[...] = jnp.zeros_like(dv_ref)

        qv = q_ref[0].astype(BF16)
        dov = do_ref[0]
        stv = st_ref[0, 0]
        masks_qk = _lane_masks(hp, qk_w) if qk_w < LANES else None
        masks_v = _lane_masks(hp, v_w) if v_w < LANES else None
        if qk_w == LANES:
            qs = [qv[:, e * LANES:(e + 1) * LANES] for e in range(hp)]
        else:
            qs = [jnp.where(masks_qk[e], qv, jnp.zeros_like(qv)) for e in range(hp)]
        if hp == 1:
            dos = [dov.astype(BF16)]
            stats = [stv[:, 0:1]]
        else:
            dos = [jnp.where(masks_v[e], dov, 0.0).astype(BF16) for e in range(hp)]
            stats = [stv[:, e * v_w:e * v_w + 1] for e in range(hp)]
        if mode == "softmax":
            prod = dov * o_ref[0]
            if hp == 1:
                dsum = [jnp.sum(prod, axis=1, keepdims=True)]
            else:
                dsum = [jnp.sum(jnp.where(masks_v[e], prod, 0.0), axis=1, keepdims=True) for e in range(hp)]
        rows = lax.broadcasted_iota(jnp.int32, (tq, tk), 0)
        cols = lax.broadcasted_iota(jnp.int32, (tq, tk), 1)
        if mode == "sb":
            tri_le, tri_lt = _tri(tk, "le"), _tri(tk, "lt")

        def kv_tile(j):
            off = pl.multiple_of(j * tk, tk)
            kj = k_ref[0, pl.ds(off, tk), :].astype(BF16)
            vj = v_ref[0, pl.ds(off, tk), :].astype(BF16)
            ks = [kj[:, e * LANES:(e + 1) * LANES] for e in range(hp)] if qk_w == LANES else [kj] * hp
            return off, ks, vj

        def scatter(off, dz_list, p_list):
            dvj = None
            for e in range(hp):
                t = _dot(p_list[e], dos[e], 0, 0)
                dvj = t if dvj is None else dvj + t
            dv_ref[0, pl.ds(off, tk), :] += dvj
            if qk_w == LANES:
                for e in range(hp):
                    dk_ref[0, pl.ds(off, tk), e * LANES:(e + 1) * LANES] += _dot(dz_list[e], qs[e], 0, 0)
            else:
                dkj = None
                for e in range(hp):
                    t = _dot(dz_list[e], qs[e], 0, 0)
                    dkj = t if dkj is None else dkj + t
                dk_ref[0, pl.ds(off, tk), :] += dkj

        def dq_add(dqs, dz_list, ks):
            out = []
            for e in range(hp):
                out.append(dqs[e] + _dot(dz_list[e], ks[e], 1, 0))
            return tuple(out)

        dq0 = tuple(jnp.zeros((tq, LANES), F32) for _ in range(hp))

        if mode == "softmax":
            def block(j, dqs, diag):
                off, ks, vj = kv_tile(j)
                dzs, ps = [], []
                for e in range(hp):
                    s = _dot(qs[e], ks[e], 1, 1) * scale
                    p = jnp.exp(s - stats[e])
                    if diag:
                        p = jnp.where(rows >= cols, p, 0.0)
                    dp = _dot(dos[e], vj, 1, 1)
                    dzs.append((p * (dp - dsum[e]) * scale).astype(BF16))
                    ps.append(p.astype(BF16))
                scatter(off, dzs, ps)
                return dq_add(dqs, dzs, ks)

            if causal:
                dqs = lax.fori_loop(0, i, lambda j, c: block(j, c, False), dq0)
                dqs = block(i, dqs, True)
            else:
                dqs = lax.fori_loop(0, nkv, lambda j, c: block(j, c, False), dq0)
        else:
            def block(j, carry, diag):
                dqs, cps, cgs = carry
                off, ks, vj = kv_tile(j)
                dzs, ps, new_p, new_g = [], [], [], []
                for e in range(hp):
                    z = _dot(qs[e], ks[e], 1, 1) * scale
                    ez = jnp.exp(-jnp.abs(z))
                    lg = jnp.log(1.0 + ez)
                    lz = jnp.minimum(z, 0.0) - lg
                    lk = -jnp.maximum(z, 0.0) - lg
                    if diag:
                        lk = jnp.where(rows > cols, lk, 0.0)
                    keep_after = stats[e] - cps[e] - _cum(lk, tri_le)
                    a = jnp.exp(lz + keep_after)
                    if diag:
                        a = jnp.where(rows > cols, a, 0.0)
                    g = _dot(dos[e], vj, 1, 1) * a
                    gsum = cgs[e] + _cum(g, tri_lt)
                    inv = 1.0 / (1.0 + ez)
                    sig = jnp.where(z >= 0.0, inv, ez * inv)
                    dz = (g * (1.0 - sig) - gsum * sig) * scale
                    if diag:
                        dz = jnp.where(rows > cols, dz, 0.0)
                    dzs.append(dz.astype(BF16))
                    ps.append(a.astype(BF16))
                    new_p.append(cps[e] + jnp.sum(lk, axis=1, keepdims=True))
                    new_g.append(cgs[e] + jnp.sum(g, axis=1, keepdims=True))
                scatter(off, dzs, ps)
                return dq_add(dqs, dzs, ks), tuple(new_p), tuple(new_g)

            zc = tuple(jnp.zeros((tq, 1), F32) for _ in range(hp))
            carry = lax.fori_loop(0, i, lambda j, c: block(j, c, False), (dq0, zc, zc))
            dqs, _, _ = block(i, carry, True)

        if qk_w == LANES:
            for e in range(hp):
                dq_ref[0, :, e * LANES:(e + 1) * LANES] = dqs[e]
        else:
            dq_ref[0] = jnp.where(masks_qk[0], dqs[0], dqs[1])

    outs, c_outs = _call(
        body, name=name, grid=(B, nh, nq),
        in_specs=[pl.BlockSpec((1, tq, qw), lambda b, h, i: (b, i, qc + h)),
                  pl.BlockSpec((1, Sk, qw), lambda b, h, i: (b, 0, kc + h)),
                  pl.BlockSpec((1, Sk, LANES), lambda b, h, i: (b, 0, vc + h)),
                  pl.BlockSpec((1, tq, LANES), lambda b, h, i: (b, i, h)),
                  pl.BlockSpec((1, 1, tq, LANES), lambda b, h, i: (b, h, i, 0)),
                  pl.BlockSpec((1, tq, LANES), lambda b, h, i: (b, i, h))],
        out_specs=(pl.BlockSpec((1, tq, qw), lambda b, h, i: (b, i, h)),
                   pl.BlockSpec((1, Sk, qw), lambda b, h, i: (b, 0, h)),
                   pl.BlockSpec((1, Sk, LANES), lambda b, h, i: (b, 0, h))),
        out_shape=(jax.ShapeDtypeStruct((B, S, nh * qw), F32), jax.ShapeDtypeStruct((B, Sk, nh * qw), F32),
                   jax.ShapeDtypeStruct((B, Sk, nh * LANES), F32)),
        scratch_shapes=[], args=(q, k, v, o, st, do), sem=("parallel", "parallel", "arbitrary"), comm=comm)
    return (*outs, c_outs) if comm is not None else tuple(outs)


def _pos(p):
    return 4 * p[0] + 2 * p[1] + p[2]


N_CHIP = 4
MESH_ID = pl.DeviceIdType.MESH
_ANY = pl.BlockSpec(memory_space=pl.ANY)


def _me():
    return lax.axis_index("x"), lax.axis_index("y"), lax.axis_index("c")


def _other_chips(x, y):
    return [(1 - x, y), (x, 1 - y), (1 - x, 1 - y)]


class _Comm:
    def __init__(self, arrs, out_shapes, n_sem, n_local, start, finish):
        self.arrs, self.out_shapes, self.start, self.finish = list(arrs), list(out_shapes), start, finish
        self.scratch = [pltpu.SemaphoreType.DMA((n_sem,)), pltpu.SemaphoreType.DMA((n_sem,)),
                        pltpu.SemaphoreType.DMA((max(n_local, 1),))]


def _run_comm(comm, name):
    n = len(comm.arrs)

    def body(*refs):
        r = (refs[:n], refs[n:2 * n], refs[2 * n], refs[2 * n + 1], refs[2 * n + 2])
        comm.start(*r)
        comm.finish(*r)

    return pl.pallas_call(
        body, name=name, in_specs=[_ANY] * n, out_specs=[_ANY] * n, out_shape=comm.out_shapes,
        scratch_shapes=comm.scratch, compiler_params=pltpu.CompilerParams(has_side_effects=True),
    )(*comm.arrs)


def _call(body, *, name, grid, in_specs, out_specs, out_shape, scratch_shapes, args, sem, comm=None):
    in_specs, out_specs, out_shape = list(in_specs), list(out_specs), list(out_shape)
    if comm is None:
        res = pl.pallas_call(body, name=name, grid=grid, in_specs=in_specs, out_specs=out_specs, out_shape=out_shape,
                             scratch_shapes=scratch_shapes, compiler_params=_cparams(sem))(*args)
        return list(res), []
    n_in, n_out, n_scr, nc = len(in_specs), len(out_specs), len(scratch_shapes), len(comm.arrs)

    def wrapped(*refs):
        ins, refs = refs[:n_in], refs[n_in:]
        c_in, refs = refs[:nc], refs[nc:]
        outs, refs = refs[:n_out], refs[n_out:]
        c_out, refs = refs[:nc], refs[nc:]
        scr, sems = refs[:n_scr], refs[n_scr:]
        ids = [pl.program_id(a) for a in range(len(grid))]
        first = functools.reduce(jnp.logical_and, [i == 0 for i in ids])
        last = functools.reduce(jnp.logical_and, [i == g - 1 for i, g in zip(ids, grid)])

        @pl.when(first)
        def _():
            comm.start(c_in, c_out, *sems)

        body(*ins, *outs, *scr)

        @pl.when(last)
        def _():
            comm.finish(c_in, c_out, *sems)

    res = pl.pallas_call(
        wrapped, name=name, grid=grid, in_specs=in_specs + [_ANY] * nc, out_specs=out_specs + [_ANY] * nc,
        out_shape=out_shape + comm.out_shapes, scratch_shapes=list(scratch_shapes) + comm.scratch,
        compiler_params=pltpu.CompilerParams(dimension_semantics=("arbitrary",) * len(grid),
                                             vmem_limit_bytes=VMEM_LIMIT, has_side_effects=True),
    )(*args, *comm.arrs)
    return list(res[:n_out]), list(res[n_out:])


def gather_two_level(bufs, *, name=None):
    n = len(bufs)

    def parts(x_refs, out_refs, send_sems, recv_sems, local_sems):
        x, y, c = _me()
        me, sibling = (x, y, c), (x, y, 1 - c)
        chips = _other_chips(x, y)

        def copy(a, k, block, to, from_input=False):
            dst = out_refs[a].at[_pos(block)]
            return pltpu.make_async_remote_copy(src_ref=x_refs[a] if from_input else dst, dst_ref=dst,
                                                send_sem=send_sems.at[7 * a + k], recv_sem=recv_sems.at[7 * a + k],
                                                device_id=to, device_id_type=MESH_ID)

        mine = [pltpu.make_async_copy(x_refs[a], out_refs[a].at[_pos(me)], local_sems.at[a]) for a in range(n)]
        first = []
        for a in range(n):
            first.append(copy(a, 0, me, sibling, from_input=True))
            first += [copy(a, 1 + j, me, (*chip, c), from_input=True) for j, chip in enumerate(chips)]
        return copy, mine, first, me, sibling, chips, c

    def start(*refs):
        _, mine, first, *_ = parts(*refs)
        for cp in mine + first:
            cp.start()

    def finish(*refs):
        copy, mine, first, me, sibling, chips, c = parts(*refs)
        passed = []
        for j, chip in enumerate(chips):
            for a in range(n):
                copy(a, 1 + j, (*chip, c), me).wait_recv()
                passed.append(copy(a, 4 + j, (*chip, c), sibling))
                passed[-1].start()
        for a in range(n):
            copy(a, 0, sibling, me).wait_recv()
            for j, chip in enumerate(chips):
                copy(a, 4 + j, (*chip, 1 - c), me).wait_recv()
        for cp in first + passed:
            cp.wait_send()
        for cp in mine:
            cp.wait()

    out_shapes = [jax.ShapeDtypeStruct((N_DEV,) + b.shape, b.dtype) for b in bufs]
    comm = _Comm(bufs, out_shapes, 7 * n, n, start, finish)
    return _run_comm(comm, name) if name else comm


def sibling_exchange(sends, *, name=None):
    n = len(sends)

    def copies(s_refs, out_refs, send_sems, recv_sems, local_sems):
        x, y, c = _me()
        return [pltpu.make_async_remote_copy(src_ref=s_refs[a].at[1 - c], dst_ref=out_refs[a], send_sem=send_sems.at[a],
                                             recv_sem=recv_sems.at[a], device_id=(x, y, 1 - c), device_id_type=MESH_ID)
                for a in range(n)]

    def start(*refs):
        for cp in copies(*refs):
            cp.start()

    def finish(*refs):
        for cp in copies(*refs):
            cp.wait()

    out_shapes = [jax.ShapeDtypeStruct(s.shape[1:], s.dtype) for s in sends]
    comm = _Comm(sends, out_shapes, n, 0, start, finish)
    return _run_comm(comm, name) if name else comm


def _flat2(shape):
    return math.prod(shape[:-1]), shape[-1]


def _row_tile(r):
    return r if r <= 512 else _pick8(r, 256)


def _pick8(n, pref):
    t = pref
    while n % t:
        t -= 8
    return t


def chip_sum(send, got, core, *, name):
    shape = got.shape[1:]
    r, cdim = _flat2(shape)
    tr = _row_tile(r)
    send = send.reshape(2, N_CHIP, r, cdim)
    got = got.reshape(N_CHIP, r, cdim)

    def body(core_ref, s_ref, g_ref, o_ref):
        o_ref[...] = (s_ref[0] + g_ref[...]).astype(o_ref.dtype)

    blk = pl.BlockSpec((N_CHIP, tr, cdim), lambda i, core_ref: (0, i, 0))
    out = pl.pallas_call(
        body, name=name,
        grid_spec=pltpu.PrefetchScalarGridSpec(
            num_scalar_prefetch=1, grid=(r // tr,),
            in_specs=[pl.BlockSpec((1, N_CHIP, tr, cdim), lambda i, core_ref: (core_ref[0], 0, i, 0)), blk],
            out_specs=blk),
        out_shape=jax.ShapeDtypeStruct((N_CHIP, r, cdim), BF16),
        compiler_params=_cparams(("parallel",)),
    )(core, send, got)
    return out.reshape((N_CHIP,) + shape)


def chip_exchange(sums, *, name=None):
    n = len(sums)

    def copies(s_refs, out_refs, send_sems, recv_sems, local_sems):
        x, y, c = _me()
        mine = 2 * x + y
        chips = _other_chips(x, y)
        local = [pltpu.make_async_copy(s_refs[a].at[mine], out_refs[a].at[mine], local_sems.at[a]) for a in range(n)]
        sends, recvs = [], []
        for a in range(n):
            for j, (px, py) in enumerate(chips):
                sems = dict(send_sem=send_sems.at[3 * a + j], recv_sem=recv_sems.at[3 * a + j], device_id=(px, py, c),
                            device_id_type=MESH_ID)
                sends.append(pltpu.make_async_remote_copy(src_ref=s_refs[a].at[2 * px + py], dst_ref=out_refs[a].at[mine], **sems))
                recvs.append(pltpu.make_async_remote_copy(src_ref=s_refs[a].at[mine], dst_ref=out_refs[a].at[2 * px + py], **sems))
        return local, sends, recvs

    def start(*refs):
        local, sends, _ = copies(*refs)
        for cp in local + sends:
            cp.start()

    def finish(*refs):
        local, sends, recvs = copies(*refs)
        for cp in recvs:
            cp.wait_recv()
        for cp in sends:
            cp.wait_send()
        for cp in local:
            cp.wait()

    out_shapes = [jax.ShapeDtypeStruct(s.shape, s.dtype) for s in sums]
    comm = _Comm(sums, out_shapes, 3 * n, n, start, finish)
    return _run_comm(comm, name) if name else comm


def sum_adamw(parts, w, m, v, *, name):
    shape = w.shape
    R, cdim = _flat2(shape)
    n_slots = parts.shape[0]
    tr = _row_tile(R)
    parts = parts.reshape(n_slots, R, cdim)
    w, m, v = (t.reshape(R, cdim) for t in (w, m, v))
    c1 = 1.0 - ADAM_B1 ** ADAM_STEP
    c2 = 1.0 - ADAM_B2 ** ADAM_STEP

    def body(p_ref, w_ref, m_ref, v_ref, g_ref, d_ref, mo_ref, vo_ref):
        g = p_ref[0].astype(F32)
        for s in range(1, n_slots):
            g = g + p_ref[s].astype(F32)
        mn = ADAM_B1 * m_ref[...] + (1.0 - ADAM_B1) * g
        vn = ADAM_B2 * v_ref[...] + (1.0 - ADAM_B2) * jnp.square(g)
        g_ref[...] = g
        mo_ref[...] = mn
        vo_ref[...] = vn
        d_ref[...] = -ADAM_LR * ((mn / c1) / (jnp.sqrt(vn / c2) + ADAM_EPS) + ADAM_WD * w_ref[...])

    rs = pl.BlockSpec((tr, cdim), lambda i: (i, 0))
    sd = jax.ShapeDtypeStruct((R, cdim), F32)
    res = pl.pallas_call(
        body, name=name, grid=(R // tr,),
        in_specs=[pl.BlockSpec((n_slots, tr, cdim), lambda i: (0, i, 0)), rs, rs, rs],
        out_specs=(rs, rs, rs, rs), out_shape=(sd, sd, sd, sd),
        compiler_params=_cparams(("parallel",)),
    )(parts, w, m, v)
    return [t.reshape(shape) for t in res]


WEIGHTS = ("ln_mix_pre", "w_in", "b_gate", "q_norm", "w_uq", "kv_norm", "w_uk", "w_uv", "mem_norm", "w_mem_kv",
           "w_branch_out", "w_out", "ln_mix_post", "ln_mlp_pre", "w_mlp_up", "w_mlp_down", "ln_mlp_post")
BIG = dict(w_in=((D_MODEL, IN_WIDTH), 1), w_uq=((Q_LORA, MLA_HEADS * (MLA_NOPE + MLA_ROPE)), 1),
           w_uk=((KV_LORA, MLA_HEADS * MLA_NOPE), 1), w_uv=((KV_LORA, MLA_HEADS * MLA_V), 1),
           w_mem_kv=((D_MODEL, 2 * QM_W), 0), w_branch_out=((N_BRANCH, BRANCH_W, D_MODEL), 2),
           w_out=((D_MODEL, D_MODEL), 0), w_mlp_up=((D_MODEL, D_FF), 1), w_mlp_down=((D_FF, D_MODEL), 0))
SMALL = tuple(n for n in WEIGHTS if n not in BIG)


def _shard_shape(name):
    full, ax = BIG[name]
    return tuple(s // N_DEV if a == ax else s for a, s in enumerate(full))


PACK_TILE = 512


def _pad_rows(a):
    r = a.shape[-2]
    to = PACK_TILE if r > PACK_TILE else 8
    pad = [(0, 0)] * a.ndim
    pad[-2] = (0, (-r) % to)
    return jnp.pad(a, pad)


def _pack_rows(arrs):
    return _pad_rows(jnp.concatenate([a.reshape(-1, LANES) for a in arrs], axis=0))


def _unpack_rows(packed, shapes, lead=()):
    out, r = [], 0
    for shp in shapes:
        n = math.prod(shp) // LANES
        out.append(packed[..., r:r + n, :].reshape(lead + tuple(shp)))
        r += n
    return out


def _slots_to_full(name, slots):
    full, ax = BIG[name]
    return jnp.moveaxis(slots, 0, ax).reshape(full)


def _full_to_slots(name, w):
    full, ax = BIG[name]
    split = full[:ax] + (N_DEV, full[ax] // N_DEV) + full[ax + 1:]
    return jnp.moveaxis(w.reshape(split), ax, 0)


def _full_to_owner(name, w):
    s = _full_to_slots(name, w)
    return jnp.swapaxes(s.reshape((N_CHIP, 2) + s.shape[1:]), 0, 1)


def _split_in(w):
    offs = np.cumsum((0,) + IN_SIZES)
    names = ("cq", "ckv", "kr", "sb", "qm", "gate")
    return {n: w[:, offs[i]:offs[i + 1]] for i, n in enumerate(names)}


def _pad_in(w):
    p = _split_in(w)
    p["kr"] = jnp.pad(p["kr"], ((0, 0), (0, KR_PAD - MLA_ROPE)))
    p["zpad"] = jnp.zeros((w.shape[0], PAD_SIZES["zpad"]), w.dtype)
    return jnp.concatenate([p[n] for n in PAD_ORDER], axis=1)


def _unpad_in(wp):
    p = {n: wp[:, PAD_OFF[n]:PAD_OFF[n] + PAD_SIZES[n]] for n in PAD_ORDER}
    p["kr"] = p["kr"][:, :MLA_ROPE]
    return jnp.concatenate([p[n] for n in ("cq", "ckv", "kr", "sb", "qm", "gate")], axis=1)


def _pad_heads(w, width):
    r = w.shape[0]
    return jnp.pad(w.reshape(r, MLA_HEADS, width), ((0, 0), (0, 0), (0, HEAD_PAD - width))).reshape(r, MLA_PAD_W)


def _unpad_heads(wp, width):
    r = wp.shape[0]
    return wp.reshape(r, MLA_HEADS, HEAD_PAD)[:, :, :width].reshape(r, MLA_HEADS * width)


def _rope_inv_freq():
    half = MLA_ROPE // 2
    inv = 1.0 / (ROPE_THETA ** (jnp.arange(half, dtype=F32) * (2.0 / MLA_ROPE)))
    tab = jnp.zeros((LANES,), F32)
    tab = tab.at[MLA_NOPE:MLA_NOPE + half].set(inv).at[MLA_NOPE + half:MLA_NOPE + MLA_ROPE].set(inv)
    return tab.reshape(1, LANES)


def _as_tuple(r):
    return r if isinstance(r, tuple) else (r,)


class _Exchange:
    def __init__(self, w):
        self.w = w
        self.rest = tuple(n for n in BIG if n != "w_in")
        self.pending = {}
        self.reduced = {}

    def first_weights(self):
        (s,) = gather_two_level([self.w["w_in"].astype(BF16)], name="gather_w_in")
        return _slots_to_full("w_in", s)

    def rest_comm(self):
        return gather_two_level([self.w[n].astype(BF16) for n in self.rest])

    def rest_weights(self, slots):
        return {n: s if n == "w_mlp_up" else _slots_to_full(n, s) for n, s in zip(self.rest, slots)}

    def reduce_start(self, tag, grads):
        names = tuple(grads)
        sends = [grads[n] for n in names]
        gots = sibling_exchange(sends, name=f"sibling_{tag}")
        core = lax.axis_index("c").astype(jnp.int32).reshape(1)
        sums = [chip_sum(s, g, core, name=f"chip_sum_{n}") for n, s, g in zip(names, sends, gots)]
        self.pending[tag] = names
        return chip_exchange(sums)

    def reduce_done(self, tag, recvs):
        self.reduced.update(zip(self.pending.pop(tag), recvs))


def _local_step(x, mem, positions, tgt, sm, ex):
    B, S, D = x.shape
    M = mem.shape[1]
    T = B * S
    x2 = x.reshape(T, D)
    mem2 = mem.reshape(B * M, D)
    pos = positions.reshape(T, 1).astype(F32)
    invf = _rope_inv_freq()
    w_in_pad = _pad_in(ex.first_weights())
    cq_col, ckv_col = PAD_OFF["cq"] // Q_LORA, PAD_OFF["ckv"] // KV_LORA
    sb_col, qm_col = PAD_OFF["sb"] // LANES, PAD_OFF["qm"] // LANES
    sb_blk = SB_HEADS * SB_DIM // LANES
    mla = dict(mode="softmax", causal=True, hp=2, qk_w=128, v_w=64, scale=(MLA_NOPE + MLA_ROPE) ** -0.5,
               nh=MLA_HEADS // 2, qc=0, kc=0, vc=0)
    sbk = dict(mode="sb", causal=True, hp=2, qk_w=64, v_w=64, scale=SB_DIM ** -0.5, nh=SB_HEADS // 2,
               qc=sb_col, kc=sb_col + sb_blk, vc=sb_col + 2 * sb_blk)
    mca = dict(mode="softmax", causal=False, hp=1, qk_w=128, v_w=128, scale=MEM_DIM ** -0.5, nh=MEM_HEADS,
               qc=qm_col, kc=0, vc=MEM_HEADS)

    h = rms_fwd(x2, sm["ln_mix_pre"], name="rms_mix_pre")
    proj = matmul(h, w_in_pad, name="mm_in")
    proj3 = proj.reshape(B, S, IN_PAD)
    o_sb, st_sb, *slots = attn_fwd(proj3, proj3, proj3, name="sb_fwd", comm=ex.rest_comm(), **sbk)
    W = ex.rest_weights(slots[0] if slots else None)
    w_uq_pad = _pad_heads(W["w_uq"], MLA_NOPE + MLA_ROPE)
    w_uk_pad = _pad_heads(W["w_uk"], MLA_NOPE)
    cqn = rms_fwd(proj, sm["q_norm"], n=Q_LORA, col=cq_col, name="rms_q")
    ckvn = rms_fwd(proj, sm["kv_norm"], n=KV_LORA, col=ckv_col, name="rms_kv")
    qpre = matmul(cqn, w_uq_pad, name="mm_uq")
    kpre = matmul(ckvn, w_uk_pad, name="mm_uk")
    v_mla = matmul(ckvn, W["w_uv"], out_dtype=BF16, name="mm_uv").reshape(B, S, -1)
    q_pad, k_pad = rope_fwd(qpre, kpre, proj, pos, invf, name="rope_fwd")
    q_pad, k_pad = q_pad.reshape(B, S, -1), k_pad.reshape(B, S, -1)
    o_mla, st_mla = attn_fwd(q_pad, k_pad, v_mla, name="mla_fwd", **mla)
    memh = rms_fwd(mem2, sm["mem_norm"], name="rms_mem")
    mkv = matmul(memh, W["w_mem_kv"], out_dtype=BF16, name="mm_memkv").reshape(B, M, -1)
    o_mem, st_mem = attn_fwd(proj3, mkv, mkv, name="mem_fwd", **mca)
    outs = [o.reshape(T, BRANCH_W) for o in (o_mla, o_sb, o_mem)]
    ps = [matmul(o, W["w_branch_out"][b], name=f"mm_bo{b}") for b, o in enumerate(outs)]
    merged = gate_merge(proj, sm["b_gate"], ps, name="gate_merge")
    y = matmul(merged, W["w_out"], name="mm_out")
    x1, h2 = mix_post_fwd(x2, y, sm["ln_mix_post"], sm["ln_mlp_pre"], name="mix_post")
    u = matmul(h2, W["w_mlp_up"], b_slots="n", act="relu2", out_dtype=BF16, name="mm_up")
    d = matmul(u, W["w_mlp_down"], name="mm_down")
    loss_p, dx2, dd, dg_mlp_post = loss_head(x1, d, sm["ln_mlp_post"], tgt.reshape(T, D), name="loss_head")

    da = matmul(dd, W["w_mlp_down"], tb=True, act="relu2_bwd", act_in=u, out_dtype=BF16, name="mm_down_dx")
    g_down = matmul(u, dd, ta=True, out_slots="m", name="mm_down_dw")
    dh2 = matmul(da, W["w_mlp_up"], tb=True, b_slots="k", name="mm_up_dx")
    g_up = matmul(h2, da, ta=True, out_slots="n", name="mm_up_dw")
    red_mlp = ex.reduce_start("mlp", dict(w_mlp_down=g_down, w_mlp_up=g_up))
    dx1, dy, dg_mlp_pre, dg_mix_post = mlp_pre_bwd(dx2, dh2, x1, sm["ln_mlp_pre"], y, sm["ln_mix_post"], name="mlp_pre_bwd")
    dmerged = matmul(dy, W["w_out"], tb=True, name="mm_out_dx")
    g_out = matmul(merged, dy, ta=True, name="mm_out_dw")
    dlog, dp0, dp1, dp2, db_gate = gate_bwd(dmerged, proj, sm["b_gate"], ps, name="gate_bwd")
    dps = (dp0, dp1, dp2)
    dos = [matmul(dps[b], W["w_branch_out"][b], tb=True, name=f"mm_bo{b}_dx").reshape(B, S, BRANCH_W) for b in range(N_BRANCH)]
    g_bo = jnp.stack([matmul(outs[b], dps[b], ta=True, name=f"mm_bo{b}_dw") for b in range(N_BRANCH)])
    red_mix = ex.reduce_start("mix", dict(w_out=_full_to_owner("w_out", g_out),
                                          w_branch_out=_full_to_owner("w_branch_out", g_bo)))
    dq_pad, dk_pad, dv_mla, *got = attn_bwd(q_pad, k_pad, v_mla, o_mla, st_mla, dos[0], name="mla_bwd", comm=red_mlp, **mla)
    ex.reduce_done("mlp", got[0] if got else None)
    dsq, dsk, dsv, *got = attn_bwd(proj3, proj3, proj3, o_sb, st_sb, dos[1], name="sb_bwd", comm=red_mix, **sbk)
    ex.reduce_done("mix", got[0] if got else None)
    dqm, dmk, dmv = attn_bwd(proj3, mkv, mkv, o_mem, st_mem, dos[2], name="mem_bwd", **mca)
    dmkv = jnp.concatenate([dmk, dmv], axis=-1).astype(BF16).reshape(B * M, -1)
    dmemh = matmul(dmkv, W["w_mem_kv"], tb=True, name="mm_memkv_dx")
    g_memkv = matmul(memh, dmkv, ta=True, name="mm_memkv_dw")
    _, dg_mem_norm = rms_bwd(dmemh, mem2, sm["mem_norm"], name="rms_mem_bwd")
    dq_pad, dk_pad, dv_mla = dq_pad.reshape(T, -1), dk_pad.reshape(T, -1), dv_mla.reshape(T, -1)
    dqpre, dkr = rope_bwd(dq_pad, dk_pad, pos, invf, name="rope_bwd")
    dcqn = matmul(dqpre, w_uq_pad, tb=True, name="mm_uq_dx")
    g_uq = _unpad_heads(matmul(cqn, dqpre, ta=True, name="mm_uq_dw"), MLA_NOPE + MLA_ROPE)
    dckvn = matmul(dk_pad, w_uk_pad, tb=True, name="mm_uk_dx")
    dckvn = matmul(dv_mla, W["w_uv"], tb=True, acc=dckvn, name="mm_uv_dx")
    g_uk = _unpad_heads(matmul(ckvn, dk_pad, ta=True, name="mm_uk_dw"), MLA_NOPE)
    g_uv = matmul(ckvn, dv_mla, ta=True, name="mm_uv_dw")
    dcq, dg_q_norm = rms_bwd(dcqn, proj, sm["q_norm"], n=Q_LORA, col=cq_col, out_dtype=BF16, name="rms_q_bwd")
    dckv, dg_kv_norm = rms_bwd(dckvn, proj, sm["kv_norm"], n=KV_LORA, col=ckv_col, out_dtype=BF16, name="rms_kv_bwd")
    pieces = dict(gate=dlog, sb=jnp.concatenate([dsq, dsk, dsv], axis=-1).reshape(T, -1), qm=dqm.reshape(T, -1),
                  ckv=dckv, cq=dcq, kr=dkr, zpad=jnp.zeros((T, PAD_SIZES["zpad"]), BF16))
    dproj = jnp.concatenate([pieces[n].astype(BF16) for n in PAD_ORDER], axis=1)
    g_in = _unpad_in(matmul(h, dproj, ta=True, name="mm_in_dw"))
    last = dict(w_in=g_in, w_uq=g_uq, w_uk=g_uk, w_uv=g_uv, w_mem_kv=g_memkv)
    red_in = ex.reduce_start("in", {n: _full_to_owner(n, g) for n, g in last.items()})
    dh, *got = _as_tuple(matmul(dproj, w_in_pad, tb=True, comm=red_in, name="mm_in_dx"))
    ex.reduce_done("in", got[0] if got else None)
    dx, dg_mix_pre = rms_bwd(dh, x2, sm["ln_mix_pre"], residual=dx1, name="rms_mix_pre_bwd")

    small = dict(ln_mix_pre=dg_mix_pre, b_gate=db_gate, q_norm=dg_q_norm, kv_norm=dg_kv_norm, mem_norm=dg_mem_norm,
                 ln_mix_post=dg_mix_post, ln_mlp_pre=dg_mlp_pre, ln_mlp_post=dg_mlp_post)
    return loss_p[0, 0], dx.reshape(B, S, D), small


def kernel(x, mem, positions, ln_mix_pre, w_in, b_gate, q_norm, w_uq, kv_norm, w_uk, w_uv, mem_norm, w_mem_kv, w_branch_out, w_out, ln_mix_post, ln_mlp_pre, w_mlp_up, w_mlp_down, ln_mlp_post, loss_target, m_ln_mix_pre, m_w_in, m_b_gate, m_q_norm, m_w_uq, m_kv_norm, m_w_uk, m_w_uv, m_mem_norm, m_w_mem_kv, m_w_branch_out, m_w_out, m_ln_mix_post, m_ln_mlp_pre, m_w_mlp_up, m_w_mlp_down, m_ln_mlp_post, v_ln_mix_pre, v_w_in, v_b_gate, v_q_norm, v_w_uq, v_kv_norm, v_w_uk, v_w_uv, v_mem_norm, v_w_mem_kv, v_w_branch_out, v_w_out, v_ln_mix_post, v_ln_mlp_pre, v_w_mlp_up, v_w_mlp_down, v_ln_mlp_post):
    given = dict(locals())
    w = {n: given[n][0] for n in WEIGHTS}
    m = {n: given["m_" + n][0] for n in WEIGHTS}
    v = {n: given["v_" + n][0] for n in WEIGHTS}
    sm = {n: w[n].reshape(1, -1) for n in SMALL}
    ex = _Exchange(w)
    loss_part, grad_x, g_small = _local_step(x, mem, positions, loss_target, sm, ex)
    loss = lax.psum(loss_part, ("x", "y", "c"))

    res = {n: sum_adamw(ex.reduced[n], w[n], m[n], v[n], name=f"adamw_{n}") for n in BIG}

    small_shapes = [w[n].shape for n in SMALL]
    parts = gather_two_level([_pack_rows([g_small[n] for n in SMALL])], name="gather_small_grads")[0]
    res_small = sum_adamw(parts, *[_pack_rows([t[n] for n in SMALL]) for t in (w, m, v)], name="adamw_replicated")
    res_small = [_unpack_rows(r, small_shapes) for r in res_small]
    for i, n in enumerate(SMALL):
        res[n] = [r[i] for r in res_small]

    out = [loss, grad_x]
    for k in range(4):
        out += [res[n][k][None] for n in WEIGHTS]
    return tuple(out)
```

```python
import functools
import math

import numpy as np
import jax
import jax.numpy as jnp
from jax import lax
from jax.experimental import pallas as pl
from jax.experimental.pallas import tpu as pltpu

F32 = jnp.float32
BF16 = jnp.bfloat16

D_MODEL = 1024
MEM_HEADS, MEM_DIM = 4, 128
MLA_HEADS, MLA_NOPE, MLA_ROPE, MLA_V = 8, 64, 32, 64
Q_LORA, KV_LORA = 384, 256
ROPE_THETA = 10000.0
SB_HEADS, SB_DIM = 8, 64
D_FF = 4 * D_MODEL
N_BRANCH, BRANCH_W = 3, 512
EPS = 1e-6
SB_W = 3 * SB_HEADS * SB_DIM
QM_W = MEM_HEADS * MEM_DIM
GATE_W = N_BRANCH * D_MODEL
IN_SIZES = (Q_LORA, KV_LORA, MLA_ROPE, SB_W, QM_W, GATE_W)
IN_WIDTH = sum(IN_SIZES)
ADAM_LR, ADAM_B1, ADAM_B2, ADAM_EPS, ADAM_WD, ADAM_STEP = 0.001, 0.9, 0.999, 1e-08, 0.01, 10

N_DEV = 8
LANES = 128
KR_PAD = LANES
PAD_ORDER = ("gate", "cq", "kr", "ckv", "sb", "qm", "zpad")
MM_TILE = 1024
PAD_SIZES = dict(gate=GATE_W, cq=Q_LORA, kr=KR_PAD, ckv=KV_LORA, sb=SB_W, qm=QM_W)
PAD_SIZES["zpad"] = (-sum(PAD_SIZES.values())) % MM_TILE
PAD_OFF = {}
_o = 0
for _n in PAD_ORDER:
    PAD_OFF[_n] = _o
    _o += PAD_SIZES[_n]
IN_PAD = _o
HEAD_PAD = LANES
MLA_PAD_W = MLA_HEADS * HEAD_PAD
VMEM_LIMIT = 48 * 1024 * 1024


def _cparams(sem):
    return pltpu.CompilerParams(dimension_semantics=sem, vmem_limit_bytes=VMEM_LIMIT)


def _pick(n, pref):
    if n <= pref:
        return n
    t = (pref // LANES) * LANES
    while t >= LANES:
        if n % t == 0:
            return t
        t -= LANES
    return n


def matmul(a, b, *, name, ta=False, tb=False, out_dtype=F32, acc=None, act=None, act_in=None,
           b_slots=None, out_slots=None, comm=None, tm_pref=MM_TILE, tn_pref=MM_TILE, tk_pref=MM_TILE):
    M, K = (a.shape[1], a.shape[0]) if ta else a.shape
    tm, tk = _pick(M, tm_pref), _pick(K, tk_pref)
    if b_slots == "n":
        assert not tb and b.shape[:2] == (N_DEV, K)
        tn = b.shape[2]
        N = N_DEV * tn
    elif b_slots == "k":
        assert tb and N_DEV * b.shape[2] == K
        N, tk = b.shape[1], b.shape[2]
        tn = _pick(N, tn_pref)
    else:
        N = b.shape[0] if tb else b.shape[1]
        assert (b.shape[1] if tb else b.shape[0]) == K
        tn = _pick(N, tn_pref)
    if out_slots == "m":
        tm = M // N_DEV
    elif out_slots == "n":
        tn = N // N_DEV
    assert out_slots is None or (acc is None and act_in is None)
    nk = K // tk
    dims = (((0 if ta else 1,), (1 if tb else 0,)), ((), ()))
    assert act in (None, "relu2", "relu2_bwd") and (act == "relu2_bwd") == (act_in is not None)

    def body(*refs):
        a_ref, b_ref = refs[0], refs[1]
        pos = 2
        acc_ref = act_ref = None
        if acc is not None:
            acc_ref = refs[pos]
            pos += 1
        if act_in is not None:
            act_ref = refs[pos]
            pos += 1
        out = refs[pos]
        scratch = refs[pos + 1:]

        part = lax.dot_general(a_ref[...].astype(BF16), b_ref[...].astype(BF16), dims,
                               preferred_element_type=F32)

        def finish(r):
            if acc_ref is not None:
                r = r + acc_ref[...]
            if act == "relu2":
                r = jnp.square(jnp.maximum(r, 0.0))
            elif act == "relu2_bwd":
                r = r * (2.0 * jnp.sqrt(act_ref[...].astype(F32)))
            out[...] = r.astype(out.dtype)

        if nk == 1:
            finish(part)
        else:
            acc_sc = scratch[0]
            k = pl.program_id(2)

            @pl.when(k == 0)
            def _():
                acc_sc[...] = part

            @pl.when(k > 0)
            def _():
                acc_sc[...] += part

            @pl.when(k == nk - 1)
            def _():
                finish(acc_sc[...])

    a_spec = pl.BlockSpec((tk, tm), lambda i, j, k: (k, i)) if ta else pl.BlockSpec((tm, tk), lambda i, j, k: (i, k))
    if b_slots == "n":
        b_spec = pl.BlockSpec((None, tk, tn), lambda i, j, k: (j, k, 0))
    elif b_slots == "k":
        b_spec = pl.BlockSpec((None, tn, tk), lambda i, j, k: (k, j, 0))
    else:
        b_spec = pl.BlockSpec((tn, tk), lambda i, j, k: (j, k)) if tb else pl.BlockSpec((tk, tn), lambda i, j, k: (k, j))
    if out_slots == "m":
        o_spec = pl.BlockSpec((None, None, tm, tn), lambda i, j, k: (i % 2, i // 2, 0, j))
        o_shape = (2, N_CHIP, tm, N)
    elif out_slots == "n":
        o_spec = pl.BlockSpec((None, None, tm, tn), lambda i, j, k: (j % 2, j // 2, i, 0))
        o_shape = (2, N_CHIP, M, tn)
    else:
        o_spec = pl.BlockSpec((tm, tn), lambda i, j, k: (i, j))
        o_shape = (M, N)
    in_specs = [a_spec, b_spec]
    args = [a, b]
    if acc is not None:
        in_specs.append(o_spec)
        args.append(acc)
    if act_in is not None:
        in_specs.append(o_spec)
        args.append(act_in)
    outs, c_outs = _call(
        body, name=name, grid=(M // tm, N // tn, nk), in_specs=in_specs, out_specs=[o_spec],
        out_shape=[jax.ShapeDtypeStruct(o_shape, out_dtype)],
        scratch_shapes=[pltpu.VMEM((tm, tn), F32)] if nk > 1 else [], args=args,
        sem=("parallel", "parallel", "arbitrary"), comm=comm)
    return (outs[0], c_outs) if comm is not None else outs[0]


ROW_TILE = 256


def _rstd(xv):
    return lax.rsqrt(jnp.mean(xv * xv, axis=-1, keepdims=True) + EPS)


def _rms_bwd_rows(dy, xv, g):
    r = _rstd(xv)
    dyg = dy * g
    dx = r * dyg - xv * (r * r * r) * jnp.mean(dyg * xv, axis=-1, keepdims=True)
    return dx, dy * xv * r


def _row_spec(tm, n, col=0):
    return pl.BlockSpec((tm, n), lambda i: (i, col))


def _vec_spec(n):
    return pl.BlockSpec((1, n), lambda i: (0, 0))


def _acc_rows(ref, i, val):
    @pl.when(i == 0)
    def _():
        ref[...] = val

    @pl.when(i > 0)
    def _():
        ref[...] += val


def rms_fwd(x, g, *, name, n=None, col=0, out_dtype=BF16):
    T = x.shape[0]
    n = x.shape[1] if n is None else n
    tm = _pick(T, ROW_TILE)

    def body(x_ref, g_ref, o_ref):
        xv = x_ref[...]
        o_ref[...] = (xv * _rstd(xv) * g_ref[...]).astype(o_ref.dtype)

    return pl.pallas_call(
        body, name=name, grid=(T // tm,), in_specs=[_row_spec(tm, n, col), _vec_spec(n)],
        out_specs=_row_spec(tm, n), out_shape=jax.ShapeDtypeStruct((T, n), out_dtype),
        compiler_params=_cparams(("parallel",)),
    )(x, g)


def rms_bwd(dy, x, g, *, name, n=None, col=0, residual=None, out_dtype=F32):
    T = x.shape[0]
    n = x.shape[1] if n is None else n
    tm = _pick(T, ROW_TILE)

    def body(*refs):
        if residual is None:
            dy_ref, x_ref, g_ref, dx_ref, dg_ref = refs
        else:
            dy_ref, x_ref, g_ref, res_ref, dx_ref, dg_ref = refs
        dx, dgr = _rms_bwd_rows(dy_ref[...].astype(F32), x_ref[...], g_ref[...])
        if residual is not None:
            dx = dx + res_ref[...]
        dx_ref[...] = dx.astype(dx_ref.dtype)
        _acc_rows(dg_ref, pl.program_id(0), jnp.sum(dgr, axis=0, keepdims=True))

    in_specs = [_row_spec(tm, n), _row_spec(tm, n, col), _vec_spec(n)]
    args = [dy, x, g]
    if residual is not None:
        in_specs.append(_row_spec(tm, n))
        args.append(residual)
    return pl.pallas_call(
        body, name=name, grid=(T // tm,), in_specs=in_specs,
        out_specs=(_row_spec(tm, n), _vec_spec(n)),
        out_shape=(jax.ShapeDtypeStruct((T, n), out_dtype), jax.ShapeDtypeStruct((1, n), F32)),
        compiler_params=_cparams(("arbitrary",)),
    )(*args)


def mix_post_fwd(x, y, g_post, g_pre2, *, name):
    T, n = x.shape
    tm = _pick(T, ROW_TILE)

    def body(x_ref, y_ref, gp_ref, g2_ref, x1_ref, h2_ref):
        yv = y_ref[...]
        x1 = x_ref[...] + yv * _rstd(yv) * gp_ref[...]
        x1_ref[...] = x1
        h2_ref[...] = (x1 * _rstd(x1) * g2_ref[...]).astype(h2_ref.dtype)

    return pl.pallas_call(
        body, name=name, grid=(T // tm,),
        in_specs=[_row_spec(tm, n), _row_spec(tm, n), _vec_spec(n), _vec_spec(n)],
        out_specs=(_row_spec(tm, n), _row_spec(tm, n)),
        out_shape=(jax.ShapeDtypeStruct((T, n), F32), jax.ShapeDtypeStruct((T, n), BF16)),
        compiler_params=_cparams(("parallel",)),
    )(x, y, g_post, g_pre2)


def loss_head(x1, d, g, tgt, *, name):
    T, n = x1.shape
    tm = _pick(T, ROW_TILE)

    def body(x1_ref, d_ref, g_ref, t_ref, loss_ref, dx2_ref, dd_ref, dg_ref):
        i = pl.program_id(0)
        dv = d_ref[...]
        gv = g_ref[...]
        err = x1_ref[...] + dv * _rstd(dv) * gv - t_ref[...]
        part = 0.5 * jnp.sum(jnp.mean(err * err, axis=-1, keepdims=True), axis=0, keepdims=True)
        _acc_rows(loss_ref, i, jnp.broadcast_to(part, (1, LANES)))
        dx2 = err * (1.0 / n)
        dx2_ref[...] = dx2
        dd, dgr = _rms_bwd_rows(dx2, dv, gv)
        dd_ref[...] = dd.astype(dd_ref.dtype)
        _acc_rows(dg_ref, i, jnp.sum(dgr, axis=0, keepdims=True))

    return pl.pallas_call(
        body, name=name, grid=(T // tm,),
        in_specs=[_row_spec(tm, n), _row_spec(tm, n), _vec_spec(n), _row_spec(tm, n)],
        out_specs=(_vec_spec(LANES), _row_spec(tm, n), _row_spec(tm, n), _vec_spec(n)),
        out_shape=(jax.ShapeDtypeStruct((1, LANES), F32), jax.ShapeDtypeStruct((T, n), F32),
                   jax.ShapeDtypeStruct((T, n), BF16), jax.ShapeDtypeStruct((1, n), F32)),
        compiler_params=_cparams(("arbitrary",)),
    )(x1, d, g, tgt)


def mlp_pre_bwd(dx2, dh2, x1, g_pre2, y, g_post, *, name):
    T, n = x1.shape
    tm = _pick(T, ROW_TILE)

    def body(dx2_ref, dh2_ref, x1_ref, g2_ref, y_ref, gp_ref, dx1_ref, dy_ref, dg2_ref, dgp_ref):
        i = pl.program_id(0)
        d1, dg2 = _rms_bwd_rows(dh2_ref[...], x1_ref[...], g2_ref[...])
        dx1 = dx2_ref[...] + d1
        dx1_ref[...] = dx1
        dy, dgp = _rms_bwd_rows(dx1, y_ref[...], gp_ref[...])
        dy_ref[...] = dy.astype(dy_ref.dtype)
        _acc_rows(dg2_ref, i, jnp.sum(dg2, axis=0, keepdims=True))
        _acc_rows(dgp_ref, i, jnp.sum(dgp, axis=0, keepdims=True))

    rs = _row_spec(tm, n)
    return pl.pallas_call(
        body, name=name, grid=(T // tm,),
        in_specs=[rs, rs, rs, _vec_spec(n), rs, _vec_spec(n)],
        out_specs=(rs, rs, _vec_spec(n), _vec_spec(n)),
        out_shape=(jax.ShapeDtypeStruct((T, n), F32), jax.ShapeDtypeStruct((T, n), BF16),
                   jax.ShapeDtypeStruct((1, n), F32), jax.ShapeDtypeStruct((1, n), F32)),
        compiler_params=_cparams(("arbitrary",)),
    )(dx2, dh2, x1, g_pre2, y, g_post)


def _gate_specs(tm):
    gcol = PAD_OFF["gate"] // D_MODEL
    lspecs = [pl.BlockSpec((tm, D_MODEL), functools.partial(lambda i, c: (i, c), c=gcol + b)) for b in range(N_BRANCH)]
    bspecs = [pl.BlockSpec((1, D_MODEL), functools.partial(lambda i, c: (0, c), c=b)) for b in range(N_BRANCH)]
    pspecs = [_row_spec(tm, D_MODEL) for _ in range(N_BRANCH)]
    return lspecs, bspecs, pspecs


def gate_merge(proj, b_gate, ps, *, name):
    T = proj.shape[0]
    tm = _pick(T, ROW_TILE)
    lspecs, bspecs, pspecs = _gate_specs(tm)

    def body(*refs):
        l_refs, b_refs, p_refs, o_ref = refs[0:3], refs[3:6], refs[6:9], refs[9]
        tot = None
        for lr, br, pr in zip(l_refs, b_refs, p_refs):
            term = jax.nn.sigmoid(lr[...] + br[...]) * pr[...]
            tot = term if tot is None else tot + term
        o_ref[...] = tot.astype(o_ref.dtype)

    return pl.pallas_call(
        body, name=name, grid=(T // tm,), in_specs=lspecs + bspecs + pspecs,
        out_specs=_row_spec(tm, D_MODEL), out_shape=jax.ShapeDtypeStruct((T, D_MODEL), BF16),
        compiler_params=_cparams(("parallel",)),
    )(proj, proj, proj, b_gate, b_gate, b_gate, *ps)


def gate_bwd(dmerged, proj, b_gate, ps, *, name):
    T = proj.shape[0]
    tm = _pick(T, ROW_TILE)
    lspecs, bspecs, pspecs = _gate_specs(tm)

    def body(*refs):
        dm_ref = refs[0]
        l_refs, b_refs, p_refs = refs[1:4], refs[4:7], refs[7:10]
        dl_ref, dp_refs, db_ref = refs[10], refs[11:14], refs[14]
        i = pl.program_id(0)
        dm = dm_ref[...]
        for b, (lr, br, pr, dpr) in enumerate(zip(l_refs, b_refs, p_refs, dp_refs)):
            gt = jax.nn.sigmoid(lr[...] + br[...])
            dpr[...] = (dm * gt).astype(dpr.dtype)
            dl = dm * pr[...] * gt * (1.0 - gt)
            dl_ref[:, b * D_MODEL:(b + 1) * D_MODEL] = dl.astype(dl_ref.dtype)
            part = jnp.sum(dl, axis=0, keepdims=True)

            @pl.when(i == 0)
            def _():
                db_ref[:, b * D_MODEL:(b + 1) * D_MODEL] = part

            @pl.when(i > 0)
            def _():
                db_ref[:, b * D_MODEL:(b + 1) * D_MODEL] += part

    return pl.pallas_call(
        body, name=name, grid=(T // tm,), in_specs=[_row_spec(tm, D_MODEL)] + lspecs + bspecs + pspecs,
        out_specs=(_row_spec(tm, GATE_W), *[_row_spec(tm, D_MODEL) for _ in range(N_BRANCH)], _vec_spec(GATE_W)),
        out_shape=(jax.ShapeDtypeStruct((T, GATE_W), BF16),
                   *[jax.ShapeDtypeStruct((T, D_MODEL), BF16) for _ in range(N_BRANCH)],
                   jax.ShapeDtypeStruct((1, GATE_W), F32)),
        compiler_params=_cparams(("arbitrary",)),
    )(dmerged, proj, proj, proj, b_gate, b_gate, b_gate, *ps)


def _rope_tables(pos_ref, invf_ref):
    ang = pos_ref[...] * invf_ref[...]
    lane = lax.broadcasted_iota(jnp.int32, (1, LANES), 1)
    s = jnp.sin(ang)
    split = MLA_NOPE + MLA_ROPE // 2
    return jnp.cos(ang), jnp.where(lane >= split, s, 0.0), jnp.where(lane < split, s, 0.0)


def _rotate(xh, c, s_hi, s_lo, sign):
    half = MLA_ROPE // 2
    up = pltpu.roll(xh, half, 1)
    down = pltpu.roll(xh, LANES - half, 1)
    return xh * c + sign * (up * s_hi - down * s_lo)


def rope_fwd(qpre, kpre, proj, pos, invf, *, name):
    T = qpre.shape[0]
    tm = _pick(T, ROW_TILE)
    kr_col = PAD_OFF["kr"] // KR_PAD

    def body(q_ref, k_ref, kr_ref, pos_ref, invf_ref, qo_ref, ko_ref):
        c, s_hi, s_lo = _rope_tables(pos_ref, invf_ref)
        kr = _rotate(pltpu.roll(kr_ref[...], MLA_NOPE, 1), c, s_hi, s_lo, 1.0)
        for h in range(MLA_HEADS):
            sl = slice(h * HEAD_PAD, (h + 1) * HEAD_PAD)
            qo_ref[:, sl] = _rotate(q_ref[:, sl], c, s_hi, s_lo, 1.0).astype(qo_ref.dtype)
            ko_ref[:, sl] = (k_ref[:, sl] + kr).astype(ko_ref.dtype)

    rs = _row_spec(tm, MLA_PAD_W)
    return pl.pallas_call(
        body, name=name, grid=(T // tm,),
        in_specs=[rs, rs, _row_spec(tm, KR_PAD, kr_col), _row_spec(tm, 1), _vec_spec(LANES)],
        out_specs=(rs, rs),
        out_shape=(jax.ShapeDtypeStruct((T, MLA_PAD_W), BF16), jax.ShapeDtypeStruct((T, MLA_PAD_W), BF16)),
        compiler_params=_cparams(("parallel",)),
    )(qpre, kpre, proj, pos, invf)


def rope_bwd(dq_pad, dk_pad, pos, invf, *, name):
    T = dq_pad.shape[0]
    tm = _pick(T, ROW_TILE)

    def body(dq_ref, dk_ref, pos_ref, invf_ref, dqo_ref, dkr_ref):
        c, s_hi, s_lo = _rope_tables(pos_ref, invf_ref)
        tot = None
        for h in range(MLA_HEADS):
            sl = slice(h * HEAD_PAD, (h + 1) * HEAD_PAD)
            dqo_ref[:, sl] = _rotate(dq_ref[:, sl], c, s_hi, s_lo, -1.0).astype(dqo_ref.dtype)
            tot = dk_ref[:, sl] if tot is None else tot + dk_ref[:, sl]
        dkr = pltpu.roll(_rotate(tot, c, s_hi, s_lo, -1.0), LANES - MLA_NOPE, 1)
        lane = lax.broadcasted_iota(jnp.int32, (1, LANES), 1)
        dkr_ref[...] = jnp.where(lane < MLA_ROPE, dkr, 0.0).astype(dkr_ref.dtype)

    rs = _row_spec(tm, MLA_PAD_W)
    return pl.pallas_call(
        body, name=name, grid=(T // tm,),
        in_specs=[rs, rs, _row_spec(tm, 1), _vec_spec(LANES)],
        out_specs=(rs, _row_spec(tm, KR_PAD)),
        out_shape=(jax.ShapeDtypeStruct((T, MLA_PAD_W), BF16), jax.ShapeDtypeStruct((T, KR_PAD), BF16)),
        compiler_params=_cparams(("parallel",)),
    )(dq_pad, dk_pad, pos, invf)


ATT_TILE = 512
NEG = -1e30


def _split_bf16(v):
    hi = v.astype(BF16)
    return hi, (v - hi.astype(F32)).astype(BF16)


def _mul(x, s):
    return x if s == 1.0 else x * s


def _dot(a, b, ca, cb):
    return lax.dot_general(a, b, (((ca,), (cb,)), ((), ())), preferred_element_type=F32)


CUM_CHUNK = 256


def _tri(kind):
    r = lax.broadcasted_iota(jnp.int32, (CUM_CHUNK, CUM_CHUNK), 0)
    c = lax.broadcasted_iota(jnp.int32, (CUM_CHUNK, CUM_CHUNK), 1)
    m = {"gt": r > c, "le": r <= c, "lt": r < c}[kind]
    t = jnp.where(m, 1.0, 0.0).astype(BF16)
    return jnp.concatenate([t, t], axis=0)


def _cum(v, tri, kind):
    n = v.shape[1] // CUM_CHUNK
    chunks = [v[:, c * CUM_CHUNK:(c + 1) * CUM_CHUNK] for c in range(n)]
    totals = [jnp.sum(ch, axis=1, keepdims=True) for ch in chunks] if n > 1 else None
    outs = []
    for c, ch in enumerate(chunks):
        r = _dot(jnp.concatenate(_split_bf16(ch), axis=1), tri, 1, 0)
        others = [] if n == 1 else totals[c + 1:] if kind == "gt" else totals[:c]
        for t in others:
            r = r + t
        outs.append(r)
    return outs[0] if n == 1 else jnp.concatenate(outs, axis=1)


def _lane_masks(hp, w):
    lane = lax.broadcasted_iota(jnp.int32, (1, LANES), 1)
    return [(lane >= e * w) & (lane < (e + 1) * w) for e in range(hp)]


def _att_dims(mode, hp, qk_w, v_w):
    assert mode in ("softmax", "sb")
    assert (hp, qk_w, v_w) in ((2, 128, 64), (2, 64, 64), (1, 128, 128))
    qw = hp * LANES if qk_w == LANES else LANES
    return qw


def attn_fwd(q, k, v, *, name, mode, causal, hp, qk_w, v_w, scale, nh, qc, kc, vc, comm=None):
    qw = _att_dims(mode, hp, qk_w, v_w)
    B, S = q.shape[0], q.shape[1]
    Sk = k.shape[1]
    tq = _pick(S, ATT_TILE)
    tk = _pick(Sk, ATT_TILE)
    if causal:
        assert tq == tk and S == Sk
    nq, nkv = S // tq, Sk // tk
    pre_scaled = math.frexp(scale)[0] == 0.5
    post = 1.0 if pre_scaled else scale

    def body(q_ref, k_ref, v_ref, o_ref, st_ref):
        i = pl.program_id(2)
        qv = q_ref[0].astype(BF16)
        masks_qk = _lane_masks(hp, qk_w) if qk_w < LANES else None
        masks_v = _lane_masks(hp, v_w) if v_w < LANES else None
        if qk_w == LANES:
            qs = [qv[:, e * LANES:(e + 1) * LANES] for e in range(hp)]
        else:
            qs = [jnp.where(masks_qk[e], qv, jnp.zeros_like(qv)) for e in range(hp)]
        if pre_scaled:
            qs = [t * scale for t in qs]
        rows = lax.broadcasted_iota(jnp.int32, (tq, tk), 0)
        cols = lax.broadcasted_iota(jnp.int32, (tq, tk), 1)
        tri = _tri("gt") if mode == "sb" else None

        def kv_tile(j):
            off = pl.multiple_of(j * tk, tk)
            kj = k_ref[0, pl.ds(off, tk), :].astype(BF16)
            vj = v_ref[0, pl.ds(off, tk), :].astype(BF16)
            ks = [kj[:, e * LANES:(e + 1) * LANES] for e in range(hp)] if qk_w == LANES else [kj] * hp
            return ks, vj

        def merge(vals):
            if hp == 1:
                return jnp.broadcast_to(vals[0], (tq, LANES))
            return jnp.where(masks_v[0], vals[0], vals[1])

        if mode == "softmax":
            def block(j, carry, diag):
                ms, ls, acc = carry
                ks, vj = kv_tile(j)
                new_m, new_l, alphas, pvs = [], [], [], []
                for e in range(hp):
                    s = _mul(_dot(qs[e], ks[e], 1, 1), post)
                    if diag:
                        s = jnp.where(rows >= cols, s, NEG)
                    m_new = jnp.maximum(ms[e], jnp.max(s, axis=1, keepdims=True))
                    alpha = jnp.exp(ms[e] - m_new)
                    p = jnp.exp(s - m_new)
                    new_l.append(alpha * ls[e] + jnp.sum(p, axis=1, keepdims=True))
                    new_m.append(m_new)
                    alphas.append(alpha)
                    pvs.append(_dot(p.astype(BF16), vj, 1, 0))
                acc = acc * merge(alphas) + merge(pvs)
                return tuple(new_m), tuple(new_l), acc

            init = (tuple(jnp.full((tq, 1), NEG, F32) for _ in range(hp)),
                    tuple(jnp.zeros((tq, 1), F32) for _ in range(hp)),
                    jnp.zeros((tq, LANES), F32))
            if causal:
                carry = lax.fori_loop(0, i, lambda j, c: block(j, c, False), init)
                ms, ls, acc = block(i, carry, True)
            else:
                ms, ls, acc = lax.fori_loop(0, nkv, lambda j, c: block(j, c, False), init)
            o_ref[0] = acc / merge(list(ls))
            st_ref[0, 0] = merge([m + jnp.log(l) for m, l in zip(ms, ls)])
        else:
            def block(j, carry, diag):
                cs_, acc = carry
                ks, vj = kv_tile(j)
                new_c, pvs = [], []
                for e in range(hp):
                    z = _mul(_dot(qs[e], ks[e], 1, 1), post)
                    lg = jnp.log(1.0 + jnp.exp(-jnp.abs(z)))
                    lz = jnp.minimum(z, 0.0) - lg
                    lk = -jnp.maximum(z, 0.0) - lg
                    if diag:
                        lk = jnp.where(rows > cols, lk, 0.0)
                    a = jnp.exp(lz + _cum(lk, tri, "gt") + cs_[e])
                    if diag:
                        a = jnp.where(rows > cols, a, 0.0)
                    pvs.append(_dot(a.astype(BF16), vj, 1, 0))
                    new_c.append(cs_[e] + jnp.sum(lk, axis=1, keepdims=True))
                return tuple(new_c), acc + merge(pvs)

            init = (tuple(jnp.zeros((tq, 1), F32) for _ in range(hp)), jnp.zeros((tq, LANES), F32))
            carry = block(i, init, True)
            cs_, acc = lax.fori_loop(0, i, lambda jj, c: block(i - 1 - jj, c, False), carry)
            o_ref[0] = acc
            st_ref[0, 0] = merge(list(cs_))

    outs, c_outs = _call(
        body, name=name, grid=(B, nh, nq),
        in_specs=[pl.BlockSpec((1, tq, qw), lambda b, h, i: (b, i, qc + h)),
                  pl.BlockSpec((1, Sk, qw), lambda b, h, i: (b, 0, kc + h)),
                  pl.BlockSpec((1, Sk, LANES), lambda b, h, i: (b, 0, vc + h))],
        out_specs=(pl.BlockSpec((1, tq, LANES), lambda b, h, i: (b, i, h)),
                   pl.BlockSpec((1, 1, tq, LANES), lambda b, h, i: (b, h, i, 0))),
        out_shape=(jax.ShapeDtypeStruct((B, S, nh * LANES), F32), jax.ShapeDtypeStruct((B, nh, S, LANES), F32)),
        scratch_shapes=[], args=(q, k, v), sem=("parallel", "parallel", "arbitrary"), comm=comm)
    return (*outs, c_outs) if comm is not None else tuple(outs)


def attn_bwd(q, k, v, o, st, do, *, name, mode, causal, hp, qk_w, v_w, scale, nh, qc, kc, vc, comm=None):
    qw = _att_dims(mode, hp, qk_w, v_w)
    B, S = q.shape[0], q.shape[1]
    Sk = k.shape[1]
    tq = _pick(S, ATT_TILE)
    tk = _pick(Sk, ATT_TILE)
    if causal:
        assert tq == tk and S == Sk
    nq, nkv = S // tq, Sk // tk
    pre_scaled = math.frexp(scale)[0] == 0.5
    post = 1.0 if pre_scaled else scale

    def body(q_ref, k_ref, v_ref, o_ref, st_ref, do_ref, dq_ref, dk_ref, dv_ref):
        i = pl.program_id(2)

        @pl.when(i == 0)
        def _():
            dk_ref[...] = jnp.zeros_like(dk_ref)
            dv_ref[...] = jnp.zeros_like(dv_ref)

        qv = q_ref[0].astype(BF16)
        dov = do_ref[0]
        stv = st_ref[0, 0]
        masks_qk = _lane_masks(hp, qk_w) if qk_w < LANES else None
        masks_v = _lane_masks(hp, v_w) if v_w < LANES else None
        if qk_w == LANES:
            qs = [qv[:, e * LANES:(e + 1) * LANES] for e in range(hp)]
        else:
            qs = [jnp.where(masks_qk[e], qv, jnp.zeros_like(qv)) for e in range(hp)]
        if pre_scaled:
            qs = [t * scale for t in qs]
        if hp == 1:
            dos = [dov.astype(BF16)]
            stats = [stv[:, 0:1]]
        else:
            dos = [jnp.where(masks_v[e], dov, 0.0).astype(BF16) for e in range(hp)]
            stats = [stv[:, e * v_w:e * v_w + 1] for e in range(hp)]
        if mode == "softmax":
            prod = dov * o_ref[0]
            if hp == 1:
                dsum = [jnp.sum(prod, axis=1, keepdims=True)]
            else:
                dsum = [jnp.sum(jnp.where(masks_v[e], prod, 0.0), axis=1, keepdims=True) for e in range(hp)]
        rows = lax.broadcasted_iota(jnp.int32, (tq, tk), 0)
        cols = lax.broadcasted_iota(jnp.int32, (tq, tk), 1)
        if mode == "sb":
            tri_le, tri_lt = _tri("le"), _tri("lt")

        def kv_tile(j):
            off = pl.multiple_of(j * tk, tk)
            kj = k_ref[0, pl.ds(off, tk), :].astype(BF16)
            vj = v_ref[0, pl.ds(off, tk), :].astype(BF16)
            ks = [kj[:, e * LANES:(e + 1) * LANES] for e in range(hp)] if qk_w == LANES else [kj] * hp
            return off, ks, vj

        def scatter(off, dz_list, p_list):
            dvj = None
            for e in range(hp):
                t = _dot(p_list[e], dos[e], 0, 0)
                dvj = t if dvj is None else dvj + t
            dv_ref[0, pl.ds(off, tk), :] += dvj
            if qk_w == LANES:
                for e in range(hp):
                    dk_ref[0, pl.ds(off, tk), e * LANES:(e + 1) * LANES] += _dot(dz_list[e], qs[e], 0, 0)
            else:
                dkj = None
                for e in range(hp):
                    t = _dot(dz_list[e], qs[e], 0, 0)
                    dkj = t if dkj is None else dkj + t
                dk_ref[0, pl.ds(off, tk), :] += dkj

        def dq_add(dqs, dz_list, ks):
            out = []
            for e in range(hp):
                out.append(dqs[e] + _dot(dz_list[e], ks[e], 1, 0))
            return tuple(out)

        dq0 = tuple(jnp.zeros((tq, LANES), F32) for _ in range(hp))

        if mode == "softmax":
            def block(j, dqs, diag):
                off, ks, vj = kv_tile(j)
                dzs, ps = [], []
                for e in range(hp):
                    s = _mul(_dot(qs[e], ks[e], 1, 1), post)
                    p = jnp.exp(s - stats[e])
                    if diag:
                        p = jnp.where(rows >= cols, p, 0.0)
                    dp = _dot(dos[e], vj, 1, 1)
                    dzs.append(_mul(p * (dp - dsum[e]), post).astype(BF16))
                    ps.append(p.astype(BF16))
                scatter(off, dzs, ps)
                return dq_add(dqs, dzs, ks)

            if causal:
                dqs = lax.fori_loop(0, i, lambda j, c: block(j, c, False), dq0)
                dqs = block(i, dqs, True)
            else:
                dqs = lax.fori_loop(0, nkv, lambda j, c: block(j, c, False), dq0)
        else:
            def block(j, carry, diag):
                dqs, cps, cgs = carry
                off, ks, vj = kv_tile(j)
                dzs, ps, new_p, new_g = [], [], [], []
                for e in range(hp):
                    z = _mul(_dot(qs[e], ks[e], 1, 1), post)
                    ez = jnp.exp(-jnp.abs(z))
                    lg = jnp.log(1.0 + ez)
                    lz = jnp.minimum(z, 0.0) - lg
                    lk = -jnp.maximum(z, 0.0) - lg
                    if diag:
                        lk = jnp.where(rows > cols, lk, 0.0)
                    keep_after = stats[e] - cps[e] - _cum(lk, tri_le, "le")
                    a = jnp.exp(lz + keep_after)
                    if diag:
                        a = jnp.where(rows > cols, a, 0.0)
                    g = _dot(dos[e], vj, 1, 1) * a
                    gsum = cgs[e] + _cum(g, tri_lt, "lt")
                    inv = 1.0 / (1.0 + ez)
                    sig = jnp.where(z >= 0.0, inv, ez * inv)
                    dz = _mul(g * (1.0 - sig) - gsum * sig, post)
                    if diag:
                        dz = jnp.where(rows > cols, dz, 0.0)
                    dzs.append(dz.astype(BF16))
                    ps.append(a.astype(BF16))
                    new_p.append(cps[e] + jnp.sum(lk, axis=1, keepdims=True))
                    new_g.append(cgs[e] + jnp.sum(g, axis=1, keepdims=True))
                scatter(off, dzs, ps)
                return dq_add(dqs, dzs, ks), tuple(new_p), tuple(new_g)

            zc = tuple(jnp.zeros((tq, 1), F32) for _ in range(hp))
            carry = lax.fori_loop(0, i, lambda j, c: block(j, c, False), (dq0, zc, zc))
            dqs, _, _ = block(i, carry, True)

        if pre_scaled:
            dqs = [t * scale for t in dqs]
        if qk_w == LANES:
            for e in range(hp):
                dq_ref[0, :, e * LANES:(e + 1) * LANES] = dqs[e]
        else:
            dq_ref[0] = jnp.where(masks_qk[0], dqs[0], dqs[1])

    outs, c_outs = _call(
        body, name=name, grid=(B, nh, nq),
        in_specs=[pl.BlockSpec((1, tq, qw), lambda b, h, i: (b, i, qc + h)),
                  pl.BlockSpec((1, Sk, qw), lambda b, h, i: (b, 0, kc + h)),
                  pl.BlockSpec((1, Sk, LANES), lambda b, h, i: (b, 0, vc + h)),
                  pl.BlockSpec((1, tq, LANES), lambda b, h, i: (b, i, h)),
                  pl.BlockSpec((1, 1, tq, LANES), lambda b, h, i: (b, h, i, 0)),
                  pl.BlockSpec((1, tq, LANES), lambda b, h, i: (b, i, h))],
        out_specs=(pl.BlockSpec((1, tq, qw), lambda b, h, i: (b, i, h)),
                   pl.BlockSpec((1, Sk, qw), lambda b, h, i: (b, 0, h)),
                   pl.BlockSpec((1, Sk, LANES), lambda b, h, i: (b, 0, h))),
        out_shape=(jax.ShapeDtypeStruct((B, S, nh * qw), F32), jax.ShapeDtypeStruct((B, Sk, nh * qw), F32),
                   jax.ShapeDtypeStruct((B, Sk, nh * LANES), F32)),
        scratch_shapes=[], args=(q, k, v, o, st, do), sem=("parallel", "parallel", "arbitrary"), comm=comm)
    return (*outs, c_outs) if comm is not None else tuple(outs)


def _pos(p):
    return 4 * p[0] + 2 * p[1] + p[2]


N_CHIP = 4
MESH_ID = pl.DeviceIdType.MESH
_ANY = pl.BlockSpec(memory_space=pl.ANY)


def _me():
    return lax.axis_index("x"), lax.axis_index("y"), lax.axis_index("c")


def _other_chips(x, y):
    return [(1 - x, y), (x, 1 - y), (1 - x, 1 - y)]


class _Comm:
    def __init__(self, arrs, out_shapes, n_sem, n_local, start, finish):
        self.arrs, self.out_shapes, self.start, self.finish = list(arrs), list(out_shapes), start, finish
        self.scratch = [pltpu.SemaphoreType.DMA((n_sem,)), pltpu.SemaphoreType.DMA((n_sem,)),
                        pltpu.SemaphoreType.DMA((max(n_local, 1),))]


def _run_comm(comm, name):
    n = len(comm.arrs)

    def body(*refs):
        r = (refs[:n], refs[n:2 * n], refs[2 * n], refs[2 * n + 1], refs[2 * n + 2])
        comm.start(*r)
        comm.finish(*r)

    return pl.pallas_call(
        body, name=name, in_specs=[_ANY] * n, out_specs=[_ANY] * n, out_shape=comm.out_shapes,
        scratch_shapes=comm.scratch, compiler_params=pltpu.CompilerParams(has_side_effects=True),
    )(*comm.arrs)


def _call(body, *, name, grid, in_specs, out_specs, out_shape, scratch_shapes, args, sem, comm=None):
    in_specs, out_specs, out_shape = list(in_specs), list(out_specs), list(out_shape)
    if comm is None:
        res = pl.pallas_call(body, name=name, grid=grid, in_specs=in_specs, out_specs=out_specs, out_shape=out_shape,
                             scratch_shapes=scratch_shapes, compiler_params=_cparams(sem))(*args)
        return list(res), []
    n_in, n_out, n_scr, nc = len(in_specs), len(out_specs), len(scratch_shapes), len(comm.arrs)

    def wrapped(*refs):
        ins, refs = refs[:n_in], refs[n_in:]
        c_in, refs = refs[:nc], refs[nc:]
        outs, refs = refs[:n_out], refs[n_out:]
        c_out, refs = refs[:nc], refs[nc:]
        scr, sems = refs[:n_scr], refs[n_scr:]
        ids = [pl.program_id(a) for a in range(len(grid))]
        first = functools.reduce(jnp.logical_and, [i == 0 for i in ids])
        last = functools.reduce(jnp.logical_and, [i == g - 1 for i, g in zip(ids, grid)])

        @pl.when(first)
        def _():
            comm.start(c_in, c_out, *sems)

        body(*ins, *outs, *scr)

        @pl.when(last)
        def _():
            comm.finish(c_in, c_out, *sems)

    res = pl.pallas_call(
        wrapped, name=name, grid=grid, in_specs=in_specs + [_ANY] * nc, out_specs=out_specs + [_ANY] * nc,
        out_shape=out_shape + comm.out_shapes, scratch_shapes=list(scratch_shapes) + comm.scratch,
        compiler_params=pltpu.CompilerParams(dimension_semantics=("arbitrary",) * len(grid),
                                             vmem_limit_bytes=VMEM_LIMIT, has_side_effects=True),
    )(*args, *comm.arrs)
    return list(res[:n_out]), list(res[n_out:])


def gather_two_level(bufs, *, name=None):
    n = len(bufs)

    def parts(x_refs, out_refs, send_sems, recv_sems, local_sems):
        x, y, c = _me()
        me, sibling = (x, y, c), (x, y, 1 - c)
        chips = _other_chips(x, y)

        def copy(a, k, block, to, from_input=False):
            dst = out_refs[a].at[_pos(block)]
            return pltpu.make_async_remote_copy(src_ref=x_refs[a] if from_input else dst, dst_ref=dst,
                                                send_sem=send_sems.at[7 * a + k], recv_sem=recv_sems.at[7 * a + k],
                                                device_id=to, device_id_type=MESH_ID)

        mine = [pltpu.make_async_copy(x_refs[a], out_refs[a].at[_pos(me)], local_sems.at[a]) for a in range(n)]
        first = []
        for a in range(n):
            first.append(copy(a, 0, me, sibling, from_input=True))
            first += [copy(a, 1 + j, me, (*chip, c), from_input=True) for j, chip in enumerate(chips)]
        return copy, mine, first, me, sibling, chips, c

    def start(*refs):
        _, mine, first, *_ = parts(*refs)
        for cp in mine + first:
            cp.start()

    def finish(*refs):
        copy, mine, first, me, sibling, chips, c = parts(*refs)
        passed = []
        for j, chip in enumerate(chips):
            for a in range(n):
                copy(a, 1 + j, (*chip, c), me).wait_recv()
                passed.append(copy(a, 4 + j, (*chip, c), sibling))
                passed[-1].start()
        for a in range(n):
            copy(a, 0, sibling, me).wait_recv()
            for j, chip in enumerate(chips):
                copy(a, 4 + j, (*chip, 1 - c), me).wait_recv()
        for cp in first + passed:
            cp.wait_send()
        for cp in mine:
            cp.wait()

    out_shapes = [jax.ShapeDtypeStruct((N_DEV,) + b.shape, b.dtype) for b in bufs]
    comm = _Comm(bufs, out_shapes, 7 * n, n, start, finish)
    return _run_comm(comm, name) if name else comm


def sibling_exchange(sends, *, name=None):
    n = len(sends)

    def copies(s_refs, out_refs, send_sems, recv_sems, local_sems):
        x, y, c = _me()
        return [pltpu.make_async_remote_copy(src_ref=s_refs[a].at[1 - c], dst_ref=out_refs[a], send_sem=send_sems.at[a],
                                             recv_sem=recv_sems.at[a], device_id=(x, y, 1 - c), device_id_type=MESH_ID)
                for a in range(n)]

    def start(*refs):
        for cp in copies(*refs):
            cp.start()

    def finish(*refs):
        for cp in copies(*refs):
            cp.wait()

    out_shapes = [jax.ShapeDtypeStruct(s.shape[1:], s.dtype) for s in sends]
    comm = _Comm(sends, out_shapes, n, 0, start, finish)
    return _run_comm(comm, name) if name else comm


def _flat2(shape):
    return math.prod(shape[:-1]), shape[-1]


def _row_tile(r):
    return r if r <= 512 else _pick8(r, 256)


def _pick8(n, pref):
    t = pref
    while n % t:
        t -= 8
    return t


def chip_sum(send, got, core, *, name):
    shape = got.shape[1:]
    r, cdim = _flat2(shape)
    tr = _row_tile(r)
    send = send.reshape(2, N_CHIP, r, cdim)
    got = got.reshape(N_CHIP, r, cdim)

    def body(core_ref, s_ref, g_ref, o_ref):
        o_ref[...] = (s_ref[0].astype(F32) + g_ref[...].astype(F32)).astype(o_ref.dtype)

    blk = pl.BlockSpec((N_CHIP, tr, cdim), lambda i, core_ref: (0, i, 0))
    out = pl.pallas_call(
        body, name=name,
        grid_spec=pltpu.PrefetchScalarGridSpec(
            num_scalar_prefetch=1, grid=(r // tr,),
            in_specs=[pl.BlockSpec((1, N_CHIP, tr, cdim), lambda i, core_ref: (core_ref[0], 0, i, 0)), blk],
            out_specs=blk),
        out_shape=jax.ShapeDtypeStruct((N_CHIP, r, cdim), BF16),
        compiler_params=_cparams(("parallel",)),
    )(core, send, got)
    return out.reshape((N_CHIP,) + shape)


def chip_exchange(sums, *, name=None):
    n = len(sums)

    def copies(s_refs, out_refs, send_sems, recv_sems, local_sems):
        x, y, c = _me()
        mine = 2 * x + y
        chips = _other_chips(x, y)
        local = [pltpu.make_async_copy(s_refs[a].at[mine], out_refs[a].at[mine], local_sems.at[a]) for a in range(n)]
        sends, recvs = [], []
        for a in range(n):
            for j, (px, py) in enumerate(chips):
                sems = dict(send_sem=send_sems.at[3 * a + j], recv_sem=recv_sems.at[3 * a + j], device_id=(px, py, c),
                            device_id_type=MESH_ID)
                sends.append(pltpu.make_async_remote_copy(src_ref=s_refs[a].at[2 * px + py], dst_ref=out_refs[a].at[mine], **sems))
                recvs.append(pltpu.make_async_remote_copy(src_ref=s_refs[a].at[mine], dst_ref=out_refs[a].at[2 * px + py], **sems))
        return local, sends, recvs

    def start(*refs):
        local, sends, _ = copies(*refs)
        for cp in local + sends:
            cp.start()

    def finish(*refs):
        local, sends, recvs = copies(*refs)
        for cp in recvs:
            cp.wait_recv()
        for cp in sends:
            cp.wait_send()
        for cp in local:
            cp.wait()

    out_shapes = [jax.ShapeDtypeStruct(s.shape, s.dtype) for s in sums]
    comm = _Comm(sums, out_shapes, 3 * n, n, start, finish)
    return _run_comm(comm, name) if name else comm


def sum_adamw(parts, w, m, v, *, name):
    shape = w.shape
    R, cdim = _flat2(shape)
    n_slots = parts.shape[0]
    tr = _row_tile(R)
    parts = parts.reshape(n_slots, R, cdim)
    w, m, v = (t.reshape(1, R, cdim) for t in (w, m, v))
    c1 = 1.0 - ADAM_B1 ** ADAM_STEP
    c2 = 1.0 - ADAM_B2 ** ADAM_STEP

    def body(p_ref, w_ref, m_ref, v_ref, g_ref, d_ref, mo_ref, vo_ref):
        g = p_ref[0].astype(F32)
        for s in range(1, n_slots):
            g = g + p_ref[s].astype(F32)
        mn = ADAM_B1 * m_ref[...] + (1.0 - ADAM_B1) * g
        vn = ADAM_B2 * v_ref[...] + (1.0 - ADAM_B2) * jnp.square(g)
        g_ref[...] = g
        mo_ref[...] = mn
        vo_ref[...] = vn
        d_ref[...] = -ADAM_LR * ((mn / c1) / (jnp.sqrt(vn / c2) + ADAM_EPS) + ADAM_WD * w_ref[...])

    rs = pl.BlockSpec((None, tr, cdim), lambda i: (0, i, 0))
    sd = jax.ShapeDtypeStruct((1, R, cdim), F32)
    res = pl.pallas_call(
        body, name=name, grid=(R // tr,),
        in_specs=[pl.BlockSpec((n_slots, tr, cdim), lambda i: (0, i, 0)), rs, rs, rs],
        out_specs=(rs, rs, rs, rs), out_shape=(sd, sd, sd, sd),
        compiler_params=_cparams(("parallel",)),
    )(parts, w, m, v)
    return [t.reshape(shape) for t in res]


WEIGHTS = ("ln_mix_pre", "w_in", "b_gate", "q_norm", "w_uq", "kv_norm", "w_uk", "w_uv", "mem_norm", "w_mem_kv",
           "w_branch_out", "w_out", "ln_mix_post", "ln_mlp_pre", "w_mlp_up", "w_mlp_down", "ln_mlp_post")
BIG = dict(w_in=((D_MODEL, IN_WIDTH), 1), w_uq=((Q_LORA, MLA_HEADS * (MLA_NOPE + MLA_ROPE)), 1),
           w_uk=((KV_LORA, MLA_HEADS * MLA_NOPE), 1), w_uv=((KV_LORA, MLA_HEADS * MLA_V), 1),
           w_mem_kv=((D_MODEL, 2 * QM_W), 0), w_branch_out=((N_BRANCH, BRANCH_W, D_MODEL), 2),
           w_out=((D_MODEL, D_MODEL), 0), w_mlp_up=((D_MODEL, D_FF), 1), w_mlp_down=((D_FF, D_MODEL), 0))
SMALL = tuple(n for n in WEIGHTS if n not in BIG)


def _shard_shape(name):
    full, ax = BIG[name]
    return tuple(s // N_DEV if a == ax else s for a, s in enumerate(full))


PACK_TILE = 512


def _pad_rows(a):
    r = a.shape[-2]
    to = PACK_TILE if r > PACK_TILE else 8
    pad = [(0, 0)] * a.ndim
    pad[-2] = (0, (-r) % to)
    return jnp.pad(a, pad)


def _pack_rows(arrs):
    return _pad_rows(jnp.concatenate([a.reshape(-1, LANES) for a in arrs], axis=0))


def _unpack_rows(packed, shapes, lead=()):
    out, r = [], 0
    for shp in shapes:
        n = math.prod(shp) // LANES
        out.append(packed[..., r:r + n, :].reshape(lead + tuple(shp)))
        r += n
    return out


def _slots_to_full(name, slots):
    full, ax = BIG[name]
    return jnp.moveaxis(slots, 0, ax).reshape(full)


def _full_to_slots(name, w):
    full, ax = BIG[name]
    split = full[:ax] + (N_DEV, full[ax] // N_DEV) + full[ax + 1:]
    return jnp.moveaxis(w.reshape(split), ax, 0)


def _full_to_owner(name, w):
    s = _full_to_slots(name, w)
    return jnp.swapaxes(s.reshape((N_CHIP, 2) + s.shape[1:]), 0, 1)


def _split_in(w):
    offs = np.cumsum((0,) + IN_SIZES)
    names = ("cq", "ckv", "kr", "sb", "qm", "gate")
    return {n: w[:, offs[i]:offs[i + 1]] for i, n in enumerate(names)}


def _pad_in(w):
    p = _split_in(w)
    p["kr"] = jnp.pad(p["kr"], ((0, 0), (0, KR_PAD - MLA_ROPE)))
    p["zpad"] = jnp.zeros((w.shape[0], PAD_SIZES["zpad"]), w.dtype)
    return jnp.concatenate([p[n] for n in PAD_ORDER], axis=1)


def _unpad_in(wp):
    p = {n: wp[:, PAD_OFF[n]:PAD_OFF[n] + PAD_SIZES[n]] for n in PAD_ORDER}
    p["kr"] = p["kr"][:, :MLA_ROPE]
    return jnp.concatenate([p[n] for n in ("cq", "ckv", "kr", "sb", "qm", "gate")], axis=1)


def _pad_heads(w, width):
    r = w.shape[0]
    return jnp.pad(w.reshape(r, MLA_HEADS, width), ((0, 0), (0, 0), (0, HEAD_PAD - width))).reshape(r, MLA_PAD_W)


def _unpad_heads(wp, width):
    r = wp.shape[0]
    return wp.reshape(r, MLA_HEADS, HEAD_PAD)[:, :, :width].reshape(r, MLA_HEADS * width)


def _rope_inv_freq():
    half = MLA_ROPE // 2
    inv = 1.0 / (ROPE_THETA ** (jnp.arange(half, dtype=F32) * (2.0 / MLA_ROPE)))
    tab = jnp.zeros((LANES,), F32)
    tab = tab.at[MLA_NOPE:MLA_NOPE + half].set(inv).at[MLA_NOPE + half:MLA_NOPE + MLA_ROPE].set(inv)
    return tab.reshape(1, LANES)


def _as_tuple(r):
    return r if isinstance(r, tuple) else (r,)


class _Exchange:
    def __init__(self, w):
        self.w = w
        self.rest = tuple(n for n in BIG if n != "w_in")
        self.pending = {}
        self.reduced = {}

    def first_weights(self):
        (s,) = gather_two_level([self.w["w_in"].astype(BF16)], name="gather_w_in")
        return _slots_to_full("w_in", s)

    def rest_comm(self):
        return gather_two_level([self.w[n].astype(BF16) for n in self.rest])

    def rest_weights(self, slots):
        return {n: s if n == "w_mlp_up" else _slots_to_full(n, s) for n, s in zip(self.rest, slots)}

    def reduce_start(self, tag, grads):
        names = tuple(grads)
        sends = [grads[n] for n in names]
        gots = sibling_exchange(sends, name=f"sibling_{tag}")
        core = lax.axis_index("c").astype(jnp.int32).reshape(1)
        sums = [chip_sum(s, g, core, name=f"chip_sum_{n}") for n, s, g in zip(names, sends, gots)]
        self.pending[tag] = names
        return chip_exchange(sums)

    def reduce_done(self, tag, recvs):
        self.reduced.update(zip(self.pending.pop(tag), recvs))


def _local_step(x, mem, positions, tgt, sm, ex):
    B, S, D = x.shape
    M = mem.shape[1]
    T = B * S
    x2 = x.reshape(T, D)
    mem2 = mem.reshape(B * M, D)
    pos = positions.reshape(T, 1).astype(F32)
    invf = _rope_inv_freq()
    w_in_pad = _pad_in(ex.first_weights())
    cq_col, ckv_col = PAD_OFF["cq"] // Q_LORA, PAD_OFF["ckv"] // KV_LORA
    sb_col, qm_col = PAD_OFF["sb"] // LANES, PAD_OFF["qm"] // LANES
    sb_blk = SB_HEADS * SB_DIM // LANES
    mla = dict(mode="softmax", causal=True, hp=2, qk_w=128, v_w=64, scale=(MLA_NOPE + MLA_ROPE) ** -0.5,
               nh=MLA_HEADS // 2, qc=0, kc=0, vc=0)
    sbk = dict(mode="sb", causal=True, hp=2, qk_w=64, v_w=64, scale=SB_DIM ** -0.5, nh=SB_HEADS // 2,
               qc=sb_col, kc=sb_col + sb_blk, vc=sb_col + 2 * sb_blk)
    mca = dict(mode="softmax", causal=False, hp=1, qk_w=128, v_w=128, scale=MEM_DIM ** -0.5, nh=MEM_HEADS,
               qc=qm_col, kc=0, vc=MEM_HEADS)

    h = rms_fwd(x2, sm["ln_mix_pre"], name="rms_mix_pre")
    proj = matmul(h, w_in_pad, name="mm_in")
    proj3 = proj.reshape(B, S, IN_PAD)
    o_sb, st_sb, *slots = attn_fwd(proj3, proj3, proj3, name="sb_fwd", comm=ex.rest_comm(), **sbk)
    W = ex.rest_weights(slots[0] if slots else None)
    w_uq_pad = _pad_heads(W["w_uq"], MLA_NOPE + MLA_ROPE)
    w_uk_pad = _pad_heads(W["w_uk"], MLA_NOPE)
    cqn = rms_fwd(proj, sm["q_norm"], n=Q_LORA, col=cq_col, name="rms_q")
    ckvn = rms_fwd(proj, sm["kv_norm"], n=KV_LORA, col=ckv_col, name="rms_kv")
    qpre = matmul(cqn, w_uq_pad, name="mm_uq")
    kpre = matmul(ckvn, w_uk_pad, name="mm_uk")
    v_mla = matmul(ckvn, W["w_uv"], out_dtype=BF16, name="mm_uv").reshape(B, S, -1)
    q_pad, k_pad = rope_fwd(qpre, kpre, proj, pos, invf, name="rope_fwd")
    q_pad, k_pad = q_pad.reshape(B, S, -1), k_pad.reshape(B, S, -1)
    o_mla, st_mla = attn_fwd(q_pad, k_pad, v_mla, name="mla_fwd", **mla)
    memh = rms_fwd(mem2, sm["mem_norm"], name="rms_mem")
    mkv = matmul(memh, W["w_mem_kv"], out_dtype=BF16, name="mm_memkv").reshape(B, M, -1)
    o_mem, st_mem = attn_fwd(proj3, mkv, mkv, name="mem_fwd", **mca)
    outs = [o.reshape(T, BRANCH_W) for o in (o_mla, o_sb, o_mem)]
    ps = [matmul(o, W["w_branch_out"][b], name=f"mm_bo{b}") for b, o in enumerate(outs)]
    merged = gate_merge(proj, sm["b_gate"], ps, name="gate_merge")
    y = matmul(merged, W["w_out"], name="mm_out")
    x1, h2 = mix_post_fwd(x2, y, sm["ln_mix_post"], sm["ln_mlp_pre"], name="mix_post")
    u = matmul(h2, W["w_mlp_up"], b_slots="n", act="relu2", out_dtype=BF16, name="mm_up")
    d = matmul(u, W["w_mlp_down"], name="mm_down")
    loss_p, dx2, dd, dg_mlp_post = loss_head(x1, d, sm["ln_mlp_post"], tgt.reshape(T, D), name="loss_head")

    da = matmul(dd, W["w_mlp_down"], tb=True, act="relu2_bwd", act_in=u, out_dtype=BF16, name="mm_down_dx")
    g_down = matmul(u, dd, ta=True, out_dtype=BF16, out_slots="m", name="mm_down_dw")
    dh2 = matmul(da, W["w_mlp_up"], tb=True, b_slots="k", name="mm_up_dx")
    g_up = matmul(h2, da, ta=True, out_dtype=BF16, out_slots="n", name="mm_up_dw")
    red_mlp = ex.reduce_start("mlp", dict(w_mlp_down=g_down, w_mlp_up=g_up))
    dx1, dy, dg_mlp_pre, dg_mix_post = mlp_pre_bwd(dx2, dh2, x1, sm["ln_mlp_pre"], y, sm["ln_mix_post"], name="mlp_pre_bwd")
    dmerged = matmul(dy, W["w_out"], tb=True, name="mm_out_dx")
    g_out = matmul(merged, dy, ta=True, out_dtype=BF16, name="mm_out_dw")
    dlog, dp0, dp1, dp2, db_gate = gate_bwd(dmerged, proj, sm["b_gate"], ps, name="gate_bwd")
    dps = (dp0, dp1, dp2)
    dos = [matmul(dps[b], W["w_branch_out"][b], tb=True, name=f"mm_bo{b}_dx").reshape(B, S, BRANCH_W) for b in range(N_BRANCH)]
    g_bo = jnp.stack([matmul(outs[b], dps[b], ta=True, out_dtype=BF16, name=f"mm_bo{b}_dw") for b in range(N_BRANCH)])
    red_mix = ex.reduce_start("mix", dict(w_out=_full_to_owner("w_out", g_out),
                                          w_branch_out=_full_to_owner("w_branch_out", g_bo)))
    dq_pad, dk_pad, dv_mla, *got = attn_bwd(q_pad, k_pad, v_mla, o_mla, st_mla, dos[0], name="mla_bwd", comm=red_mlp, **mla)
    ex.reduce_done("mlp", got[0] if got else None)
    dsq, dsk, dsv, *got = attn_bwd(proj3, proj3, proj3, o_sb, st_sb, dos[1], name="sb_bwd", comm=red_mix, **sbk)
    ex.reduce_done("mix", got[0] if got else None)
    dqm, dmk, dmv = attn_bwd(proj3, mkv, mkv, o_mem, st_mem, dos[2], name="mem_bwd", **mca)
    dmkv = jnp.concatenate([dmk, dmv], axis=-1).astype(BF16).reshape(B * M, -1)
    dmemh = matmul(dmkv, W["w_mem_kv"], tb=True, name="mm_memkv_dx")
    g_memkv = matmul(memh, dmkv, ta=True, out_dtype=BF16, name="mm_memkv_dw")
    _, dg_mem_norm = rms_bwd(dmemh, mem2, sm["mem_norm"], name="rms_mem_bwd")
    dq_pad, dk_pad, dv_mla = dq_pad.reshape(T, -1), dk_pad.reshape(T, -1), dv_mla.reshape(T, -1)
    dqpre, dkr = rope_bwd(dq_pad, dk_pad, pos, invf, name="rope_bwd")
    dcqn = matmul(dqpre, w_uq_pad, tb=True, name="mm_uq_dx")
    g_uq = _unpad_heads(matmul(cqn, dqpre, ta=True, out_dtype=BF16, name="mm_uq_dw"), MLA_NOPE + MLA_ROPE)
    dckvn = matmul(dk_pad, w_uk_pad, tb=True, name="mm_uk_dx")
    dckvn = matmul(dv_mla, W["w_uv"], tb=True, acc=dckvn, name="mm_uv_dx")
    g_uk = _unpad_heads(matmul(ckvn, dk_pad, ta=True, out_dtype=BF16, name="mm_uk_dw"), MLA_NOPE)
    g_uv = matmul(ckvn, dv_mla, ta=True, out_dtype=BF16, name="mm_uv_dw")
    dcq, dg_q_norm = rms_bwd(dcqn, proj, sm["q_norm"], n=Q_LORA, col=cq_col, out_dtype=BF16, name="rms_q_bwd")
    dckv, dg_kv_norm = rms_bwd(dckvn, proj, sm["kv_norm"], n=KV_LORA, col=ckv_col, out_dtype=BF16, name="rms_kv_bwd")
    pieces = dict(gate=dlog, sb=jnp.concatenate([dsq, dsk, dsv], axis=-1).reshape(T, -1), qm=dqm.reshape(T, -1),
                  ckv=dckv, cq=dcq, kr=dkr, zpad=jnp.zeros((T, PAD_SIZES["zpad"]), BF16))
    dproj = jnp.concatenate([pieces[n].astype(BF16) for n in PAD_ORDER], axis=1)
    g_in = _unpad_in(matmul(h, dproj, ta=True, out_dtype=BF16, name="mm_in_dw"))
    last = dict(w_in=g_in, w_uq=g_uq, w_uk=g_uk, w_uv=g_uv, w_mem_kv=g_memkv)
    red_in = ex.reduce_start("in", {n: _full_to_owner(n, g) for n, g in last.items()})
    dh, *got = _as_tuple(matmul(dproj, w_in_pad, tb=True, comm=red_in, name="mm_in_dx"))
    ex.reduce_done("in", got[0] if got else None)
    dx, dg_mix_pre = rms_bwd(dh, x2, sm["ln_mix_pre"], residual=dx1, name="rms_mix_pre_bwd")

    small = dict(ln_mix_pre=dg_mix_pre, b_gate=db_gate, q_norm=dg_q_norm, kv_norm=dg_kv_norm, mem_norm=dg_mem_norm,
                 ln_mix_post=dg_mix_post, ln_mlp_pre=dg_mlp_pre, ln_mlp_post=dg_mlp_post)
    return loss_p, dx.reshape(B, S, D), small


def kernel(x, mem, positions, ln_mix_pre, w_in, b_gate, q_norm, w_uq, kv_norm, w_uk, w_uv, mem_norm, w_mem_kv, w_branch_out, w_out, ln_mix_post, ln_mlp_pre, w_mlp_up, w_mlp_down, ln_mlp_post, loss_target, m_ln_mix_pre, m_w_in, m_b_gate, m_q_norm, m_w_uq, m_kv_norm, m_w_uk, m_w_uv, m_mem_norm, m_w_mem_kv, m_w_branch_out, m_w_out, m_ln_mix_post, m_ln_mlp_pre, m_w_mlp_up, m_w_mlp_down, m_ln_mlp_post, v_ln_mix_pre, v_w_in, v_b_gate, v_q_norm, v_w_uq, v_kv_norm, v_w_uk, v_w_uv, v_mem_norm, v_w_mem_kv, v_w_branch_out, v_w_out, v_ln_mix_post, v_ln_mlp_pre, v_w_mlp_up, v_w_mlp_down, v_ln_mlp_post):
    given = dict(locals())
    w = {n: given[n][0] for n in WEIGHTS}
    m = {n: given["m_" + n][0] for n in WEIGHTS}
    v = {n: given["v_" + n][0] for n in WEIGHTS}
    sm = {n: w[n].reshape(1, -1) for n in SMALL}
    ex = _Exchange(w)
    loss_row, grad_x, g_small = _local_step(x, mem, positions, loss_target, sm, ex)

    res = {n: sum_adamw(ex.reduced[n], given[n], given["m_" + n], given["v_" + n], name=f"adamw_{n}") for n in BIG}

    small_shapes = [w[n].shape for n in SMALL] + [(LANES,)]
    parts = gather_two_level([_pack_rows([g_small[n] for n in SMALL] + [loss_row])], name="gather_small_grads")[0]
    no_row = jnp.zeros((1, LANES), F32)
    res_small = sum_adamw(parts, *[_pack_rows([t[n] for n in SMALL] + [no_row]) for t in (w, m, v)], name="adamw_replicated")
    res_small = [_unpack_rows(r, small_shapes) for r in res_small]
    for i, n in enumerate(SMALL):
        res[n] = [r[i][None] for r in res_small]
    loss = res_small[0][len(SMALL)][0]

    out = [loss, grad_x]
    for k in range(4):
        out += [res[n][k] for n in WEIGHTS]
    return tuple(out)
```

```python
import functools
import math

import numpy as np
import jax
import jax.numpy as jnp
from jax import lax
from jax.experimental import pallas as pl
from jax.experimental.pallas import tpu as pltpu

F32 = jnp.float32
BF16 = jnp.bfloat16

D_MODEL = 1024
MEM_HEADS, MEM_DIM = 4, 128
MLA_HEADS, MLA_NOPE, MLA_ROPE, MLA_V = 8, 64, 32, 64
Q_LORA, KV_LORA = 384, 256
ROPE_THETA = 10000.0
SB_HEADS, SB_DIM = 8, 64
D_FF = 4 * D_MODEL
N_BRANCH, BRANCH_W = 3, 512
EPS = 1e-6
SB_W = 3 * SB_HEADS * SB_DIM
QM_W = MEM_HEADS * MEM_DIM
GATE_W = N_BRANCH * D_MODEL
IN_SIZES = (Q_LORA, KV_LORA, MLA_ROPE, SB_W, QM_W, GATE_W)
IN_WIDTH = sum(IN_SIZES)
ADAM_LR, ADAM_B1, ADAM_B2, ADAM_EPS, ADAM_WD, ADAM_STEP = 0.001, 0.9, 0.999, 1e-08, 0.01, 10

N_DEV = 8
LANES = 128
KR_PAD = LANES
PAD_ORDER = ("gate", "cq", "kr", "ckv", "sb", "qm", "zpad")
MM_TILE = 1024
PAD_SIZES = dict(gate=GATE_W, cq=Q_LORA, kr=KR_PAD, ckv=KV_LORA, sb=SB_W, qm=QM_W)
PAD_SIZES["zpad"] = (-sum(PAD_SIZES.values())) % MM_TILE
PAD_OFF = {}
_o = 0
for _n in PAD_ORDER:
    PAD_OFF[_n] = _o
    _o += PAD_SIZES[_n]
IN_PAD = _o
HEAD_PAD = LANES
MLA_PAD_W = MLA_HEADS * HEAD_PAD
VMEM_LIMIT = 48 * 1024 * 1024


def _cparams(sem):
    return pltpu.CompilerParams(dimension_semantics=sem, vmem_limit_bytes=VMEM_LIMIT)


def _pick(n, pref):
    if n <= pref:
        return n
    t = (pref // LANES) * LANES
    while t >= LANES:
        if n % t == 0:
            return t
        t -= LANES
    return n


def matmul(a, b, *, name, ta=False, tb=False, out_dtype=F32, acc=None, act=None, act_in=None,
           b_slots=None, out_slots=None, comm=None, tm_pref=MM_TILE, tn_pref=MM_TILE, tk_pref=MM_TILE):
    M, K = (a.shape[1], a.shape[0]) if ta else a.shape
    tm, tk = _pick(M, tm_pref), _pick(K, tk_pref)
    if b_slots == "n":
        assert not tb and b.shape[:2] == (N_DEV, K)
        tn = b.shape[2]
        N = N_DEV * tn
    elif b_slots == "k":
        assert tb and N_DEV * b.shape[2] == K
        N, tk = b.shape[1], b.shape[2]
        tn = _pick(N, tn_pref)
    else:
        N = b.shape[0] if tb else b.shape[1]
        assert (b.shape[1] if tb else b.shape[0]) == K
        tn = _pick(N, tn_pref)
    if out_slots == "m":
        tm = M // N_DEV
    elif out_slots == "n":
        tn = N // N_DEV
    assert out_slots is None or (acc is None and act_in is None)
    nk = K // tk
    dims = (((0 if ta else 1,), (1 if tb else 0,)), ((), ()))
    assert act in (None, "relu2", "relu2_bwd") and (act == "relu2_bwd") == (act_in is not None)

    def body(*refs):
        a_ref, b_ref = refs[0], refs[1]
        pos = 2
        acc_ref = act_ref = None
        if acc is not None:
            acc_ref = refs[pos]
            pos += 1
        if act_in is not None:
            act_ref = refs[pos]
            pos += 1
        out = refs[pos]
        scratch = refs[pos + 1:]

        part = lax.dot_general(a_ref[...].astype(BF16), b_ref[...].astype(BF16), dims,
                               preferred_element_type=F32)

        def finish(r):
            if acc_ref is not None:
                r = r + acc_ref[...]
            if act == "relu2":
                r = jnp.square(jnp.maximum(r, 0.0))
            elif act == "relu2_bwd":
                r = r * (2.0 * jnp.sqrt(act_ref[...].astype(F32)))
            out[...] = r.astype(out.dtype)

        if nk == 1:
            finish(part)
        else:
            acc_sc = scratch[0]
            k = pl.program_id(2)

            @pl.when(k == 0)
            def _():
                acc_sc[...] = part

            @pl.when(k > 0)
            def _():
                acc_sc[...] += part

            @pl.when(k == nk - 1)
            def _():
                finish(acc_sc[...])

    a_spec = pl.BlockSpec((tk, tm), lambda i, j, k: (k, i)) if ta else pl.BlockSpec((tm, tk), lambda i, j, k: (i, k))
    if b_slots == "n":
        b_spec = pl.BlockSpec((None, tk, tn), lambda i, j, k: (j, k, 0))
    elif b_slots == "k":
        b_spec = pl.BlockSpec((None, tn, tk), lambda i, j, k: (k, j, 0))
    else:
        b_spec = pl.BlockSpec((tn, tk), lambda i, j, k: (j, k)) if tb else pl.BlockSpec((tk, tn), lambda i, j, k: (k, j))
    if out_slots == "m":
        o_spec = pl.BlockSpec((None, None, tm, tn), lambda i, j, k: (i % 2, i // 2, 0, j))
        o_shape = (2, N_CHIP, tm, N)
    elif out_slots == "n":
        o_spec = pl.BlockSpec((None, None, tm, tn), lambda i, j, k: (j % 2, j // 2, i, 0))
        o_shape = (2, N_CHIP, M, tn)
    else:
        o_spec = pl.BlockSpec((tm, tn), lambda i, j, k: (i, j))
        o_shape = (M, N)
    in_specs = [a_spec, b_spec]
    args = [a, b]
    if acc is not None:
        in_specs.append(o_spec)
        args.append(acc)
    if act_in is not None:
        in_specs.append(o_spec)
        args.append(act_in)
    outs, c_outs = _call(
        body, name=name, grid=(M // tm, N // tn, nk), in_specs=in_specs, out_specs=[o_spec],
        out_shape=[jax.ShapeDtypeStruct(o_shape, out_dtype)],
        scratch_shapes=[pltpu.VMEM((tm, tn), F32)] if nk > 1 else [], args=args,
        sem=("parallel", "parallel", "arbitrary"), comm=comm)
    return (outs[0], c_outs) if comm is not None else outs[0]


ROW_TILE = 256


def _rstd(xv):
    return lax.rsqrt(jnp.mean(xv * xv, axis=-1, keepdims=True) + EPS)


def _rms_bwd_rows(dy, xv, g):
    r = _rstd(xv)
    dyg = dy * g
    dx = r * dyg - xv * (r * r * r) * jnp.mean(dyg * xv, axis=-1, keepdims=True)
    return dx, dy * xv * r


def _row_spec(tm, n, col=0):
    return pl.BlockSpec((tm, n), lambda i: (i, col))


def _vec_spec(n):
    return pl.BlockSpec((1, n), lambda i: (0, 0))


def _acc_rows(ref, i, val):
    @pl.when(i == 0)
    def _():
        ref[...] = val

    @pl.when(i > 0)
    def _():
        ref[...] += val


def rms_fwd(x, g, *, name, n=None, col=0, out_dtype=BF16):
    T = x.shape[0]
    n = x.shape[1] if n is None else n
    tm = _pick(T, ROW_TILE)

    def body(x_ref, g_ref, o_ref):
        xv = x_ref[...]
        o_ref[...] = (xv * _rstd(xv) * g_ref[...]).astype(o_ref.dtype)

    return pl.pallas_call(
        body, name=name, grid=(T // tm,), in_specs=[_row_spec(tm, n, col), _vec_spec(n)],
        out_specs=_row_spec(tm, n), out_shape=jax.ShapeDtypeStruct((T, n), out_dtype),
        compiler_params=_cparams(("parallel",)),
    )(x, g)


def rms_bwd(dy, x, g, *, name, n=None, col=0, residual=None, out_dtype=F32):
    T = x.shape[0]
    n = x.shape[1] if n is None else n
    tm = _pick(T, ROW_TILE)

    def body(*refs):
        if residual is None:
            dy_ref, x_ref, g_ref, dx_ref, dg_ref = refs
        else:
            dy_ref, x_ref, g_ref, res_ref, dx_ref, dg_ref = refs
        dx, dgr = _rms_bwd_rows(dy_ref[...].astype(F32), x_ref[...], g_ref[...])
        if residual is not None:
            dx = dx + res_ref[...]
        dx_ref[...] = dx.astype(dx_ref.dtype)
        _acc_rows(dg_ref, pl.program_id(0), jnp.sum(dgr, axis=0, keepdims=True))

    in_specs = [_row_spec(tm, n), _row_spec(tm, n, col), _vec_spec(n)]
    args = [dy, x, g]
    if residual is not None:
        in_specs.append(_row_spec(tm, n))
        args.append(residual)
    return pl.pallas_call(
        body, name=name, grid=(T // tm,), in_specs=in_specs,
        out_specs=(_row_spec(tm, n), _vec_spec(n)),
        out_shape=(jax.ShapeDtypeStruct((T, n), out_dtype), jax.ShapeDtypeStruct((1, n), F32)),
        compiler_params=_cparams(("arbitrary",)),
    )(*args)


def mix_post_fwd(x, y, g_post, g_pre2, *, name):
    T, n = x.shape
    tm = _pick(T, ROW_TILE)

    def body(x_ref, y_ref, gp_ref, g2_ref, x1_ref, h2_ref):
        yv = y_ref[...]
        x1 = x_ref[...] + yv * _rstd(yv) * gp_ref[...]
        x1_ref[...] = x1
        h2_ref[...] = (x1 * _rstd(x1) * g2_ref[...]).astype(h2_ref.dtype)

    return pl.pallas_call(
        body, name=name, grid=(T // tm,),
        in_specs=[_row_spec(tm, n), _row_spec(tm, n), _vec_spec(n), _vec_spec(n)],
        out_specs=(_row_spec(tm, n), _row_spec(tm, n)),
        out_shape=(jax.ShapeDtypeStruct((T, n), F32), jax.ShapeDtypeStruct((T, n), BF16)),
        compiler_params=_cparams(("parallel",)),
    )(x, y, g_post, g_pre2)


def loss_head(x1, d, g, tgt, *, name):
    T, n = x1.shape
    tm = _pick(T, ROW_TILE)

    def body(x1_ref, d_ref, g_ref, t_ref, loss_ref, dx2_ref, dd_ref, dg_ref):
        i = pl.program_id(0)
        dv = d_ref[...]
        gv = g_ref[...]
        err = x1_ref[...] + dv * _rstd(dv) * gv - t_ref[...]
        part = 0.5 * jnp.sum(jnp.mean(err * err, axis=-1, keepdims=True), axis=0, keepdims=True)
        _acc_rows(loss_ref, i, jnp.broadcast_to(part, (1, LANES)))
        dx2 = err * (1.0 / n)
        dx2_ref[...] = dx2
        dd, dgr = _rms_bwd_rows(dx2, dv, gv)
        dd_ref[...] = dd.astype(dd_ref.dtype)
        _acc_rows(dg_ref, i, jnp.sum(dgr, axis=0, keepdims=True))

    return pl.pallas_call(
        body, name=name, grid=(T // tm,),
        in_specs=[_row_spec(tm, n), _row_spec(tm, n), _vec_spec(n), _row_spec(tm, n)],
        out_specs=(_vec_spec(LANES), _row_spec(tm, n), _row_spec(tm, n), _vec_spec(n)),
        out_shape=(jax.ShapeDtypeStruct((1, LANES), F32), jax.ShapeDtypeStruct((T, n), F32),
                   jax.ShapeDtypeStruct((T, n), BF16), jax.ShapeDtypeStruct((1, n), F32)),
        compiler_params=_cparams(("arbitrary",)),
    )(x1, d, g, tgt)


def mlp_pre_bwd(dx2, dh2, x1, g_pre2, y, g_post, *, name):
    T, n = x1.shape
    tm = _pick(T, ROW_TILE)

    def body(dx2_ref, dh2_ref, x1_ref, g2_ref, y_ref, gp_ref, dx1_ref, dy_ref, dg2_ref, dgp_ref):
        i = pl.program_id(0)
        d1, dg2 = _rms_bwd_rows(dh2_ref[...], x1_ref[...], g2_ref[...])
        dx1 = dx2_ref[...] + d1
        dx1_ref[...] = dx1
        dy, dgp = _rms_bwd_rows(dx1, y_ref[...], gp_ref[...])
        dy_ref[...] = dy.astype(dy_ref.dtype)
        _acc_rows(dg2_ref, i, jnp.sum(dg2, axis=0, keepdims=True))
        _acc_rows(dgp_ref, i, jnp.sum(dgp, axis=0, keepdims=True))

    rs = _row_spec(tm, n)
    return pl.pallas_call(
        body, name=name, grid=(T // tm,),
        in_specs=[rs, rs, rs, _vec_spec(n), rs, _vec_spec(n)],
        out_specs=(rs, rs, _vec_spec(n), _vec_spec(n)),
        out_shape=(jax.ShapeDtypeStruct((T, n), F32), jax.ShapeDtypeStruct((T, n), BF16),
                   jax.ShapeDtypeStruct((1, n), F32), jax.ShapeDtypeStruct((1, n), F32)),
        compiler_params=_cparams(("arbitrary",)),
    )(dx2, dh2, x1, g_pre2, y, g_post)


def _gate_specs(tm):
    gcol = PAD_OFF["gate"] // D_MODEL
    lspecs = [pl.BlockSpec((tm, D_MODEL), functools.partial(lambda i, c: (i, c), c=gcol + b)) for b in range(N_BRANCH)]
    bspecs = [pl.BlockSpec((1, D_MODEL), functools.partial(lambda i, c: (0, c), c=b)) for b in range(N_BRANCH)]
    pspecs = [_row_spec(tm, D_MODEL) for _ in range(N_BRANCH)]
    return lspecs, bspecs, pspecs


def gate_merge(proj, b_gate, ps, *, name):
    T = proj.shape[0]
    tm = _pick(T, ROW_TILE)
    lspecs, bspecs, pspecs = _gate_specs(tm)

    def body(*refs):
        l_refs, b_refs, p_refs, o_ref = refs[0:3], refs[3:6], refs[6:9], refs[9]
        tot = None
        for lr, br, pr in zip(l_refs, b_refs, p_refs):
            term = jax.nn.sigmoid(lr[...] + br[...]) * pr[...].astype(F32)
            tot = term if tot is None else tot + term
        o_ref[...] = tot.astype(o_ref.dtype)

    return pl.pallas_call(
        body, name=name, grid=(T // tm,), in_specs=lspecs + bspecs + pspecs,
        out_specs=_row_spec(tm, D_MODEL), out_shape=jax.ShapeDtypeStruct((T, D_MODEL), BF16),
        compiler_params=_cparams(("parallel",)),
    )(proj, proj, proj, b_gate, b_gate, b_gate, *ps)


def gate_bwd(dmerged, proj, b_gate, ps, *, name):
    T = proj.shape[0]
    tm = _pick(T, ROW_TILE)
    lspecs, bspecs, pspecs = _gate_specs(tm)

    def body(*refs):
        dm_ref = refs[0]
        l_refs, b_refs, p_refs = refs[1:4], refs[4:7], refs[7:10]
        dl_ref, dp_refs, db_ref = refs[10], refs[11:14], refs[14]
        i = pl.program_id(0)
        dm = dm_ref[...]
        for b, (lr, br, pr, dpr) in enumerate(zip(l_refs, b_refs, p_refs, dp_refs)):
            gt = jax.nn.sigmoid(lr[...] + br[...])
            dpr[...] = (dm * gt).astype(dpr.dtype)
            dl = dm * pr[...].astype(F32) * gt * (1.0 - gt)
            dl_ref[:, b * D_MODEL:(b + 1) * D_MODEL] = dl.astype(dl_ref.dtype)
            part = jnp.sum(dl, axis=0, keepdims=True)

            @pl.when(i == 0)
            def _():
                db_ref[:, b * D_MODEL:(b + 1) * D_MODEL] = part

            @pl.when(i > 0)
            def _():
                db_ref[:, b * D_MODEL:(b + 1) * D_MODEL] += part

    return pl.pallas_call(
        body, name=name, grid=(T // tm,), in_specs=[_row_spec(tm, D_MODEL)] + lspecs + bspecs + pspecs,
        out_specs=(_row_spec(tm, GATE_W), *[_row_spec(tm, D_MODEL) for _ in range(N_BRANCH)], _vec_spec(GATE_W)),
        out_shape=(jax.ShapeDtypeStruct((T, GATE_W), BF16),
                   *[jax.ShapeDtypeStruct((T, D_MODEL), BF16) for _ in range(N_BRANCH)],
                   jax.ShapeDtypeStruct((1, GATE_W), F32)),
        compiler_params=_cparams(("arbitrary",)),
    )(dmerged, proj, proj, proj, b_gate, b_gate, b_gate, *ps)


def _rope_tables(pos_ref, invf_ref):
    ang = pos_ref[...] * invf_ref[...]
    lane = lax.broadcasted_iota(jnp.int32, (1, LANES), 1)
    s = jnp.sin(ang)
    split = MLA_NOPE + MLA_ROPE // 2
    return jnp.cos(ang), jnp.where(lane >= split, s, 0.0), jnp.where(lane < split, s, 0.0)


def _rotate(xh, c, s_hi, s_lo, sign):
    half = MLA_ROPE // 2
    up = pltpu.roll(xh, half, 1)
    down = pltpu.roll(xh, LANES - half, 1)
    return xh * c + sign * (up * s_hi - down * s_lo)


def rope_fwd(qpre, kpre, proj, pos, invf, *, name):
    T = qpre.shape[0]
    tm = _pick(T, ROW_TILE)
    kr_col = PAD_OFF["kr"] // KR_PAD

    def body(q_ref, k_ref, kr_ref, pos_ref, invf_ref, qo_ref, ko_ref):
        c, s_hi, s_lo = _rope_tables(pos_ref, invf_ref)
        kr = _rotate(pltpu.roll(kr_ref[...], MLA_NOPE, 1), c, s_hi, s_lo, 1.0)
        for h in range(MLA_HEADS):
            sl = slice(h * HEAD_PAD, (h + 1) * HEAD_PAD)
            qo_ref[:, sl] = _rotate(q_ref[:, sl], c, s_hi, s_lo, 1.0).astype(qo_ref.dtype)
            ko_ref[:, sl] = (k_ref[:, sl] + kr).astype(ko_ref.dtype)

    rs = _row_spec(tm, MLA_PAD_W)
    return pl.pallas_call(
        body, name=name, grid=(T // tm,),
        in_specs=[rs, rs, _row_spec(tm, KR_PAD, kr_col), _row_spec(tm, 1), _vec_spec(LANES)],
        out_specs=(rs, rs),
        out_shape=(jax.ShapeDtypeStruct((T, MLA_PAD_W), BF16), jax.ShapeDtypeStruct((T, MLA_PAD_W), BF16)),
        compiler_params=_cparams(("parallel",)),
    )(qpre, kpre, proj, pos, invf)


def rope_bwd(dq_pad, dk_pad, pos, invf, *, name):
    T = dq_pad.shape[0]
    tm = _pick(T, ROW_TILE)

    def body(dq_ref, dk_ref, pos_ref, invf_ref, dqo_ref, dkr_ref):
        c, s_hi, s_lo = _rope_tables(pos_ref, invf_ref)
        tot = None
        for h in range(MLA_HEADS):
            sl = slice(h * HEAD_PAD, (h + 1) * HEAD_PAD)
            dqo_ref[:, sl] = _rotate(dq_ref[:, sl], c, s_hi, s_lo, -1.0).astype(dqo_ref.dtype)
            tot = dk_ref[:, sl] if tot is None else tot + dk_ref[:, sl]
        dkr = pltpu.roll(_rotate(tot, c, s_hi, s_lo, -1.0), LANES - MLA_NOPE, 1)
        lane = lax.broadcasted_iota(jnp.int32, (1, LANES), 1)
        dkr_ref[...] = jnp.where(lane < MLA_ROPE, dkr, 0.0).astype(dkr_ref.dtype)

    rs = _row_spec(tm, MLA_PAD_W)
    return pl.pallas_call(
        body, name=name, grid=(T // tm,),
        in_specs=[rs, rs, _row_spec(tm, 1), _vec_spec(LANES)],
        out_specs=(rs, _row_spec(tm, KR_PAD)),
        out_shape=(jax.ShapeDtypeStruct((T, MLA_PAD_W), BF16), jax.ShapeDtypeStruct((T, KR_PAD), BF16)),
        compiler_params=_cparams(("parallel",)),
    )(dq_pad, dk_pad, pos, invf)


ATT_TILE = 512
NEG = -1e30


def _split_bf16(v):
    hi = v.astype(BF16)
    return hi, (v - hi.astype(F32)).astype(BF16)


def _mul(x, s):
    return x if s == 1.0 else x * s


def _dot(a, b, ca, cb):
    return lax.dot_general(a, b, (((ca,), (cb,)), ((), ())), preferred_element_type=F32)


CUM_CHUNK = 256


def _tri(kind, terms=2):
    r = lax.broadcasted_iota(jnp.int32, (CUM_CHUNK, CUM_CHUNK), 0)
    c = lax.broadcasted_iota(jnp.int32, (CUM_CHUNK, CUM_CHUNK), 1)
    m = {"gt": r > c, "le": r <= c, "lt": r < c}[kind]
    t = jnp.where(m, 1.0, 0.0).astype(BF16)
    return jnp.concatenate([t] * terms, axis=0)


def _cum(v, tri, kind):
    n = v.shape[1] // CUM_CHUNK
    two = tri.shape[0] == 2 * CUM_CHUNK
    chunks = [v[:, c * CUM_CHUNK:(c + 1) * CUM_CHUNK] for c in range(n)]
    totals = [jnp.sum(ch, axis=1, keepdims=True) for ch in chunks] if n > 1 else None
    outs = []
    for c, ch in enumerate(chunks):
        r = _dot(jnp.concatenate(_split_bf16(ch), axis=1) if two else ch.astype(BF16), tri, 1, 0)
        others = [] if n == 1 else totals[c + 1:] if kind == "gt" else totals[:c]
        for t in others:
            r = r + t
        outs.append(r)
    return outs[0] if n == 1 else jnp.concatenate(outs, axis=1)


def _lane_masks(hp, w):
    lane = lax.broadcasted_iota(jnp.int32, (1, LANES), 1)
    return [(lane >= e * w) & (lane < (e + 1) * w) for e in range(hp)]


def _att_dims(mode, hp, qk_w, v_w):
    assert mode in ("softmax", "sb")
    assert (hp, qk_w, v_w) in ((2, 128, 64), (2, 64, 64), (1, 128, 128))
    qw = hp * LANES if qk_w == LANES else LANES
    return qw


def attn_fwd(q, k, v, *, name, mode, causal, hp, qk_w, v_w, scale, nh, qc, kc, vc, comm=None):
    qw = _att_dims(mode, hp, qk_w, v_w)
    B, S = q.shape[0], q.shape[1]
    Sk = k.shape[1]
    tq = _pick(S, ATT_TILE)
    tk = _pick(Sk, ATT_TILE)
    if causal:
        assert tq == tk and S == Sk
    nq, nkv = S // tq, Sk // tk
    pre_scaled = math.frexp(scale)[0] == 0.5
    post = 1.0 if pre_scaled else scale

    def body(q_ref, k_ref, v_ref, o_ref, st_ref):
        i = pl.program_id(2)
        qv = q_ref[0].astype(BF16)
        masks_qk = _lane_masks(hp, qk_w) if qk_w < LANES else None
        masks_v = _lane_masks(hp, v_w) if v_w < LANES else None
        if qk_w == LANES:
            qs = [qv[:, e * LANES:(e + 1) * LANES] for e in range(hp)]
        else:
            qs = [jnp.where(masks_qk[e], qv, jnp.zeros_like(qv)) for e in range(hp)]
        if pre_scaled:
            qs = [t * scale for t in qs]
        rows = lax.broadcasted_iota(jnp.int32, (tq, tk), 0)
        cols = lax.broadcasted_iota(jnp.int32, (tq, tk), 1)
        tri = _tri("gt") if mode == "sb" else None

        def kv_tile(j):
            off = pl.multiple_of(j * tk, tk)
            kj = k_ref[0, pl.ds(off, tk), :].astype(BF16)
            vj = v_ref[0, pl.ds(off, tk), :].astype(BF16)
            ks = [kj[:, e * LANES:(e + 1) * LANES] for e in range(hp)] if qk_w == LANES else [kj] * hp
            return ks, vj

        def merge(vals):
            if hp == 1:
                return jnp.broadcast_to(vals[0], (tq, LANES))
            return jnp.where(masks_v[0], vals[0], vals[1])

        if mode == "softmax":
            def block(j, carry, diag):
                ms, ls, acc = carry
                ks, vj = kv_tile(j)
                new_m, new_l, alphas, pvs = [], [], [], []
                for e in range(hp):
                    s = _mul(_dot(qs[e], ks[e], 1, 1), post)
                    if diag:
                        s = jnp.where(rows >= cols, s, NEG)
                    m_new = jnp.maximum(ms[e], jnp.max(s, axis=1, keepdims=True))
                    alpha = jnp.exp(ms[e] - m_new)
                    p = jnp.exp(s - m_new)
                    new_l.append(alpha * ls[e] + jnp.sum(p, axis=1, keepdims=True))
                    new_m.append(m_new)
                    alphas.append(alpha)
                    pvs.append(_dot(p.astype(BF16), vj, 1, 0))
                acc = acc * merge(alphas) + merge(pvs)
                return tuple(new_m), tuple(new_l), acc

            init = (tuple(jnp.full((tq, 1), NEG, F32) for _ in range(hp)),
                    tuple(jnp.zeros((tq, 1), F32) for _ in range(hp)),
                    jnp.zeros((tq, LANES), F32))
            if causal:
                carry = lax.fori_loop(0, i, lambda j, c: block(j, c, False), init)
                ms, ls, acc = block(i, carry, True)
            else:
                ms, ls, acc = lax.fori_loop(0, nkv, lambda j, c: block(j, c, False), init)
            o_ref[0] = acc / merge(list(ls))
            st_ref[0, 0] = merge([m + jnp.log(l) for m, l in zip(ms, ls)])
        else:
            def block(j, carry, diag):
                cs_, acc = carry
                ks, vj = kv_tile(j)
                new_c, pvs = [], []
                for e in range(hp):
                    z = _mul(_dot(qs[e], ks[e], 1, 1), post)
                    lk = -jnp.maximum(z, 0.0) - jnp.log(1.0 + jnp.exp(-jnp.abs(z)))
                    lz = lk + z
                    if diag:
                        lk = jnp.where(rows > cols, lk, 0.0)
                    a = jnp.exp(lz + _cum(lk, tri, "gt") + cs_[e])
                    if diag:
                        a = jnp.where(rows > cols, a, 0.0)
                    pvs.append(_dot(a.astype(BF16), vj, 1, 0))
                    new_c.append(cs_[e] + jnp.sum(lk, axis=1, keepdims=True))
                return tuple(new_c), acc + merge(pvs)

            init = (tuple(jnp.zeros((tq, 1), F32) for _ in range(hp)), jnp.zeros((tq, LANES), F32))
            carry = block(i, init, True)
            cs_, acc = lax.fori_loop(0, i, lambda jj, c: block(i - 1 - jj, c, False), carry)
            o_ref[0] = acc
            st_ref[0, 0] = merge(list(cs_))

    outs, c_outs = _call(
        body, name=name, grid=(B, nh, nq),
        in_specs=[pl.BlockSpec((1, tq, qw), lambda b, h, i: (b, i, qc + h)),
                  pl.BlockSpec((1, Sk, qw), lambda b, h, i: (b, 0, kc + h)),
                  pl.BlockSpec((1, Sk, LANES), lambda b, h, i: (b, 0, vc + h))],
        out_specs=(pl.BlockSpec((1, tq, LANES), lambda b, h, i: (b, i, h)),
                   pl.BlockSpec((1, 1, tq, LANES), lambda b, h, i: (b, h, i, 0))),
        out_shape=(jax.ShapeDtypeStruct((B, S, nh * LANES), F32), jax.ShapeDtypeStruct((B, nh, S, LANES), F32)),
        scratch_shapes=[], args=(q, k, v), sem=("parallel", "parallel", "arbitrary"), comm=comm)
    return (*outs, c_outs) if comm is not None else tuple(outs)


def attn_bwd(q, k, v, o, st, do, *, name, mode, causal, hp, qk_w, v_w, scale, nh, qc, kc, vc, comm=None):
    qw = _att_dims(mode, hp, qk_w, v_w)
    B, S = q.shape[0], q.shape[1]
    Sk = k.shape[1]
    tq = _pick(S, ATT_TILE)
    tk = _pick(Sk, ATT_TILE)
    if causal:
        assert tq == tk and S == Sk
    nq, nkv = S // tq, Sk // tk
    pre_scaled = math.frexp(scale)[0] == 0.5
    post = 1.0 if pre_scaled else scale

    def body(q_ref, k_ref, v_ref, o_ref, st_ref, do_ref, dq_ref, dk_ref, dv_ref):
        i = pl.program_id(2)

        @pl.when(i == 0)
        def _():
            dk_ref[...] = jnp.zeros_like(dk_ref)
            dv_ref[...] = jnp.zeros_like(dv_ref)

        qv = q_ref[0].astype(BF16)
        dov = do_ref[0]
        stv = st_ref[0, 0]
        masks_qk = _lane_masks(hp, qk_w) if qk_w < LANES else None
        masks_v = _lane_masks(hp, v_w) if v_w < LANES else None
        if qk_w == LANES:
            qs = [qv[:, e * LANES:(e + 1) * LANES] for e in range(hp)]
        else:
            qs = [jnp.where(masks_qk[e], qv, jnp.zeros_like(qv)) for e in range(hp)]
        if pre_scaled:
            qs = [t * scale for t in qs]
        if hp == 1:
            dos = [dov.astype(BF16)]
            stats = [stv[:, 0:1]]
        else:
            dos = [jnp.where(masks_v[e], dov, 0.0).astype(BF16) for e in range(hp)]
            stats = [stv[:, e * v_w:e * v_w + 1] for e in range(hp)]
        if mode == "softmax":
            prod = dov * o_ref[0]
            if hp == 1:
                dsum = [jnp.sum(prod, axis=1, keepdims=True)]
            else:
                dsum = [jnp.sum(jnp.where(masks_v[e], prod, 0.0), axis=1, keepdims=True) for e in range(hp)]
        rows = lax.broadcasted_iota(jnp.int32, (tq, tk), 0)
        cols = lax.broadcasted_iota(jnp.int32, (tq, tk), 1)
        if mode == "sb":
            tri_le, tri_lt = _tri("le"), _tri("lt", terms=1)

        def kv_tile(j):
            off = pl.multiple_of(j * tk, tk)
            kj = k_ref[0, pl.ds(off, tk), :].astype(BF16)
            vj = v_ref[0, pl.ds(off, tk), :].astype(BF16)
            ks = [kj[:, e * LANES:(e + 1) * LANES] for e in range(hp)] if qk_w == LANES else [kj] * hp
            return off, ks, vj

        def scatter(off, dz_list, p_list):
            dvj = None
            for e in range(hp):
                t = _dot(p_list[e], dos[e], 0, 0)
                dvj = t if dvj is None else dvj + t
            dv_ref[0, pl.ds(off, tk), :] += dvj
            if qk_w == LANES:
                for e in range(hp):
                    dk_ref[0, pl.ds(off, tk), e * LANES:(e + 1) * LANES] += _dot(dz_list[e], qs[e], 0, 0)
            else:
                dkj = None
                for e in range(hp):
                    t = _dot(dz_list[e], qs[e], 0, 0)
                    dkj = t if dkj is None else dkj + t
                dk_ref[0, pl.ds(off, tk), :] += dkj

        def dq_add(dqs, dz_list, ks):
            out = []
            for e in range(hp):
                out.append(dqs[e] + _dot(dz_list[e], ks[e], 1, 0))
            return tuple(out)

        dq0 = tuple(jnp.zeros((tq, LANES), F32) for _ in range(hp))

        if mode == "softmax":
            def block(j, dqs, diag):
                off, ks, vj = kv_tile(j)
                dzs, ps = [], []
                for e in range(hp):
                    s = _mul(_dot(qs[e], ks[e], 1, 1), post)
                    p = jnp.exp(s - stats[e])
                    if diag:
                        p = jnp.where(rows >= cols, p, 0.0)
                    dp = _dot(dos[e], vj, 1, 1)
                    dzs.append(_mul(p * (dp - dsum[e]), post).astype(BF16))
                    ps.append(p.astype(BF16))
                scatter(off, dzs, ps)
                return dq_add(dqs, dzs, ks)

            if causal:
                dqs = lax.fori_loop(0, i, lambda j, c: block(j, c, False), dq0)
                dqs = block(i, dqs, True)
            else:
                dqs = lax.fori_loop(0, nkv, lambda j, c: block(j, c, False), dq0)
        else:
            def block(j, carry, diag):
                dqs, cps, cgs = carry
                off, ks, vj = kv_tile(j)
                dzs, ps, new_p, new_g = [], [], [], []
                for e in range(hp):
                    z = _mul(_dot(qs[e], ks[e], 1, 1), post)
                    lk = -jnp.maximum(z, 0.0) - jnp.log(1.0 + jnp.exp(-jnp.abs(z)))
                    lz = lk + z
                    sig, keep = jnp.exp(lz), jnp.exp(lk)
                    if diag:
                        lk = jnp.where(rows > cols, lk, 0.0)
                    keep_after = stats[e] - cps[e] - _cum(lk, tri_le, "le")
                    a = jnp.exp(lz + keep_after)
                    if diag:
                        a = jnp.where(rows > cols, a, 0.0)
                    g = _dot(dos[e], vj, 1, 1) * a
                    gsum = cgs[e] + _cum(g, tri_lt, "lt")
                    dz = _mul(g * keep - gsum * sig, post)
                    if diag:
                        dz = jnp.where(rows > cols, dz, 0.0)
                    dzs.append(dz.astype(BF16))
                    ps.append(a.astype(BF16))
                    new_p.append(cps[e] + jnp.sum(lk, axis=1, keepdims=True))
                    new_g.append(cgs[e] + jnp.sum(g, axis=1, keepdims=True))
                scatter(off, dzs, ps)
                return dq_add(dqs, dzs, ks), tuple(new_p), tuple(new_g)

            zc = tuple(jnp.zeros((tq, 1), F32) for _ in range(hp))
            carry = lax.fori_loop(0, i, lambda j, c: block(j, c, False), (dq0, zc, zc))
            dqs, _, _ = block(i, carry, True)

        if pre_scaled:
            dqs = [t * scale for t in dqs]
        if qk_w == LANES:
            for e in range(hp):
                dq_ref[0, :, e * LANES:(e + 1) * LANES] = dqs[e]
        else:
            dq_ref[0] = jnp.where(masks_qk[0], dqs[0], dqs[1])

    outs, c_outs = _call(
        body, name=name, grid=(B, nh, nq),
        in_specs=[pl.BlockSpec((1, tq, qw), lambda b, h, i: (b, i, qc + h)),
                  pl.BlockSpec((1, Sk, qw), lambda b, h, i: (b, 0, kc + h)),
                  pl.BlockSpec((1, Sk, LANES), lambda b, h, i: (b, 0, vc + h)),
                  pl.BlockSpec((1, tq, LANES), lambda b, h, i: (b, i, h)),
                  pl.BlockSpec((1, 1, tq, LANES), lambda b, h, i: (b, h, i, 0)),
                  pl.BlockSpec((1, tq, LANES), lambda b, h, i: (b, i, h))],
        out_specs=(pl.BlockSpec((1, tq, qw), lambda b, h, i: (b, i, h)),
                   pl.BlockSpec((1, Sk, qw), lambda b, h, i: (b, 0, h)),
                   pl.BlockSpec((1, Sk, LANES), lambda b, h, i: (b, 0, h))),
        out_shape=(jax.ShapeDtypeStruct((B, S, nh * qw), F32), jax.ShapeDtypeStruct((B, Sk, nh * qw), F32),
                   jax.ShapeDtypeStruct((B, Sk, nh * LANES), F32)),
        scratch_shapes=[], args=(q, k, v, o, st, do), sem=("parallel", "parallel", "arbitrary"), comm=comm)
    return (*outs, c_outs) if comm is not None else tuple(outs)


def _pos(p):
    return 4 * p[0] + 2 * p[1] + p[2]


N_CHIP = 4
MESH_ID = pl.DeviceIdType.MESH
_ANY = pl.BlockSpec(memory_space=pl.ANY)


def _me():
    return lax.axis_index("x"), lax.axis_index("y"), lax.axis_index("c")


def _other_chips(x, y):
    return [(1 - x, y), (x, 1 - y), (1 - x, 1 - y)]


class _Comm:
    def __init__(self, arrs, out_shapes, n_sem, n_local, start, finish):
        self.arrs, self.out_shapes, self.start, self.finish = list(arrs), list(out_shapes), start, finish
        self.scratch = [pltpu.SemaphoreType.DMA((n_sem,)), pltpu.SemaphoreType.DMA((n_sem,)),
                        pltpu.SemaphoreType.DMA((max(n_local, 1),))]


def _run_comm(comm, name):
    n = len(comm.arrs)

    def body(*refs):
        r = (refs[:n], refs[n:2 * n], refs[2 * n], refs[2 * n + 1], refs[2 * n + 2])
        comm.start(*r)
        comm.finish(*r)

    return pl.pallas_call(
        body, name=name, in_specs=[_ANY] * n, out_specs=[_ANY] * n, out_shape=comm.out_shapes,
        scratch_shapes=comm.scratch, compiler_params=pltpu.CompilerParams(has_side_effects=True),
    )(*comm.arrs)


def _call(body, *, name, grid, in_specs, out_specs, out_shape, scratch_shapes, args, sem, comm=None):
    in_specs, out_specs, out_shape = list(in_specs), list(out_specs), list(out_shape)
    if comm is None:
        res = pl.pallas_call(body, name=name, grid=grid, in_specs=in_specs, out_specs=out_specs, out_shape=out_shape,
                             scratch_shapes=scratch_shapes, compiler_params=_cparams(sem))(*args)
        return list(res), []
    n_in, n_out, n_scr, nc = len(in_specs), len(out_specs), len(scratch_shapes), len(comm.arrs)

    def wrapped(*refs):
        ins, refs = refs[:n_in], refs[n_in:]
        c_in, refs = refs[:nc], refs[nc:]
        outs, refs = refs[:n_out], refs[n_out:]
        c_out, refs = refs[:nc], refs[nc:]
        scr, sems = refs[:n_scr], refs[n_scr:]
        ids = [pl.program_id(a) for a in range(len(grid))]
        first = functools.reduce(jnp.logical_and, [i == 0 for i in ids])
        last = functools.reduce(jnp.logical_and, [i == g - 1 for i, g in zip(ids, grid)])

        @pl.when(first)
        def _():
            comm.start(c_in, c_out, *sems)

        body(*ins, *outs, *scr)

        @pl.when(last)
        def _():
            comm.finish(c_in, c_out, *sems)

    res = pl.pallas_call(
        wrapped, name=name, grid=grid, in_specs=in_specs + [_ANY] * nc, out_specs=out_specs + [_ANY] * nc,
        out_shape=out_shape + comm.out_shapes, scratch_shapes=list(scratch_shapes) + comm.scratch,
        compiler_params=pltpu.CompilerParams(dimension_semantics=("arbitrary",) * len(grid),
                                             vmem_limit_bytes=VMEM_LIMIT, has_side_effects=True),
    )(*args, *comm.arrs)
    return list(res[:n_out]), list(res[n_out:])


def gather_two_level(bufs, *, name=None):
    n = len(bufs)

    def parts(x_refs, out_refs, send_sems, recv_sems, local_sems):
        x, y, c = _me()
        me, sibling = (x, y, c), (x, y, 1 - c)
        chips = _other_chips(x, y)

        def copy(a, k, block, to, from_input=False):
            dst = out_refs[a].at[_pos(block)]
            return pltpu.make_async_remote_copy(src_ref=x_refs[a] if from_input else dst, dst_ref=dst,
                                                send_sem=send_sems.at[7 * a + k], recv_sem=recv_sems.at[7 * a + k],
                                                device_id=to, device_id_type=MESH_ID)

        mine = [pltpu.make_async_copy(x_refs[a], out_refs[a].at[_pos(me)], local_sems.at[a]) for a in range(n)]
        first = []
        for a in range(n):
            first.append(copy(a, 0, me, sibling, from_input=True))
            first += [copy(a, 1 + j, me, (*chip, c), from_input=True) for j, chip in enumerate(chips)]
        return copy, mine, first, me, sibling, chips, c

    def start(*refs):
        _, mine, first, *_ = parts(*refs)
        for cp in mine + first:
            cp.start()

    def finish(*refs):
        copy, mine, first, me, sibling, chips, c = parts(*refs)
        passed = []
        for j, chip in enumerate(chips):
            for a in range(n):
                copy(a, 1 + j, (*chip, c), me).wait_recv()
                passed.append(copy(a, 4 + j, (*chip, c), sibling))
                passed[-1].start()
        for a in range(n):
            copy(a, 0, sibling, me).wait_recv()
            for j, chip in enumerate(chips):
                copy(a, 4 + j, (*chip, 1 - c), me).wait_recv()
        for cp in first + passed:
            cp.wait_send()
        for cp in mine:
            cp.wait()

    out_shapes = [jax.ShapeDtypeStruct((N_DEV,) + b.shape, b.dtype) for b in bufs]
    comm = _Comm(bufs, out_shapes, 7 * n, n, start, finish)
    return _run_comm(comm, name) if name else comm


def sibling_exchange(sends, *, name=None):
    n = len(sends)

    def copies(s_refs, out_refs, send_sems, recv_sems, local_sems):
        x, y, c = _me()
        return [pltpu.make_async_remote_copy(src_ref=s_refs[a].at[1 - c], dst_ref=out_refs[a], send_sem=send_sems.at[a],
                                             recv_sem=recv_sems.at[a], device_id=(x, y, 1 - c), device_id_type=MESH_ID)
                for a in range(n)]

    def start(*refs):
        for cp in copies(*refs):
            cp.start()

    def finish(*refs):
        for cp in copies(*refs):
            cp.wait()

    out_shapes = [jax.ShapeDtypeStruct(s.shape[1:], s.dtype) for s in sends]
    comm = _Comm(sends, out_shapes, n, 0, start, finish)
    return _run_comm(comm, name) if name else comm


def _flat2(shape):
    return math.prod(shape[:-1]), shape[-1]


def _row_tile(r):
    return r if r <= 512 else _pick8(r, 256)


def _pick8(n, pref):
    t = pref
    while n % t:
        t -= 8
    return t


def chip_sum(send, got, core, *, name):
    shape = got.shape[1:]
    r, cdim = _flat2(shape)
    tr = _row_tile(r)
    send = send.reshape(2, N_CHIP, r, cdim)
    got = got.reshape(N_CHIP, r, cdim)

    def body(core_ref, s_ref, g_ref, o_ref):
        o_ref[...] = (s_ref[0].astype(F32) + g_ref[...].astype(F32)).astype(o_ref.dtype)

    blk = pl.BlockSpec((N_CHIP, tr, cdim), lambda i, core_ref: (0, i, 0))
    out = pl.pallas_call(
        body, name=name,
        grid_spec=pltpu.PrefetchScalarGridSpec(
            num_scalar_prefetch=1, grid=(r // tr,),
            in_specs=[pl.BlockSpec((1, N_CHIP, tr, cdim), lambda i, core_ref: (core_ref[0], 0, i, 0)), blk],
            out_specs=blk),
        out_shape=jax.ShapeDtypeStruct((N_CHIP, r, cdim), BF16),
        compiler_params=_cparams(("parallel",)),
    )(core, send, got)
    return out.reshape((N_CHIP,) + shape)


def chip_exchange(sums, *, name=None):
    n = len(sums)

    def copies(s_refs, out_refs, send_sems, recv_sems, local_sems):
        x, y, c = _me()
        mine = 2 * x + y
        chips = _other_chips(x, y)
        local = [pltpu.make_async_copy(s_refs[a].at[mine], out_refs[a].at[mine], local_sems.at[a]) for a in range(n)]
        sends, recvs = [], []
        for a in range(n):
            for j, (px, py) in enumerate(chips):
                sems = dict(send_sem=send_sems.at[3 * a + j], recv_sem=recv_sems.at[3 * a + j], device_id=(px, py, c),
                            device_id_type=MESH_ID)
                sends.append(pltpu.make_async_remote_copy(src_ref=s_refs[a].at[2 * px + py], dst_ref=out_refs[a].at[mine], **sems))
                recvs.append(pltpu.make_async_remote_copy(src_ref=s_refs[a].at[mine], dst_ref=out_refs[a].at[2 * px + py], **sems))
        return local, sends, recvs

    def start(*refs):
        local, sends, _ = copies(*refs)
        for cp in local + sends:
            cp.start()

    def finish(*refs):
        local, sends, recvs = copies(*refs)
        for cp in recvs:
            cp.wait_recv()
        for cp in sends:
            cp.wait_send()
        for cp in local:
            cp.wait()

    out_shapes = [jax.ShapeDtypeStruct(s.shape, s.dtype) for s in sums]
    comm = _Comm(sums, out_shapes, 3 * n, n, start, finish)
    return _run_comm(comm, name) if name else comm


def sum_adamw(parts, w, m, v, *, name):
    shape = w.shape
    R, cdim = _flat2(shape)
    n_slots = parts.shape[0]
    tr = _row_tile(R)
    parts = parts.reshape(n_slots, R, cdim)
    w, m, v = (t.reshape(1, R, cdim) for t in (w, m, v))
    c1 = 1.0 - ADAM_B1 ** ADAM_STEP
    c2 = 1.0 - ADAM_B2 ** ADAM_STEP

    def body(p_ref, w_ref, m_ref, v_ref, g_ref, d_ref, mo_ref, vo_ref):
        g = p_ref[0].astype(F32)
        for s in range(1, n_slots):
            g = g + p_ref[s].astype(F32)
        mn = ADAM_B1 * m_ref[...] + (1.0 - ADAM_B1) * g
        vn = ADAM_B2 * v_ref[...] + (1.0 - ADAM_B2) * jnp.square(g)
        g_ref[...] = g
        mo_ref[...] = mn
        vo_ref[...] = vn
        d_ref[...] = -ADAM_LR * ((mn / c1) / (jnp.sqrt(vn / c2) + ADAM_EPS) + ADAM_WD * w_ref[...])

    rs = pl.BlockSpec((None, tr, cdim), lambda i: (0, i, 0))
    sd = jax.ShapeDtypeStruct((1, R, cdim), F32)
    res = pl.pallas_call(
        body, name=name, grid=(R // tr,),
        in_specs=[pl.BlockSpec((n_slots, tr, cdim), lambda i: (0, i, 0)), rs, rs, rs],
        out_specs=(rs, rs, rs, rs), out_shape=(sd, sd, sd, sd),
        compiler_params=_cparams(("parallel",)),
    )(parts, w, m, v)
    return [t.reshape(shape) for t in res]


WEIGHTS = ("ln_mix_pre", "w_in", "b_gate", "q_norm", "w_uq", "kv_norm", "w_uk", "w_uv", "mem_norm", "w_mem_kv",
           "w_branch_out", "w_out", "ln_mix_post", "ln_mlp_pre", "w_mlp_up", "w_mlp_down", "ln_mlp_post")
BIG = dict(w_in=((D_MODEL, IN_WIDTH), 1), w_uq=((Q_LORA, MLA_HEADS * (MLA_NOPE + MLA_ROPE)), 1),
           w_uk=((KV_LORA, MLA_HEADS * MLA_NOPE), 1), w_uv=((KV_LORA, MLA_HEADS * MLA_V), 1),
           w_mem_kv=((D_MODEL, 2 * QM_W), 0), w_branch_out=((N_BRANCH, BRANCH_W, D_MODEL), 2),
           w_out=((D_MODEL, D_MODEL), 0), w_mlp_up=((D_MODEL, D_FF), 1), w_mlp_down=((D_FF, D_MODEL), 0))
SMALL = tuple(n for n in WEIGHTS if n not in BIG)


def _shard_shape(name):
    full, ax = BIG[name]
    return tuple(s // N_DEV if a == ax else s for a, s in enumerate(full))


PACK_TILE = 512


def _pad_rows(a):
    r = a.shape[-2]
    to = PACK_TILE if r > PACK_TILE else 8
    pad = [(0, 0)] * a.ndim
    pad[-2] = (0, (-r) % to)
    return jnp.pad(a, pad)


def _pack_rows(arrs):
    return _pad_rows(jnp.concatenate([a.reshape(-1, LANES) for a in arrs], axis=0))


def _unpack_rows(packed, shapes, lead=()):
    out, r = [], 0
    for shp in shapes:
        n = math.prod(shp) // LANES
        out.append(packed[..., r:r + n, :].reshape(lead + tuple(shp)))
        r += n
    return out


def _slots_to_full(name, slots):
    full, ax = BIG[name]
    return jnp.moveaxis(slots, 0, ax).reshape(full)


def _full_to_slots(name, w):
    full, ax = BIG[name]
    split = full[:ax] + (N_DEV, full[ax] // N_DEV) + full[ax + 1:]
    return jnp.moveaxis(w.reshape(split), ax, 0)


def _full_to_owner(name, w):
    s = _full_to_slots(name, w)
    return jnp.swapaxes(s.reshape((N_CHIP, 2) + s.shape[1:]), 0, 1)


def _split_in(w):
    offs = np.cumsum((0,) + IN_SIZES)
    names = ("cq", "ckv", "kr", "sb", "qm", "gate")
    return {n: w[:, offs[i]:offs[i + 1]] for i, n in enumerate(names)}


def _pad_in(w):
    p = _split_in(w)
    p["kr"] = jnp.pad(p["kr"], ((0, 0), (0, KR_PAD - MLA_ROPE)))
    p["zpad"] = jnp.zeros((w.shape[0], PAD_SIZES["zpad"]), w.dtype)
    return jnp.concatenate([p[n] for n in PAD_ORDER], axis=1)


def _unpad_in(wp):
    p = {n: wp[:, PAD_OFF[n]:PAD_OFF[n] + PAD_SIZES[n]] for n in PAD_ORDER}
    p["kr"] = p["kr"][:, :MLA_ROPE]
    return jnp.concatenate([p[n] for n in ("cq", "ckv", "kr", "sb", "qm", "gate")], axis=1)


def _in_segments():
    orig = dict(zip(("cq", "ckv", "kr", "sb", "qm", "gate"), zip(np.cumsum((0,) + IN_SIZES[:-1]).tolist(), IN_SIZES)))
    shard = IN_WIDTH // N_DEV
    segs = []
    for n in PAD_ORDER:
        if n not in orig:
            continue
        o0, w = orig[n]
        c = o0
        while c < o0 + w:
            s = c // shard
            e = min(o0 + w, (s + 1) * shard)
            segs.append((s, c - s * shard, PAD_OFF[n] + (c - o0), e - c))
            c = e
    return segs


def in_slots_to_pad(slots, *, name):
    _, R, shard = slots.shape
    tr = _pick8(R, ROW_TILE)
    segs = _in_segments()

    def body(in_ref, out_ref):
        out_ref[...] = jnp.zeros_like(out_ref)
        for s, a, d, w in segs:
            out_ref[:, d:d + w] = in_ref[s, :, a:a + w]

    return pl.pallas_call(
        body, name=name, grid=(R // tr,), in_specs=[pl.BlockSpec((N_DEV, tr, shard), lambda i: (0, i, 0))],
        out_specs=pl.BlockSpec((tr, IN_PAD), lambda i: (i, 0)), out_shape=jax.ShapeDtypeStruct((R, IN_PAD), slots.dtype),
        compiler_params=_cparams(("parallel",)),
    )(slots)


def in_pad_to_owner(g, *, name):
    R = g.shape[0]
    shard = IN_WIDTH // N_DEV
    tr = _pick8(R, ROW_TILE)
    segs = _in_segments()

    def body(in_ref, out_ref):
        for s, a, d, w in segs:
            out_ref[s % 2, s // 2, :, a:a + w] = in_ref[:, d:d + w]

    return pl.pallas_call(
        body, name=name, grid=(R // tr,), in_specs=[pl.BlockSpec((tr, IN_PAD), lambda i: (i, 0))],
        out_specs=pl.BlockSpec((2, N_CHIP, tr, shard), lambda i: (0, 0, i, 0)),
        out_shape=jax.ShapeDtypeStruct((2, N_CHIP, R, shard), g.dtype),
        compiler_params=_cparams(("parallel",)),
    )(g)


def _pad_heads(w, width):
    r = w.shape[0]
    return jnp.pad(w.reshape(r, MLA_HEADS, width), ((0, 0), (0, 0), (0, HEAD_PAD - width))).reshape(r, MLA_PAD_W)


def _unpad_heads(wp, width):
    r = wp.shape[0]
    return wp.reshape(r, MLA_HEADS, HEAD_PAD)[:, :, :width].reshape(r, MLA_HEADS * width)


def _rope_inv_freq():
    half = MLA_ROPE // 2
    inv = 1.0 / (ROPE_THETA ** (jnp.arange(half, dtype=F32) * (2.0 / MLA_ROPE)))
    tab = jnp.zeros((LANES,), F32)
    tab = tab.at[MLA_NOPE:MLA_NOPE + half].set(inv).at[MLA_NOPE + half:MLA_NOPE + MLA_ROPE].set(inv)
    return tab.reshape(1, LANES)


def _as_tuple(r):
    return r if isinstance(r, tuple) else (r,)


class _Exchange:
    def __init__(self, w):
        self.w = w
        self.rest = tuple(n for n in BIG if n != "w_in")
        self.pending = {}
        self.reduced = {}

    def first_weights(self):
        (s,) = gather_two_level([self.w["w_in"].astype(BF16)], name="gather_w_in")
        return in_slots_to_pad(s, name="w_in_layout")

    def rest_comm(self):
        return gather_two_level([self.w[n].astype(BF16) for n in self.rest])

    def rest_weights(self, slots):
        return {n: s if n == "w_mlp_up" else _slots_to_full(n, s) for n, s in zip(self.rest, slots)}

    def reduce_start(self, tag, grads):
        names = tuple(grads)
        sends = [grads[n] for n in names]
        gots = sibling_exchange(sends, name=f"sibling_{tag}")
        core = lax.axis_index("c").astype(jnp.int32).reshape(1)
        sums = [chip_sum(s, g, core, name=f"chip_sum_{n}") for n, s, g in zip(names, sends, gots)]
        self.pending[tag] = names
        return chip_exchange(sums)

    def reduce_done(self, tag, recvs):
        self.reduced.update(zip(self.pending.pop(tag), recvs))


def _local_step(x, mem, positions, tgt, sm, ex):
    B, S, D = x.shape
    M = mem.shape[1]
    T = B * S
    x2 = x.reshape(T, D)
    mem2 = mem.reshape(B * M, D)
    pos = positions.reshape(T, 1).astype(F32)
    invf = _rope_inv_freq()
    w_in_pad = ex.first_weights()
    cq_col, ckv_col = PAD_OFF["cq"] // Q_LORA, PAD_OFF["ckv"] // KV_LORA
    sb_col, qm_col = PAD_OFF["sb"] // LANES, PAD_OFF["qm"] // LANES
    sb_blk = SB_HEADS * SB_DIM // LANES
    mla = dict(mode="softmax", causal=True, hp=2, qk_w=128, v_w=64, scale=(MLA_NOPE + MLA_ROPE) ** -0.5,
               nh=MLA_HEADS // 2, qc=0, kc=0, vc=0)
    sbk = dict(mode="sb", causal=True, hp=2, qk_w=64, v_w=64, scale=SB_DIM ** -0.5, nh=SB_HEADS // 2,
               qc=sb_col, kc=sb_col + sb_blk, vc=sb_col + 2 * sb_blk)
    mca = dict(mode="softmax", causal=False, hp=1, qk_w=128, v_w=128, scale=MEM_DIM ** -0.5, nh=MEM_HEADS,
               qc=qm_col, kc=0, vc=MEM_HEADS)

    h = rms_fwd(x2, sm["ln_mix_pre"], name="rms_mix_pre")
    proj = matmul(h, w_in_pad, name="mm_in")
    proj3 = proj.reshape(B, S, IN_PAD)
    o_sb, st_sb, *slots = attn_fwd(proj3, proj3, proj3, name="sb_fwd", comm=ex.rest_comm(), **sbk)
    W = ex.rest_weights(slots[0] if slots else None)
    w_uq_pad = _pad_heads(W["w_uq"], MLA_NOPE + MLA_ROPE)
    w_uk_pad = _pad_heads(W["w_uk"], MLA_NOPE)
    cqn = rms_fwd(proj, sm["q_norm"], n=Q_LORA, col=cq_col, name="rms_q")
    ckvn = rms_fwd(proj, sm["kv_norm"], n=KV_LORA, col=ckv_col, name="rms_kv")
    qpre = matmul(cqn, w_uq_pad, name="mm_uq")
    kpre = matmul(ckvn, w_uk_pad, name="mm_uk")
    v_mla = matmul(ckvn, W["w_uv"], out_dtype=BF16, name="mm_uv").reshape(B, S, -1)
    q_pad, k_pad = rope_fwd(qpre, kpre, proj, pos, invf, name="rope_fwd")
    q_pad, k_pad = q_pad.reshape(B, S, -1), k_pad.reshape(B, S, -1)
    o_mla, st_mla = attn_fwd(q_pad, k_pad, v_mla, name="mla_fwd", **mla)
    memh = rms_fwd(mem2, sm["mem_norm"], name="rms_mem")
    mkv = matmul(memh, W["w_mem_kv"], out_dtype=BF16, name="mm_memkv").reshape(B, M, -1)
    o_mem, st_mem = attn_fwd(proj3, mkv, mkv, name="mem_fwd", **mca)
    outs = [o.reshape(T, BRANCH_W) for o in (o_mla, o_sb, o_mem)]
    ps = [matmul(o, W["w_branch_out"][b], out_dtype=BF16, name=f"mm_bo{b}") for b, o in enumerate(outs)]
    merged = gate_merge(proj, sm["b_gate"], ps, name="gate_merge")
    y = matmul(merged, W["w_out"], name="mm_out")
    x1, h2 = mix_post_fwd(x2, y, sm["ln_mix_post"], sm["ln_mlp_pre"], name="mix_post")
    u = matmul(h2, W["w_mlp_up"], b_slots="n", act="relu2", out_dtype=BF16, name="mm_up")
    d = matmul(u, W["w_mlp_down"], name="mm_down")
    loss_p, dx2, dd, dg_mlp_post = loss_head(x1, d, sm["ln_mlp_post"], tgt.reshape(T, D), name="loss_head")

    da = matmul(dd, W["w_mlp_down"], tb=True, act="relu2_bwd", act_in=u, out_dtype=BF16, name="mm_down_dx")
    g_down = matmul(u, dd, ta=True, out_dtype=BF16, out_slots="m", name="mm_down_dw")
    dh2 = matmul(da, W["w_mlp_up"], tb=True, b_slots="k", name="mm_up_dx")
    g_up = matmul(h2, da, ta=True, out_dtype=BF16, out_slots="n", name="mm_up_dw")
    red_mlp = ex.reduce_start("mlp", dict(w_mlp_down=g_down, w_mlp_up=g_up))
    dx1, dy, dg_mlp_pre, dg_mix_post = mlp_pre_bwd(dx2, dh2, x1, sm["ln_mlp_pre"], y, sm["ln_mix_post"], name="mlp_pre_bwd")
    dmerged = matmul(dy, W["w_out"], tb=True, name="mm_out_dx")
    g_out = matmul(merged, dy, ta=True, out_dtype=BF16, name="mm_out_dw")
    dlog, dp0, dp1, dp2, db_gate = gate_bwd(dmerged, proj, sm["b_gate"], ps, name="gate_bwd")
    dps = (dp0, dp1, dp2)
    dos = [matmul(dps[b], W["w_branch_out"][b], tb=True, name=f"mm_bo{b}_dx").reshape(B, S, BRANCH_W) for b in range(N_BRANCH)]
    g_bo = jnp.stack([matmul(outs[b], dps[b], ta=True, out_dtype=BF16, name=f"mm_bo{b}_dw") for b in range(N_BRANCH)])
    red_mix = ex.reduce_start("mix", dict(w_out=_full_to_owner("w_out", g_out),
                                          w_branch_out=_full_to_owner("w_branch_out", g_bo)))
    dq_pad, dk_pad, dv_mla, *got = attn_bwd(q_pad, k_pad, v_mla, o_mla, st_mla, dos[0], name="mla_bwd", comm=red_mlp, **mla)
    ex.reduce_done("mlp", got[0] if got else None)
    dsq, dsk, dsv, *got = attn_bwd(proj3, proj3, proj3, o_sb, st_sb, dos[1], name="sb_bwd", comm=red_mix, **sbk)
    ex.reduce_done("mix", got[0] if got else None)
    dqm, dmk, dmv = attn_bwd(proj3, mkv, mkv, o_mem, st_mem, dos[2], name="mem_bwd", **mca)
    dmkv = jnp.concatenate([dmk, dmv], axis=-1).astype(BF16).reshape(B * M, -1)
    dmemh = matmul(dmkv, W["w_mem_kv"], tb=True, name="mm_memkv_dx")
    g_memkv = matmul(memh, dmkv, ta=True, out_dtype=BF16, name="mm_memkv_dw")
    _, dg_mem_norm = rms_bwd(dmemh, mem2, sm["mem_norm"], name="rms_mem_bwd")
    dq_pad, dk_pad, dv_mla = dq_pad.reshape(T, -1), dk_pad.reshape(T, -1), dv_mla.reshape(T, -1)
    dqpre, dkr = rope_bwd(dq_pad, dk_pad, pos, invf, name="rope_bwd")
    dcqn = matmul(dqpre, w_uq_pad, tb=True, name="mm_uq_dx")
    g_uq = _unpad_heads(matmul(cqn, dqpre, ta=True, out_dtype=BF16, name="mm_uq_dw"), MLA_NOPE + MLA_ROPE)
    dckvn = matmul(dk_pad, w_uk_pad, tb=True, name="mm_uk_dx")
    dckvn = matmul(dv_mla, W["w_uv"], tb=True, acc=dckvn, name="mm_uv_dx")
    g_uk = _unpad_heads(matmul(ckvn, dk_pad, ta=True, out_dtype=BF16, name="mm_uk_dw"), MLA_NOPE)
    g_uv = matmul(ckvn, dv_mla, ta=True, out_dtype=BF16, name="mm_uv_dw")
    dcq, dg_q_norm = rms_bwd(dcqn, proj, sm["q_norm"], n=Q_LORA, col=cq_col, out_dtype=BF16, name="rms_q_bwd")
    dckv, dg_kv_norm = rms_bwd(dckvn, proj, sm["kv_norm"], n=KV_LORA, col=ckv_col, out_dtype=BF16, name="rms_kv_bwd")
    pieces = dict(gate=dlog, sb=jnp.concatenate([dsq, dsk, dsv], axis=-1).reshape(T, -1), qm=dqm.reshape(T, -1),
                  ckv=dckv, cq=dcq, kr=dkr, zpad=jnp.zeros((T, PAD_SIZES["zpad"]), BF16))
    dproj = jnp.concatenate([pieces[n].astype(BF16) for n in PAD_ORDER], axis=1)
    g_in = in_pad_to_owner(matmul(h, dproj, ta=True, out_dtype=BF16, name="mm_in_dw"), name="g_in_layout")
    last = dict(w_uq=g_uq, w_uk=g_uk, w_uv=g_uv, w_mem_kv=g_memkv)
    red_in = ex.reduce_start("in", dict(w_in=g_in, **{n: _full_to_owner(n, g) for n, g in last.items()}))
    dh, *got = _as_tuple(matmul(dproj, w_in_pad, tb=True, comm=red_in, name="mm_in_dx"))
    ex.reduce_done("in", got[0] if got else None)
    dx, dg_mix_pre = rms_bwd(dh, x2, sm["ln_mix_pre"], residual=dx1, name="rms_mix_pre_bwd")

    small = dict(ln_mix_pre=dg_mix_pre, b_gate=db_gate, q_norm=dg_q_norm, kv_norm=dg_kv_norm, mem_norm=dg_mem_norm,
                 ln_mix_post=dg_mix_post, ln_mlp_pre=dg_mlp_pre, ln_mlp_post=dg_mlp_post)
    return loss_p, dx.reshape(B, S, D), small


def kernel(x, mem, positions, ln_mix_pre, w_in, b_gate, q_norm, w_uq, kv_norm, w_uk, w_uv, mem_norm, w_mem_kv, w_branch_out, w_out, ln_mix_post, ln_mlp_pre, w_mlp_up, w_mlp_down, ln_mlp_post, loss_target, m_ln_mix_pre, m_w_in, m_b_gate, m_q_norm, m_w_uq, m_kv_norm, m_w_uk, m_w_uv, m_mem_norm, m_w_mem_kv, m_w_branch_out, m_w_out, m_ln_mix_post, m_ln_mlp_pre, m_w_mlp_up, m_w_mlp_down, m_ln_mlp_post, v_ln_mix_pre, v_w_in, v_b_gate, v_q_norm, v_w_uq, v_kv_norm, v_w_uk, v_w_uv, v_mem_norm, v_w_mem_kv, v_w_branch_out, v_w_out, v_ln_mix_post, v_ln_mlp_pre, v_w_mlp_up, v_w_mlp_down, v_ln_mlp_post):
    given = dict(locals())
    w = {n: given[n][0] for n in WEIGHTS}
    m = {n: given["m_" + n][0] for n in WEIGHTS}
    v = {n: given["v_" + n][0] for n in WEIGHTS}
    sm = {n: w[n].reshape(1, -1) for n in SMALL}
    ex = _Exchange(w)
    loss_row, grad_x, g_small = _local_step(x, mem, positions, loss_target, sm, ex)

    res = {n: sum_adamw(ex.reduced[n], given[n], given["m_" + n], given["v_" + n], name=f"adamw_{n}") for n in BIG}

    small_shapes = [w[n].shape for n in SMALL] + [(LANES,)]
    parts = gather_two_level([_pack_rows([g_small[n] for n in SMALL] + [loss_row])], name="gather_small_grads")[0]
    no_row = jnp.zeros((1, LANES), F32)
    res_small = sum_adamw(parts, *[_pack_rows([t[n] for n in SMALL] + [no_row]) for t in (w, m, v)], name="adamw_replicated")
    res_small = [_unpack_rows(r, small_shapes) for r in res_small]
    for i, n in enumerate(SMALL):
        res[n] = [r[i][None] for r in res_small]
    loss = res_small[0][len(SMALL)][0]

    out = [loss, grad_x]
    for k in range(4):
        out += [res[n][k] for n in WEIGHTS]
    return tuple(out)
```

```python
import functools
import math

import numpy as np
import jax
import jax.numpy as jnp
from jax import lax
from jax.experimental import pallas as pl
from jax.experimental.pallas import tpu as pltpu

F32 = jnp.float32
BF16 = jnp.bfloat16

D_MODEL = 1024
MEM_HEADS, MEM_DIM = 4, 128
MLA_HEADS, MLA_NOPE, MLA_ROPE, MLA_V = 8, 64, 32, 64
Q_LORA, KV_LORA = 384, 256
ROPE_THETA = 10000.0
SB_HEADS, SB_DIM = 8, 64
D_FF = 4 * D_MODEL
N_BRANCH, BRANCH_W = 3, 512
EPS = 1e-6
SB_W = 3 * SB_HEADS * SB_DIM
QM_W = MEM_HEADS * MEM_DIM
GATE_W = N_BRANCH * D_MODEL
IN_SIZES = (Q_LORA, KV_LORA, MLA_ROPE, SB_W, QM_W, GATE_W)
IN_WIDTH = sum(IN_SIZES)
ADAM_LR, ADAM_B1, ADAM_B2, ADAM_EPS, ADAM_WD, ADAM_STEP = 0.001, 0.9, 0.999, 1e-08, 0.01, 10

N_DEV = 8
LANES = 128
KR_PAD = LANES
PAD_ORDER = ("gate", "cq", "kr", "ckv", "sb", "qm", "zpad")
MM_TILE = 1024
MM_ROWS = 2048
MM_DEPTH = 4096
MM_VMEM = 40 * 1024 * 1024
PAD_SIZES = dict(gate=GATE_W, cq=Q_LORA, kr=KR_PAD, ckv=KV_LORA, sb=SB_W, qm=QM_W)
PAD_SIZES["zpad"] = (-sum(PAD_SIZES.values())) % MM_TILE
PAD_OFF = {}
_o = 0
for _n in PAD_ORDER:
    PAD_OFF[_n] = _o
    _o += PAD_SIZES[_n]
IN_PAD = _o
HEAD_PAD = LANES
MLA_PAD_W = MLA_HEADS * HEAD_PAD
VMEM_LIMIT = 48 * 1024 * 1024


def _cparams(sem):
    return pltpu.CompilerParams(dimension_semantics=sem, vmem_limit_bytes=VMEM_LIMIT)


def _pick(n, pref):
    if n <= pref:
        return n
    t = (pref // LANES) * LANES
    while t >= LANES:
        if n % t == 0:
            return t
        t -= LANES
    return n


def matmul(a, b, *, name, ta=False, tb=False, out_dtype=F32, acc=None, act=None, act_in=None,
           b_slots=None, out_slots=None, comm=None, tm_pref=MM_ROWS, tn_pref=MM_TILE, tk_pref=MM_DEPTH):
    M, K = (a.shape[1], a.shape[0]) if ta else a.shape
    tm, tk = _pick(M, tm_pref), _pick(K, tk_pref)
    if b_slots == "n":
        assert not tb and b.shape[:2] == (N_DEV, K)
        tn = b.shape[2]
        N = N_DEV * tn
    elif b_slots == "k":
        assert tb and N_DEV * b.shape[2] == K
        N, tk = b.shape[1], b.shape[2]
        tn = _pick(N, tn_pref)
    else:
        N = b.shape[0] if tb else b.shape[1]
        assert (b.shape[1] if tb else b.shape[0]) == K
        tn = _pick(N, tn_pref)
    if out_slots == "m":
        tm = M // N_DEV
    elif out_slots == "n":
        tn = N // N_DEV
    assert out_slots is None or (acc is None and act_in is None)

    def block_bytes(tm_, tk_):
        ab = tm_ * tk_ * a.dtype.itemsize + tk_ * tn * b.dtype.itemsize
        per_out = jnp.dtype(out_dtype).itemsize + sum(t.dtype.itemsize for t in (acc, act_in) if t is not None)
        return 2 * (ab + tm_ * tn * per_out) + (tm_ * tn * 4 if K // tk_ > 1 else 0)

    while block_bytes(tm, tk) > MM_VMEM:
        if tm > 512 and out_slots != "m" and M % (tm // 2) == 0:
            tm //= 2
        elif tk > 512 and b_slots != "k" and K % (tk // 2) == 0:
            tk //= 2
        else:
            break
    nk = K // tk
    dims = (((0 if ta else 1,), (1 if tb else 0,)), ((), ()))
    assert act in (None, "relu2", "relu2_bwd") and (act == "relu2_bwd") == (act_in is not None)

    def body(*refs):
        a_ref, b_ref = refs[0], refs[1]
        pos = 2
        acc_ref = act_ref = None
        if acc is not None:
            acc_ref = refs[pos]
            pos += 1
        if act_in is not None:
            act_ref = refs[pos]
            pos += 1
        out = refs[pos]
        scratch = refs[pos + 1:]

        part = lax.dot_general(a_ref[...].astype(BF16), b_ref[...].astype(BF16), dims,
                               preferred_element_type=F32)

        def finish(r):
            if acc_ref is not None:
                r = r + acc_ref[...]
            if act == "relu2":
                r = jnp.square(jnp.maximum(r, 0.0))
            elif act == "relu2_bwd":
                r = r * (2.0 * jnp.sqrt(act_ref[...].astype(F32)))
            out[...] = r.astype(out.dtype)

        if nk == 1:
            finish(part)
        else:
            acc_sc = scratch[0]
            k = pl.program_id(2)

            @pl.when(k == 0)
            def _():
                acc_sc[...] = part

            @pl.when(k > 0)
            def _():
                acc_sc[...] += part

            @pl.when(k == nk - 1)
            def _():
                finish(acc_sc[...])

    a_spec = pl.BlockSpec((tk, tm), lambda i, j, k: (k, i)) if ta else pl.BlockSpec((tm, tk), lambda i, j, k: (i, k))
    if b_slots == "n":
        b_spec = pl.BlockSpec((None, tk, tn), lambda i, j, k: (j, k, 0))
    elif b_slots == "k":
        b_spec = pl.BlockSpec((None, tn, tk), lambda i, j, k: (k, j, 0))
    else:
        b_spec = pl.BlockSpec((tn, tk), lambda i, j, k: (j, k)) if tb else pl.BlockSpec((tk, tn), lambda i, j, k: (k, j))
    if out_slots == "m":
        o_spec = pl.BlockSpec((None, None, tm, tn), lambda i, j, k: (i % 2, i // 2, 0, j))
        o_shape = (2, N_CHIP, tm, N)
    elif out_slots == "n":
        o_spec = pl.BlockSpec((None, None, tm, tn), lambda i, j, k: (j % 2, j // 2, i, 0))
        o_shape = (2, N_CHIP, M, tn)
    else:
        o_spec = pl.BlockSpec((tm, tn), lambda i, j, k: (i, j))
        o_shape = (M, N)
    in_specs = [a_spec, b_spec]
    args = [a, b]
    if acc is not None:
        in_specs.append(o_spec)
        args.append(acc)
    if act_in is not None:
        in_specs.append(o_spec)
        args.append(act_in)
    outs, c_outs = _call(
        body, name=name, grid=(M // tm, N // tn, nk), in_specs=in_specs, out_specs=[o_spec],
        out_shape=[jax.ShapeDtypeStruct(o_shape, out_dtype)],
        scratch_shapes=[pltpu.VMEM((tm, tn), F32)] if nk > 1 else [], args=args,
        sem=("parallel", "parallel", "arbitrary"), comm=comm)
    return (outs[0], c_outs) if comm is not None else outs[0]


ROW_TILE = 256


def _rstd(xv):
    return lax.rsqrt(jnp.mean(xv * xv, axis=-1, keepdims=True) + EPS)


def _rms_bwd_rows(dy, xv, g):
    r = _rstd(xv)
    dyg = dy * g
    dx = r * dyg - xv * (r * r * r) * jnp.mean(dyg * xv, axis=-1, keepdims=True)
    return dx, dy * xv * r


def _row_spec(tm, n, col=0):
    return pl.BlockSpec((tm, n), lambda i: (i, col))


def _vec_spec(n):
    return pl.BlockSpec((1, n), lambda i: (0, 0))


def _acc_rows(ref, i, val):
    @pl.when(i == 0)
    def _():
        ref[...] = val

    @pl.when(i > 0)
    def _():
        ref[...] += val


def rms_fwd(x, g, *, name, n=None, col=0, out_dtype=BF16):
    T = x.shape[0]
    n = x.shape[1] if n is None else n
    tm = _pick(T, ROW_TILE)

    def body(x_ref, g_ref, o_ref):
        xv = x_ref[...]
        o_ref[...] = (xv * _rstd(xv) * g_ref[...]).astype(o_ref.dtype)

    return pl.pallas_call(
        body, name=name, grid=(T // tm,), in_specs=[_row_spec(tm, n, col), _vec_spec(n)],
        out_specs=_row_spec(tm, n), out_shape=jax.ShapeDtypeStruct((T, n), out_dtype),
        compiler_params=_cparams(("parallel",)),
    )(x, g)


def rms_bwd(dy, x, g, *, name, n=None, col=0, residual=None, out_dtype=F32):
    T = x.shape[0]
    n = x.shape[1] if n is None else n
    tm = _pick(T, ROW_TILE)

    def body(*refs):
        if residual is None:
            dy_ref, x_ref, g_ref, dx_ref, dg_ref = refs
        else:
            dy_ref, x_ref, g_ref, res_ref, dx_ref, dg_ref = refs
        dx, dgr = _rms_bwd_rows(dy_ref[...].astype(F32), x_ref[...], g_ref[...])
        if residual is not None:
            dx = dx + res_ref[...]
        dx_ref[...] = dx.astype(dx_ref.dtype)
        _acc_rows(dg_ref, pl.program_id(0), jnp.sum(dgr, axis=0, keepdims=True))

    in_specs = [_row_spec(tm, n), _row_spec(tm, n, col), _vec_spec(n)]
    args = [dy, x, g]
    if residual is not None:
        in_specs.append(_row_spec(tm, n))
        args.append(residual)
    return pl.pallas_call(
        body, name=name, grid=(T // tm,), in_specs=in_specs,
        out_specs=(_row_spec(tm, n), _vec_spec(n)),
        out_shape=(jax.ShapeDtypeStruct((T, n), out_dtype), jax.ShapeDtypeStruct((1, n), F32)),
        compiler_params=_cparams(("arbitrary",)),
    )(*args)


def mix_post_fwd(x, y, g_post, g_pre2, *, name):
    T, n = x.shape
    tm = _pick(T, ROW_TILE)

    def body(x_ref, y_ref, gp_ref, g2_ref, x1_ref, h2_ref):
        yv = y_ref[...]
        x1 = x_ref[...] + yv * _rstd(yv) * gp_ref[...]
        x1_ref[...] = x1
        h2_ref[...] = (x1 * _rstd(x1) * g2_ref[...]).astype(h2_ref.dtype)

    return pl.pallas_call(
        body, name=name, grid=(T // tm,),
        in_specs=[_row_spec(tm, n), _row_spec(tm, n), _vec_spec(n), _vec_spec(n)],
        out_specs=(_row_spec(tm, n), _row_spec(tm, n)),
        out_shape=(jax.ShapeDtypeStruct((T, n), F32), jax.ShapeDtypeStruct((T, n), BF16)),
        compiler_params=_cparams(("parallel",)),
    )(x, y, g_post, g_pre2)


def loss_head(x1, d, g, tgt, *, name):
    T, n = x1.shape
    tm = _pick(T, ROW_TILE)

    def body(x1_ref, d_ref, g_ref, t_ref, loss_ref, dx2_ref, dd_ref, dg_ref):
        i = pl.program_id(0)
        dv = d_ref[...]
        gv = g_ref[...]
        err = x1_ref[...] + dv * _rstd(dv) * gv - t_ref[...]
        part = 0.5 * jnp.sum(jnp.mean(err * err, axis=-1, keepdims=True), axis=0, keepdims=True)
        _acc_rows(loss_ref, i, jnp.broadcast_to(part, (1, LANES)))
        dx2 = err * (1.0 / n)
        dx2_ref[...] = dx2
        dd, dgr = _rms_bwd_rows(dx2, dv, gv)
        dd_ref[...] = dd.astype(dd_ref.dtype)
        _acc_rows(dg_ref, i, jnp.sum(dgr, axis=0, keepdims=True))

    return pl.pallas_call(
        body, name=name, grid=(T // tm,),
        in_specs=[_row_spec(tm, n), _row_spec(tm, n), _vec_spec(n), _row_spec(tm, n)],
        out_specs=(_vec_spec(LANES), _row_spec(tm, n), _row_spec(tm, n), _vec_spec(n)),
        out_shape=(jax.ShapeDtypeStruct((1, LANES), F32), jax.ShapeDtypeStruct((T, n), F32),
                   jax.ShapeDtypeStruct((T, n), BF16), jax.ShapeDtypeStruct((1, n), F32)),
        compiler_params=_cparams(("arbitrary",)),
    )(x1, d, g, tgt)


def mlp_pre_bwd(dx2, dh2, x1, g_pre2, y, g_post, *, name):
    T, n = x1.shape
    tm = _pick(T, ROW_TILE)

    def body(dx2_ref, dh2_ref, x1_ref, g2_ref, y_ref, gp_ref, dx1_ref, dy_ref, dg2_ref, dgp_ref):
        i = pl.program_id(0)
        d1, dg2 = _rms_bwd_rows(dh2_ref[...], x1_ref[...], g2_ref[...])
        dx1 = dx2_ref[...] + d1
        dx1_ref[...] = dx1
        dy, dgp = _rms_bwd_rows(dx1, y_ref[...], gp_ref[...])
        dy_ref[...] = dy.astype(dy_ref.dtype)
        _acc_rows(dg2_ref, i, jnp.sum(dg2, axis=0, keepdims=True))
        _acc_rows(dgp_ref, i, jnp.sum(dgp, axis=0, keepdims=True))

    rs = _row_spec(tm, n)
    return pl.pallas_call(
        body, name=name, grid=(T // tm,),
        in_specs=[rs, rs, rs, _vec_spec(n), rs, _vec_spec(n)],
        out_specs=(rs, rs, _vec_spec(n), _vec_spec(n)),
        out_shape=(jax.ShapeDtypeStruct((T, n), F32), jax.ShapeDtypeStruct((T, n), BF16),
                   jax.ShapeDtypeStruct((1, n), F32), jax.ShapeDtypeStruct((1, n), F32)),
        compiler_params=_cparams(("arbitrary",)),
    )(dx2, dh2, x1, g_pre2, y, g_post)


def _gate_specs(tm):
    gcol = PAD_OFF["gate"] // D_MODEL
    lspecs = [pl.BlockSpec((tm, D_MODEL), functools.partial(lambda i, c: (i, c), c=gcol + b)) for b in range(N_BRANCH)]
    bspecs = [pl.BlockSpec((1, D_MODEL), functools.partial(lambda i, c: (0, c), c=b)) for b in range(N_BRANCH)]
    pspecs = [_row_spec(tm, D_MODEL) for _ in range(N_BRANCH)]
    return lspecs, bspecs, pspecs


def gate_merge(proj, b_gate, ps, *, name):
    T = proj.shape[0]
    tm = _pick(T, ROW_TILE)
    lspecs, bspecs, pspecs = _gate_specs(tm)

    def body(*refs):
        l_refs, b_refs, p_refs, o_ref = refs[0:3], refs[3:6], refs[6:9], refs[9]
        tot = None
        for lr, br, pr in zip(l_refs, b_refs, p_refs):
            term = jax.nn.sigmoid(lr[...] + br[...]) * pr[...].astype(F32)
            tot = term if tot is None else tot + term
        o_ref[...] = tot.astype(o_ref.dtype)

    return pl.pallas_call(
        body, name=name, grid=(T // tm,), in_specs=lspecs + bspecs + pspecs,
        out_specs=_row_spec(tm, D_MODEL), out_shape=jax.ShapeDtypeStruct((T, D_MODEL), BF16),
        compiler_params=_cparams(("parallel",)),
    )(proj, proj, proj, b_gate, b_gate, b_gate, *ps)


def gate_bwd(dmerged, proj, b_gate, ps, *, name):
    T = proj.shape[0]
    tm = _pick(T, ROW_TILE)
    lspecs, bspecs, pspecs = _gate_specs(tm)

    def body(*refs):
        dm_ref = refs[0]
        l_refs, b_refs, p_refs = refs[1:4], refs[4:7], refs[7:10]
        dl_ref, dp_refs, db_ref = refs[10], refs[11:14], refs[14]
        i = pl.program_id(0)
        dm = dm_ref[...]
        for b, (lr, br, pr, dpr) in enumerate(zip(l_refs, b_refs, p_refs, dp_refs)):
            gt = jax.nn.sigmoid(lr[...] + br[...])
            dpr[...] = (dm * gt).astype(dpr.dtype)
            dl = dm * pr[...].astype(F32) * gt * (1.0 - gt)
            dl_ref[:, b * D_MODEL:(b + 1) * D_MODEL] = dl.astype(dl_ref.dtype)
            part = jnp.sum(dl, axis=0, keepdims=True)

            @pl.when(i == 0)
            def _():
                db_ref[:, b * D_MODEL:(b + 1) * D_MODEL] = part

            @pl.when(i > 0)
            def _():
                db_ref[:, b * D_MODEL:(b + 1) * D_MODEL] += part

    return pl.pallas_call(
        body, name=name, grid=(T // tm,), in_specs=[_row_spec(tm, D_MODEL)] + lspecs + bspecs + pspecs,
        out_specs=(_row_spec(tm, GATE_W), *[_row_spec(tm, D_MODEL) for _ in range(N_BRANCH)], _vec_spec(GATE_W)),
        out_shape=(jax.ShapeDtypeStruct((T, GATE_W), BF16),
                   *[jax.ShapeDtypeStruct((T, D_MODEL), BF16) for _ in range(N_BRANCH)],
                   jax.ShapeDtypeStruct((1, GATE_W), F32)),
        compiler_params=_cparams(("arbitrary",)),
    )(dmerged, proj, proj, proj, b_gate, b_gate, b_gate, *ps)


def _rope_tables(pos_ref, invf_ref):
    ang = pos_ref[...] * invf_ref[...]
    lane = lax.broadcasted_iota(jnp.int32, (1, LANES), 1)
    s = jnp.sin(ang)
    split = MLA_NOPE + MLA_ROPE // 2
    return jnp.cos(ang), jnp.where(lane >= split, s, 0.0), jnp.where(lane < split, s, 0.0)


def _rotate(xh, c, s_hi, s_lo, sign):
    half = MLA_ROPE // 2
    up = pltpu.roll(xh, half, 1)
    down = pltpu.roll(xh, LANES - half, 1)
    return xh * c + sign * (up * s_hi - down * s_lo)


def rope_fwd(qpre, kpre, proj, pos, invf, *, name):
    T = qpre.shape[0]
    tm = _pick(T, ROW_TILE)
    kr_col = PAD_OFF["kr"] // KR_PAD

    def body(q_ref, k_ref, kr_ref, pos_ref, invf_ref, qo_ref, ko_ref):
        c, s_hi, s_lo = _rope_tables(pos_ref, invf_ref)
        kr = _rotate(pltpu.roll(kr_ref[...], MLA_NOPE, 1), c, s_hi, s_lo, 1.0)
        for h in range(MLA_HEADS):
            sl = slice(h * HEAD_PAD, (h + 1) * HEAD_PAD)
            qo_ref[:, sl] = _rotate(q_ref[:, sl], c, s_hi, s_lo, 1.0).astype(qo_ref.dtype)
            ko_ref[:, sl] = (k_ref[:, sl] + kr).astype(ko_ref.dtype)

    rs = _row_spec(tm, MLA_PAD_W)
    return pl.pallas_call(
        body, name=name, grid=(T // tm,),
        in_specs=[rs, rs, _row_spec(tm, KR_PAD, kr_col), _row_spec(tm, 1), _vec_spec(LANES)],
        out_specs=(rs, rs),
        out_shape=(jax.ShapeDtypeStruct((T, MLA_PAD_W), BF16), jax.ShapeDtypeStruct((T, MLA_PAD_W), BF16)),
        compiler_params=_cparams(("parallel",)),
    )(qpre, kpre, proj, pos, invf)


def rope_bwd(dq_pad, dk_pad, pos, invf, *, name):
    T = dq_pad.shape[0]
    tm = _pick(T, ROW_TILE)

    def body(dq_ref, dk_ref, pos_ref, invf_ref, dqo_ref, dkr_ref):
        c, s_hi, s_lo = _rope_tables(pos_ref, invf_ref)
        tot = None
        for h in range(MLA_HEADS):
            sl = slice(h * HEAD_PAD, (h + 1) * HEAD_PAD)
            dqo_ref[:, sl] = _rotate(dq_ref[:, sl], c, s_hi, s_lo, -1.0).astype(dqo_ref.dtype)
            tot = dk_ref[:, sl] if tot is None else tot + dk_ref[:, sl]
        dkr = pltpu.roll(_rotate(tot, c, s_hi, s_lo, -1.0), LANES - MLA_NOPE, 1)
        lane = lax.broadcasted_iota(jnp.int32, (1, LANES), 1)
        dkr_ref[...] = jnp.where(lane < MLA_ROPE, dkr, 0.0).astype(dkr_ref.dtype)

    rs = _row_spec(tm, MLA_PAD_W)
    return pl.pallas_call(
        body, name=name, grid=(T // tm,),
        in_specs=[rs, rs, _row_spec(tm, 1), _vec_spec(LANES)],
        out_specs=(rs, _row_spec(tm, KR_PAD)),
        out_shape=(jax.ShapeDtypeStruct((T, MLA_PAD_W), BF16), jax.ShapeDtypeStruct((T, KR_PAD), BF16)),
        compiler_params=_cparams(("parallel",)),
    )(dq_pad, dk_pad, pos, invf)


ATT_TILE = 512
NEG = -1e30


def _split_bf16(v):
    hi = v.astype(BF16)
    return hi, (v - hi.astype(F32)).astype(BF16)


def _mul(x, s):
    return x if s == 1.0 else x * s


def _dot(a, b, ca, cb):
    return lax.dot_general(a, b, (((ca,), (cb,)), ((), ())), preferred_element_type=F32)


CUM_CHUNK = 256


def _tri(kind, terms=2):
    r = lax.broadcasted_iota(jnp.int32, (CUM_CHUNK, CUM_CHUNK), 0)
    c = lax.broadcasted_iota(jnp.int32, (CUM_CHUNK, CUM_CHUNK), 1)
    m = {"gt": r > c, "le": r <= c, "lt": r < c}[kind]
    t = jnp.where(m, 1.0, 0.0).astype(BF16)
    return jnp.concatenate([t] * terms, axis=0)


def _cum(v, tri, kind):
    n = v.shape[1] // CUM_CHUNK
    two = tri.shape[0] == 2 * CUM_CHUNK
    chunks = [v[:, c * CUM_CHUNK:(c + 1) * CUM_CHUNK] for c in range(n)]
    totals = [jnp.sum(ch, axis=1, keepdims=True) for ch in chunks] if n > 1 else None
    outs = []
    for c, ch in enumerate(chunks):
        r = _dot(jnp.concatenate(_split_bf16(ch), axis=1) if two else ch.astype(BF16), tri, 1, 0)
        others = [] if n == 1 else totals[c + 1:] if kind == "gt" else totals[:c]
        for t in others:
            r = r + t
        outs.append(r)
    return outs[0] if n == 1 else jnp.concatenate(outs, axis=1)


def _lane_masks(hp, w):
    lane = lax.broadcasted_iota(jnp.int32, (1, LANES), 1)
    return [(lane >= e * w) & (lane < (e + 1) * w) for e in range(hp)]


def _att_dims(mode, hp, qk_w, v_w):
    assert mode in ("softmax", "sb")
    assert (hp, qk_w, v_w) in ((2, 128, 64), (2, 64, 64), (1, 128, 128))
    qw = hp * LANES if qk_w == LANES else LANES
    return qw


def attn_fwd(q, k, v, *, name, mode, causal, hp, qk_w, v_w, scale, nh, qc, kc, vc, comm=None):
    qw = _att_dims(mode, hp, qk_w, v_w)
    B, S = q.shape[0], q.shape[1]
    Sk = k.shape[1]
    tq = _pick(S, ATT_TILE)
    tk = _pick(Sk, ATT_TILE)
    if causal:
        assert tq == tk and S == Sk
    nq, nkv = S // tq, Sk // tk
    pre_scaled = math.frexp(scale)[0] == 0.5
    post = 1.0 if pre_scaled else scale

    def body(q_ref, k_ref, v_ref, o_ref, st_ref):
        i = pl.program_id(2)
        qv = q_ref[0].astype(BF16)
        masks_qk = _lane_masks(hp, qk_w) if qk_w < LANES else None
        masks_v = _lane_masks(hp, v_w) if v_w < LANES else None
        if qk_w == LANES:
            qs = [qv[:, e * LANES:(e + 1) * LANES] for e in range(hp)]
        else:
            qs = [jnp.where(masks_qk[e], qv, jnp.zeros_like(qv)) for e in range(hp)]
        if pre_scaled:
            qs = [t * scale for t in qs]
        rows = lax.broadcasted_iota(jnp.int32, (tq, tk), 0)
        cols = lax.broadcasted_iota(jnp.int32, (tq, tk), 1)
        tri = _tri("gt") if mode == "sb" else None

        def kv_tile(j):
            off = pl.multiple_of(j * tk, tk)
            kj = k_ref[0, pl.ds(off, tk), :].astype(BF16)
            vj = v_ref[0, pl.ds(off, tk), :].astype(BF16)
            ks = [kj[:, e * LANES:(e + 1) * LANES] for e in range(hp)] if qk_w == LANES else [kj] * hp
            return ks, vj

        def merge(vals):
            if hp == 1:
                return jnp.broadcast_to(vals[0], (tq, LANES))
            return jnp.where(masks_v[0], vals[0], vals[1])

        if mode == "softmax":
            def block(j, carry, diag):
                ms, ls, acc = carry
                ks, vj = kv_tile(j)
                new_m, new_l, alphas, pvs = [], [], [], []
                for e in range(hp):
                    s = _mul(_dot(qs[e], ks[e], 1, 1), post)
                    if diag:
                        s = jnp.where(rows >= cols, s, NEG)
                    m_new = jnp.maximum(ms[e], jnp.max(s, axis=1, keepdims=True))
                    alpha = jnp.exp(ms[e] - m_new)
                    p = jnp.exp(s - m_new)
                    new_l.append(alpha * ls[e] + jnp.sum(p, axis=1, keepdims=True))
                    new_m.append(m_new)
                    alphas.append(alpha)
                    pvs.append(_dot(p.astype(BF16), vj, 1, 0))
                acc = acc * merge(alphas) + merge(pvs)
                return tuple(new_m), tuple(new_l), acc

            init = (tuple(jnp.full((tq, 1), NEG, F32) for _ in range(hp)),
                    tuple(jnp.zeros((tq, 1), F32) for _ in range(hp)),
                    jnp.zeros((tq, LANES), F32))
            if causal:
                carry = lax.fori_loop(0, i, lambda j, c: block(j, c, False), init)
                ms, ls, acc = block(i, carry, True)
            else:
                ms, ls, acc = lax.fori_loop(0, nkv, lambda j, c: block(j, c, False), init)
            o_ref[0] = acc / merge(list(ls))
            st_ref[0, 0] = merge([m + jnp.log(l) for m, l in zip(ms, ls)])
        else:
            def block(j, carry, diag):
                cs_, acc = carry
                ks, vj = kv_tile(j)
                new_c, pvs = [], []
                for e in range(hp):
                    z = _mul(_dot(qs[e], ks[e], 1, 1), post)
                    lk = -jnp.maximum(z, 0.0) - jnp.log(1.0 + jnp.exp(-jnp.abs(z)))
                    lz = lk + z
                    if diag:
                        lk = jnp.where(rows > cols, lk, 0.0)
                    a = jnp.exp(lz + _cum(lk, tri, "gt") + cs_[e])
                    if diag:
                        a = jnp.where(rows > cols, a, 0.0)
                    pvs.append(_dot(a.astype(BF16), vj, 1, 0))
                    new_c.append(cs_[e] + jnp.sum(lk, axis=1, keepdims=True))
                return tuple(new_c), acc + merge(pvs)

            init = (tuple(jnp.zeros((tq, 1), F32) for _ in range(hp)), jnp.zeros((tq, LANES), F32))
            carry = block(i, init, True)
            cs_, acc = lax.fori_loop(0, i, lambda jj, c: block(i - 1 - jj, c, False), carry)
            o_ref[0] = acc
            st_ref[0, 0] = merge(list(cs_))

    outs, c_outs = _call(
        body, name=name, grid=(B, nh, nq),
        in_specs=[pl.BlockSpec((1, tq, qw), lambda b, h, i: (b, i, qc + h)),
                  pl.BlockSpec((1, Sk, qw), lambda b, h, i: (b, 0, kc + h)),
                  pl.BlockSpec((1, Sk, LANES), lambda b, h, i: (b, 0, vc + h))],
        out_specs=(pl.BlockSpec((1, tq, LANES), lambda b, h, i: (b, i, h)),
                   pl.BlockSpec((1, 1, tq, LANES), lambda b, h, i: (b, h, i, 0))),
        out_shape=(jax.ShapeDtypeStruct((B, S, nh * LANES), F32), jax.ShapeDtypeStruct((B, nh, S, LANES), F32)),
        scratch_shapes=[], args=(q, k, v), sem=("parallel", "parallel", "arbitrary"), comm=comm)
    return (*outs, c_outs) if comm is not None else tuple(outs)


def attn_bwd(q, k, v, o, st, do, *, name, mode, causal, hp, qk_w, v_w, scale, nh, qc, kc, vc, comm=None):
    qw = _att_dims(mode, hp, qk_w, v_w)
    B, S = q.shape[0], q.shape[1]
    Sk = k.shape[1]
    tq = _pick(S, ATT_TILE)
    tk = _pick(Sk, ATT_TILE)
    if causal:
        assert tq == tk and S == Sk
    nq, nkv = S // tq, Sk // tk
    pre_scaled = math.frexp(scale)[0] == 0.5
    post = 1.0 if pre_scaled else scale

    def body(q_ref, k_ref, v_ref, o_ref, st_ref, do_ref, dq_ref, dk_ref, dv_ref):
        i = pl.program_id(2)

        @pl.when(i == 0)
        def _():
            dk_ref[...] = jnp.zeros_like(dk_ref)
            dv_ref[...] = jnp.zeros_like(dv_ref)

        qv = q_ref[0].astype(BF16)
        dov = do_ref[0]
        stv = st_ref[0, 0]
        masks_qk = _lane_masks(hp, qk_w) if qk_w < LANES else None
        masks_v = _lane_masks(hp, v_w) if v_w < LANES else None
        if qk_w == LANES:
            qs = [qv[:, e * LANES:(e + 1) * LANES] for e in range(hp)]
        else:
            qs = [jnp.where(masks_qk[e], qv, jnp.zeros_like(qv)) for e in range(hp)]
        if pre_scaled:
            qs = [t * scale for t in qs]
        if hp == 1:
            dos = [dov.astype(BF16)]
            stats = [stv[:, 0:1]]
        else:
            dos = [jnp.where(masks_v[e], dov, 0.0).astype(BF16) for e in range(hp)]
            stats = [stv[:, e * v_w:e * v_w + 1] for e in range(hp)]
        if mode == "softmax":
            prod = dov * o_ref[0]
            if hp == 1:
                dsum = [jnp.sum(prod, axis=1, keepdims=True)]
            else:
                dsum = [jnp.sum(jnp.where(masks_v[e], prod, 0.0), axis=1, keepdims=True) for e in range(hp)]
        rows = lax.broadcasted_iota(jnp.int32, (tq, tk), 0)
        cols = lax.broadcasted_iota(jnp.int32, (tq, tk), 1)
        if mode == "sb":
            tri_le, tri_lt = _tri("le"), _tri("lt", terms=1)

        def kv_tile(j):
            off = pl.multiple_of(j * tk, tk)
            kj = k_ref[0, pl.ds(off, tk), :].astype(BF16)
            vj = v_ref[0, pl.ds(off, tk), :].astype(BF16)
            ks = [kj[:, e * LANES:(e + 1) * LANES] for e in range(hp)] if qk_w == LANES else [kj] * hp
            return off, ks, vj

        def scatter(off, dz_list, p_list):
            dvj = None
            for e in range(hp):
                t = _dot(p_list[e], dos[e], 0, 0)
                dvj = t if dvj is None else dvj + t
            dv_ref[0, pl.ds(off, tk), :] += dvj
            if qk_w == LANES:
                for e in range(hp):
                    dk_ref[0, pl.ds(off, tk), e * LANES:(e + 1) * LANES] += _dot(dz_list[e], qs[e], 0, 0)
            else:
                dkj = None
                for e in range(hp):
                    t = _dot(dz_list[e], qs[e], 0, 0)
                    dkj = t if dkj is None else dkj + t
                dk_ref[0, pl.ds(off, tk), :] += dkj

        def dq_add(dqs, dz_list, ks):
            out = []
            for e in range(hp):
                out.append(dqs[e] + _dot(dz_list[e], ks[e], 1, 0))
            return tuple(out)

        dq0 = tuple(jnp.zeros((tq, LANES), F32) for _ in range(hp))

        if mode == "softmax":
            def block(j, dqs, diag):
                off, ks, vj = kv_tile(j)
                dzs, ps = [], []
                for e in range(hp):
                    s = _mul(_dot(qs[e], ks[e], 1, 1), post)
                    p = jnp.exp(s - stats[e])
                    if diag:
                        p = jnp.where(rows >= cols, p, 0.0)
                    dp = _dot(dos[e], vj, 1, 1)
                    dzs.append(_mul(p * (dp - dsum[e]), post).astype(BF16))
                    ps.append(p.astype(BF16))
                scatter(off, dzs, ps)
                return dq_add(dqs, dzs, ks)

            if causal:
                dqs = lax.fori_loop(0, i, lambda j, c: block(j, c, False), dq0)
                dqs = block(i, dqs, True)
            else:
                dqs = lax.fori_loop(0, nkv, lambda j, c: block(j, c, False), dq0)
        else:
            def block(j, carry, diag):
                dqs, cps, cgs = carry
                off, ks, vj = kv_tile(j)
                dzs, ps, new_p, new_g = [], [], [], []
                for e in range(hp):
                    z = _mul(_dot(qs[e], ks[e], 1, 1), post)
                    lk = -jnp.maximum(z, 0.0) - jnp.log(1.0 + jnp.exp(-jnp.abs(z)))
                    lz = lk + z
                    sig, keep = jnp.exp(lz), jnp.exp(lk)
                    if diag:
                        lk = jnp.where(rows > cols, lk, 0.0)
                    keep_after = stats[e] - cps[e] - _cum(lk, tri_le, "le")
                    a = jnp.exp(lz + keep_after)
                    if diag:
                        a = jnp.where(rows > cols, a, 0.0)
                    g = _dot(dos[e], vj, 1, 1) * a
                    gsum = cgs[e] + _cum(g, tri_lt, "lt")
                    dz = _mul(g * keep - gsum * sig, post)
                    if diag:
                        dz = jnp.where(rows > cols, dz, 0.0)
                    dzs.append(dz.astype(BF16))
                    ps.append(a.astype(BF16))
                    new_p.append(cps[e] + jnp.sum(lk, axis=1, keepdims=True))
                    new_g.append(cgs[e] + jnp.sum(g, axis=1, keepdims=True))
                scatter(off, dzs, ps)
                return dq_add(dqs, dzs, ks), tuple(new_p), tuple(new_g)

            zc = tuple(jnp.zeros((tq, 1), F32) for _ in range(hp))
            carry = lax.fori_loop(0, i, lambda j, c: block(j, c, False), (dq0, zc, zc))
            dqs, _, _ = block(i, carry, True)

        if pre_scaled:
            dqs = [t * scale for t in dqs]
        if qk_w == LANES:
            for e in range(hp):
                dq_ref[0, :, e * LANES:(e + 1) * LANES] = dqs[e]
        else:
            dq_ref[0] = jnp.where(masks_qk[0], dqs[0], dqs[1])

    outs, c_outs = _call(
        body, name=name, grid=(B, nh, nq),
        in_specs=[pl.BlockSpec((1, tq, qw), lambda b, h, i: (b, i, qc + h)),
                  pl.BlockSpec((1, Sk, qw), lambda b, h, i: (b, 0, kc + h)),
                  pl.BlockSpec((1, Sk, LANES), lambda b, h, i: (b, 0, vc + h)),
                  pl.BlockSpec((1, tq, LANES), lambda b, h, i: (b, i, h)),
                  pl.BlockSpec((1, 1, tq, LANES), lambda b, h, i: (b, h, i, 0)),
                  pl.BlockSpec((1, tq, LANES), lambda b, h, i: (b, i, h))],
        out_specs=(pl.BlockSpec((1, tq, qw), lambda b, h, i: (b, i, h)),
                   pl.BlockSpec((1, Sk, qw), lambda b, h, i: (b, 0, h)),
                   pl.BlockSpec((1, Sk, LANES), lambda b, h, i: (b, 0, h))),
        out_shape=(jax.ShapeDtypeStruct((B, S, nh * qw), F32), jax.ShapeDtypeStruct((B, Sk, nh * qw), F32),
                   jax.ShapeDtypeStruct((B, Sk, nh * LANES), F32)),
        scratch_shapes=[], args=(q, k, v, o, st, do), sem=("parallel", "parallel", "arbitrary"), comm=comm)
    return (*outs, c_outs) if comm is not None else tuple(outs)


def _pos(p):
    return 4 * p[0] + 2 * p[1] + p[2]


N_CHIP = 4
MESH_ID = pl.DeviceIdType.MESH
_ANY = pl.BlockSpec(memory_space=pl.ANY)


def _me():
    return lax.axis_index("x"), lax.axis_index("y"), lax.axis_index("c")


def _other_chips(x, y):
    return [(1 - x, y), (x, 1 - y), (1 - x, 1 - y)]


class _Comm:
    def __init__(self, arrs, out_shapes, n_sem, n_local, start, finish):
        self.arrs, self.out_shapes, self.start, self.finish = list(arrs), list(out_shapes), start, finish
        self.scratch = [pltpu.SemaphoreType.DMA((n_sem,)), pltpu.SemaphoreType.DMA((n_sem,)),
                        pltpu.SemaphoreType.DMA((max(n_local, 1),))]


def _run_comm(comm, name):
    n = len(comm.arrs)

    def body(*refs):
        r = (refs[:n], refs[n:2 * n], refs[2 * n], refs[2 * n + 1], refs[2 * n + 2])
        comm.start(*r)
        comm.finish(*r)

    return pl.pallas_call(
        body, name=name, in_specs=[_ANY] * n, out_specs=[_ANY] * n, out_shape=comm.out_shapes,
        scratch_shapes=comm.scratch, compiler_params=pltpu.CompilerParams(has_side_effects=True),
    )(*comm.arrs)


def _call(body, *, name, grid, in_specs, out_specs, out_shape, scratch_shapes, args, sem, comm=None):
    in_specs, out_specs, out_shape = list(in_specs), list(out_specs), list(out_shape)
    if comm is None:
        res = pl.pallas_call(body, name=name, grid=grid, in_specs=in_specs, out_specs=out_specs, out_shape=out_shape,
                             scratch_shapes=scratch_shapes, compiler_params=_cparams(sem))(*args)
        return list(res), []
    n_in, n_out, n_scr, nc = len(in_specs), len(out_specs), len(scratch_shapes), len(comm.arrs)

    def wrapped(*refs):
        ins, refs = refs[:n_in], refs[n_in:]
        c_in, refs = refs[:nc], refs[nc:]
        outs, refs = refs[:n_out], refs[n_out:]
        c_out, refs = refs[:nc], refs[nc:]
        scr, sems = refs[:n_scr], refs[n_scr:]
        ids = [pl.program_id(a) for a in range(len(grid))]
        first = functools.reduce(jnp.logical_and, [i == 0 for i in ids])
        last = functools.reduce(jnp.logical_and, [i == g - 1 for i, g in zip(ids, grid)])

        @pl.when(first)
        def _():
            comm.start(c_in, c_out, *sems)

        body(*ins, *outs, *scr)

        @pl.when(last)
        def _():
            comm.finish(c_in, c_out, *sems)

    res = pl.pallas_call(
        wrapped, name=name, grid=grid, in_specs=in_specs + [_ANY] * nc, out_specs=out_specs + [_ANY] * nc,
        out_shape=out_shape + comm.out_shapes, scratch_shapes=list(scratch_shapes) + comm.scratch,
        compiler_params=pltpu.CompilerParams(dimension_semantics=("arbitrary",) * len(grid),
                                             vmem_limit_bytes=VMEM_LIMIT, has_side_effects=True),
    )(*args, *comm.arrs)
    return list(res[:n_out]), list(res[n_out:])


def gather_two_level(bufs, *, name=None):
    n = len(bufs)

    def parts(x_refs, out_refs, send_sems, recv_sems, local_sems):
        x, y, c = _me()
        me, sibling = (x, y, c), (x, y, 1 - c)
        chips = _other_chips(x, y)

        def copy(a, k, block, to, from_input=False):
            dst = out_refs[a].at[_pos(block)]
            return pltpu.make_async_remote_copy(src_ref=x_refs[a] if from_input else dst, dst_ref=dst,
                                                send_sem=send_sems.at[7 * a + k], recv_sem=recv_sems.at[7 * a + k],
                                                device_id=to, device_id_type=MESH_ID)

        mine = [pltpu.make_async_copy(x_refs[a], out_refs[a].at[_pos(me)], local_sems.at[a]) for a in range(n)]
        first = []
        for a in range(n):
            first.append(copy(a, 0, me, sibling, from_input=True))
            first += [copy(a, 1 + j, me, (*chip, c), from_input=True) for j, chip in enumerate(chips)]
        return copy, mine, first, me, sibling, chips, c

    def start(*refs):
        _, mine, first, *_ = parts(*refs)
        for cp in mine + first:
            cp.start()

    def finish(*refs):
        copy, mine, first, me, sibling, chips, c = parts(*refs)
        passed = []
        for j, chip in enumerate(chips):
            for a in range(n):
                copy(a, 1 + j, (*chip, c), me).wait_recv()
                passed.append(copy(a, 4 + j, (*chip, c), sibling))
                passed[-1].start()
        for a in range(n):
            copy(a, 0, sibling, me).wait_recv()
            for j, chip in enumerate(chips):
                copy(a, 4 + j, (*chip, 1 - c), me).wait_recv()
        for cp in first + passed:
            cp.wait_send()
        for cp in mine:
            cp.wait()

    out_shapes = [jax.ShapeDtypeStruct((N_DEV,) + b.shape, b.dtype) for b in bufs]
    comm = _Comm(bufs, out_shapes, 7 * n, n, start, finish)
    return _run_comm(comm, name) if name else comm


def sibling_exchange(sends, *, name=None):
    n = len(sends)

    def copies(s_refs, out_refs, send_sems, recv_sems, local_sems):
        x, y, c = _me()
        return [pltpu.make_async_remote_copy(src_ref=s_refs[a].at[1 - c], dst_ref=out_refs[a], send_sem=send_sems.at[a],
                                             recv_sem=recv_sems.at[a], device_id=(x, y, 1 - c), device_id_type=MESH_ID)
                for a in range(n)]

    def start(*refs):
        for cp in copies(*refs):
            cp.start()

    def finish(*refs):
        for cp in copies(*refs):
            cp.wait()

    out_shapes = [jax.ShapeDtypeStruct(s.shape[1:], s.dtype) for s in sends]
    comm = _Comm(sends, out_shapes, n, 0, start, finish)
    return _run_comm(comm, name) if name else comm


def _flat2(shape):
    return math.prod(shape[:-1]), shape[-1]


def _row_tile(r):
    return r if r <= 512 else _pick8(r, 256)


def _pick8(n, pref):
    t = pref
    while n % t:
        t -= 8
    return t


def chip_sum(send, got, core, *, name):
    shape = got.shape[1:]
    r, cdim = _flat2(shape)
    tr = _row_tile(r)
    send = send.reshape(2, N_CHIP, r, cdim)
    got = got.reshape(N_CHIP, r, cdim)

    def body(core_ref, s_ref, g_ref, o_ref):
        o_ref[...] = (s_ref[0].astype(F32) + g_ref[...].astype(F32)).astype(o_ref.dtype)

    blk = pl.BlockSpec((N_CHIP, tr, cdim), lambda i, core_ref: (0, i, 0))
    out = pl.pallas_call(
        body, name=name,
        grid_spec=pltpu.PrefetchScalarGridSpec(
            num_scalar_prefetch=1, grid=(r // tr,),
            in_specs=[pl.BlockSpec((1, N_CHIP, tr, cdim), lambda i, core_ref: (core_ref[0], 0, i, 0)), blk],
            out_specs=blk),
        out_shape=jax.ShapeDtypeStruct((N_CHIP, r, cdim), BF16),
        compiler_params=_cparams(("parallel",)),
    )(core, send, got)
    return out.reshape((N_CHIP,) + shape)


def chip_exchange(sums, *, name=None):
    n = len(sums)

    def copies(s_refs, out_refs, send_sems, recv_sems, local_sems):
        x, y, c = _me()
        mine = 2 * x + y
        chips = _other_chips(x, y)
        local = [pltpu.make_async_copy(s_refs[a].at[mine], out_refs[a].at[mine], local_sems.at[a]) for a in range(n)]
        sends, recvs = [], []
        for a in range(n):
            for j, (px, py) in enumerate(chips):
                sems = dict(send_sem=send_sems.at[3 * a + j], recv_sem=recv_sems.at[3 * a + j], device_id=(px, py, c),
                            device_id_type=MESH_ID)
                sends.append(pltpu.make_async_remote_copy(src_ref=s_refs[a].at[2 * px + py], dst_ref=out_refs[a].at[mine], **sems))
                recvs.append(pltpu.make_async_remote_copy(src_ref=s_refs[a].at[mine], dst_ref=out_refs[a].at[2 * px + py], **sems))
        return local, sends, recvs

    def start(*refs):
        local, sends, _ = copies(*refs)
        for cp in local + sends:
            cp.start()

    def finish(*refs):
        local, sends, recvs = copies(*refs)
        for cp in recvs:
            cp.wait_recv()
        for cp in sends:
            cp.wait_send()
        for cp in local:
            cp.wait()

    out_shapes = [jax.ShapeDtypeStruct(s.shape, s.dtype) for s in sums]
    comm = _Comm(sums, out_shapes, 3 * n, n, start, finish)
    return _run_comm(comm, name) if name else comm


def sum_adamw(parts, w, m, v, *, name):
    shape = w.shape
    R, cdim = _flat2(shape)
    n_slots = parts.shape[0]
    tr = _row_tile(R)
    parts = parts.reshape(n_slots, R, cdim)
    w, m, v = (t.reshape(1, R, cdim) for t in (w, m, v))
    c1 = 1.0 - ADAM_B1 ** ADAM_STEP
    c2 = 1.0 - ADAM_B2 ** ADAM_STEP

    def body(p_ref, w_ref, m_ref, v_ref, g_ref, d_ref, mo_ref, vo_ref):
        g = p_ref[0].astype(F32)
        for s in range(1, n_slots):
            g = g + p_ref[s].astype(F32)
        mn = ADAM_B1 * m_ref[...] + (1.0 - ADAM_B1) * g
        vn = ADAM_B2 * v_ref[...] + (1.0 - ADAM_B2) * jnp.square(g)
        g_ref[...] = g
        mo_ref[...] = mn
        vo_ref[...] = vn
        d_ref[...] = -ADAM_LR * ((mn / c1) / (jnp.sqrt(vn / c2) + ADAM_EPS) + ADAM_WD * w_ref[...])

    rs = pl.BlockSpec((None, tr, cdim), lambda i: (0, i, 0))
    sd = jax.ShapeDtypeStruct((1, R, cdim), F32)
    res = pl.pallas_call(
        body, name=name, grid=(R // tr,),
        in_specs=[pl.BlockSpec((n_slots, tr, cdim), lambda i: (0, i, 0)), rs, rs, rs],
        out_specs=(rs, rs, rs, rs), out_shape=(sd, sd, sd, sd),
        compiler_params=_cparams(("parallel",)),
    )(parts, w, m, v)
    return [t.reshape(shape) for t in res]


WEIGHTS = ("ln_mix_pre", "w_in", "b_gate", "q_norm", "w_uq", "kv_norm", "w_uk", "w_uv", "mem_norm", "w_mem_kv",
           "w_branch_out", "w_out", "ln_mix_post", "ln_mlp_pre", "w_mlp_up", "w_mlp_down", "ln_mlp_post")
BIG = dict(w_in=((D_MODEL, IN_WIDTH), 1), w_uq=((Q_LORA, MLA_HEADS * (MLA_NOPE + MLA_ROPE)), 1),
           w_uk=((KV_LORA, MLA_HEADS * MLA_NOPE), 1), w_uv=((KV_LORA, MLA_HEADS * MLA_V), 1),
           w_mem_kv=((D_MODEL, 2 * QM_W), 0), w_branch_out=((N_BRANCH, BRANCH_W, D_MODEL), 2),
           w_out=((D_MODEL, D_MODEL), 0), w_mlp_up=((D_MODEL, D_FF), 1), w_mlp_down=((D_FF, D_MODEL), 0))
SMALL = tuple(n for n in WEIGHTS if n not in BIG)


def _shard_shape(name):
    full, ax = BIG[name]
    return tuple(s // N_DEV if a == ax else s for a, s in enumerate(full))


PACK_TILE = 512


def _pad_rows(a):
    r = a.shape[-2]
    to = PACK_TILE if r > PACK_TILE else 8
    pad = [(0, 0)] * a.ndim
    pad[-2] = (0, (-r) % to)
    return jnp.pad(a, pad)


def _pack_rows(arrs):
    return _pad_rows(jnp.concatenate([a.reshape(-1, LANES) for a in arrs], axis=0))


def _unpack_rows(packed, shapes, lead=()):
    out, r = [], 0
    for shp in shapes:
        n = math.prod(shp) // LANES
        out.append(packed[..., r:r + n, :].reshape(lead + tuple(shp)))
        r += n
    return out


def _slots_to_full(name, slots):
    full, ax = BIG[name]
    return jnp.moveaxis(slots, 0, ax).reshape(full)


def _full_to_slots(name, w):
    full, ax = BIG[name]
    split = full[:ax] + (N_DEV, full[ax] // N_DEV) + full[ax + 1:]
    return jnp.moveaxis(w.reshape(split), ax, 0)


def _full_to_owner(name, w):
    s = _full_to_slots(name, w)
    return jnp.swapaxes(s.reshape((N_CHIP, 2) + s.shape[1:]), 0, 1)


def _split_in(w):
    offs = np.cumsum((0,) + IN_SIZES)
    names = ("cq", "ckv", "kr", "sb", "qm", "gate")
    return {n: w[:, offs[i]:offs[i + 1]] for i, n in enumerate(names)}


def _pad_in(w):
    p = _split_in(w)
    p["kr"] = jnp.pad(p["kr"], ((0, 0), (0, KR_PAD - MLA_ROPE)))
    p["zpad"] = jnp.zeros((w.shape[0], PAD_SIZES["zpad"]), w.dtype)
    return jnp.concatenate([p[n] for n in PAD_ORDER], axis=1)


def _unpad_in(wp):
    p = {n: wp[:, PAD_OFF[n]:PAD_OFF[n] + PAD_SIZES[n]] for n in PAD_ORDER}
    p["kr"] = p["kr"][:, :MLA_ROPE]
    return jnp.concatenate([p[n] for n in ("cq", "ckv", "kr", "sb", "qm", "gate")], axis=1)


def _in_segments():
    orig = dict(zip(("cq", "ckv", "kr", "sb", "qm", "gate"), zip(np.cumsum((0,) + IN_SIZES[:-1]).tolist(), IN_SIZES)))
    shard = IN_WIDTH // N_DEV
    segs = []
    for n in PAD_ORDER:
        if n not in orig:
            continue
        o0, w = orig[n]
        c = o0
        while c < o0 + w:
            s = c // shard
            e = min(o0 + w, (s + 1) * shard)
            segs.append((s, c - s * shard, PAD_OFF[n] + (c - o0), e - c))
            c = e
    return segs


def in_slots_to_pad(slots, *, name):
    _, R, shard = slots.shape
    tr = _pick8(R, ROW_TILE)
    segs = _in_segments()

    def body(in_ref, out_ref):
        out_ref[...] = jnp.zeros_like(out_ref)
        for s, a, d, w in segs:
            out_ref[:, d:d + w] = in_ref[s, :, a:a + w]

    return pl.pallas_call(
        body, name=name, grid=(R // tr,), in_specs=[pl.BlockSpec((N_DEV, tr, shard), lambda i: (0, i, 0))],
        out_specs=pl.BlockSpec((tr, IN_PAD), lambda i: (i, 0)), out_shape=jax.ShapeDtypeStruct((R, IN_PAD), slots.dtype),
        compiler_params=_cparams(("parallel",)),
    )(slots)


def in_pad_to_owner(g, *, name):
    R = g.shape[0]
    shard = IN_WIDTH // N_DEV
    tr = _pick8(R, ROW_TILE)
    segs = _in_segments()

    def body(in_ref, out_ref):
        for s, a, d, w in segs:
            out_ref[s % 2, s // 2, :, a:a + w] = in_ref[:, d:d + w]

    return pl.pallas_call(
        body, name=name, grid=(R // tr,), in_specs=[pl.BlockSpec((tr, IN_PAD), lambda i: (i, 0))],
        out_specs=pl.BlockSpec((2, N_CHIP, tr, shard), lambda i: (0, 0, i, 0)),
        out_shape=jax.ShapeDtypeStruct((2, N_CHIP, R, shard), g.dtype),
        compiler_params=_cparams(("parallel",)),
    )(g)


def _pad_heads(w, width):
    r = w.shape[0]
    return jnp.pad(w.reshape(r, MLA_HEADS, width), ((0, 0), (0, 0), (0, HEAD_PAD - width))).reshape(r, MLA_PAD_W)


def _unpad_heads(wp, width):
    r = wp.shape[0]
    return wp.reshape(r, MLA_HEADS, HEAD_PAD)[:, :, :width].reshape(r, MLA_HEADS * width)


def _rope_inv_freq():
    half = MLA_ROPE // 2
    inv = 1.0 / (ROPE_THETA ** (jnp.arange(half, dtype=F32) * (2.0 / MLA_ROPE)))
    tab = jnp.zeros((LANES,), F32)
    tab = tab.at[MLA_NOPE:MLA_NOPE + half].set(inv).at[MLA_NOPE + half:MLA_NOPE + MLA_ROPE].set(inv)
    return tab.reshape(1, LANES)


def _as_tuple(r):
    return r if isinstance(r, tuple) else (r,)


class _Exchange:
    def __init__(self, w):
        self.w = w
        self.rest = tuple(n for n in BIG if n != "w_in")
        self.pending = {}
        self.reduced = {}

    def first_weights(self):
        (s,) = gather_two_level([self.w["w_in"].astype(BF16)], name="gather_w_in")
        return in_slots_to_pad(s, name="w_in_layout")

    def rest_comm(self):
        return gather_two_level([self.w[n].astype(BF16) for n in self.rest])

    def rest_weights(self, slots):
        return {n: s if n == "w_mlp_up" else _slots_to_full(n, s) for n, s in zip(self.rest, slots)}

    def reduce_start(self, tag, grads):
        names = tuple(grads)
        sends = [grads[n] for n in names]
        gots = sibling_exchange(sends, name=f"sibling_{tag}")
        core = lax.axis_index("c").astype(jnp.int32).reshape(1)
        sums = [chip_sum(s, g, core, name=f"chip_sum_{n}") for n, s, g in zip(names, sends, gots)]
        self.pending[tag] = names
        return chip_exchange(sums)

    def reduce_done(self, tag, recvs):
        self.reduced.update(zip(self.pending.pop(tag), recvs))


def _local_step(x, mem, positions, tgt, sm, ex):
    B, S, D = x.shape
    M = mem.shape[1]
    T = B * S
    x2 = x.reshape(T, D)
    mem2 = mem.reshape(B * M, D)
    pos = positions.reshape(T, 1).astype(F32)
    invf = _rope_inv_freq()
    w_in_pad = ex.first_weights()
    cq_col, ckv_col = PAD_OFF["cq"] // Q_LORA, PAD_OFF["ckv"] // KV_LORA
    sb_col, qm_col = PAD_OFF["sb"] // LANES, PAD_OFF["qm"] // LANES
    sb_blk = SB_HEADS * SB_DIM // LANES
    mla = dict(mode="softmax", causal=True, hp=2, qk_w=128, v_w=64, scale=(MLA_NOPE + MLA_ROPE) ** -0.5,
               nh=MLA_HEADS // 2, qc=0, kc=0, vc=0)
    sbk = dict(mode="sb", causal=True, hp=2, qk_w=64, v_w=64, scale=SB_DIM ** -0.5, nh=SB_HEADS // 2,
               qc=sb_col, kc=sb_col + sb_blk, vc=sb_col + 2 * sb_blk)
    mca = dict(mode="softmax", causal=False, hp=1, qk_w=128, v_w=128, scale=MEM_DIM ** -0.5, nh=MEM_HEADS,
               qc=qm_col, kc=0, vc=MEM_HEADS)

    h = rms_fwd(x2, sm["ln_mix_pre"], name="rms_mix_pre")
    proj = matmul(h, w_in_pad, name="mm_in")
    proj3 = proj.reshape(B, S, IN_PAD)
    o_sb, st_sb, *slots = attn_fwd(proj3, proj3, proj3, name="sb_fwd", comm=ex.rest_comm(), **sbk)
    W = ex.rest_weights(slots[0] if slots else None)
    w_uq_pad = _pad_heads(W["w_uq"], MLA_NOPE + MLA_ROPE)
    w_uk_pad = _pad_heads(W["w_uk"], MLA_NOPE)
    cqn = rms_fwd(proj, sm["q_norm"], n=Q_LORA, col=cq_col, name="rms_q")
    ckvn = rms_fwd(proj, sm["kv_norm"], n=KV_LORA, col=ckv_col, name="rms_kv")
    qpre = matmul(cqn, w_uq_pad, name="mm_uq")
    kpre = matmul(ckvn, w_uk_pad, name="mm_uk")
    v_mla = matmul(ckvn, W["w_uv"], out_dtype=BF16, name="mm_uv").reshape(B, S, -1)
    q_pad, k_pad = rope_fwd(qpre, kpre, proj, pos, invf, name="rope_fwd")
    q_pad, k_pad = q_pad.reshape(B, S, -1), k_pad.reshape(B, S, -1)
    o_mla, st_mla = attn_fwd(q_pad, k_pad, v_mla, name="mla_fwd", **mla)
    memh = rms_fwd(mem2, sm["mem_norm"], name="rms_mem")
    mkv = matmul(memh, W["w_mem_kv"], out_dtype=BF16, name="mm_memkv").reshape(B, M, -1)
    o_mem, st_mem = attn_fwd(proj3, mkv, mkv, name="mem_fwd", **mca)
    outs = [o.reshape(T, BRANCH_W) for o in (o_mla, o_sb, o_mem)]
    ps = [matmul(o, W["w_branch_out"][b], out_dtype=BF16, name=f"mm_bo{b}") for b, o in enumerate(outs)]
    merged = gate_merge(proj, sm["b_gate"], ps, name="gate_merge")
    y = matmul(merged, W["w_out"], name="mm_out")
    x1, h2 = mix_post_fwd(x2, y, sm["ln_mix_post"], sm["ln_mlp_pre"], name="mix_post")
    u = matmul(h2, W["w_mlp_up"], b_slots="n", act="relu2", out_dtype=BF16, name="mm_up")
    d = matmul(u, W["w_mlp_down"], name="mm_down")
    loss_p, dx2, dd, dg_mlp_post = loss_head(x1, d, sm["ln_mlp_post"], tgt.reshape(T, D), name="loss_head")

    da = matmul(dd, W["w_mlp_down"], tb=True, act="relu2_bwd", act_in=u, out_dtype=BF16, name="mm_down_dx")
    g_down = matmul(u, dd, ta=True, out_dtype=BF16, out_slots="m", name="mm_down_dw")
    dh2 = matmul(da, _slots_to_full("w_mlp_up", W["w_mlp_up"]), tb=True, name="mm_up_dx")
    g_up = matmul(h2, da, ta=True, out_dtype=BF16, out_slots="n", name="mm_up_dw")
    red_mlp = ex.reduce_start("mlp", dict(w_mlp_down=g_down, w_mlp_up=g_up))
    dx1, dy, dg_mlp_pre, dg_mix_post = mlp_pre_bwd(dx2, dh2, x1, sm["ln_mlp_pre"], y, sm["ln_mix_post"], name="mlp_pre_bwd")
    dmerged = matmul(dy, W["w_out"], tb=True, name="mm_out_dx")
    g_out = matmul(merged, dy, ta=True, out_dtype=BF16, name="mm_out_dw")
    dlog, dp0, dp1, dp2, db_gate = gate_bwd(dmerged, proj, sm["b_gate"], ps, name="gate_bwd")
    dps = (dp0, dp1, dp2)
    dos = [matmul(dps[b], W["w_branch_out"][b], tb=True, name=f"mm_bo{b}_dx").reshape(B, S, BRANCH_W) for b in range(N_BRANCH)]
    g_bo = jnp.stack([matmul(outs[b], dps[b], ta=True, out_dtype=BF16, name=f"mm_bo{b}_dw") for b in range(N_BRANCH)])
    red_mix = ex.reduce_start("mix", dict(w_out=_full_to_owner("w_out", g_out),
                                          w_branch_out=_full_to_owner("w_branch_out", g_bo)))
    dq_pad, dk_pad, dv_mla, *got = attn_bwd(q_pad, k_pad, v_mla, o_mla, st_mla, dos[0], name="mla_bwd", comm=red_mlp, **mla)
    ex.reduce_done("mlp", got[0] if got else None)
    dsq, dsk, dsv, *got = attn_bwd(proj3, proj3, proj3, o_sb, st_sb, dos[1], name="sb_bwd", comm=red_mix, **sbk)
    ex.reduce_done("mix", got[0] if got else None)
    dqm, dmk, dmv = attn_bwd(proj3, mkv, mkv, o_mem, st_mem, dos[2], name="mem_bwd", **mca)
    dmkv = jnp.concatenate([dmk, dmv], axis=-1).astype(BF16).reshape(B * M, -1)
    dmemh = matmul(dmkv, W["w_mem_kv"], tb=True, name="mm_memkv_dx")
    g_memkv = matmul(memh, dmkv, ta=True, out_dtype=BF16, name="mm_memkv_dw")
    _, dg_mem_norm = rms_bwd(dmemh, mem2, sm["mem_norm"], name="rms_mem_bwd")
    dq_pad, dk_pad, dv_mla = dq_pad.reshape(T, -1), dk_pad.reshape(T, -1), dv_mla.reshape(T, -1)
    dqpre, dkr = rope_bwd(dq_pad, dk_pad, pos, invf, name="rope_bwd")
    dcqn = matmul(dqpre, w_uq_pad, tb=True, name="mm_uq_dx")
    g_uq = _unpad_heads(matmul(cqn, dqpre, ta=True, out_dtype=BF16, name="mm_uq_dw"), MLA_NOPE + MLA_ROPE)
    dckvn = matmul(dk_pad, w_uk_pad, tb=True, name="mm_uk_dx")
    dckvn = matmul(dv_mla, W["w_uv"], tb=True, acc=dckvn, name="mm_uv_dx")
    g_uk = _unpad_heads(matmul(ckvn, dk_pad, ta=True, out_dtype=BF16, name="mm_uk_dw"), MLA_NOPE)
    g_uv = matmul(ckvn, dv_mla, ta=True, out_dtype=BF16, name="mm_uv_dw")
    dcq, dg_q_norm = rms_bwd(dcqn, proj, sm["q_norm"], n=Q_LORA, col=cq_col, out_dtype=BF16, name="rms_q_bwd")
    dckv, dg_kv_norm = rms_bwd(dckvn, proj, sm["kv_norm"], n=KV_LORA, col=ckv_col, out_dtype=BF16, name="rms_kv_bwd")
    pieces = dict(gate=dlog, sb=jnp.concatenate([dsq, dsk, dsv], axis=-1).reshape(T, -1), qm=dqm.reshape(T, -1),
                  ckv=dckv, cq=dcq, kr=dkr, zpad=jnp.zeros((T, PAD_SIZES["zpad"]), BF16))
    dproj = jnp.concatenate([pieces[n].astype(BF16) for n in PAD_ORDER], axis=1)
    g_in = in_pad_to_owner(matmul(h, dproj, ta=True, out_dtype=BF16, name="mm_in_dw"), name="g_in_layout")
    last = dict(w_uq=g_uq, w_uk=g_uk, w_uv=g_uv, w_mem_kv=g_memkv)
    red_in = ex.reduce_start("in", dict(w_in=g_in, **{n: _full_to_owner(n, g) for n, g in last.items()}))
    dh, *got = _as_tuple(matmul(dproj, w_in_pad, tb=True, comm=red_in, name="mm_in_dx"))
    ex.reduce_done("in", got[0] if got else None)
    dx, dg_mix_pre = rms_bwd(dh, x2, sm["ln_mix_pre"], residual=dx1, name="rms_mix_pre_bwd")

    small = dict(ln_mix_pre=dg_mix_pre, b_gate=db_gate, q_norm=dg_q_norm, kv_norm=dg_kv_norm, mem_norm=dg_mem_norm,
                 ln_mix_post=dg_mix_post, ln_mlp_pre=dg_mlp_pre, ln_mlp_post=dg_mlp_post)
    return loss_p, dx.reshape(B, S, D), small


def kernel(x, mem, positions, ln_mix_pre, w_in, b_gate, q_norm, w_uq, kv_norm, w_uk, w_uv, mem_norm, w_mem_kv, w_branch_out, w_out, ln_mix_post, ln_mlp_pre, w_mlp_up, w_mlp_down, ln_mlp_post, loss_target, m_ln_mix_pre, m_w_in, m_b_gate, m_q_norm, m_w_uq, m_kv_norm, m_w_uk, m_w_uv, m_mem_norm, m_w_mem_kv, m_w_branch_out, m_w_out, m_ln_mix_post, m_ln_mlp_pre, m_w_mlp_up, m_w_mlp_down, m_ln_mlp_post, v_ln_mix_pre, v_w_in, v_b_gate, v_q_norm, v_w_uq, v_kv_norm, v_w_uk, v_w_uv, v_mem_norm, v_w_mem_kv, v_w_branch_out, v_w_out, v_ln_mix_post, v_ln_mlp_pre, v_w_mlp_up, v_w_mlp_down, v_ln_mlp_post):
    given = dict(locals())
    w = {n: given[n][0] for n in WEIGHTS}
    m = {n: given["m_" + n][0] for n in WEIGHTS}
    v = {n: given["v_" + n][0] for n in WEIGHTS}
    sm = {n: w[n].reshape(1, -1) for n in SMALL}
    ex = _Exchange(w)
    loss_row, grad_x, g_small = _local_step(x, mem, positions, loss_target, sm, ex)

    res = {n: sum_adamw(ex.reduced[n], given[n], given["m_" + n], given["v_" + n], name=f"adamw_{n}") for n in BIG}

    small_shapes = [w[n].shape for n in SMALL] + [(LANES,)]
    parts = gather_two_level([_pack_rows([g_small[n] for n in SMALL] + [loss_row])], name="gather_small_grads")[0]
    no_row = jnp.zeros((1, LANES), F32)
    res_small = sum_adamw(parts, *[_pack_rows([t[n] for n in SMALL] + [no_row]) for t in (w, m, v)], name="adamw_replicated")
    res_small = [_unpack_rows(r, small_shapes) for r in res_small]
    for i, n in enumerate(SMALL):
        res[n] = [r[i][None] for r in res_small]
    loss = res_small[0][len(SMALL)][0]

    out = [loss, grad_x]
    for k in range(4):
        out += [res[n][k] for n in WEIGHTS]
    return tuple(out)
```

```python
import functools
import math

import numpy as np
import jax
import jax.numpy as jnp
from jax import lax
from jax.experimental import pallas as pl
from jax.experimental.pallas import tpu as pltpu

F32 = jnp.float32
BF16 = jnp.bfloat16

D_MODEL = 1024
MEM_HEADS, MEM_DIM = 4, 128
MLA_HEADS, MLA_NOPE, MLA_ROPE, MLA_V = 8, 64, 32, 64
Q_LORA, KV_LORA = 384, 256
ROPE_THETA = 10000.0
SB_HEADS, SB_DIM = 8, 64
D_FF = 4 * D_MODEL
N_BRANCH, BRANCH_W = 3, 512
EPS = 1e-6
SB_W = 3 * SB_HEADS * SB_DIM
QM_W = MEM_HEADS * MEM_DIM
GATE_W = N_BRANCH * D_MODEL
IN_SIZES = (Q_LORA, KV_LORA, MLA_ROPE, SB_W, QM_W, GATE_W)
IN_WIDTH = sum(IN_SIZES)
ADAM_LR, ADAM_B1, ADAM_B2, ADAM_EPS, ADAM_WD, ADAM_STEP = 0.001, 0.9, 0.999, 1e-08, 0.01, 10

N_DEV = 8
LANES = 128
KR_PAD = LANES
PAD_ORDER = ("gate", "cq", "kr", "ckv", "sb", "qm", "zpad")
MM_TILE = 1024
MM_ROWS = 2048
MM_DEPTH = 4096
MM_VMEM = 40 * 1024 * 1024
PAD_SIZES = dict(gate=GATE_W, cq=Q_LORA, kr=KR_PAD, ckv=KV_LORA, sb=SB_W, qm=QM_W)
PAD_SIZES["zpad"] = (-sum(PAD_SIZES.values())) % MM_TILE
PAD_OFF = {}
_o = 0
for _n in PAD_ORDER:
    PAD_OFF[_n] = _o
    _o += PAD_SIZES[_n]
IN_PAD = _o
HEAD_PAD = LANES
MLA_PAD_W = MLA_HEADS * HEAD_PAD
VMEM_LIMIT = 48 * 1024 * 1024


def _cparams(sem):
    return pltpu.CompilerParams(dimension_semantics=sem, vmem_limit_bytes=VMEM_LIMIT)


def _pick(n, pref):
    if n <= pref:
        return n
    t = (pref // LANES) * LANES
    while t >= LANES:
        if n % t == 0:
            return t
        t -= LANES
    return n


def matmul(a, b, *, name, ta=False, tb=False, out_dtype=F32, acc=None, act=None, act_in=None,
           b_slots=None, out_slots=None, comm=None, tm_pref=MM_ROWS, tn_pref=MM_TILE, tk_pref=MM_DEPTH):
    M, K = (a.shape[1], a.shape[0]) if ta else a.shape
    tm, tk = _pick(M, tm_pref), _pick(K, tk_pref)
    if b_slots == "n":
        assert not tb and b.shape[:2] == (N_DEV, K)
        tn = b.shape[2]
        N = N_DEV * tn
    elif b_slots == "k":
        assert tb and N_DEV * b.shape[2] == K
        N, tk = b.shape[1], b.shape[2]
        tn = _pick(N, tn_pref)
    else:
        N = b.shape[0] if tb else b.shape[1]
        assert (b.shape[1] if tb else b.shape[0]) == K
        tn = _pick(N, tn_pref)
    if out_slots == "m":
        tm = M // N_DEV
    elif out_slots == "n":
        tn = N // N_DEV
    assert out_slots is None or (acc is None and act_in is None)

    def block_bytes(tm_, tk_):
        ab = tm_ * tk_ * a.dtype.itemsize + tk_ * tn * b.dtype.itemsize
        per_out = jnp.dtype(out_dtype).itemsize + sum(t.dtype.itemsize for t in (acc, act_in) if t is not None)
        return 2 * (ab + tm_ * tn * per_out) + (tm_ * tn * 4 if K // tk_ > 1 else 0)

    while block_bytes(tm, tk) > MM_VMEM:
        if tm > 512 and out_slots != "m" and M % (tm // 2) == 0:
            tm //= 2
        elif tk > 512 and b_slots != "k" and K % (tk // 2) == 0:
            tk //= 2
        else:
            break
    nk = K // tk
    dims = (((0 if ta else 1,), (1 if tb else 0,)), ((), ()))
    assert act in (None, "relu2", "relu2_bwd") and (act == "relu2_bwd") == (act_in is not None)

    def body(*refs):
        a_ref, b_ref = refs[0], refs[1]
        pos = 2
        acc_ref = act_ref = None
        if acc is not None:
            acc_ref = refs[pos]
            pos += 1
        if act_in is not None:
            act_ref = refs[pos]
            pos += 1
        out = refs[pos]
        scratch = refs[pos + 1:]

        part = lax.dot_general(a_ref[...].astype(BF16), b_ref[...].astype(BF16), dims,
                               preferred_element_type=F32)

        def finish(r):
            if acc_ref is not None:
                r = r + acc_ref[...]
            if act == "relu2":
                r = jnp.square(jnp.maximum(r, 0.0))
            elif act == "relu2_bwd":
                r = r * (2.0 * jnp.sqrt(act_ref[...].astype(F32)))
            out[...] = r.astype(out.dtype)

        if nk == 1:
            finish(part)
        else:
            acc_sc = scratch[0]
            k = pl.program_id(2)

            @pl.when(k == 0)
            def _():
                acc_sc[...] = part

            @pl.when(k > 0)
            def _():
                acc_sc[...] += part

            @pl.when(k == nk - 1)
            def _():
                finish(acc_sc[...])

    a_spec = pl.BlockSpec((tk, tm), lambda i, j, k: (k, i)) if ta else pl.BlockSpec((tm, tk), lambda i, j, k: (i, k))
    if b_slots == "n":
        b_spec = pl.BlockSpec((None, tk, tn), lambda i, j, k: (j, k, 0))
    elif b_slots == "k":
        b_spec = pl.BlockSpec((None, tn, tk), lambda i, j, k: (k, j, 0))
    else:
        b_spec = pl.BlockSpec((tn, tk), lambda i, j, k: (j, k)) if tb else pl.BlockSpec((tk, tn), lambda i, j, k: (k, j))
    if out_slots == "m":
        o_spec = pl.BlockSpec((None, None, tm, tn), lambda i, j, k: (i % 2, i // 2, 0, j))
        o_shape = (2, N_CHIP, tm, N)
    elif out_slots == "n":
        o_spec = pl.BlockSpec((None, None, tm, tn), lambda i, j, k: (j % 2, j // 2, i, 0))
        o_shape = (2, N_CHIP, M, tn)
    else:
        o_spec = pl.BlockSpec((tm, tn), lambda i, j, k: (i, j))
        o_shape = (M, N)
    in_specs = [a_spec, b_spec]
    args = [a, b]
    if acc is not None:
        in_specs.append(o_spec)
        args.append(acc)
    if act_in is not None:
        in_specs.append(o_spec)
        args.append(act_in)
    outs, c_outs = _call(
        body, name=name, grid=(M // tm, N // tn, nk), in_specs=in_specs, out_specs=[o_spec],
        out_shape=[jax.ShapeDtypeStruct(o_shape, out_dtype)],
        scratch_shapes=[pltpu.VMEM((tm, tn), F32)] if nk > 1 else [], args=args,
        sem=("parallel", "parallel", "arbitrary"), comm=comm)
    return (outs[0], c_outs) if comm is not None else outs[0]


ROW_TILE = 256


def _rstd(xv):
    return lax.rsqrt(jnp.mean(xv * xv, axis=-1, keepdims=True) + EPS)


def _rms_bwd_rows(dy, xv, g):
    r = _rstd(xv)
    dyg = dy * g
    dx = r * dyg - xv * (r * r * r) * jnp.mean(dyg * xv, axis=-1, keepdims=True)
    return dx, dy * xv * r


def _row_spec(tm, n, col=0):
    return pl.BlockSpec((tm, n), lambda i: (i, col))


def _vec_spec(n):
    return pl.BlockSpec((1, n), lambda i: (0, 0))


def _acc_rows(ref, i, val):
    @pl.when(i == 0)
    def _():
        ref[...] = val

    @pl.when(i > 0)
    def _():
        ref[...] += val


def rms_fwd(x, g, *, name, n=None, col=0, out_dtype=BF16, comm=None):
    T = x.shape[0]
    n = x.shape[1] if n is None else n
    tm = _pick(T, ROW_TILE)

    def body(x_ref, g_ref, o_ref):
        xv = x_ref[...]
        o_ref[...] = (xv * _rstd(xv) * g_ref[...]).astype(o_ref.dtype)

    outs, c_outs = _call(
        body, name=name, grid=(T // tm,), in_specs=[_row_spec(tm, n, col), _vec_spec(n)],
        out_specs=[_row_spec(tm, n)], out_shape=[jax.ShapeDtypeStruct((T, n), out_dtype)],
        scratch_shapes=[], args=(x, g), sem=("parallel",), comm=comm)
    return (outs[0], c_outs) if comm is not None else outs[0]


def rms_bwd(dy, x, g, *, name, n=None, col=0, residual=None, out_dtype=F32):
    T = x.shape[0]
    n = x.shape[1] if n is None else n
    tm = _pick(T, ROW_TILE)

    def body(*refs):
        if residual is None:
            dy_ref, x_ref, g_ref, dx_ref, dg_ref = refs
        else:
            dy_ref, x_ref, g_ref, res_ref, dx_ref, dg_ref = refs
        dx, dgr = _rms_bwd_rows(dy_ref[...].astype(F32), x_ref[...], g_ref[...])
        if residual is not None:
            dx = dx + res_ref[...]
        dx_ref[...] = dx.astype(dx_ref.dtype)
        _acc_rows(dg_ref, pl.program_id(0), jnp.sum(dgr, axis=0, keepdims=True))

    in_specs = [_row_spec(tm, n), _row_spec(tm, n, col), _vec_spec(n)]
    args = [dy, x, g]
    if residual is not None:
        in_specs.append(_row_spec(tm, n))
        args.append(residual)
    return pl.pallas_call(
        body, name=name, grid=(T // tm,), in_specs=in_specs,
        out_specs=(_row_spec(tm, n), _vec_spec(n)),
        out_shape=(jax.ShapeDtypeStruct((T, n), out_dtype), jax.ShapeDtypeStruct((1, n), F32)),
        compiler_params=_cparams(("arbitrary",)),
    )(*args)


def mix_post_fwd(x, y, g_post, g_pre2, *, name):
    T, n = x.shape
    tm = _pick(T, ROW_TILE)

    def body(x_ref, y_ref, gp_ref, g2_ref, x1_ref, h2_ref):
        yv = y_ref[...]
        x1 = x_ref[...] + yv * _rstd(yv) * gp_ref[...]
        x1_ref[...] = x1
        h2_ref[...] = (x1 * _rstd(x1) * g2_ref[...]).astype(h2_ref.dtype)

    return pl.pallas_call(
        body, name=name, grid=(T // tm,),
        in_specs=[_row_spec(tm, n), _row_spec(tm, n), _vec_spec(n), _vec_spec(n)],
        out_specs=(_row_spec(tm, n), _row_spec(tm, n)),
        out_shape=(jax.ShapeDtypeStruct((T, n), F32), jax.ShapeDtypeStruct((T, n), BF16)),
        compiler_params=_cparams(("parallel",)),
    )(x, y, g_post, g_pre2)


def loss_head(x1, d, g, tgt, *, name):
    T, n = x1.shape
    tm = _pick(T, ROW_TILE)

    def body(x1_ref, d_ref, g_ref, t_ref, loss_ref, dx2_ref, dd_ref, dg_ref):
        i = pl.program_id(0)
        dv = d_ref[...]
        gv = g_ref[...]
        err = x1_ref[...] + dv * _rstd(dv) * gv - t_ref[...]
        part = 0.5 * jnp.sum(jnp.mean(err * err, axis=-1, keepdims=True), axis=0, keepdims=True)
        _acc_rows(loss_ref, i, jnp.broadcast_to(part, (1, LANES)))
        dx2 = err * (1.0 / n)
        dx2_ref[...] = dx2
        dd, dgr = _rms_bwd_rows(dx2, dv, gv)
        dd_ref[...] = dd.astype(dd_ref.dtype)
        _acc_rows(dg_ref, i, jnp.sum(dgr, axis=0, keepdims=True))

    return pl.pallas_call(
        body, name=name, grid=(T // tm,),
        in_specs=[_row_spec(tm, n), _row_spec(tm, n), _vec_spec(n), _row_spec(tm, n)],
        out_specs=(_vec_spec(LANES), _row_spec(tm, n), _row_spec(tm, n), _vec_spec(n)),
        out_shape=(jax.ShapeDtypeStruct((1, LANES), F32), jax.ShapeDtypeStruct((T, n), F32),
                   jax.ShapeDtypeStruct((T, n), BF16), jax.ShapeDtypeStruct((1, n), F32)),
        compiler_params=_cparams(("arbitrary",)),
    )(x1, d, g, tgt)


def mlp_pre_bwd(dx2, dh2, x1, g_pre2, y, g_post, *, name, comm=None):
    T, n = x1.shape
    tm = _pick(T, ROW_TILE)

    def body(dx2_ref, dh2_ref, x1_ref, g2_ref, y_ref, gp_ref, dx1_ref, dy_ref, dg2_ref, dgp_ref):
        i = pl.program_id(0)
        d1, dg2 = _rms_bwd_rows(dh2_ref[...], x1_ref[...], g2_ref[...])
        dx1 = dx2_ref[...] + d1
        dx1_ref[...] = dx1
        dy, dgp = _rms_bwd_rows(dx1, y_ref[...], gp_ref[...])
        dy_ref[...] = dy.astype(dy_ref.dtype)
        _acc_rows(dg2_ref, i, jnp.sum(dg2, axis=0, keepdims=True))
        _acc_rows(dgp_ref, i, jnp.sum(dgp, axis=0, keepdims=True))

    rs = _row_spec(tm, n)
    outs, c_outs = _call(
        body, name=name, grid=(T // tm,),
        in_specs=[rs, rs, rs, _vec_spec(n), rs, _vec_spec(n)],
        out_specs=(rs, rs, _vec_spec(n), _vec_spec(n)),
        out_shape=(jax.ShapeDtypeStruct((T, n), F32), jax.ShapeDtypeStruct((T, n), BF16),
                   jax.ShapeDtypeStruct((1, n), F32), jax.ShapeDtypeStruct((1, n), F32)),
        scratch_shapes=[], args=(dx2, dh2, x1, g_pre2, y, g_post), sem=("arbitrary",), comm=comm)
    return (*outs, c_outs) if comm is not None else tuple(outs)


def _gate_specs(tm):
    gcol = PAD_OFF["gate"] // D_MODEL
    lspecs = [pl.BlockSpec((tm, D_MODEL), functools.partial(lambda i, c: (i, c), c=gcol + b)) for b in range(N_BRANCH)]
    bspecs = [pl.BlockSpec((1, D_MODEL), functools.partial(lambda i, c: (0, c), c=b)) for b in range(N_BRANCH)]
    pspecs = [_row_spec(tm, D_MODEL) for _ in range(N_BRANCH)]
    return lspecs, bspecs, pspecs


def gate_merge(proj, b_gate, ps, *, name):
    T = proj.shape[0]
    tm = _pick(T, ROW_TILE)
    lspecs, bspecs, pspecs = _gate_specs(tm)

    def body(*refs):
        l_refs, b_refs, p_refs, o_ref = refs[0:3], refs[3:6], refs[6:9], refs[9]
        tot = None
        for lr, br, pr in zip(l_refs, b_refs, p_refs):
            term = jax.nn.sigmoid(lr[...] + br[...]) * pr[...].astype(F32)
            tot = term if tot is None else tot + term
        o_ref[...] = tot.astype(o_ref.dtype)

    return pl.pallas_call(
        body, name=name, grid=(T // tm,), in_specs=lspecs + bspecs + pspecs,
        out_specs=_row_spec(tm, D_MODEL), out_shape=jax.ShapeDtypeStruct((T, D_MODEL), BF16),
        compiler_params=_cparams(("parallel",)),
    )(proj, proj, proj, b_gate, b_gate, b_gate, *ps)


def gate_bwd(dmerged, proj, b_gate, ps, *, name):
    T = proj.shape[0]
    tm = _pick(T, ROW_TILE)
    lspecs, bspecs, pspecs = _gate_specs(tm)

    def body(*refs):
        dm_ref = refs[0]
        l_refs, b_refs, p_refs = refs[1:4], refs[4:7], refs[7:10]
        dl_ref, dp_refs, db_ref = refs[10], refs[11:14], refs[14]
        i = pl.program_id(0)
        dm = dm_ref[...]
        for b, (lr, br, pr, dpr) in enumerate(zip(l_refs, b_refs, p_refs, dp_refs)):
            gt = jax.nn.sigmoid(lr[...] + br[...])
            dpr[...] = (dm * gt).astype(dpr.dtype)
            dl = dm * pr[...].astype(F32) * gt * (1.0 - gt)
            dl_ref[:, b * D_MODEL:(b + 1) * D_MODEL] = dl.astype(dl_ref.dtype)
            part = jnp.sum(dl, axis=0, keepdims=True)

            @pl.when(i == 0)
            def _():
                db_ref[:, b * D_MODEL:(b + 1) * D_MODEL] = part

            @pl.when(i > 0)
            def _():
                db_ref[:, b * D_MODEL:(b + 1) * D_MODEL] += part

    return pl.pallas_call(
        body, name=name, grid=(T // tm,), in_specs=[_row_spec(tm, D_MODEL)] + lspecs + bspecs + pspecs,
        out_specs=(_row_spec(tm, GATE_W), *[_row_spec(tm, D_MODEL) for _ in range(N_BRANCH)], _vec_spec(GATE_W)),
        out_shape=(jax.ShapeDtypeStruct((T, GATE_W), BF16),
                   *[jax.ShapeDtypeStruct((T, D_MODEL), BF16) for _ in range(N_BRANCH)],
                   jax.ShapeDtypeStruct((1, GATE_W), F32)),
        compiler_params=_cparams(("arbitrary",)),
    )(dmerged, proj, proj, proj, b_gate, b_gate, b_gate, *ps)


def _rope_tables(pos_ref, invf_ref):
    ang = pos_ref[...] * invf_ref[...]
    lane = lax.broadcasted_iota(jnp.int32, (1, LANES), 1)
    s = jnp.sin(ang)
    split = MLA_NOPE + MLA_ROPE // 2
    return jnp.cos(ang), jnp.where(lane >= split, s, 0.0), jnp.where(lane < split, s, 0.0)


def _rotate(xh, c, s_hi, s_lo, sign):
    half = MLA_ROPE // 2
    up = pltpu.roll(xh, half, 1)
    down = pltpu.roll(xh, LANES - half, 1)
    return xh * c + sign * (up * s_hi - down * s_lo)


def rope_fwd(qpre, kpre, proj, pos, invf, *, name):
    T = qpre.shape[0]
    tm = _pick(T, ROW_TILE)
    kr_col = PAD_OFF["kr"] // KR_PAD

    def body(q_ref, k_ref, kr_ref, pos_ref, invf_ref, qo_ref, ko_ref):
        c, s_hi, s_lo = _rope_tables(pos_ref, invf_ref)
        kr = _rotate(pltpu.roll(kr_ref[...], MLA_NOPE, 1), c, s_hi, s_lo, 1.0)
        for h in range(MLA_HEADS):
            sl = slice(h * HEAD_PAD, (h + 1) * HEAD_PAD)
            qo_ref[:, sl] = _rotate(q_ref[:, sl], c, s_hi, s_lo, 1.0).astype(qo_ref.dtype)
            ko_ref[:, sl] = (k_ref[:, sl] + kr).astype(ko_ref.dtype)

    rs = _row_spec(tm, MLA_PAD_W)
    return pl.pallas_call(
        body, name=name, grid=(T // tm,),
        in_specs=[rs, rs, _row_spec(tm, KR_PAD, kr_col), _row_spec(tm, 1), _vec_spec(LANES)],
        out_specs=(rs, rs),
        out_shape=(jax.ShapeDtypeStruct((T, MLA_PAD_W), BF16), jax.ShapeDtypeStruct((T, MLA_PAD_W), BF16)),
        compiler_params=_cparams(("parallel",)),
    )(qpre, kpre, proj, pos, invf)


def rope_bwd(dq_pad, dk_pad, pos, invf, *, name):
    T = dq_pad.shape[0]
    tm = _pick(T, ROW_TILE)

    def body(dq_ref, dk_ref, pos_ref, invf_ref, dqo_ref, dkr_ref):
        c, s_hi, s_lo = _rope_tables(pos_ref, invf_ref)
        tot = None
        for h in range(MLA_HEADS):
            sl = slice(h * HEAD_PAD, (h + 1) * HEAD_PAD)
            dqo_ref[:, sl] = _rotate(dq_ref[:, sl], c, s_hi, s_lo, -1.0).astype(dqo_ref.dtype)
            tot = dk_ref[:, sl] if tot is None else tot + dk_ref[:, sl]
        dkr = pltpu.roll(_rotate(tot, c, s_hi, s_lo, -1.0), LANES - MLA_NOPE, 1)
        lane = lax.broadcasted_iota(jnp.int32, (1, LANES), 1)
        dkr_ref[...] = jnp.where(lane < MLA_ROPE, dkr, 0.0).astype(dkr_ref.dtype)

    rs = _row_spec(tm, MLA_PAD_W)
    return pl.pallas_call(
        body, name=name, grid=(T // tm,),
        in_specs=[rs, rs, _row_spec(tm, 1), _vec_spec(LANES)],
        out_specs=(rs, _row_spec(tm, KR_PAD)),
        out_shape=(jax.ShapeDtypeStruct((T, MLA_PAD_W), BF16), jax.ShapeDtypeStruct((T, KR_PAD), BF16)),
        compiler_params=_cparams(("parallel",)),
    )(dq_pad, dk_pad, pos, invf)


ATT_TILE = 512
NEG = -1e30


def _split_bf16(v):
    hi = v.astype(BF16)
    return hi, (v - hi.astype(F32)).astype(BF16)


def _mul(x, s):
    return x if s == 1.0 else x * s


def _dot(a, b, ca, cb):
    return lax.dot_general(a, b, (((ca,), (cb,)), ((), ())), preferred_element_type=F32)


CUM_CHUNK = 256


def _tri(kind, terms=2):
    r = lax.broadcasted_iota(jnp.int32, (CUM_CHUNK, CUM_CHUNK), 0)
    c = lax.broadcasted_iota(jnp.int32, (CUM_CHUNK, CUM_CHUNK), 1)
    m = {"gt": r > c, "le": r <= c, "lt": r < c}[kind]
    t = jnp.where(m, 1.0, 0.0).astype(BF16)
    return jnp.concatenate([t] * terms, axis=0)


def _cum(v, tri, kind):
    n = v.shape[1] // CUM_CHUNK
    two = tri.shape[0] == 2 * CUM_CHUNK
    chunks = [v[:, c * CUM_CHUNK:(c + 1) * CUM_CHUNK] for c in range(n)]
    totals = [jnp.sum(ch, axis=1, keepdims=True) for ch in chunks] if n > 1 else None
    outs = []
    for c, ch in enumerate(chunks):
        r = _dot(jnp.concatenate(_split_bf16(ch), axis=1) if two else ch.astype(BF16), tri, 1, 0)
        others = [] if n == 1 else totals[c + 1:] if kind == "gt" else totals[:c]
        for t in others:
            r = r + t
        outs.append(r)
    return outs[0] if n == 1 else jnp.concatenate(outs, axis=1)


def _lane_masks(hp, w):
    lane = lax.broadcasted_iota(jnp.int32, (1, LANES), 1)
    return [(lane >= e * w) & (lane < (e + 1) * w) for e in range(hp)]


def _att_dims(mode, hp, qk_w, v_w):
    assert mode in ("softmax", "sb")
    assert (hp, qk_w, v_w) in ((2, 128, 64), (2, 64, 64), (1, 128, 128))
    qw = hp * LANES if qk_w == LANES else LANES
    return qw


def attn_fwd(q, k, v, *, name, mode, causal, hp, qk_w, v_w, scale, nh, qc, kc, vc, comm=None):
    qw = _att_dims(mode, hp, qk_w, v_w)
    B, S = q.shape[0], q.shape[1]
    Sk = k.shape[1]
    tq = _pick(S, ATT_TILE)
    tk = _pick(Sk, ATT_TILE)
    if causal:
        assert tq == tk and S == Sk
    nq, nkv = S // tq, Sk // tk
    pre_scaled = math.frexp(scale)[0] == 0.5
    post = 1.0 if pre_scaled else scale

    def body(q_ref, k_ref, v_ref, o_ref, st_ref):
        i = pl.program_id(2)
        qv = q_ref[0].astype(BF16)
        masks_qk = _lane_masks(hp, qk_w) if qk_w < LANES else None
        masks_v = _lane_masks(hp, v_w) if v_w < LANES else None
        if qk_w == LANES:
            qs = [qv[:, e * LANES:(e + 1) * LANES] for e in range(hp)]
        else:
            qs = [jnp.where(masks_qk[e], qv, jnp.zeros_like(qv)) for e in range(hp)]
        if pre_scaled:
            qs = [t * scale for t in qs]
        rows = lax.broadcasted_iota(jnp.int32, (tq, tk), 0)
        cols = lax.broadcasted_iota(jnp.int32, (tq, tk), 1)
        tri = _tri("gt") if mode == "sb" else None

        def kv_tile(j):
            off = pl.multiple_of(j * tk, tk)
            kj = k_ref[0, pl.ds(off, tk), :].astype(BF16)
            vj = v_ref[0, pl.ds(off, tk), :].astype(BF16)
            ks = [kj[:, e * LANES:(e + 1) * LANES] for e in range(hp)] if qk_w == LANES else [kj] * hp
            return ks, vj

        def merge(vals):
            if hp == 1:
                return jnp.broadcast_to(vals[0], (tq, LANES))
            return jnp.where(masks_v[0], vals[0], vals[1])

        if mode == "softmax":
            def block(j, carry, diag):
                ms, ls, acc = carry
                ks, vj = kv_tile(j)
                new_m, new_l, alphas, pvs = [], [], [], []
                for e in range(hp):
                    s = _mul(_dot(qs[e], ks[e], 1, 1), post)
                    if diag:
                        s = jnp.where(rows >= cols, s, NEG)
                    m_new = jnp.maximum(ms[e], jnp.max(s, axis=1, keepdims=True))
                    alpha = jnp.exp(ms[e] - m_new)
                    p = jnp.exp(s - m_new)
                    new_l.append(alpha * ls[e] + jnp.sum(p, axis=1, keepdims=True))
                    new_m.append(m_new)
                    alphas.append(alpha)
                    pvs.append(_dot(p.astype(BF16), vj, 1, 0))
                acc = acc * merge(alphas) + merge(pvs)
                return tuple(new_m), tuple(new_l), acc

            init = (tuple(jnp.full((tq, 1), NEG, F32) for _ in range(hp)),
                    tuple(jnp.zeros((tq, 1), F32) for _ in range(hp)),
                    jnp.zeros((tq, LANES), F32))
            if causal:
                carry = lax.fori_loop(0, i, lambda j, c: block(j, c, False), init)
                ms, ls, acc = block(i, carry, True)
            else:
                ms, ls, acc = lax.fori_loop(0, nkv, lambda j, c: block(j, c, False), init)
            o_ref[0] = acc / merge(list(ls))
            st_ref[0, 0] = merge([m + jnp.log(l) for m, l in zip(ms, ls)])
        else:
            def block(j, carry, diag):
                cs_, acc = carry
                ks, vj = kv_tile(j)
                new_c, pvs = [], []
                for e in range(hp):
                    z = _mul(_dot(qs[e], ks[e], 1, 1), post)
                    lk = -jnp.maximum(z, 0.0) - jnp.log(1.0 + jnp.exp(-jnp.abs(z)))
                    lz = lk + z
                    if diag:
                        lk = jnp.where(rows > cols, lk, 0.0)
                    a = jnp.exp(lz + _cum(lk, tri, "gt") + cs_[e])
                    if diag:
                        a = jnp.where(rows > cols, a, 0.0)
                    pvs.append(_dot(a.astype(BF16), vj, 1, 0))
                    new_c.append(cs_[e] + jnp.sum(lk, axis=1, keepdims=True))
                return tuple(new_c), acc + merge(pvs)

            init = (tuple(jnp.zeros((tq, 1), F32) for _ in range(hp)), jnp.zeros((tq, LANES), F32))
            carry = block(i, init, True)
            cs_, acc = lax.fori_loop(0, i, lambda jj, c: block(i - 1 - jj, c, False), carry)
            o_ref[0] = acc
            st_ref[0, 0] = merge(list(cs_))

    outs, c_outs = _call(
        body, name=name, grid=(B, nh, nq),
        in_specs=[pl.BlockSpec((1, tq, qw), lambda b, h, i: (b, i, qc + h)),
                  pl.BlockSpec((1, Sk, qw), lambda b, h, i: (b, 0, kc + h)),
                  pl.BlockSpec((1, Sk, LANES), lambda b, h, i: (b, 0, vc + h))],
        out_specs=(pl.BlockSpec((1, tq, LANES), lambda b, h, i: (b, i, h)),
                   pl.BlockSpec((1, 1, tq, LANES), lambda b, h, i: (b, h, i, 0))),
        out_shape=(jax.ShapeDtypeStruct((B, S, nh * LANES), F32), jax.ShapeDtypeStruct((B, nh, S, LANES), F32)),
        scratch_shapes=[], args=(q, k, v), sem=("parallel", "parallel", "arbitrary"), comm=comm)
    return (*outs, c_outs) if comm is not None else tuple(outs)


def attn_bwd(q, k, v, o, st, do, *, name, mode, causal, hp, qk_w, v_w, scale, nh, qc, kc, vc, comm=None):
    qw = _att_dims(mode, hp, qk_w, v_w)
    B, S = q.shape[0], q.shape[1]
    Sk = k.shape[1]
    tq = _pick(S, ATT_TILE)
    tk = _pick(Sk, ATT_TILE)
    if causal:
        assert tq == tk and S == Sk
    nq, nkv = S // tq, Sk // tk
    pre_scaled = math.frexp(scale)[0] == 0.5
    post = 1.0 if pre_scaled else scale

    def body(q_ref, k_ref, v_ref, o_ref, st_ref, do_ref, dq_ref, dk_ref, dv_ref):
        i = pl.program_id(2)

        @pl.when(i == 0)
        def _():
            dk_ref[...] = jnp.zeros_like(dk_ref)
            dv_ref[...] = jnp.zeros_like(dv_ref)

        qv = q_ref[0].astype(BF16)
        dov = do_ref[0]
        stv = st_ref[0, 0]
        masks_qk = _lane_masks(hp, qk_w) if qk_w < LANES else None
        masks_v = _lane_masks(hp, v_w) if v_w < LANES else None
        if qk_w == LANES:
            qs = [qv[:, e * LANES:(e + 1) * LANES] for e in range(hp)]
        else:
            qs = [jnp.where(masks_qk[e], qv, jnp.zeros_like(qv)) for e in range(hp)]
        if pre_scaled:
            qs = [t * scale for t in qs]
        if hp == 1:
            dos = [dov.astype(BF16)]
            stats = [stv[:, 0:1]]
        else:
            dos = [jnp.where(masks_v[e], dov, 0.0).astype(BF16) for e in range(hp)]
            stats = [stv[:, e * v_w:e * v_w + 1] for e in range(hp)]
        if mode == "softmax":
            prod = dov * o_ref[0]
            if hp == 1:
                dsum = [jnp.sum(prod, axis=1, keepdims=True)]
            else:
                dsum = [jnp.sum(jnp.where(masks_v[e], prod, 0.0), axis=1, keepdims=True) for e in range(hp)]
        rows = lax.broadcasted_iota(jnp.int32, (tq, tk), 0)
        cols = lax.broadcasted_iota(jnp.int32, (tq, tk), 1)
        if mode == "sb":
            tri_le, tri_lt = _tri("le"), _tri("lt", terms=1)

        def kv_tile(j):
            off = pl.multiple_of(j * tk, tk)
            kj = k_ref[0, pl.ds(off, tk), :].astype(BF16)
            vj = v_ref[0, pl.ds(off, tk), :].astype(BF16)
            ks = [kj[:, e * LANES:(e + 1) * LANES] for e in range(hp)] if qk_w == LANES else [kj] * hp
            return off, ks, vj

        def scatter(off, dz_list, p_list):
            dvj = None
            for e in range(hp):
                t = _dot(p_list[e], dos[e], 0, 0)
                dvj = t if dvj is None else dvj + t
            dv_ref[0, pl.ds(off, tk), :] += dvj
            if qk_w == LANES:
                for e in range(hp):
                    dk_ref[0, pl.ds(off, tk), e * LANES:(e + 1) * LANES] += _dot(dz_list[e], qs[e], 0, 0)
            else:
                dkj = None
                for e in range(hp):
                    t = _dot(dz_list[e], qs[e], 0, 0)
                    dkj = t if dkj is None else dkj + t
                dk_ref[0, pl.ds(off, tk), :] += dkj

        def dq_add(dqs, dz_list, ks):
            out = []
            for e in range(hp):
                out.append(dqs[e] + _dot(dz_list[e], ks[e], 1, 0))
            return tuple(out)

        dq0 = tuple(jnp.zeros((tq, LANES), F32) for _ in range(hp))

        if mode == "softmax":
            def block(j, dqs, diag):
                off, ks, vj = kv_tile(j)
                dzs, ps = [], []
                for e in range(hp):
                    s = _mul(_dot(qs[e], ks[e], 1, 1), post)
                    p = jnp.exp(s - stats[e])
                    if diag:
                        p = jnp.where(rows >= cols, p, 0.0)
                    dp = _dot(dos[e], vj, 1, 1)
                    dzs.append(_mul(p * (dp - dsum[e]), post).astype(BF16))
                    ps.append(p.astype(BF16))
                scatter(off, dzs, ps)
                return dq_add(dqs, dzs, ks)

            if causal:
                dqs = lax.fori_loop(0, i, lambda j, c: block(j, c, False), dq0)
                dqs = block(i, dqs, True)
            else:
                dqs = lax.fori_loop(0, nkv, lambda j, c: block(j, c, False), dq0)
        else:
            def block(j, carry, diag):
                dqs, cps, cgs = carry
                off, ks, vj = kv_tile(j)
                dzs, ps, new_p, new_g = [], [], [], []
                for e in range(hp):
                    z = _mul(_dot(qs[e], ks[e], 1, 1), post)
                    lk = -jnp.maximum(z, 0.0) - jnp.log(1.0 + jnp.exp(-jnp.abs(z)))
                    lz = lk + z
                    sig = jnp.exp(lz)
                    keep = 1.0 - sig
                    if diag:
                        lk = jnp.where(rows > cols, lk, 0.0)
                    keep_after = stats[e] - cps[e] - _cum(lk, tri_le, "le")
                    a = jnp.exp(lz + keep_after)
                    if diag:
                        a = jnp.where(rows > cols, a, 0.0)
                    g = _dot(dos[e], vj, 1, 1) * a
                    gsum = cgs[e] + _cum(g, tri_lt, "lt")
                    dz = _mul(g * keep - gsum * sig, post)
                    if diag:
                        dz = jnp.where(rows > cols, dz, 0.0)
                    dzs.append(dz.astype(BF16))
                    ps.append(a.astype(BF16))
                    new_p.append(cps[e] + jnp.sum(lk, axis=1, keepdims=True))
                    new_g.append(cgs[e] + jnp.sum(g, axis=1, keepdims=True))
                scatter(off, dzs, ps)
                return dq_add(dqs, dzs, ks), tuple(new_p), tuple(new_g)

            zc = tuple(jnp.zeros((tq, 1), F32) for _ in range(hp))
            carry = lax.fori_loop(0, i, lambda j, c: block(j, c, False), (dq0, zc, zc))
            dqs, _, _ = block(i, carry, True)

        if pre_scaled:
            dqs = [t * scale for t in dqs]
        if qk_w == LANES:
            for e in range(hp):
                dq_ref[0, :, e * LANES:(e + 1) * LANES] = dqs[e]
        else:
            dq_ref[0] = jnp.where(masks_qk[0], dqs[0], dqs[1])

    outs, c_outs = _call(
        body, name=name, grid=(B, nh, nq),
        in_specs=[pl.BlockSpec((1, tq, qw), lambda b, h, i: (b, i, qc + h)),
                  pl.BlockSpec((1, Sk, qw), lambda b, h, i: (b, 0, kc + h)),
                  pl.BlockSpec((1, Sk, LANES), lambda b, h, i: (b, 0, vc + h)),
                  pl.BlockSpec((1, tq, LANES), lambda b, h, i: (b, i, h)),
                  pl.BlockSpec((1, 1, tq, LANES), lambda b, h, i: (b, h, i, 0)),
                  pl.BlockSpec((1, tq, LANES), lambda b, h, i: (b, i, h))],
        out_specs=(pl.BlockSpec((1, tq, qw), lambda b, h, i: (b, i, h)),
                   pl.BlockSpec((1, Sk, qw), lambda b, h, i: (b, 0, h)),
                   pl.BlockSpec((1, Sk, LANES), lambda b, h, i: (b, 0, h))),
        out_shape=(jax.ShapeDtypeStruct((B, S, nh * qw), F32), jax.ShapeDtypeStruct((B, Sk, nh * qw), F32),
                   jax.ShapeDtypeStruct((B, Sk, nh * LANES), F32)),
        scratch_shapes=[], args=(q, k, v, o, st, do), sem=("parallel", "parallel", "arbitrary"), comm=comm)
    return (*outs, c_outs) if comm is not None else tuple(outs)


def _pos(p):
    return 4 * p[0] + 2 * p[1] + p[2]


N_CHIP = 4
MESH_ID = pl.DeviceIdType.MESH
_ANY = pl.BlockSpec(memory_space=pl.ANY)


def _me():
    return lax.axis_index("x"), lax.axis_index("y"), lax.axis_index("c")


def _other_chips(x, y):
    return [(1 - x, y), (x, 1 - y), (1 - x, 1 - y)]


class _Comm:
    def __init__(self, arrs, out_shapes, n_sem, n_local, start, finish):
        self.arrs, self.out_shapes, self.start, self.finish = list(arrs), list(out_shapes), start, finish
        self.scratch = [pltpu.SemaphoreType.DMA((n_sem,)), pltpu.SemaphoreType.DMA((n_sem,)),
                        pltpu.SemaphoreType.DMA((max(n_local, 1),))]


def _run_comm(comm, name):
    n = len(comm.arrs)

    def body(*refs):
        r = (refs[:n], refs[n:2 * n], refs[2 * n], refs[2 * n + 1], refs[2 * n + 2])
        comm.start(*r)
        comm.finish(*r)

    return pl.pallas_call(
        body, name=name, in_specs=[_ANY] * n, out_specs=[_ANY] * n, out_shape=comm.out_shapes,
        scratch_shapes=comm.scratch, compiler_params=pltpu.CompilerParams(has_side_effects=True),
    )(*comm.arrs)


def _call(body, *, name, grid, in_specs, out_specs, out_shape, scratch_shapes, args, sem, comm=None):
    in_specs, out_specs, out_shape = list(in_specs), list(out_specs), list(out_shape)
    if comm is None:
        res = pl.pallas_call(body, name=name, grid=grid, in_specs=in_specs, out_specs=out_specs, out_shape=out_shape,
                             scratch_shapes=scratch_shapes, compiler_params=_cparams(sem))(*args)
        return list(res), []
    n_in, n_out, n_scr, nc = len(in_specs), len(out_specs), len(scratch_shapes), len(comm.arrs)

    def wrapped(*refs):
        ins, refs = refs[:n_in], refs[n_in:]
        c_in, refs = refs[:nc], refs[nc:]
        outs, refs = refs[:n_out], refs[n_out:]
        c_out, refs = refs[:nc], refs[nc:]
        scr, sems = refs[:n_scr], refs[n_scr:]
        ids = [pl.program_id(a) for a in range(len(grid))]
        first = functools.reduce(jnp.logical_and, [i == 0 for i in ids])
        last = functools.reduce(jnp.logical_and, [i == g - 1 for i, g in zip(ids, grid)])

        @pl.when(first)
        def _():
            comm.start(c_in, c_out, *sems)

        body(*ins, *outs, *scr)

        @pl.when(last)
        def _():
            comm.finish(c_in, c_out, *sems)

    res = pl.pallas_call(
        wrapped, name=name, grid=grid, in_specs=in_specs + [_ANY] * nc, out_specs=out_specs + [_ANY] * nc,
        out_shape=out_shape + comm.out_shapes, scratch_shapes=list(scratch_shapes) + comm.scratch,
        compiler_params=pltpu.CompilerParams(dimension_semantics=("arbitrary",) * len(grid),
                                             vmem_limit_bytes=VMEM_LIMIT, has_side_effects=True),
    )(*args, *comm.arrs)
    return list(res[:n_out]), list(res[n_out:])


def gather_two_level(bufs, *, name=None):
    n = len(bufs)

    def parts(x_refs, out_refs, send_sems, recv_sems, local_sems):
        x, y, c = _me()
        me, sibling = (x, y, c), (x, y, 1 - c)
        chips = _other_chips(x, y)

        def copy(a, k, block, to, from_input=False):
            dst = out_refs[a].at[_pos(block)]
            return pltpu.make_async_remote_copy(src_ref=x_refs[a] if from_input else dst, dst_ref=dst,
                                                send_sem=send_sems.at[7 * a + k], recv_sem=recv_sems.at[7 * a + k],
                                                device_id=to, device_id_type=MESH_ID)

        mine = [pltpu.make_async_copy(x_refs[a], out_refs[a].at[_pos(me)], local_sems.at[a]) for a in range(n)]
        first = []
        for a in range(n):
            first.append(copy(a, 0, me, sibling, from_input=True))
            first += [copy(a, 1 + j, me, (*chip, c), from_input=True) for j, chip in enumerate(chips)]
        return copy, mine, first, me, sibling, chips, c

    def start(*refs):
        _, mine, first, *_ = parts(*refs)
        for cp in mine + first:
            cp.start()

    def finish(*refs):
        copy, mine, first, me, sibling, chips, c = parts(*refs)
        passed = []
        for j, chip in enumerate(chips):
            for a in range(n):
                copy(a, 1 + j, (*chip, c), me).wait_recv()
                passed.append(copy(a, 4 + j, (*chip, c), sibling))
                passed[-1].start()
        for a in range(n):
            copy(a, 0, sibling, me).wait_recv()
            for j, chip in enumerate(chips):
                copy(a, 4 + j, (*chip, 1 - c), me).wait_recv()
        for cp in first + passed:
            cp.wait_send()
        for cp in mine:
            cp.wait()

    out_shapes = [jax.ShapeDtypeStruct((N_DEV,) + b.shape, b.dtype) for b in bufs]
    comm = _Comm(bufs, out_shapes, 7 * n, n, start, finish)
    return _run_comm(comm, name) if name else comm


def sibling_exchange(sends, *, name=None):
    n = len(sends)

    def copies(s_refs, out_refs, send_sems, recv_sems, local_sems):
        x, y, c = _me()
        return [pltpu.make_async_remote_copy(src_ref=s_refs[a].at[1 - c], dst_ref=out_refs[a], send_sem=send_sems.at[a],
                                             recv_sem=recv_sems.at[a], device_id=(x, y, 1 - c), device_id_type=MESH_ID)
                for a in range(n)]

    def start(*refs):
        for cp in copies(*refs):
            cp.start()

    def finish(*refs):
        for cp in copies(*refs):
            cp.wait()

    out_shapes = [jax.ShapeDtypeStruct(s.shape[1:], s.dtype) for s in sends]
    comm = _Comm(sends, out_shapes, n, 0, start, finish)
    return _run_comm(comm, name) if name else comm


def _flat2(shape):
    return math.prod(shape[:-1]), shape[-1]


def _row_tile(r):
    return r if r <= 512 else _pick8(r, 256)


def _pick8(n, pref):
    t = pref
    while n % t:
        t -= 8
    return t


def chip_sum(send, got, core, *, name):
    shape = got.shape[1:]
    r, cdim = _flat2(shape)
    tr = _row_tile(r)
    send = send.reshape(2, N_CHIP, r, cdim)
    got = got.reshape(N_CHIP, r, cdim)

    def body(core_ref, s_ref, g_ref, o_ref):
        o_ref[...] = (s_ref[0].astype(F32) + g_ref[...].astype(F32)).astype(o_ref.dtype)

    blk = pl.BlockSpec((N_CHIP, tr, cdim), lambda i, core_ref: (0, i, 0))
    out = pl.pallas_call(
        body, name=name,
        grid_spec=pltpu.PrefetchScalarGridSpec(
            num_scalar_prefetch=1, grid=(r // tr,),
            in_specs=[pl.BlockSpec((1, N_CHIP, tr, cdim), lambda i, core_ref: (core_ref[0], 0, i, 0)), blk],
            out_specs=blk),
        out_shape=jax.ShapeDtypeStruct((N_CHIP, r, cdim), BF16),
        compiler_params=_cparams(("parallel",)),
    )(core, send, got)
    return out.reshape((N_CHIP,) + shape)


def chip_exchange(sums, *, name=None):
    n = len(sums)

    def copies(s_refs, out_refs, send_sems, recv_sems, local_sems):
        x, y, c = _me()
        mine = 2 * x + y
        chips = _other_chips(x, y)
        local = [pltpu.make_async_copy(s_refs[a].at[mine], out_refs[a].at[mine], local_sems.at[a]) for a in range(n)]
        sends, recvs = [], []
        for a in range(n):
            for j, (px, py) in enumerate(chips):
                sems = dict(send_sem=send_sems.at[3 * a + j], recv_sem=recv_sems.at[3 * a + j], device_id=(px, py, c),
                            device_id_type=MESH_ID)
                sends.append(pltpu.make_async_remote_copy(src_ref=s_refs[a].at[2 * px + py], dst_ref=out_refs[a].at[mine], **sems))
                recvs.append(pltpu.make_async_remote_copy(src_ref=s_refs[a].at[mine], dst_ref=out_refs[a].at[2 * px + py], **sems))
        return local, sends, recvs

    def start(*refs):
        local, sends, _ = copies(*refs)
        for cp in local + sends:
            cp.start()

    def finish(*refs):
        local, sends, recvs = copies(*refs)
        for cp in recvs:
            cp.wait_recv()
        for cp in sends:
            cp.wait_send()
        for cp in local:
            cp.wait()

    out_shapes = [jax.ShapeDtypeStruct(s.shape, s.dtype) for s in sums]
    comm = _Comm(sums, out_shapes, 3 * n, n, start, finish)
    return _run_comm(comm, name) if name else comm


def sum_adamw(parts, w, m, v, *, name):
    shape = w.shape
    R, cdim = _flat2(shape)
    n_slots = parts.shape[0]
    tr = _row_tile(R)
    parts = parts.reshape(n_slots, R, cdim)
    w, m, v = (t.reshape(1, R, cdim) for t in (w, m, v))
    c1 = 1.0 - ADAM_B1 ** ADAM_STEP
    c2 = 1.0 - ADAM_B2 ** ADAM_STEP

    def body(p_ref, w_ref, m_ref, v_ref, g_ref, d_ref, mo_ref, vo_ref):
        g = p_ref[0].astype(F32)
        for s in range(1, n_slots):
            g = g + p_ref[s].astype(F32)
        mn = ADAM_B1 * m_ref[...] + (1.0 - ADAM_B1) * g
        vn = ADAM_B2 * v_ref[...] + (1.0 - ADAM_B2) * jnp.square(g)
        g_ref[...] = g
        mo_ref[...] = mn
        vo_ref[...] = vn
        d_ref[...] = -ADAM_LR * ((mn / c1) / (jnp.sqrt(vn / c2) + ADAM_EPS) + ADAM_WD * w_ref[...])

    rs = pl.BlockSpec((None, tr, cdim), lambda i: (0, i, 0))
    sd = jax.ShapeDtypeStruct((1, R, cdim), F32)
    res = pl.pallas_call(
        body, name=name, grid=(R // tr,),
        in_specs=[pl.BlockSpec((n_slots, tr, cdim), lambda i: (0, i, 0)), rs, rs, rs],
        out_specs=(rs, rs, rs, rs), out_shape=(sd, sd, sd, sd),
        compiler_params=_cparams(("parallel",)),
    )(parts, w, m, v)
    return [t.reshape(shape) for t in res]


WEIGHTS = ("ln_mix_pre", "w_in", "b_gate", "q_norm", "w_uq", "kv_norm", "w_uk", "w_uv", "mem_norm", "w_mem_kv",
           "w_branch_out", "w_out", "ln_mix_post", "ln_mlp_pre", "w_mlp_up", "w_mlp_down", "ln_mlp_post")
BIG = dict(w_in=((D_MODEL, IN_WIDTH), 1), w_uq=((Q_LORA, MLA_HEADS * (MLA_NOPE + MLA_ROPE)), 1),
           w_uk=((KV_LORA, MLA_HEADS * MLA_NOPE), 1), w_uv=((KV_LORA, MLA_HEADS * MLA_V), 1),
           w_mem_kv=((D_MODEL, 2 * QM_W), 0), w_branch_out=((N_BRANCH, BRANCH_W, D_MODEL), 2),
           w_out=((D_MODEL, D_MODEL), 0), w_mlp_up=((D_MODEL, D_FF), 1), w_mlp_down=((D_FF, D_MODEL), 0))
SMALL = tuple(n for n in WEIGHTS if n not in BIG)


def _shard_shape(name):
    full, ax = BIG[name]
    return tuple(s // N_DEV if a == ax else s for a, s in enumerate(full))


PACK_TILE = 512


def _pad_rows(a):
    r = a.shape[-2]
    to = PACK_TILE if r > PACK_TILE else 8
    pad = [(0, 0)] * a.ndim
    pad[-2] = (0, (-r) % to)
    return jnp.pad(a, pad)


def _pack_rows(arrs):
    return _pad_rows(jnp.concatenate([a.reshape(-1, LANES) for a in arrs], axis=0))


def _unpack_rows(packed, shapes, lead=()):
    out, r = [], 0
    for shp in shapes:
        n = math.prod(shp) // LANES
        out.append(packed[..., r:r + n, :].reshape(lead + tuple(shp)))
        r += n
    return out


def _slots_to_full(name, slots):
    full, ax = BIG[name]
    return jnp.moveaxis(slots, 0, ax).reshape(full)


def _full_to_slots(name, w):
    full, ax = BIG[name]
    split = full[:ax] + (N_DEV, full[ax] // N_DEV) + full[ax + 1:]
    return jnp.moveaxis(w.reshape(split), ax, 0)


def _full_to_owner(name, w):
    s = _full_to_slots(name, w)
    return jnp.swapaxes(s.reshape((N_CHIP, 2) + s.shape[1:]), 0, 1)


def _split_in(w):
    offs = np.cumsum((0,) + IN_SIZES)
    names = ("cq", "ckv", "kr", "sb", "qm", "gate")
    return {n: w[:, offs[i]:offs[i + 1]] for i, n in enumerate(names)}


def _pad_in(w):
    p = _split_in(w)
    p["kr"] = jnp.pad(p["kr"], ((0, 0), (0, KR_PAD - MLA_ROPE)))
    p["zpad"] = jnp.zeros((w.shape[0], PAD_SIZES["zpad"]), w.dtype)
    return jnp.concatenate([p[n] for n in PAD_ORDER], axis=1)


def _unpad_in(wp):
    p = {n: wp[:, PAD_OFF[n]:PAD_OFF[n] + PAD_SIZES[n]] for n in PAD_ORDER}
    p["kr"] = p["kr"][:, :MLA_ROPE]
    return jnp.concatenate([p[n] for n in ("cq", "ckv", "kr", "sb", "qm", "gate")], axis=1)


def _in_segments():
    orig = dict(zip(("cq", "ckv", "kr", "sb", "qm", "gate"), zip(np.cumsum((0,) + IN_SIZES[:-1]).tolist(), IN_SIZES)))
    shard = IN_WIDTH // N_DEV
    segs = []
    for n in PAD_ORDER:
        if n not in orig:
            continue
        o0, w = orig[n]
        c = o0
        while c < o0 + w:
            s = c // shard
            e = min(o0 + w, (s + 1) * shard)
            segs.append((s, c - s * shard, PAD_OFF[n] + (c - o0), e - c))
            c = e
    return segs


def in_slots_to_pad(slots, *, name):
    _, R, shard = slots.shape
    tr = _pick8(R, ROW_TILE)
    segs = _in_segments()

    def body(in_ref, out_ref):
        out_ref[...] = jnp.zeros_like(out_ref)
        for s, a, d, w in segs:
            out_ref[:, d:d + w] = in_ref[s, :, a:a + w]

    return pl.pallas_call(
        body, name=name, grid=(R // tr,), in_specs=[pl.BlockSpec((N_DEV, tr, shard), lambda i: (0, i, 0))],
        out_specs=pl.BlockSpec((tr, IN_PAD), lambda i: (i, 0)), out_shape=jax.ShapeDtypeStruct((R, IN_PAD), slots.dtype),
        compiler_params=_cparams(("parallel",)),
    )(slots)


def in_pad_to_owner(g, *, name):
    R = g.shape[0]
    shard = IN_WIDTH // N_DEV
    tr = _pick8(R, ROW_TILE)
    segs = _in_segments()

    def body(in_ref, out_ref):
        for s, a, d, w in segs:
            out_ref[s % 2, s // 2, :, a:a + w] = in_ref[:, d:d + w]

    return pl.pallas_call(
        body, name=name, grid=(R // tr,), in_specs=[pl.BlockSpec((tr, IN_PAD), lambda i: (i, 0))],
        out_specs=pl.BlockSpec((2, N_CHIP, tr, shard), lambda i: (0, 0, i, 0)),
        out_shape=jax.ShapeDtypeStruct((2, N_CHIP, R, shard), g.dtype),
        compiler_params=_cparams(("parallel",)),
    )(g)


def _pad_heads(w, width):
    r = w.shape[0]
    return jnp.pad(w.reshape(r, MLA_HEADS, width), ((0, 0), (0, 0), (0, HEAD_PAD - width))).reshape(r, MLA_PAD_W)


def _unpad_heads(wp, width):
    r = wp.shape[0]
    return wp.reshape(r, MLA_HEADS, HEAD_PAD)[:, :, :width].reshape(r, MLA_HEADS * width)


def _rope_inv_freq():
    half = MLA_ROPE // 2
    inv = 1.0 / (ROPE_THETA ** (jnp.arange(half, dtype=F32) * (2.0 / MLA_ROPE)))
    tab = jnp.zeros((LANES,), F32)
    tab = tab.at[MLA_NOPE:MLA_NOPE + half].set(inv).at[MLA_NOPE + half:MLA_NOPE + MLA_ROPE].set(inv)
    return tab.reshape(1, LANES)


def _as_tuple(r):
    return r if isinstance(r, tuple) else (r,)


class _Exchange:
    def __init__(self, w):
        self.w = w
        self.rest = tuple(n for n in BIG if n != "w_in")
        self.pending = {}
        self.reduced = {}

    def first_comm(self):
        return gather_two_level([self.w["w_in"].astype(BF16)])

    def first_weights(self, slots):
        return in_slots_to_pad(slots[0], name="w_in_layout")

    def rest_comm(self):
        return gather_two_level([self.w[n].astype(BF16) for n in self.rest])

    def rest_weights(self, slots):
        return {n: s if n == "w_mlp_up" else _slots_to_full(n, s) for n, s in zip(self.rest, slots)}

    def sibling_comm(self, tag, grads):
        self.pending[tag] = (tuple(grads), [grads[n] for n in grads])
        return sibling_exchange(self.pending[tag][1])

    def chip_comm(self, tag, gots):
        names, sends = self.pending[tag]
        core = lax.axis_index("c").astype(jnp.int32).reshape(1)
        sums = [chip_sum(s, g, core, name=f"chip_sum_{n}") for n, s, g in zip(names, sends, gots)]
        self.pending[tag] = names
        return chip_exchange(sums)

    def reduce_start(self, tag, grads):
        return self.chip_comm(tag, _run_comm(self.sibling_comm(tag, grads), f"sibling_{tag}"))

    def reduce_done(self, tag, recvs):
        self.reduced.update(zip(self.pending.pop(tag), recvs))


def _local_step(x, mem, positions, tgt, sm, ex):
    B, S, D = x.shape
    M = mem.shape[1]
    T = B * S
    x2 = x.reshape(T, D)
    mem2 = mem.reshape(B * M, D)
    pos = positions.reshape(T, 1).astype(F32)
    invf = _rope_inv_freq()
    cq_col, ckv_col = PAD_OFF["cq"] // Q_LORA, PAD_OFF["ckv"] // KV_LORA
    sb_col, qm_col = PAD_OFF["sb"] // LANES, PAD_OFF["qm"] // LANES
    sb_blk = SB_HEADS * SB_DIM // LANES
    mla = dict(mode="softmax", causal=True, hp=2, qk_w=128, v_w=64, scale=(MLA_NOPE + MLA_ROPE) ** -0.5,
               nh=MLA_HEADS // 2, qc=0, kc=0, vc=0)
    sbk = dict(mode="sb", causal=True, hp=2, qk_w=64, v_w=64, scale=SB_DIM ** -0.5, nh=SB_HEADS // 2,
               qc=sb_col, kc=sb_col + sb_blk, vc=sb_col + 2 * sb_blk)
    mca = dict(mode="softmax", causal=False, hp=1, qk_w=128, v_w=128, scale=MEM_DIM ** -0.5, nh=MEM_HEADS,
               qc=qm_col, kc=0, vc=MEM_HEADS)

    h, *slots = _as_tuple(rms_fwd(x2, sm["ln_mix_pre"], name="rms_mix_pre", comm=ex.first_comm()))
    w_in_pad = ex.first_weights(slots[0] if slots else None)
    proj = matmul(h, w_in_pad, name="mm_in")
    proj3 = proj.reshape(B, S, IN_PAD)
    o_sb, st_sb, *slots = attn_fwd(proj3, proj3, proj3, name="sb_fwd", comm=ex.rest_comm(), **sbk)
    W = ex.rest_weights(slots[0] if slots else None)
    w_uq_pad = _pad_heads(W["w_uq"], MLA_NOPE + MLA_ROPE)
    w_uk_pad = _pad_heads(W["w_uk"], MLA_NOPE)
    cqn = rms_fwd(proj, sm["q_norm"], n=Q_LORA, col=cq_col, name="rms_q")
    ckvn = rms_fwd(proj, sm["kv_norm"], n=KV_LORA, col=ckv_col, name="rms_kv")
    qpre = matmul(cqn, w_uq_pad, name="mm_uq")
    kpre = matmul(ckvn, w_uk_pad, name="mm_uk")
    v_mla = matmul(ckvn, W["w_uv"], out_dtype=BF16, name="mm_uv").reshape(B, S, -1)
    q_pad, k_pad = rope_fwd(qpre, kpre, proj, pos, invf, name="rope_fwd")
    q_pad, k_pad = q_pad.reshape(B, S, -1), k_pad.reshape(B, S, -1)
    o_mla, st_mla = attn_fwd(q_pad, k_pad, v_mla, name="mla_fwd", **mla)
    memh = rms_fwd(mem2, sm["mem_norm"], name="rms_mem")
    mkv = matmul(memh, W["w_mem_kv"], out_dtype=BF16, name="mm_memkv").reshape(B, M, -1)
    o_mem, st_mem = attn_fwd(proj3, mkv, mkv, name="mem_fwd", **mca)
    outs = [o.reshape(T, BRANCH_W) for o in (o_mla, o_sb, o_mem)]
    ps = [matmul(o, W["w_branch_out"][b], out_dtype=BF16, name=f"mm_bo{b}") for b, o in enumerate(outs)]
    merged = gate_merge(proj, sm["b_gate"], ps, name="gate_merge")
    y = matmul(merged, W["w_out"], name="mm_out")
    x1, h2 = mix_post_fwd(x2, y, sm["ln_mix_post"], sm["ln_mlp_pre"], name="mix_post")
    u = matmul(h2, W["w_mlp_up"], b_slots="n", act="relu2", out_dtype=BF16, name="mm_up")
    d = matmul(u, W["w_mlp_down"], name="mm_down")
    loss_p, dx2, dd, dg_mlp_post = loss_head(x1, d, sm["ln_mlp_post"], tgt.reshape(T, D), name="loss_head")

    da = matmul(dd, W["w_mlp_down"], tb=True, act="relu2_bwd", act_in=u, out_dtype=BF16, name="mm_down_dx")
    g_down = matmul(u, dd, ta=True, out_dtype=BF16, out_slots="m", name="mm_down_dw")
    dh2 = matmul(da, _slots_to_full("w_mlp_up", W["w_mlp_up"]), tb=True, name="mm_up_dx")
    g_up = matmul(h2, da, ta=True, out_dtype=BF16, out_slots="n", name="mm_up_dw")
    sib = ex.sibling_comm("mlp", dict(w_mlp_down=g_down, w_mlp_up=g_up))
    dx1, dy, dg_mlp_pre, dg_mix_post, *got = mlp_pre_bwd(dx2, dh2, x1, sm["ln_mlp_pre"], y, sm["ln_mix_post"],
                                                          name="mlp_pre_bwd", comm=sib)
    red_mlp = ex.chip_comm("mlp", got[0] if got else None)
    dmerged = matmul(dy, W["w_out"], tb=True, name="mm_out_dx")
    g_out = matmul(merged, dy, ta=True, out_dtype=BF16, name="mm_out_dw")
    dlog, dp0, dp1, dp2, db_gate = gate_bwd(dmerged, proj, sm["b_gate"], ps, name="gate_bwd")
    dps = (dp0, dp1, dp2)
    g_bo = jnp.stack([matmul(outs[b], dps[b], ta=True, out_dtype=BF16, name=f"mm_bo{b}_dw") for b in range(N_BRANCH)])
    sib = ex.sibling_comm("mix", dict(w_out=_full_to_owner("w_out", g_out),
                                      w_branch_out=_full_to_owner("w_branch_out", g_bo)))
    dos = [matmul(dps[b], W["w_branch_out"][b], tb=True, name=f"mm_bo{b}_dx") for b in range(N_BRANCH - 1)]
    last_do, *got = _as_tuple(matmul(dps[-1], W["w_branch_out"][N_BRANCH - 1], tb=True, comm=sib,
                                     name=f"mm_bo{N_BRANCH - 1}_dx"))
    red_mix = ex.chip_comm("mix", got[0] if got else None)
    dos = [t.reshape(B, S, BRANCH_W) for t in dos + [last_do]]
    dq_pad, dk_pad, dv_mla, *got = attn_bwd(q_pad, k_pad, v_mla, o_mla, st_mla, dos[0], name="mla_bwd", comm=red_mlp, **mla)
    ex.reduce_done("mlp", got[0] if got else None)
    dsq, dsk, dsv, *got = attn_bwd(proj3, proj3, proj3, o_sb, st_sb, dos[1], name="sb_bwd", comm=red_mix, **sbk)
    ex.reduce_done("mix", got[0] if got else None)
    dqm, dmk, dmv = attn_bwd(proj3, mkv, mkv, o_mem, st_mem, dos[2], name="mem_bwd", **mca)
    dmkv = jnp.concatenate([dmk, dmv], axis=-1).astype(BF16).reshape(B * M, -1)
    dmemh = matmul(dmkv, W["w_mem_kv"], tb=True, name="mm_memkv_dx")
    g_memkv = matmul(memh, dmkv, ta=True, out_dtype=BF16, name="mm_memkv_dw")
    _, dg_mem_norm = rms_bwd(dmemh, mem2, sm["mem_norm"], name="rms_mem_bwd")
    dq_pad, dk_pad, dv_mla = dq_pad.reshape(T, -1), dk_pad.reshape(T, -1), dv_mla.reshape(T, -1)
    dqpre, dkr = rope_bwd(dq_pad, dk_pad, pos, invf, name="rope_bwd")
    dcqn = matmul(dqpre, w_uq_pad, tb=True, name="mm_uq_dx")
    g_uq = _unpad_heads(matmul(cqn, dqpre, ta=True, out_dtype=BF16, name="mm_uq_dw"), MLA_NOPE + MLA_ROPE)
    dckvn = matmul(dk_pad, w_uk_pad, tb=True, name="mm_uk_dx")
    dckvn = matmul(dv_mla, W["w_uv"], tb=True, acc=dckvn, name="mm_uv_dx")
    g_uk = _unpad_heads(matmul(ckvn, dk_pad, ta=True, out_dtype=BF16, name="mm_uk_dw"), MLA_NOPE)
    g_uv = matmul(ckvn, dv_mla, ta=True, out_dtype=BF16, name="mm_uv_dw")
    dcq, dg_q_norm = rms_bwd(dcqn, proj, sm["q_norm"], n=Q_LORA, col=cq_col, out_dtype=BF16, name="rms_q_bwd")
    dckv, dg_kv_norm = rms_bwd(dckvn, proj, sm["kv_norm"], n=KV_LORA, col=ckv_col, out_dtype=BF16, name="rms_kv_bwd")
    pieces = dict(gate=dlog, sb=jnp.concatenate([dsq, dsk, dsv], axis=-1).reshape(T, -1), qm=dqm.reshape(T, -1),
                  ckv=dckv, cq=dcq, kr=dkr, zpad=jnp.zeros((T, PAD_SIZES["zpad"]), BF16))
    dproj = jnp.concatenate([pieces[n].astype(BF16) for n in PAD_ORDER], axis=1)
    g_in = in_pad_to_owner(matmul(h, dproj, ta=True, out_dtype=BF16, name="mm_in_dw"), name="g_in_layout")
    last = dict(w_uq=g_uq, w_uk=g_uk, w_uv=g_uv, w_mem_kv=g_memkv)
    red_in = ex.reduce_start("in", dict(w_in=g_in, **{n: _full_to_owner(n, g) for n, g in last.items()}))
    dh, *got = _as_tuple(matmul(dproj, w_in_pad, tb=True, comm=red_in, name="mm_in_dx"))
    ex.reduce_done("in", got[0] if got else None)
    dx, dg_mix_pre = rms_bwd(dh, x2, sm["ln_mix_pre"], residual=dx1, name="rms_mix_pre_bwd")

    small = dict(ln_mix_pre=dg_mix_pre, b_gate=db_gate, q_norm=dg_q_norm, kv_norm=dg_kv_norm, mem_norm=dg_mem_norm,
                 ln_mix_post=dg_mix_post, ln_mlp_pre=dg_mlp_pre, ln_mlp_post=dg_mlp_post)
    return loss_p, dx.reshape(B, S, D), small


def kernel(x, mem, positions, ln_mix_pre, w_in, b_gate, q_norm, w_uq, kv_norm, w_uk, w_uv, mem_norm, w_mem_kv, w_branch_out, w_out, ln_mix_post, ln_mlp_pre, w_mlp_up, w_mlp_down, ln_mlp_post, loss_target, m_ln_mix_pre, m_w_in, m_b_gate, m_q_norm, m_w_uq, m_kv_norm, m_w_uk, m_w_uv, m_mem_norm, m_w_mem_kv, m_w_branch_out, m_w_out, m_ln_mix_post, m_ln_mlp_pre, m_w_mlp_up, m_w_mlp_down, m_ln_mlp_post, v_ln_mix_pre, v_w_in, v_b_gate, v_q_norm, v_w_uq, v_kv_norm, v_w_uk, v_w_uv, v_mem_norm, v_w_mem_kv, v_w_branch_out, v_w_out, v_ln_mix_post, v_ln_mlp_pre, v_w_mlp_up, v_w_mlp_down, v_ln_mlp_post):
    given = dict(locals())
    w = {n: given[n][0] for n in WEIGHTS}
    m = {n: given["m_" + n][0] for n in WEIGHTS}
    v = {n: given["v_" + n][0] for n in WEIGHTS}
    sm = {n: w[n].reshape(1, -1) for n in SMALL}
    ex = _Exchange(w)
    loss_row, grad_x, g_small = _local_step(x, mem, positions, loss_target, sm, ex)

    res = {n: sum_adamw(ex.reduced[n], given[n], given["m_" + n], given["v_" + n], name=f"adamw_{n}") for n in BIG}

    small_shapes = [w[n].shape for n in SMALL] + [(LANES,)]
    parts = gather_two_level([_pack_rows([g_small[n] for n in SMALL] + [loss_row])], name="gather_small_grads")[0]
    no_row = jnp.zeros((1, LANES), F32)
    res_small = sum_adamw(parts, *[_pack_rows([t[n] for n in SMALL] + [no_row]) for t in (w, m, v)], name="adamw_replicated")
    res_small = [_unpack_rows(r, small_shapes) for r in res_small]
    for i, n in enumerate(SMALL):
        res[n] = [r[i][None] for r in res_small]
    loss = res_small[0][len(SMALL)][0]

    out = [loss, grad_x]
    for k in range(4):
        out += [res[n][k] for n in WEIGHTS]
    return tuple(out)
```

```python
import functools
import math

import numpy as np
import jax
import jax.numpy as jnp
from jax import lax
from jax.experimental import pallas as pl
from jax.experimental.pallas import tpu as pltpu

F32 = jnp.float32
BF16 = jnp.bfloat16

D_MODEL = 1024
MEM_HEADS, MEM_DIM = 4, 128
MLA_HEADS, MLA_NOPE, MLA_ROPE, MLA_V = 8, 64, 32, 64
Q_LORA, KV_LORA = 384, 256
ROPE_THETA = 10000.0
SB_HEADS, SB_DIM = 8, 64
D_FF = 4 * D_MODEL
N_BRANCH, BRANCH_W = 3, 512
EPS = 1e-6
SB_W = 3 * SB_HEADS * SB_DIM
QM_W = MEM_HEADS * MEM_DIM
GATE_W = N_BRANCH * D_MODEL
IN_SIZES = (Q_LORA, KV_LORA, MLA_ROPE, SB_W, QM_W, GATE_W)
IN_WIDTH = sum(IN_SIZES)
ADAM_LR, ADAM_B1, ADAM_B2, ADAM_EPS, ADAM_WD, ADAM_STEP = 0.001, 0.9, 0.999, 1e-08, 0.01, 10

N_DEV = 8
LANES = 128
KR_PAD = LANES
PAD_ORDER = ("gate", "cq", "kr", "ckv", "sb", "qm", "zpad")
MM_TILE = 1024
MM_ROWS = 2048
MM_DEPTH = 4096
MM_VMEM = 40 * 1024 * 1024
PAD_SIZES = dict(gate=GATE_W, cq=Q_LORA, kr=KR_PAD, ckv=KV_LORA, sb=SB_W, qm=QM_W)
PAD_SIZES["zpad"] = (-sum(PAD_SIZES.values())) % MM_TILE
PAD_OFF = {}
_o = 0
for _n in PAD_ORDER:
    PAD_OFF[_n] = _o
    _o += PAD_SIZES[_n]
IN_PAD = _o
HEAD_PAD = LANES
MLA_PAD_W = MLA_HEADS * HEAD_PAD
VMEM_LIMIT = 48 * 1024 * 1024


def _cparams(sem):
    return pltpu.CompilerParams(dimension_semantics=sem, vmem_limit_bytes=VMEM_LIMIT)


def _pick(n, pref):
    if n <= pref:
        return n
    t = (pref // LANES) * LANES
    while t >= LANES:
        if n % t == 0:
            return t
        t -= LANES
    return n


def matmul(a, b, *, name, ta=False, tb=False, out_dtype=F32, acc=None, act=None, act_in=None,
           b_slots=None, out_slots=None, rows=None, comm=None, tm_pref=MM_ROWS, tn_pref=MM_TILE, tk_pref=MM_DEPTH):
    M, K = (a.shape[1], a.shape[0]) if ta else a.shape
    tm, tk = _pick(M, tm_pref), _pick(K, tk_pref)
    if b_slots == "n":
        assert not tb and b.shape[:2] == (N_DEV, K)
        tn = b.shape[2]
        N = N_DEV * tn
    elif b_slots == "k":
        assert tb and N_DEV * b.shape[2] == K
        N, tk = b.shape[1], b.shape[2]
        tn = _pick(N, tn_pref)
    else:
        N = b.shape[0] if tb else b.shape[1]
        assert (b.shape[1] if tb else b.shape[0]) == K
        tn = _pick(N, tn_pref)
    if out_slots == "m":
        tm = M // N_DEV
    elif out_slots == "n":
        tn = N // N_DEV
    assert out_slots is None or (acc is None and act_in is None)
    if rows is not None:
        assert acc is None and act is None and out_slots is None
        tn = N
        row_ins, vec_ins = list(rows["row_ins"]), list(rows["vec_ins"])
        row_outs, vec_outs = list(rows["row_outs"]), list(rows["vec_outs"])

    def block_bytes(tm_, tk_):
        ab = tm_ * tk_ * a.dtype.itemsize + tk_ * tn * b.dtype.itemsize
        per_out = jnp.dtype(out_dtype).itemsize + sum(t.dtype.itemsize for t in (acc, act_in) if t is not None)
        if rows is not None:
            per_out = sum(t.dtype.itemsize for t in row_ins) + sum(jnp.dtype(d).itemsize for d in row_outs)
        return 2 * (ab + tm_ * tn * per_out) + (tm_ * tn * 4 if K // tk_ > 1 else 0)

    while block_bytes(tm, tk) > MM_VMEM:
        if tm > 512 and out_slots != "m" and M % (tm // 2) == 0:
            tm //= 2
        elif tk > 512 and b_slots != "k" and K % (tk // 2) == 0:
            tk //= 2
        else:
            break
    nk = K // tk
    dims = (((0 if ta else 1,), (1 if tb else 0,)), ((), ()))
    assert act in (None, "relu2", "relu2_bwd") and (act == "relu2_bwd") == (act_in is not None)

    def body(*refs):
        a_ref, b_ref = refs[0], refs[1]
        pos = 2
        acc_ref = act_ref = None
        if acc is not None:
            acc_ref = refs[pos]
            pos += 1
        if act_in is not None:
            act_ref = refs[pos]
            pos += 1
        if rows is not None:
            n_ri, n_vi, n_ro, n_vo = len(row_ins), len(vec_ins), len(row_outs), len(vec_outs)
            ri_refs, vi_refs = refs[pos:pos + n_ri], refs[pos + n_ri:pos + n_ri + n_vi]
            pos += n_ri + n_vi
            ro_refs, vo_refs = refs[pos:pos + n_ro], refs[pos + n_ro:pos + n_ro + n_vo]
            pos += n_ro + n_vo - 1
            row_i = pl.program_id(0)
        out = refs[pos]
        scratch = refs[pos + 1:]

        part = lax.dot_general(a_ref[...].astype(BF16), b_ref[...].astype(BF16), dims,
                               preferred_element_type=F32)

        def finish(r):
            if rows is not None:
                ro, vo = rows["fn"](r, *[t[...] for t in ri_refs], *[t[...] for t in vi_refs])
                for ref, val in zip(ro_refs, ro):
                    ref[...] = val.astype(ref.dtype)
                for ref, val in zip(vo_refs, vo):
                    _acc_rows(ref, row_i, val)
                return
            if acc_ref is not None:
                r = r + acc_ref[...]
            if act == "relu2":
                r = jnp.square(jnp.maximum(r, 0.0))
            elif act == "relu2_bwd":
                r = r * (2.0 * jnp.sqrt(act_ref[...].astype(F32)))
            out[...] = r.astype(out.dtype)

        if nk == 1:
            finish(part)
        else:
            acc_sc = scratch[0]
            k = pl.program_id(2)

            @pl.when(k == 0)
            def _():
                acc_sc[...] = part

            @pl.when(k > 0)
            def _():
                acc_sc[...] += part

            @pl.when(k == nk - 1)
            def _():
                finish(acc_sc[...])

    a_spec = pl.BlockSpec((tk, tm), lambda i, j, k: (k, i)) if ta else pl.BlockSpec((tm, tk), lambda i, j, k: (i, k))
    if b_slots == "n":
        b_spec = pl.BlockSpec((None, tk, tn), lambda i, j, k: (j, k, 0))
    elif b_slots == "k":
        b_spec = pl.BlockSpec((None, tn, tk), lambda i, j, k: (k, j, 0))
    else:
        b_spec = pl.BlockSpec((tn, tk), lambda i, j, k: (j, k)) if tb else pl.BlockSpec((tk, tn), lambda i, j, k: (k, j))
    if out_slots == "m":
        o_spec = pl.BlockSpec((None, None, tm, tn), lambda i, j, k: (i % 2, i // 2, 0, j))
        o_shape = (2, N_CHIP, tm, N)
    elif out_slots == "n":
        o_spec = pl.BlockSpec((None, None, tm, tn), lambda i, j, k: (j % 2, j // 2, i, 0))
        o_shape = (2, N_CHIP, M, tn)
    else:
        o_spec = pl.BlockSpec((tm, tn), lambda i, j, k: (i, j))
        o_shape = (M, N)
    in_specs = [a_spec, b_spec]
    args = [a, b]
    if acc is not None:
        in_specs.append(o_spec)
        args.append(acc)
    if act_in is not None:
        in_specs.append(o_spec)
        args.append(act_in)
    out_specs, out_shapes, sem = [o_spec], [jax.ShapeDtypeStruct(o_shape, out_dtype)], ("parallel", "parallel", "arbitrary")
    if rows is not None:
        def vec_spec(w):
            return pl.BlockSpec((1, w), lambda i, j, k: (0, 0))

        in_specs += [o_spec] * len(row_ins) + [vec_spec(t.shape[1]) for t in vec_ins]
        args += row_ins + vec_ins
        out_specs = [o_spec] * len(row_outs) + [vec_spec(w) for w in vec_outs]
        out_shapes = ([jax.ShapeDtypeStruct((M, N), d) for d in row_outs]
                      + [jax.ShapeDtypeStruct((1, w), F32) for w in vec_outs])
        sem = ("arbitrary",) * 3
    outs, c_outs = _call(
        body, name=name, grid=(M // tm, N // tn, nk), in_specs=in_specs, out_specs=out_specs, out_shape=out_shapes,
        scratch_shapes=[pltpu.VMEM((tm, tn), F32)] if nk > 1 else [], args=args, sem=sem, comm=comm)
    if rows is not None:
        return (*outs, c_outs) if comm is not None else tuple(outs)
    return (outs[0], c_outs) if comm is not None else outs[0]


ROW_TILE = 256


def _rstd(xv):
    return lax.rsqrt(jnp.mean(xv * xv, axis=-1, keepdims=True) + EPS)


def _rms_bwd_rows(dy, xv, g):
    r = _rstd(xv)
    dyg = dy * g
    dx = r * dyg - xv * (r * r * r) * jnp.mean(dyg * xv, axis=-1, keepdims=True)
    return dx, dy * xv * r


def _row_spec(tm, n, col=0):
    return pl.BlockSpec((tm, n), lambda i: (i, col))


def _vec_spec(n):
    return pl.BlockSpec((1, n), lambda i: (0, 0))


def _acc_rows(ref, i, val):
    @pl.when(i == 0)
    def _():
        ref[...] = val

    @pl.when(i > 0)
    def _():
        ref[...] += val


def rms_fwd(x, g, *, name, n=None, col=0, out_dtype=BF16, comm=None):
    T = x.shape[0]
    n = x.shape[1] if n is None else n
    tm = _pick(T, ROW_TILE)

    def body(x_ref, g_ref, o_ref):
        xv = x_ref[...]
        o_ref[...] = (xv * _rstd(xv) * g_ref[...]).astype(o_ref.dtype)

    outs, c_outs = _call(
        body, name=name, grid=(T // tm,), in_specs=[_row_spec(tm, n, col), _vec_spec(n)],
        out_specs=[_row_spec(tm, n)], out_shape=[jax.ShapeDtypeStruct((T, n), out_dtype)],
        scratch_shapes=[], args=(x, g), sem=("parallel",), comm=comm)
    return (outs[0], c_outs) if comm is not None else outs[0]


def rms_bwd(dy, x, g, *, name, n=None, col=0, residual=None, out_dtype=F32):
    T = x.shape[0]
    n = x.shape[1] if n is None else n
    tm = _pick(T, ROW_TILE)

    def body(*refs):
        if residual is None:
            dy_ref, x_ref, g_ref, dx_ref, dg_ref = refs
        else:
            dy_ref, x_ref, g_ref, res_ref, dx_ref, dg_ref = refs
        dx, dgr = _rms_bwd_rows(dy_ref[...].astype(F32), x_ref[...], g_ref[...])
        if residual is not None:
            dx = dx + res_ref[...]
        dx_ref[...] = dx.astype(dx_ref.dtype)
        _acc_rows(dg_ref, pl.program_id(0), jnp.sum(dgr, axis=0, keepdims=True))

    in_specs = [_row_spec(tm, n), _row_spec(tm, n, col), _vec_spec(n)]
    args = [dy, x, g]
    if residual is not None:
        in_specs.append(_row_spec(tm, n))
        args.append(residual)
    return pl.pallas_call(
        body, name=name, grid=(T // tm,), in_specs=in_specs,
        out_specs=(_row_spec(tm, n), _vec_spec(n)),
        out_shape=(jax.ShapeDtypeStruct((T, n), out_dtype), jax.ShapeDtypeStruct((1, n), F32)),
        compiler_params=_cparams(("arbitrary",)),
    )(*args)


def _mix_post_rows(y, x, g_post, g_pre2):
    x1 = x + y * _rstd(y) * g_post
    return [y, x1, x1 * _rstd(x1) * g_pre2], []


def _loss_rows(d, x1, tgt, g):
    err = x1 + d * _rstd(d) * g - tgt
    part = 0.5 * jnp.sum(jnp.mean(err * err, axis=-1, keepdims=True), axis=0, keepdims=True)
    dx2 = err * (1.0 / d.shape[-1])
    dd, dgr = _rms_bwd_rows(dx2, d, g)
    return [dx2, dd], [jnp.broadcast_to(part, (1, LANES)), jnp.sum(dgr, axis=0, keepdims=True)]


def _mlp_pre_bwd_rows(dh2, dx2, x1, y, g_pre2, g_post):
    d1, dg2 = _rms_bwd_rows(dh2, x1, g_pre2)
    dx1 = dx2 + d1
    dy, dgp = _rms_bwd_rows(dx1, y, g_post)
    return [dx1, dy], [jnp.sum(dg2, axis=0, keepdims=True), jnp.sum(dgp, axis=0, keepdims=True)]


def _mix_pre_bwd_rows(dh, x, dx1, g):
    dx, dgr = _rms_bwd_rows(dh, x, g)
    return [dx + dx1], [jnp.sum(dgr, axis=0, keepdims=True)]


def _gate_specs(tm):
    gcol = PAD_OFF["gate"] // D_MODEL
    lspecs = [pl.BlockSpec((tm, D_MODEL), functools.partial(lambda i, c: (i, c), c=gcol + b)) for b in range(N_BRANCH)]
    bspecs = [pl.BlockSpec((1, D_MODEL), functools.partial(lambda i, c: (0, c), c=b)) for b in range(N_BRANCH)]
    pspecs = [_row_spec(tm, D_MODEL) for _ in range(N_BRANCH)]
    return lspecs, bspecs, pspecs


def gate_merge(proj, b_gate, ps, *, name):
    T = proj.shape[0]
    tm = _pick(T, ROW_TILE)
    lspecs, bspecs, pspecs = _gate_specs(tm)

    def body(*refs):
        l_refs, b_refs, p_refs, o_ref = refs[0:3], refs[3:6], refs[6:9], refs[9]
        tot = None
        for lr, br, pr in zip(l_refs, b_refs, p_refs):
            term = jax.nn.sigmoid(lr[...] + br[...]) * pr[...].astype(F32)
            tot = term if tot is None else tot + term
        o_ref[...] = tot.astype(o_ref.dtype)

    return pl.pallas_call(
        body, name=name, grid=(T // tm,), in_specs=lspecs + bspecs + pspecs,
        out_specs=_row_spec(tm, D_MODEL), out_shape=jax.ShapeDtypeStruct((T, D_MODEL), BF16),
        compiler_params=_cparams(("parallel",)),
    )(proj, proj, proj, b_gate, b_gate, b_gate, *ps)


def gate_bwd(dmerged, proj, b_gate, ps, *, name):
    T = proj.shape[0]
    tm = _pick(T, ROW_TILE)
    lspecs, bspecs, pspecs = _gate_specs(tm)

    def body(*refs):
        dm_ref = refs[0]
        l_refs, b_refs, p_refs = refs[1:4], refs[4:7], refs[7:10]
        dl_ref, dp_refs, db_ref = refs[10], refs[11:14], refs[14]
        i = pl.program_id(0)
        dm = dm_ref[...]
        for b, (lr, br, pr, dpr) in enumerate(zip(l_refs, b_refs, p_refs, dp_refs)):
            gt = jax.nn.sigmoid(lr[...] + br[...])
            dpr[...] = (dm * gt).astype(dpr.dtype)
            dl = dm * pr[...].astype(F32) * gt * (1.0 - gt)
            dl_ref[:, b * D_MODEL:(b + 1) * D_MODEL] = dl.astype(dl_ref.dtype)
            part = jnp.sum(dl, axis=0, keepdims=True)

            @pl.when(i == 0)
            def _():
                db_ref[:, b * D_MODEL:(b + 1) * D_MODEL] = part

            @pl.when(i > 0)
            def _():
                db_ref[:, b * D_MODEL:(b + 1) * D_MODEL] += part

    return pl.pallas_call(
        body, name=name, grid=(T // tm,), in_specs=[_row_spec(tm, D_MODEL)] + lspecs + bspecs + pspecs,
        out_specs=(_row_spec(tm, GATE_W), *[_row_spec(tm, D_MODEL) for _ in range(N_BRANCH)], _vec_spec(GATE_W)),
        out_shape=(jax.ShapeDtypeStruct((T, GATE_W), BF16),
                   *[jax.ShapeDtypeStruct((T, D_MODEL), BF16) for _ in range(N_BRANCH)],
                   jax.ShapeDtypeStruct((1, GATE_W), F32)),
        compiler_params=_cparams(("arbitrary",)),
    )(dmerged, proj, proj, proj, b_gate, b_gate, b_gate, *ps)


def _rope_tables(pos_ref, invf_ref):
    ang = pos_ref[...] * invf_ref[...]
    lane = lax.broadcasted_iota(jnp.int32, (1, LANES), 1)
    s = jnp.sin(ang)
    split = MLA_NOPE + MLA_ROPE // 2
    return jnp.cos(ang), jnp.where(lane >= split, s, 0.0), jnp.where(lane < split, s, 0.0)


def _rotate(xh, c, s_hi, s_lo, sign):
    half = MLA_ROPE // 2
    up = pltpu.roll(xh, half, 1)
    down = pltpu.roll(xh, LANES - half, 1)
    return xh * c + sign * (up * s_hi - down * s_lo)


def rope_fwd(qpre, kpre, proj, pos, invf, *, name):
    T = qpre.shape[0]
    tm = _pick(T, ROW_TILE)
    kr_col = PAD_OFF["kr"] // KR_PAD

    def body(q_ref, k_ref, kr_ref, pos_ref, invf_ref, qo_ref, ko_ref):
        c, s_hi, s_lo = _rope_tables(pos_ref, invf_ref)
        kr = _rotate(pltpu.roll(kr_ref[...], MLA_NOPE, 1), c, s_hi, s_lo, 1.0)
        for h in range(MLA_HEADS):
            sl = slice(h * HEAD_PAD, (h + 1) * HEAD_PAD)
            qo_ref[:, sl] = _rotate(q_ref[:, sl], c, s_hi, s_lo, 1.0).astype(qo_ref.dtype)
            ko_ref[:, sl] = (k_ref[:, sl] + kr).astype(ko_ref.dtype)

    rs = _row_spec(tm, MLA_PAD_W)
    return pl.pallas_call(
        body, name=name, grid=(T // tm,),
        in_specs=[rs, rs, _row_spec(tm, KR_PAD, kr_col), _row_spec(tm, 1), _vec_spec(LANES)],
        out_specs=(rs, rs),
        out_shape=(jax.ShapeDtypeStruct((T, MLA_PAD_W), BF16), jax.ShapeDtypeStruct((T, MLA_PAD_W), BF16)),
        compiler_params=_cparams(("parallel",)),
    )(qpre, kpre, proj, pos, invf)


def rope_bwd(dq_pad, dk_pad, pos, invf, *, name):
    T = dq_pad.shape[0]
    tm = _pick(T, ROW_TILE)

    def body(dq_ref, dk_ref, pos_ref, invf_ref, dqo_ref, dkr_ref):
        c, s_hi, s_lo = _rope_tables(pos_ref, invf_ref)
        tot = None
        for h in range(MLA_HEADS):
            sl = slice(h * HEAD_PAD, (h + 1) * HEAD_PAD)
            dqo_ref[:, sl] = _rotate(dq_ref[:, sl], c, s_hi, s_lo, -1.0).astype(dqo_ref.dtype)
            tot = dk_ref[:, sl] if tot is None else tot + dk_ref[:, sl]
        dkr = pltpu.roll(_rotate(tot, c, s_hi, s_lo, -1.0), LANES - MLA_NOPE, 1)
        lane = lax.broadcasted_iota(jnp.int32, (1, LANES), 1)
        dkr_ref[...] = jnp.where(lane < MLA_ROPE, dkr, 0.0).astype(dkr_ref.dtype)

    rs = _row_spec(tm, MLA_PAD_W)
    return pl.pallas_call(
        body, name=name, grid=(T // tm,),
        in_specs=[rs, rs, _row_spec(tm, 1), _vec_spec(LANES)],
        out_specs=(rs, _row_spec(tm, KR_PAD)),
        out_shape=(jax.ShapeDtypeStruct((T, MLA_PAD_W), BF16), jax.ShapeDtypeStruct((T, KR_PAD), BF16)),
        compiler_params=_cparams(("parallel",)),
    )(dq_pad, dk_pad, pos, invf)


ATT_TILE = 512
NEG = -1e30


def _split_bf16(v):
    hi = v.astype(BF16)
    return hi, (v - hi.astype(F32)).astype(BF16)


def _mul(x, s):
    return x if s == 1.0 else x * s


def _dot(a, b, ca, cb):
    return lax.dot_general(a, b, (((ca,), (cb,)), ((), ())), preferred_element_type=F32)


CUM_CHUNK = 256


def _tri(kind, terms=2):
    r = lax.broadcasted_iota(jnp.int32, (CUM_CHUNK, CUM_CHUNK), 0)
    c = lax.broadcasted_iota(jnp.int32, (CUM_CHUNK, CUM_CHUNK), 1)
    m = {"gt": r > c, "le": r <= c, "lt": r < c}[kind]
    t = jnp.where(m, 1.0, 0.0).astype(BF16)
    return jnp.concatenate([t] * terms, axis=0)


def _cum(v, tri, kind):
    n = v.shape[1] // CUM_CHUNK
    two = tri.shape[0] == 2 * CUM_CHUNK
    chunks = [v[:, c * CUM_CHUNK:(c + 1) * CUM_CHUNK] for c in range(n)]
    totals = [jnp.sum(ch, axis=1, keepdims=True) for ch in chunks] if n > 1 else None
    outs = []
    for c, ch in enumerate(chunks):
        r = _dot(jnp.concatenate(_split_bf16(ch), axis=1) if two else ch.astype(BF16), tri, 1, 0)
        others = [] if n == 1 else totals[c + 1:] if kind == "gt" else totals[:c]
        for t in others:
            r = r + t
        outs.append(r)
    return outs[0] if n == 1 else jnp.concatenate(outs, axis=1)


def _lane_masks(hp, w):
    lane = lax.broadcasted_iota(jnp.int32, (1, LANES), 1)
    return [(lane >= e * w) & (lane < (e + 1) * w) for e in range(hp)]


def _att_dims(mode, hp, qk_w, v_w):
    assert mode in ("softmax", "sb")
    assert (hp, qk_w, v_w) in ((2, 128, 64), (2, 64, 64), (1, 128, 128))
    qw = hp * LANES if qk_w == LANES else LANES
    return qw


def attn_fwd(q, k, v, *, name, mode, causal, hp, qk_w, v_w, scale, nh, qc, kc, vc, comm=None):
    qw = _att_dims(mode, hp, qk_w, v_w)
    B, S = q.shape[0], q.shape[1]
    Sk = k.shape[1]
    tq = _pick(S, ATT_TILE)
    tk = _pick(Sk, ATT_TILE)
    if causal:
        assert tq == tk and S == Sk
    nq, nkv = S // tq, Sk // tk
    pre_scaled = math.frexp(scale)[0] == 0.5
    post = 1.0 if pre_scaled else scale

    def body(q_ref, k_ref, v_ref, o_ref, st_ref):
        i = pl.program_id(2)
        qv = q_ref[0].astype(BF16)
        masks_qk = _lane_masks(hp, qk_w) if qk_w < LANES else None
        masks_v = _lane_masks(hp, v_w) if v_w < LANES else None
        if qk_w == LANES:
            qs = [qv[:, e * LANES:(e + 1) * LANES] for e in range(hp)]
        else:
            qs = [jnp.where(masks_qk[e], qv, jnp.zeros_like(qv)) for e in range(hp)]
        if pre_scaled:
            qs = [t * scale for t in qs]
        rows = lax.broadcasted_iota(jnp.int32, (tq, tk), 0)
        cols = lax.broadcasted_iota(jnp.int32, (tq, tk), 1)
        tri = _tri("gt") if mode == "sb" else None

        def kv_tile(j):
            off = pl.multiple_of(j * tk, tk)
            kj = k_ref[0, pl.ds(off, tk), :].astype(BF16)
            vj = v_ref[0, pl.ds(off, tk), :].astype(BF16)
            ks = [kj[:, e * LANES:(e + 1) * LANES] for e in range(hp)] if qk_w == LANES else [kj] * hp
            return ks, vj

        def merge(vals):
            if hp == 1:
                return jnp.broadcast_to(vals[0], (tq, LANES))
            return jnp.where(masks_v[0], vals[0], vals[1])

        if mode == "softmax":
            def block(j, carry, diag):
                ms, ls, acc = carry
                ks, vj = kv_tile(j)
                new_m, new_l, alphas, pvs = [], [], [], []
                for e in range(hp):
                    s = _mul(_dot(qs[e], ks[e], 1, 1), post)
                    if diag:
                        s = jnp.where(rows >= cols, s, NEG)
                    m_new = jnp.maximum(ms[e], jnp.max(s, axis=1, keepdims=True))
                    alpha = jnp.exp(ms[e] - m_new)
                    p = jnp.exp(s - m_new)
                    new_l.append(alpha * ls[e] + jnp.sum(p, axis=1, keepdims=True))
                    new_m.append(m_new)
                    alphas.append(alpha)
                    pvs.append(_dot(p.astype(BF16), vj, 1, 0))
                acc = acc * merge(alphas) + merge(pvs)
                return tuple(new_m), tuple(new_l), acc

            init = (tuple(jnp.full((tq, 1), NEG, F32) for _ in range(hp)),
                    tuple(jnp.zeros((tq, 1), F32) for _ in range(hp)),
                    jnp.zeros((tq, LANES), F32))
            if causal:
                carry = lax.fori_loop(0, i, lambda j, c: block(j, c, False), init)
                ms, ls, acc = block(i, carry, True)
            else:
                ms, ls, acc = lax.fori_loop(0, nkv, lambda j, c: block(j, c, False), init)
            o_ref[0] = acc / merge(list(ls))
            st_ref[0, 0] = merge([m + jnp.log(l) for m, l in zip(ms, ls)])
        else:
            def block(j, carry, diag):
                cs_, acc = carry
                ks, vj = kv_tile(j)
                new_c, pvs = [], []
                for e in range(hp):
                    z = _mul(_dot(qs[e], ks[e], 1, 1), post)
                    lk = -jnp.maximum(z, 0.0) - jnp.log(1.0 + jnp.exp(-jnp.abs(z)))
                    lz = lk + z
                    if diag:
                        lk = jnp.where(rows > cols, lk, 0.0)
                    a = jnp.exp(lz + _cum(lk, tri, "gt") + cs_[e])
                    if diag:
                        a = jnp.where(rows > cols, a, 0.0)
                    pvs.append(_dot(a.astype(BF16), vj, 1, 0))
                    new_c.append(cs_[e] + jnp.sum(lk, axis=1, keepdims=True))
                return tuple(new_c), acc + merge(pvs)

            init = (tuple(jnp.zeros((tq, 1), F32) for _ in range(hp)), jnp.zeros((tq, LANES), F32))
            carry = block(i, init, True)
            cs_, acc = lax.fori_loop(0, i, lambda jj, c: block(i - 1 - jj, c, False), carry)
            o_ref[0] = acc
            st_ref[0, 0] = merge(list(cs_))

    outs, c_outs = _call(
        body, name=name, grid=(B, nh, nq),
        in_specs=[pl.BlockSpec((1, tq, qw), lambda b, h, i: (b, i, qc + h)),
                  pl.BlockSpec((1, Sk, qw), lambda b, h, i: (b, 0, kc + h)),
                  pl.BlockSpec((1, Sk, LANES), lambda b, h, i: (b, 0, vc + h))],
        out_specs=(pl.BlockSpec((1, tq, LANES), lambda b, h, i: (b, i, h)),
                   pl.BlockSpec((1, 1, tq, LANES), lambda b, h, i: (b, h, i, 0))),
        out_shape=(jax.ShapeDtypeStruct((B, S, nh * LANES), F32), jax.ShapeDtypeStruct((B, nh, S, LANES), F32)),
        scratch_shapes=[], args=(q, k, v), sem=("parallel", "parallel", "arbitrary"), comm=comm)
    return (*outs, c_outs) if comm is not None else tuple(outs)


def attn_bwd(q, k, v, o, st, do, *, name, mode, causal, hp, qk_w, v_w, scale, nh, qc, kc, vc, comm=None):
    qw = _att_dims(mode, hp, qk_w, v_w)
    B, S = q.shape[0], q.shape[1]
    Sk = k.shape[1]
    tq = _pick(S, ATT_TILE)
    tk = _pick(Sk, ATT_TILE)
    if causal:
        assert tq == tk and S == Sk
    nq, nkv = S // tq, Sk // tk
    pre_scaled = math.frexp(scale)[0] == 0.5
    post = 1.0 if pre_scaled else scale

    def body(q_ref, k_ref, v_ref, o_ref, st_ref, do_ref, dq_ref, dk_ref, dv_ref):
        i = pl.program_id(2)

        @pl.when(i == 0)
        def _():
            dk_ref[...] = jnp.zeros_like(dk_ref)
            dv_ref[...] = jnp.zeros_like(dv_ref)

        qv = q_ref[0].astype(BF16)
        dov = do_ref[0]
        stv = st_ref[0, 0]
        masks_qk = _lane_masks(hp, qk_w) if qk_w < LANES else None
        masks_v = _lane_masks(hp, v_w) if v_w < LANES else None
        if qk_w == LANES:
            qs = [qv[:, e * LANES:(e + 1) * LANES] for e in range(hp)]
        else:
            qs = [jnp.where(masks_qk[e], qv, jnp.zeros_like(qv)) for e in range(hp)]
        if pre_scaled:
            qs = [t * scale for t in qs]
        if hp == 1:
            dos = [dov.astype(BF16)]
            stats = [stv[:, 0:1]]
        else:
            dos = [jnp.where(masks_v[e], dov, 0.0).astype(BF16) for e in range(hp)]
            stats = [stv[:, e * v_w:e * v_w + 1] for e in range(hp)]
        if mode == "softmax":
            prod = dov * o_ref[0]
            if hp == 1:
                dsum = [jnp.sum(prod, axis=1, keepdims=True)]
            else:
                dsum = [jnp.sum(jnp.where(masks_v[e], prod, 0.0), axis=1, keepdims=True) for e in range(hp)]
        rows = lax.broadcasted_iota(jnp.int32, (tq, tk), 0)
        cols = lax.broadcasted_iota(jnp.int32, (tq, tk), 1)
        if mode == "sb":
            tri_le, tri_lt = _tri("le"), _tri("lt", terms=1)

        def kv_tile(j):
            off = pl.multiple_of(j * tk, tk)
            kj = k_ref[0, pl.ds(off, tk), :].astype(BF16)
            vj = v_ref[0, pl.ds(off, tk), :].astype(BF16)
            ks = [kj[:, e * LANES:(e + 1) * LANES] for e in range(hp)] if qk_w == LANES else [kj] * hp
            return off, ks, vj

        def scatter(off, dz_list, p_list):
            dvj = None
            for e in range(hp):
                t = _dot(p_list[e], dos[e], 0, 0)
                dvj = t if dvj is None else dvj + t
            dv_ref[0, pl.ds(off, tk), :] += dvj
            if qk_w == LANES:
                for e in range(hp):
                    dk_ref[0, pl.ds(off, tk), e * LANES:(e + 1) * LANES] += _dot(dz_list[e], qs[e], 0, 0)
            else:
                dkj = None
                for e in range(hp):
                    t = _dot(dz_list[e], qs[e], 0, 0)
                    dkj = t if dkj is None else dkj + t
                dk_ref[0, pl.ds(off, tk), :] += dkj

        def dq_add(dqs, dz_list, ks):
            out = []
            for e in range(hp):
                out.append(dqs[e] + _dot(dz_list[e], ks[e], 1, 0))
            return tuple(out)

        dq0 = tuple(jnp.zeros((tq, LANES), F32) for _ in range(hp))

        if mode == "softmax":
            def block(j, dqs, diag):
                off, ks, vj = kv_tile(j)
                dzs, ps = [], []
                for e in range(hp):
                    s = _mul(_dot(qs[e], ks[e], 1, 1), post)
                    p = jnp.exp(s - stats[e])
                    if diag:
                        p = jnp.where(rows >= cols, p, 0.0)
                    dp = _dot(dos[e], vj, 1, 1)
                    dzs.append(_mul(p * (dp - dsum[e]), post).astype(BF16))
                    ps.append(p.astype(BF16))
                scatter(off, dzs, ps)
                return dq_add(dqs, dzs, ks)

            if causal:
                dqs = lax.fori_loop(0, i, lambda j, c: block(j, c, False), dq0)
                dqs = block(i, dqs, True)
            else:
                dqs = lax.fori_loop(0, nkv, lambda j, c: block(j, c, False), dq0)
        else:
            def block(j, carry, diag):
                dqs, cps, cgs = carry
                off, ks, vj = kv_tile(j)
                dzs, ps, new_p, new_g = [], [], [], []
                for e in range(hp):
                    z = _mul(_dot(qs[e], ks[e], 1, 1), post)
                    lk = -jnp.maximum(z, 0.0) - jnp.log(1.0 + jnp.exp(-jnp.abs(z)))
                    lz = lk + z
                    sig = jnp.exp(lz)
                    keep = 1.0 - sig
                    if diag:
                        lk = jnp.where(rows > cols, lk, 0.0)
                    keep_after = stats[e] - cps[e] - _cum(lk, tri_le, "le")
                    a = jnp.exp(lz + keep_after)
                    if diag:
                        a = jnp.where(rows > cols, a, 0.0)
                    g = _dot(dos[e], vj, 1, 1) * a
                    gsum = cgs[e] + _cum(g, tri_lt, "lt")
                    dz = _mul(g * keep - gsum * sig, post)
                    if diag:
                        dz = jnp.where(rows > cols, dz, 0.0)
                    dzs.append(dz.astype(BF16))
                    ps.append(a.astype(BF16))
                    new_p.append(cps[e] + jnp.sum(lk, axis=1, keepdims=True))
                    new_g.append(cgs[e] + jnp.sum(g, axis=1, keepdims=True))
                scatter(off, dzs, ps)
                return dq_add(dqs, dzs, ks), tuple(new_p), tuple(new_g)

            zc = tuple(jnp.zeros((tq, 1), F32) for _ in range(hp))
            carry = lax.fori_loop(0, i, lambda j, c: block(j, c, False), (dq0, zc, zc))
            dqs, _, _ = block(i, carry, True)

        if pre_scaled:
            dqs = [t * scale for t in dqs]
        if qk_w == LANES:
            for e in range(hp):
                dq_ref[0, :, e * LANES:(e + 1) * LANES] = dqs[e]
        else:
            dq_ref[0] = jnp.where(masks_qk[0], dqs[0], dqs[1])

    outs, c_outs = _call(
        body, name=name, grid=(B, nh, nq),
        in_specs=[pl.BlockSpec((1, tq, qw), lambda b, h, i: (b, i, qc + h)),
                  pl.BlockSpec((1, Sk, qw), lambda b, h, i: (b, 0, kc + h)),
                  pl.BlockSpec((1, Sk, LANES), lambda b, h, i: (b, 0, vc + h)),
                  pl.BlockSpec((1, tq, LANES), lambda b, h, i: (b, i, h)),
                  pl.BlockSpec((1, 1, tq, LANES), lambda b, h, i: (b, h, i, 0)),
                  pl.BlockSpec((1, tq, LANES), lambda b, h, i: (b, i, h))],
        out_specs=(pl.BlockSpec((1, tq, qw), lambda b, h, i: (b, i, h)),
                   pl.BlockSpec((1, Sk, qw), lambda b, h, i: (b, 0, h)),
                   pl.BlockSpec((1, Sk, LANES), lambda b, h, i: (b, 0, h))),
        out_shape=(jax.ShapeDtypeStruct((B, S, nh * qw), F32), jax.ShapeDtypeStruct((B, Sk, nh * qw), F32),
                   jax.ShapeDtypeStruct((B, Sk, nh * LANES), F32)),
        scratch_shapes=[], args=(q, k, v, o, st, do), sem=("parallel", "parallel", "arbitrary"), comm=comm)
    return (*outs, c_outs) if comm is not None else tuple(outs)


def _pos(p):
    return 4 * p[0] + 2 * p[1] + p[2]


N_CHIP = 4
MESH_ID = pl.DeviceIdType.MESH
_ANY = pl.BlockSpec(memory_space=pl.ANY)


def _me():
    return lax.axis_index("x"), lax.axis_index("y"), lax.axis_index("c")


def _other_chips(x, y):
    return [(1 - x, y), (x, 1 - y), (1 - x, 1 - y)]


class _Comm:
    def __init__(self, arrs, out_shapes, n_sem, n_local, start, finish):
        self.arrs, self.out_shapes, self.start, self.finish = list(arrs), list(out_shapes), start, finish
        self.scratch = [pltpu.SemaphoreType.DMA((n_sem,)), pltpu.SemaphoreType.DMA((n_sem,)),
                        pltpu.SemaphoreType.DMA((max(n_local, 1),))]


def _run_comm(comm, name):
    n = len(comm.arrs)

    def body(*refs):
        r = (refs[:n], refs[n:2 * n], refs[2 * n], refs[2 * n + 1], refs[2 * n + 2])
        comm.start(*r)
        comm.finish(*r)

    return pl.pallas_call(
        body, name=name, in_specs=[_ANY] * n, out_specs=[_ANY] * n, out_shape=comm.out_shapes,
        scratch_shapes=comm.scratch, compiler_params=pltpu.CompilerParams(has_side_effects=True),
    )(*comm.arrs)


def _call(body, *, name, grid, in_specs, out_specs, out_shape, scratch_shapes, args, sem, comm=None):
    in_specs, out_specs, out_shape = list(in_specs), list(out_specs), list(out_shape)
    if comm is None:
        res = pl.pallas_call(body, name=name, grid=grid, in_specs=in_specs, out_specs=out_specs, out_shape=out_shape,
                             scratch_shapes=scratch_shapes, compiler_params=_cparams(sem))(*args)
        return list(res), []
    n_in, n_out, n_scr, nc = len(in_specs), len(out_specs), len(scratch_shapes), len(comm.arrs)

    def wrapped(*refs):
        ins, refs = refs[:n_in], refs[n_in:]
        c_in, refs = refs[:nc], refs[nc:]
        outs, refs = refs[:n_out], refs[n_out:]
        c_out, refs = refs[:nc], refs[nc:]
        scr, sems = refs[:n_scr], refs[n_scr:]
        ids = [pl.program_id(a) for a in range(len(grid))]
        first = functools.reduce(jnp.logical_and, [i == 0 for i in ids])
        last = functools.reduce(jnp.logical_and, [i == g - 1 for i, g in zip(ids, grid)])

        @pl.when(first)
        def _():
            comm.start(c_in, c_out, *sems)

        body(*ins, *outs, *scr)

        @pl.when(last)
        def _():
            comm.finish(c_in, c_out, *sems)

    res = pl.pallas_call(
        wrapped, name=name, grid=grid, in_specs=in_specs + [_ANY] * nc, out_specs=out_specs + [_ANY] * nc,
        out_shape=out_shape + comm.out_shapes, scratch_shapes=list(scratch_shapes) + comm.scratch,
        compiler_params=pltpu.CompilerParams(dimension_semantics=("arbitrary",) * len(grid),
                                             vmem_limit_bytes=VMEM_LIMIT, has_side_effects=True),
    )(*args, *comm.arrs)
    return list(res[:n_out]), list(res[n_out:])


def gather_two_level(bufs, *, name=None):
    n = len(bufs)

    def parts(x_refs, out_refs, send_sems, recv_sems, local_sems):
        x, y, c = _me()
        me, sibling = (x, y, c), (x, y, 1 - c)
        chips = _other_chips(x, y)

        def copy(a, k, block, to, from_input=False):
            dst = out_refs[a].at[_pos(block)]
            return pltpu.make_async_remote_copy(src_ref=x_refs[a] if from_input else dst, dst_ref=dst,
                                                send_sem=send_sems.at[7 * a + k], recv_sem=recv_sems.at[7 * a + k],
                                                device_id=to, device_id_type=MESH_ID)

        mine = [pltpu.make_async_copy(x_refs[a], out_refs[a].at[_pos(me)], local_sems.at[a]) for a in range(n)]
        first = []
        for a in range(n):
            first.append(copy(a, 0, me, sibling, from_input=True))
            first += [copy(a, 1 + j, me, (*chip, c), from_input=True) for j, chip in enumerate(chips)]
        return copy, mine, first, me, sibling, chips, c

    def start(*refs):
        _, mine, first, *_ = parts(*refs)
        for cp in mine + first:
            cp.start()

    def finish(*refs):
        copy, mine, first, me, sibling, chips, c = parts(*refs)
        passed = []
        for j, chip in enumerate(chips):
            for a in range(n):
                copy(a, 1 + j, (*chip, c), me).wait_recv()
                passed.append(copy(a, 4 + j, (*chip, c), sibling))
                passed[-1].start()
        for a in range(n):
            copy(a, 0, sibling, me).wait_recv()
            for j, chip in enumerate(chips):
                copy(a, 4 + j, (*chip, 1 - c), me).wait_recv()
        for cp in first + passed:
            cp.wait_send()
        for cp in mine:
            cp.wait()

    out_shapes = [jax.ShapeDtypeStruct((N_DEV,) + b.shape, b.dtype) for b in bufs]
    comm = _Comm(bufs, out_shapes, 7 * n, n, start, finish)
    return _run_comm(comm, name) if name else comm


def sibling_exchange(sends, *, name=None):
    n = len(sends)

    def copies(s_refs, out_refs, send_sems, recv_sems, local_sems):
        x, y, c = _me()
        return [pltpu.make_async_remote_copy(src_ref=s_refs[a].at[1 - c], dst_ref=out_refs[a], send_sem=send_sems.at[a],
                                             recv_sem=recv_sems.at[a], device_id=(x, y, 1 - c), device_id_type=MESH_ID)
                for a in range(n)]

    def start(*refs):
        for cp in copies(*refs):
            cp.start()

    def finish(*refs):
        for cp in copies(*refs):
            cp.wait()

    out_shapes = [jax.ShapeDtypeStruct(s.shape[1:], s.dtype) for s in sends]
    comm = _Comm(sends, out_shapes, n, 0, start, finish)
    return _run_comm(comm, name) if name else comm


def _flat2(shape):
    return math.prod(shape[:-1]), shape[-1]


def _row_tile(r):
    return r if r <= 512 else _pick8(r, 256)


def _pick8(n, pref):
    t = pref
    while n % t:
        t -= 8
    return t


def chip_sum(send, got, core, *, name):
    shape = got.shape[1:]
    r, cdim = _flat2(shape)
    tr = _row_tile(r)
    send = send.reshape(2, N_CHIP, r, cdim)
    got = got.reshape(N_CHIP, r, cdim)

    def body(core_ref, s_ref, g_ref, o_ref):
        o_ref[...] = (s_ref[0].astype(F32) + g_ref[...].astype(F32)).astype(o_ref.dtype)

    blk = pl.BlockSpec((N_CHIP, tr, cdim), lambda i, core_ref: (0, i, 0))
    out = pl.pallas_call(
        body, name=name,
        grid_spec=pltpu.PrefetchScalarGridSpec(
            num_scalar_prefetch=1, grid=(r // tr,),
            in_specs=[pl.BlockSpec((1, N_CHIP, tr, cdim), lambda i, core_ref: (core_ref[0], 0, i, 0)), blk],
            out_specs=blk),
        out_shape=jax.ShapeDtypeStruct((N_CHIP, r, cdim), BF16),
        compiler_params=_cparams(("parallel",)),
    )(core, send, got)
    return out.reshape((N_CHIP,) + shape)


def chip_exchange(sums, *, name=None):
    n = len(sums)

    def copies(s_refs, out_refs, send_sems, recv_sems, local_sems):
        x, y, c = _me()
        mine = 2 * x + y
        chips = _other_chips(x, y)
        local = [pltpu.make_async_copy(s_refs[a].at[mine], out_refs[a].at[mine], local_sems.at[a]) for a in range(n)]
        sends, recvs = [], []
        for a in range(n):
            for j, (px, py) in enumerate(chips):
                sems = dict(send_sem=send_sems.at[3 * a + j], recv_sem=recv_sems.at[3 * a + j], device_id=(px, py, c),
                            device_id_type=MESH_ID)
                sends.append(pltpu.make_async_remote_copy(src_ref=s_refs[a].at[2 * px + py], dst_ref=out_refs[a].at[mine], **sems))
                recvs.append(pltpu.make_async_remote_copy(src_ref=s_refs[a].at[mine], dst_ref=out_refs[a].at[2 * px + py], **sems))
        return local, sends, recvs

    def start(*refs):
        local, sends, _ = copies(*refs)
        for cp in local + sends:
            cp.start()

    def finish(*refs):
        local, sends, recvs = copies(*refs)
        for cp in recvs:
            cp.wait_recv()
        for cp in sends:
            cp.wait_send()
        for cp in local:
            cp.wait()

    out_shapes = [jax.ShapeDtypeStruct(s.shape, s.dtype) for s in sums]
    comm = _Comm(sums, out_shapes, 3 * n, n, start, finish)
    return _run_comm(comm, name) if name else comm


def sum_adamw(parts, w, m, v, *, name):
    shape = w.shape
    R, cdim = _flat2(shape)
    n_slots = parts.shape[0]
    tr = _row_tile(R)
    parts = parts.reshape(n_slots, R, cdim)
    w, m, v = (t.reshape(1, R, cdim) for t in (w, m, v))
    c1 = 1.0 - ADAM_B1 ** ADAM_STEP
    c2 = 1.0 - ADAM_B2 ** ADAM_STEP

    def body(p_ref, w_ref, m_ref, v_ref, g_ref, d_ref, mo_ref, vo_ref):
        g = p_ref[0].astype(F32)
        for s in range(1, n_slots):
            g = g + p_ref[s].astype(F32)
        mn = ADAM_B1 * m_ref[...] + (1.0 - ADAM_B1) * g
        vn = ADAM_B2 * v_ref[...] + (1.0 - ADAM_B2) * jnp.square(g)
        g_ref[...] = g
        mo_ref[...] = mn
        vo_ref[...] = vn
        d_ref[...] = -ADAM_LR * ((mn / c1) / (jnp.sqrt(vn / c2) + ADAM_EPS) + ADAM_WD * w_ref[...])

    rs = pl.BlockSpec((None, tr, cdim), lambda i: (0, i, 0))
    sd = jax.ShapeDtypeStruct((1, R, cdim), F32)
    res = pl.pallas_call(
        body, name=name, grid=(R // tr,),
        in_specs=[pl.BlockSpec((n_slots, tr, cdim), lambda i: (0, i, 0)), rs, rs, rs],
        out_specs=(rs, rs, rs, rs), out_shape=(sd, sd, sd, sd),
        compiler_params=_cparams(("parallel",)),
    )(parts, w, m, v)
    return [t.reshape(shape) for t in res]


WEIGHTS = ("ln_mix_pre", "w_in", "b_gate", "q_norm", "w_uq", "kv_norm", "w_uk", "w_uv", "mem_norm", "w_mem_kv",
           "w_branch_out", "w_out", "ln_mix_post", "ln_mlp_pre", "w_mlp_up", "w_mlp_down", "ln_mlp_post")
BIG = dict(w_in=((D_MODEL, IN_WIDTH), 1), w_uq=((Q_LORA, MLA_HEADS * (MLA_NOPE + MLA_ROPE)), 1),
           w_uk=((KV_LORA, MLA_HEADS * MLA_NOPE), 1), w_uv=((KV_LORA, MLA_HEADS * MLA_V), 1),
           w_mem_kv=((D_MODEL, 2 * QM_W), 0), w_branch_out=((N_BRANCH, BRANCH_W, D_MODEL), 2),
           w_out=((D_MODEL, D_MODEL), 0), w_mlp_up=((D_MODEL, D_FF), 1), w_mlp_down=((D_FF, D_MODEL), 0))
SMALL = tuple(n for n in WEIGHTS if n not in BIG)


def _shard_shape(name):
    full, ax = BIG[name]
    return tuple(s // N_DEV if a == ax else s for a, s in enumerate(full))


PACK_TILE = 512


def _pad_rows(a):
    r = a.shape[-2]
    to = PACK_TILE if r > PACK_TILE else 8
    pad = [(0, 0)] * a.ndim
    pad[-2] = (0, (-r) % to)
    return jnp.pad(a, pad)


def _pack_rows(arrs):
    return _pad_rows(jnp.concatenate([a.reshape(-1, LANES) for a in arrs], axis=0))


def _unpack_rows(packed, shapes, lead=()):
    out, r = [], 0
    for shp in shapes:
        n = math.prod(shp) // LANES
        out.append(packed[..., r:r + n, :].reshape(lead + tuple(shp)))
        r += n
    return out


def _slots_to_full(name, slots):
    full, ax = BIG[name]
    return jnp.moveaxis(slots, 0, ax).reshape(full)


def _full_to_slots(name, w):
    full, ax = BIG[name]
    split = full[:ax] + (N_DEV, full[ax] // N_DEV) + full[ax + 1:]
    return jnp.moveaxis(w.reshape(split), ax, 0)


def _full_to_owner(name, w):
    s = _full_to_slots(name, w)
    return jnp.swapaxes(s.reshape((N_CHIP, 2) + s.shape[1:]), 0, 1)


def _split_in(w):
    offs = np.cumsum((0,) + IN_SIZES)
    names = ("cq", "ckv", "kr", "sb", "qm", "gate")
    return {n: w[:, offs[i]:offs[i + 1]] for i, n in enumerate(names)}


def _pad_in(w):
    p = _split_in(w)
    p["kr"] = jnp.pad(p["kr"], ((0, 0), (0, KR_PAD - MLA_ROPE)))
    p["zpad"] = jnp.zeros((w.shape[0], PAD_SIZES["zpad"]), w.dtype)
    return jnp.concatenate([p[n] for n in PAD_ORDER], axis=1)


def _unpad_in(wp):
    p = {n: wp[:, PAD_OFF[n]:PAD_OFF[n] + PAD_SIZES[n]] for n in PAD_ORDER}
    p["kr"] = p["kr"][:, :MLA_ROPE]
    return jnp.concatenate([p[n] for n in ("cq", "ckv", "kr", "sb", "qm", "gate")], axis=1)


def _in_segments():
    orig = dict(zip(("cq", "ckv", "kr", "sb", "qm", "gate"), zip(np.cumsum((0,) + IN_SIZES[:-1]).tolist(), IN_SIZES)))
    shard = IN_WIDTH // N_DEV
    segs = []
    for n in PAD_ORDER:
        if n not in orig:
            continue
        o0, w = orig[n]
        c = o0
        while c < o0 + w:
            s = c // shard
            e = min(o0 + w, (s + 1) * shard)
            segs.append((s, c - s * shard, PAD_OFF[n] + (c - o0), e - c))
            c = e
    return segs


def in_slots_to_pad(slots, *, name):
    _, R, shard = slots.shape
    tr = _pick8(R, ROW_TILE)
    segs = _in_segments()

    def body(in_ref, out_ref):
        out_ref[...] = jnp.zeros_like(out_ref)
        for s, a, d, w in segs:
            out_ref[:, d:d + w] = in_ref[s, :, a:a + w]

    return pl.pallas_call(
        body, name=name, grid=(R // tr,), in_specs=[pl.BlockSpec((N_DEV, tr, shard), lambda i: (0, i, 0))],
        out_specs=pl.BlockSpec((tr, IN_PAD), lambda i: (i, 0)), out_shape=jax.ShapeDtypeStruct((R, IN_PAD), slots.dtype),
        compiler_params=_cparams(("parallel",)),
    )(slots)


def in_pad_to_owner(g, *, name):
    R = g.shape[0]
    shard = IN_WIDTH // N_DEV
    tr = _pick8(R, ROW_TILE)
    segs = _in_segments()

    def body(in_ref, out_ref):
        for s, a, d, w in segs:
            out_ref[s % 2, s // 2, :, a:a + w] = in_ref[:, d:d + w]

    return pl.pallas_call(
        body, name=name, grid=(R // tr,), in_specs=[pl.BlockSpec((tr, IN_PAD), lambda i: (i, 0))],
        out_specs=pl.BlockSpec((2, N_CHIP, tr, shard), lambda i: (0, 0, i, 0)),
        out_shape=jax.ShapeDtypeStruct((2, N_CHIP, R, shard), g.dtype),
        compiler_params=_cparams(("parallel",)),
    )(g)


def _pad_heads(w, width):
    r = w.shape[0]
    return jnp.pad(w.reshape(r, MLA_HEADS, width), ((0, 0), (0, 0), (0, HEAD_PAD - width))).reshape(r, MLA_PAD_W)


def _unpad_heads(wp, width):
    r = wp.shape[0]
    return wp.reshape(r, MLA_HEADS, HEAD_PAD)[:, :, :width].reshape(r, MLA_HEADS * width)


def _rope_inv_freq():
    half = MLA_ROPE // 2
    inv = 1.0 / (ROPE_THETA ** (jnp.arange(half, dtype=F32) * (2.0 / MLA_ROPE)))
    tab = jnp.zeros((LANES,), F32)
    tab = tab.at[MLA_NOPE:MLA_NOPE + half].set(inv).at[MLA_NOPE + half:MLA_NOPE + MLA_ROPE].set(inv)
    return tab.reshape(1, LANES)


def _as_tuple(r):
    return r if isinstance(r, tuple) else (r,)


class _Exchange:
    def __init__(self, w):
        self.w = w
        self.rest = tuple(n for n in BIG if n != "w_in")
        self.pending = {}
        self.reduced = {}

    def first_comm(self):
        return gather_two_level([self.w["w_in"].astype(BF16)])

    def first_weights(self, slots):
        return in_slots_to_pad(slots[0], name="w_in_layout")

    def rest_comm(self):
        return gather_two_level([self.w[n].astype(BF16) for n in self.rest])

    def rest_weights(self, slots):
        return {n: s if n == "w_mlp_up" else _slots_to_full(n, s) for n, s in zip(self.rest, slots)}

    def sibling_comm(self, tag, grads):
        self.pending[tag] = (tuple(grads), [grads[n] for n in grads])
        return sibling_exchange(self.pending[tag][1])

    def chip_comm(self, tag, gots):
        names, sends = self.pending[tag]
        core = lax.axis_index("c").astype(jnp.int32).reshape(1)
        sums = [chip_sum(s, g, core, name=f"chip_sum_{n}") for n, s, g in zip(names, sends, gots)]
        self.pending[tag] = names
        return chip_exchange(sums)

    def reduce_start(self, tag, grads):
        return self.chip_comm(tag, _run_comm(self.sibling_comm(tag, grads), f"sibling_{tag}"))

    def reduce_done(self, tag, recvs):
        self.reduced.update(zip(self.pending.pop(tag), recvs))


def _local_step(x, mem, positions, tgt, sm, ex):
    B, S, D = x.shape
    M = mem.shape[1]
    T = B * S
    x2 = x.reshape(T, D)
    mem2 = mem.reshape(B * M, D)
    pos = positions.reshape(T, 1).astype(F32)
    invf = _rope_inv_freq()
    cq_col, ckv_col = PAD_OFF["cq"] // Q_LORA, PAD_OFF["ckv"] // KV_LORA
    sb_col, qm_col = PAD_OFF["sb"] // LANES, PAD_OFF["qm"] // LANES
    sb_blk = SB_HEADS * SB_DIM // LANES
    mla = dict(mode="softmax", causal=True, hp=2, qk_w=128, v_w=64, scale=(MLA_NOPE + MLA_ROPE) ** -0.5,
               nh=MLA_HEADS // 2, qc=0, kc=0, vc=0)
    sbk = dict(mode="sb", causal=True, hp=2, qk_w=64, v_w=64, scale=SB_DIM ** -0.5, nh=SB_HEADS // 2,
               qc=sb_col, kc=sb_col + sb_blk, vc=sb_col + 2 * sb_blk)
    mca = dict(mode="softmax", causal=False, hp=1, qk_w=128, v_w=128, scale=MEM_DIM ** -0.5, nh=MEM_HEADS,
               qc=qm_col, kc=0, vc=MEM_HEADS)

    h, *slots = _as_tuple(rms_fwd(x2, sm["ln_mix_pre"], name="rms_mix_pre", comm=ex.first_comm()))
    w_in_pad = ex.first_weights(slots[0] if slots else None)
    proj = matmul(h, w_in_pad, name="mm_in")
    proj3 = proj.reshape(B, S, IN_PAD)
    o_sb, st_sb, *slots = attn_fwd(proj3, proj3, proj3, name="sb_fwd", comm=ex.rest_comm(), **sbk)
    W = ex.rest_weights(slots[0] if slots else None)
    w_uq_pad = _pad_heads(W["w_uq"], MLA_NOPE + MLA_ROPE)
    w_uk_pad = _pad_heads(W["w_uk"], MLA_NOPE)
    cqn = rms_fwd(proj, sm["q_norm"], n=Q_LORA, col=cq_col, name="rms_q")
    ckvn = rms_fwd(proj, sm["kv_norm"], n=KV_LORA, col=ckv_col, name="rms_kv")
    qpre = matmul(cqn, w_uq_pad, name="mm_uq")
    kpre = matmul(ckvn, w_uk_pad, name="mm_uk")
    v_mla = matmul(ckvn, W["w_uv"], out_dtype=BF16, name="mm_uv").reshape(B, S, -1)
    q_pad, k_pad = rope_fwd(qpre, kpre, proj, pos, invf, name="rope_fwd")
    q_pad, k_pad = q_pad.reshape(B, S, -1), k_pad.reshape(B, S, -1)
    o_mla, st_mla = attn_fwd(q_pad, k_pad, v_mla, name="mla_fwd", **mla)
    memh = rms_fwd(mem2, sm["mem_norm"], name="rms_mem")
    mkv = matmul(memh, W["w_mem_kv"], out_dtype=BF16, name="mm_memkv").reshape(B, M, -1)
    o_mem, st_mem = attn_fwd(proj3, mkv, mkv, name="mem_fwd", **mca)
    outs = [o.reshape(T, BRANCH_W) for o in (o_mla, o_sb, o_mem)]
    ps = [matmul(o, W["w_branch_out"][b], out_dtype=BF16, name=f"mm_bo{b}") for b, o in enumerate(outs)]
    merged = gate_merge(proj, sm["b_gate"], ps, name="gate_merge")
    y, x1, h2 = matmul(merged, W["w_out"], name="mm_out_norms",
                       rows=dict(fn=_mix_post_rows, row_ins=[x2], vec_ins=[sm["ln_mix_post"], sm["ln_mlp_pre"]],
                                 row_outs=[F32, F32, BF16], vec_outs=[]))
    u = matmul(h2, W["w_mlp_up"], b_slots="n", act="relu2", out_dtype=BF16, name="mm_up")
    dx2, dd, loss_p, dg_mlp_post = matmul(
        u, W["w_mlp_down"], name="mm_down_loss",
        rows=dict(fn=_loss_rows, row_ins=[x1, tgt.reshape(T, D)], vec_ins=[sm["ln_mlp_post"]], row_outs=[F32, BF16],
                  vec_outs=[LANES, D]))

    da = matmul(dd, W["w_mlp_down"], tb=True, act="relu2_bwd", act_in=u, out_dtype=BF16, name="mm_down_dx")
    g_down = matmul(u, dd, ta=True, out_dtype=BF16, out_slots="m", name="mm_down_dw")
    g_up = matmul(h2, da, ta=True, out_dtype=BF16, out_slots="n", name="mm_up_dw")
    sib = ex.sibling_comm("mlp", dict(w_mlp_down=g_down, w_mlp_up=g_up))
    dx1, dy, dg_mlp_pre, dg_mix_post, *got = matmul(
        da, _slots_to_full("w_mlp_up", W["w_mlp_up"]), tb=True, name="mm_up_dx_norms", comm=sib,
        rows=dict(fn=_mlp_pre_bwd_rows, row_ins=[dx2, x1, y], vec_ins=[sm["ln_mlp_pre"], sm["ln_mix_post"]],
                  row_outs=[F32, BF16], vec_outs=[D, D]))
    red_mlp = ex.chip_comm("mlp", got[0] if got else None)
    dmerged = matmul(dy, W["w_out"], tb=True, name="mm_out_dx")
    g_out = matmul(merged, dy, ta=True, out_dtype=BF16, name="mm_out_dw")
    dlog, dp0, dp1, dp2, db_gate = gate_bwd(dmerged, proj, sm["b_gate"], ps, name="gate_bwd")
    dps = (dp0, dp1, dp2)
    g_bo = jnp.stack([matmul(outs[b], dps[b], ta=True, out_dtype=BF16, name=f"mm_bo{b}_dw") for b in range(N_BRANCH)])
    sib = ex.sibling_comm("mix", dict(w_out=_full_to_owner("w_out", g_out),
                                      w_branch_out=_full_to_owner("w_branch_out", g_bo)))
    dos = [matmul(dps[b], W["w_branch_out"][b], tb=True, name=f"mm_bo{b}_dx") for b in range(N_BRANCH - 1)]
    last_do, *got = _as_tuple(matmul(dps[-1], W["w_branch_out"][N_BRANCH - 1], tb=True, comm=sib,
                                     name=f"mm_bo{N_BRANCH - 1}_dx"))
    red_mix = ex.chip_comm("mix", got[0] if got else None)
    dos = [t.reshape(B, S, BRANCH_W) for t in dos + [last_do]]
    dq_pad, dk_pad, dv_mla, *got = attn_bwd(q_pad, k_pad, v_mla, o_mla, st_mla, dos[0], name="mla_bwd", comm=red_mlp, **mla)
    ex.reduce_done("mlp", got[0] if got else None)
    dsq, dsk, dsv, *got = attn_bwd(proj3, proj3, proj3, o_sb, st_sb, dos[1], name="sb_bwd", comm=red_mix, **sbk)
    ex.reduce_done("mix", got[0] if got else None)
    dqm, dmk, dmv = attn_bwd(proj3, mkv, mkv, o_mem, st_mem, dos[2], name="mem_bwd", **mca)
    dmkv = jnp.concatenate([dmk, dmv], axis=-1).astype(BF16).reshape(B * M, -1)
    dmemh = matmul(dmkv, W["w_mem_kv"], tb=True, name="mm_memkv_dx")
    g_memkv = matmul(memh, dmkv, ta=True, out_dtype=BF16, name="mm_memkv_dw")
    _, dg_mem_norm = rms_bwd(dmemh, mem2, sm["mem_norm"], name="rms_mem_bwd")
    dq_pad, dk_pad, dv_mla = dq_pad.reshape(T, -1), dk_pad.reshape(T, -1), dv_mla.reshape(T, -1)
    dqpre, dkr = rope_bwd(dq_pad, dk_pad, pos, invf, name="rope_bwd")
    dcqn = matmul(dqpre, w_uq_pad, tb=True, name="mm_uq_dx")
    g_uq = _unpad_heads(matmul(cqn, dqpre, ta=True, out_dtype=BF16, name="mm_uq_dw"), MLA_NOPE + MLA_ROPE)
    dckvn = matmul(dk_pad, w_uk_pad, tb=True, name="mm_uk_dx")
    dckvn = matmul(dv_mla, W["w_uv"], tb=True, acc=dckvn, name="mm_uv_dx")
    g_uk = _unpad_heads(matmul(ckvn, dk_pad, ta=True, out_dtype=BF16, name="mm_uk_dw"), MLA_NOPE)
    g_uv = matmul(ckvn, dv_mla, ta=True, out_dtype=BF16, name="mm_uv_dw")
    dcq, dg_q_norm = rms_bwd(dcqn, proj, sm["q_norm"], n=Q_LORA, col=cq_col, out_dtype=BF16, name="rms_q_bwd")
    dckv, dg_kv_norm = rms_bwd(dckvn, proj, sm["kv_norm"], n=KV_LORA, col=ckv_col, out_dtype=BF16, name="rms_kv_bwd")
    pieces = dict(gate=dlog, sb=jnp.concatenate([dsq, dsk, dsv], axis=-1).reshape(T, -1), qm=dqm.reshape(T, -1),
                  ckv=dckv, cq=dcq, kr=dkr, zpad=jnp.zeros((T, PAD_SIZES["zpad"]), BF16))
    dproj = jnp.concatenate([pieces[n].astype(BF16) for n in PAD_ORDER], axis=1)
    g_in = in_pad_to_owner(matmul(h, dproj, ta=True, out_dtype=BF16, name="mm_in_dw"), name="g_in_layout")
    last = dict(w_uq=g_uq, w_uk=g_uk, w_uv=g_uv, w_mem_kv=g_memkv)
    red_in = ex.reduce_start("in", dict(w_in=g_in, **{n: _full_to_owner(n, g) for n, g in last.items()}))
    dx, dg_mix_pre, *got = matmul(
        dproj, w_in_pad, tb=True, comm=red_in, name="mm_in_dx_norm",
        rows=dict(fn=_mix_pre_bwd_rows, row_ins=[x2, dx1], vec_ins=[sm["ln_mix_pre"]], row_outs=[F32], vec_outs=[D]))
    ex.reduce_done("in", got[0] if got else None)

    small = dict(ln_mix_pre=dg_mix_pre, b_gate=db_gate, q_norm=dg_q_norm, kv_norm=dg_kv_norm, mem_norm=dg_mem_norm,
                 ln_mix_post=dg_mix_post, ln_mlp_pre=dg_mlp_pre, ln_mlp_post=dg_mlp_post)
    return loss_p, dx.reshape(B, S, D), small


def kernel(x, mem, positions, ln_mix_pre, w_in, b_gate, q_norm, w_uq, kv_norm, w_uk, w_uv, mem_norm, w_mem_kv, w_branch_out, w_out, ln_mix_post, ln_mlp_pre, w_mlp_up, w_mlp_down, ln_mlp_post, loss_target, m_ln_mix_pre, m_w_in, m_b_gate, m_q_norm, m_w_uq, m_kv_norm, m_w_uk, m_w_uv, m_mem_norm, m_w_mem_kv, m_w_branch_out, m_w_out, m_ln_mix_post, m_ln_mlp_pre, m_w_mlp_up, m_w_mlp_down, m_ln_mlp_post, v_ln_mix_pre, v_w_in, v_b_gate, v_q_norm, v_w_uq, v_kv_norm, v_w_uk, v_w_uv, v_mem_norm, v_w_mem_kv, v_w_branch_out, v_w_out, v_ln_mix_post, v_ln_mlp_pre, v_w_mlp_up, v_w_mlp_down, v_ln_mlp_post):
    given = dict(locals())
    w = {n: given[n][0] for n in WEIGHTS}
    m = {n: given["m_" + n][0] for n in WEIGHTS}
    v = {n: given["v_" + n][0] for n in WEIGHTS}
    sm = {n: w[n].reshape(1, -1) for n in SMALL}
    ex = _Exchange(w)
    loss_row, grad_x, g_small = _local_step(x, mem, positions, loss_target, sm, ex)

    res = {n: sum_adamw(ex.reduced[n], given[n], given["m_" + n], given["v_" + n], name=f"adamw_{n}") for n in BIG}

    small_shapes = [w[n].shape for n in SMALL] + [(LANES,)]
    parts = gather_two_level([_pack_rows([g_small[n] for n in SMALL] + [loss_row])], name="gather_small_grads")[0]
    no_row = jnp.zeros((1, LANES), F32)
    res_small = sum_adamw(parts, *[_pack_rows([t[n] for n in SMALL] + [no_row]) for t in (w, m, v)], name="adamw_replicated")
    res_small = [_unpack_rows(r, small_shapes) for r in res_small]
    for i, n in enumerate(SMALL):
        res[n] = [r[i][None] for r in res_small]
    loss = res_small[0][len(SMALL)][0]

    out = [loss, grad_x]
    for k in range(4):
        out += [res[n][k] for n in WEIGHTS]
    return tuple(out)
```

```python
import functools
import math

import numpy as np
import jax
import jax.numpy as jnp
from jax import lax
from jax.experimental import pallas as pl
from jax.experimental.pallas import tpu as pltpu

F32 = jnp.float32
BF16 = jnp.bfloat16

D_MODEL = 1024
MEM_HEADS, MEM_DIM = 4, 128
MLA_HEADS, MLA_NOPE, MLA_ROPE, MLA_V = 8, 64, 32, 64
Q_LORA, KV_LORA = 384, 256
ROPE_THETA = 10000.0
SB_HEADS, SB_DIM = 8, 64
D_FF = 4 * D_MODEL
N_BRANCH, BRANCH_W = 3, 512
EPS = 1e-6
SB_W = 3 * SB_HEADS * SB_DIM
QM_W = MEM_HEADS * MEM_DIM
GATE_W = N_BRANCH * D_MODEL
IN_SIZES = (Q_LORA, KV_LORA, MLA_ROPE, SB_W, QM_W, GATE_W)
IN_WIDTH = sum(IN_SIZES)
ADAM_LR, ADAM_B1, ADAM_B2, ADAM_EPS, ADAM_WD, ADAM_STEP = 0.001, 0.9, 0.999, 1e-08, 0.01, 10

N_DEV = 8
LANES = 128
KR_PAD = LANES
PAD_ORDER = ("gate", "cq", "kr", "ckv", "sb", "qm", "zpad")
MM_TILE = 1024
MM_ROWS = 2048
MM_DEPTH = 4096
MM_VMEM = 40 * 1024 * 1024
PAD_SIZES = dict(gate=GATE_W, cq=Q_LORA, kr=KR_PAD, ckv=KV_LORA, sb=SB_W, qm=QM_W)
PAD_SIZES["zpad"] = (-sum(PAD_SIZES.values())) % MM_TILE
PAD_OFF = {}
_o = 0
for _n in PAD_ORDER:
    PAD_OFF[_n] = _o
    _o += PAD_SIZES[_n]
IN_PAD = _o
HEAD_PAD = LANES
MLA_PAD_W = MLA_HEADS * HEAD_PAD
VMEM_LIMIT = 48 * 1024 * 1024


def _cparams(sem):
    return pltpu.CompilerParams(dimension_semantics=sem, vmem_limit_bytes=VMEM_LIMIT)


def _pick(n, pref):
    if n <= pref:
        return n
    t = (pref // LANES) * LANES
    while t >= LANES:
        if n % t == 0:
            return t
        t -= LANES
    return n


def matmul(a, b, *, name, ta=False, tb=False, out_dtype=F32, acc=None, act=None, act_in=None,
           b_slots=None, out_slots=None, rows=None, comm=None, tm_pref=MM_ROWS, tn_pref=MM_TILE, tk_pref=MM_DEPTH):
    M, K = (a.shape[1], a.shape[0]) if ta else a.shape
    tm, tk = _pick(M, tm_pref), _pick(K, tk_pref)
    if b_slots == "n":
        assert not tb and b.shape[:2] == (N_DEV, K)
        tn = b.shape[2]
        N = N_DEV * tn
    elif b_slots == "k":
        assert tb and N_DEV * b.shape[2] == K
        N, tk = b.shape[1], b.shape[2]
        tn = _pick(N, tn_pref)
    else:
        N = b.shape[0] if tb else b.shape[1]
        assert (b.shape[1] if tb else b.shape[0]) == K
        tn = _pick(N, tn_pref)
    if out_slots == "m":
        tm = M // N_DEV
    elif out_slots == "n":
        tn = N // N_DEV
    assert out_slots is None or (acc is None and act_in is None)
    if rows is not None:
        assert acc is None and act is None and out_slots is None
        tn = N
        row_ins, vec_ins = list(rows["row_ins"]), list(rows["vec_ins"])
        row_outs, vec_outs = list(rows["row_outs"]), list(rows["vec_outs"])

    def block_bytes(tm_, tk_):
        ab = tm_ * tk_ * a.dtype.itemsize + tk_ * tn * b.dtype.itemsize
        per_out = jnp.dtype(out_dtype).itemsize + sum(t.dtype.itemsize for t in (acc, act_in) if t is not None)
        if rows is not None:
            per_out = sum(t.dtype.itemsize for t in row_ins) + sum(jnp.dtype(d).itemsize for d in row_outs)
        return 2 * (ab + tm_ * tn * per_out) + (tm_ * tn * 4 if K // tk_ > 1 else 0)

    while block_bytes(tm, tk) > MM_VMEM:
        if tm > 512 and out_slots != "m" and M % (tm // 2) == 0:
            tm //= 2
        elif tk > 512 and b_slots != "k" and K % (tk // 2) == 0:
            tk //= 2
        else:
            break
    nk = K // tk
    dims = (((0 if ta else 1,), (1 if tb else 0,)), ((), ()))
    assert act in (None, "relu2", "relu2_bwd") and (act == "relu2_bwd") == (act_in is not None)

    def body(*refs):
        a_ref, b_ref = refs[0], refs[1]
        pos = 2
        acc_ref = act_ref = None
        if acc is not None:
            acc_ref = refs[pos]
            pos += 1
        if act_in is not None:
            act_ref = refs[pos]
            pos += 1
        if rows is not None:
            n_ri, n_vi, n_ro, n_vo = len(row_ins), len(vec_ins), len(row_outs), len(vec_outs)
            ri_refs, vi_refs = refs[pos:pos + n_ri], refs[pos + n_ri:pos + n_ri + n_vi]
            pos += n_ri + n_vi
            ro_refs, vo_refs = refs[pos:pos + n_ro], refs[pos + n_ro:pos + n_ro + n_vo]
            pos += n_ro + n_vo - 1
            row_i = pl.program_id(0)
        out = refs[pos]
        scratch = refs[pos + 1:]

        part = lax.dot_general(a_ref[...].astype(BF16), b_ref[...].astype(BF16), dims,
                               preferred_element_type=F32)

        def finish(r):
            if rows is not None:
                ro, vo = rows["fn"](r, *[t[...] for t in ri_refs], *[t[...] for t in vi_refs])
                for ref, val in zip(ro_refs, ro):
                    ref[...] = val.astype(ref.dtype)
                for ref, val in zip(vo_refs, vo):
                    _acc_rows(ref, row_i, val)
                return
            if acc_ref is not None:
                r = r + acc_ref[...]
            if act == "relu2":
                r = jnp.square(jnp.maximum(r, 0.0))
            elif act == "relu2_bwd":
                r = r * (2.0 * jnp.sqrt(act_ref[...].astype(F32)))
            out[...] = r.astype(out.dtype)

        if nk == 1:
            finish(part)
        else:
            acc_sc = scratch[0]
            k = pl.program_id(2)

            @pl.when(k == 0)
            def _():
                acc_sc[...] = part

            @pl.when(k > 0)
            def _():
                acc_sc[...] += part

            @pl.when(k == nk - 1)
            def _():
                finish(acc_sc[...])

    a_spec = pl.BlockSpec((tk, tm), lambda i, j, k: (k, i)) if ta else pl.BlockSpec((tm, tk), lambda i, j, k: (i, k))
    if b_slots == "n":
        b_spec = pl.BlockSpec((None, tk, tn), lambda i, j, k: (j, k, 0))
    elif b_slots == "k":
        b_spec = pl.BlockSpec((None, tn, tk), lambda i, j, k: (k, j, 0))
    else:
        b_spec = pl.BlockSpec((tn, tk), lambda i, j, k: (j, k)) if tb else pl.BlockSpec((tk, tn), lambda i, j, k: (k, j))
    if out_slots == "m":
        o_spec = pl.BlockSpec((None, None, tm, tn), lambda i, j, k: (i % 2, i // 2, 0, j))
        o_shape = (2, N_CHIP, tm, N)
    elif out_slots == "n":
        o_spec = pl.BlockSpec((None, None, tm, tn), lambda i, j, k: (j % 2, j // 2, i, 0))
        o_shape = (2, N_CHIP, M, tn)
    else:
        o_spec = pl.BlockSpec((tm, tn), lambda i, j, k: (i, j))
        o_shape = (M, N)
    in_specs = [a_spec, b_spec]
    args = [a, b]
    if acc is not None:
        in_specs.append(o_spec)
        args.append(acc)
    if act_in is not None:
        in_specs.append(o_spec)
        args.append(act_in)
    out_specs, out_shapes, sem = [o_spec], [jax.ShapeDtypeStruct(o_shape, out_dtype)], ("parallel", "parallel", "arbitrary")
    if rows is not None:
        def vec_spec(w):
            return pl.BlockSpec((1, w), lambda i, j, k: (0, 0))

        in_specs += [o_spec] * len(row_ins) + [vec_spec(t.shape[1]) for t in vec_ins]
        args += row_ins + vec_ins
        out_specs = [o_spec] * len(row_outs) + [vec_spec(w) for w in vec_outs]
        out_shapes = ([jax.ShapeDtypeStruct((M, N), d) for d in row_outs]
                      + [jax.ShapeDtypeStruct((1, w), F32) for w in vec_outs])
        sem = ("arbitrary",) * 3
    outs, c_outs = _call(
        body, name=name, grid=(M // tm, N // tn, nk), in_specs=in_specs, out_specs=out_specs, out_shape=out_shapes,
        scratch_shapes=[pltpu.VMEM((tm, tn), F32)] if nk > 1 else [], args=args, sem=sem, comm=comm)
    if rows is not None:
        return (*outs, c_outs) if comm is not None else tuple(outs)
    return (outs[0], c_outs) if comm is not None else outs[0]


ROW_TILE = 256


def _rstd(xv):
    return lax.rsqrt(jnp.mean(xv * xv, axis=-1, keepdims=True) + EPS)


def _rms_bwd_rows(dy, xv, g):
    r = _rstd(xv)
    dyg = dy * g
    dx = r * dyg - xv * (r * r * r) * jnp.mean(dyg * xv, axis=-1, keepdims=True)
    return dx, dy * xv * r


def _row_spec(tm, n, col=0):
    return pl.BlockSpec((tm, n), lambda i: (i, col))


def _vec_spec(n):
    return pl.BlockSpec((1, n), lambda i: (0, 0))


def _acc_rows(ref, i, val):
    @pl.when(i == 0)
    def _():
        ref[...] = val

    @pl.when(i > 0)
    def _():
        ref[...] += val


def rms_fwd(x, g, *, name, n=None, col=0, out_dtype=BF16, comm=None):
    T = x.shape[0]
    n = x.shape[1] if n is None else n
    tm = _pick(T, ROW_TILE)

    def body(x_ref, g_ref, o_ref):
        xv = x_ref[...].astype(F32)
        o_ref[...] = (xv * _rstd(xv) * g_ref[...]).astype(o_ref.dtype)

    outs, c_outs = _call(
        body, name=name, grid=(T // tm,), in_specs=[_row_spec(tm, n, col), _vec_spec(n)],
        out_specs=[_row_spec(tm, n)], out_shape=[jax.ShapeDtypeStruct((T, n), out_dtype)],
        scratch_shapes=[], args=(x, g), sem=("parallel",), comm=comm)
    return (outs[0], c_outs) if comm is not None else outs[0]


def rms_bwd(dy, x, g, *, name, n=None, col=0, out_dtype=F32, comm=None):
    T = x.shape[0]
    n = x.shape[1] if n is None else n
    tm = _pick(T, ROW_TILE)

    def body(dy_ref, x_ref, g_ref, dx_ref, dg_ref):
        dx, dgr = _rms_bwd_rows(dy_ref[...].astype(F32), x_ref[...].astype(F32), g_ref[...])
        dx_ref[...] = dx.astype(dx_ref.dtype)
        _acc_rows(dg_ref, pl.program_id(0), jnp.sum(dgr, axis=0, keepdims=True))

    outs, c_outs = _call(
        body, name=name, grid=(T // tm,), in_specs=[_row_spec(tm, n), _row_spec(tm, n, col), _vec_spec(n)],
        out_specs=(_row_spec(tm, n), _vec_spec(n)),
        out_shape=(jax.ShapeDtypeStruct((T, n), out_dtype), jax.ShapeDtypeStruct((1, n), F32)),
        scratch_shapes=[], args=(dy, x, g), sem=("arbitrary",), comm=comm)
    return (*outs, c_outs) if comm is not None else tuple(outs)


def _mix_post_rows(y, x, g_post, g_pre2):
    x1 = x + y * _rstd(y) * g_post
    return [y, x1, x1 * _rstd(x1) * g_pre2], []


def _loss_rows(d, x1, tgt, g):
    err = x1 + d * _rstd(d) * g - tgt
    part = 0.5 * jnp.sum(jnp.mean(err * err, axis=-1, keepdims=True), axis=0, keepdims=True)
    dx2 = err * (1.0 / d.shape[-1])
    dd, dgr = _rms_bwd_rows(dx2, d, g)
    return [dx2, dd], [jnp.broadcast_to(part, (1, LANES)), jnp.sum(dgr, axis=0, keepdims=True)]


def _mlp_pre_bwd_rows(dh2, dx2, x1, y, g_pre2, g_post):
    d1, dg2 = _rms_bwd_rows(dh2, x1, g_pre2)
    dx1 = dx2 + d1
    dy, dgp = _rms_bwd_rows(dx1, y, g_post)
    return [dx1, dy], [jnp.sum(dg2, axis=0, keepdims=True), jnp.sum(dgp, axis=0, keepdims=True)]


def _mix_pre_bwd_rows(dh, x, dx1, g):
    dx, dgr = _rms_bwd_rows(dh, x, g)
    return [dx + dx1], [jnp.sum(dgr, axis=0, keepdims=True)]


def _gate_specs(tm):
    gcol = PAD_OFF["gate"] // D_MODEL
    lspecs = [pl.BlockSpec((tm, D_MODEL), functools.partial(lambda i, c: (i, c), c=gcol + b)) for b in range(N_BRANCH)]
    bspecs = [pl.BlockSpec((1, D_MODEL), functools.partial(lambda i, c: (0, c), c=b)) for b in range(N_BRANCH)]
    pspecs = [_row_spec(tm, D_MODEL) for _ in range(N_BRANCH)]
    return lspecs, bspecs, pspecs


def gate_merge(proj, b_gate, ps, *, name):
    T = proj.shape[0]
    tm = _pick(T, ROW_TILE)
    lspecs, bspecs, pspecs = _gate_specs(tm)

    def body(*refs):
        l_refs, b_refs, p_refs, o_ref = refs[0:3], refs[3:6], refs[6:9], refs[9]
        tot = None
        for lr, br, pr in zip(l_refs, b_refs, p_refs):
            term = jax.nn.sigmoid(lr[...].astype(F32) + br[...]) * pr[...].astype(F32)
            tot = term if tot is None else tot + term
        o_ref[...] = tot.astype(o_ref.dtype)

    return pl.pallas_call(
        body, name=name, grid=(T // tm,), in_specs=lspecs + bspecs + pspecs,
        out_specs=_row_spec(tm, D_MODEL), out_shape=jax.ShapeDtypeStruct((T, D_MODEL), BF16),
        compiler_params=_cparams(("parallel",)),
    )(proj, proj, proj, b_gate, b_gate, b_gate, *ps)


def gate_bwd(dmerged, proj, b_gate, ps, *, name):
    T = proj.shape[0]
    tm = _pick(T, ROW_TILE)
    lspecs, bspecs, pspecs = _gate_specs(tm)

    def body(*refs):
        dm_ref = refs[0]
        l_refs, b_refs, p_refs = refs[1:4], refs[4:7], refs[7:10]
        dl_ref, dp_refs, db_ref = refs[10], refs[11:14], refs[14]
        i = pl.program_id(0)
        dm = dm_ref[...]
        for b, (lr, br, pr, dpr) in enumerate(zip(l_refs, b_refs, p_refs, dp_refs)):
            gt = jax.nn.sigmoid(lr[...].astype(F32) + br[...])
            dpr[...] = (dm * gt).astype(dpr.dtype)
            dl = dm * pr[...].astype(F32) * gt * (1.0 - gt)
            dl_ref[:, b * D_MODEL:(b + 1) * D_MODEL] = dl.astype(dl_ref.dtype)
            part = jnp.sum(dl, axis=0, keepdims=True)

            @pl.when(i == 0)
            def _():
                db_ref[:, b * D_MODEL:(b + 1) * D_MODEL] = part

            @pl.when(i > 0)
            def _():
                db_ref[:, b * D_MODEL:(b + 1) * D_MODEL] += part

    return pl.pallas_call(
        body, name=name, grid=(T // tm,), in_specs=[_row_spec(tm, D_MODEL)] + lspecs + bspecs + pspecs,
        out_specs=(_row_spec(tm, GATE_W), *[_row_spec(tm, D_MODEL) for _ in range(N_BRANCH)], _vec_spec(GATE_W)),
        out_shape=(jax.ShapeDtypeStruct((T, GATE_W), BF16),
                   *[jax.ShapeDtypeStruct((T, D_MODEL), BF16) for _ in range(N_BRANCH)],
                   jax.ShapeDtypeStruct((1, GATE_W), F32)),
        compiler_params=_cparams(("arbitrary",)),
    )(dmerged, proj, proj, proj, b_gate, b_gate, b_gate, *ps)


def _rope_tables(pos_ref, invf_ref):
    ang = pos_ref[...] * invf_ref[...]
    lane = lax.broadcasted_iota(jnp.int32, (1, LANES), 1)
    s = jnp.sin(ang)
    split = MLA_NOPE + MLA_ROPE // 2
    return jnp.cos(ang), jnp.where(lane >= split, s, 0.0), jnp.where(lane < split, s, 0.0)


def _rotate(xh, c, s_hi, s_lo, sign):
    half = MLA_ROPE // 2
    up = pltpu.roll(xh, half, 1)
    down = pltpu.roll(xh, LANES - half, 1)
    return xh * c + sign * (up * s_hi - down * s_lo)


def rope_fwd(qpre, kpre, proj, pos, invf, *, name):
    T = qpre.shape[0]
    tm = _pick(T, ROW_TILE)
    kr_col = PAD_OFF["kr"] // KR_PAD

    def body(q_ref, k_ref, kr_ref, pos_ref, invf_ref, qo_ref, ko_ref):
        c, s_hi, s_lo = _rope_tables(pos_ref, invf_ref)
        kr = _rotate(pltpu.roll(kr_ref[...].astype(F32), MLA_NOPE, 1), c, s_hi, s_lo, 1.0)
        for h in range(MLA_HEADS):
            sl = slice(h * HEAD_PAD, (h + 1) * HEAD_PAD)
            qo_ref[:, sl] = _rotate(q_ref[:, sl], c, s_hi, s_lo, 1.0).astype(qo_ref.dtype)
            ko_ref[:, sl] = (k_ref[:, sl] + kr).astype(ko_ref.dtype)

    rs = _row_spec(tm, MLA_PAD_W)
    return pl.pallas_call(
        body, name=name, grid=(T // tm,),
        in_specs=[rs, rs, _row_spec(tm, KR_PAD, kr_col), _row_spec(tm, 1), _vec_spec(LANES)],
        out_specs=(rs, rs),
        out_shape=(jax.ShapeDtypeStruct((T, MLA_PAD_W), BF16), jax.ShapeDtypeStruct((T, MLA_PAD_W), BF16)),
        compiler_params=_cparams(("parallel",)),
    )(qpre, kpre, proj, pos, invf)


def rope_bwd(dq_pad, dk_pad, pos, invf, *, name):
    T = dq_pad.shape[0]
    tm = _pick(T, ROW_TILE)

    def body(dq_ref, dk_ref, pos_ref, invf_ref, dqo_ref, dkr_ref):
        c, s_hi, s_lo = _rope_tables(pos_ref, invf_ref)
        tot = None
        for h in range(MLA_HEADS):
            sl = slice(h * HEAD_PAD, (h + 1) * HEAD_PAD)
            dqo_ref[:, sl] = _rotate(dq_ref[:, sl], c, s_hi, s_lo, -1.0).astype(dqo_ref.dtype)
            tot = dk_ref[:, sl] if tot is None else tot + dk_ref[:, sl]
        dkr = pltpu.roll(_rotate(tot, c, s_hi, s_lo, -1.0), LANES - MLA_NOPE, 1)
        lane = lax.broadcasted_iota(jnp.int32, (1, LANES), 1)
        dkr_ref[...] = jnp.where(lane < MLA_ROPE, dkr, 0.0).astype(dkr_ref.dtype)

    rs = _row_spec(tm, MLA_PAD_W)
    return pl.pallas_call(
        body, name=name, grid=(T // tm,),
        in_specs=[rs, rs, _row_spec(tm, 1), _vec_spec(LANES)],
        out_specs=(rs, _row_spec(tm, KR_PAD)),
        out_shape=(jax.ShapeDtypeStruct((T, MLA_PAD_W), BF16), jax.ShapeDtypeStruct((T, KR_PAD), BF16)),
        compiler_params=_cparams(("parallel",)),
    )(dq_pad, dk_pad, pos, invf)


ATT_TILE = 512
NEG = -1e30


def _split_bf16(v):
    hi = v.astype(BF16)
    return hi, (v - hi.astype(F32)).astype(BF16)


def _mul(x, s):
    return x if s == 1.0 else x * s


def _dot(a, b, ca, cb):
    return lax.dot_general(a, b, (((ca,), (cb,)), ((), ())), preferred_element_type=F32)


CUM_CHUNK = 256


def _tri(kind, terms=2):
    r = lax.broadcasted_iota(jnp.int32, (CUM_CHUNK, CUM_CHUNK), 0)
    c = lax.broadcasted_iota(jnp.int32, (CUM_CHUNK, CUM_CHUNK), 1)
    m = {"gt": r > c, "le": r <= c, "lt": r < c}[kind]
    t = jnp.where(m, 1.0, 0.0).astype(BF16)
    return jnp.concatenate([t] * terms, axis=0)


def _cum(v, tri, kind):
    n = v.shape[1] // CUM_CHUNK
    two = tri.shape[0] == 2 * CUM_CHUNK
    chunks = [v[:, c * CUM_CHUNK:(c + 1) * CUM_CHUNK] for c in range(n)]
    totals = [jnp.sum(ch, axis=1, keepdims=True) for ch in chunks] if n > 1 else None
    outs = []
    for c, ch in enumerate(chunks):
        r = _dot(jnp.concatenate(_split_bf16(ch), axis=1) if two else ch.astype(BF16), tri, 1, 0)
        others = [] if n == 1 else totals[c + 1:] if kind == "gt" else totals[:c]
        for t in others:
            r = r + t
        outs.append(r)
    return outs[0] if n == 1 else jnp.concatenate(outs, axis=1)


def _lane_masks(hp, w):
    lane = lax.broadcasted_iota(jnp.int32, (1, LANES), 1)
    return [(lane >= e * w) & (lane < (e + 1) * w) for e in range(hp)]


def _att_dims(mode, hp, qk_w, v_w):
    assert mode in ("softmax", "sb")
    assert (hp, qk_w, v_w) in ((2, 128, 64), (2, 64, 64), (1, 128, 128))
    qw = hp * LANES if qk_w == LANES else LANES
    return qw


def attn_fwd(q, k, v, *, name, mode, causal, hp, qk_w, v_w, scale, nh, qc, kc, vc, comm=None):
    qw = _att_dims(mode, hp, qk_w, v_w)
    B, S = q.shape[0], q.shape[1]
    Sk = k.shape[1]
    tq = _pick(S, ATT_TILE)
    tk = _pick(Sk, ATT_TILE)
    if causal:
        assert tq == tk and S == Sk
    nq, nkv = S // tq, Sk // tk
    pre_scaled = math.frexp(scale)[0] == 0.5
    post = 1.0 if pre_scaled else scale

    def body(q_ref, k_ref, v_ref, o_ref, st_ref):
        i = pl.program_id(2)
        qv = q_ref[0].astype(BF16)
        masks_qk = _lane_masks(hp, qk_w) if qk_w < LANES else None
        masks_v = _lane_masks(hp, v_w) if v_w < LANES else None
        if qk_w == LANES:
            qs = [qv[:, e * LANES:(e + 1) * LANES] for e in range(hp)]
        else:
            qs = [jnp.where(masks_qk[e], qv, jnp.zeros_like(qv)) for e in range(hp)]
        if pre_scaled:
            qs = [t * scale for t in qs]
        rows = lax.broadcasted_iota(jnp.int32, (tq, tk), 0)
        cols = lax.broadcasted_iota(jnp.int32, (tq, tk), 1)
        tri = _tri("gt") if mode == "sb" else None

        def kv_tile(j):
            off = pl.multiple_of(j * tk, tk)
            kj = k_ref[0, pl.ds(off, tk), :].astype(BF16)
            vj = v_ref[0, pl.ds(off, tk), :].astype(BF16)
            ks = [kj[:, e * LANES:(e + 1) * LANES] for e in range(hp)] if qk_w == LANES else [kj] * hp
            return ks, vj

        def merge(vals):
            if hp == 1:
                return jnp.broadcast_to(vals[0], (tq, LANES))
            return jnp.where(masks_v[0], vals[0], vals[1])

        if mode == "softmax":
            def block(j, carry, diag):
                ms, ls, acc = carry
                ks, vj = kv_tile(j)
                new_m, new_l, alphas, pvs = [], [], [], []
                for e in range(hp):
                    s = _mul(_dot(qs[e], ks[e], 1, 1), post)
                    if diag:
                        s = jnp.where(rows >= cols, s, NEG)
                    m_new = jnp.maximum(ms[e], jnp.max(s, axis=1, keepdims=True))
                    alpha = jnp.exp(ms[e] - m_new)
                    p = jnp.exp(s - m_new)
                    new_l.append(alpha * ls[e] + jnp.sum(p, axis=1, keepdims=True))
                    new_m.append(m_new)
                    alphas.append(alpha)
                    pvs.append(_dot(p.astype(BF16), vj, 1, 0))
                acc = acc * merge(alphas) + merge(pvs)
                return tuple(new_m), tuple(new_l), acc

            init = (tuple(jnp.full((tq, 1), NEG, F32) for _ in range(hp)),
                    tuple(jnp.zeros((tq, 1), F32) for _ in range(hp)),
                    jnp.zeros((tq, LANES), F32))
            if causal:
                carry = lax.fori_loop(0, i, lambda j, c: block(j, c, False), init)
                ms, ls, acc = block(i, carry, True)
            else:
                ms, ls, acc = lax.fori_loop(0, nkv, lambda j, c: block(j, c, False), init)
            o_ref[0] = acc / merge(list(ls))
            st_ref[0, 0] = merge([m + jnp.log(l) for m, l in zip(ms, ls)])
        else:
            def block(j, carry, diag):
                cs_, acc = carry
                ks, vj = kv_tile(j)
                new_c, pvs = [], []
                for e in range(hp):
                    z = _mul(_dot(qs[e], ks[e], 1, 1), post)
                    lk = -jnp.maximum(z, 0.0) - jnp.log(1.0 + jnp.exp(-jnp.abs(z)))
                    lz = lk + z
                    if diag:
                        lk = jnp.where(rows > cols, lk, 0.0)
                    a = jnp.exp(lz + _cum(lk, tri, "gt") + cs_[e])
                    if diag:
                        a = jnp.where(rows > cols, a, 0.0)
                    pvs.append(_dot(a.astype(BF16), vj, 1, 0))
                    new_c.append(cs_[e] + jnp.sum(lk, axis=1, keepdims=True))
                return tuple(new_c), acc + merge(pvs)

            init = (tuple(jnp.zeros((tq, 1), F32) for _ in range(hp)), jnp.zeros((tq, LANES), F32))
            carry = block(i, init, True)
            cs_, acc = lax.fori_loop(0, i, lambda jj, c: block(i - 1 - jj, c, False), carry)
            o_ref[0] = acc
            st_ref[0, 0] = merge(list(cs_))

    outs, c_outs = _call(
        body, name=name, grid=(B, nh, nq),
        in_specs=[pl.BlockSpec((1, tq, qw), lambda b, h, i: (b, i, qc + h)),
                  pl.BlockSpec((1, Sk, qw), lambda b, h, i: (b, 0, kc + h)),
                  pl.BlockSpec((1, Sk, LANES), lambda b, h, i: (b, 0, vc + h))],
        out_specs=(pl.BlockSpec((1, tq, LANES), lambda b, h, i: (b, i, h)),
                   pl.BlockSpec((1, 1, tq, LANES), lambda b, h, i: (b, h, i, 0))),
        out_shape=(jax.ShapeDtypeStruct((B, S, nh * LANES), F32), jax.ShapeDtypeStruct((B, nh, S, LANES), F32)),
        scratch_shapes=[], args=(q, k, v), sem=("parallel", "parallel", "arbitrary"), comm=comm)
    return (*outs, c_outs) if comm is not None else tuple(outs)


def attn_bwd(q, k, v, o, st, do, *, name, mode, causal, hp, qk_w, v_w, scale, nh, qc, kc, vc, comm=None):
    qw = _att_dims(mode, hp, qk_w, v_w)
    B, S = q.shape[0], q.shape[1]
    Sk = k.shape[1]
    tq = _pick(S, ATT_TILE)
    tk = _pick(Sk, ATT_TILE)
    if causal:
        assert tq == tk and S == Sk
    nq, nkv = S // tq, Sk // tk
    pre_scaled = math.frexp(scale)[0] == 0.5
    post = 1.0 if pre_scaled else scale

    def body(q_ref, k_ref, v_ref, o_ref, st_ref, do_ref, dq_ref, dk_ref, dv_ref):
        i = pl.program_id(2)

        @pl.when(i == 0)
        def _():
            dk_ref[...] = jnp.zeros_like(dk_ref)
            dv_ref[...] = jnp.zeros_like(dv_ref)

        qv = q_ref[0].astype(BF16)
        dov = do_ref[0]
        stv = st_ref[0, 0]
        masks_qk = _lane_masks(hp, qk_w) if qk_w < LANES else None
        masks_v = _lane_masks(hp, v_w) if v_w < LANES else None
        if qk_w == LANES:
            qs = [qv[:, e * LANES:(e + 1) * LANES] for e in range(hp)]
        else:
            qs = [jnp.where(masks_qk[e], qv, jnp.zeros_like(qv)) for e in range(hp)]
        if pre_scaled:
            qs = [t * scale for t in qs]
        if hp == 1:
            dos = [dov.astype(BF16)]
            stats = [stv[:, 0:1]]
        else:
            dos = [jnp.where(masks_v[e], dov, 0.0).astype(BF16) for e in range(hp)]
            stats = [stv[:, e * v_w:e * v_w + 1] for e in range(hp)]
        if mode == "softmax":
            prod = dov * o_ref[0]
            if hp == 1:
                dsum = [jnp.sum(prod, axis=1, keepdims=True)]
            else:
                dsum = [jnp.sum(jnp.where(masks_v[e], prod, 0.0), axis=1, keepdims=True) for e in range(hp)]
        rows = lax.broadcasted_iota(jnp.int32, (tq, tk), 0)
        cols = lax.broadcasted_iota(jnp.int32, (tq, tk), 1)
        if mode == "sb":
            tri_le, tri_lt = _tri("le"), _tri("lt", terms=1)

        def kv_tile(j):
            off = pl.multiple_of(j * tk, tk)
            kj = k_ref[0, pl.ds(off, tk), :].astype(BF16)
            vj = v_ref[0, pl.ds(off, tk), :].astype(BF16)
            ks = [kj[:, e * LANES:(e + 1) * LANES] for e in range(hp)] if qk_w == LANES else [kj] * hp
            return off, ks, vj

        def scatter(off, dz_list, p_list):
            dvj = None
            for e in range(hp):
                t = _dot(p_list[e], dos[e], 0, 0)
                dvj = t if dvj is None else dvj + t
            dv_ref[0, pl.ds(off, tk), :] += dvj
            if qk_w == LANES:
                for e in range(hp):
                    dk_ref[0, pl.ds(off, tk), e * LANES:(e + 1) * LANES] += _dot(dz_list[e], qs[e], 0, 0)
            else:
                dkj = None
                for e in range(hp):
                    t = _dot(dz_list[e], qs[e], 0, 0)
                    dkj = t if dkj is None else dkj + t
                dk_ref[0, pl.ds(off, tk), :] += dkj

        def dq_add(dqs, dz_list, ks):
            out = []
            for e in range(hp):
                out.append(dqs[e] + _dot(dz_list[e], ks[e], 1, 0))
            return tuple(out)

        dq0 = tuple(jnp.zeros((tq, LANES), F32) for _ in range(hp))

        if mode == "softmax":
            def block(j, dqs, diag):
                off, ks, vj = kv_tile(j)
                dzs, ps = [], []
                for e in range(hp):
                    s = _mul(_dot(qs[e], ks[e], 1, 1), post)
                    p = jnp.exp(s - stats[e])
                    if diag:
                        p = jnp.where(rows >= cols, p, 0.0)
                    dp = _dot(dos[e], vj, 1, 1)
                    dzs.append(_mul(p * (dp - dsum[e]), post).astype(BF16))
                    ps.append(p.astype(BF16))
                scatter(off, dzs, ps)
                return dq_add(dqs, dzs, ks)

            if causal:
                dqs = lax.fori_loop(0, i, lambda j, c: block(j, c, False), dq0)
                dqs = block(i, dqs, True)
            else:
                dqs = lax.fori_loop(0, nkv, lambda j, c: block(j, c, False), dq0)
        else:
            def block(j, carry, diag):
                dqs, cps, cgs = carry
                off, ks, vj = kv_tile(j)
                dzs, ps, new_p, new_g = [], [], [], []
                for e in range(hp):
                    z = _mul(_dot(qs[e], ks[e], 1, 1), post)
                    lk = -jnp.maximum(z, 0.0) - jnp.log(1.0 + jnp.exp(-jnp.abs(z)))
                    lz = lk + z
                    sig = jnp.exp(lz)
                    keep = 1.0 - sig
                    if diag:
                        lk = jnp.where(rows > cols, lk, 0.0)
                    keep_after = stats[e] - cps[e] - _cum(lk, tri_le, "le")
                    a = jnp.exp(lz + keep_after)
                    if diag:
                        a = jnp.where(rows > cols, a, 0.0)
                    g = _dot(dos[e], vj, 1, 1) * a
                    gsum = cgs[e] + _cum(g, tri_lt, "lt")
                    dz = _mul(g * keep - gsum * sig, post)
                    if diag:
                        dz = jnp.where(rows > cols, dz, 0.0)
                    dzs.append(dz.astype(BF16))
                    ps.append(a.astype(BF16))
                    new_p.append(cps[e] + jnp.sum(lk, axis=1, keepdims=True))
                    new_g.append(cgs[e] + jnp.sum(g, axis=1, keepdims=True))
                scatter(off, dzs, ps)
                return dq_add(dqs, dzs, ks), tuple(new_p), tuple(new_g)

            zc = tuple(jnp.zeros((tq, 1), F32) for _ in range(hp))
            carry = lax.fori_loop(0, i, lambda j, c: block(j, c, False), (dq0, zc, zc))
            dqs, _, _ = block(i, carry, True)

        if pre_scaled:
            dqs = [t * scale for t in dqs]
        if qk_w == LANES:
            for e in range(hp):
                dq_ref[0, :, e * LANES:(e + 1) * LANES] = dqs[e]
        else:
            dq_ref[0] = jnp.where(masks_qk[0], dqs[0], dqs[1])

    outs, c_outs = _call(
        body, name=name, grid=(B, nh, nq),
        in_specs=[pl.BlockSpec((1, tq, qw), lambda b, h, i: (b, i, qc + h)),
                  pl.BlockSpec((1, Sk, qw), lambda b, h, i: (b, 0, kc + h)),
                  pl.BlockSpec((1, Sk, LANES), lambda b, h, i: (b, 0, vc + h)),
                  pl.BlockSpec((1, tq, LANES), lambda b, h, i: (b, i, h)),
                  pl.BlockSpec((1, 1, tq, LANES), lambda b, h, i: (b, h, i, 0)),
                  pl.BlockSpec((1, tq, LANES), lambda b, h, i: (b, i, h))],
        out_specs=(pl.BlockSpec((1, tq, qw), lambda b, h, i: (b, i, h)),
                   pl.BlockSpec((1, Sk, qw), lambda b, h, i: (b, 0, h)),
                   pl.BlockSpec((1, Sk, LANES), lambda b, h, i: (b, 0, h))),
        out_shape=(jax.ShapeDtypeStruct((B, S, nh * qw), F32), jax.ShapeDtypeStruct((B, Sk, nh * qw), F32),
                   jax.ShapeDtypeStruct((B, Sk, nh * LANES), F32)),
        scratch_shapes=[], args=(q, k, v, o, st, do), sem=("parallel", "parallel", "arbitrary"), comm=comm)
    return (*outs, c_outs) if comm is not None else tuple(outs)


def _pos(p):
    return 4 * p[0] + 2 * p[1] + p[2]


N_CHIP = 4
MESH_ID = pl.DeviceIdType.MESH
_ANY = pl.BlockSpec(memory_space=pl.ANY)


def _me():
    return lax.axis_index("x"), lax.axis_index("y"), lax.axis_index("c")


def _other_chips(x, y):
    return [(1 - x, y), (x, 1 - y), (1 - x, 1 - y)]


class _Comm:
    def __init__(self, arrs, out_shapes, n_sem, n_local, start, finish):
        self.arrs, self.out_shapes, self.start, self.finish = list(arrs), list(out_shapes), start, finish
        self.scratch = [pltpu.SemaphoreType.DMA((n_sem,)), pltpu.SemaphoreType.DMA((n_sem,)),
                        pltpu.SemaphoreType.DMA((max(n_local, 1),))]


def _run_comm(comm, name):
    n = len(comm.arrs)

    def body(*refs):
        r = (refs[:n], refs[n:2 * n], refs[2 * n], refs[2 * n + 1], refs[2 * n + 2])
        comm.start(*r)
        comm.finish(*r)

    return pl.pallas_call(
        body, name=name, in_specs=[_ANY] * n, out_specs=[_ANY] * n, out_shape=comm.out_shapes,
        scratch_shapes=comm.scratch, compiler_params=pltpu.CompilerParams(has_side_effects=True),
    )(*comm.arrs)


def _call(body, *, name, grid, in_specs, out_specs, out_shape, scratch_shapes, args, sem, comm=None):
    in_specs, out_specs, out_shape = list(in_specs), list(out_specs), list(out_shape)
    if comm is None:
        res = pl.pallas_call(body, name=name, grid=grid, in_specs=in_specs, out_specs=out_specs, out_shape=out_shape,
                             scratch_shapes=scratch_shapes, compiler_params=_cparams(sem))(*args)
        return list(res), []
    n_in, n_out, n_scr, nc = len(in_specs), len(out_specs), len(scratch_shapes), len(comm.arrs)

    def wrapped(*refs):
        ins, refs = refs[:n_in], refs[n_in:]
        c_in, refs = refs[:nc], refs[nc:]
        outs, refs = refs[:n_out], refs[n_out:]
        c_out, refs = refs[:nc], refs[nc:]
        scr, sems = refs[:n_scr], refs[n_scr:]
        ids = [pl.program_id(a) for a in range(len(grid))]
        first = functools.reduce(jnp.logical_and, [i == 0 for i in ids])
        last = functools.reduce(jnp.logical_and, [i == g - 1 for i, g in zip(ids, grid)])

        @pl.when(first)
        def _():
            comm.start(c_in, c_out, *sems)

        body(*ins, *outs, *scr)

        @pl.when(last)
        def _():
            comm.finish(c_in, c_out, *sems)

    res = pl.pallas_call(
        wrapped, name=name, grid=grid, in_specs=in_specs + [_ANY] * nc, out_specs=out_specs + [_ANY] * nc,
        out_shape=out_shape + comm.out_shapes, scratch_shapes=list(scratch_shapes) + comm.scratch,
        compiler_params=pltpu.CompilerParams(dimension_semantics=("arbitrary",) * len(grid),
                                             vmem_limit_bytes=VMEM_LIMIT, has_side_effects=True),
    )(*args, *comm.arrs)
    return list(res[:n_out]), list(res[n_out:])


def gather_two_level(bufs, *, name=None):
    n = len(bufs)

    def parts(x_refs, out_refs, send_sems, recv_sems, local_sems):
        x, y, c = _me()
        me, sibling = (x, y, c), (x, y, 1 - c)
        chips = _other_chips(x, y)

        def copy(a, k, block, to, from_input=False):
            dst = out_refs[a].at[_pos(block)]
            return pltpu.make_async_remote_copy(src_ref=x_refs[a] if from_input else dst, dst_ref=dst,
                                                send_sem=send_sems.at[7 * a + k], recv_sem=recv_sems.at[7 * a + k],
                                                device_id=to, device_id_type=MESH_ID)

        mine = [pltpu.make_async_copy(x_refs[a], out_refs[a].at[_pos(me)], local_sems.at[a]) for a in range(n)]
        first = []
        for a in range(n):
            first.append(copy(a, 0, me, sibling, from_input=True))
            first += [copy(a, 1 + j, me, (*chip, c), from_input=True) for j, chip in enumerate(chips)]
        return copy, mine, first, me, sibling, chips, c

    def start(*refs):
        _, mine, first, *_ = parts(*refs)
        for cp in mine + first:
            cp.start()

    def finish(*refs):
        copy, mine, first, me, sibling, chips, c = parts(*refs)
        passed = []
        for j, chip in enumerate(chips):
            for a in range(n):
                copy(a, 1 + j, (*chip, c), me).wait_recv()
                passed.append(copy(a, 4 + j, (*chip, c), sibling))
                passed[-1].start()
        for a in range(n):
            copy(a, 0, sibling, me).wait_recv()
            for j, chip in enumerate(chips):
                copy(a, 4 + j, (*chip, 1 - c), me).wait_recv()
        for cp in first + passed:
            cp.wait_send()
        for cp in mine:
            cp.wait()

    out_shapes = [jax.ShapeDtypeStruct((N_DEV,) + b.shape, b.dtype) for b in bufs]
    comm = _Comm(bufs, out_shapes, 7 * n, n, start, finish)
    return _run_comm(comm, name) if name else comm


def sibling_exchange(sends, *, name=None):
    n = len(sends)

    def copies(s_refs, out_refs, send_sems, recv_sems, local_sems):
        x, y, c = _me()
        return [pltpu.make_async_remote_copy(src_ref=s_refs[a].at[1 - c], dst_ref=out_refs[a], send_sem=send_sems.at[a],
                                             recv_sem=recv_sems.at[a], device_id=(x, y, 1 - c), device_id_type=MESH_ID)
                for a in range(n)]

    def start(*refs):
        for cp in copies(*refs):
            cp.start()

    def finish(*refs):
        for cp in copies(*refs):
            cp.wait()

    out_shapes = [jax.ShapeDtypeStruct(s.shape[1:], s.dtype) for s in sends]
    comm = _Comm(sends, out_shapes, n, 0, start, finish)
    return _run_comm(comm, name) if name else comm


def _flat2(shape):
    return math.prod(shape[:-1]), shape[-1]


def _row_tile(r):
    return r if r <= 512 else _pick8(r, 256)


def _pick8(n, pref):
    t = pref
    while n % t:
        t -= 8
    return t


def chip_sum(send, got, core, *, name):
    shape = got.shape[1:]
    r, cdim = _flat2(shape)
    tr = _row_tile(r)
    send = send.reshape(2, N_CHIP, r, cdim)
    got = got.reshape(N_CHIP, r, cdim)

    def body(core_ref, s_ref, g_ref, o_ref):
        o_ref[...] = (s_ref[0].astype(F32) + g_ref[...].astype(F32)).astype(o_ref.dtype)

    blk = pl.BlockSpec((N_CHIP, tr, cdim), lambda i, core_ref: (0, i, 0))
    out = pl.pallas_call(
        body, name=name,
        grid_spec=pltpu.PrefetchScalarGridSpec(
            num_scalar_prefetch=1, grid=(r // tr,),
            in_specs=[pl.BlockSpec((1, N_CHIP, tr, cdim), lambda i, core_ref: (core_ref[0], 0, i, 0)), blk],
            out_specs=blk),
        out_shape=jax.ShapeDtypeStruct((N_CHIP, r, cdim), BF16),
        compiler_params=_cparams(("parallel",)),
    )(core, send, got)
    return out.reshape((N_CHIP,) + shape)


def chip_exchange(sums, *, name=None):
    n = len(sums)

    def copies(s_refs, out_refs, send_sems, recv_sems, local_sems):
        x, y, c = _me()
        mine = 2 * x + y
        chips = _other_chips(x, y)
        local = [pltpu.make_async_copy(s_refs[a].at[mine], out_refs[a].at[mine], local_sems.at[a]) for a in range(n)]
        sends, recvs = [], []
        for a in range(n):
            for j, (px, py) in enumerate(chips):
                sems = dict(send_sem=send_sems.at[3 * a + j], recv_sem=recv_sems.at[3 * a + j], device_id=(px, py, c),
                            device_id_type=MESH_ID)
                sends.append(pltpu.make_async_remote_copy(src_ref=s_refs[a].at[2 * px + py], dst_ref=out_refs[a].at[mine], **sems))
                recvs.append(pltpu.make_async_remote_copy(src_ref=s_refs[a].at[mine], dst_ref=out_refs[a].at[2 * px + py], **sems))
        return local, sends, recvs

    def start(*refs):
        local, sends, _ = copies(*refs)
        for cp in local + sends:
            cp.start()

    def finish(*refs):
        local, sends, recvs = copies(*refs)
        for cp in recvs:
            cp.wait_recv()
        for cp in sends:
            cp.wait_send()
        for cp in local:
            cp.wait()

    out_shapes = [jax.ShapeDtypeStruct(s.shape, s.dtype) for s in sums]
    comm = _Comm(sums, out_shapes, 3 * n, n, start, finish)
    return _run_comm(comm, name) if name else comm


def sum_adamw(parts, w, m, v, *, name):
    shape = w.shape
    R, cdim = _flat2(shape)
    n_slots = parts.shape[0]
    tr = _row_tile(R)
    parts = parts.reshape(n_slots, R, cdim)
    w, m, v = (t.reshape(1, R, cdim) for t in (w, m, v))
    c1 = 1.0 - ADAM_B1 ** ADAM_STEP
    c2 = 1.0 - ADAM_B2 ** ADAM_STEP

    def body(p_ref, w_ref, m_ref, v_ref, g_ref, d_ref, mo_ref, vo_ref):
        g = p_ref[0].astype(F32)
        for s in range(1, n_slots):
            g = g + p_ref[s].astype(F32)
        mn = ADAM_B1 * m_ref[...] + (1.0 - ADAM_B1) * g
        vn = ADAM_B2 * v_ref[...] + (1.0 - ADAM_B2) * jnp.square(g)
        g_ref[...] = g
        mo_ref[...] = mn
        vo_ref[...] = vn
        d_ref[...] = -ADAM_LR * ((mn / c1) / (jnp.sqrt(vn / c2) + ADAM_EPS) + ADAM_WD * w_ref[...])

    rs = pl.BlockSpec((None, tr, cdim), lambda i: (0, i, 0))
    sd = jax.ShapeDtypeStruct((1, R, cdim), F32)
    res = pl.pallas_call(
        body, name=name, grid=(R // tr,),
        in_specs=[pl.BlockSpec((n_slots, tr, cdim), lambda i: (0, i, 0)), rs, rs, rs],
        out_specs=(rs, rs, rs, rs), out_shape=(sd, sd, sd, sd),
        compiler_params=_cparams(("parallel",)),
    )(parts, w, m, v)
    return [t.reshape(shape) for t in res]


WEIGHTS = ("ln_mix_pre", "w_in", "b_gate", "q_norm", "w_uq", "kv_norm", "w_uk", "w_uv", "mem_norm", "w_mem_kv",
           "w_branch_out", "w_out", "ln_mix_post", "ln_mlp_pre", "w_mlp_up", "w_mlp_down", "ln_mlp_post")
BIG = dict(w_in=((D_MODEL, IN_WIDTH), 1), w_uq=((Q_LORA, MLA_HEADS * (MLA_NOPE + MLA_ROPE)), 1),
           w_uk=((KV_LORA, MLA_HEADS * MLA_NOPE), 1), w_uv=((KV_LORA, MLA_HEADS * MLA_V), 1),
           w_mem_kv=((D_MODEL, 2 * QM_W), 0), w_branch_out=((N_BRANCH, BRANCH_W, D_MODEL), 2),
           w_out=((D_MODEL, D_MODEL), 0), w_mlp_up=((D_MODEL, D_FF), 1), w_mlp_down=((D_FF, D_MODEL), 0))
SMALL = tuple(n for n in WEIGHTS if n not in BIG)


def _shard_shape(name):
    full, ax = BIG[name]
    return tuple(s // N_DEV if a == ax else s for a, s in enumerate(full))


PACK_TILE = 512


def _pad_rows(a):
    r = a.shape[-2]
    to = PACK_TILE if r > PACK_TILE else 8
    pad = [(0, 0)] * a.ndim
    pad[-2] = (0, (-r) % to)
    return jnp.pad(a, pad)


def _pack_rows(arrs):
    return _pad_rows(jnp.concatenate([a.reshape(-1, LANES) for a in arrs], axis=0))


def _unpack_rows(packed, shapes, lead=()):
    out, r = [], 0
    for shp in shapes:
        n = math.prod(shp) // LANES
        out.append(packed[..., r:r + n, :].reshape(lead + tuple(shp)))
        r += n
    return out


def _slots_to_full(name, slots):
    full, ax = BIG[name]
    return jnp.moveaxis(slots, 0, ax).reshape(full)


def _full_to_slots(name, w):
    full, ax = BIG[name]
    split = full[:ax] + (N_DEV, full[ax] // N_DEV) + full[ax + 1:]
    return jnp.moveaxis(w.reshape(split), ax, 0)


def _full_to_owner(name, w):
    s = _full_to_slots(name, w)
    return jnp.swapaxes(s.reshape((N_CHIP, 2) + s.shape[1:]), 0, 1)


def _split_in(w):
    offs = np.cumsum((0,) + IN_SIZES)
    names = ("cq", "ckv", "kr", "sb", "qm", "gate")
    return {n: w[:, offs[i]:offs[i + 1]] for i, n in enumerate(names)}


def _pad_in(w):
    p = _split_in(w)
    p["kr"] = jnp.pad(p["kr"], ((0, 0), (0, KR_PAD - MLA_ROPE)))
    p["zpad"] = jnp.zeros((w.shape[0], PAD_SIZES["zpad"]), w.dtype)
    return jnp.concatenate([p[n] for n in PAD_ORDER], axis=1)


def _unpad_in(wp):
    p = {n: wp[:, PAD_OFF[n]:PAD_OFF[n] + PAD_SIZES[n]] for n in PAD_ORDER}
    p["kr"] = p["kr"][:, :MLA_ROPE]
    return jnp.concatenate([p[n] for n in ("cq", "ckv", "kr", "sb", "qm", "gate")], axis=1)


def _in_segments():
    orig = dict(zip(("cq", "ckv", "kr", "sb", "qm", "gate"), zip(np.cumsum((0,) + IN_SIZES[:-1]).tolist(), IN_SIZES)))
    shard = IN_WIDTH // N_DEV
    segs = []
    for n in PAD_ORDER:
        if n not in orig:
            continue
        o0, w = orig[n]
        c = o0
        while c < o0 + w:
            s = c // shard
            e = min(o0 + w, (s + 1) * shard)
            segs.append((s, c - s * shard, PAD_OFF[n] + (c - o0), e - c))
            c = e
    return segs


def in_slots_to_pad(slots, *, name):
    _, R, shard = slots.shape
    tr = _pick8(R, ROW_TILE)
    segs = _in_segments()

    def body(in_ref, out_ref):
        out_ref[...] = jnp.zeros_like(out_ref)
        for s, a, d, w in segs:
            out_ref[:, d:d + w] = in_ref[s, :, a:a + w]

    return pl.pallas_call(
        body, name=name, grid=(R // tr,), in_specs=[pl.BlockSpec((N_DEV, tr, shard), lambda i: (0, i, 0))],
        out_specs=pl.BlockSpec((tr, IN_PAD), lambda i: (i, 0)), out_shape=jax.ShapeDtypeStruct((R, IN_PAD), slots.dtype),
        compiler_params=_cparams(("parallel",)),
    )(slots)


def in_pad_to_owner(g, *, name):
    R = g.shape[0]
    shard = IN_WIDTH // N_DEV
    tr = _pick8(R, ROW_TILE)
    segs = _in_segments()

    def body(in_ref, out_ref):
        for s, a, d, w in segs:
            out_ref[s % 2, s // 2, :, a:a + w] = in_ref[:, d:d + w]

    return pl.pallas_call(
        body, name=name, grid=(R // tr,), in_specs=[pl.BlockSpec((tr, IN_PAD), lambda i: (i, 0))],
        out_specs=pl.BlockSpec((2, N_CHIP, tr, shard), lambda i: (0, 0, i, 0)),
        out_shape=jax.ShapeDtypeStruct((2, N_CHIP, R, shard), g.dtype),
        compiler_params=_cparams(("parallel",)),
    )(g)


def _pad_heads(w, width):
    r = w.shape[0]
    return jnp.pad(w.reshape(r, MLA_HEADS, width), ((0, 0), (0, 0), (0, HEAD_PAD - width))).reshape(r, MLA_PAD_W)


def _unpad_heads(wp, width):
    r = wp.shape[0]
    return wp.reshape(r, MLA_HEADS, HEAD_PAD)[:, :, :width].reshape(r, MLA_HEADS * width)


def _rope_inv_freq():
    half = MLA_ROPE // 2
    inv = 1.0 / (ROPE_THETA ** (jnp.arange(half, dtype=F32) * (2.0 / MLA_ROPE)))
    tab = jnp.zeros((LANES,), F32)
    tab = tab.at[MLA_NOPE:MLA_NOPE + half].set(inv).at[MLA_NOPE + half:MLA_NOPE + MLA_ROPE].set(inv)
    return tab.reshape(1, LANES)


def _as_tuple(r):
    return r if isinstance(r, tuple) else (r,)


class _Exchange:
    def __init__(self, w):
        self.w = w
        self.rest = tuple(n for n in BIG if n != "w_in")
        self.pending = {}
        self.reduced = {}

    def first_comm(self):
        return gather_two_level([self.w["w_in"].astype(BF16)])

    def first_weights(self, slots):
        return in_slots_to_pad(slots[0], name="w_in_layout")

    def rest_comm(self):
        return gather_two_level([self.w[n].astype(BF16) for n in self.rest])

    def rest_weights(self, slots):
        return {n: s if n == "w_mlp_up" else _slots_to_full(n, s) for n, s in zip(self.rest, slots)}

    def sibling_comm(self, tag, grads):
        self.pending[tag] = (tuple(grads), [grads[n] for n in grads])
        return sibling_exchange(self.pending[tag][1])

    def chip_comm(self, tag, gots):
        names, sends = self.pending[tag]
        core = lax.axis_index("c").astype(jnp.int32).reshape(1)
        sums = [chip_sum(s, g, core, name=f"chip_sum_{n}") for n, s, g in zip(names, sends, gots)]
        self.pending[tag] = names
        return chip_exchange(sums)

    def reduce_start(self, tag, grads):
        return self.chip_comm(tag, _run_comm(self.sibling_comm(tag, grads), f"sibling_{tag}"))

    def reduce_done(self, tag, recvs):
        self.reduced.update(zip(self.pending.pop(tag), recvs))


def _local_step(x, mem, positions, tgt, sm, ex):
    B, S, D = x.shape
    M = mem.shape[1]
    T = B * S
    x2 = x.reshape(T, D)
    mem2 = mem.reshape(B * M, D)
    pos = positions.reshape(T, 1).astype(F32)
    invf = _rope_inv_freq()
    cq_col, ckv_col = PAD_OFF["cq"] // Q_LORA, PAD_OFF["ckv"] // KV_LORA
    sb_col, qm_col = PAD_OFF["sb"] // LANES, PAD_OFF["qm"] // LANES
    sb_blk = SB_HEADS * SB_DIM // LANES
    mla = dict(mode="softmax", causal=True, hp=2, qk_w=128, v_w=64, scale=(MLA_NOPE + MLA_ROPE) ** -0.5,
               nh=MLA_HEADS // 2, qc=0, kc=0, vc=0)
    sbk = dict(mode="sb", causal=True, hp=2, qk_w=64, v_w=64, scale=SB_DIM ** -0.5, nh=SB_HEADS // 2,
               qc=sb_col, kc=sb_col + sb_blk, vc=sb_col + 2 * sb_blk)
    mca = dict(mode="softmax", causal=False, hp=1, qk_w=128, v_w=128, scale=MEM_DIM ** -0.5, nh=MEM_HEADS,
               qc=qm_col, kc=0, vc=MEM_HEADS)

    h, *slots = _as_tuple(rms_fwd(x2, sm["ln_mix_pre"], name="rms_mix_pre", comm=ex.first_comm()))
    w_in_pad = ex.first_weights(slots[0] if slots else None)
    proj = matmul(h, w_in_pad, out_dtype=BF16, name="mm_in")
    proj3 = proj.reshape(B, S, IN_PAD)
    o_sb, st_sb, *slots = attn_fwd(proj3, proj3, proj3, name="sb_fwd", comm=ex.rest_comm(), **sbk)
    W = ex.rest_weights(slots[0] if slots else None)
    w_uq_pad = _pad_heads(W["w_uq"], MLA_NOPE + MLA_ROPE)
    w_uk_pad = _pad_heads(W["w_uk"], MLA_NOPE)
    cqn = rms_fwd(proj, sm["q_norm"], n=Q_LORA, col=cq_col, name="rms_q")
    ckvn = rms_fwd(proj, sm["kv_norm"], n=KV_LORA, col=ckv_col, name="rms_kv")
    qpre = matmul(cqn, w_uq_pad, name="mm_uq")
    kpre = matmul(ckvn, w_uk_pad, name="mm_uk")
    v_mla = matmul(ckvn, W["w_uv"], out_dtype=BF16, name="mm_uv").reshape(B, S, -1)
    q_pad, k_pad = rope_fwd(qpre, kpre, proj, pos, invf, name="rope_fwd")
    q_pad, k_pad = q_pad.reshape(B, S, -1), k_pad.reshape(B, S, -1)
    o_mla, st_mla = attn_fwd(q_pad, k_pad, v_mla, name="mla_fwd", **mla)
    memh = rms_fwd(mem2, sm["mem_norm"], name="rms_mem")
    mkv = matmul(memh, W["w_mem_kv"], out_dtype=BF16, name="mm_memkv").reshape(B, M, -1)
    o_mem, st_mem = attn_fwd(proj3, mkv, mkv, name="mem_fwd", **mca)
    outs = [o.reshape(T, BRANCH_W) for o in (o_mla, o_sb, o_mem)]
    ps = [matmul(o, W["w_branch_out"][b], out_dtype=BF16, name=f"mm_bo{b}") for b, o in enumerate(outs)]
    merged = gate_merge(proj, sm["b_gate"], ps, name="gate_merge")
    y, x1, h2 = matmul(merged, W["w_out"], name="mm_out_norms",
                       rows=dict(fn=_mix_post_rows, row_ins=[x2], vec_ins=[sm["ln_mix_post"], sm["ln_mlp_pre"]],
                                 row_outs=[F32, F32, BF16], vec_outs=[]))
    u = matmul(h2, W["w_mlp_up"], b_slots="n", act="relu2", out_dtype=BF16, name="mm_up")
    dx2, dd, loss_p, dg_mlp_post = matmul(
        u, W["w_mlp_down"], name="mm_down_loss",
        rows=dict(fn=_loss_rows, row_ins=[x1, tgt.reshape(T, D)], vec_ins=[sm["ln_mlp_post"]], row_outs=[F32, BF16],
                  vec_outs=[LANES, D]))

    da = matmul(dd, W["w_mlp_down"], tb=True, act="relu2_bwd", act_in=u, out_dtype=BF16, name="mm_down_dx")
    g_down = matmul(u, dd, ta=True, out_dtype=BF16, out_slots="m", name="mm_down_dw")
    g_up = matmul(h2, da, ta=True, out_dtype=BF16, out_slots="n", name="mm_up_dw")
    sib = ex.sibling_comm("mlp", dict(w_mlp_down=g_down, w_mlp_up=g_up))
    dx1, dy, dg_mlp_pre, dg_mix_post, *got = matmul(
        da, _slots_to_full("w_mlp_up", W["w_mlp_up"]), tb=True, name="mm_up_dx_norms", comm=sib,
        rows=dict(fn=_mlp_pre_bwd_rows, row_ins=[dx2, x1, y], vec_ins=[sm["ln_mlp_pre"], sm["ln_mix_post"]],
                  row_outs=[F32, BF16], vec_outs=[D, D]))
    red_mlp = ex.chip_comm("mlp", got[0] if got else None)
    dmerged = matmul(dy, W["w_out"], tb=True, name="mm_out_dx")
    g_out = matmul(merged, dy, ta=True, out_dtype=BF16, name="mm_out_dw")
    dlog, dp0, dp1, dp2, db_gate = gate_bwd(dmerged, proj, sm["b_gate"], ps, name="gate_bwd")
    dps = (dp0, dp1, dp2)
    g_bo = jnp.stack([matmul(outs[b], dps[b], ta=True, out_dtype=BF16, name=f"mm_bo{b}_dw") for b in range(N_BRANCH)])
    dos = [matmul(dps[b], W["w_branch_out"][b], tb=True, name=f"mm_bo{b}_dx").reshape(B, S, BRANCH_W)
           for b in range(N_BRANCH)]
    dq_pad, dk_pad, dv_mla, *got = attn_bwd(q_pad, k_pad, v_mla, o_mla, st_mla, dos[0], name="mla_bwd", comm=red_mlp, **mla)
    ex.reduce_done("mlp", got[0] if got else None)
    dqm, dmk, dmv = attn_bwd(proj3, mkv, mkv, o_mem, st_mem, dos[2], name="mem_bwd", **mca)
    dmkv = jnp.concatenate([dmk, dmv], axis=-1).astype(BF16).reshape(B * M, -1)
    dmemh = matmul(dmkv, W["w_mem_kv"], tb=True, name="mm_memkv_dx")
    g_memkv = matmul(memh, dmkv, ta=True, out_dtype=BF16, name="mm_memkv_dw")
    _, dg_mem_norm = rms_bwd(dmemh, mem2, sm["mem_norm"], name="rms_mem_bwd")
    dq_pad, dk_pad, dv_mla = dq_pad.reshape(T, -1), dk_pad.reshape(T, -1), dv_mla.reshape(T, -1)
    dqpre, dkr = rope_bwd(dq_pad, dk_pad, pos, invf, name="rope_bwd")
    dcqn = matmul(dqpre, w_uq_pad, tb=True, name="mm_uq_dx")
    g_uq = _unpad_heads(matmul(cqn, dqpre, ta=True, out_dtype=BF16, name="mm_uq_dw"), MLA_NOPE + MLA_ROPE)
    dckvn = matmul(dk_pad, w_uk_pad, tb=True, name="mm_uk_dx")
    dckvn = matmul(dv_mla, W["w_uv"], tb=True, acc=dckvn, name="mm_uv_dx")
    g_uk = _unpad_heads(matmul(ckvn, dk_pad, ta=True, out_dtype=BF16, name="mm_uk_dw"), MLA_NOPE)
    g_uv = matmul(ckvn, dv_mla, ta=True, out_dtype=BF16, name="mm_uv_dw")
    dcq, dg_q_norm = rms_bwd(dcqn, proj, sm["q_norm"], n=Q_LORA, col=cq_col, out_dtype=BF16, name="rms_q_bwd")
    mix = dict(w_out=g_out, w_branch_out=g_bo, w_uq=g_uq, w_uk=g_uk, w_uv=g_uv, w_mem_kv=g_memkv)
    sib = ex.sibling_comm("mix", {n: _full_to_owner(n, g) for n, g in mix.items()})
    dckv, dg_kv_norm, *got = rms_bwd(dckvn, proj, sm["kv_norm"], n=KV_LORA, col=ckv_col, out_dtype=BF16,
                                     name="rms_kv_bwd", comm=sib)
    red_mix = ex.chip_comm("mix", got[0] if got else None)
    dsq, dsk, dsv, *got = attn_bwd(proj3, proj3, proj3, o_sb, st_sb, dos[1], name="sb_bwd", comm=red_mix, **sbk)
    ex.reduce_done("mix", got[0] if got else None)
    pieces = dict(gate=dlog, sb=jnp.concatenate([dsq, dsk, dsv], axis=-1).reshape(T, -1), qm=dqm.reshape(T, -1),
                  ckv=dckv, cq=dcq, kr=dkr, zpad=jnp.zeros((T, PAD_SIZES["zpad"]), BF16))
    dproj = jnp.concatenate([pieces[n].astype(BF16) for n in PAD_ORDER], axis=1)
    g_in = in_pad_to_owner(matmul(h, dproj, ta=True, out_dtype=BF16, name="mm_in_dw"), name="g_in_layout")
    red_in = ex.reduce_start("in", dict(w_in=g_in))
    dx, dg_mix_pre, *got = matmul(
        dproj, w_in_pad, tb=True, comm=red_in, name="mm_in_dx_norm",
        rows=dict(fn=_mix_pre_bwd_rows, row_ins=[x2, dx1], vec_ins=[sm["ln_mix_pre"]], row_outs=[F32], vec_outs=[D]))
    ex.reduce_done("in", got[0] if got else None)

    small = dict(ln_mix_pre=dg_mix_pre, b_gate=db_gate, q_norm=dg_q_norm, kv_norm=dg_kv_norm, mem_norm=dg_mem_norm,
                 ln_mix_post=dg_mix_post, ln_mlp_pre=dg_mlp_pre, ln_mlp_post=dg_mlp_post)
    return loss_p, dx.reshape(B, S, D), small


def kernel(x, mem, positions, ln_mix_pre, w_in, b_gate, q_norm, w_uq, kv_norm, w_uk, w_uv, mem_norm, w_mem_kv, w_branch_out, w_out, ln_mix_post, ln_mlp_pre, w_mlp_up, w_mlp_down, ln_mlp_post, loss_target, m_ln_mix_pre, m_w_in, m_b_gate, m_q_norm, m_w_uq, m_kv_norm, m_w_uk, m_w_uv, m_mem_norm, m_w_mem_kv, m_w_branch_out, m_w_out, m_ln_mix_post, m_ln_mlp_pre, m_w_mlp_up, m_w_mlp_down, m_ln_mlp_post, v_ln_mix_pre, v_w_in, v_b_gate, v_q_norm, v_w_uq, v_kv_norm, v_w_uk, v_w_uv, v_mem_norm, v_w_mem_kv, v_w_branch_out, v_w_out, v_ln_mix_post, v_ln_mlp_pre, v_w_mlp_up, v_w_mlp_down, v_ln_mlp_post):
    given = dict(locals())
    w = {n: given[n][0] for n in WEIGHTS}
    m = {n: given["m_" + n][0] for n in WEIGHTS}
    v = {n: given["v_" + n][0] for n in WEIGHTS}
    sm = {n: w[n].reshape(1, -1) for n in SMALL}
    ex = _Exchange(w)
    loss_row, grad_x, g_small = _local_step(x, mem, positions, loss_target, sm, ex)

    res = {n: sum_adamw(ex.reduced[n], given[n], given["m_" + n], given["v_" + n], name=f"adamw_{n}") for n in BIG}

    small_shapes = [w[n].shape for n in SMALL] + [(LANES,)]
    parts = gather_two_level([_pack_rows([g_small[n] for n in SMALL] + [loss_row])], name="gather_small_grads")[0]
    no_row = jnp.zeros((1, LANES), F32)
    res_small = sum_adamw(parts, *[_pack_rows([t[n] for n in SMALL] + [no_row]) for t in (w, m, v)], name="adamw_replicated")
    res_small = [_unpack_rows(r, small_shapes) for r in res_small]
    for i, n in enumerate(SMALL):
        res[n] = [r[i][None] for r in res_small]
    loss = res_small[0][len(SMALL)][0]

    out = [loss, grad_x]
    for k in range(4):
        out += [res[n][k] for n in WEIGHTS]
    return tuple(out)
```

```python
import functools
import math

import numpy as np
import jax
import jax.numpy as jnp
from jax import lax
from jax.experimental import pallas as pl
from jax.experimental.pallas import tpu as pltpu

F32 = jnp.float32
BF16 = jnp.bfloat16

D_MODEL = 1024
MEM_HEADS, MEM_DIM = 4, 128
MLA_HEADS, MLA_NOPE, MLA_ROPE, MLA_V = 8, 64, 32, 64
Q_LORA, KV_LORA = 384, 256
ROPE_THETA = 10000.0
SB_HEADS, SB_DIM = 8, 64
D_FF = 4 * D_MODEL
N_BRANCH, BRANCH_W = 3, 512
EPS = 1e-6
SB_W = 3 * SB_HEADS * SB_DIM
QM_W = MEM_HEADS * MEM_DIM
GATE_W = N_BRANCH * D_MODEL
IN_SIZES = (Q_LORA, KV_LORA, MLA_ROPE, SB_W, QM_W, GATE_W)
IN_WIDTH = sum(IN_SIZES)
ADAM_LR, ADAM_B1, ADAM_B2, ADAM_EPS, ADAM_WD, ADAM_STEP = 0.001, 0.9, 0.999, 1e-08, 0.01, 10

N_DEV = 8
LANES = 128
KR_PAD = LANES
PAD_ORDER = ("gate", "cq", "kr", "ckv", "sb", "qm", "zpad")
MM_TILE = 1024
MM_ROWS = 2048
MM_DEPTH = 4096
MM_VMEM = 40 * 1024 * 1024
PAD_SIZES = dict(gate=GATE_W, cq=Q_LORA, kr=KR_PAD, ckv=KV_LORA, sb=SB_W, qm=QM_W)
PAD_SIZES["zpad"] = (-sum(PAD_SIZES.values())) % MM_TILE
PAD_OFF = {}
_o = 0
for _n in PAD_ORDER:
    PAD_OFF[_n] = _o
    _o += PAD_SIZES[_n]
IN_PAD = _o
HEAD_PAD = LANES
MLA_PAD_W = MLA_HEADS * HEAD_PAD
VMEM_LIMIT = 48 * 1024 * 1024


def _cparams(sem):
    return pltpu.CompilerParams(dimension_semantics=sem, vmem_limit_bytes=VMEM_LIMIT)


def _pick(n, pref):
    if n <= pref:
        return n
    t = (pref // LANES) * LANES
    while t >= LANES:
        if n % t == 0:
            return t
        t -= LANES
    return n


def matmul(a, b, *, name, ta=False, tb=False, out_dtype=F32, acc=None, act=None, act_in=None,
           b_slots=None, out_slots=None, rows=None, comm=None, tm_pref=MM_ROWS, tn_pref=MM_TILE, tk_pref=MM_DEPTH):
    M, K = (a.shape[1], a.shape[0]) if ta else a.shape
    tm, tk = _pick(M, tm_pref), _pick(K, tk_pref)
    if b_slots == "n":
        assert not tb and b.shape[:2] == (N_DEV, K)
        tn = b.shape[2]
        N = N_DEV * tn
    elif b_slots == "k":
        assert tb and N_DEV * b.shape[2] == K
        N, tk = b.shape[1], b.shape[2]
        tn = _pick(N, tn_pref)
    else:
        N = b.shape[0] if tb else b.shape[1]
        assert (b.shape[1] if tb else b.shape[0]) == K
        tn = _pick(N, tn_pref)
    if out_slots == "m":
        tm = M // N_DEV
    elif out_slots == "n":
        tn = N // N_DEV
    assert out_slots is None or (acc is None and act_in is None)
    if rows is not None:
        assert acc is None and act is None and out_slots is None
        tn = N
        row_ins, vec_ins = list(rows["row_ins"]), list(rows["vec_ins"])
        row_outs, vec_outs = list(rows["row_outs"]), list(rows["vec_outs"])

    def block_bytes(tm_, tk_):
        ab = tm_ * tk_ * a.dtype.itemsize + tk_ * tn * b.dtype.itemsize
        per_out = jnp.dtype(out_dtype).itemsize + sum(t.dtype.itemsize for t in (acc, act_in) if t is not None)
        if rows is not None:
            per_out = sum(t.dtype.itemsize for t in row_ins) + sum(jnp.dtype(d).itemsize for d in row_outs)
        return 2 * (ab + tm_ * tn * per_out) + (tm_ * tn * 4 if K // tk_ > 1 else 0)

    while block_bytes(tm, tk) > MM_VMEM:
        if tm > 512 and out_slots != "m" and M % (tm // 2) == 0:
            tm //= 2
        elif tk > 512 and b_slots != "k" and K % (tk // 2) == 0:
            tk //= 2
        else:
            break
    nk = K // tk
    dims = (((0 if ta else 1,), (1 if tb else 0,)), ((), ()))
    assert act in (None, "relu2", "relu2_bwd") and (act == "relu2_bwd") == (act_in is not None)

    def body(*refs):
        a_ref, b_ref = refs[0], refs[1]
        pos = 2
        acc_ref = act_ref = None
        if acc is not None:
            acc_ref = refs[pos]
            pos += 1
        if act_in is not None:
            act_ref = refs[pos]
            pos += 1
        if rows is not None:
            n_ri, n_vi, n_ro, n_vo = len(row_ins), len(vec_ins), len(row_outs), len(vec_outs)
            ri_refs, vi_refs = refs[pos:pos + n_ri], refs[pos + n_ri:pos + n_ri + n_vi]
            pos += n_ri + n_vi
            ro_refs, vo_refs = refs[pos:pos + n_ro], refs[pos + n_ro:pos + n_ro + n_vo]
            pos += n_ro + n_vo - 1
            row_i = pl.program_id(0)
        out = refs[pos]
        scratch = refs[pos + 1:]

        part = lax.dot_general(a_ref[...].astype(BF16), b_ref[...].astype(BF16), dims,
                               preferred_element_type=F32)

        def finish(r):
            if rows is not None:
                ro, vo = rows["fn"](r, *[t[...] for t in ri_refs], *[t[...] for t in vi_refs])
                for ref, val in zip(ro_refs, ro):
                    ref[...] = val.astype(ref.dtype)
                for ref, val in zip(vo_refs, vo):
                    _acc_rows(ref, row_i, val)
                return
            if acc_ref is not None:
                r = r + acc_ref[...]
            if act == "relu2":
                r = jnp.square(jnp.maximum(r, 0.0))
            elif act == "relu2_bwd":
                r = r * (2.0 * jnp.sqrt(act_ref[...].astype(F32)))
            out[...] = r.astype(out.dtype)

        if nk == 1:
            finish(part)
        else:
            acc_sc = scratch[0]
            k = pl.program_id(2)

            @pl.when(k == 0)
            def _():
                acc_sc[...] = part

            @pl.when(k > 0)
            def _():
                acc_sc[...] += part

            @pl.when(k == nk - 1)
            def _():
                finish(acc_sc[...])

    a_spec = pl.BlockSpec((tk, tm), lambda i, j, k: (k, i)) if ta else pl.BlockSpec((tm, tk), lambda i, j, k: (i, k))
    if b_slots == "n":
        b_spec = pl.BlockSpec((None, tk, tn), lambda i, j, k: (j, k, 0))
    elif b_slots == "k":
        b_spec = pl.BlockSpec((None, tn, tk), lambda i, j, k: (k, j, 0))
    else:
        b_spec = pl.BlockSpec((tn, tk), lambda i, j, k: (j, k)) if tb else pl.BlockSpec((tk, tn), lambda i, j, k: (k, j))
    if out_slots == "m":
        o_spec = pl.BlockSpec((None, None, tm, tn), lambda i, j, k: (i % 2, i // 2, 0, j))
        o_shape = (2, N_CHIP, tm, N)
    elif out_slots == "n":
        o_spec = pl.BlockSpec((None, None, tm, tn), lambda i, j, k: (j % 2, j // 2, i, 0))
        o_shape = (2, N_CHIP, M, tn)
    else:
        o_spec = pl.BlockSpec((tm, tn), lambda i, j, k: (i, j))
        o_shape = (M, N)
    in_specs = [a_spec, b_spec]
    args = [a, b]
    if acc is not None:
        in_specs.append(o_spec)
        args.append(acc)
    if act_in is not None:
        in_specs.append(o_spec)
        args.append(act_in)
    out_specs, out_shapes, sem = [o_spec], [jax.ShapeDtypeStruct(o_shape, out_dtype)], ("parallel", "parallel", "arbitrary")
    if rows is not None:
        def vec_spec(w):
            return pl.BlockSpec((1, w), lambda i, j, k: (0, 0))

        in_specs += [o_spec] * len(row_ins) + [vec_spec(t.shape[1]) for t in vec_ins]
        args += row_ins + vec_ins
        out_specs = [o_spec] * len(row_outs) + [vec_spec(w) for w in vec_outs]
        out_shapes = ([jax.ShapeDtypeStruct((M, N), d) for d in row_outs]
                      + [jax.ShapeDtypeStruct((1, w), F32) for w in vec_outs])
        sem = ("arbitrary",) * 3
    outs, c_outs = _call(
        body, name=name, grid=(M // tm, N // tn, nk), in_specs=in_specs, out_specs=out_specs, out_shape=out_shapes,
        scratch_shapes=[pltpu.VMEM((tm, tn), F32)] if nk > 1 else [], args=args, sem=sem, comm=comm)
    if rows is not None:
        return (*outs, c_outs) if comm is not None else tuple(outs)
    return (outs[0], c_outs) if comm is not None else outs[0]


ROW_TILE = 256


def _rstd(xv):
    return lax.rsqrt(jnp.mean(xv * xv, axis=-1, keepdims=True) + EPS)


def _rms_bwd_rows(dy, xv, g):
    r = _rstd(xv)
    dyg = dy * g
    dx = r * dyg - xv * (r * r * r) * jnp.mean(dyg * xv, axis=-1, keepdims=True)
    return dx, dy * xv * r


def _row_spec(tm, n, col=0):
    return pl.BlockSpec((tm, n), lambda i: (i, col))


def _vec_spec(n):
    return pl.BlockSpec((1, n), lambda i: (0, 0))


def _acc_rows(ref, i, val):
    @pl.when(i == 0)
    def _():
        ref[...] = val

    @pl.when(i > 0)
    def _():
        ref[...] += val


def rms_fwd(x, g, *, name, n=None, col=0, out_dtype=BF16, comm=None):
    T = x.shape[0]
    n = x.shape[1] if n is None else n
    tm = _pick(T, ROW_TILE)

    def body(x_ref, g_ref, o_ref):
        xv = x_ref[...].astype(F32)
        o_ref[...] = (xv * _rstd(xv) * g_ref[...]).astype(o_ref.dtype)

    outs, c_outs = _call(
        body, name=name, grid=(T // tm,), in_specs=[_row_spec(tm, n, col), _vec_spec(n)],
        out_specs=[_row_spec(tm, n)], out_shape=[jax.ShapeDtypeStruct((T, n), out_dtype)],
        scratch_shapes=[], args=(x, g), sem=("parallel",), comm=comm)
    return (outs[0], c_outs) if comm is not None else outs[0]


def rms_bwd(dy, x, g, *, name, n=None, col=0, out_dtype=F32, comm=None):
    T = x.shape[0]
    n = x.shape[1] if n is None else n
    tm = _pick(T, ROW_TILE)

    def body(dy_ref, x_ref, g_ref, dx_ref, dg_ref):
        dx, dgr = _rms_bwd_rows(dy_ref[...].astype(F32), x_ref[...].astype(F32), g_ref[...])
        dx_ref[...] = dx.astype(dx_ref.dtype)
        _acc_rows(dg_ref, pl.program_id(0), jnp.sum(dgr, axis=0, keepdims=True))

    outs, c_outs = _call(
        body, name=name, grid=(T // tm,), in_specs=[_row_spec(tm, n), _row_spec(tm, n, col), _vec_spec(n)],
        out_specs=(_row_spec(tm, n), _vec_spec(n)),
        out_shape=(jax.ShapeDtypeStruct((T, n), out_dtype), jax.ShapeDtypeStruct((1, n), F32)),
        scratch_shapes=[], args=(dy, x, g), sem=("arbitrary",), comm=comm)
    return (*outs, c_outs) if comm is not None else tuple(outs)


def _mix_post_rows(y, x, g_post, g_pre2):
    x1 = x + y * _rstd(y) * g_post
    return [y, x1, x1 * _rstd(x1) * g_pre2], []


def _loss_rows(d, x1, tgt, g):
    err = x1 + d * _rstd(d) * g - tgt
    part = 0.5 * jnp.sum(jnp.mean(err * err, axis=-1, keepdims=True), axis=0, keepdims=True)
    dx2 = err * (1.0 / d.shape[-1])
    dd, dgr = _rms_bwd_rows(dx2, d, g)
    return [dx2, dd], [jnp.broadcast_to(part, (1, LANES)), jnp.sum(dgr, axis=0, keepdims=True)]


def _mlp_pre_bwd_rows(dh2, dx2, x1, y, g_pre2, g_post):
    d1, dg2 = _rms_bwd_rows(dh2, x1, g_pre2)
    dx1 = dx2 + d1
    dy, dgp = _rms_bwd_rows(dx1, y, g_post)
    return [dx1, dy], [jnp.sum(dg2, axis=0, keepdims=True), jnp.sum(dgp, axis=0, keepdims=True)]


def _mix_pre_bwd_rows(dh, x, dx1, g):
    dx, dgr = _rms_bwd_rows(dh, x, g)
    return [dx + dx1], [jnp.sum(dgr, axis=0, keepdims=True)]


def _gate_specs(tm):
    gcol = PAD_OFF["gate"] // D_MODEL
    lspecs = [pl.BlockSpec((tm, D_MODEL), functools.partial(lambda i, c: (i, c), c=gcol + b)) for b in range(N_BRANCH)]
    bspecs = [pl.BlockSpec((1, D_MODEL), functools.partial(lambda i, c: (0, c), c=b)) for b in range(N_BRANCH)]
    pspecs = [_row_spec(tm, D_MODEL) for _ in range(N_BRANCH)]
    return lspecs, bspecs, pspecs


def gate_merge(proj, b_gate, ps, *, name):
    T = proj.shape[0]
    tm = _pick(T, ROW_TILE)
    lspecs, bspecs, pspecs = _gate_specs(tm)

    def body(*refs):
        l_refs, b_refs, p_refs, o_ref = refs[0:3], refs[3:6], refs[6:9], refs[9]
        tot = None
        for lr, br, pr in zip(l_refs, b_refs, p_refs):
            term = jax.nn.sigmoid(lr[...].astype(F32) + br[...]) * pr[...].astype(F32)
            tot = term if tot is None else tot + term
        o_ref[...] = tot.astype(o_ref.dtype)

    return pl.pallas_call(
        body, name=name, grid=(T // tm,), in_specs=lspecs + bspecs + pspecs,
        out_specs=_row_spec(tm, D_MODEL), out_shape=jax.ShapeDtypeStruct((T, D_MODEL), BF16),
        compiler_params=_cparams(("parallel",)),
    )(proj, proj, proj, b_gate, b_gate, b_gate, *ps)


def gate_bwd(dmerged, proj, b_gate, ps, *, name):
    T = proj.shape[0]
    tm = _pick(T, ROW_TILE)
    lspecs, bspecs, pspecs = _gate_specs(tm)

    def body(*refs):
        dm_ref = refs[0]
        l_refs, b_refs, p_refs = refs[1:4], refs[4:7], refs[7:10]
        dl_ref, dp_refs, db_ref = refs[10], refs[11:14], refs[14]
        i = pl.program_id(0)
        dm = dm_ref[...]
        for b, (lr, br, pr, dpr) in enumerate(zip(l_refs, b_refs, p_refs, dp_refs)):
            gt = jax.nn.sigmoid(lr[...].astype(F32) + br[...])
            dpr[...] = (dm * gt).astype(dpr.dtype)
            dl = dm * pr[...].astype(F32) * gt * (1.0 - gt)
            dl_ref[:, b * D_MODEL:(b + 1) * D_MODEL] = dl.astype(dl_ref.dtype)
            part = jnp.sum(dl, axis=0, keepdims=True)

            @pl.when(i == 0)
            def _():
                db_ref[:, b * D_MODEL:(b + 1) * D_MODEL] = part

            @pl.when(i > 0)
            def _():
                db_ref[:, b * D_MODEL:(b + 1) * D_MODEL] += part

    return pl.pallas_call(
        body, name=name, grid=(T // tm,), in_specs=[_row_spec(tm, D_MODEL)] + lspecs + bspecs + pspecs,
        out_specs=(_row_spec(tm, GATE_W), *[_row_spec(tm, D_MODEL) for _ in range(N_BRANCH)], _vec_spec(GATE_W)),
        out_shape=(jax.ShapeDtypeStruct((T, GATE_W), BF16),
                   *[jax.ShapeDtypeStruct((T, D_MODEL), BF16) for _ in range(N_BRANCH)],
                   jax.ShapeDtypeStruct((1, GATE_W), F32)),
        compiler_params=_cparams(("arbitrary",)),
    )(dmerged, proj, proj, proj, b_gate, b_gate, b_gate, *ps)


def _rope_tables(pos_ref, invf_ref):
    ang = pos_ref[...] * invf_ref[...]
    lane = lax.broadcasted_iota(jnp.int32, (1, LANES), 1)
    s = jnp.sin(ang)
    split = MLA_NOPE + MLA_ROPE // 2
    return jnp.cos(ang), jnp.where(lane >= split, s, 0.0), jnp.where(lane < split, s, 0.0)


def _rotate(xh, c, s_hi, s_lo, sign):
    half = MLA_ROPE // 2
    up = pltpu.roll(xh, half, 1)
    down = pltpu.roll(xh, LANES - half, 1)
    return xh * c + sign * (up * s_hi - down * s_lo)


def rope_fwd(qpre, kpre, proj, pos, invf, *, name):
    T = qpre.shape[0]
    tm = _pick(T, ROW_TILE)
    kr_col = PAD_OFF["kr"] // KR_PAD

    def body(q_ref, k_ref, kr_ref, pos_ref, invf_ref, qo_ref, ko_ref):
        c, s_hi, s_lo = _rope_tables(pos_ref, invf_ref)
        kr = _rotate(pltpu.roll(kr_ref[...].astype(F32), MLA_NOPE, 1), c, s_hi, s_lo, 1.0)
        for h in range(MLA_HEADS):
            sl = slice(h * HEAD_PAD, (h + 1) * HEAD_PAD)
            qo_ref[:, sl] = _rotate(q_ref[:, sl].astype(F32), c, s_hi, s_lo, 1.0).astype(qo_ref.dtype)
            ko_ref[:, sl] = (k_ref[:, sl].astype(F32) + kr).astype(ko_ref.dtype)

    rs = _row_spec(tm, MLA_PAD_W)
    return pl.pallas_call(
        body, name=name, grid=(T // tm,),
        in_specs=[rs, rs, _row_spec(tm, KR_PAD, kr_col), _row_spec(tm, 1), _vec_spec(LANES)],
        out_specs=(rs, rs),
        out_shape=(jax.ShapeDtypeStruct((T, MLA_PAD_W), BF16), jax.ShapeDtypeStruct((T, MLA_PAD_W), BF16)),
        compiler_params=_cparams(("parallel",)),
    )(qpre, kpre, proj, pos, invf)


def rope_bwd(dq_pad, dk_pad, pos, invf, *, name):
    T = dq_pad.shape[0]
    tm = _pick(T, ROW_TILE)

    def body(dq_ref, dk_ref, pos_ref, invf_ref, dqo_ref, dkr_ref):
        c, s_hi, s_lo = _rope_tables(pos_ref, invf_ref)
        tot = None
        for h in range(MLA_HEADS):
            sl = slice(h * HEAD_PAD, (h + 1) * HEAD_PAD)
            dqo_ref[:, sl] = _rotate(dq_ref[:, sl].astype(F32), c, s_hi, s_lo, -1.0).astype(dqo_ref.dtype)
            tot = dk_ref[:, sl] if tot is None else tot + dk_ref[:, sl]
        dkr = pltpu.roll(_rotate(tot, c, s_hi, s_lo, -1.0), LANES - MLA_NOPE, 1)
        lane = lax.broadcasted_iota(jnp.int32, (1, LANES), 1)
        dkr_ref[...] = jnp.where(lane < MLA_ROPE, dkr, 0.0).astype(dkr_ref.dtype)

    rs = _row_spec(tm, MLA_PAD_W)
    return pl.pallas_call(
        body, name=name, grid=(T // tm,),
        in_specs=[rs, rs, _row_spec(tm, 1), _vec_spec(LANES)],
        out_specs=(rs, _row_spec(tm, KR_PAD)),
        out_shape=(jax.ShapeDtypeStruct((T, MLA_PAD_W), BF16), jax.ShapeDtypeStruct((T, KR_PAD), BF16)),
        compiler_params=_cparams(("parallel",)),
    )(dq_pad, dk_pad, pos, invf)


ATT_TILE = 512
NEG = -1e30


def _split_bf16(v):
    hi = v.astype(BF16)
    return hi, (v - hi.astype(F32)).astype(BF16)


def _mul(x, s):
    return x if s == 1.0 else x * s


def _dot(a, b, ca, cb):
    return lax.dot_general(a, b, (((ca,), (cb,)), ((), ())), preferred_element_type=F32)


CUM_CHUNK = 256


def _tri(kind, terms=2):
    r = lax.broadcasted_iota(jnp.int32, (CUM_CHUNK, CUM_CHUNK), 0)
    c = lax.broadcasted_iota(jnp.int32, (CUM_CHUNK, CUM_CHUNK), 1)
    m = {"gt": r > c, "le": r <= c, "lt": r < c}[kind]
    t = jnp.where(m, 1.0, 0.0).astype(BF16)
    return jnp.concatenate([t] * terms, axis=0)


def _cum(v, tri, kind):
    n = v.shape[1] // CUM_CHUNK
    two = tri.shape[0] == 2 * CUM_CHUNK
    chunks = [v[:, c * CUM_CHUNK:(c + 1) * CUM_CHUNK] for c in range(n)]
    totals = [jnp.sum(ch, axis=1, keepdims=True) for ch in chunks] if n > 1 else None
    outs = []
    for c, ch in enumerate(chunks):
        r = _dot(jnp.concatenate(_split_bf16(ch), axis=1) if two else ch.astype(BF16), tri, 1, 0)
        others = [] if n == 1 else totals[c + 1:] if kind == "gt" else totals[:c]
        for t in others:
            r = r + t
        outs.append(r)
    return outs[0] if n == 1 else jnp.concatenate(outs, axis=1)


def _lane_masks(hp, w):
    lane = lax.broadcasted_iota(jnp.int32, (1, LANES), 1)
    return [(lane >= e * w) & (lane < (e + 1) * w) for e in range(hp)]


def _att_dims(mode, hp, qk_w, v_w):
    assert mode in ("softmax", "sb")
    assert (hp, qk_w, v_w) in ((2, 128, 64), (2, 64, 64), (1, 128, 128))
    qw = hp * LANES if qk_w == LANES else LANES
    return qw


def attn_fwd(q, k, v, *, name, mode, causal, hp, qk_w, v_w, scale, nh, qc, kc, vc, comm=None):
    qw = _att_dims(mode, hp, qk_w, v_w)
    B, S = q.shape[0], q.shape[1]
    Sk = k.shape[1]
    tq = _pick(S, ATT_TILE)
    tk = _pick(Sk, ATT_TILE)
    if causal:
        assert tq == tk and S == Sk
    nq, nkv = S // tq, Sk // tk
    pre_scaled = math.frexp(scale)[0] == 0.5
    post = 1.0 if pre_scaled else scale

    def body(q_ref, k_ref, v_ref, o_ref, st_ref):
        i = pl.program_id(2)
        qv = q_ref[0].astype(BF16)
        masks_qk = _lane_masks(hp, qk_w) if qk_w < LANES else None
        masks_v = _lane_masks(hp, v_w) if v_w < LANES else None
        if qk_w == LANES:
            qs = [qv[:, e * LANES:(e + 1) * LANES] for e in range(hp)]
        else:
            qs = [jnp.where(masks_qk[e], qv, jnp.zeros_like(qv)) for e in range(hp)]
        if pre_scaled:
            qs = [t * scale for t in qs]
        rows = lax.broadcasted_iota(jnp.int32, (tq, tk), 0)
        cols = lax.broadcasted_iota(jnp.int32, (tq, tk), 1)
        tri = _tri("gt") if mode == "sb" else None

        def kv_tile(j):
            off = pl.multiple_of(j * tk, tk)
            kj = k_ref[0, pl.ds(off, tk), :].astype(BF16)
            vj = v_ref[0, pl.ds(off, tk), :].astype(BF16)
            ks = [kj[:, e * LANES:(e + 1) * LANES] for e in range(hp)] if qk_w == LANES else [kj] * hp
            return ks, vj

        def merge(vals):
            if hp == 1:
                return jnp.broadcast_to(vals[0], (tq, LANES))
            return jnp.where(masks_v[0], vals[0], vals[1])

        if mode == "softmax":
            def block(j, carry, diag):
                ms, ls, acc = carry
                ks, vj = kv_tile(j)
                new_m, new_l, alphas, pvs = [], [], [], []
                for e in range(hp):
                    s = _mul(_dot(qs[e], ks[e], 1, 1), post)
                    if diag:
                        s = jnp.where(rows >= cols, s, NEG)
                    m_new = jnp.maximum(ms[e], jnp.max(s, axis=1, keepdims=True))
                    alpha = jnp.exp(ms[e] - m_new)
                    p = jnp.exp(s - m_new)
                    new_l.append(alpha * ls[e] + jnp.sum(p, axis=1, keepdims=True))
                    new_m.append(m_new)
                    alphas.append(alpha)
                    pvs.append(_dot(p.astype(BF16), vj, 1, 0))
                acc = acc * merge(alphas) + merge(pvs)
                return tuple(new_m), tuple(new_l), acc

            init = (tuple(jnp.full((tq, 1), NEG, F32) for _ in range(hp)),
                    tuple(jnp.zeros((tq, 1), F32) for _ in range(hp)),
                    jnp.zeros((tq, LANES), F32))
            if causal:
                carry = lax.fori_loop(0, i, lambda j, c: block(j, c, False), init)
                ms, ls, acc = block(i, carry, True)
            else:
                ms, ls, acc = lax.fori_loop(0, nkv, lambda j, c: block(j, c, False), init)
            o_ref[0] = acc / merge(list(ls))
            st_ref[0, 0] = merge([m + jnp.log(l) for m, l in zip(ms, ls)])
        else:
            def block(j, carry, diag):
                cs_, acc = carry
                ks, vj = kv_tile(j)
                new_c, pvs = [], []
                for e in range(hp):
                    z = _mul(_dot(qs[e], ks[e], 1, 1), post)
                    lk = -jnp.maximum(z, 0.0) - jnp.log(1.0 + jnp.exp(-jnp.abs(z)))
                    lz = lk + z
                    if diag:
                        lk = jnp.where(rows > cols, lk, 0.0)
                    a = jnp.exp(lz + _cum(lk, tri, "gt") + cs_[e])
                    if diag:
                        a = jnp.where(rows > cols, a, 0.0)
                    pvs.append(_dot(a.astype(BF16), vj, 1, 0))
                    new_c.append(cs_[e] + jnp.sum(lk, axis=1, keepdims=True))
                return tuple(new_c), acc + merge(pvs)

            init = (tuple(jnp.zeros((tq, 1), F32) for _ in range(hp)), jnp.zeros((tq, LANES), F32))
            carry = block(i, init, True)
            cs_, acc = lax.fori_loop(0, i, lambda jj, c: block(i - 1 - jj, c, False), carry)
            o_ref[0] = acc
            st_ref[0, 0] = merge(list(cs_))

    outs, c_outs = _call(
        body, name=name, grid=(B, nh, nq),
        in_specs=[pl.BlockSpec((1, tq, qw), lambda b, h, i: (b, i, qc + h)),
                  pl.BlockSpec((1, Sk, qw), lambda b, h, i: (b, 0, kc + h)),
                  pl.BlockSpec((1, Sk, LANES), lambda b, h, i: (b, 0, vc + h))],
        out_specs=(pl.BlockSpec((1, tq, LANES), lambda b, h, i: (b, i, h)),
                   pl.BlockSpec((1, 1, tq, LANES), lambda b, h, i: (b, h, i, 0))),
        out_shape=(jax.ShapeDtypeStruct((B, S, nh * LANES), F32), jax.ShapeDtypeStruct((B, nh, S, LANES), F32)),
        scratch_shapes=[], args=(q, k, v), sem=("parallel", "parallel", "arbitrary"), comm=comm)
    return (*outs, c_outs) if comm is not None else tuple(outs)


def attn_bwd(q, k, v, o, st, do, *, name, mode, causal, hp, qk_w, v_w, scale, nh, qc, kc, vc, comm=None):
    qw = _att_dims(mode, hp, qk_w, v_w)
    B, S = q.shape[0], q.shape[1]
    Sk = k.shape[1]
    tq = _pick(S, ATT_TILE)
    tk = _pick(Sk, ATT_TILE)
    if causal:
        assert tq == tk and S == Sk
    nq, nkv = S // tq, Sk // tk
    pre_scaled = math.frexp(scale)[0] == 0.5
    post = 1.0 if pre_scaled else scale

    def body(q_ref, k_ref, v_ref, o_ref, st_ref, do_ref, dq_ref, dk_ref, dv_ref):
        i = pl.program_id(2)

        @pl.when(i == 0)
        def _():
            dk_ref[...] = jnp.zeros_like(dk_ref)
            dv_ref[...] = jnp.zeros_like(dv_ref)

        qv = q_ref[0].astype(BF16)
        dov = do_ref[0]
        stv = st_ref[0, 0]
        masks_qk = _lane_masks(hp, qk_w) if qk_w < LANES else None
        masks_v = _lane_masks(hp, v_w) if v_w < LANES else None
        if qk_w == LANES:
            qs = [qv[:, e * LANES:(e + 1) * LANES] for e in range(hp)]
        else:
            qs = [jnp.where(masks_qk[e], qv, jnp.zeros_like(qv)) for e in range(hp)]
        if pre_scaled:
            qs = [t * scale for t in qs]
        if hp == 1:
            dos = [dov.astype(BF16)]
            stats = [stv[:, 0:1]]
        else:
            dos = [jnp.where(masks_v[e], dov, 0.0).astype(BF16) for e in range(hp)]
            stats = [stv[:, e * v_w:e * v_w + 1] for e in range(hp)]
        if mode == "softmax":
            prod = dov * o_ref[0]
            if hp == 1:
                dsum = [jnp.sum(prod, axis=1, keepdims=True)]
            else:
                dsum = [jnp.sum(jnp.where(masks_v[e], prod, 0.0), axis=1, keepdims=True) for e in range(hp)]
        rows = lax.broadcasted_iota(jnp.int32, (tq, tk), 0)
        cols = lax.broadcasted_iota(jnp.int32, (tq, tk), 1)
        if mode == "sb":
            tri_le, tri_lt = _tri("le"), _tri("lt", terms=1)

        def kv_tile(j):
            off = pl.multiple_of(j * tk, tk)
            kj = k_ref[0, pl.ds(off, tk), :].astype(BF16)
            vj = v_ref[0, pl.ds(off, tk), :].astype(BF16)
            ks = [kj[:, e * LANES:(e + 1) * LANES] for e in range(hp)] if qk_w == LANES else [kj] * hp
            return off, ks, vj

        def scatter(off, dz_list, p_list):
            dvj = None
            for e in range(hp):
                t = _dot(p_list[e], dos[e], 0, 0)
                dvj = t if dvj is None else dvj + t
            dv_ref[0, pl.ds(off, tk), :] += dvj
            if qk_w == LANES:
                for e in range(hp):
                    dk_ref[0, pl.ds(off, tk), e * LANES:(e + 1) * LANES] += _dot(dz_list[e], qs[e], 0, 0)
            else:
                dkj = None
                for e in range(hp):
                    t = _dot(dz_list[e], qs[e], 0, 0)
                    dkj = t if dkj is None else dkj + t
                dk_ref[0, pl.ds(off, tk), :] += dkj

        def dq_add(dqs, dz_list, ks):
            out = []
            for e in range(hp):
                out.append(dqs[e] + _dot(dz_list[e], ks[e], 1, 0))
            return tuple(out)

        dq0 = tuple(jnp.zeros((tq, LANES), F32) for _ in range(hp))

        if mode == "softmax":
            def block(j, dqs, diag):
                off, ks, vj = kv_tile(j)
                dzs, ps = [], []
                for e in range(hp):
                    s = _mul(_dot(qs[e], ks[e], 1, 1), post)
                    p = jnp.exp(s - stats[e])
                    if diag:
                        p = jnp.where(rows >= cols, p, 0.0)
                    dp = _dot(dos[e], vj, 1, 1)
                    dzs.append(_mul(p * (dp - dsum[e]), post).astype(BF16))
                    ps.append(p.astype(BF16))
                scatter(off, dzs, ps)
                return dq_add(dqs, dzs, ks)

            if causal:
                dqs = lax.fori_loop(0, i, lambda j, c: block(j, c, False), dq0)
                dqs = block(i, dqs, True)
            else:
                dqs = lax.fori_loop(0, nkv, lambda j, c: block(j, c, False), dq0)
        else:
            def block(j, carry, diag):
                dqs, cps, cgs = carry
                off, ks, vj = kv_tile(j)
                dzs, ps, new_p, new_g = [], [], [], []
                for e in range(hp):
                    z = _mul(_dot(qs[e], ks[e], 1, 1), post)
                    lk = -jnp.maximum(z, 0.0) - jnp.log(1.0 + jnp.exp(-jnp.abs(z)))
                    lz = lk + z
                    sig = jnp.exp(lz)
                    keep = 1.0 - sig
                    if diag:
                        lk = jnp.where(rows > cols, lk, 0.0)
                    keep_after = stats[e] - cps[e] - _cum(lk, tri_le, "le")
                    a = jnp.exp(lz + keep_after)
                    if diag:
                        a = jnp.where(rows > cols, a, 0.0)
                    g = _dot(dos[e], vj, 1, 1) * a
                    gsum = cgs[e] + _cum(g, tri_lt, "lt")
                    dz = _mul(g * keep - gsum * sig, post)
                    if diag:
                        dz = jnp.where(rows > cols, dz, 0.0)
                    dzs.append(dz.astype(BF16))
                    ps.append(a.astype(BF16))
                    new_p.append(cps[e] + jnp.sum(lk, axis=1, keepdims=True))
                    new_g.append(cgs[e] + jnp.sum(g, axis=1, keepdims=True))
                scatter(off, dzs, ps)
                return dq_add(dqs, dzs, ks), tuple(new_p), tuple(new_g)

            zc = tuple(jnp.zeros((tq, 1), F32) for _ in range(hp))
            carry = lax.fori_loop(0, i, lambda j, c: block(j, c, False), (dq0, zc, zc))
            dqs, _, _ = block(i, carry, True)

        if pre_scaled:
            dqs = [t * scale for t in dqs]
        if qk_w == LANES:
            for e in range(hp):
                dq_ref[0, :, e * LANES:(e + 1) * LANES] = dqs[e].astype(dq_ref.dtype)
        else:
            dq_ref[0] = jnp.where(masks_qk[0], dqs[0], dqs[1]).astype(dq_ref.dtype)

    outs, c_outs = _call(
        body, name=name, grid=(B, nh, nq),
        in_specs=[pl.BlockSpec((1, tq, qw), lambda b, h, i: (b, i, qc + h)),
                  pl.BlockSpec((1, Sk, qw), lambda b, h, i: (b, 0, kc + h)),
                  pl.BlockSpec((1, Sk, LANES), lambda b, h, i: (b, 0, vc + h)),
                  pl.BlockSpec((1, tq, LANES), lambda b, h, i: (b, i, h)),
                  pl.BlockSpec((1, 1, tq, LANES), lambda b, h, i: (b, h, i, 0)),
                  pl.BlockSpec((1, tq, LANES), lambda b, h, i: (b, i, h))],
        out_specs=(pl.BlockSpec((1, tq, qw), lambda b, h, i: (b, i, h)),
                   pl.BlockSpec((1, Sk, qw), lambda b, h, i: (b, 0, h)),
                   pl.BlockSpec((1, Sk, LANES), lambda b, h, i: (b, 0, h))),
        out_shape=(jax.ShapeDtypeStruct((B, S, nh * qw), BF16), jax.ShapeDtypeStruct((B, Sk, nh * qw), F32),
                   jax.ShapeDtypeStruct((B, Sk, nh * LANES), F32)),
        scratch_shapes=[], args=(q, k, v, o, st, do), sem=("parallel", "parallel", "arbitrary"), comm=comm)
    return (*outs, c_outs) if comm is not None else tuple(outs)


def _pos(p):
    return 4 * p[0] + 2 * p[1] + p[2]


N_CHIP = 4
MESH_ID = pl.DeviceIdType.MESH
_ANY = pl.BlockSpec(memory_space=pl.ANY)


def _me():
    return lax.axis_index("x"), lax.axis_index("y"), lax.axis_index("c")


def _other_chips(x, y):
    return [(1 - x, y), (x, 1 - y), (1 - x, 1 - y)]


class _Comm:
    def __init__(self, arrs, out_shapes, n_sem, n_local, start, finish):
        self.arrs, self.out_shapes, self.start, self.finish = list(arrs), list(out_shapes), start, finish
        self.scratch = [pltpu.SemaphoreType.DMA((n_sem,)), pltpu.SemaphoreType.DMA((n_sem,)),
                        pltpu.SemaphoreType.DMA((max(n_local, 1),))]


def _run_comm(comm, name):
    n = len(comm.arrs)

    def body(*refs):
        r = (refs[:n], refs[n:2 * n], refs[2 * n], refs[2 * n + 1], refs[2 * n + 2])
        comm.start(*r)
        comm.finish(*r)

    return pl.pallas_call(
        body, name=name, in_specs=[_ANY] * n, out_specs=[_ANY] * n, out_shape=comm.out_shapes,
        scratch_shapes=comm.scratch, compiler_params=pltpu.CompilerParams(has_side_effects=True),
    )(*comm.arrs)


def _call(body, *, name, grid, in_specs, out_specs, out_shape, scratch_shapes, args, sem, comm=None):
    in_specs, out_specs, out_shape = list(in_specs), list(out_specs), list(out_shape)
    if comm is None:
        res = pl.pallas_call(body, name=name, grid=grid, in_specs=in_specs, out_specs=out_specs, out_shape=out_shape,
                             scratch_shapes=scratch_shapes, compiler_params=_cparams(sem))(*args)
        return list(res), []
    n_in, n_out, n_scr, nc = len(in_specs), len(out_specs), len(scratch_shapes), len(comm.arrs)

    def wrapped(*refs):
        ins, refs = refs[:n_in], refs[n_in:]
        c_in, refs = refs[:nc], refs[nc:]
        outs, refs = refs[:n_out], refs[n_out:]
        c_out, refs = refs[:nc], refs[nc:]
        scr, sems = refs[:n_scr], refs[n_scr:]
        ids = [pl.program_id(a) for a in range(len(grid))]
        first = functools.reduce(jnp.logical_and, [i == 0 for i in ids])
        last = functools.reduce(jnp.logical_and, [i == g - 1 for i, g in zip(ids, grid)])

        @pl.when(first)
        def _():
            comm.start(c_in, c_out, *sems)

        body(*ins, *outs, *scr)

        @pl.when(last)
        def _():
            comm.finish(c_in, c_out, *sems)

    res = pl.pallas_call(
        wrapped, name=name, grid=grid, in_specs=in_specs + [_ANY] * nc, out_specs=out_specs + [_ANY] * nc,
        out_shape=out_shape + comm.out_shapes, scratch_shapes=list(scratch_shapes) + comm.scratch,
        compiler_params=pltpu.CompilerParams(dimension_semantics=("arbitrary",) * len(grid),
                                             vmem_limit_bytes=VMEM_LIMIT, has_side_effects=True),
    )(*args, *comm.arrs)
    return list(res[:n_out]), list(res[n_out:])


def gather_two_level(bufs, *, name=None):
    n = len(bufs)

    def parts(x_refs, out_refs, send_sems, recv_sems, local_sems):
        x, y, c = _me()
        me, sibling = (x, y, c), (x, y, 1 - c)
        chips = _other_chips(x, y)

        def copy(a, k, block, to, from_input=False):
            dst = out_refs[a].at[_pos(block)]
            return pltpu.make_async_remote_copy(src_ref=x_refs[a] if from_input else dst, dst_ref=dst,
                                                send_sem=send_sems.at[7 * a + k], recv_sem=recv_sems.at[7 * a + k],
                                                device_id=to, device_id_type=MESH_ID)

        mine = [pltpu.make_async_copy(x_refs[a], out_refs[a].at[_pos(me)], local_sems.at[a]) for a in range(n)]
        first = []
        for a in range(n):
            first.append(copy(a, 0, me, sibling, from_input=True))
            first += [copy(a, 1 + j, me, (*chip, c), from_input=True) for j, chip in enumerate(chips)]
        return copy, mine, first, me, sibling, chips, c

    def start(*refs):
        _, mine, first, *_ = parts(*refs)
        for cp in mine + first:
            cp.start()

    def finish(*refs):
        copy, mine, first, me, sibling, chips, c = parts(*refs)
        passed = []
        for j, chip in enumerate(chips):
            for a in range(n):
                copy(a, 1 + j, (*chip, c), me).wait_recv()
                passed.append(copy(a, 4 + j, (*chip, c), sibling))
                passed[-1].start()
        for a in range(n):
            copy(a, 0, sibling, me).wait_recv()
            for j, chip in enumerate(chips):
                copy(a, 4 + j, (*chip, 1 - c), me).wait_recv()
        for cp in first + passed:
            cp.wait_send()
        for cp in mine:
            cp.wait()

    out_shapes = [jax.ShapeDtypeStruct((N_DEV,) + b.shape, b.dtype) for b in bufs]
    comm = _Comm(bufs, out_shapes, 7 * n, n, start, finish)
    return _run_comm(comm, name) if name else comm


def sibling_exchange(sends, *, name=None):
    n = len(sends)

    def copies(s_refs, out_refs, send_sems, recv_sems, local_sems):
        x, y, c = _me()
        return [pltpu.make_async_remote_copy(src_ref=s_refs[a].at[1 - c], dst_ref=out_refs[a], send_sem=send_sems.at[a],
                                             recv_sem=recv_sems.at[a], device_id=(x, y, 1 - c), device_id_type=MESH_ID)
                for a in range(n)]

    def start(*refs):
        for cp in copies(*refs):
            cp.start()

    def finish(*refs):
        for cp in copies(*refs):
            cp.wait()

    out_shapes = [jax.ShapeDtypeStruct(s.shape[1:], s.dtype) for s in sends]
    comm = _Comm(sends, out_shapes, n, 0, start, finish)
    return _run_comm(comm, name) if name else comm


def _flat2(shape):
    return math.prod(shape[:-1]), shape[-1]


def _row_tile(r):
    return r if r <= 512 else _pick8(r, 256)


def _pick8(n, pref):
    t = pref
    while n % t:
        t -= 8
    return t


def chip_sum(send, got, core, *, name):
    shape = got.shape[1:]
    r, cdim = _flat2(shape)
    tr = _row_tile(r)
    send = send.reshape(2, N_CHIP, r, cdim)
    got = got.reshape(N_CHIP, r, cdim)

    def body(core_ref, s_ref, g_ref, o_ref):
        o_ref[...] = (s_ref[0].astype(F32) + g_ref[...].astype(F32)).astype(o_ref.dtype)

    blk = pl.BlockSpec((N_CHIP, tr, cdim), lambda i, core_ref: (0, i, 0))
    out = pl.pallas_call(
        body, name=name,
        grid_spec=pltpu.PrefetchScalarGridSpec(
            num_scalar_prefetch=1, grid=(r // tr,),
            in_specs=[pl.BlockSpec((1, N_CHIP, tr, cdim), lambda i, core_ref: (core_ref[0], 0, i, 0)), blk],
            out_specs=blk),
        out_shape=jax.ShapeDtypeStruct((N_CHIP, r, cdim), BF16),
        compiler_params=_cparams(("parallel",)),
    )(core, send, got)
    return out.reshape((N_CHIP,) + shape)


def chip_exchange(sums, *, name=None):
    n = len(sums)

    def copies(s_refs, out_refs, send_sems, recv_sems, local_sems):
        x, y, c = _me()
        mine = 2 * x + y
        chips = _other_chips(x, y)
        local = [pltpu.make_async_copy(s_refs[a].at[mine], out_refs[a].at[mine], local_sems.at[a]) for a in range(n)]
        sends, recvs = [], []
        for a in range(n):
            for j, (px, py) in enumerate(chips):
                sems = dict(send_sem=send_sems.at[3 * a + j], recv_sem=recv_sems.at[3 * a + j], device_id=(px, py, c),
                            device_id_type=MESH_ID)
                sends.append(pltpu.make_async_remote_copy(src_ref=s_refs[a].at[2 * px + py], dst_ref=out_refs[a].at[mine], **sems))
                recvs.append(pltpu.make_async_remote_copy(src_ref=s_refs[a].at[mine], dst_ref=out_refs[a].at[2 * px + py], **sems))
        return local, sends, recvs

    def start(*refs):
        local, sends, _ = copies(*refs)
        for cp in local + sends:
            cp.start()

    def finish(*refs):
        local, sends, recvs = copies(*refs)
        for cp in recvs:
            cp.wait_recv()
        for cp in sends:
            cp.wait_send()
        for cp in local:
            cp.wait()

    out_shapes = [jax.ShapeDtypeStruct(s.shape, s.dtype) for s in sums]
    comm = _Comm(sums, out_shapes, 3 * n, n, start, finish)
    return _run_comm(comm, name) if name else comm


def sum_adamw(parts, w, m, v, *, name):
    shape = w.shape
    R, cdim = _flat2(shape)
    n_slots = parts.shape[0]
    tr = _row_tile(R)
    parts = parts.reshape(n_slots, R, cdim)
    w, m, v = (t.reshape(1, R, cdim) for t in (w, m, v))
    c1 = 1.0 - ADAM_B1 ** ADAM_STEP
    c2 = 1.0 - ADAM_B2 ** ADAM_STEP

    def body(p_ref, w_ref, m_ref, v_ref, g_ref, d_ref, mo_ref, vo_ref):
        g = p_ref[0].astype(F32)
        for s in range(1, n_slots):
            g = g + p_ref[s].astype(F32)
        mn = ADAM_B1 * m_ref[...] + (1.0 - ADAM_B1) * g
        vn = ADAM_B2 * v_ref[...] + (1.0 - ADAM_B2) * jnp.square(g)
        g_ref[...] = g
        mo_ref[...] = mn
        vo_ref[...] = vn
        d_ref[...] = -ADAM_LR * ((mn / c1) / (jnp.sqrt(vn / c2) + ADAM_EPS) + ADAM_WD * w_ref[...])

    rs = pl.BlockSpec((None, tr, cdim), lambda i: (0, i, 0))
    sd = jax.ShapeDtypeStruct((1, R, cdim), F32)
    res = pl.pallas_call(
        body, name=name, grid=(R // tr,),
        in_specs=[pl.BlockSpec((n_slots, tr, cdim), lambda i: (0, i, 0)), rs, rs, rs],
        out_specs=(rs, rs, rs, rs), out_shape=(sd, sd, sd, sd),
        compiler_params=_cparams(("parallel",)),
    )(parts, w, m, v)
    return [t.reshape(shape) for t in res]


WEIGHTS = ("ln_mix_pre", "w_in", "b_gate", "q_norm", "w_uq", "kv_norm", "w_uk", "w_uv", "mem_norm", "w_mem_kv",
           "w_branch_out", "w_out", "ln_mix_post", "ln_mlp_pre", "w_mlp_up", "w_mlp_down", "ln_mlp_post")
BIG = dict(w_in=((D_MODEL, IN_WIDTH), 1), w_uq=((Q_LORA, MLA_HEADS * (MLA_NOPE + MLA_ROPE)), 1),
           w_uk=((KV_LORA, MLA_HEADS * MLA_NOPE), 1), w_uv=((KV_LORA, MLA_HEADS * MLA_V), 1),
           w_mem_kv=((D_MODEL, 2 * QM_W), 0), w_branch_out=((N_BRANCH, BRANCH_W, D_MODEL), 2),
           w_out=((D_MODEL, D_MODEL), 0), w_mlp_up=((D_MODEL, D_FF), 1), w_mlp_down=((D_FF, D_MODEL), 0))
SMALL = tuple(n for n in WEIGHTS if n not in BIG)


PACK_TILE = 512


def _pad_rows(a):
    r = a.shape[-2]
    to = PACK_TILE if r > PACK_TILE else 8
    pad = [(0, 0)] * a.ndim
    pad[-2] = (0, (-r) % to)
    return jnp.pad(a, pad)


def _pack_rows(arrs):
    return _pad_rows(jnp.concatenate([a.reshape(-1, LANES) for a in arrs], axis=0))


def _unpack_rows(packed, shapes, lead=()):
    out, r = [], 0
    for shp in shapes:
        n = math.prod(shp) // LANES
        out.append(packed[..., r:r + n, :].reshape(lead + tuple(shp)))
        r += n
    return out


def _slots_to_full(name, slots):
    full, ax = BIG[name]
    return jnp.moveaxis(slots, 0, ax).reshape(full)


def _full_to_slots(name, w):
    full, ax = BIG[name]
    split = full[:ax] + (N_DEV, full[ax] // N_DEV) + full[ax + 1:]
    return jnp.moveaxis(w.reshape(split), ax, 0)


def _full_to_owner(name, w):
    s = _full_to_slots(name, w)
    return jnp.swapaxes(s.reshape((N_CHIP, 2) + s.shape[1:]), 0, 1)


def _in_segments():
    orig = dict(zip(("cq", "ckv", "kr", "sb", "qm", "gate"), zip(np.cumsum((0,) + IN_SIZES[:-1]).tolist(), IN_SIZES)))
    shard = IN_WIDTH // N_DEV
    segs = []
    for n in PAD_ORDER:
        if n not in orig:
            continue
        o0, w = orig[n]
        c = o0
        while c < o0 + w:
            s = c // shard
            e = min(o0 + w, (s + 1) * shard)
            segs.append((s, c - s * shard, PAD_OFF[n] + (c - o0), e - c))
            c = e
    return segs


def in_slots_to_pad(slots, *, name):
    _, R, shard = slots.shape
    tr = _pick8(R, ROW_TILE)
    segs = _in_segments()

    def body(in_ref, out_ref):
        out_ref[...] = jnp.zeros_like(out_ref)
        for s, a, d, w in segs:
            out_ref[:, d:d + w] = in_ref[s, :, a:a + w]

    return pl.pallas_call(
        body, name=name, grid=(R // tr,), in_specs=[pl.BlockSpec((N_DEV, tr, shard), lambda i: (0, i, 0))],
        out_specs=pl.BlockSpec((tr, IN_PAD), lambda i: (i, 0)), out_shape=jax.ShapeDtypeStruct((R, IN_PAD), slots.dtype),
        compiler_params=_cparams(("parallel",)),
    )(slots)


def in_pad_to_owner(g, *, name):
    R = g.shape[0]
    shard = IN_WIDTH // N_DEV
    tr = _pick8(R, ROW_TILE)
    segs = _in_segments()

    def body(in_ref, out_ref):
        for s, a, d, w in segs:
            out_ref[s % 2, s // 2, :, a:a + w] = in_ref[:, d:d + w]

    return pl.pallas_call(
        body, name=name, grid=(R // tr,), in_specs=[pl.BlockSpec((tr, IN_PAD), lambda i: (i, 0))],
        out_specs=pl.BlockSpec((2, N_CHIP, tr, shard), lambda i: (0, 0, i, 0)),
        out_shape=jax.ShapeDtypeStruct((2, N_CHIP, R, shard), g.dtype),
        compiler_params=_cparams(("parallel",)),
    )(g)


def _pad_heads(w, width):
    r = w.shape[0]
    return jnp.pad(w.reshape(r, MLA_HEADS, width), ((0, 0), (0, 0), (0, HEAD_PAD - width))).reshape(r, MLA_PAD_W)


def _unpad_heads(wp, width):
    r = wp.shape[0]
    return wp.reshape(r, MLA_HEADS, HEAD_PAD)[:, :, :width].reshape(r, MLA_HEADS * width)


def _rope_inv_freq():
    half = MLA_ROPE // 2
    inv = 1.0 / (ROPE_THETA ** (jnp.arange(half, dtype=F32) * (2.0 / MLA_ROPE)))
    tab = jnp.zeros((LANES,), F32)
    tab = tab.at[MLA_NOPE:MLA_NOPE + half].set(inv).at[MLA_NOPE + half:MLA_NOPE + MLA_ROPE].set(inv)
    return tab.reshape(1, LANES)


def _as_tuple(r):
    return r if isinstance(r, tuple) else (r,)


class _Exchange:
    def __init__(self, w):
        self.w = w
        self.rest = tuple(n for n in BIG if n != "w_in")
        self.pending = {}
        self.reduced = {}

    def first_comm(self):
        return gather_two_level([self.w["w_in"].astype(BF16)])

    def first_weights(self, slots):
        return in_slots_to_pad(slots[0], name="w_in_layout")

    def rest_comm(self):
        return gather_two_level([self.w[n].astype(BF16) for n in self.rest])

    def rest_weights(self, slots):
        return {n: s if n == "w_mlp_up" else _slots_to_full(n, s) for n, s in zip(self.rest, slots)}

    def sibling_comm(self, tag, grads):
        self.pending[tag] = (tuple(grads), [grads[n] for n in grads])
        return sibling_exchange(self.pending[tag][1])

    def chip_comm(self, tag, gots):
        names, sends = self.pending[tag]
        core = lax.axis_index("c").astype(jnp.int32).reshape(1)
        sums = [chip_sum(s, g, core, name=f"chip_sum_{n}") for n, s, g in zip(names, sends, gots)]
        self.pending[tag] = names
        return chip_exchange(sums)

    def reduce_start(self, tag, grads):
        return self.chip_comm(tag, _run_comm(self.sibling_comm(tag, grads), f"sibling_{tag}"))

    def reduce_done(self, tag, recvs):
        self.reduced.update(zip(self.pending.pop(tag), recvs))


def _local_step(x, mem, positions, tgt, sm, ex):
    B, S, D = x.shape
    M = mem.shape[1]
    T = B * S
    x2 = x.reshape(T, D)
    mem2 = mem.reshape(B * M, D)
    pos = positions.reshape(T, 1).astype(F32)
    invf = _rope_inv_freq()
    cq_col, ckv_col = PAD_OFF["cq"] // Q_LORA, PAD_OFF["ckv"] // KV_LORA
    sb_col, qm_col = PAD_OFF["sb"] // LANES, PAD_OFF["qm"] // LANES
    sb_blk = SB_HEADS * SB_DIM // LANES
    mla = dict(mode="softmax", causal=True, hp=2, qk_w=128, v_w=64, scale=(MLA_NOPE + MLA_ROPE) ** -0.5,
               nh=MLA_HEADS // 2, qc=0, kc=0, vc=0)
    sbk = dict(mode="sb", causal=True, hp=2, qk_w=64, v_w=64, scale=SB_DIM ** -0.5, nh=SB_HEADS // 2,
               qc=sb_col, kc=sb_col + sb_blk, vc=sb_col + 2 * sb_blk)
    mca = dict(mode="softmax", causal=False, hp=1, qk_w=128, v_w=128, scale=MEM_DIM ** -0.5, nh=MEM_HEADS,
               qc=qm_col, kc=0, vc=MEM_HEADS)

    h, *slots = _as_tuple(rms_fwd(x2, sm["ln_mix_pre"], name="rms_mix_pre", comm=ex.first_comm()))
    w_in_pad = ex.first_weights(slots[0] if slots else None)
    proj = matmul(h, w_in_pad, out_dtype=BF16, name="mm_in")
    proj3 = proj.reshape(B, S, IN_PAD)
    o_sb, st_sb, *slots = attn_fwd(proj3, proj3, proj3, name="sb_fwd", comm=ex.rest_comm(), **sbk)
    W = ex.rest_weights(slots[0] if slots else None)
    w_uq_pad = _pad_heads(W["w_uq"], MLA_NOPE + MLA_ROPE)
    w_uk_pad = _pad_heads(W["w_uk"], MLA_NOPE)
    cqn = rms_fwd(proj, sm["q_norm"], n=Q_LORA, col=cq_col, name="rms_q")
    ckvn = rms_fwd(proj, sm["kv_norm"], n=KV_LORA, col=ckv_col, name="rms_kv")
    qpre = matmul(cqn, w_uq_pad, out_dtype=BF16, name="mm_uq")
    kpre = matmul(ckvn, w_uk_pad, out_dtype=BF16, name="mm_uk")
    v_mla = matmul(ckvn, W["w_uv"], out_dtype=BF16, name="mm_uv").reshape(B, S, -1)
    q_pad, k_pad = rope_fwd(qpre, kpre, proj, pos, invf, name="rope_fwd")
    q_pad, k_pad = q_pad.reshape(B, S, -1), k_pad.reshape(B, S, -1)
    o_mla, st_mla = attn_fwd(q_pad, k_pad, v_mla, name="mla_fwd", **mla)
    memh = rms_fwd(mem2, sm["mem_norm"], name="rms_mem")
    mkv = matmul(memh, W["w_mem_kv"], out_dtype=BF16, name="mm_memkv").reshape(B, M, -1)
    o_mem, st_mem = attn_fwd(proj3, mkv, mkv, name="mem_fwd", **mca)
    outs = [o.reshape(T, BRANCH_W) for o in (o_mla, o_sb, o_mem)]
    ps = [matmul(o, W["w_branch_out"][b], out_dtype=BF16, name=f"mm_bo{b}") for b, o in enumerate(outs)]
    merged = gate_merge(proj, sm["b_gate"], ps, name="gate_merge")
    y, x1, h2 = matmul(merged, W["w_out"], name="mm_out_norms",
                       rows=dict(fn=_mix_post_rows, row_ins=[x2], vec_ins=[sm["ln_mix_post"], sm["ln_mlp_pre"]],
                                 row_outs=[F32, F32, BF16], vec_outs=[]))
    u = matmul(h2, W["w_mlp_up"], b_slots="n", act="relu2", out_dtype=BF16, name="mm_up")
    dx2, dd, loss_p, dg_mlp_post = matmul(
        u, W["w_mlp_down"], name="mm_down_loss",
        rows=dict(fn=_loss_rows, row_ins=[x1, tgt.reshape(T, D)], vec_ins=[sm["ln_mlp_post"]], row_outs=[F32, BF16],
                  vec_outs=[LANES, D]))

    da = matmul(dd, W["w_mlp_down"], tb=True, act="relu2_bwd", act_in=u, out_dtype=BF16, name="mm_down_dx")
    g_down = matmul(u, dd, ta=True, out_dtype=BF16, out_slots="m", name="mm_down_dw")
    g_up = matmul(h2, da, ta=True, out_dtype=BF16, out_slots="n", name="mm_up_dw")
    sib = ex.sibling_comm("mlp", dict(w_mlp_down=g_down, w_mlp_up=g_up))
    dx1, dy, dg_mlp_pre, dg_mix_post, *got = matmul(
        da, _slots_to_full("w_mlp_up", W["w_mlp_up"]), tb=True, name="mm_up_dx_norms", comm=sib,
        rows=dict(fn=_mlp_pre_bwd_rows, row_ins=[dx2, x1, y], vec_ins=[sm["ln_mlp_pre"], sm["ln_mix_post"]],
                  row_outs=[F32, BF16], vec_outs=[D, D]))
    red_mlp = ex.chip_comm("mlp", got[0] if got else None)
    dmerged = matmul(dy, W["w_out"], tb=True, name="mm_out_dx")
    g_out = matmul(merged, dy, ta=True, out_dtype=BF16, name="mm_out_dw")
    dlog, dp0, dp1, dp2, db_gate = gate_bwd(dmerged, proj, sm["b_gate"], ps, name="gate_bwd")
    dps = (dp0, dp1, dp2)
    g_bo = jnp.stack([matmul(outs[b], dps[b], ta=True, out_dtype=BF16, name=f"mm_bo{b}_dw") for b in range(N_BRANCH)])
    dos = [matmul(dps[b], W["w_branch_out"][b], tb=True, out_dtype=BF16, name=f"mm_bo{b}_dx").reshape(B, S, BRANCH_W)
           for b in range(N_BRANCH)]
    dq_pad, dk_pad, dv_mla, *got = attn_bwd(q_pad, k_pad, v_mla, o_mla, st_mla, dos[0], name="mla_bwd", comm=red_mlp, **mla)
    ex.reduce_done("mlp", got[0] if got else None)
    dqm, dmk, dmv = attn_bwd(proj3, mkv, mkv, o_mem, st_mem, dos[2], name="mem_bwd", **mca)
    dmkv = jnp.concatenate([dmk, dmv], axis=-1).astype(BF16).reshape(B * M, -1)
    dmemh = matmul(dmkv, W["w_mem_kv"], tb=True, name="mm_memkv_dx")
    g_memkv = matmul(memh, dmkv, ta=True, out_dtype=BF16, name="mm_memkv_dw")
    _, dg_mem_norm = rms_bwd(dmemh, mem2, sm["mem_norm"], name="rms_mem_bwd")
    dq_pad, dk_pad, dv_mla = dq_pad.reshape(T, -1), dk_pad.reshape(T, -1), dv_mla.reshape(T, -1)
    dqpre, dkr = rope_bwd(dq_pad, dk_pad, pos, invf, name="rope_bwd")
    dcqn = matmul(dqpre, w_uq_pad, tb=True, name="mm_uq_dx")
    g_uq = _unpad_heads(matmul(cqn, dqpre, ta=True, out_dtype=BF16, name="mm_uq_dw"), MLA_NOPE + MLA_ROPE)
    dckvn = matmul(dk_pad, w_uk_pad, tb=True, name="mm_uk_dx")
    dckvn = matmul(dv_mla, W["w_uv"], tb=True, acc=dckvn, name="mm_uv_dx")
    g_uk = _unpad_heads(matmul(ckvn, dk_pad, ta=True, out_dtype=BF16, name="mm_uk_dw"), MLA_NOPE)
    g_uv = matmul(ckvn, dv_mla, ta=True, out_dtype=BF16, name="mm_uv_dw")
    dcq, dg_q_norm = rms_bwd(dcqn, proj, sm["q_norm"], n=Q_LORA, col=cq_col, out_dtype=BF16, name="rms_q_bwd")
    mix = dict(w_out=g_out, w_branch_out=g_bo, w_uq=g_uq, w_uk=g_uk, w_uv=g_uv, w_mem_kv=g_memkv)
    sib = ex.sibling_comm("mix", {n: _full_to_owner(n, g) for n, g in mix.items()})
    dckv, dg_kv_norm, *got = rms_bwd(dckvn, proj, sm["kv_norm"], n=KV_LORA, col=ckv_col, out_dtype=BF16,
                                     name="rms_kv_bwd", comm=sib)
    red_mix = ex.chip_comm("mix", got[0] if got else None)
    dsq, dsk, dsv, *got = attn_bwd(proj3, proj3, proj3, o_sb, st_sb, dos[1], name="sb_bwd", comm=red_mix, **sbk)
    ex.reduce_done("mix", got[0] if got else None)
    pieces = dict(gate=dlog, sb=jnp.concatenate([dsq, dsk, dsv], axis=-1).reshape(T, -1), qm=dqm.reshape(T, -1),
                  ckv=dckv, cq=dcq, kr=dkr, zpad=jnp.zeros((T, PAD_SIZES["zpad"]), BF16))
    dproj = jnp.concatenate([pieces[n].astype(BF16) for n in PAD_ORDER], axis=1)
    g_in = in_pad_to_owner(matmul(h, dproj, ta=True, out_dtype=BF16, name="mm_in_dw"), name="g_in_layout")
    red_in = ex.reduce_start("in", dict(w_in=g_in))
    dx, dg_mix_pre, *got = matmul(
        dproj, w_in_pad, tb=True, comm=red_in, name="mm_in_dx_norm",
        rows=dict(fn=_mix_pre_bwd_rows, row_ins=[x2, dx1], vec_ins=[sm["ln_mix_pre"]], row_outs=[F32], vec_outs=[D]))
    ex.reduce_done("in", got[0] if got else None)

    small = dict(ln_mix_pre=dg_mix_pre, b_gate=db_gate, q_norm=dg_q_norm, kv_norm=dg_kv_norm, mem_norm=dg_mem_norm,
                 ln_mix_post=dg_mix_post, ln_mlp_pre=dg_mlp_pre, ln_mlp_post=dg_mlp_post)
    return loss_p, dx.reshape(B, S, D), small


def kernel(x, mem, positions, ln_mix_pre, w_in, b_gate, q_norm, w_uq, kv_norm, w_uk, w_uv, mem_norm, w_mem_kv, w_branch_out, w_out, ln_mix_post, ln_mlp_pre, w_mlp_up, w_mlp_down, ln_mlp_post, loss_target, m_ln_mix_pre, m_w_in, m_b_gate, m_q_norm, m_w_uq, m_kv_norm, m_w_uk, m_w_uv, m_mem_norm, m_w_mem_kv, m_w_branch_out, m_w_out, m_ln_mix_post, m_ln_mlp_pre, m_w_mlp_up, m_w_mlp_down, m_ln_mlp_post, v_ln_mix_pre, v_w_in, v_b_gate, v_q_norm, v_w_uq, v_kv_norm, v_w_uk, v_w_uv, v_mem_norm, v_w_mem_kv, v_w_branch_out, v_w_out, v_ln_mix_post, v_ln_mlp_pre, v_w_mlp_up, v_w_mlp_down, v_ln_mlp_post):
    given = dict(locals())
    w = {n: given[n][0] for n in WEIGHTS}
    m = {n: given["m_" + n][0] for n in WEIGHTS}
    v = {n: given["v_" + n][0] for n in WEIGHTS}
    sm = {n: w[n].reshape(1, -1) for n in SMALL}
    ex = _Exchange(w)
    loss_row, grad_x, g_small = _local_step(x, mem, positions, loss_target, sm, ex)

    res = {n: sum_adamw(ex.reduced[n], given[n], given["m_" + n], given["v_" + n], name=f"adamw_{n}") for n in BIG}

    small_shapes = [w[n].shape for n in SMALL] + [(LANES,)]
    parts = gather_two_level([_pack_rows([g_small[n] for n in SMALL] + [loss_row])], name="gather_small_grads")[0]
    no_row = jnp.zeros((1, LANES), F32)
    res_small = sum_adamw(parts, *[_pack_rows([t[n] for n in SMALL] + [no_row]) for t in (w, m, v)], name="adamw_replicated")
    res_small = [_unpack_rows(r, small_shapes) for r in res_small]
    for i, n in enumerate(SMALL):
        res[n] = [r[i][None] for r in res_small]
    loss = res_small[0][len(SMALL)][0]

    out = [loss, grad_x]
    for k in range(4):
        out += [res[n][k] for n in WEIGHTS]
    return tuple(out)
```

```python
import functools
import math

import numpy as np
import jax
import jax.numpy as jnp
from jax import lax
from jax.experimental import pallas as pl
from jax.experimental.pallas import tpu as pltpu

F32 = jnp.float32
BF16 = jnp.bfloat16

D_MODEL = 1024
MEM_HEADS, MEM_DIM = 4, 128
MLA_HEADS, MLA_NOPE, MLA_ROPE, MLA_V = 8, 64, 32, 64
Q_LORA, KV_LORA = 384, 256
ROPE_THETA = 10000.0
SB_HEADS, SB_DIM = 8, 64
D_FF = 4 * D_MODEL
N_BRANCH, BRANCH_W = 3, 512
EPS = 1e-6
SB_W = 3 * SB_HEADS * SB_DIM
QM_W = MEM_HEADS * MEM_DIM
GATE_W = N_BRANCH * D_MODEL
IN_SIZES = (Q_LORA, KV_LORA, MLA_ROPE, SB_W, QM_W, GATE_W)
IN_WIDTH = sum(IN_SIZES)
ADAM_LR, ADAM_B1, ADAM_B2, ADAM_EPS, ADAM_WD, ADAM_STEP = 0.001, 0.9, 0.999, 1e-08, 0.01, 10

N_DEV = 8
LANES = 128
KR_PAD = LANES
PAD_ORDER = ("gate", "cq", "kr", "ckv", "sb", "qm", "zpad")
MM_TILE = 1024
MM_ROWS = 2048
MM_DEPTH = 4096
MM_VMEM = 40 * 1024 * 1024
PAD_SIZES = dict(gate=GATE_W, cq=Q_LORA, kr=KR_PAD, ckv=KV_LORA, sb=SB_W, qm=QM_W)
PAD_SIZES["zpad"] = (-sum(PAD_SIZES.values())) % MM_TILE
PAD_OFF = {}
_o = 0
for _n in PAD_ORDER:
    PAD_OFF[_n] = _o
    _o += PAD_SIZES[_n]
IN_PAD = _o
HEAD_PAD = LANES
MLA_PAD_W = MLA_HEADS * HEAD_PAD
VMEM_LIMIT = 48 * 1024 * 1024


def _cparams(sem):
    return pltpu.CompilerParams(dimension_semantics=sem, vmem_limit_bytes=VMEM_LIMIT)


def _pick(n, pref):
    if n <= pref:
        return n
    t = (pref // LANES) * LANES
    while t >= LANES:
        if n % t == 0:
            return t
        t -= LANES
    return n


def matmul(a, b, *, name, ta=False, tb=False, out_dtype=F32, acc=None, act=None, act_in=None,
           b_slots=None, out_slots=None, rows=None, comm=None, tm_pref=MM_ROWS, tn_pref=MM_TILE, tk_pref=MM_DEPTH):
    M, K = (a.shape[1], a.shape[0]) if ta else a.shape
    tm, tk = _pick(M, tm_pref), _pick(K, tk_pref)
    if b_slots == "n":
        assert not tb and b.shape[:2] == (N_DEV, K)
        tn = b.shape[2]
        N = N_DEV * tn
    elif b_slots == "k":
        assert tb and N_DEV * b.shape[2] == K
        N, tk = b.shape[1], b.shape[2]
        tn = _pick(N, tn_pref)
    else:
        N = b.shape[0] if tb else b.shape[1]
        assert (b.shape[1] if tb else b.shape[0]) == K
        tn = _pick(N, tn_pref)
    if out_slots == "m":
        tm = M // N_DEV
    elif out_slots == "n":
        tn = N // N_DEV
    assert out_slots is None or (acc is None and act_in is None)
    if rows is not None:
        assert acc is None and act is None and out_slots is None
        tn = N
        row_ins, vec_ins = list(rows["row_ins"]), list(rows["vec_ins"])
        row_outs, vec_outs = list(rows["row_outs"]), list(rows["vec_outs"])

    def block_bytes(tm_, tk_):
        ab = tm_ * tk_ * a.dtype.itemsize + tk_ * tn * b.dtype.itemsize
        per_out = jnp.dtype(out_dtype).itemsize * (2 if act == "relu2" else 1)
        per_out += sum(t.dtype.itemsize for t in (acc, act_in) if t is not None)
        if rows is not None:
            per_out = sum(t.dtype.itemsize for t in row_ins) + sum(jnp.dtype(d).itemsize for d in row_outs)
        return 2 * (ab + tm_ * tn * per_out) + (tm_ * tn * 4 if K // tk_ > 1 else 0)

    while block_bytes(tm, tk) > MM_VMEM:
        if tm > 512 and out_slots != "m" and M % (tm // 2) == 0:
            tm //= 2
        elif tk > 512 and b_slots != "k" and K % (tk // 2) == 0:
            tk //= 2
        else:
            break
    nk = K // tk
    dims = (((0 if ta else 1,), (1 if tb else 0,)), ((), ()))
    assert act in (None, "relu2", "relu2_bwd") and (act == "relu2_bwd") == (act_in is not None)

    def body(*refs):
        a_ref, b_ref = refs[0], refs[1]
        pos = 2
        acc_ref = act_ref = None
        if acc is not None:
            acc_ref = refs[pos]
            pos += 1
        if act_in is not None:
            act_ref = refs[pos]
            pos += 1
        if rows is not None:
            n_ri, n_vi, n_ro, n_vo = len(row_ins), len(vec_ins), len(row_outs), len(vec_outs)
            ri_refs, vi_refs = refs[pos:pos + n_ri], refs[pos + n_ri:pos + n_ri + n_vi]
            pos += n_ri + n_vi
            ro_refs, vo_refs = refs[pos:pos + n_ro], refs[pos + n_ro:pos + n_ro + n_vo]
            pos += n_ro + n_vo - 1
            row_i = pl.program_id(0)
        out = refs[pos]
        if act == "relu2":
            pos += 1
            out_relu = refs[pos]
        scratch = refs[pos + 1:]

        part = lax.dot_general(a_ref[...].astype(BF16), b_ref[...].astype(BF16), dims,
                               preferred_element_type=F32)

        def finish(r):
            if rows is not None:
                ro, vo = rows["fn"](r, *[t[...] for t in ri_refs], *[t[...] for t in vi_refs])
                for ref, val in zip(ro_refs, ro):
                    ref[...] = val.astype(ref.dtype)
                for ref, val in zip(vo_refs, vo):
                    _acc_rows(ref, row_i, val)
                return
            if acc_ref is not None:
                r = r + acc_ref[...]
            if act == "relu2":
                r = jnp.maximum(r, 0.0)
                out_relu[...] = r.astype(out_relu.dtype)
                r = jnp.square(r)
            elif act == "relu2_bwd":
                r = r * (2.0 * act_ref[...].astype(F32))
            out[...] = r.astype(out.dtype)

        if nk == 1:
            finish(part)
        else:
            acc_sc = scratch[0]
            k = pl.program_id(2)

            @pl.when(k == 0)
            def _():
                acc_sc[...] = part

            @pl.when(k > 0)
            def _():
                acc_sc[...] += part

            @pl.when(k == nk - 1)
            def _():
                finish(acc_sc[...])

    a_spec = pl.BlockSpec((tk, tm), lambda i, j, k: (k, i)) if ta else pl.BlockSpec((tm, tk), lambda i, j, k: (i, k))
    if b_slots == "n":
        b_spec = pl.BlockSpec((None, tk, tn), lambda i, j, k: (j, k, 0))
    elif b_slots == "k":
        b_spec = pl.BlockSpec((None, tn, tk), lambda i, j, k: (k, j, 0))
    else:
        b_spec = pl.BlockSpec((tn, tk), lambda i, j, k: (j, k)) if tb else pl.BlockSpec((tk, tn), lambda i, j, k: (k, j))
    if out_slots == "m":
        o_spec = pl.BlockSpec((None, None, tm, tn), lambda i, j, k: (i % 2, i // 2, 0, j))
        o_shape = (2, N_CHIP, tm, N)
    elif out_slots == "n":
        o_spec = pl.BlockSpec((None, None, tm, tn), lambda i, j, k: (j % 2, j // 2, i, 0))
        o_shape = (2, N_CHIP, M, tn)
    else:
        o_spec = pl.BlockSpec((tm, tn), lambda i, j, k: (i, j))
        o_shape = (M, N)
    in_specs = [a_spec, b_spec]
    args = [a, b]
    if acc is not None:
        in_specs.append(o_spec)
        args.append(acc)
    if act_in is not None:
        in_specs.append(o_spec)
        args.append(act_in)
    out_specs, out_shapes, sem = [o_spec], [jax.ShapeDtypeStruct(o_shape, out_dtype)], ("parallel", "parallel", "arbitrary")
    if act == "relu2":
        out_specs, out_shapes = out_specs * 2, out_shapes * 2
    if rows is not None:
        def vec_spec(w):
            return pl.BlockSpec((1, w), lambda i, j, k: (0, 0))

        in_specs += [o_spec] * len(row_ins) + [vec_spec(t.shape[1]) for t in vec_ins]
        args += row_ins + vec_ins
        out_specs = [o_spec] * len(row_outs) + [vec_spec(w) for w in vec_outs]
        out_shapes = ([jax.ShapeDtypeStruct((M, N), d) for d in row_outs]
                      + [jax.ShapeDtypeStruct((1, w), F32) for w in vec_outs])
        sem = ("arbitrary",) * 3
    outs, c_outs = _call(
        body, name=name, grid=(M // tm, N // tn, nk), in_specs=in_specs, out_specs=out_specs, out_shape=out_shapes,
        scratch_shapes=[pltpu.VMEM((tm, tn), F32)] if nk > 1 else [], args=args, sem=sem, comm=comm)
    if rows is not None:
        return (*outs, c_outs) if comm is not None else tuple(outs)
    if act == "relu2":
        assert comm is None
        return outs[0], outs[1]
    return (outs[0], c_outs) if comm is not None else outs[0]


ROW_TILE = 256


def _rstd(xv):
    return lax.rsqrt(jnp.mean(xv * xv, axis=-1, keepdims=True) + EPS)


def _rms_bwd_rows(dy, xv, g):
    r = _rstd(xv)
    dyg = dy * g
    dx = r * dyg - xv * (r * r * r) * jnp.mean(dyg * xv, axis=-1, keepdims=True)
    return dx, dy * xv * r


def _row_spec(tm, n, col=0):
    return pl.BlockSpec((tm, n), lambda i: (i, col))


def _vec_spec(n):
    return pl.BlockSpec((1, n), lambda i: (0, 0))


def _acc_rows(ref, i, val):
    @pl.when(i == 0)
    def _():
        ref[...] = val

    @pl.when(i > 0)
    def _():
        ref[...] += val


def rms_fwd(x, g, *, name, n=None, col=0, out_dtype=BF16, comm=None):
    T = x.shape[0]
    n = x.shape[1] if n is None else n
    tm = _pick(T, ROW_TILE)

    def body(x_ref, g_ref, o_ref):
        xv = x_ref[...].astype(F32)
        o_ref[...] = (xv * _rstd(xv) * g_ref[...]).astype(o_ref.dtype)

    outs, c_outs = _call(
        body, name=name, grid=(T // tm,), in_specs=[_row_spec(tm, n, col), _vec_spec(n)],
        out_specs=[_row_spec(tm, n)], out_shape=[jax.ShapeDtypeStruct((T, n), out_dtype)],
        scratch_shapes=[], args=(x, g), sem=("parallel",), comm=comm)
    return (outs[0], c_outs) if comm is not None else outs[0]


def rms_bwd(dy, x, g, *, name, n=None, col=0, out_dtype=F32, comm=None):
    T = x.shape[0]
    n = x.shape[1] if n is None else n
    tm = _pick(T, ROW_TILE)

    def body(dy_ref, x_ref, g_ref, dx_ref, dg_ref):
        dx, dgr = _rms_bwd_rows(dy_ref[...].astype(F32), x_ref[...].astype(F32), g_ref[...])
        dx_ref[...] = dx.astype(dx_ref.dtype)
        _acc_rows(dg_ref, pl.program_id(0), jnp.sum(dgr, axis=0, keepdims=True))

    outs, c_outs = _call(
        body, name=name, grid=(T // tm,), in_specs=[_row_spec(tm, n), _row_spec(tm, n, col), _vec_spec(n)],
        out_specs=(_row_spec(tm, n), _vec_spec(n)),
        out_shape=(jax.ShapeDtypeStruct((T, n), out_dtype), jax.ShapeDtypeStruct((1, n), F32)),
        scratch_shapes=[], args=(dy, x, g), sem=("arbitrary",), comm=comm)
    return (*outs, c_outs) if comm is not None else tuple(outs)


def _mix_post_rows(y, x, g_post, g_pre2):
    x1 = x + y * _rstd(y) * g_post
    return [y, x1, x1 * _rstd(x1) * g_pre2], []


def _loss_rows(d, x1, tgt, g):
    err = x1 + d * _rstd(d) * g - tgt
    part = 0.5 * jnp.sum(jnp.mean(err * err, axis=-1, keepdims=True), axis=0, keepdims=True)
    dx2 = err * (1.0 / d.shape[-1])
    dd, dgr = _rms_bwd_rows(dx2, d, g)
    return [dx2, dd], [jnp.broadcast_to(part, (1, LANES)), jnp.sum(dgr, axis=0, keepdims=True)]


def _mlp_pre_bwd_rows(dh2, dx2, x1, y, g_pre2, g_post):
    d1, dg2 = _rms_bwd_rows(dh2, x1, g_pre2)
    dx1 = dx2 + d1
    dy, dgp = _rms_bwd_rows(dx1, y, g_post)
    return [dx1, dy], [jnp.sum(dg2, axis=0, keepdims=True), jnp.sum(dgp, axis=0, keepdims=True)]


def _mix_pre_bwd_rows(dh, x, dx1, g):
    dx, dgr = _rms_bwd_rows(dh, x, g)
    return [dx + dx1], [jnp.sum(dgr, axis=0, keepdims=True)]


def _gate_specs(tm):
    gcol = PAD_OFF["gate"] // D_MODEL
    lspecs = [pl.BlockSpec((tm, D_MODEL), functools.partial(lambda i, c: (i, c), c=gcol + b)) for b in range(N_BRANCH)]
    bspecs = [pl.BlockSpec((1, D_MODEL), functools.partial(lambda i, c: (0, c), c=b)) for b in range(N_BRANCH)]
    pspecs = [_row_spec(tm, D_MODEL) for _ in range(N_BRANCH)]
    return lspecs, bspecs, pspecs


def gate_merge(proj, b_gate, ps, *, name):
    T = proj.shape[0]
    tm = _pick(T, ROW_TILE)
    lspecs, bspecs, pspecs = _gate_specs(tm)

    def body(*refs):
        l_refs, b_refs, p_refs, o_ref = refs[0:3], refs[3:6], refs[6:9], refs[9]
        tot = None
        for lr, br, pr in zip(l_refs, b_refs, p_refs):
            term = jax.nn.sigmoid(lr[...].astype(F32) + br[...]) * pr[...].astype(F32)
            tot = term if tot is None else tot + term
        o_ref[...] = tot.astype(o_ref.dtype)

    return pl.pallas_call(
        body, name=name, grid=(T // tm,), in_specs=lspecs + bspecs + pspecs,
        out_specs=_row_spec(tm, D_MODEL), out_shape=jax.ShapeDtypeStruct((T, D_MODEL), BF16),
        compiler_params=_cparams(("parallel",)),
    )(proj, proj, proj, b_gate, b_gate, b_gate, *ps)


def gate_bwd(dmerged, proj, b_gate, ps, *, name):
    T = proj.shape[0]
    tm = _pick(T, ROW_TILE)
    lspecs, bspecs, pspecs = _gate_specs(tm)

    def body(*refs):
        dm_ref = refs[0]
        l_refs, b_refs, p_refs = refs[1:4], refs[4:7], refs[7:10]
        dl_ref, dp_refs, db_ref = refs[10], refs[11:14], refs[14]
        i = pl.program_id(0)
        dm = dm_ref[...]
        for b, (lr, br, pr, dpr) in enumerate(zip(l_refs, b_refs, p_refs, dp_refs)):
            gt = jax.nn.sigmoid(lr[...].astype(F32) + br[...])
            dpr[...] = (dm * gt).astype(dpr.dtype)
            dl = dm * pr[...].astype(F32) * gt * (1.0 - gt)
            dl_ref[:, b * D_MODEL:(b + 1) * D_MODEL] = dl.astype(dl_ref.dtype)
            part = jnp.sum(dl, axis=0, keepdims=True)

            @pl.when(i == 0)
            def _():
                db_ref[:, b * D_MODEL:(b + 1) * D_MODEL] = part

            @pl.when(i > 0)
            def _():
                db_ref[:, b * D_MODEL:(b + 1) * D_MODEL] += part

    return pl.pallas_call(
        body, name=name, grid=(T // tm,), in_specs=[_row_spec(tm, D_MODEL)] + lspecs + bspecs + pspecs,
        out_specs=(_row_spec(tm, GATE_W), *[_row_spec(tm, D_MODEL) for _ in range(N_BRANCH)], _vec_spec(GATE_W)),
        out_shape=(jax.ShapeDtypeStruct((T, GATE_W), BF16),
                   *[jax.ShapeDtypeStruct((T, D_MODEL), BF16) for _ in range(N_BRANCH)],
                   jax.ShapeDtypeStruct((1, GATE_W), F32)),
        compiler_params=_cparams(("arbitrary",)),
    )(dmerged, proj, proj, proj, b_gate, b_gate, b_gate, *ps)


def _rope_tables(pos_ref, invf_ref):
    ang = pos_ref[...] * invf_ref[...]
    lane = lax.broadcasted_iota(jnp.int32, (1, LANES), 1)
    s = jnp.sin(ang)
    split = MLA_NOPE + MLA_ROPE // 2
    return jnp.cos(ang), jnp.where(lane >= split, s, 0.0), jnp.where(lane < split, s, 0.0)


def _rotate(xh, c, s_hi, s_lo, sign):
    half = MLA_ROPE // 2
    up = pltpu.roll(xh, half, 1)
    down = pltpu.roll(xh, LANES - half, 1)
    return xh * c + sign * (up * s_hi - down * s_lo)


def rope_fwd(qpre, kpre, proj, pos, invf, *, name):
    T = qpre.shape[0]
    tm = _pick(T, ROW_TILE)
    kr_col = PAD_OFF["kr"] // KR_PAD

    def body(q_ref, k_ref, kr_ref, pos_ref, invf_ref, qo_ref, ko_ref):
        c, s_hi, s_lo = _rope_tables(pos_ref, invf_ref)
        kr = _rotate(pltpu.roll(kr_ref[...].astype(F32), MLA_NOPE, 1), c, s_hi, s_lo, 1.0)
        for h in range(MLA_HEADS):
            sl = slice(h * HEAD_PAD, (h + 1) * HEAD_PAD)
            qo_ref[:, sl] = _rotate(q_ref[:, sl].astype(F32), c, s_hi, s_lo, 1.0).astype(qo_ref.dtype)
            ko_ref[:, sl] = (k_ref[:, sl].astype(F32) + kr).astype(ko_ref.dtype)

    rs = _row_spec(tm, MLA_PAD_W)
    return pl.pallas_call(
        body, name=name, grid=(T // tm,),
        in_specs=[rs, rs, _row_spec(tm, KR_PAD, kr_col), _row_spec(tm, 1), _vec_spec(LANES)],
        out_specs=(rs, rs),
        out_shape=(jax.ShapeDtypeStruct((T, MLA_PAD_W), BF16), jax.ShapeDtypeStruct((T, MLA_PAD_W), BF16)),
        compiler_params=_cparams(("parallel",)),
    )(qpre, kpre, proj, pos, invf)


def rope_bwd(dq_pad, dk_pad, pos, invf, *, name):
    T = dq_pad.shape[0]
    tm = _pick(T, ROW_TILE)

    def body(dq_ref, dk_ref, pos_ref, invf_ref, dqo_ref, dkr_ref):
        c, s_hi, s_lo = _rope_tables(pos_ref, invf_ref)
        tot = None
        for h in range(MLA_HEADS):
            sl = slice(h * HEAD_PAD, (h + 1) * HEAD_PAD)
            dqo_ref[:, sl] = _rotate(dq_ref[:, sl].astype(F32), c, s_hi, s_lo, -1.0).astype(dqo_ref.dtype)
            tot = dk_ref[:, sl] if tot is None else tot + dk_ref[:, sl]
        dkr = pltpu.roll(_rotate(tot, c, s_hi, s_lo, -1.0), LANES - MLA_NOPE, 1)
        lane = lax.broadcasted_iota(jnp.int32, (1, LANES), 1)
        dkr_ref[...] = jnp.where(lane < MLA_ROPE, dkr, 0.0).astype(dkr_ref.dtype)

    rs = _row_spec(tm, MLA_PAD_W)
    return pl.pallas_call(
        body, name=name, grid=(T // tm,),
        in_specs=[rs, rs, _row_spec(tm, 1), _vec_spec(LANES)],
        out_specs=(rs, _row_spec(tm, KR_PAD)),
        out_shape=(jax.ShapeDtypeStruct((T, MLA_PAD_W), BF16), jax.ShapeDtypeStruct((T, KR_PAD), BF16)),
        compiler_params=_cparams(("parallel",)),
    )(dq_pad, dk_pad, pos, invf)


ATT_TILE = 512
NEG = -1e30


def _split_bf16(v):
    hi = v.astype(BF16)
    return hi, (v - hi.astype(F32)).astype(BF16)


def _mul(x, s):
    return x if s == 1.0 else x * s


def _dot(a, b, ca, cb):
    return lax.dot_general(a, b, (((ca,), (cb,)), ((), ())), preferred_element_type=F32)


CUM_CHUNK = 256


def _tri(kind, terms=2):
    r = lax.broadcasted_iota(jnp.int32, (CUM_CHUNK, CUM_CHUNK), 0)
    c = lax.broadcasted_iota(jnp.int32, (CUM_CHUNK, CUM_CHUNK), 1)
    m = {"gt": r > c, "le": r <= c, "lt": r < c}[kind]
    t = jnp.where(m, 1.0, 0.0).astype(BF16)
    return jnp.concatenate([t] * terms, axis=0)


def _cum(v, tri, kind):
    n = v.shape[1] // CUM_CHUNK
    two = tri.shape[0] == 2 * CUM_CHUNK
    chunks = [v[:, c * CUM_CHUNK:(c + 1) * CUM_CHUNK] for c in range(n)]
    totals = [jnp.sum(ch, axis=1, keepdims=True) for ch in chunks] if n > 1 else None
    outs = []
    for c, ch in enumerate(chunks):
        r = _dot(jnp.concatenate(_split_bf16(ch), axis=1) if two else ch.astype(BF16), tri, 1, 0)
        others = [] if n == 1 else totals[c + 1:] if kind == "gt" else totals[:c]
        for t in others:
            r = r + t
        outs.append(r)
    return outs[0] if n == 1 else jnp.concatenate(outs, axis=1)


def _lane_masks(hp, w):
    lane = lax.broadcasted_iota(jnp.int32, (1, LANES), 1)
    return [(lane >= e * w) & (lane < (e + 1) * w) for e in range(hp)]


def _att_dims(mode, hp, qk_w, v_w):
    assert mode in ("softmax", "sb")
    assert (hp, qk_w, v_w) in ((2, 128, 64), (2, 64, 64), (1, 128, 128))
    qw = hp * LANES if qk_w == LANES else LANES
    return qw


def attn_fwd(q, k, v, *, name, mode, causal, hp, qk_w, v_w, scale, nh, qc, kc, vc, comm=None):
    qw = _att_dims(mode, hp, qk_w, v_w)
    B, S = q.shape[0], q.shape[1]
    Sk = k.shape[1]
    tq = _pick(S, ATT_TILE)
    tk = _pick(Sk, ATT_TILE)
    if causal:
        assert tq == tk and S == Sk
    nq, nkv = S // tq, Sk // tk
    pre_scaled = math.frexp(scale)[0] == 0.5
    post = 1.0 if pre_scaled else scale

    def body(q_ref, k_ref, v_ref, o_ref, st_ref):
        i = pl.program_id(2)
        qv = q_ref[0].astype(BF16)
        masks_qk = _lane_masks(hp, qk_w) if qk_w < LANES else None
        masks_v = _lane_masks(hp, v_w) if v_w < LANES else None
        if qk_w == LANES:
            qs = [qv[:, e * LANES:(e + 1) * LANES] for e in range(hp)]
        else:
            qs = [jnp.where(masks_qk[e], qv, jnp.zeros_like(qv)) for e in range(hp)]
        if pre_scaled:
            qs = [t * scale for t in qs]
        rows = lax.broadcasted_iota(jnp.int32, (tq, tk), 0)
        cols = lax.broadcasted_iota(jnp.int32, (tq, tk), 1)
        tri = _tri("gt") if mode == "sb" else None

        def kv_tile(j):
            off = pl.multiple_of(j * tk, tk)
            kj = k_ref[0, pl.ds(off, tk), :].astype(BF16)
            vj = v_ref[0, pl.ds(off, tk), :].astype(BF16)
            ks = [kj[:, e * LANES:(e + 1) * LANES] for e in range(hp)] if qk_w == LANES else [kj] * hp
            return ks, vj

        def merge(vals):
            if hp == 1:
                return jnp.broadcast_to(vals[0], (tq, LANES))
            return jnp.where(masks_v[0], vals[0], vals[1])

        if mode == "softmax":
            def block(j, carry, diag):
                ms, ls, acc = carry
                ks, vj = kv_tile(j)
                new_m, new_l, alphas, pvs = [], [], [], []
                for e in range(hp):
                    s = _mul(_dot(qs[e], ks[e], 1, 1), post)
                    if diag:
                        s = jnp.where(rows >= cols, s, NEG)
                    m_new = jnp.maximum(ms[e], jnp.max(s, axis=1, keepdims=True))
                    alpha = jnp.exp(ms[e] - m_new)
                    p = jnp.exp(s - m_new)
                    new_l.append(alpha * ls[e] + jnp.sum(p, axis=1, keepdims=True))
                    new_m.append(m_new)
                    alphas.append(alpha)
                    pvs.append(_dot(p.astype(BF16), vj, 1, 0))
                acc = acc * merge(alphas) + merge(pvs)
                return tuple(new_m), tuple(new_l), acc

            init = (tuple(jnp.full((tq, 1), NEG, F32) for _ in range(hp)),
                    tuple(jnp.zeros((tq, 1), F32) for _ in range(hp)),
                    jnp.zeros((tq, LANES), F32))
            if causal:
                carry = lax.fori_loop(0, i, lambda j, c: block(j, c, False), init)
                ms, ls, acc = block(i, carry, True)
            else:
                ms, ls, acc = lax.fori_loop(0, nkv, lambda j, c: block(j, c, False), init)
            o_ref[0] = acc / merge(list(ls))
            st_ref[0, 0] = merge([m + jnp.log(l) for m, l in zip(ms, ls)])
        else:
            def block(j, carry, diag):
                cs_, acc = carry
                ks, vj = kv_tile(j)
                new_c, pvs = [], []
                for e in range(hp):
                    z = _mul(_dot(qs[e], ks[e], 1, 1), post)
                    lk = -jnp.maximum(z, 0.0) - jnp.log(1.0 + jnp.exp(-jnp.abs(z)))
                    lz = lk + z
                    if diag:
                        lk = jnp.where(rows > cols, lk, 0.0)
                    a = jnp.exp(lz + _cum(lk, tri, "gt") + cs_[e])
                    if diag:
                        a = jnp.where(rows > cols, a, 0.0)
                    pvs.append(_dot(a.astype(BF16), vj, 1, 0))
                    new_c.append(cs_[e] + jnp.sum(lk, axis=1, keepdims=True))
                return tuple(new_c), acc + merge(pvs)

            init = (tuple(jnp.zeros((tq, 1), F32) for _ in range(hp)), jnp.zeros((tq, LANES), F32))
            carry = block(i, init, True)
            cs_, acc = lax.fori_loop(0, i, lambda jj, c: block(i - 1 - jj, c, False), carry)
            o_ref[0] = acc
            st_ref[0, 0] = merge(list(cs_))

    outs, c_outs = _call(
        body, name=name, grid=(B, nh, nq),
        in_specs=[pl.BlockSpec((1, tq, qw), lambda b, h, i: (b, i, qc + h)),
                  pl.BlockSpec((1, Sk, qw), lambda b, h, i: (b, 0, kc + h)),
                  pl.BlockSpec((1, Sk, LANES), lambda b, h, i: (b, 0, vc + h))],
        out_specs=(pl.BlockSpec((1, tq, LANES), lambda b, h, i: (b, i, h)),
                   pl.BlockSpec((1, 1, tq, LANES), lambda b, h, i: (b, h, i, 0))),
        out_shape=(jax.ShapeDtypeStruct((B, S, nh * LANES), F32), jax.ShapeDtypeStruct((B, nh, S, LANES), F32)),
        scratch_shapes=[], args=(q, k, v), sem=("parallel", "parallel", "arbitrary"), comm=comm)
    return (*outs, c_outs) if comm is not None else tuple(outs)


def attn_bwd(q, k, v, o, st, do, *, name, mode, causal, hp, qk_w, v_w, scale, nh, qc, kc, vc, comm=None):
    qw = _att_dims(mode, hp, qk_w, v_w)
    B, S = q.shape[0], q.shape[1]
    Sk = k.shape[1]
    tq = _pick(S, ATT_TILE)
    tk = _pick(Sk, ATT_TILE)
    if causal:
        assert tq == tk and S == Sk
    nq, nkv = S // tq, Sk // tk
    pre_scaled = math.frexp(scale)[0] == 0.5
    post = 1.0 if pre_scaled else scale

    def body(q_ref, k_ref, v_ref, o_ref, st_ref, do_ref, dq_ref, dk_ref, dv_ref):
        i = pl.program_id(2)

        @pl.when(i == 0)
        def _():
            dk_ref[...] = jnp.zeros_like(dk_ref)
            dv_ref[...] = jnp.zeros_like(dv_ref)

        qv = q_ref[0].astype(BF16)
        dov = do_ref[0]
        stv = st_ref[0, 0]
        masks_qk = _lane_masks(hp, qk_w) if qk_w < LANES else None
        masks_v = _lane_masks(hp, v_w) if v_w < LANES else None
        if qk_w == LANES:
            qs = [qv[:, e * LANES:(e + 1) * LANES] for e in range(hp)]
        else:
            qs = [jnp.where(masks_qk[e], qv, jnp.zeros_like(qv)) for e in range(hp)]
        if pre_scaled:
            qs = [t * scale for t in qs]
        if hp == 1:
            dos = [dov.astype(BF16)]
            stats = [stv[:, 0:1]]
        else:
            dos = [jnp.where(masks_v[e], dov, 0.0).astype(BF16) for e in range(hp)]
            stats = [stv[:, e * v_w:e * v_w + 1] for e in range(hp)]
        if mode == "softmax":
            prod = dov * o_ref[0]
            if hp == 1:
                dsum = [jnp.sum(prod, axis=1, keepdims=True)]
            else:
                dsum = [jnp.sum(jnp.where(masks_v[e], prod, 0.0), axis=1, keepdims=True) for e in range(hp)]
        rows = lax.broadcasted_iota(jnp.int32, (tq, tk), 0)
        cols = lax.broadcasted_iota(jnp.int32, (tq, tk), 1)
        if mode == "sb":
            tri_le, tri_lt = _tri("le"), _tri("lt", terms=1)

        def kv_tile(j):
            off = pl.multiple_of(j * tk, tk)
            kj = k_ref[0, pl.ds(off, tk), :].astype(BF16)
            vj = v_ref[0, pl.ds(off, tk), :].astype(BF16)
            ks = [kj[:, e * LANES:(e + 1) * LANES] for e in range(hp)] if qk_w == LANES else [kj] * hp
            return off, ks, vj

        def scatter(off, dz_list, p_list):
            dvj = None
            for e in range(hp):
                t = _dot(p_list[e], dos[e], 0, 0)
                dvj = t if dvj is None else dvj + t
            dv_ref[0, pl.ds(off, tk), :] += dvj
            if qk_w == LANES:
                for e in range(hp):
                    dk_ref[0, pl.ds(off, tk), e * LANES:(e + 1) * LANES] += _dot(dz_list[e], qs[e], 0, 0)
            else:
                dkj = None
                for e in range(hp):
                    t = _dot(dz_list[e], qs[e], 0, 0)
                    dkj = t if dkj is None else dkj + t
                dk_ref[0, pl.ds(off, tk), :] += dkj

        def dq_add(dqs, dz_list, ks):
            out = []
            for e in range(hp):
                out.append(dqs[e] + _dot(dz_list[e], ks[e], 1, 0))
            return tuple(out)

        dq0 = tuple(jnp.zeros((tq, LANES), F32) for _ in range(hp))

        if mode == "softmax":
            def block(j, dqs, diag):
                off, ks, vj = kv_tile(j)
                dzs, ps = [], []
                for e in range(hp):
                    s = _mul(_dot(qs[e], ks[e], 1, 1), post)
                    p = jnp.exp(s - stats[e])
                    if diag:
                        p = jnp.where(rows >= cols, p, 0.0)
                    dp = _dot(dos[e], vj, 1, 1)
                    dzs.append(_mul(p * (dp - dsum[e]), post).astype(BF16))
                    ps.append(p.astype(BF16))
                scatter(off, dzs, ps)
                return dq_add(dqs, dzs, ks)

            if causal:
                dqs = lax.fori_loop(0, i, lambda j, c: block(j, c, False), dq0)
                dqs = block(i, dqs, True)
            else:
                dqs = lax.fori_loop(0, nkv, lambda j, c: block(j, c, False), dq0)
        else:
            def block(j, carry, diag):
                dqs, cps, cgs = carry
                off, ks, vj = kv_tile(j)
                dzs, ps, new_p, new_g = [], [], [], []
                for e in range(hp):
                    z = _mul(_dot(qs[e], ks[e], 1, 1), post)
                    lk = -jnp.maximum(z, 0.0) - jnp.log(1.0 + jnp.exp(-jnp.abs(z)))
                    lz = lk + z
                    sig = jnp.exp(lz)
                    keep = 1.0 - sig
                    if diag:
                        lk = jnp.where(rows > cols, lk, 0.0)
                    keep_after = stats[e] - cps[e] - _cum(lk, tri_le, "le")
                    a = jnp.exp(lz + keep_after)
                    if diag:
                        a = jnp.where(rows > cols, a, 0.0)
                    g = _dot(dos[e], vj, 1, 1) * a
                    gsum = cgs[e] + _cum(g, tri_lt, "lt")
                    dz = _mul(g * keep - gsum * sig, post)
                    if diag:
                        dz = jnp.where(rows > cols, dz, 0.0)
                    dzs.append(dz.astype(BF16))
                    ps.append(a.astype(BF16))
                    new_p.append(cps[e] + jnp.sum(lk, axis=1, keepdims=True))
                    new_g.append(cgs[e] + jnp.sum(g, axis=1, keepdims=True))
                scatter(off, dzs, ps)
                return dq_add(dqs, dzs, ks), tuple(new_p), tuple(new_g)

            zc = tuple(jnp.zeros((tq, 1), F32) for _ in range(hp))
            carry = lax.fori_loop(0, i, lambda j, c: block(j, c, False), (dq0, zc, zc))
            dqs, _, _ = block(i, carry, True)

        if pre_scaled:
            dqs = [t * scale for t in dqs]
        if qk_w == LANES:
            for e in range(hp):
                dq_ref[0, :, e * LANES:(e + 1) * LANES] = dqs[e].astype(dq_ref.dtype)
        else:
            dq_ref[0] = jnp.where(masks_qk[0], dqs[0], dqs[1]).astype(dq_ref.dtype)

    outs, c_outs = _call(
        body, name=name, grid=(B, nh, nq),
        in_specs=[pl.BlockSpec((1, tq, qw), lambda b, h, i: (b, i, qc + h)),
                  pl.BlockSpec((1, Sk, qw), lambda b, h, i: (b, 0, kc + h)),
                  pl.BlockSpec((1, Sk, LANES), lambda b, h, i: (b, 0, vc + h)),
                  pl.BlockSpec((1, tq, LANES), lambda b, h, i: (b, i, h)),
                  pl.BlockSpec((1, 1, tq, LANES), lambda b, h, i: (b, h, i, 0)),
                  pl.BlockSpec((1, tq, LANES), lambda b, h, i: (b, i, h))],
        out_specs=(pl.BlockSpec((1, tq, qw), lambda b, h, i: (b, i, h)),
                   pl.BlockSpec((1, Sk, qw), lambda b, h, i: (b, 0, h)),
                   pl.BlockSpec((1, Sk, LANES), lambda b, h, i: (b, 0, h))),
        out_shape=(jax.ShapeDtypeStruct((B, S, nh * qw), BF16), jax.ShapeDtypeStruct((B, Sk, nh * qw), F32),
                   jax.ShapeDtypeStruct((B, Sk, nh * LANES), F32)),
        scratch_shapes=[], args=(q, k, v, o, st, do), sem=("parallel", "parallel", "arbitrary"), comm=comm)
    return (*outs, c_outs) if comm is not None else tuple(outs)


def _pos(p):
    return 4 * p[0] + 2 * p[1] + p[2]


N_CHIP = 4
MESH_ID = pl.DeviceIdType.MESH
_ANY = pl.BlockSpec(memory_space=pl.ANY)


def _me():
    return lax.axis_index("x"), lax.axis_index("y"), lax.axis_index("c")


def _other_chips(x, y):
    return [(1 - x, y), (x, 1 - y), (1 - x, 1 - y)]


class _Comm:
    def __init__(self, arrs, out_shapes, n_sem, n_local, start, finish):
        self.arrs, self.out_shapes, self.start, self.finish = list(arrs), list(out_shapes), start, finish
        self.scratch = [pltpu.SemaphoreType.DMA((n_sem,)), pltpu.SemaphoreType.DMA((n_sem,)),
                        pltpu.SemaphoreType.DMA((max(n_local, 1),))]


def _run_comm(comm, name):
    n = len(comm.arrs)

    def body(*refs):
        r = (refs[:n], refs[n:2 * n], refs[2 * n], refs[2 * n + 1], refs[2 * n + 2])
        comm.start(*r)
        comm.finish(*r)

    return pl.pallas_call(
        body, name=name, in_specs=[_ANY] * n, out_specs=[_ANY] * n, out_shape=comm.out_shapes,
        scratch_shapes=comm.scratch, compiler_params=pltpu.CompilerParams(has_side_effects=True),
    )(*comm.arrs)


def _call(body, *, name, grid, in_specs, out_specs, out_shape, scratch_shapes, args, sem, comm=None):
    in_specs, out_specs, out_shape = list(in_specs), list(out_specs), list(out_shape)
    if comm is None:
        res = pl.pallas_call(body, name=name, grid=grid, in_specs=in_specs, out_specs=out_specs, out_shape=out_shape,
                             scratch_shapes=scratch_shapes, compiler_params=_cparams(sem))(*args)
        return list(res), []
    n_in, n_out, n_scr, nc = len(in_specs), len(out_specs), len(scratch_shapes), len(comm.arrs)

    def wrapped(*refs):
        ins, refs = refs[:n_in], refs[n_in:]
        c_in, refs = refs[:nc], refs[nc:]
        outs, refs = refs[:n_out], refs[n_out:]
        c_out, refs = refs[:nc], refs[nc:]
        scr, sems = refs[:n_scr], refs[n_scr:]
        ids = [pl.program_id(a) for a in range(len(grid))]
        first = functools.reduce(jnp.logical_and, [i == 0 for i in ids])
        last = functools.reduce(jnp.logical_and, [i == g - 1 for i, g in zip(ids, grid)])

        @pl.when(first)
        def _():
            comm.start(c_in, c_out, *sems)

        body(*ins, *outs, *scr)

        @pl.when(last)
        def _():
            comm.finish(c_in, c_out, *sems)

    res = pl.pallas_call(
        wrapped, name=name, grid=grid, in_specs=in_specs + [_ANY] * nc, out_specs=out_specs + [_ANY] * nc,
        out_shape=out_shape + comm.out_shapes, scratch_shapes=list(scratch_shapes) + comm.scratch,
        compiler_params=pltpu.CompilerParams(dimension_semantics=("arbitrary",) * len(grid),
                                             vmem_limit_bytes=VMEM_LIMIT, has_side_effects=True),
    )(*args, *comm.arrs)
    return list(res[:n_out]), list(res[n_out:])


def gather_two_level(bufs, *, name=None):
    n = len(bufs)

    def parts(x_refs, out_refs, send_sems, recv_sems, local_sems):
        x, y, c = _me()
        me, sibling = (x, y, c), (x, y, 1 - c)
        chips = _other_chips(x, y)

        def copy(a, k, block, to, from_input=False):
            dst = out_refs[a].at[_pos(block)]
            return pltpu.make_async_remote_copy(src_ref=x_refs[a] if from_input else dst, dst_ref=dst,
                                                send_sem=send_sems.at[7 * a + k], recv_sem=recv_sems.at[7 * a + k],
                                                device_id=to, device_id_type=MESH_ID)

        mine = [pltpu.make_async_copy(x_refs[a], out_refs[a].at[_pos(me)], local_sems.at[a]) for a in range(n)]
        first = []
        for a in range(n):
            first.append(copy(a, 0, me, sibling, from_input=True))
            first += [copy(a, 1 + j, me, (*chip, c), from_input=True) for j, chip in enumerate(chips)]
        return copy, mine, first, me, sibling, chips, c

    def start(*refs):
        _, mine, first, *_ = parts(*refs)
        for cp in mine + first:
            cp.start()

    def finish(*refs):
        copy, mine, first, me, sibling, chips, c = parts(*refs)
        passed = []
        for j, chip in enumerate(chips):
            for a in range(n):
                copy(a, 1 + j, (*chip, c), me).wait_recv()
                passed.append(copy(a, 4 + j, (*chip, c), sibling))
                passed[-1].start()
        for a in range(n):
            copy(a, 0, sibling, me).wait_recv()
            for j, chip in enumerate(chips):
                copy(a, 4 + j, (*chip, 1 - c), me).wait_recv()
        for cp in first + passed:
            cp.wait_send()
        for cp in mine:
            cp.wait()

    out_shapes = [jax.ShapeDtypeStruct((N_DEV,) + b.shape, b.dtype) for b in bufs]
    comm = _Comm(bufs, out_shapes, 7 * n, n, start, finish)
    return _run_comm(comm, name) if name else comm


def sibling_exchange(sends, *, name=None):
    n = len(sends)

    def copies(s_refs, out_refs, send_sems, recv_sems, local_sems):
        x, y, c = _me()
        return [pltpu.make_async_remote_copy(src_ref=s_refs[a].at[1 - c], dst_ref=out_refs[a], send_sem=send_sems.at[a],
                                             recv_sem=recv_sems.at[a], device_id=(x, y, 1 - c), device_id_type=MESH_ID)
                for a in range(n)]

    def start(*refs):
        for cp in copies(*refs):
            cp.start()

    def finish(*refs):
        for cp in copies(*refs):
            cp.wait()

    out_shapes = [jax.ShapeDtypeStruct(s.shape[1:], s.dtype) for s in sends]
    comm = _Comm(sends, out_shapes, n, 0, start, finish)
    return _run_comm(comm, name) if name else comm


def _flat2(shape):
    return math.prod(shape[:-1]), shape[-1]


def _row_tile(r):
    return r if r <= 512 else _pick8(r, 256)


def _pick8(n, pref):
    t = pref
    while n % t:
        t -= 8
    return t


def chip_sum(send, got, core, *, name):
    shape = got.shape[1:]
    r, cdim = _flat2(shape)
    tr = _row_tile(r)
    send = send.reshape(2, N_CHIP, r, cdim)
    got = got.reshape(N_CHIP, r, cdim)

    def body(core_ref, s_ref, g_ref, o_ref):
        o_ref[...] = (s_ref[0].astype(F32) + g_ref[...].astype(F32)).astype(o_ref.dtype)

    blk = pl.BlockSpec((N_CHIP, tr, cdim), lambda i, core_ref: (0, i, 0))
    out = pl.pallas_call(
        body, name=name,
        grid_spec=pltpu.PrefetchScalarGridSpec(
            num_scalar_prefetch=1, grid=(r // tr,),
            in_specs=[pl.BlockSpec((1, N_CHIP, tr, cdim), lambda i, core_ref: (core_ref[0], 0, i, 0)), blk],
            out_specs=blk),
        out_shape=jax.ShapeDtypeStruct((N_CHIP, r, cdim), BF16),
        compiler_params=_cparams(("parallel",)),
    )(core, send, got)
    return out.reshape((N_CHIP,) + shape)


def chip_exchange(sums, *, name=None):
    n = len(sums)

    def copies(s_refs, out_refs, send_sems, recv_sems, local_sems):
        x, y, c = _me()
        mine = 2 * x + y
        chips = _other_chips(x, y)
        local = [pltpu.make_async_copy(s_refs[a].at[mine], out_refs[a].at[mine], local_sems.at[a]) for a in range(n)]
        sends, recvs = [], []
        for a in range(n):
            for j, (px, py) in enumerate(chips):
                sems = dict(send_sem=send_sems.at[3 * a + j], recv_sem=recv_sems.at[3 * a + j], device_id=(px, py, c),
                            device_id_type=MESH_ID)
                sends.append(pltpu.make_async_remote_copy(src_ref=s_refs[a].at[2 * px + py], dst_ref=out_refs[a].at[mine], **sems))
                recvs.append(pltpu.make_async_remote_copy(src_ref=s_refs[a].at[mine], dst_ref=out_refs[a].at[2 * px + py], **sems))
        return local, sends, recvs

    def start(*refs):
        local, sends, _ = copies(*refs)
        for cp in local + sends:
            cp.start()

    def finish(*refs):
        local, sends, recvs = copies(*refs)
        for cp in recvs:
            cp.wait_recv()
        for cp in sends:
            cp.wait_send()
        for cp in local:
            cp.wait()

    out_shapes = [jax.ShapeDtypeStruct(s.shape, s.dtype) for s in sums]
    comm = _Comm(sums, out_shapes, 3 * n, n, start, finish)
    return _run_comm(comm, name) if name else comm


def sum_adamw(parts, w, m, v, *, name):
    shape = w.shape
    R, cdim = _flat2(shape)
    n_slots = parts.shape[0]
    tr = _row_tile(R)
    parts = parts.reshape(n_slots, R, cdim)
    w, m, v = (t.reshape(1, R, cdim) for t in (w, m, v))
    c1 = 1.0 - ADAM_B1 ** ADAM_STEP
    c2 = 1.0 - ADAM_B2 ** ADAM_STEP

    def body(p_ref, w_ref, m_ref, v_ref, g_ref, d_ref, mo_ref, vo_ref):
        g = p_ref[0].astype(F32)
        for s in range(1, n_slots):
            g = g + p_ref[s].astype(F32)
        mn = ADAM_B1 * m_ref[...] + (1.0 - ADAM_B1) * g
        vn = ADAM_B2 * v_ref[...] + (1.0 - ADAM_B2) * jnp.square(g)
        g_ref[...] = g
        mo_ref[...] = mn
        vo_ref[...] = vn
        d_ref[...] = -ADAM_LR * ((mn / c1) / (jnp.sqrt(vn / c2) + ADAM_EPS) + ADAM_WD * w_ref[...])

    rs = pl.BlockSpec((None, tr, cdim), lambda i: (0, i, 0))
    sd = jax.ShapeDtypeStruct((1, R, cdim), F32)
    res = pl.pallas_call(
        body, name=name, grid=(R // tr,),
        in_specs=[pl.BlockSpec((n_slots, tr, cdim), lambda i: (0, i, 0)), rs, rs, rs],
        out_specs=(rs, rs, rs, rs), out_shape=(sd, sd, sd, sd),
        compiler_params=_cparams(("parallel",)),
    )(parts, w, m, v)
    return [t.reshape(shape) for t in res]


WEIGHTS = ("ln_mix_pre", "w_in", "b_gate", "q_norm", "w_uq", "kv_norm", "w_uk", "w_uv", "mem_norm", "w_mem_kv",
           "w_branch_out", "w_out", "ln_mix_post", "ln_mlp_pre", "w_mlp_up", "w_mlp_down", "ln_mlp_post")
BIG = dict(w_in=((D_MODEL, IN_WIDTH), 1), w_uq=((Q_LORA, MLA_HEADS * (MLA_NOPE + MLA_ROPE)), 1),
           w_uk=((KV_LORA, MLA_HEADS * MLA_NOPE), 1), w_uv=((KV_LORA, MLA_HEADS * MLA_V), 1),
           w_mem_kv=((D_MODEL, 2 * QM_W), 0), w_branch_out=((N_BRANCH, BRANCH_W, D_MODEL), 2),
           w_out=((D_MODEL, D_MODEL), 0), w_mlp_up=((D_MODEL, D_FF), 1), w_mlp_down=((D_FF, D_MODEL), 0))
SMALL = tuple(n for n in WEIGHTS if n not in BIG)


PACK_TILE = 512


def _pad_rows(a):
    r = a.shape[-2]
    to = PACK_TILE if r > PACK_TILE else 8
    pad = [(0, 0)] * a.ndim
    pad[-2] = (0, (-r) % to)
    return jnp.pad(a, pad)


def _pack_rows(arrs):
    return _pad_rows(jnp.concatenate([a.reshape(-1, LANES) for a in arrs], axis=0))


def _unpack_rows(packed, shapes, lead=()):
    out, r = [], 0
    for shp in shapes:
        n = math.prod(shp) // LANES
        out.append(packed[..., r:r + n, :].reshape(lead + tuple(shp)))
        r += n
    return out


def _slots_to_full(name, slots):
    full, ax = BIG[name]
    return jnp.moveaxis(slots, 0, ax).reshape(full)


def _full_to_slots(name, w):
    full, ax = BIG[name]
    split = full[:ax] + (N_DEV, full[ax] // N_DEV) + full[ax + 1:]
    return jnp.moveaxis(w.reshape(split), ax, 0)


def _full_to_owner(name, w):
    s = _full_to_slots(name, w)
    return jnp.swapaxes(s.reshape((N_CHIP, 2) + s.shape[1:]), 0, 1)


def _in_segments():
    orig = dict(zip(("cq", "ckv", "kr", "sb", "qm", "gate"), zip(np.cumsum((0,) + IN_SIZES[:-1]).tolist(), IN_SIZES)))
    shard = IN_WIDTH // N_DEV
    segs = []
    for n in PAD_ORDER:
        if n not in orig:
            continue
        o0, w = orig[n]
        c = o0
        while c < o0 + w:
            s = c // shard
            e = min(o0 + w, (s + 1) * shard)
            segs.append((s, c - s * shard, PAD_OFF[n] + (c - o0), e - c))
            c = e
    return segs


def in_slots_to_pad(slots, *, name):
    _, R, shard = slots.shape
    tr = _pick8(R, ROW_TILE)
    segs = _in_segments()

    def body(in_ref, out_ref):
        out_ref[...] = jnp.zeros_like(out_ref)
        for s, a, d, w in segs:
            out_ref[:, d:d + w] = in_ref[s, :, a:a + w]

    return pl.pallas_call(
        body, name=name, grid=(R // tr,), in_specs=[pl.BlockSpec((N_DEV, tr, shard), lambda i: (0, i, 0))],
        out_specs=pl.BlockSpec((tr, IN_PAD), lambda i: (i, 0)), out_shape=jax.ShapeDtypeStruct((R, IN_PAD), slots.dtype),
        compiler_params=_cparams(("parallel",)),
    )(slots)


def in_pad_to_owner(g, *, name):
    R = g.shape[0]
    shard = IN_WIDTH // N_DEV
    tr = _pick8(R, ROW_TILE)
    segs = _in_segments()

    def body(in_ref, out_ref):
        for s, a, d, w in segs:
            out_ref[s % 2, s // 2, :, a:a + w] = in_ref[:, d:d + w]

    return pl.pallas_call(
        body, name=name, grid=(R // tr,), in_specs=[pl.BlockSpec((tr, IN_PAD), lambda i: (i, 0))],
        out_specs=pl.BlockSpec((2, N_CHIP, tr, shard), lambda i: (0, 0, i, 0)),
        out_shape=jax.ShapeDtypeStruct((2, N_CHIP, R, shard), g.dtype),
        compiler_params=_cparams(("parallel",)),
    )(g)


def _pad_heads(w, width):
    r = w.shape[0]
    return jnp.pad(w.reshape(r, MLA_HEADS, width), ((0, 0), (0, 0), (0, HEAD_PAD - width))).reshape(r, MLA_PAD_W)


def _unpad_heads(wp, width):
    r = wp.shape[0]
    return wp.reshape(r, MLA_HEADS, HEAD_PAD)[:, :, :width].reshape(r, MLA_HEADS * width)


def _rope_inv_freq():
    half = MLA_ROPE // 2
    inv = 1.0 / (ROPE_THETA ** (jnp.arange(half, dtype=F32) * (2.0 / MLA_ROPE)))
    tab = jnp.zeros((LANES,), F32)
    tab = tab.at[MLA_NOPE:MLA_NOPE + half].set(inv).at[MLA_NOPE + half:MLA_NOPE + MLA_ROPE].set(inv)
    return tab.reshape(1, LANES)


def _as_tuple(r):
    return r if isinstance(r, tuple) else (r,)


class _Exchange:
    def __init__(self, w):
        self.w = w
        self.rest = tuple(n for n in BIG if n != "w_in")
        self.pending = {}
        self.reduced = {}

    def first_comm(self):
        return gather_two_level([self.w["w_in"].astype(BF16)])

    def first_weights(self, slots):
        return in_slots_to_pad(slots[0], name="w_in_layout")

    def rest_comm(self):
        return gather_two_level([self.w[n].astype(BF16) for n in self.rest])

    def rest_weights(self, slots):
        return {n: s if n == "w_mlp_up" else _slots_to_full(n, s) for n, s in zip(self.rest, slots)}

    def sibling_comm(self, tag, grads):
        self.pending[tag] = (tuple(grads), [grads[n] for n in grads])
        return sibling_exchange(self.pending[tag][1])

    def chip_comm(self, tag, gots):
        names, sends = self.pending[tag]
        core = lax.axis_index("c").astype(jnp.int32).reshape(1)
        sums = [chip_sum(s, g, core, name=f"chip_sum_{n}") for n, s, g in zip(names, sends, gots)]
        self.pending[tag] = names
        return chip_exchange(sums)

    def reduce_start(self, tag, grads):
        return self.chip_comm(tag, _run_comm(self.sibling_comm(tag, grads), f"sibling_{tag}"))

    def reduce_done(self, tag, recvs):
        self.reduced.update(zip(self.pending.pop(tag), recvs))


def _local_step(x, mem, positions, tgt, sm, ex):
    B, S, D = x.shape
    M = mem.shape[1]
    T = B * S
    x2 = x.reshape(T, D)
    mem2 = mem.reshape(B * M, D)
    pos = positions.reshape(T, 1).astype(F32)
    invf = _rope_inv_freq()
    cq_col, ckv_col = PAD_OFF["cq"] // Q_LORA, PAD_OFF["ckv"] // KV_LORA
    sb_col, qm_col = PAD_OFF["sb"] // LANES, PAD_OFF["qm"] // LANES
    sb_blk = SB_HEADS * SB_DIM // LANES
    mla = dict(mode="softmax", causal=True, hp=2, qk_w=128, v_w=64, scale=(MLA_NOPE + MLA_ROPE) ** -0.5,
               nh=MLA_HEADS // 2, qc=0, kc=0, vc=0)
    sbk = dict(mode="sb", causal=True, hp=2, qk_w=64, v_w=64, scale=SB_DIM ** -0.5, nh=SB_HEADS // 2,
               qc=sb_col, kc=sb_col + sb_blk, vc=sb_col + 2 * sb_blk)
    mca = dict(mode="softmax", causal=False, hp=1, qk_w=128, v_w=128, scale=MEM_DIM ** -0.5, nh=MEM_HEADS,
               qc=qm_col, kc=0, vc=MEM_HEADS)

    h, *slots = _as_tuple(rms_fwd(x2, sm["ln_mix_pre"], name="rms_mix_pre", comm=ex.first_comm()))
    w_in_pad = ex.first_weights(slots[0] if slots else None)
    proj = matmul(h, w_in_pad, out_dtype=BF16, name="mm_in")
    proj3 = proj.reshape(B, S, IN_PAD)
    o_sb, st_sb, *slots = attn_fwd(proj3, proj3, proj3, name="sb_fwd", comm=ex.rest_comm(), **sbk)
    W = ex.rest_weights(slots[0] if slots else None)
    w_uq_pad = _pad_heads(W["w_uq"], MLA_NOPE + MLA_ROPE)
    w_uk_pad = _pad_heads(W["w_uk"], MLA_NOPE)
    cqn = rms_fwd(proj, sm["q_norm"], n=Q_LORA, col=cq_col, name="rms_q")
    ckvn = rms_fwd(proj, sm["kv_norm"], n=KV_LORA, col=ckv_col, name="rms_kv")
    qpre = matmul(cqn, w_uq_pad, out_dtype=BF16, name="mm_uq")
    kpre = matmul(ckvn, w_uk_pad, out_dtype=BF16, name="mm_uk")
    v_mla = matmul(ckvn, W["w_uv"], out_dtype=BF16, name="mm_uv").reshape(B, S, -1)
    q_pad, k_pad = rope_fwd(qpre, kpre, proj, pos, invf, name="rope_fwd")
    q_pad, k_pad = q_pad.reshape(B, S, -1), k_pad.reshape(B, S, -1)
    o_mla, st_mla = attn_fwd(q_pad, k_pad, v_mla, name="mla_fwd", **mla)
    memh = rms_fwd(mem2, sm["mem_norm"], name="rms_mem")
    mkv = matmul(memh, W["w_mem_kv"], out_dtype=BF16, name="mm_memkv").reshape(B, M, -1)
    o_mem, st_mem = attn_fwd(proj3, mkv, mkv, name="mem_fwd", **mca)
    outs = [o.reshape(T, BRANCH_W) for o in (o_mla, o_sb, o_mem)]
    ps = [matmul(o, W["w_branch_out"][b], out_dtype=BF16, name=f"mm_bo{b}") for b, o in enumerate(outs)]
    merged = gate_merge(proj, sm["b_gate"], ps, name="gate_merge")
    y, x1, h2 = matmul(merged, W["w_out"], name="mm_out_norms",
                       rows=dict(fn=_mix_post_rows, row_ins=[x2], vec_ins=[sm["ln_mix_post"], sm["ln_mlp_pre"]],
                                 row_outs=[F32, F32, BF16], vec_outs=[]))
    u, relu_a = matmul(h2, W["w_mlp_up"], b_slots="n", act="relu2", out_dtype=BF16, name="mm_up")
    dx2, dd, loss_p, dg_mlp_post = matmul(
        u, W["w_mlp_down"], name="mm_down_loss",
        rows=dict(fn=_loss_rows, row_ins=[x1, tgt.reshape(T, D)], vec_ins=[sm["ln_mlp_post"]], row_outs=[F32, BF16],
                  vec_outs=[LANES, D]))

    da = matmul(dd, W["w_mlp_down"], tb=True, act="relu2_bwd", act_in=relu_a, out_dtype=BF16, name="mm_down_dx")
    g_down = matmul(u, dd, ta=True, out_dtype=BF16, out_slots="m", name="mm_down_dw")
    g_up = matmul(h2, da, ta=True, out_dtype=BF16, out_slots="n", name="mm_up_dw")
    sib = ex.sibling_comm("mlp", dict(w_mlp_down=g_down, w_mlp_up=g_up))
    dx1, dy, dg_mlp_pre, dg_mix_post, *got = matmul(
        da, _slots_to_full("w_mlp_up", W["w_mlp_up"]), tb=True, name="mm_up_dx_norms", comm=sib,
        rows=dict(fn=_mlp_pre_bwd_rows, row_ins=[dx2, x1, y], vec_ins=[sm["ln_mlp_pre"], sm["ln_mix_post"]],
                  row_outs=[F32, BF16], vec_outs=[D, D]))
    red_mlp = ex.chip_comm("mlp", got[0] if got else None)
    dmerged = matmul(dy, W["w_out"], tb=True, name="mm_out_dx")
    g_out = matmul(merged, dy, ta=True, out_dtype=BF16, name="mm_out_dw")
    dlog, dp0, dp1, dp2, db_gate = gate_bwd(dmerged, proj, sm["b_gate"], ps, name="gate_bwd")
    dps = (dp0, dp1, dp2)
    g_bo = jnp.stack([matmul(outs[b], dps[b], ta=True, out_dtype=BF16, name=f"mm_bo{b}_dw") for b in range(N_BRANCH)])
    dos = [matmul(dps[b], W["w_branch_out"][b], tb=True, out_dtype=BF16, name=f"mm_bo{b}_dx").reshape(B, S, BRANCH_W)
           for b in range(N_BRANCH)]
    dq_pad, dk_pad, dv_mla, *got = attn_bwd(q_pad, k_pad, v_mla, o_mla, st_mla, dos[0], name="mla_bwd", comm=red_mlp, **mla)
    ex.reduce_done("mlp", got[0] if got else None)
    dqm, dmk, dmv = attn_bwd(proj3, mkv, mkv, o_mem, st_mem, dos[2], name="mem_bwd", **mca)
    dmkv = jnp.concatenate([dmk, dmv], axis=-1).astype(BF16).reshape(B * M, -1)
    dmemh = matmul(dmkv, W["w_mem_kv"], tb=True, name="mm_memkv_dx")
    g_memkv = matmul(memh, dmkv, ta=True, out_dtype=BF16, name="mm_memkv_dw")
    _, dg_mem_norm = rms_bwd(dmemh, mem2, sm["mem_norm"], name="rms_mem_bwd")
    dq_pad, dk_pad, dv_mla = dq_pad.reshape(T, -1), dk_pad.reshape(T, -1), dv_mla.reshape(T, -1)
    dqpre, dkr = rope_bwd(dq_pad, dk_pad, pos, invf, name="rope_bwd")
    dcqn = matmul(dqpre, w_uq_pad, tb=True, name="mm_uq_dx")
    g_uq = _unpad_heads(matmul(cqn, dqpre, ta=True, out_dtype=BF16, name="mm_uq_dw"), MLA_NOPE + MLA_ROPE)
    dckvn = matmul(dk_pad, w_uk_pad, tb=True, name="mm_uk_dx")
    dckvn = matmul(dv_mla, W["w_uv"], tb=True, acc=dckvn, name="mm_uv_dx")
    g_uk = _unpad_heads(matmul(ckvn, dk_pad, ta=True, out_dtype=BF16, name="mm_uk_dw"), MLA_NOPE)
    g_uv = matmul(ckvn, dv_mla, ta=True, out_dtype=BF16, name="mm_uv_dw")
    dcq, dg_q_norm = rms_bwd(dcqn, proj, sm["q_norm"], n=Q_LORA, col=cq_col, out_dtype=BF16, name="rms_q_bwd")
    mix = dict(w_out=g_out, w_branch_out=g_bo, w_uq=g_uq, w_uk=g_uk, w_uv=g_uv, w_mem_kv=g_memkv)
    sib = ex.sibling_comm("mix", {n: _full_to_owner(n, g) for n, g in mix.items()})
    dckv, dg_kv_norm, *got = rms_bwd(dckvn, proj, sm["kv_norm"], n=KV_LORA, col=ckv_col, out_dtype=BF16,
                                     name="rms_kv_bwd", comm=sib)
    red_mix = ex.chip_comm("mix", got[0] if got else None)
    dsq, dsk, dsv, *got = attn_bwd(proj3, proj3, proj3, o_sb, st_sb, dos[1], name="sb_bwd", comm=red_mix, **sbk)
    ex.reduce_done("mix", got[0] if got else None)
    pieces = dict(gate=dlog, sb=jnp.concatenate([dsq, dsk, dsv], axis=-1).reshape(T, -1), qm=dqm.reshape(T, -1),
                  ckv=dckv, cq=dcq, kr=dkr, zpad=jnp.zeros((T, PAD_SIZES["zpad"]), BF16))
    dproj = jnp.concatenate([pieces[n].astype(BF16) for n in PAD_ORDER], axis=1)
    g_in = in_pad_to_owner(matmul(h, dproj, ta=True, out_dtype=BF16, name="mm_in_dw"), name="g_in_layout")
    red_in = ex.reduce_start("in", dict(w_in=g_in))
    dx, dg_mix_pre, *got = matmul(
        dproj, w_in_pad, tb=True, comm=red_in, name="mm_in_dx_norm",
        rows=dict(fn=_mix_pre_bwd_rows, row_ins=[x2, dx1], vec_ins=[sm["ln_mix_pre"]], row_outs=[F32], vec_outs=[D]))
    ex.reduce_done("in", got[0] if got else None)

    small = dict(ln_mix_pre=dg_mix_pre, b_gate=db_gate, q_norm=dg_q_norm, kv_norm=dg_kv_norm, mem_norm=dg_mem_norm,
                 ln_mix_post=dg_mix_post, ln_mlp_pre=dg_mlp_pre, ln_mlp_post=dg_mlp_post)
    return loss_p, dx.reshape(B, S, D), small


def kernel(x, mem, positions, ln_mix_pre, w_in, b_gate, q_norm, w_uq, kv_norm, w_uk, w_uv, mem_norm, w_mem_kv, w_branch_out, w_out, ln_mix_post, ln_mlp_pre, w_mlp_up, w_mlp_down, ln_mlp_post, loss_target, m_ln_mix_pre, m_w_in, m_b_gate, m_q_norm, m_w_uq, m_kv_norm, m_w_uk, m_w_uv, m_mem_norm, m_w_mem_kv, m_w_branch_out, m_w_out, m_ln_mix_post, m_ln_mlp_pre, m_w_mlp_up, m_w_mlp_down, m_ln_mlp_post, v_ln_mix_pre, v_w_in, v_b_gate, v_q_norm, v_w_uq, v_kv_norm, v_w_uk, v_w_uv, v_mem_norm, v_w_mem_kv, v_w_branch_out, v_w_out, v_ln_mix_post, v_ln_mlp_pre, v_w_mlp_up, v_w_mlp_down, v_ln_mlp_post):
    given = dict(locals())
    w = {n: given[n][0] for n in WEIGHTS}
    m = {n: given["m_" + n][0] for n in WEIGHTS}
    v = {n: given["v_" + n][0] for n in WEIGHTS}
    sm = {n: w[n].reshape(1, -1) for n in SMALL}
    ex = _Exchange(w)
    loss_row, grad_x, g_small = _local_step(x, mem, positions, loss_target, sm, ex)

    res = {n: sum_adamw(ex.reduced[n], given[n], given["m_" + n], given["v_" + n], name=f"adamw_{n}") for n in BIG}

    small_shapes = [w[n].shape for n in SMALL] + [(LANES,)]
    parts = gather_two_level([_pack_rows([g_small[n] for n in SMALL] + [loss_row])], name="gather_small_grads")[0]
    no_row = jnp.zeros((1, LANES), F32)
    res_small = sum_adamw(parts, *[_pack_rows([t[n] for n in SMALL] + [no_row]) for t in (w, m, v)], name="adamw_replicated")
    res_small = [_unpack_rows(r, small_shapes) for r in res_small]
    for i, n in enumerate(SMALL):
        res[n] = [r[i][None] for r in res_small]
    loss = res_small[0][len(SMALL)][0]

    out = [loss, grad_x]
    for k in range(4):
        out += [res[n][k] for n in WEIGHTS]
    return tuple(out)
```

```python
import functools
import math

import numpy as np
import jax
import jax.numpy as jnp
from jax import lax
from jax.experimental import pallas as pl
from jax.experimental.pallas import tpu as pltpu

F32 = jnp.float32
BF16 = jnp.bfloat16

D_MODEL = 1024
MEM_HEADS, MEM_DIM = 4, 128
MLA_HEADS, MLA_NOPE, MLA_ROPE, MLA_V = 8, 64, 32, 64
Q_LORA, KV_LORA = 384, 256
ROPE_THETA = 10000.0
SB_HEADS, SB_DIM = 8, 64
D_FF = 4 * D_MODEL
N_BRANCH, BRANCH_W = 3, 512
EPS = 1e-6
SB_W = 3 * SB_HEADS * SB_DIM
QM_W = MEM_HEADS * MEM_DIM
GATE_W = N_BRANCH * D_MODEL
IN_SIZES = (Q_LORA, KV_LORA, MLA_ROPE, SB_W, QM_W, GATE_W)
IN_WIDTH = sum(IN_SIZES)
ADAM_LR, ADAM_B1, ADAM_B2, ADAM_EPS, ADAM_WD, ADAM_STEP = 0.001, 0.9, 0.999, 1e-08, 0.01, 10

N_DEV = 8
LANES = 128
KR_PAD = LANES
PAD_ORDER = ("gate", "cq", "kr", "ckv", "sb", "qm", "zpad")
MM_TILE = 1024
MM_ROWS = 2048
MM_DEPTH = 4096
MM_VMEM = 40 * 1024 * 1024
PAD_SIZES = dict(gate=GATE_W, cq=Q_LORA, kr=KR_PAD, ckv=KV_LORA, sb=SB_W, qm=QM_W)
PAD_SIZES["zpad"] = (-sum(PAD_SIZES.values())) % MM_TILE
PAD_OFF = {}
_o = 0
for _n in PAD_ORDER:
    PAD_OFF[_n] = _o
    _o += PAD_SIZES[_n]
IN_PAD = _o
HEAD_PAD = LANES
MLA_PAD_W = MLA_HEADS * HEAD_PAD
VMEM_LIMIT = 48 * 1024 * 1024


def _cparams(sem):
    return pltpu.CompilerParams(dimension_semantics=sem, vmem_limit_bytes=VMEM_LIMIT)


def _pick(n, pref):
    if n <= pref:
        return n
    t = (pref // LANES) * LANES
    while t >= LANES:
        if n % t == 0:
            return t
        t -= LANES
    return n


def matmul(a, b, *, name, ta=False, tb=False, out_dtype=F32, acc=None, act=None, act_in=None,
           b_slots=None, out_slots=None, rows=None, comm=None, tm_pref=MM_ROWS, tn_pref=MM_TILE, tk_pref=MM_DEPTH):
    M, K = (a.shape[1], a.shape[0]) if ta else a.shape
    tm, tk = _pick(M, tm_pref), _pick(K, tk_pref)
    if b_slots == "n":
        assert not tb and b.shape[:2] == (N_DEV, K)
        tn = b.shape[2]
        N = N_DEV * tn
    elif b_slots == "k":
        assert tb and N_DEV * b.shape[2] == K
        N, tk = b.shape[1], b.shape[2]
        tn = _pick(N, tn_pref)
    else:
        N = b.shape[0] if tb else b.shape[1]
        assert (b.shape[1] if tb else b.shape[0]) == K
        tn = _pick(N, tn_pref)
    if out_slots == "m":
        tm = M // N_DEV
    elif out_slots == "n":
        tn = N // N_DEV
    assert out_slots is None or (acc is None and act_in is None)
    if rows is not None:
        assert acc is None and act is None and out_slots is None
        tn = N
        row_ins, vec_ins = list(rows["row_ins"]), list(rows["vec_ins"])
        row_outs, vec_outs = list(rows["row_outs"]), list(rows["vec_outs"])

    def block_bytes(tm_, tk_):
        ab = tm_ * tk_ * a.dtype.itemsize + tk_ * tn * b.dtype.itemsize
        per_out = jnp.dtype(out_dtype).itemsize * (2 if act == "relu2" else 1)
        per_out += sum(t.dtype.itemsize for t in (acc, act_in) if t is not None)
        if rows is not None:
            per_out = sum(t.dtype.itemsize for t in row_ins) + sum(jnp.dtype(d).itemsize for d in row_outs)
        return 2 * (ab + tm_ * tn * per_out) + (tm_ * tn * 4 if K // tk_ > 1 else 0)

    while block_bytes(tm, tk) > MM_VMEM:
        if tm > 512 and out_slots != "m" and M % (tm // 2) == 0:
            tm //= 2
        elif tk > 512 and b_slots != "k" and K % (tk // 2) == 0:
            tk //= 2
        else:
            break
    nk = K // tk
    dims = (((0 if ta else 1,), (1 if tb else 0,)), ((), ()))
    assert act in (None, "relu2", "relu2_bwd") and (act == "relu2_bwd") == (act_in is not None)

    def body(*refs):
        a_ref, b_ref = refs[0], refs[1]
        pos = 2
        acc_ref = act_ref = None
        if acc is not None:
            acc_ref = refs[pos]
            pos += 1
        if act_in is not None:
            act_ref = refs[pos]
            pos += 1
        if rows is not None:
            n_ri, n_vi, n_ro, n_vo = len(row_ins), len(vec_ins), len(row_outs), len(vec_outs)
            ri_refs, vi_refs = refs[pos:pos + n_ri], refs[pos + n_ri:pos + n_ri + n_vi]
            pos += n_ri + n_vi
            ro_refs, vo_refs = refs[pos:pos + n_ro], refs[pos + n_ro:pos + n_ro + n_vo]
            pos += n_ro + n_vo - 1
            row_i = pl.program_id(0)
        out = refs[pos]
        if act == "relu2":
            pos += 1
            out_relu = refs[pos]
        scratch = refs[pos + 1:]

        part = lax.dot_general(a_ref[...].astype(BF16), b_ref[...].astype(BF16), dims,
                               preferred_element_type=F32)

        def finish(r):
            if rows is not None:
                ro, vo = rows["fn"](r, *[t[...] for t in ri_refs], *[t[...] for t in vi_refs])
                for ref, val in zip(ro_refs, ro):
                    ref[...] = val.astype(ref.dtype)
                for ref, val in zip(vo_refs, vo):
                    _acc_rows(ref, row_i, val)
                return
            if acc_ref is not None:
                r = r + acc_ref[...]
            if act == "relu2":
                r = jnp.maximum(r, 0.0)
                out_relu[...] = r.astype(out_relu.dtype)
                r = jnp.square(r)
            elif act == "relu2_bwd":
                r = r * (2.0 * act_ref[...].astype(F32))
            out[...] = r.astype(out.dtype)

        if nk == 1:
            finish(part)
        else:
            acc_sc = scratch[0]
            k = pl.program_id(2)

            @pl.when(k == 0)
            def _():
                acc_sc[...] = part

            @pl.when(k > 0)
            def _():
                acc_sc[...] += part

            @pl.when(k == nk - 1)
            def _():
                finish(acc_sc[...])

    a_spec = pl.BlockSpec((tk, tm), lambda i, j, k: (k, i)) if ta else pl.BlockSpec((tm, tk), lambda i, j, k: (i, k))
    if b_slots == "n":
        b_spec = pl.BlockSpec((None, tk, tn), lambda i, j, k: (j, k, 0))
    elif b_slots == "k":
        b_spec = pl.BlockSpec((None, tn, tk), lambda i, j, k: (k, j, 0))
    else:
        b_spec = pl.BlockSpec((tn, tk), lambda i, j, k: (j, k)) if tb else pl.BlockSpec((tk, tn), lambda i, j, k: (k, j))
    if out_slots == "m":
        o_spec = pl.BlockSpec((None, None, tm, tn), lambda i, j, k: (i % 2, i // 2, 0, j))
        o_shape = (2, N_CHIP, tm, N)
    elif out_slots == "n":
        o_spec = pl.BlockSpec((None, None, tm, tn), lambda i, j, k: (j % 2, j // 2, i, 0))
        o_shape = (2, N_CHIP, M, tn)
    else:
        o_spec = pl.BlockSpec((tm, tn), lambda i, j, k: (i, j))
        o_shape = (M, N)
    in_specs = [a_spec, b_spec]
    args = [a, b]
    if acc is not None:
        in_specs.append(o_spec)
        args.append(acc)
    if act_in is not None:
        in_specs.append(o_spec)
        args.append(act_in)
    out_specs, out_shapes, sem = [o_spec], [jax.ShapeDtypeStruct(o_shape, out_dtype)], ("parallel", "parallel", "arbitrary")
    if act == "relu2":
        out_specs, out_shapes = out_specs * 2, out_shapes * 2
    if rows is not None:
        def vec_spec(w):
            return pl.BlockSpec((1, w), lambda i, j, k: (0, 0))

        in_specs += [o_spec] * len(row_ins) + [vec_spec(t.shape[1]) for t in vec_ins]
        args += row_ins + vec_ins
        out_specs = [o_spec] * len(row_outs) + [vec_spec(w) for w in vec_outs]
        out_shapes = ([jax.ShapeDtypeStruct((M, N), d) for d in row_outs]
                      + [jax.ShapeDtypeStruct((1, w), F32) for w in vec_outs])
        sem = ("arbitrary",) * 3
    outs, c_outs = _call(
        body, name=name, grid=(M // tm, N // tn, nk), in_specs=in_specs, out_specs=out_specs, out_shape=out_shapes,
        scratch_shapes=[pltpu.VMEM((tm, tn), F32)] if nk > 1 else [], args=args, sem=sem, comm=comm)
    if rows is not None:
        return (*outs, c_outs) if comm is not None else tuple(outs)
    if act == "relu2":
        assert comm is None
        return outs[0], outs[1]
    return (outs[0], c_outs) if comm is not None else outs[0]


ROW_TILE = 512


def _rstd(xv):
    return lax.rsqrt(jnp.mean(xv * xv, axis=-1, keepdims=True) + EPS)


def _rms_bwd_rows(dy, xv, g):
    r = _rstd(xv)
    dyg = dy * g
    dx = r * dyg - xv * (r * r * r) * jnp.mean(dyg * xv, axis=-1, keepdims=True)
    return dx, dy * xv * r


def _row_spec(tm, n, col=0):
    return pl.BlockSpec((tm, n), lambda i: (i, col))


def _vec_spec(n):
    return pl.BlockSpec((1, n), lambda i: (0, 0))


def _acc_rows(ref, i, val):
    @pl.when(i == 0)
    def _():
        ref[...] = val

    @pl.when(i > 0)
    def _():
        ref[...] += val


def rms_fwd(x, g, *, name, n=None, col=0, out_dtype=BF16, comm=None):
    T = x.shape[0]
    n = x.shape[1] if n is None else n
    tm = _pick(T, ROW_TILE)

    def body(x_ref, g_ref, o_ref):
        xv = x_ref[...].astype(F32)
        o_ref[...] = (xv * _rstd(xv) * g_ref[...]).astype(o_ref.dtype)

    outs, c_outs = _call(
        body, name=name, grid=(T // tm,), in_specs=[_row_spec(tm, n, col), _vec_spec(n)],
        out_specs=[_row_spec(tm, n)], out_shape=[jax.ShapeDtypeStruct((T, n), out_dtype)],
        scratch_shapes=[], args=(x, g), sem=("parallel",), comm=comm)
    return (outs[0], c_outs) if comm is not None else outs[0]


def rms_bwd(dy, x, g, *, name, n=None, col=0, out_dtype=F32, comm=None):
    T = x.shape[0]
    n = x.shape[1] if n is None else n
    tm = _pick(T, ROW_TILE)

    def body(dy_ref, x_ref, g_ref, dx_ref, dg_ref):
        dx, dgr = _rms_bwd_rows(dy_ref[...].astype(F32), x_ref[...].astype(F32), g_ref[...])
        dx_ref[...] = dx.astype(dx_ref.dtype)
        _acc_rows(dg_ref, pl.program_id(0), jnp.sum(dgr, axis=0, keepdims=True))

    outs, c_outs = _call(
        body, name=name, grid=(T // tm,), in_specs=[_row_spec(tm, n), _row_spec(tm, n, col), _vec_spec(n)],
        out_specs=(_row_spec(tm, n), _vec_spec(n)),
        out_shape=(jax.ShapeDtypeStruct((T, n), out_dtype), jax.ShapeDtypeStruct((1, n), F32)),
        scratch_shapes=[], args=(dy, x, g), sem=("arbitrary",), comm=comm)
    return (*outs, c_outs) if comm is not None else tuple(outs)


def _mix_post_rows(y, x, g_post, g_pre2):
    x1 = x + y * _rstd(y) * g_post
    return [y, x1, x1 * _rstd(x1) * g_pre2], []


def _loss_rows(d, x1, tgt, g):
    err = x1 + d * _rstd(d) * g - tgt
    part = 0.5 * jnp.sum(jnp.mean(err * err, axis=-1, keepdims=True), axis=0, keepdims=True)
    dx2 = err * (1.0 / d.shape[-1])
    dd, dgr = _rms_bwd_rows(dx2, d, g)
    return [dx2, dd], [jnp.broadcast_to(part, (1, LANES)), jnp.sum(dgr, axis=0, keepdims=True)]


def _mlp_pre_bwd_rows(dh2, dx2, x1, y, g_pre2, g_post):
    d1, dg2 = _rms_bwd_rows(dh2, x1, g_pre2)
    dx1 = dx2 + d1
    dy, dgp = _rms_bwd_rows(dx1, y, g_post)
    return [dx1, dy], [jnp.sum(dg2, axis=0, keepdims=True), jnp.sum(dgp, axis=0, keepdims=True)]


def _mix_pre_bwd_rows(dh, x, dx1, g):
    dx, dgr = _rms_bwd_rows(dh, x, g)
    return [dx + dx1], [jnp.sum(dgr, axis=0, keepdims=True)]


def _gate_specs(tm):
    gcol = PAD_OFF["gate"] // D_MODEL
    lspecs = [pl.BlockSpec((tm, D_MODEL), functools.partial(lambda i, c: (i, c), c=gcol + b)) for b in range(N_BRANCH)]
    bspecs = [pl.BlockSpec((1, D_MODEL), functools.partial(lambda i, c: (0, c), c=b)) for b in range(N_BRANCH)]
    pspecs = [_row_spec(tm, D_MODEL) for _ in range(N_BRANCH)]
    return lspecs, bspecs, pspecs


def gate_merge(proj, b_gate, ps, *, name):
    T = proj.shape[0]
    tm = _pick(T, ROW_TILE)
    lspecs, bspecs, pspecs = _gate_specs(tm)

    def body(*refs):
        l_refs, b_refs, p_refs, o_ref = refs[0:3], refs[3:6], refs[6:9], refs[9]
        tot = None
        for lr, br, pr in zip(l_refs, b_refs, p_refs):
            term = jax.nn.sigmoid(lr[...].astype(F32) + br[...]) * pr[...].astype(F32)
            tot = term if tot is None else tot + term
        o_ref[...] = tot.astype(o_ref.dtype)

    return pl.pallas_call(
        body, name=name, grid=(T // tm,), in_specs=lspecs + bspecs + pspecs,
        out_specs=_row_spec(tm, D_MODEL), out_shape=jax.ShapeDtypeStruct((T, D_MODEL), BF16),
        compiler_params=_cparams(("parallel",)),
    )(proj, proj, proj, b_gate, b_gate, b_gate, *ps)


def gate_bwd(dmerged, proj, b_gate, ps, *, name):
    T = proj.shape[0]
    tm = _pick(T, ROW_TILE)
    lspecs, bspecs, pspecs = _gate_specs(tm)

    def body(*refs):
        dm_ref = refs[0]
        l_refs, b_refs, p_refs = refs[1:4], refs[4:7], refs[7:10]
        dl_ref, dp_refs, db_ref = refs[10], refs[11:14], refs[14]
        i = pl.program_id(0)
        dm = dm_ref[...]
        for b, (lr, br, pr, dpr) in enumerate(zip(l_refs, b_refs, p_refs, dp_refs)):
            gt = jax.nn.sigmoid(lr[...].astype(F32) + br[...])
            dpr[...] = (dm * gt).astype(dpr.dtype)
            dl = dm * pr[...].astype(F32) * gt * (1.0 - gt)
            dl_ref[:, b * D_MODEL:(b + 1) * D_MODEL] = dl.astype(dl_ref.dtype)
            part = jnp.sum(dl, axis=0, keepdims=True)

            @pl.when(i == 0)
            def _():
                db_ref[:, b * D_MODEL:(b + 1) * D_MODEL] = part

            @pl.when(i > 0)
            def _():
                db_ref[:, b * D_MODEL:(b + 1) * D_MODEL] += part

    return pl.pallas_call(
        body, name=name, grid=(T // tm,), in_specs=[_row_spec(tm, D_MODEL)] + lspecs + bspecs + pspecs,
        out_specs=(_row_spec(tm, GATE_W), *[_row_spec(tm, D_MODEL) for _ in range(N_BRANCH)], _vec_spec(GATE_W)),
        out_shape=(jax.ShapeDtypeStruct((T, GATE_W), BF16),
                   *[jax.ShapeDtypeStruct((T, D_MODEL), BF16) for _ in range(N_BRANCH)],
                   jax.ShapeDtypeStruct((1, GATE_W), F32)),
        compiler_params=_cparams(("arbitrary",)),
    )(dmerged, proj, proj, proj, b_gate, b_gate, b_gate, *ps)


def _rope_tables(pos_ref, invf_ref):
    ang = pos_ref[...] * invf_ref[...]
    lane = lax.broadcasted_iota(jnp.int32, (1, LANES), 1)
    s = jnp.sin(ang)
    split = MLA_NOPE + MLA_ROPE // 2
    return jnp.cos(ang), jnp.where(lane >= split, s, 0.0), jnp.where(lane < split, s, 0.0)


def _rotate(xh, c, s_hi, s_lo, sign):
    half = MLA_ROPE // 2
    up = pltpu.roll(xh, half, 1)
    down = pltpu.roll(xh, LANES - half, 1)
    return xh * c + sign * (up * s_hi - down * s_lo)


def rope_fwd(qpre, kpre, proj, pos, invf, *, name):
    T = qpre.shape[0]
    tm = _pick(T, ROW_TILE)
    kr_col = PAD_OFF["kr"] // KR_PAD

    def body(q_ref, k_ref, kr_ref, pos_ref, invf_ref, qo_ref, ko_ref):
        c, s_hi, s_lo = _rope_tables(pos_ref, invf_ref)
        kr = _rotate(pltpu.roll(kr_ref[...].astype(F32), MLA_NOPE, 1), c, s_hi, s_lo, 1.0)
        for h in range(MLA_HEADS):
            sl = slice(h * HEAD_PAD, (h + 1) * HEAD_PAD)
            qo_ref[:, sl] = _rotate(q_ref[:, sl].astype(F32), c, s_hi, s_lo, 1.0).astype(qo_ref.dtype)
            ko_ref[:, sl] = (k_ref[:, sl].astype(F32) + kr).astype(ko_ref.dtype)

    rs = _row_spec(tm, MLA_PAD_W)
    return pl.pallas_call(
        body, name=name, grid=(T // tm,),
        in_specs=[rs, rs, _row_spec(tm, KR_PAD, kr_col), _row_spec(tm, 1), _vec_spec(LANES)],
        out_specs=(rs, rs),
        out_shape=(jax.ShapeDtypeStruct((T, MLA_PAD_W), BF16), jax.ShapeDtypeStruct((T, MLA_PAD_W), BF16)),
        compiler_params=_cparams(("parallel",)),
    )(qpre, kpre, proj, pos, invf)


def rope_bwd(dq_pad, dk_pad, pos, invf, *, name):
    T = dq_pad.shape[0]
    tm = _pick(T, ROW_TILE)

    def body(dq_ref, dk_ref, pos_ref, invf_ref, dqo_ref, dkr_ref):
        c, s_hi, s_lo = _rope_tables(pos_ref, invf_ref)
        tot = None
        for h in range(MLA_HEADS):
            sl = slice(h * HEAD_PAD, (h + 1) * HEAD_PAD)
            dqo_ref[:, sl] = _rotate(dq_ref[:, sl].astype(F32), c, s_hi, s_lo, -1.0).astype(dqo_ref.dtype)
            tot = dk_ref[:, sl] if tot is None else tot + dk_ref[:, sl]
        dkr = pltpu.roll(_rotate(tot, c, s_hi, s_lo, -1.0), LANES - MLA_NOPE, 1)
        lane = lax.broadcasted_iota(jnp.int32, (1, LANES), 1)
        dkr_ref[...] = jnp.where(lane < MLA_ROPE, dkr, 0.0).astype(dkr_ref.dtype)

    rs = _row_spec(tm, MLA_PAD_W)
    return pl.pallas_call(
        body, name=name, grid=(T // tm,),
        in_specs=[rs, rs, _row_spec(tm, 1), _vec_spec(LANES)],
        out_specs=(rs, _row_spec(tm, KR_PAD)),
        out_shape=(jax.ShapeDtypeStruct((T, MLA_PAD_W), BF16), jax.ShapeDtypeStruct((T, KR_PAD), BF16)),
        compiler_params=_cparams(("parallel",)),
    )(dq_pad, dk_pad, pos, invf)


ATT_TILE = 512
NEG = -1e30


def _split_bf16(v):
    hi = v.astype(BF16)
    return hi, (v - hi.astype(F32)).astype(BF16)


def _mul(x, s):
    return x if s == 1.0 else x * s


def _dot(a, b, ca, cb):
    return lax.dot_general(a, b, (((ca,), (cb,)), ((), ())), preferred_element_type=F32)


CUM_CHUNK = 256


def _tri(kind, terms=2):
    r = lax.broadcasted_iota(jnp.int32, (CUM_CHUNK, CUM_CHUNK), 0)
    c = lax.broadcasted_iota(jnp.int32, (CUM_CHUNK, CUM_CHUNK), 1)
    m = {"gt": r > c, "le": r <= c, "lt": r < c}[kind]
    t = jnp.where(m, 1.0, 0.0).astype(BF16)
    return jnp.concatenate([t] * terms, axis=0)


def _cum(v, tri, kind):
    n = v.shape[1] // CUM_CHUNK
    two = tri.shape[0] == 2 * CUM_CHUNK
    chunks = [v[:, c * CUM_CHUNK:(c + 1) * CUM_CHUNK] for c in range(n)]
    totals = [jnp.sum(ch, axis=1, keepdims=True) for ch in chunks] if n > 1 else None
    outs = []
    for c, ch in enumerate(chunks):
        r = _dot(jnp.concatenate(_split_bf16(ch), axis=1) if two else ch.astype(BF16), tri, 1, 0)
        others = [] if n == 1 else totals[c + 1:] if kind == "gt" else totals[:c]
        for t in others:
            r = r + t
        outs.append(r)
    return outs[0] if n == 1 else jnp.concatenate(outs, axis=1)


def _lane_masks(hp, w):
    lane = lax.broadcasted_iota(jnp.int32, (1, LANES), 1)
    return [(lane >= e * w) & (lane < (e + 1) * w) for e in range(hp)]


def _att_dims(mode, hp, qk_w, v_w):
    assert mode in ("softmax", "sb")
    assert (hp, qk_w, v_w) in ((2, 128, 64), (2, 64, 64), (1, 128, 128))
    qw = hp * LANES if qk_w == LANES else LANES
    return qw


def attn_fwd(q, k, v, *, name, mode, causal, hp, qk_w, v_w, scale, nh, qc, kc, vc, comm=None):
    qw = _att_dims(mode, hp, qk_w, v_w)
    B, S = q.shape[0], q.shape[1]
    Sk = k.shape[1]
    tq = _pick(S, ATT_TILE)
    tk = _pick(Sk, ATT_TILE)
    if causal:
        assert tq == tk and S == Sk
    nq, nkv = S // tq, Sk // tk
    pre_scaled = math.frexp(scale)[0] == 0.5
    post = 1.0 if pre_scaled else scale

    def body(q_ref, k_ref, v_ref, o_ref, st_ref):
        i = pl.program_id(2)
        qv = q_ref[0].astype(BF16)
        masks_qk = _lane_masks(hp, qk_w) if qk_w < LANES else None
        masks_v = _lane_masks(hp, v_w) if v_w < LANES else None
        if qk_w == LANES:
            qs = [qv[:, e * LANES:(e + 1) * LANES] for e in range(hp)]
        else:
            qs = [jnp.where(masks_qk[e], qv, jnp.zeros_like(qv)) for e in range(hp)]
        if pre_scaled:
            qs = [t * scale for t in qs]
        rows = lax.broadcasted_iota(jnp.int32, (tq, tk), 0)
        cols = lax.broadcasted_iota(jnp.int32, (tq, tk), 1)
        tri = _tri("gt") if mode == "sb" else None

        def kv_tile(j):
            off = pl.multiple_of(j * tk, tk)
            kj = k_ref[0, pl.ds(off, tk), :].astype(BF16)
            vj = v_ref[0, pl.ds(off, tk), :].astype(BF16)
            ks = [kj[:, e * LANES:(e + 1) * LANES] for e in range(hp)] if qk_w == LANES else [kj] * hp
            return ks, vj

        def merge(vals):
            if hp == 1:
                return jnp.broadcast_to(vals[0], (tq, LANES))
            return jnp.where(masks_v[0], vals[0], vals[1])

        if mode == "softmax":
            def block(j, carry, diag):
                ms, ls, acc = carry
                ks, vj = kv_tile(j)
                new_m, new_l, alphas, pvs = [], [], [], []
                for e in range(hp):
                    s = _mul(_dot(qs[e], ks[e], 1, 1), post)
                    if diag:
                        s = jnp.where(rows >= cols, s, NEG)
                    m_new = jnp.maximum(ms[e], jnp.max(s, axis=1, keepdims=True))
                    alpha = jnp.exp(ms[e] - m_new)
                    p = jnp.exp(s - m_new)
                    new_l.append(alpha * ls[e] + jnp.sum(p, axis=1, keepdims=True))
                    new_m.append(m_new)
                    alphas.append(alpha)
                    pvs.append(_dot(p.astype(BF16), vj, 1, 0))
                acc = acc * merge(alphas) + merge(pvs)
                return tuple(new_m), tuple(new_l), acc

            init = (tuple(jnp.full((tq, 1), NEG, F32) for _ in range(hp)),
                    tuple(jnp.zeros((tq, 1), F32) for _ in range(hp)),
                    jnp.zeros((tq, LANES), F32))
            if causal:
                carry = lax.fori_loop(0, i, lambda j, c: block(j, c, False), init)
                ms, ls, acc = block(i, carry, True)
            else:
                ms, ls, acc = lax.fori_loop(0, nkv, lambda j, c: block(j, c, False), init)
            o_ref[0] = acc / merge(list(ls))
            st_ref[0, 0] = merge([m + jnp.log(l) for m, l in zip(ms, ls)])
        else:
            def block(j, carry, diag):
                cs_, acc = carry
                ks, vj = kv_tile(j)
                new_c, pvs = [], []
                for e in range(hp):
                    z = _mul(_dot(qs[e], ks[e], 1, 1), post)
                    lk = -jnp.maximum(z, 0.0) - jnp.log(1.0 + jnp.exp(-jnp.abs(z)))
                    lz = lk + z
                    if diag:
                        lk = jnp.where(rows > cols, lk, 0.0)
                    a = jnp.exp(lz + _cum(lk, tri, "gt") + cs_[e])
                    if diag:
                        a = jnp.where(rows > cols, a, 0.0)
                    pvs.append(_dot(a.astype(BF16), vj, 1, 0))
                    new_c.append(cs_[e] + jnp.sum(lk, axis=1, keepdims=True))
                return tuple(new_c), acc + merge(pvs)

            init = (tuple(jnp.zeros((tq, 1), F32) for _ in range(hp)), jnp.zeros((tq, LANES), F32))
            carry = block(i, init, True)
            cs_, acc = lax.fori_loop(0, i, lambda jj, c: block(i - 1 - jj, c, False), carry)
            o_ref[0] = acc
            st_ref[0, 0] = merge(list(cs_))

    outs, c_outs = _call(
        body, name=name, grid=(B, nh, nq),
        in_specs=[pl.BlockSpec((1, tq, qw), lambda b, h, i: (b, i, qc + h)),
                  pl.BlockSpec((1, Sk, qw), lambda b, h, i: (b, 0, kc + h)),
                  pl.BlockSpec((1, Sk, LANES), lambda b, h, i: (b, 0, vc + h))],
        out_specs=(pl.BlockSpec((1, tq, LANES), lambda b, h, i: (b, i, h)),
                   pl.BlockSpec((1, 1, tq, LANES), lambda b, h, i: (b, h, i, 0))),
        out_shape=(jax.ShapeDtypeStruct((B, S, nh * LANES), F32), jax.ShapeDtypeStruct((B, nh, S, LANES), F32)),
        scratch_shapes=[], args=(q, k, v), sem=("parallel", "parallel", "arbitrary"), comm=comm)
    return (*outs, c_outs) if comm is not None else tuple(outs)


def attn_bwd(q, k, v, o, st, do, *, name, mode, causal, hp, qk_w, v_w, scale, nh, qc, kc, vc, comm=None):
    qw = _att_dims(mode, hp, qk_w, v_w)
    B, S = q.shape[0], q.shape[1]
    Sk = k.shape[1]
    tq = _pick(S, ATT_TILE)
    tk = _pick(Sk, ATT_TILE)
    if causal:
        assert tq == tk and S == Sk
    nq, nkv = S // tq, Sk // tk
    pre_scaled = math.frexp(scale)[0] == 0.5
    post = 1.0 if pre_scaled else scale

    def body(q_ref, k_ref, v_ref, o_ref, st_ref, do_ref, dq_ref, dk_ref, dv_ref):
        i = pl.program_id(2)

        @pl.when(i == 0)
        def _():
            dk_ref[...] = jnp.zeros_like(dk_ref)
            dv_ref[...] = jnp.zeros_like(dv_ref)

        qv = q_ref[0].astype(BF16)
        dov = do_ref[0]
        stv = st_ref[0, 0]
        masks_qk = _lane_masks(hp, qk_w) if qk_w < LANES else None
        masks_v = _lane_masks(hp, v_w) if v_w < LANES else None
        if qk_w == LANES:
            qs = [qv[:, e * LANES:(e + 1) * LANES] for e in range(hp)]
        else:
            qs = [jnp.where(masks_qk[e], qv, jnp.zeros_like(qv)) for e in range(hp)]
        if pre_scaled:
            qs = [t * scale for t in qs]
        if hp == 1:
            dos = [dov.astype(BF16)]
            stats = [stv[:, 0:1]]
        else:
            dos = [jnp.where(masks_v[e], dov, 0.0).astype(BF16) for e in range(hp)]
            stats = [stv[:, e * v_w:e * v_w + 1] for e in range(hp)]
        if mode == "softmax":
            prod = dov * o_ref[0]
            if hp == 1:
                dsum = [jnp.sum(prod, axis=1, keepdims=True)]
            else:
                dsum = [jnp.sum(jnp.where(masks_v[e], prod, 0.0), axis=1, keepdims=True) for e in range(hp)]
        rows = lax.broadcasted_iota(jnp.int32, (tq, tk), 0)
        cols = lax.broadcasted_iota(jnp.int32, (tq, tk), 1)
        if mode == "sb":
            tri_le, tri_lt = _tri("le"), _tri("lt", terms=1)

        def kv_tile(j):
            off = pl.multiple_of(j * tk, tk)
            kj = k_ref[0, pl.ds(off, tk), :].astype(BF16)
            vj = v_ref[0, pl.ds(off, tk), :].astype(BF16)
            ks = [kj[:, e * LANES:(e + 1) * LANES] for e in range(hp)] if qk_w == LANES else [kj] * hp
            return off, ks, vj

        def scatter(off, dz_list, p_list):
            dvj = None
            for e in range(hp):
                t = _dot(p_list[e], dos[e], 0, 0)
                dvj = t if dvj is None else dvj + t
            dv_ref[0, pl.ds(off, tk), :] += dvj
            if qk_w == LANES:
                for e in range(hp):
                    dk_ref[0, pl.ds(off, tk), e * LANES:(e + 1) * LANES] += _dot(dz_list[e], qs[e], 0, 0)
            else:
                dkj = None
                for e in range(hp):
                    t = _dot(dz_list[e], qs[e], 0, 0)
                    dkj = t if dkj is None else dkj + t
                dk_ref[0, pl.ds(off, tk), :] += dkj

        def dq_add(dqs, dz_list, ks):
            out = []
            for e in range(hp):
                out.append(dqs[e] + _dot(dz_list[e], ks[e], 1, 0))
            return tuple(out)

        dq0 = tuple(jnp.zeros((tq, LANES), F32) for _ in range(hp))

        if mode == "softmax":
            def block(j, dqs, diag):
                off, ks, vj = kv_tile(j)
                dzs, ps = [], []
                for e in range(hp):
                    s = _mul(_dot(qs[e], ks[e], 1, 1), post)
                    p = jnp.exp(s - stats[e])
                    if diag:
                        p = jnp.where(rows >= cols, p, 0.0)
                    dp = _dot(dos[e], vj, 1, 1)
                    dzs.append(_mul(p * (dp - dsum[e]), post).astype(BF16))
                    ps.append(p.astype(BF16))
                scatter(off, dzs, ps)
                return dq_add(dqs, dzs, ks)

            if causal:
                dqs = lax.fori_loop(0, i, lambda j, c: block(j, c, False), dq0)
                dqs = block(i, dqs, True)
            else:
                dqs = lax.fori_loop(0, nkv, lambda j, c: block(j, c, False), dq0)
        else:
            def block(j, carry, diag):
                dqs, cps, cgs = carry
                off, ks, vj = kv_tile(j)
                dzs, ps, new_p, new_g = [], [], [], []
                for e in range(hp):
                    z = _mul(_dot(qs[e], ks[e], 1, 1), post)
                    lk = -jnp.maximum(z, 0.0) - jnp.log(1.0 + jnp.exp(-jnp.abs(z)))
                    lz = lk + z
                    sig = jnp.exp(lz)
                    keep = 1.0 - sig
                    if diag:
                        lk = jnp.where(rows > cols, lk, 0.0)
                    keep_after = stats[e] - cps[e] - _cum(lk, tri_le, "le")
                    a = jnp.exp(lz + keep_after)
                    if diag:
                        a = jnp.where(rows > cols, a, 0.0)
                    g = _dot(dos[e], vj, 1, 1) * a
                    gsum = cgs[e] + _cum(g, tri_lt, "lt")
                    dz = _mul(g * keep - gsum * sig, post)
                    if diag:
                        dz = jnp.where(rows > cols, dz, 0.0)
                    dzs.append(dz.astype(BF16))
                    ps.append(a.astype(BF16))
                    new_p.append(cps[e] + jnp.sum(lk, axis=1, keepdims=True))
                    new_g.append(cgs[e] + jnp.sum(g, axis=1, keepdims=True))
                scatter(off, dzs, ps)
                return dq_add(dqs, dzs, ks), tuple(new_p), tuple(new_g)

            zc = tuple(jnp.zeros((tq, 1), F32) for _ in range(hp))
            carry = lax.fori_loop(0, i, lambda j, c: block(j, c, False), (dq0, zc, zc))
            dqs, _, _ = block(i, carry, True)

        if pre_scaled:
            dqs = [t * scale for t in dqs]
        if qk_w == LANES:
            for e in range(hp):
                dq_ref[0, :, e * LANES:(e + 1) * LANES] = dqs[e].astype(dq_ref.dtype)
        else:
            dq_ref[0] = jnp.where(masks_qk[0], dqs[0], dqs[1]).astype(dq_ref.dtype)

    outs, c_outs = _call(
        body, name=name, grid=(B, nh, nq),
        in_specs=[pl.BlockSpec((1, tq, qw), lambda b, h, i: (b, i, qc + h)),
                  pl.BlockSpec((1, Sk, qw), lambda b, h, i: (b, 0, kc + h)),
                  pl.BlockSpec((1, Sk, LANES), lambda b, h, i: (b, 0, vc + h)),
                  pl.BlockSpec((1, tq, LANES), lambda b, h, i: (b, i, h)),
                  pl.BlockSpec((1, 1, tq, LANES), lambda b, h, i: (b, h, i, 0)),
                  pl.BlockSpec((1, tq, LANES), lambda b, h, i: (b, i, h))],
        out_specs=(pl.BlockSpec((1, tq, qw), lambda b, h, i: (b, i, h)),
                   pl.BlockSpec((1, Sk, qw), lambda b, h, i: (b, 0, h)),
                   pl.BlockSpec((1, Sk, LANES), lambda b, h, i: (b, 0, h))),
        out_shape=(jax.ShapeDtypeStruct((B, S, nh * qw), BF16), jax.ShapeDtypeStruct((B, Sk, nh * qw), F32),
                   jax.ShapeDtypeStruct((B, Sk, nh * LANES), F32)),
        scratch_shapes=[], args=(q, k, v, o, st, do), sem=("parallel", "parallel", "arbitrary"), comm=comm)
    return (*outs, c_outs) if comm is not None else tuple(outs)


def _pos(p):
    return 4 * p[0] + 2 * p[1] + p[2]


N_CHIP = 4
MESH_ID = pl.DeviceIdType.MESH
_ANY = pl.BlockSpec(memory_space=pl.ANY)


def _me():
    return lax.axis_index("x"), lax.axis_index("y"), lax.axis_index("c")


def _other_chips(x, y):
    return [(1 - x, y), (x, 1 - y), (1 - x, 1 - y)]


class _Comm:
    def __init__(self, arrs, out_shapes, n_sem, n_local, start, finish):
        self.arrs, self.out_shapes, self.start, self.finish = list(arrs), list(out_shapes), start, finish
        self.scratch = [pltpu.SemaphoreType.DMA((n_sem,)), pltpu.SemaphoreType.DMA((n_sem,)),
                        pltpu.SemaphoreType.DMA((max(n_local, 1),))]


def _run_comm(comm, name):
    n = len(comm.arrs)

    def body(*refs):
        r = (refs[:n], refs[n:2 * n], refs[2 * n], refs[2 * n + 1], refs[2 * n + 2])
        comm.start(*r)
        comm.finish(*r)

    return pl.pallas_call(
        body, name=name, in_specs=[_ANY] * n, out_specs=[_ANY] * n, out_shape=comm.out_shapes,
        scratch_shapes=comm.scratch, compiler_params=pltpu.CompilerParams(has_side_effects=True),
    )(*comm.arrs)


def _call(body, *, name, grid, in_specs, out_specs, out_shape, scratch_shapes, args, sem, comm=None):
    in_specs, out_specs, out_shape = list(in_specs), list(out_specs), list(out_shape)
    if comm is None:
        res = pl.pallas_call(body, name=name, grid=grid, in_specs=in_specs, out_specs=out_specs, out_shape=out_shape,
                             scratch_shapes=scratch_shapes, compiler_params=_cparams(sem))(*args)
        return list(res), []
    n_in, n_out, n_scr, nc = len(in_specs), len(out_specs), len(scratch_shapes), len(comm.arrs)

    def wrapped(*refs):
        ins, refs = refs[:n_in], refs[n_in:]
        c_in, refs = refs[:nc], refs[nc:]
        outs, refs = refs[:n_out], refs[n_out:]
        c_out, refs = refs[:nc], refs[nc:]
        scr, sems = refs[:n_scr], refs[n_scr:]
        ids = [pl.program_id(a) for a in range(len(grid))]
        first = functools.reduce(jnp.logical_and, [i == 0 for i in ids])
        last = functools.reduce(jnp.logical_and, [i == g - 1 for i, g in zip(ids, grid)])

        @pl.when(first)
        def _():
            comm.start(c_in, c_out, *sems)

        body(*ins, *outs, *scr)

        @pl.when(last)
        def _():
            comm.finish(c_in, c_out, *sems)

    res = pl.pallas_call(
        wrapped, name=name, grid=grid, in_specs=in_specs + [_ANY] * nc, out_specs=out_specs + [_ANY] * nc,
        out_shape=out_shape + comm.out_shapes, scratch_shapes=list(scratch_shapes) + comm.scratch,
        compiler_params=pltpu.CompilerParams(dimension_semantics=("arbitrary",) * len(grid),
                                             vmem_limit_bytes=VMEM_LIMIT, has_side_effects=True),
    )(*args, *comm.arrs)
    return list(res[:n_out]), list(res[n_out:])


def gather_two_level(bufs, *, name=None):
    n = len(bufs)

    def parts(x_refs, out_refs, send_sems, recv_sems, local_sems):
        x, y, c = _me()
        me, sibling = (x, y, c), (x, y, 1 - c)
        chips = _other_chips(x, y)

        def copy(a, k, block, to, from_input=False):
            dst = out_refs[a].at[_pos(block)]
            return pltpu.make_async_remote_copy(src_ref=x_refs[a] if from_input else dst, dst_ref=dst,
                                                send_sem=send_sems.at[7 * a + k], recv_sem=recv_sems.at[7 * a + k],
                                                device_id=to, device_id_type=MESH_ID)

        mine = [pltpu.make_async_copy(x_refs[a], out_refs[a].at[_pos(me)], local_sems.at[a]) for a in range(n)]
        first = []
        for a in range(n):
            first.append(copy(a, 0, me, sibling, from_input=True))
            first += [copy(a, 1 + j, me, (*chip, c), from_input=True) for j, chip in enumerate(chips)]
        return copy, mine, first, me, sibling, chips, c

    def start(*refs):
        _, mine, first, *_ = parts(*refs)
        for cp in mine + first:
            cp.start()

    def finish(*refs):
        copy, mine, first, me, sibling, chips, c = parts(*refs)
        passed = []
        for j, chip in enumerate(chips):
            for a in range(n):
                copy(a, 1 + j, (*chip, c), me).wait_recv()
                passed.append(copy(a, 4 + j, (*chip, c), sibling))
                passed[-1].start()
        for a in range(n):
            copy(a, 0, sibling, me).wait_recv()
            for j, chip in enumerate(chips):
                copy(a, 4 + j, (*chip, 1 - c), me).wait_recv()
        for cp in first + passed:
            cp.wait_send()
        for cp in mine:
            cp.wait()

    out_shapes = [jax.ShapeDtypeStruct((N_DEV,) + b.shape, b.dtype) for b in bufs]
    comm = _Comm(bufs, out_shapes, 7 * n, n, start, finish)
    return _run_comm(comm, name) if name else comm


def sibling_exchange(sends, *, name=None):
    n = len(sends)

    def copies(s_refs, out_refs, send_sems, recv_sems, local_sems):
        x, y, c = _me()
        return [pltpu.make_async_remote_copy(src_ref=s_refs[a].at[1 - c], dst_ref=out_refs[a], send_sem=send_sems.at[a],
                                             recv_sem=recv_sems.at[a], device_id=(x, y, 1 - c), device_id_type=MESH_ID)
                for a in range(n)]

    def start(*refs):
        for cp in copies(*refs):
            cp.start()

    def finish(*refs):
        for cp in copies(*refs):
            cp.wait()

    out_shapes = [jax.ShapeDtypeStruct(s.shape[1:], s.dtype) for s in sends]
    comm = _Comm(sends, out_shapes, n, 0, start, finish)
    return _run_comm(comm, name) if name else comm


def _flat2(shape):
    return math.prod(shape[:-1]), shape[-1]


def _row_tile(r):
    return r if r <= 512 else _pick8(r, 256)


def _pick8(n, pref):
    t = pref
    while n % t:
        t -= 8
    return t


def chip_sum(send, got, core, *, name):
    shape = got.shape[1:]
    r, cdim = _flat2(shape)
    tr = _row_tile(r)
    send = send.reshape(2, N_CHIP, r, cdim)
    got = got.reshape(N_CHIP, r, cdim)

    def body(core_ref, s_ref, g_ref, o_ref):
        o_ref[...] = (s_ref[0].astype(F32) + g_ref[...].astype(F32)).astype(o_ref.dtype)

    blk = pl.BlockSpec((N_CHIP, tr, cdim), lambda i, core_ref: (0, i, 0))
    out = pl.pallas_call(
        body, name=name,
        grid_spec=pltpu.PrefetchScalarGridSpec(
            num_scalar_prefetch=1, grid=(r // tr,),
            in_specs=[pl.BlockSpec((1, N_CHIP, tr, cdim), lambda i, core_ref: (core_ref[0], 0, i, 0)), blk],
            out_specs=blk),
        out_shape=jax.ShapeDtypeStruct((N_CHIP, r, cdim), BF16),
        compiler_params=_cparams(("parallel",)),
    )(core, send, got)
    return out.reshape((N_CHIP,) + shape)


def chip_exchange(sums, *, name=None):
    n = len(sums)

    def copies(s_refs, out_refs, send_sems, recv_sems, local_sems):
        x, y, c = _me()
        mine = 2 * x + y
        chips = _other_chips(x, y)
        local = [pltpu.make_async_copy(s_refs[a].at[mine], out_refs[a].at[mine], local_sems.at[a]) for a in range(n)]
        sends, recvs = [], []
        for a in range(n):
            for j, (px, py) in enumerate(chips):
                sems = dict(send_sem=send_sems.at[3 * a + j], recv_sem=recv_sems.at[3 * a + j], device_id=(px, py, c),
                            device_id_type=MESH_ID)
                sends.append(pltpu.make_async_remote_copy(src_ref=s_refs[a].at[2 * px + py], dst_ref=out_refs[a].at[mine], **sems))
                recvs.append(pltpu.make_async_remote_copy(src_ref=s_refs[a].at[mine], dst_ref=out_refs[a].at[2 * px + py], **sems))
        return local, sends, recvs

    def start(*refs):
        local, sends, _ = copies(*refs)
        for cp in local + sends:
            cp.start()

    def finish(*refs):
        local, sends, recvs = copies(*refs)
        for cp in recvs:
            cp.wait_recv()
        for cp in sends:
            cp.wait_send()
        for cp in local:
            cp.wait()

    out_shapes = [jax.ShapeDtypeStruct(s.shape, s.dtype) for s in sums]
    comm = _Comm(sums, out_shapes, 3 * n, n, start, finish)
    return _run_comm(comm, name) if name else comm


def sum_adamw(parts, w, m, v, *, name):
    shape = w.shape
    R, cdim = _flat2(shape)
    n_slots = parts.shape[0]
    tr = _row_tile(R)
    parts = parts.reshape(n_slots, R, cdim)
    w, m, v = (t.reshape(1, R, cdim) for t in (w, m, v))
    c1 = 1.0 - ADAM_B1 ** ADAM_STEP
    c2 = 1.0 - ADAM_B2 ** ADAM_STEP

    def body(p_ref, w_ref, m_ref, v_ref, g_ref, d_ref, mo_ref, vo_ref):
        g = p_ref[0].astype(F32)
        for s in range(1, n_slots):
            g = g + p_ref[s].astype(F32)
        mn = ADAM_B1 * m_ref[...] + (1.0 - ADAM_B1) * g
        vn = ADAM_B2 * v_ref[...] + (1.0 - ADAM_B2) * jnp.square(g)
        g_ref[...] = g
        mo_ref[...] = mn
        vo_ref[...] = vn
        d_ref[...] = -ADAM_LR * ((mn / c1) / (jnp.sqrt(vn / c2) + ADAM_EPS) + ADAM_WD * w_ref[...])

    rs = pl.BlockSpec((None, tr, cdim), lambda i: (0, i, 0))
    sd = jax.ShapeDtypeStruct((1, R, cdim), F32)
    res = pl.pallas_call(
        body, name=name, grid=(R // tr,),
        in_specs=[pl.BlockSpec((n_slots, tr, cdim), lambda i: (0, i, 0)), rs, rs, rs],
        out_specs=(rs, rs, rs, rs), out_shape=(sd, sd, sd, sd),
        compiler_params=_cparams(("parallel",)),
    )(parts, w, m, v)
    return [t.reshape(shape) for t in res]


WEIGHTS = ("ln_mix_pre", "w_in", "b_gate", "q_norm", "w_uq", "kv_norm", "w_uk", "w_uv", "mem_norm", "w_mem_kv",
           "w_branch_out", "w_out", "ln_mix_post", "ln_mlp_pre", "w_mlp_up", "w_mlp_down", "ln_mlp_post")
BIG = dict(w_in=((D_MODEL, IN_WIDTH), 1), w_uq=((Q_LORA, MLA_HEADS * (MLA_NOPE + MLA_ROPE)), 1),
           w_uk=((KV_LORA, MLA_HEADS * MLA_NOPE), 1), w_uv=((KV_LORA, MLA_HEADS * MLA_V), 1),
           w_mem_kv=((D_MODEL, 2 * QM_W), 0), w_branch_out=((N_BRANCH, BRANCH_W, D_MODEL), 2),
           w_out=((D_MODEL, D_MODEL), 0), w_mlp_up=((D_MODEL, D_FF), 1), w_mlp_down=((D_FF, D_MODEL), 0))
SMALL = tuple(n for n in WEIGHTS if n not in BIG)


PACK_TILE = 512


def _pad_rows(a):
    r = a.shape[-2]
    to = PACK_TILE if r > PACK_TILE else 8
    pad = [(0, 0)] * a.ndim
    pad[-2] = (0, (-r) % to)
    return jnp.pad(a, pad)


def _pack_rows(arrs):
    return _pad_rows(jnp.concatenate([a.reshape(-1, LANES) for a in arrs], axis=0))


def _unpack_rows(packed, shapes, lead=()):
    out, r = [], 0
    for shp in shapes:
        n = math.prod(shp) // LANES
        out.append(packed[..., r:r + n, :].reshape(lead + tuple(shp)))
        r += n
    return out


def _slots_to_full(name, slots):
    full, ax = BIG[name]
    return jnp.moveaxis(slots, 0, ax).reshape(full)


def _full_to_slots(name, w):
    full, ax = BIG[name]
    split = full[:ax] + (N_DEV, full[ax] // N_DEV) + full[ax + 1:]
    return jnp.moveaxis(w.reshape(split), ax, 0)


def _full_to_owner(name, w):
    s = _full_to_slots(name, w)
    return jnp.swapaxes(s.reshape((N_CHIP, 2) + s.shape[1:]), 0, 1)


def _in_segments():
    orig = dict(zip(("cq", "ckv", "kr", "sb", "qm", "gate"), zip(np.cumsum((0,) + IN_SIZES[:-1]).tolist(), IN_SIZES)))
    shard = IN_WIDTH // N_DEV
    segs = []
    for n in PAD_ORDER:
        if n not in orig:
            continue
        o0, w = orig[n]
        c = o0
        while c < o0 + w:
            s = c // shard
            e = min(o0 + w, (s + 1) * shard)
            segs.append((s, c - s * shard, PAD_OFF[n] + (c - o0), e - c))
            c = e
    return segs


def in_slots_to_pad(slots, *, name):
    _, R, shard = slots.shape
    tr = _pick8(R, ROW_TILE)
    segs = _in_segments()

    def body(in_ref, out_ref):
        out_ref[...] = jnp.zeros_like(out_ref)
        for s, a, d, w in segs:
            out_ref[:, d:d + w] = in_ref[s, :, a:a + w]

    return pl.pallas_call(
        body, name=name, grid=(R // tr,), in_specs=[pl.BlockSpec((N_DEV, tr, shard), lambda i: (0, i, 0))],
        out_specs=pl.BlockSpec((tr, IN_PAD), lambda i: (i, 0)), out_shape=jax.ShapeDtypeStruct((R, IN_PAD), slots.dtype),
        compiler_params=_cparams(("parallel",)),
    )(slots)


def in_pad_to_owner(g, *, name):
    R = g.shape[0]
    shard = IN_WIDTH // N_DEV
    tr = _pick8(R, ROW_TILE)
    segs = _in_segments()

    def body(in_ref, out_ref):
        for s, a, d, w in segs:
            out_ref[s % 2, s // 2, :, a:a + w] = in_ref[:, d:d + w]

    return pl.pallas_call(
        body, name=name, grid=(R // tr,), in_specs=[pl.BlockSpec((tr, IN_PAD), lambda i: (i, 0))],
        out_specs=pl.BlockSpec((2, N_CHIP, tr, shard), lambda i: (0, 0, i, 0)),
        out_shape=jax.ShapeDtypeStruct((2, N_CHIP, R, shard), g.dtype),
        compiler_params=_cparams(("parallel",)),
    )(g)


def _pad_heads(w, width):
    r = w.shape[0]
    return jnp.pad(w.reshape(r, MLA_HEADS, width), ((0, 0), (0, 0), (0, HEAD_PAD - width))).reshape(r, MLA_PAD_W)


def _unpad_heads(wp, width):
    r = wp.shape[0]
    return wp.reshape(r, MLA_HEADS, HEAD_PAD)[:, :, :width].reshape(r, MLA_HEADS * width)


def _rope_inv_freq():
    half = MLA_ROPE // 2
    inv = 1.0 / (ROPE_THETA ** (jnp.arange(half, dtype=F32) * (2.0 / MLA_ROPE)))
    tab = jnp.zeros((LANES,), F32)
    tab = tab.at[MLA_NOPE:MLA_NOPE + half].set(inv).at[MLA_NOPE + half:MLA_NOPE + MLA_ROPE].set(inv)
    return tab.reshape(1, LANES)


def _as_tuple(r):
    return r if isinstance(r, tuple) else (r,)


class _Exchange:
    def __init__(self, w):
        self.w = w
        self.rest = tuple(n for n in BIG if n != "w_in")
        self.pending = {}
        self.reduced = {}

    def first_comm(self):
        return gather_two_level([self.w["w_in"].astype(BF16)])

    def first_weights(self, slots):
        return in_slots_to_pad(slots[0], name="w_in_layout")

    def rest_comm(self):
        return gather_two_level([self.w[n].astype(BF16) for n in self.rest])

    def rest_weights(self, slots):
        return {n: s if n == "w_mlp_up" else _slots_to_full(n, s) for n, s in zip(self.rest, slots)}

    def sibling_comm(self, tag, grads):
        self.pending[tag] = (tuple(grads), [grads[n] for n in grads])
        return sibling_exchange(self.pending[tag][1])

    def chip_comm(self, tag, gots):
        names, sends = self.pending[tag]
        core = lax.axis_index("c").astype(jnp.int32).reshape(1)
        sums = [chip_sum(s, g, core, name=f"chip_sum_{n}") for n, s, g in zip(names, sends, gots)]
        self.pending[tag] = names
        return chip_exchange(sums)

    def reduce_start(self, tag, grads):
        return self.chip_comm(tag, _run_comm(self.sibling_comm(tag, grads), f"sibling_{tag}"))

    def reduce_done(self, tag, recvs):
        self.reduced.update(zip(self.pending.pop(tag), recvs))


def _local_step(x, mem, positions, tgt, sm, ex):
    B, S, D = x.shape
    M = mem.shape[1]
    T = B * S
    x2 = x.reshape(T, D)
    mem2 = mem.reshape(B * M, D)
    pos = positions.reshape(T, 1).astype(F32)
    invf = _rope_inv_freq()
    cq_col, ckv_col = PAD_OFF["cq"] // Q_LORA, PAD_OFF["ckv"] // KV_LORA
    sb_col, qm_col = PAD_OFF["sb"] // LANES, PAD_OFF["qm"] // LANES
    sb_blk = SB_HEADS * SB_DIM // LANES
    mla = dict(mode="softmax", causal=True, hp=2, qk_w=128, v_w=64, scale=(MLA_NOPE + MLA_ROPE) ** -0.5,
               nh=MLA_HEADS // 2, qc=0, kc=0, vc=0)
    sbk = dict(mode="sb", causal=True, hp=2, qk_w=64, v_w=64, scale=SB_DIM ** -0.5, nh=SB_HEADS // 2,
               qc=sb_col, kc=sb_col + sb_blk, vc=sb_col + 2 * sb_blk)
    mca = dict(mode="softmax", causal=False, hp=1, qk_w=128, v_w=128, scale=MEM_DIM ** -0.5, nh=MEM_HEADS,
               qc=qm_col, kc=0, vc=MEM_HEADS)

    h, *slots = _as_tuple(rms_fwd(x2, sm["ln_mix_pre"], name="rms_mix_pre", comm=ex.first_comm()))
    w_in_pad = ex.first_weights(slots[0] if slots else None)
    proj = matmul(h, w_in_pad, out_dtype=BF16, name="mm_in")
    proj3 = proj.reshape(B, S, IN_PAD)
    o_sb, st_sb, *slots = attn_fwd(proj3, proj3, proj3, name="sb_fwd", comm=ex.rest_comm(), **sbk)
    W = ex.rest_weights(slots[0] if slots else None)
    w_uq_pad = _pad_heads(W["w_uq"], MLA_NOPE + MLA_ROPE)
    w_uk_pad = _pad_heads(W["w_uk"], MLA_NOPE)
    cqn = rms_fwd(proj, sm["q_norm"], n=Q_LORA, col=cq_col, name="rms_q")
    ckvn = rms_fwd(proj, sm["kv_norm"], n=KV_LORA, col=ckv_col, name="rms_kv")
    qpre = matmul(cqn, w_uq_pad, out_dtype=BF16, name="mm_uq")
    kpre = matmul(ckvn, w_uk_pad, out_dtype=BF16, name="mm_uk")
    v_mla = matmul(ckvn, W["w_uv"], out_dtype=BF16, name="mm_uv").reshape(B, S, -1)
    q_pad, k_pad = rope_fwd(qpre, kpre, proj, pos, invf, name="rope_fwd")
    q_pad, k_pad = q_pad.reshape(B, S, -1), k_pad.reshape(B, S, -1)
    o_mla, st_mla = attn_fwd(q_pad, k_pad, v_mla, name="mla_fwd", **mla)
    memh = rms_fwd(mem2, sm["mem_norm"], name="rms_mem")
    mkv = matmul(memh, W["w_mem_kv"], out_dtype=BF16, name="mm_memkv").reshape(B, M, -1)
    o_mem, st_mem = attn_fwd(proj3, mkv, mkv, name="mem_fwd", **mca)
    outs = [o.reshape(T, BRANCH_W) for o in (o_mla, o_sb, o_mem)]
    ps = [matmul(o, W["w_branch_out"][b], out_dtype=BF16, name=f"mm_bo{b}") for b, o in enumerate(outs)]
    merged = gate_merge(proj, sm["b_gate"], ps, name="gate_merge")
    y, x1, h2 = matmul(merged, W["w_out"], name="mm_out_norms",
                       rows=dict(fn=_mix_post_rows, row_ins=[x2], vec_ins=[sm["ln_mix_post"], sm["ln_mlp_pre"]],
                                 row_outs=[F32, F32, BF16], vec_outs=[]))
    u, relu_a = matmul(h2, W["w_mlp_up"], b_slots="n", act="relu2", out_dtype=BF16, name="mm_up")
    dx2, dd, loss_p, dg_mlp_post = matmul(
        u, W["w_mlp_down"], name="mm_down_loss",
        rows=dict(fn=_loss_rows, row_ins=[x1, tgt.reshape(T, D)], vec_ins=[sm["ln_mlp_post"]], row_outs=[F32, BF16],
                  vec_outs=[LANES, D]))

    da = matmul(dd, W["w_mlp_down"], tb=True, act="relu2_bwd", act_in=relu_a, out_dtype=BF16, name="mm_down_dx")
    g_down = matmul(u, dd, ta=True, out_dtype=BF16, out_slots="m", name="mm_down_dw")
    g_up = matmul(h2, da, ta=True, out_dtype=BF16, out_slots="n", name="mm_up_dw")
    sib = ex.sibling_comm("mlp", dict(w_mlp_down=g_down, w_mlp_up=g_up))
    dx1, dy, dg_mlp_pre, dg_mix_post, *got = matmul(
        da, _slots_to_full("w_mlp_up", W["w_mlp_up"]), tb=True, name="mm_up_dx_norms", comm=sib,
        rows=dict(fn=_mlp_pre_bwd_rows, row_ins=[dx2, x1, y], vec_ins=[sm["ln_mlp_pre"], sm["ln_mix_post"]],
                  row_outs=[F32, BF16], vec_outs=[D, D]))
    red_mlp = ex.chip_comm("mlp", got[0] if got else None)
    dmerged = matmul(dy, W["w_out"], tb=True, name="mm_out_dx")
    g_out = matmul(merged, dy, ta=True, out_dtype=BF16, name="mm_out_dw")
    dlog, dp0, dp1, dp2, db_gate = gate_bwd(dmerged, proj, sm["b_gate"], ps, name="gate_bwd")
    dps = (dp0, dp1, dp2)
    g_bo = jnp.stack([matmul(outs[b], dps[b], ta=True, out_dtype=BF16, name=f"mm_bo{b}_dw") for b in range(N_BRANCH)])
    dos = [matmul(dps[b], W["w_branch_out"][b], tb=True, out_dtype=BF16, name=f"mm_bo{b}_dx").reshape(B, S, BRANCH_W)
           for b in range(N_BRANCH)]
    dq_pad, dk_pad, dv_mla, *got = attn_bwd(q_pad, k_pad, v_mla, o_mla, st_mla, dos[0], name="mla_bwd", comm=red_mlp, **mla)
    ex.reduce_done("mlp", got[0] if got else None)
    dqm, dmk, dmv = attn_bwd(proj3, mkv, mkv, o_mem, st_mem, dos[2], name="mem_bwd", **mca)
    dmkv = jnp.concatenate([dmk, dmv], axis=-1).astype(BF16).reshape(B * M, -1)
    dmemh = matmul(dmkv, W["w_mem_kv"], tb=True, name="mm_memkv_dx")
    g_memkv = matmul(memh, dmkv, ta=True, out_dtype=BF16, name="mm_memkv_dw")
    _, dg_mem_norm = rms_bwd(dmemh, mem2, sm["mem_norm"], name="rms_mem_bwd")
    dq_pad, dk_pad, dv_mla = dq_pad.reshape(T, -1), dk_pad.reshape(T, -1), dv_mla.reshape(T, -1)
    dqpre, dkr = rope_bwd(dq_pad, dk_pad, pos, invf, name="rope_bwd")
    dcqn = matmul(dqpre, w_uq_pad, tb=True, name="mm_uq_dx")
    g_uq = _unpad_heads(matmul(cqn, dqpre, ta=True, out_dtype=BF16, name="mm_uq_dw"), MLA_NOPE + MLA_ROPE)
    dckvn = matmul(dk_pad, w_uk_pad, tb=True, name="mm_uk_dx")
    dckvn = matmul(dv_mla, W["w_uv"], tb=True, acc=dckvn, name="mm_uv_dx")
    g_uk = _unpad_heads(matmul(ckvn, dk_pad, ta=True, out_dtype=BF16, name="mm_uk_dw"), MLA_NOPE)
    g_uv = matmul(ckvn, dv_mla, ta=True, out_dtype=BF16, name="mm_uv_dw")
    dcq, dg_q_norm = rms_bwd(dcqn, proj, sm["q_norm"], n=Q_LORA, col=cq_col, out_dtype=BF16, name="rms_q_bwd")
    mix = dict(w_out=g_out, w_branch_out=g_bo, w_uq=g_uq, w_uk=g_uk, w_uv=g_uv, w_mem_kv=g_memkv)
    sib = ex.sibling_comm("mix", {n: _full_to_owner(n, g) for n, g in mix.items()})
    dckv, dg_kv_norm, *got = rms_bwd(dckvn, proj, sm["kv_norm"], n=KV_LORA, col=ckv_col, out_dtype=BF16,
                                     name="rms_kv_bwd", comm=sib)
    red_mix = ex.chip_comm("mix", got[0] if got else None)
    dsq, dsk, dsv, *got = attn_bwd(proj3, proj3, proj3, o_sb, st_sb, dos[1], name="sb_bwd", comm=red_mix, **sbk)
    ex.reduce_done("mix", got[0] if got else None)
    pieces = dict(gate=dlog, sb=jnp.concatenate([dsq, dsk, dsv], axis=-1).reshape(T, -1), qm=dqm.reshape(T, -1),
                  ckv=dckv, cq=dcq, kr=dkr, zpad=jnp.zeros((T, PAD_SIZES["zpad"]), BF16))
    dproj = jnp.concatenate([pieces[n].astype(BF16) for n in PAD_ORDER], axis=1)
    g_in = in_pad_to_owner(matmul(h, dproj, ta=True, out_dtype=BF16, name="mm_in_dw"), name="g_in_layout")
    red_in = ex.reduce_start("in", dict(w_in=g_in))
    dx, dg_mix_pre, *got = matmul(
        dproj, w_in_pad, tb=True, comm=red_in, name="mm_in_dx_norm",
        rows=dict(fn=_mix_pre_bwd_rows, row_ins=[x2, dx1], vec_ins=[sm["ln_mix_pre"]], row_outs=[F32], vec_outs=[D]))
    ex.reduce_done("in", got[0] if got else None)

    small = dict(ln_mix_pre=dg_mix_pre, b_gate=db_gate, q_norm=dg_q_norm, kv_norm=dg_kv_norm, mem_norm=dg_mem_norm,
                 ln_mix_post=dg_mix_post, ln_mlp_pre=dg_mlp_pre, ln_mlp_post=dg_mlp_post)
    return loss_p, dx.reshape(B, S, D), small


def kernel(x, mem, positions, ln_mix_pre, w_in, b_gate, q_norm, w_uq, kv_norm, w_uk, w_uv, mem_norm, w_mem_kv, w_branch_out, w_out, ln_mix_post, ln_mlp_pre, w_mlp_up, w_mlp_down, ln_mlp_post, loss_target, m_ln_mix_pre, m_w_in, m_b_gate, m_q_norm, m_w_uq, m_kv_norm, m_w_uk, m_w_uv, m_mem_norm, m_w_mem_kv, m_w_branch_out, m_w_out, m_ln_mix_post, m_ln_mlp_pre, m_w_mlp_up, m_w_mlp_down, m_ln_mlp_post, v_ln_mix_pre, v_w_in, v_b_gate, v_q_norm, v_w_uq, v_kv_norm, v_w_uk, v_w_uv, v_mem_norm, v_w_mem_kv, v_w_branch_out, v_w_out, v_ln_mix_post, v_ln_mlp_pre, v_w_mlp_up, v_w_mlp_down, v_ln_mlp_post):
    given = dict(locals())
    w = {n: given[n][0] for n in WEIGHTS}
    m = {n: given["m_" + n][0] for n in WEIGHTS}
    v = {n: given["v_" + n][0] for n in WEIGHTS}
    sm = {n: w[n].reshape(1, -1) for n in SMALL}
    ex = _Exchange(w)
    loss_row, grad_x, g_small = _local_step(x, mem, positions, loss_target, sm, ex)

    res = {n: sum_adamw(ex.reduced[n], given[n], given["m_" + n], given["v_" + n], name=f"adamw_{n}") for n in BIG}

    small_shapes = [w[n].shape for n in SMALL] + [(LANES,)]
    parts = gather_two_level([_pack_rows([g_small[n] for n in SMALL] + [loss_row])], name="gather_small_grads")[0]
    no_row = jnp.zeros((1, LANES), F32)
    res_small = sum_adamw(parts, *[_pack_rows([t[n] for n in SMALL] + [no_row]) for t in (w, m, v)], name="adamw_replicated")
    res_small = [_unpack_rows(r, small_shapes) for r in res_small]
    for i, n in enumerate(SMALL):
        res[n] = [r[i][None] for r in res_small]
    loss = res_small[0][len(SMALL)][0]

    out = [loss, grad_x]
    for k in range(4):
        out += [res[n][k] for n in WEIGHTS]
    return tuple(out)
```

```python
import functools
import math

import numpy as np
import jax
import jax.numpy as jnp
from jax import lax
from jax.experimental import pallas as pl
from jax.experimental.pallas import tpu as pltpu

F32 = jnp.float32
BF16 = jnp.bfloat16

D_MODEL = 1024
MEM_HEADS, MEM_DIM = 4, 128
MLA_HEADS, MLA_NOPE, MLA_ROPE, MLA_V = 8, 64, 32, 64
Q_LORA, KV_LORA = 384, 256
ROPE_THETA = 10000.0
SB_HEADS, SB_DIM = 8, 64
D_FF = 4 * D_MODEL
N_BRANCH, BRANCH_W = 3, 512
EPS = 1e-6
SB_W = 3 * SB_HEADS * SB_DIM
QM_W = MEM_HEADS * MEM_DIM
GATE_W = N_BRANCH * D_MODEL
IN_SIZES = (Q_LORA, KV_LORA, MLA_ROPE, SB_W, QM_W, GATE_W)
IN_WIDTH = sum(IN_SIZES)
ADAM_LR, ADAM_B1, ADAM_B2, ADAM_EPS, ADAM_WD, ADAM_STEP = 0.001, 0.9, 0.999, 1e-08, 0.01, 10

N_DEV = 8
LANES = 128
KR_PAD = LANES
PAD_ORDER = ("gate", "cq", "kr", "ckv", "sb", "qm", "zpad")
MM_TILE = 1024
MM_ROWS = 2048
MM_DEPTH = 4096
MM_VMEM = 40 * 1024 * 1024
PAD_SIZES = dict(gate=GATE_W, cq=Q_LORA, kr=KR_PAD, ckv=KV_LORA, sb=SB_W, qm=QM_W)
PAD_SIZES["zpad"] = (-sum(PAD_SIZES.values())) % MM_TILE
PAD_OFF = {}
_o = 0
for _n in PAD_ORDER:
    PAD_OFF[_n] = _o
    _o += PAD_SIZES[_n]
IN_PAD = _o
HEAD_PAD = LANES
MLA_PAD_W = MLA_HEADS * HEAD_PAD
VMEM_LIMIT = 48 * 1024 * 1024


def _cparams(sem):
    return pltpu.CompilerParams(dimension_semantics=sem, vmem_limit_bytes=VMEM_LIMIT)


def _pick(n, pref):
    if n <= pref:
        return n
    t = (pref // LANES) * LANES
    while t >= LANES:
        if n % t == 0:
            return t
        t -= LANES
    return n


def matmul(a, b, *, name, ta=False, tb=False, out_dtype=F32, acc=None, act=None, act_in=None,
           b_slots=None, out_slots=None, rows=None, comm=None, tm_pref=MM_ROWS, tn_pref=MM_TILE, tk_pref=MM_DEPTH):
    M, K = (a.shape[1], a.shape[0]) if ta else a.shape
    tm, tk = _pick(M, tm_pref), _pick(K, tk_pref)
    if b_slots == "n":
        assert not tb and b.shape[:2] == (N_DEV, K)
        tn = b.shape[2]
        N = N_DEV * tn
    elif b_slots == "k":
        assert tb and N_DEV * b.shape[2] == K
        N, tk = b.shape[1], b.shape[2]
        tn = _pick(N, tn_pref)
    else:
        N = b.shape[0] if tb else b.shape[1]
        assert (b.shape[1] if tb else b.shape[0]) == K
        tn = _pick(N, tn_pref)
    if out_slots == "m":
        tm = M // N_DEV
    elif out_slots == "n":
        tn = N // N_DEV
    assert out_slots is None or (acc is None and act_in is None)
    if rows is not None:
        assert acc is None and act is None and out_slots is None
        tn = N
        row_ins, vec_ins = list(rows["row_ins"]), list(rows["vec_ins"])
        row_outs, vec_outs = list(rows["row_outs"]), list(rows["vec_outs"])

    def block_bytes(tm_, tk_):
        ab = tm_ * tk_ * a.dtype.itemsize + tk_ * tn * b.dtype.itemsize
        per_out = jnp.dtype(out_dtype).itemsize * (2 if act == "relu2" else 1)
        per_out += sum(t.dtype.itemsize for t in (acc, act_in) if t is not None)
        if rows is not None:
            per_out = sum(t.dtype.itemsize for t in row_ins) + sum(jnp.dtype(d).itemsize for d in row_outs)
        return 2 * (ab + tm_ * tn * per_out) + (tm_ * tn * 4 if K // tk_ > 1 else 0)

    while block_bytes(tm, tk) > MM_VMEM:
        if tm > 512 and out_slots != "m" and M % (tm // 2) == 0:
            tm //= 2
        elif tk > 512 and b_slots != "k" and K % (tk // 2) == 0:
            tk //= 2
        else:
            break
    nk = K // tk
    dims = (((0 if ta else 1,), (1 if tb else 0,)), ((), ()))
    assert act in (None, "relu2", "relu2_bwd") and (act == "relu2_bwd") == (act_in is not None)

    def body(*refs):
        a_ref, b_ref = refs[0], refs[1]
        pos = 2
        acc_ref = act_ref = None
        if acc is not None:
            acc_ref = refs[pos]
            pos += 1
        if act_in is not None:
            act_ref = refs[pos]
            pos += 1
        if rows is not None:
            n_ri, n_vi, n_ro, n_vo = len(row_ins), len(vec_ins), len(row_outs), len(vec_outs)
            ri_refs, vi_refs = refs[pos:pos + n_ri], refs[pos + n_ri:pos + n_ri + n_vi]
            pos += n_ri + n_vi
            ro_refs, vo_refs = refs[pos:pos + n_ro], refs[pos + n_ro:pos + n_ro + n_vo]
            pos += n_ro + n_vo - 1
            row_i = pl.program_id(0)
        out = refs[pos]
        if act == "relu2":
            pos += 1
            out_relu = refs[pos]
        scratch = refs[pos + 1:]

        part = lax.dot_general(a_ref[...].astype(BF16), b_ref[...].astype(BF16), dims,
                               preferred_element_type=F32)

        def finish(r):
            if rows is not None:
                ro, vo = rows["fn"](r, *[t[...] for t in ri_refs], *[t[...] for t in vi_refs])
                for ref, val in zip(ro_refs, ro):
                    ref[...] = val.astype(ref.dtype)
                for ref, val in zip(vo_refs, vo):
                    _acc_rows(ref, row_i, val)
                return
            if acc_ref is not None:
                r = r + acc_ref[...]
            if act == "relu2":
                r = jnp.maximum(r, 0.0)
                out_relu[...] = r.astype(out_relu.dtype)
                r = jnp.square(r)
            elif act == "relu2_bwd":
                r = r * (2.0 * act_ref[...].astype(F32))
            out[...] = r.astype(out.dtype)

        if nk == 1:
            finish(part)
        else:
            acc_sc = scratch[0]
            k = pl.program_id(2)

            @pl.when(k == 0)
            def _():
                acc_sc[...] = part

            @pl.when(k > 0)
            def _():
                acc_sc[...] += part

            @pl.when(k == nk - 1)
            def _():
                finish(acc_sc[...])

    a_spec = pl.BlockSpec((tk, tm), lambda i, j, k: (k, i)) if ta else pl.BlockSpec((tm, tk), lambda i, j, k: (i, k))
    if b_slots == "n":
        b_spec = pl.BlockSpec((None, tk, tn), lambda i, j, k: (j, k, 0))
    elif b_slots == "k":
        b_spec = pl.BlockSpec((None, tn, tk), lambda i, j, k: (k, j, 0))
    else:
        b_spec = pl.BlockSpec((tn, tk), lambda i, j, k: (j, k)) if tb else pl.BlockSpec((tk, tn), lambda i, j, k: (k, j))
    if out_slots == "m":
        o_spec = pl.BlockSpec((None, None, tm, tn), lambda i, j, k: (i % 2, i // 2, 0, j))
        o_shape = (2, N_CHIP, tm, N)
    elif out_slots == "n":
        o_spec = pl.BlockSpec((None, None, tm, tn), lambda i, j, k: (j % 2, j // 2, i, 0))
        o_shape = (2, N_CHIP, M, tn)
    else:
        o_spec = pl.BlockSpec((tm, tn), lambda i, j, k: (i, j))
        o_shape = (M, N)
    in_specs = [a_spec, b_spec]
    args = [a, b]
    if acc is not None:
        in_specs.append(o_spec)
        args.append(acc)
    if act_in is not None:
        in_specs.append(o_spec)
        args.append(act_in)
    out_specs, out_shapes, sem = [o_spec], [jax.ShapeDtypeStruct(o_shape, out_dtype)], ("parallel", "parallel", "arbitrary")
    if act == "relu2":
        out_specs, out_shapes = out_specs * 2, out_shapes * 2
    if rows is not None:
        def vec_spec(w):
            return pl.BlockSpec((1, w), lambda i, j, k: (0, 0))

        in_specs += [o_spec] * len(row_ins) + [vec_spec(t.shape[1]) for t in vec_ins]
        args += row_ins + vec_ins
        out_specs = [o_spec] * len(row_outs) + [vec_spec(w) for w in vec_outs]
        out_shapes = ([jax.ShapeDtypeStruct((M, N), d) for d in row_outs]
                      + [jax.ShapeDtypeStruct((1, w), F32) for w in vec_outs])
        sem = ("arbitrary",) * 3
    outs, c_outs = _call(
        body, name=name, grid=(M // tm, N // tn, nk), in_specs=in_specs, out_specs=out_specs, out_shape=out_shapes,
        scratch_shapes=[pltpu.VMEM((tm, tn), F32)] if nk > 1 else [], args=args, sem=sem, comm=comm)
    if rows is not None:
        return (*outs, c_outs) if comm is not None else tuple(outs)
    if act == "relu2":
        assert comm is None
        return outs[0], outs[1]
    return (outs[0], c_outs) if comm is not None else outs[0]


ROW_TILE = 512


def _rstd(xv):
    return lax.rsqrt(jnp.mean(xv * xv, axis=-1, keepdims=True) + EPS)


def _rms_bwd_rows(dy, xv, g):
    r = _rstd(xv)
    dyg = dy * g
    dx = r * dyg - xv * (r * r * r) * jnp.mean(dyg * xv, axis=-1, keepdims=True)
    return dx, dy * xv * r


def _row_spec(tm, n, col=0):
    return pl.BlockSpec((tm, n), lambda i: (i, col))


def _vec_spec(n):
    return pl.BlockSpec((1, n), lambda i: (0, 0))


def _acc_rows(ref, i, val):
    @pl.when(i == 0)
    def _():
        ref[...] = val

    @pl.when(i > 0)
    def _():
        ref[...] += val


def rms_fwd(x, g, *, name, n=None, col=0, out_dtype=BF16, comm=None):
    T = x.shape[0]
    n = x.shape[1] if n is None else n
    tm = _pick(T, ROW_TILE)

    def body(x_ref, g_ref, o_ref):
        xv = x_ref[...].astype(F32)
        o_ref[...] = (xv * _rstd(xv) * g_ref[...]).astype(o_ref.dtype)

    outs, c_outs = _call(
        body, name=name, grid=(T // tm,), in_specs=[_row_spec(tm, n, col), _vec_spec(n)],
        out_specs=[_row_spec(tm, n)], out_shape=[jax.ShapeDtypeStruct((T, n), out_dtype)],
        scratch_shapes=[], args=(x, g), sem=("parallel",), comm=comm)
    return (outs[0], c_outs) if comm is not None else outs[0]


def rms_bwd(dy, x, g, *, name, n=None, col=0, out_dtype=F32, comm=None):
    T = x.shape[0]
    n = x.shape[1] if n is None else n
    tm = _pick(T, ROW_TILE)

    def body(dy_ref, x_ref, g_ref, dx_ref, dg_ref):
        dx, dgr = _rms_bwd_rows(dy_ref[...].astype(F32), x_ref[...].astype(F32), g_ref[...])
        dx_ref[...] = dx.astype(dx_ref.dtype)
        _acc_rows(dg_ref, pl.program_id(0), jnp.sum(dgr, axis=0, keepdims=True))

    outs, c_outs = _call(
        body, name=name, grid=(T // tm,), in_specs=[_row_spec(tm, n), _row_spec(tm, n, col), _vec_spec(n)],
        out_specs=(_row_spec(tm, n), _vec_spec(n)),
        out_shape=(jax.ShapeDtypeStruct((T, n), out_dtype), jax.ShapeDtypeStruct((1, n), F32)),
        scratch_shapes=[], args=(dy, x, g), sem=("arbitrary",), comm=comm)
    return (*outs, c_outs) if comm is not None else tuple(outs)


def _mix_post_rows(y, x, g_post, g_pre2):
    x1 = x + y * _rstd(y) * g_post
    return [y, x1, x1 * _rstd(x1) * g_pre2], []


def _loss_rows(d, x1, tgt, g):
    err = x1 + d * _rstd(d) * g - tgt
    part = 0.5 * jnp.sum(jnp.mean(err * err, axis=-1, keepdims=True), axis=0, keepdims=True)
    dx2 = err * (1.0 / d.shape[-1])
    dd, dgr = _rms_bwd_rows(dx2, d, g)
    return [dx2, dd], [jnp.broadcast_to(part, (1, LANES)), jnp.sum(dgr, axis=0, keepdims=True)]


def _mlp_pre_bwd_rows(dh2, dx2, x1, y, g_pre2, g_post):
    d1, dg2 = _rms_bwd_rows(dh2, x1, g_pre2)
    dx1 = dx2 + d1
    dy, dgp = _rms_bwd_rows(dx1, y, g_post)
    return [dx1, dy], [jnp.sum(dg2, axis=0, keepdims=True), jnp.sum(dgp, axis=0, keepdims=True)]


def _mix_pre_bwd_rows(dh, x, dx1, g):
    dx, dgr = _rms_bwd_rows(dh, x, g)
    return [dx + dx1], [jnp.sum(dgr, axis=0, keepdims=True)]


def _gate_specs(tm):
    gcol = PAD_OFF["gate"] // D_MODEL
    lspecs = [pl.BlockSpec((tm, D_MODEL), functools.partial(lambda i, c: (i, c), c=gcol + b)) for b in range(N_BRANCH)]
    bspecs = [pl.BlockSpec((1, D_MODEL), functools.partial(lambda i, c: (0, c), c=b)) for b in range(N_BRANCH)]
    pspecs = [_row_spec(tm, D_MODEL) for _ in range(N_BRANCH)]
    return lspecs, bspecs, pspecs


def gate_merge(proj, b_gate, ps, *, name):
    T = proj.shape[0]
    tm = _pick(T, ROW_TILE)
    lspecs, bspecs, pspecs = _gate_specs(tm)

    def body(*refs):
        l_refs, b_refs, p_refs, o_ref = refs[0:3], refs[3:6], refs[6:9], refs[9]
        tot = None
        for lr, br, pr in zip(l_refs, b_refs, p_refs):
            term = jax.nn.sigmoid(lr[...].astype(F32) + br[...]) * pr[...].astype(F32)
            tot = term if tot is None else tot + term
        o_ref[...] = tot.astype(o_ref.dtype)

    return pl.pallas_call(
        body, name=name, grid=(T // tm,), in_specs=lspecs + bspecs + pspecs,
        out_specs=_row_spec(tm, D_MODEL), out_shape=jax.ShapeDtypeStruct((T, D_MODEL), BF16),
        compiler_params=_cparams(("parallel",)),
    )(proj, proj, proj, b_gate, b_gate, b_gate, *ps)


def gate_bwd(dmerged, proj, b_gate, ps, *, name):
    T = proj.shape[0]
    tm = _pick(T, ROW_TILE)
    lspecs, bspecs, pspecs = _gate_specs(tm)

    def body(*refs):
        dm_ref = refs[0]
        l_refs, b_refs, p_refs = refs[1:4], refs[4:7], refs[7:10]
        dl_ref, dp_refs, db_ref = refs[10], refs[11:14], refs[14]
        i = pl.program_id(0)
        dm = dm_ref[...]
        for b, (lr, br, pr, dpr) in enumerate(zip(l_refs, b_refs, p_refs, dp_refs)):
            gt = jax.nn.sigmoid(lr[...].astype(F32) + br[...])
            dpr[...] = (dm * gt).astype(dpr.dtype)
            dl = dm * pr[...].astype(F32) * gt * (1.0 - gt)
            dl_ref[:, b * D_MODEL:(b + 1) * D_MODEL] = dl.astype(dl_ref.dtype)
            part = jnp.sum(dl, axis=0, keepdims=True)

            @pl.when(i == 0)
            def _():
                db_ref[:, b * D_MODEL:(b + 1) * D_MODEL] = part

            @pl.when(i > 0)
            def _():
                db_ref[:, b * D_MODEL:(b + 1) * D_MODEL] += part

    return pl.pallas_call(
        body, name=name, grid=(T // tm,), in_specs=[_row_spec(tm, D_MODEL)] + lspecs + bspecs + pspecs,
        out_specs=(_row_spec(tm, GATE_W), *[_row_spec(tm, D_MODEL) for _ in range(N_BRANCH)], _vec_spec(GATE_W)),
        out_shape=(jax.ShapeDtypeStruct((T, GATE_W), BF16),
                   *[jax.ShapeDtypeStruct((T, D_MODEL), BF16) for _ in range(N_BRANCH)],
                   jax.ShapeDtypeStruct((1, GATE_W), F32)),
        compiler_params=_cparams(("arbitrary",)),
    )(dmerged, proj, proj, proj, b_gate, b_gate, b_gate, *ps)


def _rope_tables(pos_ref, invf_ref):
    ang = pos_ref[...] * invf_ref[...]
    lane = lax.broadcasted_iota(jnp.int32, (1, LANES), 1)
    s = jnp.sin(ang)
    split = MLA_NOPE + MLA_ROPE // 2
    return jnp.cos(ang), jnp.where(lane >= split, s, 0.0), jnp.where(lane < split, s, 0.0)


def _rotate(xh, c, s_hi, s_lo, sign):
    half = MLA_ROPE // 2
    up = pltpu.roll(xh, half, 1)
    down = pltpu.roll(xh, LANES - half, 1)
    return xh * c + sign * (up * s_hi - down * s_lo)


def rope_fwd(qpre, kpre, proj, pos, invf, *, name):
    T = qpre.shape[0]
    tm = _pick(T, ROW_TILE)
    kr_col = PAD_OFF["kr"] // KR_PAD

    def body(q_ref, k_ref, kr_ref, pos_ref, invf_ref, qo_ref, ko_ref):
        c, s_hi, s_lo = _rope_tables(pos_ref, invf_ref)
        kr = _rotate(pltpu.roll(kr_ref[...].astype(F32), MLA_NOPE, 1), c, s_hi, s_lo, 1.0)
        for h in range(MLA_HEADS):
            sl = slice(h * HEAD_PAD, (h + 1) * HEAD_PAD)
            qo_ref[:, sl] = _rotate(q_ref[:, sl].astype(F32), c, s_hi, s_lo, 1.0).astype(qo_ref.dtype)
            ko_ref[:, sl] = (k_ref[:, sl].astype(F32) + kr).astype(ko_ref.dtype)

    rs = _row_spec(tm, MLA_PAD_W)
    return pl.pallas_call(
        body, name=name, grid=(T // tm,),
        in_specs=[rs, rs, _row_spec(tm, KR_PAD, kr_col), _row_spec(tm, 1), _vec_spec(LANES)],
        out_specs=(rs, rs),
        out_shape=(jax.ShapeDtypeStruct((T, MLA_PAD_W), BF16), jax.ShapeDtypeStruct((T, MLA_PAD_W), BF16)),
        compiler_params=_cparams(("parallel",)),
    )(qpre, kpre, proj, pos, invf)


def rope_bwd(dq_pad, dk_pad, pos, invf, *, name):
    T = dq_pad.shape[0]
    tm = _pick(T, ROW_TILE)

    def body(dq_ref, dk_ref, pos_ref, invf_ref, dqo_ref, dkr_ref):
        c, s_hi, s_lo = _rope_tables(pos_ref, invf_ref)
        tot = None
        for h in range(MLA_HEADS):
            sl = slice(h * HEAD_PAD, (h + 1) * HEAD_PAD)
            dqo_ref[:, sl] = _rotate(dq_ref[:, sl].astype(F32), c, s_hi, s_lo, -1.0).astype(dqo_ref.dtype)
            tot = dk_ref[:, sl] if tot is None else tot + dk_ref[:, sl]
        dkr = pltpu.roll(_rotate(tot, c, s_hi, s_lo, -1.0), LANES - MLA_NOPE, 1)
        lane = lax.broadcasted_iota(jnp.int32, (1, LANES), 1)
        dkr_ref[...] = jnp.where(lane < MLA_ROPE, dkr, 0.0).astype(dkr_ref.dtype)

    rs = _row_spec(tm, MLA_PAD_W)
    return pl.pallas_call(
        body, name=name, grid=(T // tm,),
        in_specs=[rs, rs, _row_spec(tm, 1), _vec_spec(LANES)],
        out_specs=(rs, _row_spec(tm, KR_PAD)),
        out_shape=(jax.ShapeDtypeStruct((T, MLA_PAD_W), BF16), jax.ShapeDtypeStruct((T, KR_PAD), BF16)),
        compiler_params=_cparams(("parallel",)),
    )(dq_pad, dk_pad, pos, invf)


ATT_TILE = 512
NEG = -1e30


def _split_bf16(v):
    hi = v.astype(BF16)
    return hi, (v - hi.astype(F32)).astype(BF16)


def _mul(x, s):
    return x if s == 1.0 else x * s


def _dot(a, b, ca, cb):
    return lax.dot_general(a, b, (((ca,), (cb,)), ((), ())), preferred_element_type=F32)


CUM_CHUNK = 256


def _tri(kind, terms=2):
    r = lax.broadcasted_iota(jnp.int32, (CUM_CHUNK, CUM_CHUNK), 0)
    c = lax.broadcasted_iota(jnp.int32, (CUM_CHUNK, CUM_CHUNK), 1)
    m = {"gt": r > c, "le": r <= c, "lt": r < c}[kind]
    t = jnp.where(m, 1.0, 0.0).astype(BF16)
    return jnp.concatenate([t] * terms, axis=0)


def _cum(v, tri, kind):
    n = v.shape[1] // CUM_CHUNK
    two = tri.shape[0] == 2 * CUM_CHUNK
    chunks = [v[:, c * CUM_CHUNK:(c + 1) * CUM_CHUNK] for c in range(n)]
    totals = [jnp.sum(ch, axis=1, keepdims=True) for ch in chunks] if n > 1 else None
    outs = []
    for c, ch in enumerate(chunks):
        r = _dot(jnp.concatenate(_split_bf16(ch), axis=1) if two else ch.astype(BF16), tri, 1, 0)
        others = [] if n == 1 else totals[c + 1:] if kind == "gt" else totals[:c]
        for t in others:
            r = r + t
        outs.append(r)
    return outs[0] if n == 1 else jnp.concatenate(outs, axis=1)


def _lane_masks(hp, w):
    lane = lax.broadcasted_iota(jnp.int32, (1, LANES), 1)
    return [(lane >= e * w) & (lane < (e + 1) * w) for e in range(hp)]


def _att_dims(mode, hp, qk_w, v_w):
    assert mode in ("softmax", "sb")
    assert (hp, qk_w, v_w) in ((2, 128, 64), (2, 64, 64), (1, 128, 128))
    qw = hp * LANES if qk_w == LANES else LANES
    return qw


def attn_fwd(q, k, v, *, name, mode, causal, hp, qk_w, v_w, scale, nh, qc, kc, vc, comm=None):
    qw = _att_dims(mode, hp, qk_w, v_w)
    B, S = q.shape[0], q.shape[1]
    Sk = k.shape[1]
    tq = _pick(S, ATT_TILE)
    tk = _pick(Sk, ATT_TILE)
    if causal:
        assert tq == tk and S == Sk
    nq, nkv = S // tq, Sk // tk
    pre_scaled = math.frexp(scale)[0] == 0.5
    post = 1.0 if pre_scaled else scale

    def body(q_ref, k_ref, v_ref, o_ref, st_ref):
        i = pl.program_id(2)
        qv = q_ref[0].astype(BF16)
        masks_qk = _lane_masks(hp, qk_w) if qk_w < LANES else None
        masks_v = _lane_masks(hp, v_w) if v_w < LANES else None
        if qk_w == LANES:
            qs = [qv[:, e * LANES:(e + 1) * LANES] for e in range(hp)]
        else:
            qs = [jnp.where(masks_qk[e], qv, jnp.zeros_like(qv)) for e in range(hp)]
        if pre_scaled:
            qs = [t * scale for t in qs]
        rows = lax.broadcasted_iota(jnp.int32, (tq, tk), 0)
        cols = lax.broadcasted_iota(jnp.int32, (tq, tk), 1)
        tri = _tri("gt") if mode == "sb" else None

        def kv_tile(j):
            off = pl.multiple_of(j * tk, tk)
            kj = k_ref[0, pl.ds(off, tk), :].astype(BF16)
            vj = v_ref[0, pl.ds(off, tk), :].astype(BF16)
            ks = [kj[:, e * LANES:(e + 1) * LANES] for e in range(hp)] if qk_w == LANES else [kj] * hp
            return ks, vj

        def merge(vals):
            if hp == 1:
                return jnp.broadcast_to(vals[0], (tq, LANES))
            return jnp.where(masks_v[0], vals[0], vals[1])

        if mode == "softmax":
            def block(j, carry, diag):
                ms, ls, acc = carry
                ks, vj = kv_tile(j)
                new_m, new_l, alphas, pvs = [], [], [], []
                for e in range(hp):
                    s = _mul(_dot(qs[e], ks[e], 1, 1), post)
                    if diag:
                        s = jnp.where(rows >= cols, s, NEG)
                    m_new = jnp.maximum(ms[e], jnp.max(s, axis=1, keepdims=True))
                    alpha = jnp.exp(ms[e] - m_new)
                    p = jnp.exp(s - m_new)
                    new_l.append(alpha * ls[e] + jnp.sum(p, axis=1, keepdims=True))
                    new_m.append(m_new)
                    alphas.append(alpha)
                    pvs.append(_dot(p.astype(BF16), vj, 1, 0))
                acc = acc * merge(alphas) + merge(pvs)
                return tuple(new_m), tuple(new_l), acc

            init = (tuple(jnp.full((tq, 1), NEG, F32) for _ in range(hp)),
                    tuple(jnp.zeros((tq, 1), F32) for _ in range(hp)),
                    jnp.zeros((tq, LANES), F32))
            if causal:
                carry = lax.fori_loop(0, i, lambda j, c: block(j, c, False), init)
                ms, ls, acc = block(i, carry, True)
            else:
                ms, ls, acc = lax.fori_loop(0, nkv, lambda j, c: block(j, c, False), init)
            o_ref[0] = acc / merge(list(ls))
            st_ref[0, 0] = merge([m + jnp.log(l) for m, l in zip(ms, ls)])
        else:
            def block(j, carry, diag):
                cs_, acc = carry
                ks, vj = kv_tile(j)
                new_c, pvs = [], []
                for e in range(hp):
                    z = _mul(_dot(qs[e], ks[e], 1, 1), post)
                    lk = -jnp.maximum(z, 0.0) - jnp.log(1.0 + jnp.exp(-jnp.abs(z)))
                    lz = lk + z
                    if diag:
                        lk = jnp.where(rows > cols, lk, 0.0)
                    a = jnp.exp(lz + _cum(lk, tri, "gt") + cs_[e])
                    if diag:
                        a = jnp.where(rows > cols, a, 0.0)
                    pvs.append(_dot(a.astype(BF16), vj, 1, 0))
                    new_c.append(cs_[e] + jnp.sum(lk, axis=1, keepdims=True))
                return tuple(new_c), acc + merge(pvs)

            init = (tuple(jnp.zeros((tq, 1), F32) for _ in range(hp)), jnp.zeros((tq, LANES), F32))
            carry = block(i, init, True)
            cs_, acc = lax.fori_loop(0, i, lambda jj, c: block(i - 1 - jj, c, False), carry)
            o_ref[0] = acc
            st_ref[0, 0] = merge(list(cs_))

    outs, c_outs = _call(
        body, name=name, grid=(B, nh, nq),
        in_specs=[pl.BlockSpec((1, tq, qw), lambda b, h, i: (b, i, qc + h)),
                  pl.BlockSpec((1, Sk, qw), lambda b, h, i: (b, 0, kc + h)),
                  pl.BlockSpec((1, Sk, LANES), lambda b, h, i: (b, 0, vc + h))],
        out_specs=(pl.BlockSpec((1, tq, LANES), lambda b, h, i: (b, i, h)),
                   pl.BlockSpec((1, 1, tq, LANES), lambda b, h, i: (b, h, i, 0))),
        out_shape=(jax.ShapeDtypeStruct((B, S, nh * LANES), F32), jax.ShapeDtypeStruct((B, nh, S, LANES), F32)),
        scratch_shapes=[], args=(q, k, v), sem=("parallel", "parallel", "arbitrary"), comm=comm)
    return (*outs, c_outs) if comm is not None else tuple(outs)


def attn_bwd(q, k, v, o, st, do, *, name, mode, causal, hp, qk_w, v_w, scale, nh, qc, kc, vc, comm=None):
    qw = _att_dims(mode, hp, qk_w, v_w)
    B, S = q.shape[0], q.shape[1]
    Sk = k.shape[1]
    tq = _pick(S, ATT_TILE)
    tk = _pick(Sk, ATT_TILE)
    if causal:
        assert tq == tk and S == Sk
    nq, nkv = S // tq, Sk // tk
    pre_scaled = math.frexp(scale)[0] == 0.5
    post = 1.0 if pre_scaled else scale

    def body(q_ref, k_ref, v_ref, o_ref, st_ref, do_ref, dq_ref, dk_ref, dv_ref):
        i = pl.program_id(2)

        @pl.when(i == 0)
        def _():
            dk_ref[...] = jnp.zeros_like(dk_ref)
            dv_ref[...] = jnp.zeros_like(dv_ref)

        qv = q_ref[0].astype(BF16)
        dov = do_ref[0]
        stv = st_ref[0, 0]
        masks_qk = _lane_masks(hp, qk_w) if qk_w < LANES else None
        masks_v = _lane_masks(hp, v_w) if v_w < LANES else None
        if qk_w == LANES:
            qs = [qv[:, e * LANES:(e + 1) * LANES] for e in range(hp)]
        else:
            qs = [jnp.where(masks_qk[e], qv, jnp.zeros_like(qv)) for e in range(hp)]
        if pre_scaled:
            qs = [t * scale for t in qs]
        if hp == 1:
            dos = [dov.astype(BF16)]
            stats = [stv[:, 0:1]]
        else:
            dos = [jnp.where(masks_v[e], dov, 0.0).astype(BF16) for e in range(hp)]
            stats = [stv[:, e * v_w:e * v_w + 1] for e in range(hp)]
        if mode == "softmax":
            prod = dov * o_ref[0]
            if hp == 1:
                dsum = [jnp.sum(prod, axis=1, keepdims=True)]
            else:
                dsum = [jnp.sum(jnp.where(masks_v[e], prod, 0.0), axis=1, keepdims=True) for e in range(hp)]
        rows = lax.broadcasted_iota(jnp.int32, (tq, tk), 0)
        cols = lax.broadcasted_iota(jnp.int32, (tq, tk), 1)
        if mode == "sb":
            tri_le, tri_lt = _tri("le"), _tri("lt", terms=1)

        def kv_tile(j):
            off = pl.multiple_of(j * tk, tk)
            kj = k_ref[0, pl.ds(off, tk), :].astype(BF16)
            vj = v_ref[0, pl.ds(off, tk), :].astype(BF16)
            ks = [kj[:, e * LANES:(e + 1) * LANES] for e in range(hp)] if qk_w == LANES else [kj] * hp
            return off, ks, vj

        def scatter(off, dz_list, p_list):
            dvj = None
            for e in range(hp):
                t = _dot(p_list[e], dos[e], 0, 0)
                dvj = t if dvj is None else dvj + t
            dv_ref[0, pl.ds(off, tk), :] += dvj
            if qk_w == LANES:
                for e in range(hp):
                    dk_ref[0, pl.ds(off, tk), e * LANES:(e + 1) * LANES] += _dot(dz_list[e], qs[e], 0, 0)
            else:
                dkj = None
                for e in range(hp):
                    t = _dot(dz_list[e], qs[e], 0, 0)
                    dkj = t if dkj is None else dkj + t
                dk_ref[0, pl.ds(off, tk), :] += dkj

        def dq_add(dqs, dz_list, ks):
            out = []
            for e in range(hp):
                out.append(dqs[e] + _dot(dz_list[e], ks[e], 1, 0))
            return tuple(out)

        dq0 = tuple(jnp.zeros((tq, LANES), F32) for _ in range(hp))

        if mode == "softmax":
            def block(j, dqs, diag):
                off, ks, vj = kv_tile(j)
                dzs, ps = [], []
                for e in range(hp):
                    s = _mul(_dot(qs[e], ks[e], 1, 1), post)
                    p = jnp.exp(s - stats[e])
                    if diag:
                        p = jnp.where(rows >= cols, p, 0.0)
                    dp = _dot(dos[e], vj, 1, 1)
                    dzs.append(_mul(p * (dp - dsum[e]), post).astype(BF16))
                    ps.append(p.astype(BF16))
                scatter(off, dzs, ps)
                return dq_add(dqs, dzs, ks)

            if causal:
                dqs = lax.fori_loop(0, i, lambda j, c: block(j, c, False), dq0)
                dqs = block(i, dqs, True)
            else:
                dqs = lax.fori_loop(0, nkv, lambda j, c: block(j, c, False), dq0)
        else:
            def block(j, carry, diag):
                dqs, cps, cgs = carry
                off, ks, vj = kv_tile(j)
                dzs, ps, new_p, new_g = [], [], [], []
                for e in range(hp):
                    z = _mul(_dot(qs[e], ks[e], 1, 1), post)
                    lk = -jnp.maximum(z, 0.0) - jnp.log(1.0 + jnp.exp(-jnp.abs(z)))
                    lz = lk + z
                    sig = jnp.exp(lz)
                    keep = 1.0 - sig
                    if diag:
                        lk = jnp.where(rows > cols, lk, 0.0)
                    keep_after = stats[e] - cps[e] - _cum(lk, tri_le, "le")
                    a = jnp.exp(lz + keep_after)
                    if diag:
                        a = jnp.where(rows > cols, a, 0.0)
                    g = _dot(dos[e], vj, 1, 1) * a
                    gsum = cgs[e] + _cum(g, tri_lt, "lt")
                    dz = _mul(g * keep - gsum * sig, post)
                    if diag:
                        dz = jnp.where(rows > cols, dz, 0.0)
                    dzs.append(dz.astype(BF16))
                    ps.append(a.astype(BF16))
                    new_p.append(cps[e] + jnp.sum(lk, axis=1, keepdims=True))
                    new_g.append(cgs[e] + jnp.sum(g, axis=1, keepdims=True))
                scatter(off, dzs, ps)
                return dq_add(dqs, dzs, ks), tuple(new_p), tuple(new_g)

            zc = tuple(jnp.zeros((tq, 1), F32) for _ in range(hp))
            carry = lax.fori_loop(0, i, lambda j, c: block(j, c, False), (dq0, zc, zc))
            dqs, _, _ = block(i, carry, True)

        if pre_scaled:
            dqs = [t * scale for t in dqs]
        if qk_w == LANES:
            for e in range(hp):
                dq_ref[0, :, e * LANES:(e + 1) * LANES] = dqs[e].astype(dq_ref.dtype)
        else:
            dq_ref[0] = jnp.where(masks_qk[0], dqs[0], dqs[1]).astype(dq_ref.dtype)

    outs, c_outs = _call(
        body, name=name, grid=(B, nh, nq),
        in_specs=[pl.BlockSpec((1, tq, qw), lambda b, h, i: (b, i, qc + h)),
                  pl.BlockSpec((1, Sk, qw), lambda b, h, i: (b, 0, kc + h)),
                  pl.BlockSpec((1, Sk, LANES), lambda b, h, i: (b, 0, vc + h)),
                  pl.BlockSpec((1, tq, LANES), lambda b, h, i: (b, i, h)),
                  pl.BlockSpec((1, 1, tq, LANES), lambda b, h, i: (b, h, i, 0)),
                  pl.BlockSpec((1, tq, LANES), lambda b, h, i: (b, i, h))],
        out_specs=(pl.BlockSpec((1, tq, qw), lambda b, h, i: (b, i, h)),
                   pl.BlockSpec((1, Sk, qw), lambda b, h, i: (b, 0, h)),
                   pl.BlockSpec((1, Sk, LANES), lambda b, h, i: (b, 0, h))),
        out_shape=(jax.ShapeDtypeStruct((B, S, nh * qw), BF16), jax.ShapeDtypeStruct((B, Sk, nh * qw), F32),
                   jax.ShapeDtypeStruct((B, Sk, nh * LANES), F32)),
        scratch_shapes=[], args=(q, k, v, o, st, do), sem=("parallel", "parallel", "arbitrary"), comm=comm)
    return (*outs, c_outs) if comm is not None else tuple(outs)


def _pos(p):
    return 4 * p[0] + 2 * p[1] + p[2]


N_CHIP = 4
MESH_ID = pl.DeviceIdType.MESH
_ANY = pl.BlockSpec(memory_space=pl.ANY)


def _me():
    return lax.axis_index("x"), lax.axis_index("y"), lax.axis_index("c")


def _other_chips(x, y):
    return [(1 - x, y), (x, 1 - y), (1 - x, 1 - y)]


class _Comm:
    def __init__(self, arrs, out_shapes, n_sem, n_local, start, finish):
        self.arrs, self.out_shapes, self.start, self.finish = list(arrs), list(out_shapes), start, finish
        self.scratch = [pltpu.SemaphoreType.DMA((n_sem,)), pltpu.SemaphoreType.DMA((n_sem,)),
                        pltpu.SemaphoreType.DMA((max(n_local, 1),))]


def _run_comm(comm, name):
    n = len(comm.arrs)

    def body(*refs):
        r = (refs[:n], refs[n:2 * n], refs[2 * n], refs[2 * n + 1], refs[2 * n + 2])
        comm.start(*r)
        comm.finish(*r)

    return pl.pallas_call(
        body, name=name, in_specs=[_ANY] * n, out_specs=[_ANY] * n, out_shape=comm.out_shapes,
        scratch_shapes=comm.scratch, compiler_params=pltpu.CompilerParams(has_side_effects=True),
    )(*comm.arrs)


def _call(body, *, name, grid, in_specs, out_specs, out_shape, scratch_shapes, args, sem, comm=None):
    in_specs, out_specs, out_shape = list(in_specs), list(out_specs), list(out_shape)
    if comm is None:
        res = pl.pallas_call(body, name=name, grid=grid, in_specs=in_specs, out_specs=out_specs, out_shape=out_shape,
                             scratch_shapes=scratch_shapes, compiler_params=_cparams(sem))(*args)
        return list(res), []
    n_in, n_out, n_scr, nc = len(in_specs), len(out_specs), len(scratch_shapes), len(comm.arrs)

    def wrapped(*refs):
        ins, refs = refs[:n_in], refs[n_in:]
        c_in, refs = refs[:nc], refs[nc:]
        outs, refs = refs[:n_out], refs[n_out:]
        c_out, refs = refs[:nc], refs[nc:]
        scr, sems = refs[:n_scr], refs[n_scr:]
        ids = [pl.program_id(a) for a in range(len(grid))]
        first = functools.reduce(jnp.logical_and, [i == 0 for i in ids])
        last = functools.reduce(jnp.logical_and, [i == g - 1 for i, g in zip(ids, grid)])

        @pl.when(first)
        def _():
            comm.start(c_in, c_out, *sems)

        body(*ins, *outs, *scr)

        @pl.when(last)
        def _():
            comm.finish(c_in, c_out, *sems)

    res = pl.pallas_call(
        wrapped, name=name, grid=grid, in_specs=in_specs + [_ANY] * nc, out_specs=out_specs + [_ANY] * nc,
        out_shape=out_shape + comm.out_shapes, scratch_shapes=list(scratch_shapes) + comm.scratch,
        compiler_params=pltpu.CompilerParams(dimension_semantics=("arbitrary",) * len(grid),
                                             vmem_limit_bytes=VMEM_LIMIT, has_side_effects=True),
    )(*args, *comm.arrs)
    return list(res[:n_out]), list(res[n_out:])


def gather_two_level(bufs, *, name=None):
    n = len(bufs)

    def parts(x_refs, out_refs, send_sems, recv_sems, local_sems):
        x, y, c = _me()
        me, sibling = (x, y, c), (x, y, 1 - c)
        chips = _other_chips(x, y)

        def copy(a, k, block, to, from_input=False):
            dst = out_refs[a].at[_pos(block)]
            return pltpu.make_async_remote_copy(src_ref=x_refs[a] if from_input else dst, dst_ref=dst,
                                                send_sem=send_sems.at[7 * a + k], recv_sem=recv_sems.at[7 * a + k],
                                                device_id=to, device_id_type=MESH_ID)

        mine = [pltpu.make_async_copy(x_refs[a], out_refs[a].at[_pos(me)], local_sems.at[a]) for a in range(n)]
        first = []
        for a in range(n):
            first.append(copy(a, 0, me, sibling, from_input=True))
            first += [copy(a, 1 + j, me, (*chip, c), from_input=True) for j, chip in enumerate(chips)]
        return copy, mine, first, me, sibling, chips, c

    def start(*refs):
        _, mine, first, *_ = parts(*refs)
        for cp in mine + first:
            cp.start()

    def finish(*refs):
        copy, mine, first, me, sibling, chips, c = parts(*refs)
        passed = []
        for j, chip in enumerate(chips):
            for a in range(n):
                copy(a, 1 + j, (*chip, c), me).wait_recv()
                passed.append(copy(a, 4 + j, (*chip, c), sibling))
                passed[-1].start()
        for a in range(n):
            copy(a, 0, sibling, me).wait_recv()
            for j, chip in enumerate(chips):
                copy(a, 4 + j, (*chip, 1 - c), me).wait_recv()
        for cp in first + passed:
            cp.wait_send()
        for cp in mine:
            cp.wait()

    out_shapes = [jax.ShapeDtypeStruct((N_DEV,) + b.shape, b.dtype) for b in bufs]
    comm = _Comm(bufs, out_shapes, 7 * n, n, start, finish)
    return _run_comm(comm, name) if name else comm


def sibling_exchange(sends, *, name=None):
    n = len(sends)

    def copies(s_refs, out_refs, send_sems, recv_sems, local_sems):
        x, y, c = _me()
        return [pltpu.make_async_remote_copy(src_ref=s_refs[a].at[1 - c], dst_ref=out_refs[a], send_sem=send_sems.at[a],
                                             recv_sem=recv_sems.at[a], device_id=(x, y, 1 - c), device_id_type=MESH_ID)
                for a in range(n)]

    def start(*refs):
        for cp in copies(*refs):
            cp.start()

    def finish(*refs):
        for cp in copies(*refs):
            cp.wait()

    out_shapes = [jax.ShapeDtypeStruct(s.shape[1:], s.dtype) for s in sends]
    comm = _Comm(sends, out_shapes, n, 0, start, finish)
    return _run_comm(comm, name) if name else comm


def _flat2(shape):
    return math.prod(shape[:-1]), shape[-1]


def _row_tile(r):
    return r if r <= 512 else _pick8(r, 256)


def _pick8(n, pref):
    t = pref
    while n % t:
        t -= 8
    return t


def chip_sum(send, got, core, *, name):
    shape = got.shape[1:]
    r, cdim = _flat2(shape)
    tr = _row_tile(r)
    send = send.reshape(2, N_CHIP, r, cdim)
    got = got.reshape(N_CHIP, r, cdim)

    def body(core_ref, s_ref, g_ref, o_ref):
        o_ref[...] = (s_ref[0].astype(F32) + g_ref[...].astype(F32)).astype(o_ref.dtype)

    blk = pl.BlockSpec((N_CHIP, tr, cdim), lambda i, core_ref: (0, i, 0))
    out = pl.pallas_call(
        body, name=name,
        grid_spec=pltpu.PrefetchScalarGridSpec(
            num_scalar_prefetch=1, grid=(r // tr,),
            in_specs=[pl.BlockSpec((1, N_CHIP, tr, cdim), lambda i, core_ref: (core_ref[0], 0, i, 0)), blk],
            out_specs=blk),
        out_shape=jax.ShapeDtypeStruct((N_CHIP, r, cdim), BF16),
        compiler_params=_cparams(("parallel",)),
    )(core, send, got)
    return out.reshape((N_CHIP,) + shape)


def chip_exchange(sums, *, name=None):
    n = len(sums)

    def copies(s_refs, out_refs, send_sems, recv_sems, local_sems):
        x, y, c = _me()
        mine = 2 * x + y
        chips = _other_chips(x, y)
        local = [pltpu.make_async_copy(s_refs[a].at[mine], out_refs[a].at[mine], local_sems.at[a]) for a in range(n)]
        sends, recvs = [], []
        for a in range(n):
            for j, (px, py) in enumerate(chips):
                sems = dict(send_sem=send_sems.at[3 * a + j], recv_sem=recv_sems.at[3 * a + j], device_id=(px, py, c),
                            device_id_type=MESH_ID)
                sends.append(pltpu.make_async_remote_copy(src_ref=s_refs[a].at[2 * px + py], dst_ref=out_refs[a].at[mine], **sems))
                recvs.append(pltpu.make_async_remote_copy(src_ref=s_refs[a].at[mine], dst_ref=out_refs[a].at[2 * px + py], **sems))
        return local, sends, recvs

    def start(*refs):
        local, sends, _ = copies(*refs)
        for cp in local + sends:
            cp.start()

    def finish(*refs):
        local, sends, recvs = copies(*refs)
        for cp in recvs:
            cp.wait_recv()
        for cp in sends:
            cp.wait_send()
        for cp in local:
            cp.wait()

    out_shapes = [jax.ShapeDtypeStruct(s.shape, s.dtype) for s in sums]
    comm = _Comm(sums, out_shapes, 3 * n, n, start, finish)
    return _run_comm(comm, name) if name else comm


def sum_adamw(parts, w, m, v, *, name, comm=None):
    shape = w.shape
    R, cdim = _flat2(shape)
    n_slots = parts.shape[0]
    tr = _row_tile(R)
    parts = parts.reshape(n_slots, R, cdim)
    w, m, v = (t.reshape(1, R, cdim) for t in (w, m, v))
    c1 = 1.0 - ADAM_B1 ** ADAM_STEP
    c2 = 1.0 - ADAM_B2 ** ADAM_STEP

    def body(p_ref, w_ref, m_ref, v_ref, g_ref, d_ref, mo_ref, vo_ref):
        g = p_ref[0].astype(F32)
        for s in range(1, n_slots):
            g = g + p_ref[s].astype(F32)
        mn = ADAM_B1 * m_ref[...] + (1.0 - ADAM_B1) * g
        vn = ADAM_B2 * v_ref[...] + (1.0 - ADAM_B2) * jnp.square(g)
        g_ref[...] = g
        mo_ref[...] = mn
        vo_ref[...] = vn
        d_ref[...] = -ADAM_LR * ((mn / c1) / (jnp.sqrt(vn / c2) + ADAM_EPS) + ADAM_WD * w_ref[...])

    rs = pl.BlockSpec((None, tr, cdim), lambda i: (0, i, 0))
    sd = jax.ShapeDtypeStruct((1, R, cdim), F32)
    res, c_outs = _call(
        body, name=name, grid=(R // tr,),
        in_specs=[pl.BlockSpec((n_slots, tr, cdim), lambda i: (0, i, 0)), rs, rs, rs],
        out_specs=(rs, rs, rs, rs), out_shape=(sd, sd, sd, sd),
        scratch_shapes=[], args=(parts, w, m, v), sem=("parallel",), comm=comm)
    res = [t.reshape(shape) for t in res]
    return (res, c_outs) if comm is not None else res


WEIGHTS = ("ln_mix_pre", "w_in", "b_gate", "q_norm", "w_uq", "kv_norm", "w_uk", "w_uv", "mem_norm", "w_mem_kv",
           "w_branch_out", "w_out", "ln_mix_post", "ln_mlp_pre", "w_mlp_up", "w_mlp_down", "ln_mlp_post")
BIG = dict(w_in=((D_MODEL, IN_WIDTH), 1), w_uq=((Q_LORA, MLA_HEADS * (MLA_NOPE + MLA_ROPE)), 1),
           w_uk=((KV_LORA, MLA_HEADS * MLA_NOPE), 1), w_uv=((KV_LORA, MLA_HEADS * MLA_V), 1),
           w_mem_kv=((D_MODEL, 2 * QM_W), 0), w_branch_out=((N_BRANCH, BRANCH_W, D_MODEL), 2),
           w_out=((D_MODEL, D_MODEL), 0), w_mlp_up=((D_MODEL, D_FF), 1), w_mlp_down=((D_FF, D_MODEL), 0))
SMALL = tuple(n for n in WEIGHTS if n not in BIG)


PACK_TILE = 512


def _pad_rows(a):
    r = a.shape[-2]
    to = PACK_TILE if r > PACK_TILE else 8
    pad = [(0, 0)] * a.ndim
    pad[-2] = (0, (-r) % to)
    return jnp.pad(a, pad)


def _pack_rows(arrs):
    return _pad_rows(jnp.concatenate([a.reshape(-1, LANES) for a in arrs], axis=0))


def _unpack_rows(packed, shapes, lead=()):
    out, r = [], 0
    for shp in shapes:
        n = math.prod(shp) // LANES
        out.append(packed[..., r:r + n, :].reshape(lead + tuple(shp)))
        r += n
    return out


def _slots_to_full(name, slots):
    full, ax = BIG[name]
    return jnp.moveaxis(slots, 0, ax).reshape(full)


def _full_to_slots(name, w):
    full, ax = BIG[name]
    split = full[:ax] + (N_DEV, full[ax] // N_DEV) + full[ax + 1:]
    return jnp.moveaxis(w.reshape(split), ax, 0)


def _full_to_owner(name, w):
    s = _full_to_slots(name, w)
    return jnp.swapaxes(s.reshape((N_CHIP, 2) + s.shape[1:]), 0, 1)


def _in_segments():
    orig = dict(zip(("cq", "ckv", "kr", "sb", "qm", "gate"), zip(np.cumsum((0,) + IN_SIZES[:-1]).tolist(), IN_SIZES)))
    shard = IN_WIDTH // N_DEV
    segs = []
    for n in PAD_ORDER:
        if n not in orig:
            continue
        o0, w = orig[n]
        c = o0
        while c < o0 + w:
            s = c // shard
            e = min(o0 + w, (s + 1) * shard)
            segs.append((s, c - s * shard, PAD_OFF[n] + (c - o0), e - c))
            c = e
    return segs


def in_slots_to_pad(slots, *, name):
    _, R, shard = slots.shape
    tr = _pick8(R, ROW_TILE)
    segs = _in_segments()

    def body(in_ref, out_ref):
        out_ref[...] = jnp.zeros_like(out_ref)
        for s, a, d, w in segs:
            out_ref[:, d:d + w] = in_ref[s, :, a:a + w]

    return pl.pallas_call(
        body, name=name, grid=(R // tr,), in_specs=[pl.BlockSpec((N_DEV, tr, shard), lambda i: (0, i, 0))],
        out_specs=pl.BlockSpec((tr, IN_PAD), lambda i: (i, 0)), out_shape=jax.ShapeDtypeStruct((R, IN_PAD), slots.dtype),
        compiler_params=_cparams(("parallel",)),
    )(slots)


def in_pad_to_owner(g, *, name):
    R = g.shape[0]
    shard = IN_WIDTH // N_DEV
    tr = _pick8(R, ROW_TILE)
    segs = _in_segments()

    def body(in_ref, out_ref):
        for s, a, d, w in segs:
            out_ref[s % 2, s // 2, :, a:a + w] = in_ref[:, d:d + w]

    return pl.pallas_call(
        body, name=name, grid=(R // tr,), in_specs=[pl.BlockSpec((tr, IN_PAD), lambda i: (i, 0))],
        out_specs=pl.BlockSpec((2, N_CHIP, tr, shard), lambda i: (0, 0, i, 0)),
        out_shape=jax.ShapeDtypeStruct((2, N_CHIP, R, shard), g.dtype),
        compiler_params=_cparams(("parallel",)),
    )(g)


def _pad_heads(w, width):
    r = w.shape[0]
    return jnp.pad(w.reshape(r, MLA_HEADS, width), ((0, 0), (0, 0), (0, HEAD_PAD - width))).reshape(r, MLA_PAD_W)


def _unpad_heads(wp, width):
    r = wp.shape[0]
    return wp.reshape(r, MLA_HEADS, HEAD_PAD)[:, :, :width].reshape(r, MLA_HEADS * width)


def _rope_inv_freq():
    half = MLA_ROPE // 2
    inv = 1.0 / (ROPE_THETA ** (jnp.arange(half, dtype=F32) * (2.0 / MLA_ROPE)))
    tab = jnp.zeros((LANES,), F32)
    tab = tab.at[MLA_NOPE:MLA_NOPE + half].set(inv).at[MLA_NOPE + half:MLA_NOPE + MLA_ROPE].set(inv)
    return tab.reshape(1, LANES)


def _as_tuple(r):
    return r if isinstance(r, tuple) else (r,)


class _Exchange:
    def __init__(self, w):
        self.w = w
        self.rest = tuple(n for n in BIG if n != "w_in")
        self.pending = {}
        self.reduced = {}

    def first_comm(self):
        return gather_two_level([self.w["w_in"].astype(BF16)])

    def first_weights(self, slots):
        return in_slots_to_pad(slots[0], name="w_in_layout")

    def rest_comm(self):
        return gather_two_level([self.w[n].astype(BF16) for n in self.rest])

    def rest_weights(self, slots):
        return {n: s if n == "w_mlp_up" else _slots_to_full(n, s) for n, s in zip(self.rest, slots)}

    def sibling_comm(self, tag, grads):
        self.pending[tag] = (tuple(grads), [grads[n] for n in grads])
        return sibling_exchange(self.pending[tag][1])

    def chip_comm(self, tag, gots):
        names, sends = self.pending[tag]
        core = lax.axis_index("c").astype(jnp.int32).reshape(1)
        sums = [chip_sum(s, g, core, name=f"chip_sum_{n}") for n, s, g in zip(names, sends, gots)]
        self.pending[tag] = names
        return chip_exchange(sums)

    def reduce_start(self, tag, grads):
        return self.chip_comm(tag, _run_comm(self.sibling_comm(tag, grads), f"sibling_{tag}"))

    def reduce_done(self, tag, recvs):
        self.reduced.update(zip(self.pending.pop(tag), recvs))


def _local_step(x, mem, positions, tgt, sm, ex):
    B, S, D = x.shape
    M = mem.shape[1]
    T = B * S
    x2 = x.reshape(T, D)
    mem2 = mem.reshape(B * M, D)
    pos = positions.reshape(T, 1).astype(F32)
    invf = _rope_inv_freq()
    cq_col, ckv_col = PAD_OFF["cq"] // Q_LORA, PAD_OFF["ckv"] // KV_LORA
    sb_col, qm_col = PAD_OFF["sb"] // LANES, PAD_OFF["qm"] // LANES
    sb_blk = SB_HEADS * SB_DIM // LANES
    mla = dict(mode="softmax", causal=True, hp=2, qk_w=128, v_w=64, scale=(MLA_NOPE + MLA_ROPE) ** -0.5,
               nh=MLA_HEADS // 2, qc=0, kc=0, vc=0)
    sbk = dict(mode="sb", causal=True, hp=2, qk_w=64, v_w=64, scale=SB_DIM ** -0.5, nh=SB_HEADS // 2,
               qc=sb_col, kc=sb_col + sb_blk, vc=sb_col + 2 * sb_blk)
    mca = dict(mode="softmax", causal=False, hp=1, qk_w=128, v_w=128, scale=MEM_DIM ** -0.5, nh=MEM_HEADS,
               qc=qm_col, kc=0, vc=MEM_HEADS)

    h, *slots = _as_tuple(rms_fwd(x2, sm["ln_mix_pre"], name="rms_mix_pre", comm=ex.first_comm()))
    w_in_pad = ex.first_weights(slots[0] if slots else None)
    proj = matmul(h, w_in_pad, out_dtype=BF16, name="mm_in")
    proj3 = proj.reshape(B, S, IN_PAD)
    o_sb, st_sb, *slots = attn_fwd(proj3, proj3, proj3, name="sb_fwd", comm=ex.rest_comm(), **sbk)
    W = ex.rest_weights(slots[0] if slots else None)
    w_uq_pad = _pad_heads(W["w_uq"], MLA_NOPE + MLA_ROPE)
    w_uk_pad = _pad_heads(W["w_uk"], MLA_NOPE)
    cqn = rms_fwd(proj, sm["q_norm"], n=Q_LORA, col=cq_col, name="rms_q")
    ckvn = rms_fwd(proj, sm["kv_norm"], n=KV_LORA, col=ckv_col, name="rms_kv")
    qpre = matmul(cqn, w_uq_pad, out_dtype=BF16, name="mm_uq")
    kpre = matmul(ckvn, w_uk_pad, out_dtype=BF16, name="mm_uk")
    v_mla = matmul(ckvn, W["w_uv"], out_dtype=BF16, name="mm_uv").reshape(B, S, -1)
    q_pad, k_pad = rope_fwd(qpre, kpre, proj, pos, invf, name="rope_fwd")
    q_pad, k_pad = q_pad.reshape(B, S, -1), k_pad.reshape(B, S, -1)
    o_mla, st_mla = attn_fwd(q_pad, k_pad, v_mla, name="mla_fwd", **mla)
    memh = rms_fwd(mem2, sm["mem_norm"], name="rms_mem")
    mkv = matmul(memh, W["w_mem_kv"], out_dtype=BF16, name="mm_memkv").reshape(B, M, -1)
    o_mem, st_mem = attn_fwd(proj3, mkv, mkv, name="mem_fwd", **mca)
    outs = [o.reshape(T, BRANCH_W) for o in (o_mla, o_sb, o_mem)]
    ps = [matmul(o, W["w_branch_out"][b], out_dtype=BF16, name=f"mm_bo{b}") for b, o in enumerate(outs)]
    merged = gate_merge(proj, sm["b_gate"], ps, name="gate_merge")
    y, x1, h2 = matmul(merged, W["w_out"], name="mm_out_norms",
                       rows=dict(fn=_mix_post_rows, row_ins=[x2], vec_ins=[sm["ln_mix_post"], sm["ln_mlp_pre"]],
                                 row_outs=[F32, F32, BF16], vec_outs=[]))
    u, relu_a = matmul(h2, W["w_mlp_up"], b_slots="n", act="relu2", out_dtype=BF16, name="mm_up")
    dx2, dd, loss_p, dg_mlp_post = matmul(
        u, W["w_mlp_down"], name="mm_down_loss",
        rows=dict(fn=_loss_rows, row_ins=[x1, tgt.reshape(T, D)], vec_ins=[sm["ln_mlp_post"]], row_outs=[F32, BF16],
                  vec_outs=[LANES, D]))

    da = matmul(dd, W["w_mlp_down"], tb=True, act="relu2_bwd", act_in=relu_a, out_dtype=BF16, name="mm_down_dx")
    g_down = matmul(u, dd, ta=True, out_dtype=BF16, out_slots="m", name="mm_down_dw")
    g_up = matmul(h2, da, ta=True, out_dtype=BF16, out_slots="n", name="mm_up_dw")
    sib = ex.sibling_comm("mlp", dict(w_mlp_down=g_down, w_mlp_up=g_up))
    dx1, dy, dg_mlp_pre, dg_mix_post, *got = matmul(
        da, _slots_to_full("w_mlp_up", W["w_mlp_up"]), tb=True, name="mm_up_dx_norms", comm=sib,
        rows=dict(fn=_mlp_pre_bwd_rows, row_ins=[dx2, x1, y], vec_ins=[sm["ln_mlp_pre"], sm["ln_mix_post"]],
                  row_outs=[F32, BF16], vec_outs=[D, D]))
    red_mlp = ex.chip_comm("mlp", got[0] if got else None)
    dmerged = matmul(dy, W["w_out"], tb=True, name="mm_out_dx")
    g_out = matmul(merged, dy, ta=True, out_dtype=BF16, name="mm_out_dw")
    dlog, dp0, dp1, dp2, db_gate = gate_bwd(dmerged, proj, sm["b_gate"], ps, name="gate_bwd")
    dps = (dp0, dp1, dp2)
    g_bo = jnp.stack([matmul(outs[b], dps[b], ta=True, out_dtype=BF16, name=f"mm_bo{b}_dw") for b in range(N_BRANCH)])
    dos = [matmul(dps[b], W["w_branch_out"][b], tb=True, out_dtype=BF16, name=f"mm_bo{b}_dx").reshape(B, S, BRANCH_W)
           for b in range(N_BRANCH)]
    dq_pad, dk_pad, dv_mla, *got = attn_bwd(q_pad, k_pad, v_mla, o_mla, st_mla, dos[0], name="mla_bwd", comm=red_mlp, **mla)
    ex.reduce_done("mlp", got[0] if got else None)
    dqm, dmk, dmv = attn_bwd(proj3, mkv, mkv, o_mem, st_mem, dos[2], name="mem_bwd", **mca)
    dmkv = jnp.concatenate([dmk, dmv], axis=-1).astype(BF16).reshape(B * M, -1)
    dmemh = matmul(dmkv, W["w_mem_kv"], tb=True, name="mm_memkv_dx")
    g_memkv = matmul(memh, dmkv, ta=True, out_dtype=BF16, name="mm_memkv_dw")
    _, dg_mem_norm = rms_bwd(dmemh, mem2, sm["mem_norm"], name="rms_mem_bwd")
    dq_pad, dk_pad, dv_mla = dq_pad.reshape(T, -1), dk_pad.reshape(T, -1), dv_mla.reshape(T, -1)
    dqpre, dkr = rope_bwd(dq_pad, dk_pad, pos, invf, name="rope_bwd")
    dcqn = matmul(dqpre, w_uq_pad, tb=True, name="mm_uq_dx")
    g_uq = _unpad_heads(matmul(cqn, dqpre, ta=True, out_dtype=BF16, name="mm_uq_dw"), MLA_NOPE + MLA_ROPE)
    dckvn = matmul(dk_pad, w_uk_pad, tb=True, name="mm_uk_dx")
    dckvn = matmul(dv_mla, W["w_uv"], tb=True, acc=dckvn, name="mm_uv_dx")
    g_uk = _unpad_heads(matmul(ckvn, dk_pad, ta=True, out_dtype=BF16, name="mm_uk_dw"), MLA_NOPE)
    g_uv = matmul(ckvn, dv_mla, ta=True, out_dtype=BF16, name="mm_uv_dw")
    dcq, dg_q_norm = rms_bwd(dcqn, proj, sm["q_norm"], n=Q_LORA, col=cq_col, out_dtype=BF16, name="rms_q_bwd")
    mix = dict(w_out=g_out, w_branch_out=g_bo, w_uq=g_uq, w_uk=g_uk, w_uv=g_uv, w_mem_kv=g_memkv)
    sib = ex.sibling_comm("mix", {n: _full_to_owner(n, g) for n, g in mix.items()})
    dckv, dg_kv_norm, *got = rms_bwd(dckvn, proj, sm["kv_norm"], n=KV_LORA, col=ckv_col, out_dtype=BF16,
                                     name="rms_kv_bwd", comm=sib)
    red_mix = ex.chip_comm("mix", got[0] if got else None)
    dsq, dsk, dsv, *got = attn_bwd(proj3, proj3, proj3, o_sb, st_sb, dos[1], name="sb_bwd", comm=red_mix, **sbk)
    ex.reduce_done("mix", got[0] if got else None)
    pieces = dict(gate=dlog, sb=jnp.concatenate([dsq, dsk, dsv], axis=-1).reshape(T, -1), qm=dqm.reshape(T, -1),
                  ckv=dckv, cq=dcq, kr=dkr, zpad=jnp.zeros((T, PAD_SIZES["zpad"]), BF16))
    dproj = jnp.concatenate([pieces[n].astype(BF16) for n in PAD_ORDER], axis=1)
    g_in = in_pad_to_owner(matmul(h, dproj, ta=True, out_dtype=BF16, name="mm_in_dw"), name="g_in_layout")
    red_in = ex.reduce_start("in", dict(w_in=g_in))
    dx, dg_mix_pre, *got = matmul(
        dproj, w_in_pad, tb=True, comm=red_in, name="mm_in_dx_norm",
        rows=dict(fn=_mix_pre_bwd_rows, row_ins=[x2, dx1], vec_ins=[sm["ln_mix_pre"]], row_outs=[F32], vec_outs=[D]))
    ex.reduce_done("in", got[0] if got else None)

    small = dict(ln_mix_pre=dg_mix_pre, b_gate=db_gate, q_norm=dg_q_norm, kv_norm=dg_kv_norm, mem_norm=dg_mem_norm,
                 ln_mix_post=dg_mix_post, ln_mlp_pre=dg_mlp_pre, ln_mlp_post=dg_mlp_post)
    return loss_p, dx.reshape(B, S, D), small


def kernel(x, mem, positions, ln_mix_pre, w_in, b_gate, q_norm, w_uq, kv_norm, w_uk, w_uv, mem_norm, w_mem_kv, w_branch_out, w_out, ln_mix_post, ln_mlp_pre, w_mlp_up, w_mlp_down, ln_mlp_post, loss_target, m_ln_mix_pre, m_w_in, m_b_gate, m_q_norm, m_w_uq, m_kv_norm, m_w_uk, m_w_uv, m_mem_norm, m_w_mem_kv, m_w_branch_out, m_w_out, m_ln_mix_post, m_ln_mlp_pre, m_w_mlp_up, m_w_mlp_down, m_ln_mlp_post, v_ln_mix_pre, v_w_in, v_b_gate, v_q_norm, v_w_uq, v_kv_norm, v_w_uk, v_w_uv, v_mem_norm, v_w_mem_kv, v_w_branch_out, v_w_out, v_ln_mix_post, v_ln_mlp_pre, v_w_mlp_up, v_w_mlp_down, v_ln_mlp_post):
    given = dict(locals())
    w = {n: given[n][0] for n in WEIGHTS}
    m = {n: given["m_" + n][0] for n in WEIGHTS}
    v = {n: given["v_" + n][0] for n in WEIGHTS}
    sm = {n: w[n].reshape(1, -1) for n in SMALL}
    ex = _Exchange(w)
    loss_row, grad_x, g_small = _local_step(x, mem, positions, loss_target, sm, ex)

    host = "w_uv"
    small_comm = gather_two_level([_pack_rows([g_small[n] for n in SMALL] + [loss_row])])
    res = {n: sum_adamw(ex.reduced[n], given[n], given["m_" + n], given["v_" + n], name=f"adamw_{n}",
                        comm=small_comm if n == host else None) for n in BIG}
    res[host], (parts,) = res[host]

    small_shapes = [w[n].shape for n in SMALL] + [(LANES,)]
    no_row = jnp.zeros((1, LANES), F32)
    res_small = sum_adamw(parts, *[_pack_rows([t[n] for n in SMALL] + [no_row]) for t in (w, m, v)], name="adamw_replicated")
    res_small = [_unpack_rows(r, small_shapes) for r in res_small]
    for i, n in enumerate(SMALL):
        res[n] = [r[i][None] for r in res_small]
    loss = res_small[0][len(SMALL)][0]

    out = [loss, grad_x]
    for k in range(4):
        out += [res[n][k] for n in WEIGHTS]
    return tuple(out)
```
